```python
import math
import jax, jax.numpy as jnp
from jax import lax
import numpy as np

D_MODEL = 1024
BATCH = 8
SEQ = 16384
DEPTH = 4

N_MEM = 256
HEAD_DIM = 64
N_MEM_HEADS = 4
MEM_WIDTH = N_MEM_HEADS * HEAD_DIM
CONV_WIDTH = D_MODEL - MEM_WIDTH
CONV_K = 3
N_Q_HEADS = CONV_WIDTH // HEAD_DIM
N_KV_HEADS = 4
GROUP = N_Q_HEADS // N_KV_HEADS
Q_WIDTH = N_Q_HEADS * HEAD_DIM
KV_WIDTH = N_KV_HEADS * HEAD_DIM
A_PROJ = 3 * CONV_WIDTH + MEM_WIDTH
B_PROJ = Q_WIDTH + MEM_WIDTH
WINDOW = 128
BLOCK = 128
REL_BUCKETS = 32
REL_MAX_DIST = 128
D_FF = ((8 * D_MODEL + 3 * 256 - 1) // (3 * 256)) * 256
N_A = DEPTH // 2
N_B = DEPTH - N_A
EPS = 1e-5

kernel_name = 'yoco_shortconv_swa_sink_hybrid'


def rmsnorm(x, g):
    x32 = x.astype(jnp.float32)
    y = x32 * lax.rsqrt(jnp.mean(x32 * x32, axis=-1, keepdims=True) + EPS)
    return (y * g.astype(jnp.float32)).astype(x.dtype)


def _rel_bucket(dist):
    max_exact = REL_BUCKETS // 2
    d = jnp.maximum(dist, 1).astype(jnp.float32)
    large = max_exact + (jnp.log(d / max_exact) / math.log(REL_MAX_DIST / max_exact)
                         * (REL_BUCKETS - max_exact)).astype(jnp.int32)
    large = jnp.minimum(large, REL_BUCKETS - 1)
    return jnp.where(dist < max_exact, dist, large)


def _band_geometry(n_blocks):
    qi = jnp.arange(BLOCK, dtype=jnp.int32)[:, None]
    kj = jnp.arange(2 * BLOCK, dtype=jnp.int32)[None, :]
    dist = qi + BLOCK - kj
    in_window = (dist >= 0) & (dist < WINDOW)
    first = (jnp.arange(n_blocks) == 0)[:, None, None]
    mask = in_window[None] & ~(first & (kj[None] < BLOCK))
    bucket = _rel_bucket(jnp.maximum(dist, 0))
    return mask, bucket


def _band(t):
    bsz, s = t.shape[0], t.shape[1]
    tb = t.reshape(bsz, s // BLOCK, BLOCK, N_KV_HEADS, HEAD_DIM)
    prev = jnp.concatenate([jnp.zeros_like(tb[:, :1]), tb[:, :-1]], axis=1)
    return jnp.concatenate([prev, tb], axis=2)


def _short_conv(u, b_gate, c_gate, w):
    v = c_gate * u
    s = v.shape[1]
    vp = jnp.pad(v, ((0, 0), (CONV_K - 1, 0), (0, 0)))
    conv = w[0] * vp[:, 0:s] + w[1] * vp[:, 1:s + 1] + w[2] * vp[:, 2:s + 2]
    return b_gate * conv


def _swa_sinks(q, k_band, v_band, sinks, rel_bias, mask, bucket):
    bsz, s, _ = q.shape
    nb = s // BLOCK
    qb = q.reshape(bsz, nb, BLOCK, N_KV_HEADS, GROUP, HEAD_DIM)
    logits = jnp.einsum('bnqhgd,bnjhd->bnhgqj', qb, k_band).astype(jnp.float32) * (HEAD_DIM ** -0.5)
    bias = jnp.transpose(rel_bias.astype(jnp.float32)[bucket], (2, 0, 1))
    bias = bias.reshape(N_KV_HEADS, GROUP, BLOCK, 2 * BLOCK)
    logits = jnp.where(mask[None, :, None, None], logits + bias, -jnp.inf)
    sink = sinks.astype(jnp.float32).reshape(N_KV_HEADS, GROUP, 1, 1)
    m = jnp.maximum(jnp.max(logits, axis=-1, keepdims=True), sink)
    p = jnp.exp(logits - m)
    denom = jnp.sum(p, axis=-1, keepdims=True) + jnp.exp(sink - m)
    probs = (p / denom).astype(v_band.dtype)
    out = jnp.einsum('bnhgqj,bnjhd->bnqhgd', probs, v_band)
    return out.reshape(bsz, s, Q_WIDTH)


def _mem_attention(q_mem, mem_k, mem_v):
    bsz, s = q_mem.shape[0], q_mem.shape[1]
    logits = jnp.einsum('bshd,bmhd->bhsm', q_mem, mem_k).astype(jnp.float32) * (HEAD_DIM ** -0.5)
    probs = jax.nn.softmax(logits, axis=-1).astype(mem_v.dtype)
    return jnp.einsum('bhsm,bmhd->bshd', probs, mem_v).reshape(bsz, s, MEM_WIDTH)


def _fwd_setup_inputs(seed: int = 0) -> dict:
    key = jax.random.key(seed)
    ks = jax.random.split(key, 20)

    def nrm(k, shape, scale):
        return jax.random.normal(k, shape, jnp.float32) * scale

    def gain(k, shape):
        return 1.0 + nrm(k, shape, 0.05)

    return {
        'x': nrm(ks[0], (BATCH, SEQ, D_MODEL), 1.0),
        'mem': nrm(ks[1], (BATCH, N_MEM, D_MODEL), 1.0),
        'norm_mix': gain(ks[2], (DEPTH, D_MODEL)),
        'norm_ffn': gain(ks[3], (DEPTH, D_MODEL)),
        'a_w_in': nrm(ks[4], (N_A, D_MODEL, A_PROJ), D_MODEL ** -0.5),
        'a_conv_w': nrm(ks[5], (N_A, CONV_K, CONV_WIDTH), CONV_K ** -0.5),
        'a_w_out': nrm(ks[6], (N_A, CONV_WIDTH + MEM_WIDTH, D_MODEL), (CONV_WIDTH + MEM_WIDTH) ** -0.5),
        'kv_norm': gain(ks[7], (D_MODEL,)),
        'w_kv': nrm(ks[8], (D_MODEL, 2 * KV_WIDTH), D_MODEL ** -0.5),
        'b_w_q': nrm(ks[9], (N_B, D_MODEL, B_PROJ), D_MODEL ** -0.5),
        'b_sinks': nrm(ks[10], (N_B, N_Q_HEADS), 0.5),
        'b_w_out': nrm(ks[11], (N_B, Q_WIDTH + MEM_WIDTH, D_MODEL), (Q_WIDTH + MEM_WIDTH) ** -0.5),
        'rel_bias': nrm(ks[12], (REL_BUCKETS, N_Q_HEADS), 0.5),
        'mem_norm': gain(ks[13], (D_MODEL,)),
        'w_mem_kv': nrm(ks[14], (DEPTH, D_MODEL, 2 * MEM_WIDTH), D_MODEL ** -0.5),
        'w_gate': nrm(ks[15], (DEPTH, D_MODEL, D_FF), D_MODEL ** -0.5),
        'w_up': nrm(ks[16], (DEPTH, D_MODEL, D_FF), D_MODEL ** -0.5),
        'w_down': nrm(ks[17], (DEPTH, D_FF, D_MODEL), D_FF ** -0.5),
        'final_norm': gain(ks[18], (D_MODEL,)),
    }


def _fwd_reference(x, mem, norm_mix, norm_ffn, a_w_in, a_conv_w, a_w_out, kv_norm, w_kv,
              b_w_q, b_sinks, b_w_out, rel_bias, mem_norm, w_mem_kv, w_gate, w_up, w_down,
              final_norm):
    bsz, s, _ = x.shape
    mask, bucket = _band_geometry(s // BLOCK)
    mem_n = rmsnorm(mem, mem_norm)
    k_band = None
    v_band = None
    for i in range(DEPTH):
        if i == N_A:
            kv = rmsnorm(x, kv_norm) @ w_kv
            k, v = jnp.split(kv, 2, axis=-1)
            k_band = _band(k.reshape(bsz, s, N_KV_HEADS, HEAD_DIM))
            v_band = _band(v.reshape(bsz, s, N_KV_HEADS, HEAD_DIM))
        mem_kv = mem_n @ w_mem_kv[i]
        mk, mv = jnp.split(mem_kv, 2, axis=-1)
        mk = mk.reshape(bsz, N_MEM, N_MEM_HEADS, HEAD_DIM)
        mv = mv.reshape(bsz, N_MEM, N_MEM_HEADS, HEAD_DIM)
        h = rmsnorm(x, norm_mix[i])
        if i < N_A:
            proj = h @ a_w_in[i]
            u, b_gate, c_gate, q_mem = jnp.split(
                proj, [CONV_WIDTH, 2 * CONV_WIDTH, 3 * CONV_WIDTH], axis=-1)
            y_tok = _short_conv(u, b_gate, c_gate, a_conv_w[i])
            w_out = a_w_out[i]
        else:
            j = i - N_A
            proj = h @ b_w_q[j]
            q, q_mem = jnp.split(proj, [Q_WIDTH], axis=-1)
            y_tok = _swa_sinks(q, k_band, v_band, b_sinks[j], rel_bias, mask, bucket)
            w_out = b_w_out[j]
        y_mem = _mem_attention(q_mem.reshape(bsz, s, N_MEM_HEADS, HEAD_DIM), mk, mv)
        x = x + jnp.concatenate([y_tok, y_mem], axis=-1) @ w_out
        h = rmsnorm(x, norm_ffn[i])
        x = x + (jax.nn.silu(h @ w_gate[i]) * (h @ w_up[i])) @ w_down[i]
    return rmsnorm(x, final_norm)


import jax as _jax
import jax.numpy as _jnp

TWIN_FORMAT = 'train_step'
FWD_PARAMS = ['x', 'mem', 'norm_mix', 'norm_ffn', 'a_w_in', 'a_conv_w', 'a_w_out', 'kv_norm', 'w_kv', 'b_w_q', 'b_sinks', 'b_w_out', 'rel_bias', 'mem_norm', 'w_mem_kv', 'w_gate', 'w_up', 'w_down', 'final_norm']
TWIN_WEIGHTS = ['norm_mix', 'norm_ffn', 'a_w_in', 'a_conv_w', 'a_w_out', 'kv_norm', 'w_kv', 'b_w_q', 'b_sinks', 'b_w_out', 'rel_bias', 'mem_norm', 'w_mem_kv', 'w_gate', 'w_up', 'w_down', 'final_norm']
TWIN_DIFF_INPUT = 'x'
TWIN_INPUTS = ['x', 'mem', 'norm_mix', 'norm_ffn', 'a_w_in', 'a_conv_w', 'a_w_out', 'kv_norm', 'w_kv', 'b_w_q', 'b_sinks', 'b_w_out', 'rel_bias', 'mem_norm', 'w_mem_kv', 'w_gate', 'w_up', 'w_down', 'final_norm', 'loss_target', 'm_norm_mix', 'm_norm_ffn', 'm_a_w_in', 'm_a_conv_w', 'm_a_w_out', 'm_kv_norm', 'm_w_kv', 'm_b_w_q', 'm_b_sinks', 'm_b_w_out', 'm_rel_bias', 'm_mem_norm', 'm_w_mem_kv', 'm_w_gate', 'm_w_up', 'm_w_down', 'm_final_norm', 'v_norm_mix', 'v_norm_ffn', 'v_a_w_in', 'v_a_conv_w', 'v_a_w_out', 'v_kv_norm', 'v_w_kv', 'v_b_w_q', 'v_b_sinks', 'v_b_w_out', 'v_rel_bias', 'v_mem_norm', 'v_w_mem_kv', 'v_w_gate', 'v_w_up', 'v_w_down', 'v_final_norm']
TWIN_OUTPUTS = ['loss', 'grad_x', 'grad_norm_mix', 'grad_norm_ffn', 'grad_a_w_in', 'grad_a_conv_w', 'grad_a_w_out', 'grad_kv_norm', 'grad_w_kv', 'grad_b_w_q', 'grad_b_sinks', 'grad_b_w_out', 'grad_rel_bias', 'grad_mem_norm', 'grad_w_mem_kv', 'grad_w_gate', 'grad_w_up', 'grad_w_down', 'grad_final_norm', 'delta_norm_mix', 'delta_norm_ffn', 'delta_a_w_in', 'delta_a_conv_w', 'delta_a_w_out', 'delta_kv_norm', 'delta_w_kv', 'delta_b_w_q', 'delta_b_sinks', 'delta_b_w_out', 'delta_rel_bias', 'delta_mem_norm', 'delta_w_mem_kv', 'delta_w_gate', 'delta_w_up', 'delta_w_down', 'delta_final_norm', 'new_m_norm_mix', 'new_m_norm_ffn', 'new_m_a_w_in', 'new_m_a_conv_w', 'new_m_a_w_out', 'new_m_kv_norm', 'new_m_w_kv', 'new_m_b_w_q', 'new_m_b_sinks', 'new_m_b_w_out', 'new_m_rel_bias', 'new_m_mem_norm', 'new_m_w_mem_kv', 'new_m_w_gate', 'new_m_w_up', 'new_m_w_down', 'new_m_final_norm', 'new_v_norm_mix', 'new_v_norm_ffn', 'new_v_a_w_in', 'new_v_a_conv_w', 'new_v_a_w_out', 'new_v_kv_norm', 'new_v_w_kv', 'new_v_b_w_q', 'new_v_b_sinks', 'new_v_b_w_out', 'new_v_rel_bias', 'new_v_mem_norm', 'new_v_w_mem_kv', 'new_v_w_gate', 'new_v_w_up', 'new_v_w_down', 'new_v_final_norm']
TWIN_LEAF_KINDS = {'loss': 'loss', 'grad_x': 'grad_x', 'grad_norm_mix': 'grad_w', 'grad_norm_ffn': 'grad_w', 'grad_a_w_in': 'grad_w', 'grad_a_conv_w': 'grad_w', 'grad_a_w_out': 'grad_w', 'grad_kv_norm': 'grad_w', 'grad_w_kv': 'grad_w', 'grad_b_w_q': 'grad_w', 'grad_b_sinks': 'grad_w', 'grad_b_w_out': 'grad_w', 'grad_rel_bias': 'grad_w', 'grad_mem_norm': 'grad_w', 'grad_w_mem_kv': 'grad_w', 'grad_w_gate': 'grad_w', 'grad_w_up': 'grad_w', 'grad_w_down': 'grad_w', 'grad_final_norm': 'grad_w', 'delta_norm_mix': 'delta_w', 'delta_norm_ffn': 'delta_w', 'delta_a_w_in': 'delta_w', 'delta_a_conv_w': 'delta_w', 'delta_a_w_out': 'delta_w', 'delta_kv_norm': 'delta_w', 'delta_w_kv': 'delta_w', 'delta_b_w_q': 'delta_w', 'delta_b_sinks': 'delta_w', 'delta_b_w_out': 'delta_w', 'delta_rel_bias': 'delta_w', 'delta_mem_norm': 'delta_w', 'delta_w_mem_kv': 'delta_w', 'delta_w_gate': 'delta_w', 'delta_w_up': 'delta_w', 'delta_w_down': 'delta_w', 'delta_final_norm': 'delta_w', 'new_m_norm_mix': 'new_m', 'new_m_norm_ffn': 'new_m', 'new_m_a_w_in': 'new_m', 'new_m_a_conv_w': 'new_m', 'new_m_a_w_out': 'new_m', 'new_m_kv_norm': 'new_m', 'new_m_w_kv': 'new_m', 'new_m_b_w_q': 'new_m', 'new_m_b_sinks': 'new_m', 'new_m_b_w_out': 'new_m', 'new_m_rel_bias': 'new_m', 'new_m_mem_norm': 'new_m', 'new_m_w_mem_kv': 'new_m', 'new_m_w_gate': 'new_m', 'new_m_w_up': 'new_m', 'new_m_w_down': 'new_m', 'new_m_final_norm': 'new_m', 'new_v_norm_mix': 'new_v', 'new_v_norm_ffn': 'new_v', 'new_v_a_w_in': 'new_v', 'new_v_a_conv_w': 'new_v', 'new_v_a_w_out': 'new_v', 'new_v_kv_norm': 'new_v', 'new_v_w_kv': 'new_v', 'new_v_b_w_q': 'new_v', 'new_v_b_sinks': 'new_v', 'new_v_b_w_out': 'new_v', 'new_v_rel_bias': 'new_v', 'new_v_mem_norm': 'new_v', 'new_v_w_mem_kv': 'new_v', 'new_v_w_gate': 'new_v', 'new_v_w_up': 'new_v', 'new_v_w_down': 'new_v', 'new_v_final_norm': 'new_v'}


def _forward(args):
    return _fwd_reference(*[args[k] for k in FWD_PARAMS])


def _output_shape():
    def fwd():
        inp = _fwd_setup_inputs(0)
        return _fwd_reference(*[inp[k] for k in FWD_PARAMS])
    out = _jax.eval_shape(fwd)
    return out.shape, out.dtype

N_MICROBATCH = 1
ADAM_LR = 0.001
ADAM_B1 = 0.9
ADAM_B2 = 0.999
ADAM_EPS = 1e-08
ADAM_WD = 0.01
ADAM_STEP = 10
PER_EXAMPLE_BATCH_AXIS = {'x': 0, 'mem': 0, 'loss_target': 0}
SHARED_INPUTS = []
_WEIGHT_DTYPES = {'norm_mix': _jnp.float32, 'norm_ffn': _jnp.float32, 'a_w_in': _jnp.float32, 'a_conv_w': _jnp.float32, 'a_w_out': _jnp.float32, 'kv_norm': _jnp.float32, 'w_kv': _jnp.float32, 'b_w_q': _jnp.float32, 'b_sinks': _jnp.float32, 'b_w_out': _jnp.float32, 'rel_bias': _jnp.float32, 'mem_norm': _jnp.float32, 'w_mem_kv': _jnp.float32, 'w_gate': _jnp.float32, 'w_up': _jnp.float32, 'w_down': _jnp.float32, 'final_norm': _jnp.float32}
MOMENT_SCALE = {'norm_mix': 3.711483e-01, 'norm_ffn': 2.169550e-01, 'a_w_in': 3.319921e-01, 'a_conv_w': 3.501452e-01, 'a_w_out': 3.049258e-01, 'kv_norm': 7.572850e-02, 'w_kv': 1.043469e-01, 'b_w_q': 3.549557e-02, 'b_sinks': 3.653905e-02, 'b_w_out': 4.112841e-02, 'rel_bias': 7.337110e-02, 'mem_norm': 4.780484e-02, 'w_mem_kv': 3.111836e-02, 'w_gate': 9.262741e-02, 'w_up': 8.985778e-02, 'w_down': 1.494076e-01, 'final_norm': 1.280568e+02}


def _to_microbatches(a, axis):
    t = _jnp.moveaxis(a, axis, 0)
    t = t.reshape((N_MICROBATCH, t.shape[0] // N_MICROBATCH) + t.shape[1:])
    return _jnp.moveaxis(t, 1, axis + 1)


def setup_inputs(seed: int = 0) -> dict:
    inp = _fwd_setup_inputs(seed)
    key = _jax.random.fold_in(_jax.random.key(seed), 7919)
    shape, _ = _output_shape()
    out = dict(inp)
    out["loss_target"] = _jax.random.normal(_jax.random.fold_in(key, 0), shape, _jnp.float32)
    for i, name in enumerate(TWIN_WEIGHTS):
        w = inp[name].astype(_jnp.float32)
        if MOMENT_SCALE is None:
            s = _jnp.sqrt(_jnp.mean(_jnp.square(w)) + 1e-30)
        else:
            s = MOMENT_SCALE[name]
        km, kv = _jax.random.split(_jax.random.fold_in(key, i + 1))
        out[name] = w
        out["m_" + name] = s * _jax.random.normal(km, w.shape, _jnp.float32)
        out["v_" + name] = (s * s) * _jax.random.uniform(kv, w.shape, _jnp.float32, 0.5, 1.5)
    if N_MICROBATCH > 1:
        for name, axis in PER_EXAMPLE_BATCH_AXIS.items():
            out[name] = _to_microbatches(out[name], axis)
    return {'x': out['x'], 'mem': out['mem'], 'norm_mix': out['norm_mix'], 'norm_ffn': out['norm_ffn'], 'a_w_in': out['a_w_in'], 'a_conv_w': out['a_conv_w'], 'a_w_out': out['a_w_out'], 'kv_norm': out['kv_norm'], 'w_kv': out['w_kv'], 'b_w_q': out['b_w_q'], 'b_sinks': out['b_sinks'], 'b_w_out': out['b_w_out'], 'rel_bias': out['rel_bias'], 'mem_norm': out['mem_norm'], 'w_mem_kv': out['w_mem_kv'], 'w_gate': out['w_gate'], 'w_up': out['w_up'], 'w_down': out['w_down'], 'final_norm': out['final_norm'], 'loss_target': out['loss_target'], 'm_norm_mix': out['m_norm_mix'], 'm_norm_ffn': out['m_norm_ffn'], 'm_a_w_in': out['m_a_w_in'], 'm_a_conv_w': out['m_a_conv_w'], 'm_a_w_out': out['m_a_w_out'], 'm_kv_norm': out['m_kv_norm'], 'm_w_kv': out['m_w_kv'], 'm_b_w_q': out['m_b_w_q'], 'm_b_sinks': out['m_b_sinks'], 'm_b_w_out': out['m_b_w_out'], 'm_rel_bias': out['m_rel_bias'], 'm_mem_norm': out['m_mem_norm'], 'm_w_mem_kv': out['m_w_mem_kv'], 'm_w_gate': out['m_w_gate'], 'm_w_up': out['m_w_up'], 'm_w_down': out['m_w_down'], 'm_final_norm': out['m_final_norm'], 'v_norm_mix': out['v_norm_mix'], 'v_norm_ffn': out['v_norm_ffn'], 'v_a_w_in': out['v_a_w_in'], 'v_a_conv_w': out['v_a_conv_w'], 'v_a_w_out': out['v_a_w_out'], 'v_kv_norm': out['v_kv_norm'], 'v_w_kv': out['v_w_kv'], 'v_b_w_q': out['v_b_w_q'], 'v_b_sinks': out['v_b_sinks'], 'v_b_w_out': out['v_b_w_out'], 'v_rel_bias': out['v_rel_bias'], 'v_mem_norm': out['v_mem_norm'], 'v_w_mem_kv': out['v_w_mem_kv'], 'v_w_gate': out['v_w_gate'], 'v_w_up': out['v_w_up'], 'v_w_down': out['v_w_down'], 'v_final_norm': out['v_final_norm']}


def _loss(weights, diff, rest, loss_target):
    with _jax.named_scope("forward"):
        args = {**rest, TWIN_DIFF_INPUT: diff, **{k: w.astype(_WEIGHT_DTYPES[k]) for k, w in weights.items()}}
        y = _forward(args)
    with _jax.named_scope("loss_head"):
        err = _jnp.square(y.astype(_jnp.float32) - loss_target)
        return 0.5 * _jnp.sum(_jnp.mean(err, axis=-1)) if err.ndim else 0.5 * err


def _adamw(w, g, m, v):
    m = ADAM_B1 * m + (1.0 - ADAM_B1) * g
    v = ADAM_B2 * v + (1.0 - ADAM_B2) * _jnp.square(g)
    m_hat = m / (1.0 - ADAM_B1 ** ADAM_STEP)
    v_hat = v / (1.0 - ADAM_B2 ** ADAM_STEP)
    delta = -ADAM_LR * (m_hat / (_jnp.sqrt(v_hat) + ADAM_EPS) + ADAM_WD * w)
    return delta, m, v


def reference(x, mem, norm_mix, norm_ffn, a_w_in, a_conv_w, a_w_out, kv_norm, w_kv, b_w_q, b_sinks, b_w_out, rel_bias, mem_norm, w_mem_kv, w_gate, w_up, w_down, final_norm, loss_target, m_norm_mix, m_norm_ffn, m_a_w_in, m_a_conv_w, m_a_w_out, m_kv_norm, m_w_kv, m_b_w_q, m_b_sinks, m_b_w_out, m_rel_bias, m_mem_norm, m_w_mem_kv, m_w_gate, m_w_up, m_w_down, m_final_norm, v_norm_mix, v_norm_ffn, v_a_w_in, v_a_conv_w, v_a_w_out, v_kv_norm, v_w_kv, v_b_w_q, v_b_sinks, v_b_w_out, v_rel_bias, v_mem_norm, v_w_mem_kv, v_w_gate, v_w_up, v_w_down, v_final_norm):
    given = dict(x=x, mem=mem, norm_mix=norm_mix, norm_ffn=norm_ffn, a_w_in=a_w_in, a_conv_w=a_conv_w, a_w_out=a_w_out, kv_norm=kv_norm, w_kv=w_kv, b_w_q=b_w_q, b_sinks=b_sinks, b_w_out=b_w_out, rel_bias=rel_bias, mem_norm=mem_norm, w_mem_kv=w_mem_kv, w_gate=w_gate, w_up=w_up, w_down=w_down, final_norm=final_norm, loss_target=loss_target, m_norm_mix=m_norm_mix, m_norm_ffn=m_norm_ffn, m_a_w_in=m_a_w_in, m_a_conv_w=m_a_conv_w, m_a_w_out=m_a_w_out, m_kv_norm=m_kv_norm, m_w_kv=m_w_kv, m_b_w_q=m_b_w_q, m_b_sinks=m_b_sinks, m_b_w_out=m_b_w_out, m_rel_bias=m_rel_bias, m_mem_norm=m_mem_norm, m_w_mem_kv=m_w_mem_kv, m_w_gate=m_w_gate, m_w_up=m_w_up, m_w_down=m_w_down, m_final_norm=m_final_norm, v_norm_mix=v_norm_mix, v_norm_ffn=v_norm_ffn, v_a_w_in=v_a_w_in, v_a_conv_w=v_a_conv_w, v_a_w_out=v_a_w_out, v_kv_norm=v_kv_norm, v_w_kv=v_w_kv, v_b_w_q=v_b_w_q, v_b_sinks=v_b_sinks, v_b_w_out=v_b_w_out, v_rel_bias=v_rel_bias, v_mem_norm=v_mem_norm, v_w_mem_kv=v_w_mem_kv, v_w_gate=v_w_gate, v_w_up=v_w_up, v_w_down=v_w_down, v_final_norm=v_final_norm)
    weights = {n: given[n] for n in TWIN_WEIGHTS}
    shared = {n: given[n] for n in SHARED_INPUTS}
    per_example = {n: given[n] for n in ['x', 'mem']}
    grad_fn = _jax.value_and_grad(_loss, argnums=(0, 1))

    def one_microbatch(ex, loss_target):
        ex = dict(ex)
        diff = ex.pop(TWIN_DIFF_INPUT)
        return grad_fn(weights, diff, {**shared, **ex}, loss_target)

    if N_MICROBATCH == 1:
        loss, (grad_w, grad_x) = one_microbatch(per_example, given["loss_target"])
    else:
        def body(carry, xs):
            loss_sum, grad_sum = carry
            l_k, (gw_k, gx_k) = one_microbatch(xs[0], xs[1])
            with _jax.named_scope("update"):
                return (loss_sum + l_k, _jax.tree.map(_jnp.add, grad_sum, gw_k)), gx_k

        init = (_jnp.zeros((), _jnp.float32), _jax.tree.map(_jnp.zeros_like, weights))
        (loss, grad_w), grad_x = _jax.lax.scan(body, init, (per_example, given["loss_target"]))
    with _jax.named_scope("update"):
        delta_w, new_m, new_v = {}, {}, {}
        for n in TWIN_WEIGHTS:
            delta_w[n], new_m[n], new_v[n] = _adamw(weights[n], grad_w[n], given["m_" + n], given["v_" + n])
    return (loss, grad_x, *[grad_w[n] for n in TWIN_WEIGHTS], *[delta_w[n] for n in TWIN_WEIGHTS],
            *[new_m[n] for n in TWIN_WEIGHTS], *[new_v[n] for n in TWIN_WEIGHTS])
```

```python
import functools
import math

import numpy as np
import jax
import jax.numpy as jnp
from jax import lax
from jax.experimental import pallas as pl
from jax.experimental.pallas import tpu as pltpu

F32 = jnp.float32
BF16 = jnp.bfloat16
MESH = pl.DeviceIdType.MESH

EPS = 1e-5
HEAD_DIM = 64
N_MEM_HEADS = 4
N_KV_HEADS = 4
GROUP = 3
N_Q_HEADS = N_KV_HEADS * GROUP
BLOCK = 128
REL_BUCKETS = 32
REL_MAX_DIST = 128
SCALE = HEAD_DIM ** -0.5
NEG = -1e30
N_CHIPS = 4
N_A = 2
DEPTH = 4

ADAM_LR = 0.001
ADAM_B1 = 0.9
ADAM_B2 = 0.999
ADAM_EPS = 1e-08
ADAM_WD = 0.01
ADAM_STEP = 10

VMEM_BIG = 56 * 1024 * 1024
PACK_W = 1024

NT = (((1,), (1,)), ((), ()))
TN = (((0,), (0,)), ((), ()))


def _cp(sem=None, vmem=None, **kw):
    return pltpu.CompilerParams(dimension_semantics=sem, vmem_limit_bytes=vmem, **kw)


def _const_spec(shape):
    nd = len(shape)
    return pl.BlockSpec(shape, lambda i, _n=nd: (0,) * _n, pipeline_mode=pl.Buffered(1))


def _row_spec(tm, n):
    return pl.BlockSpec((tm, n), lambda i: (i, 0))


def _rms_parts(xv):
    r = lax.rsqrt(jnp.mean(xv * xv, axis=-1, keepdims=True) + EPS)
    return xv * r, r


def _sigmoid(z):
    return 1.0 / (1.0 + jnp.exp(-z))


def _ff_chunks(f):
    if f % 512 == 0 or f % 256 != 0:
        return [(0, f)] if f <= 1536 else [(0, f // 2), (f // 2, f - f // 2)]
    n = f // 256
    a = (n + 1) // 2 * 256
    return [(0, a), (a, f - a)]


def _norm_mm(x, g, w, tm, name):
    t, d = x.shape
    n = w.shape[1]

    def body(x_ref, g_ref, w_ref, o_ref):
        xhat, _ = _rms_parts(x_ref[...])
        h = (xhat * g_ref[...]).astype(BF16)
        o_ref[...] = jnp.dot(h, w_ref[...], preferred_element_type=F32).astype(BF16)

    return pl.pallas_call(
        body, name=name, grid=(t // tm,),
        in_specs=[_row_spec(tm, d), _const_spec((1, d)), _const_spec((d, n))],
        out_specs=_row_spec(tm, n),
        out_shape=jax.ShapeDtypeStruct((t, n), BF16),
        compiler_params=_cp(("parallel",), VMEM_BIG),
    )(x, g, w)


def _mm_nt_normbwd(dproj, w, x_in, g, dres, tm, name):
    t, d = x_in.shape
    n = w.shape[1]

    def body(dp_ref, w_ref, x_ref, g_ref, dr_ref, dx_ref, dg_ref, h_ref):
        i = pl.program_id(0)
        xhat, r = _rms_parts(x_ref[...])
        gv = g_ref[...]
        h_ref[...] = (xhat * gv).astype(BF16)
        dh = lax.dot_general(dp_ref[...], w_ref[...], NT, preferred_element_type=F32)
        dxhat = dh * gv
        dx = r * (dxhat - xhat * jnp.mean(dxhat * xhat, axis=-1, keepdims=True))
        dx_ref[...] = dr_ref[...] + dx

        @pl.when(i == 0)
        def _():
            dg_ref[...] = jnp.zeros_like(dg_ref)

        dg_ref[...] += jnp.sum(dh * xhat, axis=0, keepdims=True)

    return pl.pallas_call(
        body, name=name, grid=(t // tm,),
        in_specs=[_row_spec(tm, n), _const_spec((d, n)), _row_spec(tm, d), _const_spec((1, d)), _row_spec(tm, d)],
        out_specs=[_row_spec(tm, d), pl.BlockSpec((1, d), lambda i: (0, 0)), _row_spec(tm, d)],
        out_shape=[jax.ShapeDtypeStruct((t, d), F32), jax.ShapeDtypeStruct((1, d), F32),
                   jax.ShapeDtypeStruct((t, d), BF16)],
        compiler_params=_cp(("arbitrary",), VMEM_BIG),
    )(dproj, w, x_in, g, dres)


def _ffn_fwd(x, g, wg, wu, wd, tm, name):
    t, d = x.shape
    f = wg.shape[1]
    chunks = _ff_chunks(f)

    def body(x_ref, g_ref, wg_ref, wu_ref, wd_ref, xo_ref, gate_ref, up_ref):
        xv = x_ref[...]
        xhat, _ = _rms_parts(xv)
        h = (xhat * g_ref[...]).astype(BF16)
        acc = xv
        for c0, cw in chunks:
            gt = jnp.dot(h, wg_ref[:, c0:c0 + cw], preferred_element_type=F32)
            ut = jnp.dot(h, wu_ref[:, c0:c0 + cw], preferred_element_type=F32)
            gate_ref[:, c0:c0 + cw] = gt.astype(BF16)
            up_ref[:, c0:c0 + cw] = ut.astype(BF16)
            a = (gt * _sigmoid(gt) * ut).astype(BF16)
            acc = acc + jnp.dot(a, wd_ref[c0:c0 + cw, :], preferred_element_type=F32)
        xo_ref[...] = acc

    return pl.pallas_call(
        body, name=name, grid=(t // tm,),
        in_specs=[_row_spec(tm, d), _const_spec((1, d)), _const_spec((d, f)), _const_spec((d, f)), _const_spec((f, d))],
        out_specs=[_row_spec(tm, d), _row_spec(tm, f), _row_spec(tm, f)],
        out_shape=[jax.ShapeDtypeStruct((t, d), F32), jax.ShapeDtypeStruct((t, f), BF16),
                   jax.ShapeDtypeStruct((t, f), BF16)],
        compiler_params=_cp(("parallel",), VMEM_BIG),
    )(x, g, wg, wu, wd)


def _ffn_bwd(dxo, xm, g, gate, up, wg, wu, wd, tm, name):
    t, d = xm.shape
    f = wg.shape[1]
    chunks = _ff_chunks(f)

    def body(dxo_ref, xm_ref, g_ref, gate_ref, up_ref, wg_ref, wu_ref, wd_ref,
             dxm_ref, dgate_ref, dup_ref, h2_ref, dg_ref):
        i = pl.program_id(0)
        dxo_v = dxo_ref[...]
        dxo_b = dxo_v.astype(BF16)
        xhat, r = _rms_parts(xm_ref[...])
        gv = g_ref[...]
        h2_ref[...] = (xhat * gv).astype(BF16)
        dh = jnp.zeros((tm, d), F32)
        for c0, cw in chunks:
            dact = lax.dot_general(dxo_b, wd_ref[c0:c0 + cw, :], NT, preferred_element_type=F32)
            gt = gate_ref[:, c0:c0 + cw].astype(F32)
            ut = up_ref[:, c0:c0 + cw].astype(F32)
            sg = _sigmoid(gt)
            sl = gt * sg
            dgt = (dact * ut * (sg * (1.0 + gt * (1.0 - sg)))).astype(BF16)
            dut = (dact * sl).astype(BF16)
            dgate_ref[:, c0:c0 + cw] = dgt
            dup_ref[:, c0:c0 + cw] = dut
            dh = dh + lax.dot_general(dgt, wg_ref[:, c0:c0 + cw], NT, preferred_element_type=F32)
            dh = dh + lax.dot_general(dut, wu_ref[:, c0:c0 + cw], NT, preferred_element_type=F32)
        dxhat = dh * gv
        dx = r * (dxhat - xhat * jnp.mean(dxhat * xhat, axis=-1, keepdims=True))
        dxm_ref[...] = dxo_v + dx

        @pl.when(i == 0)
        def _():
            dg_ref[...] = jnp.zeros_like(dg_ref)

        dg_ref[...] += jnp.sum(dh * xhat, axis=0, keepdims=True)

    return pl.pallas_call(
        body, name=name, grid=(t // tm,),
        in_specs=[_row_spec(tm, d), _row_spec(tm, d), _const_spec((1, d)), _row_spec(tm, f), _row_spec(tm, f),
                  _const_spec((d, f)), _const_spec((d, f)), _const_spec((f, d))],
        out_specs=[_row_spec(tm, d), _row_spec(tm, f), _row_spec(tm, f), _row_spec(tm, d),
                   pl.BlockSpec((1, d), lambda i: (0, 0))],
        out_shape=[jax.ShapeDtypeStruct((t, d), F32), jax.ShapeDtypeStruct((t, f), BF16),
                   jax.ShapeDtypeStruct((t, f), BF16),
                   jax.ShapeDtypeStruct((t, d), BF16), jax.ShapeDtypeStruct((1, d), F32)],
        compiler_params=_cp(("arbitrary",), VMEM_BIG),
    )(dxo, xm, g, gate, up, wg, wu, wd)


def _wgrad(a, b, tt, name):
    t, k = a.shape
    n = b.shape[1]

    def body(a_ref, b_ref, o_ref):
        i = pl.program_id(0)

        @pl.when(i == 0)
        def _():
            o_ref[...] = jnp.zeros_like(o_ref)

        o_ref[...] += lax.dot_general(a_ref[...].astype(BF16), b_ref[...].astype(BF16), TN,
                                      preferred_element_type=F32)

    return pl.pallas_call(
        body, name=name, grid=(t // tt,),
        in_specs=[_row_spec(tt, k), _row_spec(tt, n)],
        out_specs=pl.BlockSpec((k, n), lambda i: (0, 0)),
        out_shape=jax.ShapeDtypeStruct((k, n), F32),
        compiler_params=_cp(("arbitrary",), VMEM_BIG),
    )(a, b)


def _wgrad_act(gate, up, b, tt, name):
    t, f = gate.shape
    n = b.shape[1]

    def body(g_ref, u_ref, b_ref, o_ref):
        i = pl.program_id(0)

        @pl.when(i == 0)
        def _():
            o_ref[...] = jnp.zeros_like(o_ref)

        gt = g_ref[...].astype(F32)
        a = (gt * _sigmoid(gt) * u_ref[...].astype(F32)).astype(BF16)
        o_ref[...] += lax.dot_general(a, b_ref[...].astype(BF16), TN, preferred_element_type=F32)

    return pl.pallas_call(
        body, name=name, grid=(t // tt,),
        in_specs=[_row_spec(tt, f), _row_spec(tt, f), _row_spec(tt, n)],
        out_specs=pl.BlockSpec((f, n), lambda i: (0, 0)),
        out_shape=jax.ShapeDtypeStruct((f, n), F32),
        compiler_params=_cp(("arbitrary",), VMEM_BIG),
    )(gate, up, b)


def _final_loss(x, g, target, tm, name):
    t, d = x.shape

    def body(x_ref, g_ref, t_ref, loss_ref, dx_ref, dg_ref):
        i = pl.program_id(0)
        xhat, r = _rms_parts(x_ref[...])
        gv = g_ref[...]
        err = xhat * gv - t_ref[...]
        dy = err * (1.0 / d)
        dxhat = dy * gv
        dx_ref[...] = r * (dxhat - xhat * jnp.mean(dxhat * xhat, axis=-1, keepdims=True))

        @pl.when(i == 0)
        def _():
            dg_ref[...] = jnp.zeros_like(dg_ref)
            loss_ref[...] = jnp.zeros_like(loss_ref)

        dg_ref[...] += jnp.sum(dy * xhat, axis=0, keepdims=True)
        part = jnp.sum(jnp.sum(err * err, axis=-1, keepdims=True), axis=0, keepdims=True) * (0.5 / d)
        loss_ref[...] += jnp.broadcast_to(part, loss_ref.shape)

    return pl.pallas_call(
        body, name=name, grid=(t // tm,),
        in_specs=[_row_spec(tm, d), _const_spec((1, d)), _row_spec(tm, d)],
        out_specs=[pl.BlockSpec((8, 128), lambda i: (0, 0)), _row_spec(tm, d), pl.BlockSpec((1, d), lambda i: (0, 0))],
        out_shape=[jax.ShapeDtypeStruct((8, 128), F32), jax.ShapeDtypeStruct((t, d), F32),
                   jax.ShapeDtypeStruct((1, d), F32)],
        compiler_params=_cp(("arbitrary",)),
    )(x, g, target)


def _block_diag(kv, n_seg, seg_rows, seg_cols):
    tiled = jnp.concatenate([kv] * n_seg, axis=0)
    ri = lax.broadcasted_iota(jnp.int32, tiled.shape, 0) // seg_rows
    ci = lax.broadcasted_iota(jnp.int32, tiled.shape, 1) // seg_cols
    return jnp.where(ri == ci, tiled, jnp.zeros_like(tiled))


def _diag_extract(acc, n_seg, seg_rows, seg_cols):
    ri = lax.broadcasted_iota(jnp.int32, acc.shape, 0) // seg_rows
    ci = lax.broadcasted_iota(jnp.int32, acc.shape, 1) // seg_cols
    masked = jnp.where(ri == ci, acc, 0.0)
    out = masked[0:seg_rows]
    for h in range(1, n_seg):
        out = out + masked[h * seg_rows:(h + 1) * seg_rows]
    return out


def _seg_softmax(s, n_seg, seg, sink_row=None):
    probs, sinkp = [], []
    for h in range(n_seg):
        sh = s[:, h * seg:(h + 1) * seg]
        m = jnp.max(sh, axis=-1, keepdims=True)
        if sink_row is not None:
            sk = sink_row[:, h * seg:h * seg + 1]
            m = jnp.maximum(m, sk)
        p = jnp.exp(sh - m)
        l = jnp.sum(p, axis=-1, keepdims=True)
        if sink_row is not None:
            es = jnp.exp(sk - m)
            l = l + es
        inv = 1.0 / l
        probs.append(p * inv)
        if sink_row is not None:
            sinkp.append(es * inv)
    return jnp.concatenate(probs, axis=1), sinkp


def _seg_softmax_bwd(p, dp, n_seg, seg):
    ds, deltas = [], []
    for h in range(n_seg):
        ph = p[:, h * seg:(h + 1) * seg]
        dph = dp[:, h * seg:(h + 1) * seg]
        delta = jnp.sum(ph * dph, axis=-1, keepdims=True)
        ds.append(ph * (dph - delta))
        deltas.append(delta)
    return jnp.concatenate(ds, axis=1), deltas


def _mem_attn_fwd(qm, mkbd, mvbd, n_mem):
    s = lax.dot_general(qm, mkbd, NT, preferred_element_type=F32) * SCALE
    p, _ = _seg_softmax(s, N_MEM_HEADS, n_mem)
    pb = p.astype(BF16)
    return jnp.dot(pb, mvbd, preferred_element_type=F32), p, pb


def _mem_attn_bwd(qm, dymem_b, p, pb, mkbd, mvbd, n_mem):
    dp = lax.dot_general(dymem_b, mvbd, NT, preferred_element_type=F32)
    ds, _ = _seg_softmax_bwd(p, dp, N_MEM_HEADS, n_mem)
    dsb = (ds * SCALE).astype(BF16)
    dqm = jnp.dot(dsb, mkbd, preferred_element_type=F32)
    dmk = lax.dot_general(dsb, qm, TN, preferred_element_type=F32)
    dmv = lax.dot_general(pb, dymem_b, TN, preferred_element_type=F32)
    return dqm, dmk, dmv


def _shift_down(v, halo, k):
    rolled = pltpu.roll(v, k, 0)
    hrolled = pltpu.roll(halo, k, 0)[0:8]
    rows = lax.broadcasted_iota(jnp.int32, (8, v.shape[1]), 0)
    first = jnp.where(rows < k, hrolled, rolled[0:8])
    return jnp.concatenate([first, rolled[8:]], axis=0)


def _shift_up(v, halo, k):
    n = v.shape[0]
    rolled = pltpu.roll(v, n - k, 0)
    hrolled = pltpu.roll(halo, 8 - k, 0)[0:8]
    rows = lax.broadcasted_iota(jnp.int32, (8, v.shape[1]), 0)
    last = jnp.where(rows >= 8 - k, hrolled, rolled[n - 8:])
    return jnp.concatenate([rolled[:n - 8], last], axis=0)


def _conv_parts(p, ph, cw, first_tile, cwid):
    u = p[:, 0:cwid].astype(F32)
    bg = p[:, cwid:2 * cwid].astype(F32)
    cg = p[:, 2 * cwid:3 * cwid].astype(F32)
    v = cg * u
    vh = ph[:, 2 * cwid:3 * cwid].astype(F32) * ph[:, 0:cwid].astype(F32)
    vh = jnp.where(first_tile, 0.0, vh)
    v1 = _shift_down(v, vh, 1)
    v2 = _shift_down(v, vh, 2)
    conv = cw[0:1, :] * v2 + cw[1:2, :] * v1 + cw[2:3, :] * v
    return u, bg, cg, v, v1, v2, conv


def _halo_prev_spec(rows, n, tm):
    per = tm // rows
    return pl.BlockSpec((rows, n), lambda i: (jnp.maximum(i * per - 1, 0), 0))


def _halo_next_spec(rows, n, tm, t):
    per = tm // rows
    last = t // rows - 1
    return pl.BlockSpec((rows, n), lambda i: (jnp.minimum((i + 1) * per, last), 0))


def _mix_a_fwd(x, proj, convw, memkv, layer, wout, tm, name):
    t, d = x.shape
    n_mem = memkv.shape[0]
    mw = N_MEM_HEADS * HEAD_DIM
    cwid = d - mw
    pw = proj.shape[1]

    def body(x_ref, p_ref, ph_ref, cw_ref, mkv_ref, wo_ref, xo_ref):
        i = pl.program_id(0)
        p = p_ref[...]
        _, bg, _, _, _, _, conv = _conv_parts(p, ph_ref[...], cw_ref[...], i == 0, cwid)
        ytok = (bg * conv).astype(BF16)
        mkv = mkv_ref[...]
        mkbd = _block_diag(mkv[:, 0:mw], N_MEM_HEADS, n_mem, HEAD_DIM)
        mvbd = _block_diag(mkv[:, mw:2 * mw], N_MEM_HEADS, n_mem, HEAD_DIM)
        ymem, _, _ = _mem_attn_fwd(p[:, 3 * cwid:3 * cwid + mw], mkbd, mvbd, n_mem)
        cat = jnp.concatenate([ytok, ymem.astype(BF16)], axis=1)
        xo_ref[...] = x_ref[...] + jnp.dot(cat, wo_ref[...], preferred_element_type=F32)

    return pl.pallas_call(
        body, name=name, grid=(t // tm,),
        in_specs=[_row_spec(tm, d), _row_spec(tm, pw), _halo_prev_spec(16, pw, tm), _const_spec((3, cwid)),
                  pl.BlockSpec((n_mem, 2 * mw), lambda i: (0, layer)), _const_spec((d, d))],
        out_specs=_row_spec(tm, d),
        out_shape=jax.ShapeDtypeStruct((t, d), F32),
        compiler_params=_cp(("parallel",), VMEM_BIG),
    )(x, proj, proj, convw, memkv, wout)


def _mix_a_bwd(dxm, proj, convw, memkv, layer, wout, tm, name):
    t, d = dxm.shape
    n_mem = memkv.shape[0]
    mw = N_MEM_HEADS * HEAD_DIM
    cwid = d - mw
    pw = proj.shape[1]
    nt = t // tm

    def body(dx_ref, dxn_ref, p_ref, ph_ref, pn_ref, cw_ref, mkv_ref, wo_ref,
             dp_ref, cat_ref, dcw_ref, dmkv_ref, dmk_acc, dmv_acc):
        i = pl.program_id(0)
        p = p_ref[...]
        cw = cw_ref[...]
        wo = wo_ref[...]
        u, bg, cg, v, v1, v2, conv = _conv_parts(p, ph_ref[...], cw, i == 0, cwid)
        dcat = lax.dot_general(dx_ref[...].astype(BF16), wo, NT, preferred_element_type=F32)
        dytok = dcat[:, 0:cwid]
        dymem_b = dcat[:, cwid:d].astype(BF16)
        pn = pn_ref[...]
        dcat_n = lax.dot_general(dxn_ref[...].astype(BF16), wo[0:cwid, :], NT, preferred_element_type=F32)
        dconv_n = jnp.where(i == nt - 1, 0.0, dcat_n * pn[:, cwid:2 * cwid].astype(F32))
        dbg = dytok * conv
        dconv = dytok * bg
        dv = cw[2:3, :] * dconv + cw[1:2, :] * _shift_up(dconv, dconv_n, 1) + cw[0:1, :] * _shift_up(dconv, dconv_n, 2)
        du = dv * cg
        dcg = dv * u
        rows8 = lax.broadcasted_iota(jnp.int32, (8, cwid), 0)
        dcw = (jnp.where(rows8 == 0, jnp.sum(dconv * v2, axis=0, keepdims=True), 0.0)
               + jnp.where(rows8 == 1, jnp.sum(dconv * v1, axis=0, keepdims=True), 0.0)
               + jnp.where(rows8 == 2, jnp.sum(dconv * v, axis=0, keepdims=True), 0.0))
        mkv = mkv_ref[...]
        mkbd = _block_diag(mkv[:, 0:mw], N_MEM_HEADS, n_mem, HEAD_DIM)
        mvbd = _block_diag(mkv[:, mw:2 * mw], N_MEM_HEADS, n_mem, HEAD_DIM)
        qm = p[:, 3 * cwid:3 * cwid + mw]
        ymem, pm, pmb = _mem_attn_fwd(qm, mkbd, mvbd, n_mem)
        dqm, dmk, dmv = _mem_attn_bwd(qm, dymem_b, pm, pmb, mkbd, mvbd, n_mem)
        cat_ref[...] = jnp.concatenate([(bg * conv).astype(BF16), ymem.astype(BF16)], axis=1)
        dp_ref[...] = jnp.concatenate([du.astype(BF16), dbg.astype(BF16), dcg.astype(BF16), dqm.astype(BF16)], axis=1)

        @pl.when(i == 0)
        def _():
            dcw_ref[...] = jnp.zeros_like(dcw_ref)
            dmk_acc[...] = jnp.zeros_like(dmk_acc)
            dmv_acc[...] = jnp.zeros_like(dmv_acc)

        dcw_ref[...] += dcw
        dmk_acc[...] += dmk
        dmv_acc[...] += dmv

        @pl.when(i == nt - 1)
        def _():
            dmkv_ref[...] = jnp.concatenate(
                [_diag_extract(dmk_acc[...], N_MEM_HEADS, n_mem, HEAD_DIM),
                 _diag_extract(dmv_acc[...], N_MEM_HEADS, n_mem, HEAD_DIM)], axis=1)

    return pl.pallas_call(
        body, name=name, grid=(nt,),
        in_specs=[_row_spec(tm, d), _halo_next_spec(16, d, tm, t), _row_spec(tm, pw), _halo_prev_spec(16, pw, tm),
                  _halo_next_spec(16, pw, tm, t), _const_spec((3, cwid)),
                  pl.BlockSpec((n_mem, 2 * mw), lambda i: (0, layer)), _const_spec((d, d))],
        out_specs=[_row_spec(tm, pw), _row_spec(tm, d), pl.BlockSpec((8, cwid), lambda i: (0, 0)),
                   pl.BlockSpec((n_mem, 2 * mw), lambda i: (0, 0))],
        out_shape=[jax.ShapeDtypeStruct((t, pw), BF16), jax.ShapeDtypeStruct((t, d), BF16),
                   jax.ShapeDtypeStruct((8, cwid), F32), jax.ShapeDtypeStruct((n_mem, 2 * mw), F32)],
        scratch_shapes=[pltpu.VMEM((N_MEM_HEADS * n_mem, mw), F32), pltpu.VMEM((N_MEM_HEADS * n_mem, mw), F32)],
        compiler_params=_cp(("arbitrary",), VMEM_BIG),
    )(dxm, dxm, proj, proj, proj, convw, memkv, wout)


def _rel_tables():
    qi = np.arange(BLOCK, dtype=np.int32)[:, None]
    kj = np.arange(2 * BLOCK, dtype=np.int32)[None, :]
    dist = qi + BLOCK - kj
    inw = (dist >= 0) & (dist < BLOCK)
    max_exact = REL_BUCKETS // 2
    dd = np.maximum(np.maximum(dist, 0), 1).astype(np.float32)
    large = max_exact + (np.log(dd / np.float32(max_exact)) / np.float32(math.log(REL_MAX_DIST / max_exact))
                         * np.float32(REL_BUCKETS - max_exact)).astype(np.int32)
    large = np.minimum(large, REL_BUCKETS - 1)
    bucket = np.where(np.maximum(dist, 0) < max_exact, np.maximum(dist, 0), large)
    return np.where(inw, bucket, -1).astype(np.int32)


def _bias_tables(rel_bias, sinks, name):
    bucket = jnp.asarray(_rel_tables())
    w = N_KV_HEADS * 2 * BLOCK

    def body(rb_ref, sk_ref, bk_ref, bias_ref, sink_ref):
        bk = bk_ref[...]
        for j in range(GROUP):
            segs, sks = [], []
            for h in range(N_KV_HEADS):
                head = GROUP * h + j
                acc = jnp.full(bk.shape, NEG, F32)
                for b in range(REL_BUCKETS):
                    acc = jnp.where(bk == b, rb_ref[b, head], acc)
                segs.append(acc)
                sks.append(jnp.full((8, 2 * BLOCK), sk_ref[0, head], F32))
            bias_ref[j] = jnp.concatenate(segs, axis=1)
            sink_ref[j] = jnp.concatenate(sks, axis=1)

    smem = pl.BlockSpec(memory_space=pltpu.SMEM)
    return pl.pallas_call(
        body, name=name,
        in_specs=[smem, smem, pl.BlockSpec(memory_space=pltpu.VMEM)],
        out_specs=[pl.BlockSpec(memory_space=pltpu.VMEM), pl.BlockSpec(memory_space=pltpu.VMEM)],
        out_shape=[jax.ShapeDtypeStruct((GROUP, BLOCK, w), F32), jax.ShapeDtypeStruct((GROUP, 8, w), F32)],
    )(rel_bias, sinks.reshape(1, N_Q_HEADS), bucket)


def _bias_bwd(dbias_a, dbias_b, name):
    bucket = jnp.asarray(_rel_tables())

    def body(da_ref, db_ref, bk_ref, o_ref):
        bk = bk_ref[...]
        ri = lax.broadcasted_iota(jnp.int32, (REL_BUCKETS, 128), 0)
        ci = lax.broadcasted_iota(jnp.int32, (REL_BUCKETS, 128), 1)
        out = jnp.zeros((REL_BUCKETS, 128), F32)
        for j in range(GROUP):
            dsum = da_ref[j] + db_ref[j]
            for h in range(N_KV_HEADS):
                head = GROUP * h + j
                seg = dsum[:, h * 2 * BLOCK:(h + 1) * 2 * BLOCK]
                for b in range(REL_BUCKETS):
                    val = jnp.sum(jnp.sum(jnp.where(bk == b, seg, 0.0), axis=0, keepdims=True), axis=1, keepdims=True)
                    out = out + jnp.where((ri == b) & (ci == head), val, 0.0)
        o_ref[...] = out

    vm = pl.BlockSpec(memory_space=pltpu.VMEM)
    return pl.pallas_call(
        body, name=name, in_specs=[vm, vm, vm], out_specs=vm,
        out_shape=jax.ShapeDtypeStruct((REL_BUCKETS, 128), F32),
    )(dbias_a, dbias_b, bucket)


def _swa_block(qb, kbd, vbd, bias_ref, sink_ref, first_block):
    w = N_KV_HEADS * 2 * BLOCK
    lane = lax.broadcasted_iota(jnp.int32, (BLOCK, w), 1)
    kill = first_block & ((lane % (2 * BLOCK)) < BLOCK)
    outs = []
    for j in range(GROUP):
        qj = qb[:, j * 256:(j + 1) * 256]
        s = lax.dot_general(qj, kbd, NT, preferred_element_type=F32) * SCALE + bias_ref[j]
        s = jnp.where(kill, NEG, s)
        p, sinkp = _seg_softmax(s, N_KV_HEADS, 2 * BLOCK, sink_ref[j][0:1, :])
        outs.append((p, p.astype(BF16), sinkp))
    return outs


def _mix_b_fwd(x, qp, kv, bias, sinkt, memkv, layer, wout, tm, name):
    t, d = x.shape
    n_mem = memkv.shape[0]
    mw = N_MEM_HEADS * HEAD_DIM
    qw = d - mw
    kw = N_KV_HEADS * HEAD_DIM
    nb = tm // BLOCK
    w = N_KV_HEADS * 2 * BLOCK

    def body(x_ref, q_ref, kv_ref, kvh_ref, bias_ref, sink_ref, mkv_ref, wo_ref, xo_ref, kvx, ytok):
        i = pl.program_id(0)
        kvx[0:BLOCK, :] = kvh_ref[...]
        kvx[BLOCK:BLOCK + tm, :] = kv_ref[...]

        def blk(b, carry):
            r0 = pl.multiple_of(b * BLOCK, BLOCK)
            win = kvx[pl.ds(r0, 2 * BLOCK), :]
            kbd = _block_diag(win[:, 0:kw], N_KV_HEADS, 2 * BLOCK, HEAD_DIM)
            vbd = _block_diag(win[:, kw:2 * kw], N_KV_HEADS, 2 * BLOCK, HEAD_DIM)
            qb = q_ref[pl.ds(r0, BLOCK), :]
            res = _swa_block(qb, kbd, vbd, bias_ref, sink_ref, (i == 0) & (b == 0))
            o = [jnp.dot(pb, vbd, preferred_element_type=F32).astype(BF16) for _, pb, _ in res]
            ytok[pl.ds(r0, BLOCK), :] = jnp.concatenate(o, axis=1)
            return carry

        lax.fori_loop(0, nb, blk, 0)
        mkv = mkv_ref[...]
        mkbd = _block_diag(mkv[:, 0:mw], N_MEM_HEADS, n_mem, HEAD_DIM)
        mvbd = _block_diag(mkv[:, mw:2 * mw], N_MEM_HEADS, n_mem, HEAD_DIM)
        ymem, _, _ = _mem_attn_fwd(q_ref[:, qw:d], mkbd, mvbd, n_mem)
        cat = jnp.concatenate([ytok[...], ymem.astype(BF16)], axis=1)
        xo_ref[...] = x_ref[...] + jnp.dot(cat, wo_ref[...], preferred_element_type=F32)

    return pl.pallas_call(
        body, name=name, grid=(t // tm,),
        in_specs=[_row_spec(tm, d), _row_spec(tm, d), _row_spec(tm, 2 * kw), _halo_prev_spec(BLOCK, 2 * kw, tm),
                  _const_spec((GROUP, BLOCK, w)), _const_spec((GROUP, 8, w)),
                  pl.BlockSpec((n_mem, 2 * mw), lambda i: (0, layer)), _const_spec((d, d))],
        out_specs=_row_spec(tm, d),
        out_shape=jax.ShapeDtypeStruct((t, d), F32),
        scratch_shapes=[pltpu.VMEM((tm + BLOCK, 2 * kw), BF16), pltpu.VMEM((tm, qw), BF16)],
        compiler_params=_cp(("parallel",), VMEM_BIG),
    )(x, qp, kv, kv, bias, sinkt, memkv, wout)


def _mix_b_bwd(dxm, qp, kv, bias, sinkt, memkv, layer, wout, tm, name):
    t, d = dxm.shape
    n_mem = memkv.shape[0]
    mw = N_MEM_HEADS * HEAD_DIM
    qw = d - mw
    kw = N_KV_HEADS * HEAD_DIM
    nb = tm // BLOCK
    nt = t // tm
    w = N_KV_HEADS * 2 * BLOCK

    def body(dx_ref, q_ref, kv_ref, kvh_ref, bias_ref, sink_ref, mkv_ref, wo_ref,
             dq_ref, cat_ref, dkv_ref, dkvh_ref, dbias_ref, dsink_ref, dmkv_ref,
             kvx, dkvx, dcat_s, dmk_acc, dmv_acc):
        i = pl.program_id(0)

        @pl.when(i == 0)
        def _():
            dbias_ref[...] = jnp.zeros_like(dbias_ref)
            dsink_ref[...] = jnp.zeros_like(dsink_ref)
            dmk_acc[...] = jnp.zeros_like(dmk_acc)
            dmv_acc[...] = jnp.zeros_like(dmv_acc)

        kvx[0:BLOCK, :] = kvh_ref[...]
        kvx[BLOCK:BLOCK + tm, :] = kv_ref[...]
        dkvx[...] = jnp.zeros_like(dkvx)
        dcat_s[...] = lax.dot_general(dx_ref[...].astype(BF16), wo_ref[...], NT,
                                      preferred_element_type=F32).astype(BF16)
        lane8 = lax.broadcasted_iota(jnp.int32, (8, 128), 1)

        def blk(b, carry):
            r0 = pl.multiple_of(b * BLOCK, BLOCK)
            win = kvx[pl.ds(r0, 2 * BLOCK), :]
            kbd = _block_diag(win[:, 0:kw], N_KV_HEADS, 2 * BLOCK, HEAD_DIM)
            vbd = _block_diag(win[:, kw:2 * kw], N_KV_HEADS, 2 * BLOCK, HEAD_DIM)
            qb = q_ref[pl.ds(r0, BLOCK), :]
            res = _swa_block(qb, kbd, vbd, bias_ref, sink_ref, (i == 0) & (b == 0))
            dkbd = jnp.zeros((w, kw), F32)
            dvbd = jnp.zeros((w, kw), F32)
            dsink = jnp.zeros((8, 128), F32)
            ys, dqs = [], []
            for j, (p, pb, sinkp) in enumerate(res):
                qj = qb[:, j * 256:(j + 1) * 256]
                do = dcat_s[pl.ds(r0, BLOCK), j * 256:(j + 1) * 256]
                ys.append(jnp.dot(pb, vbd, preferred_element_type=F32).astype(BF16))
                dp = lax.dot_general(do, vbd, NT, preferred_element_type=F32)
                ds, deltas = _seg_softmax_bwd(p, dp, N_KV_HEADS, 2 * BLOCK)
                dbias_ref[j] += ds
                for h in range(N_KV_HEADS):
                    val = -jnp.sum(sinkp[h] * deltas[h], axis=0, keepdims=True)
                    dsink = dsink + jnp.where(lane8 == 4 * j + h, val, 0.0)
                dsb = (ds * SCALE).astype(BF16)
                dqs.append(jnp.dot(dsb, kbd, preferred_element_type=F32).astype(BF16))
                dkbd = dkbd + lax.dot_general(dsb, qj, TN, preferred_element_type=F32)
                dvbd = dvbd + lax.dot_general(pb, do, TN, preferred_element_type=F32)
            cat_ref[pl.ds(r0, BLOCK), 0:qw] = jnp.concatenate(ys, axis=1)
            dq_ref[pl.ds(r0, BLOCK), 0:qw] = jnp.concatenate(dqs, axis=1)
            dsink_ref[...] += dsink
            dwin = jnp.concatenate([_diag_extract(dkbd, N_KV_HEADS, 2 * BLOCK, HEAD_DIM),
                                    _diag_extract(dvbd, N_KV_HEADS, 2 * BLOCK, HEAD_DIM)], axis=1)
            dkvx[pl.ds(r0, 2 * BLOCK), :] += dwin
            return carry

        lax.fori_loop(0, nb, blk, 0)
        dkvh_ref[0] = dkvx[0:BLOCK, :]
        dkv_ref[...] = dkvx[BLOCK:BLOCK + tm, :]

        mkv = mkv_ref[...]
        mkbd = _block_diag(mkv[:, 0:mw], N_MEM_HEADS, n_mem, HEAD_DIM)
        mvbd = _block_diag(mkv[:, mw:2 * mw], N_MEM_HEADS, n_mem, HEAD_DIM)
        qm = q_ref[:, qw:d]
        ymem, pm, pmb = _mem_attn_fwd(qm, mkbd, mvbd, n_mem)
        dqm, dmk, dmv = _mem_attn_bwd(qm, dcat_s[:, qw:d], pm, pmb, mkbd, mvbd, n_mem)
        cat_ref[:, qw:d] = ymem.astype(BF16)
        dq_ref[:, qw:d] = dqm.astype(BF16)
        dmk_acc[...] += dmk
        dmv_acc[...] += dmv

        @pl.when(i == nt - 1)
        def _():
            dmkv_ref[...] = jnp.concatenate(
                [_diag_extract(dmk_acc[...], N_MEM_HEADS, n_mem, HEAD_DIM),
                 _diag_extract(dmv_acc[...], N_MEM_HEADS, n_mem, HEAD_DIM)], axis=1)

    return pl.pallas_call(
        body, name=name, grid=(nt,),
        in_specs=[_row_spec(tm, d), _row_spec(tm, d), _row_spec(tm, 2 * kw), _halo_prev_spec(BLOCK, 2 * kw, tm),
                  _const_spec((GROUP, BLOCK, w)), _const_spec((GROUP, 8, w)),
                  pl.BlockSpec((n_mem, 2 * mw), lambda i: (0, layer)), _const_spec((d, d))],
        out_specs=[_row_spec(tm, d), _row_spec(tm, d), _row_spec(tm, 2 * kw),
                   pl.BlockSpec((1, BLOCK, 2 * kw), lambda i: (i, 0, 0)),
                   pl.BlockSpec((GROUP, BLOCK, w), lambda i: (0, 0, 0)), pl.BlockSpec((8, 128), lambda i: (0, 0)),
                   pl.BlockSpec((n_mem, 2 * mw), lambda i: (0, 0))],
        out_shape=[jax.ShapeDtypeStruct((t, d), BF16), jax.ShapeDtypeStruct((t, d), BF16),
                   jax.ShapeDtypeStruct((t, 2 * kw), F32), jax.ShapeDtypeStruct((nt, BLOCK, 2 * kw), F32),
                   jax.ShapeDtypeStruct((GROUP, BLOCK, w), F32), jax.ShapeDtypeStruct((8, 128), F32),
                   jax.ShapeDtypeStruct((n_mem, 2 * mw), F32)],
        scratch_shapes=[pltpu.VMEM((tm + BLOCK, 2 * kw), BF16), pltpu.VMEM((tm + BLOCK, 2 * kw), F32),
                        pltpu.VMEM((tm, d), BF16),
                        pltpu.VMEM((N_MEM_HEADS * n_mem, mw), F32), pltpu.VMEM((N_MEM_HEADS * n_mem, mw), F32)],
        compiler_params=_cp(("arbitrary",), VMEM_BIG),
    )(dxm, qp, kv, kv, bias, sinkt, memkv, wout)


def _kv_assemble(main_a, halo_a, main_b, halo_b, tm, name):
    t, n = main_a.shape
    nt = t // tm

    def body(ma_ref, ha_ref, mb_ref, hb_ref, o_ref):
        i = pl.program_id(0)
        s = ma_ref[...] + mb_ref[...]
        tail = jnp.where(i == nt - 1, 0.0, ha_ref[0] + hb_ref[0])
        o_ref[...] = jnp.concatenate([s[0:tm - BLOCK], s[tm - BLOCK:] + tail], axis=0).astype(BF16)

    halo_spec = pl.BlockSpec((1, BLOCK, n), lambda i: (jnp.minimum(i + 1, nt - 1), 0, 0))
    return pl.pallas_call(
        body, name=name, grid=(nt,),
        in_specs=[_row_spec(tm, n), halo_spec, _row_spec(tm, n), halo_spec],
        out_specs=_row_spec(tm, n),
        out_shape=jax.ShapeDtypeStruct((t, n), BF16),
        compiler_params=_cp(("parallel",)),
    )(main_a, halo_a, main_b, halo_b)


def _adam_math(w, g, m, v):
    m2 = ADAM_B1 * m + (1.0 - ADAM_B1) * g
    v2 = ADAM_B2 * v + (1.0 - ADAM_B2) * (g * g)
    m_hat = m2 / (1.0 - ADAM_B1 ** ADAM_STEP)
    v_hat = v2 / (1.0 - ADAM_B2 ** ADAM_STEP)
    delta = -ADAM_LR * (m_hat / (jnp.sqrt(v_hat) + ADAM_EPS) + ADAM_WD * w)
    return delta, m2, v2


def _adamw_sharded(w, mine, theirs, m, v, name):
    shape = w.shape
    c = shape[-1]
    r = int(np.prod(shape[:-1]))
    tr = r
    for cand in (512, 256, 128, 64, 32, 16):
        if r % cand == 0:
            tr = cand
            break

    def body(w_ref, a_ref, b_ref, m_ref, v_ref, g_ref, d_ref, mo_ref, vo_ref):
        g = a_ref[0].astype(F32) + b_ref[0].astype(F32)
        for k in range(1, N_CHIPS):
            g = g + (a_ref[k].astype(F32) + b_ref[k].astype(F32))
        delta, m2, v2 = _adam_math(w_ref[...], g, m_ref[...], v_ref[...])
        g_ref[...] = g
        d_ref[...] = delta
        mo_ref[...] = m2
        vo_ref[...] = v2

    rs = pl.BlockSpec((tr, c), lambda i: (i, 0))
    ps = pl.BlockSpec((N_CHIPS, tr, c), lambda i: (0, i, 0))
    sd = jax.ShapeDtypeStruct((r, c), F32)
    outs = pl.pallas_call(
        body, name=name, grid=(r // tr,),
        in_specs=[rs, ps, ps, rs, rs], out_specs=[rs, rs, rs, rs], out_shape=[sd, sd, sd, sd],
        compiler_params=_cp(("parallel",)),
    )(w.reshape(r, c), mine.reshape(N_CHIPS, r, c), theirs.reshape(N_CHIPS, r, c), m.reshape(r, c), v.reshape(r, c))
    return [o.reshape(shape) for o in outs]


def _adamw_packed(w, g, m, v, name):
    def body(w_ref, g_ref, m_ref, v_ref, d_ref, mo_ref, vo_ref):
        delta, m2, v2 = _adam_math(w_ref[...], g_ref[...], m_ref[...], v_ref[...])
        d_ref[...] = delta
        mo_ref[...] = m2
        vo_ref[...] = v2

    vm = pl.BlockSpec(memory_space=pltpu.VMEM)
    sd = jax.ShapeDtypeStruct(w.shape, F32)
    return pl.pallas_call(body, name=name, in_specs=[vm] * 4, out_specs=[vm] * 3, out_shape=[sd] * 3)(w, g, m, v)


def _place():
    return lax.axis_index("x"), lax.axis_index("y"), lax.axis_index("c")


def _chip_exchange(arrays, gather, name):
    n = len(arrays)
    hbm = pl.BlockSpec(memory_space=pltpu.HBM)

    def body(*refs):
        ins, outs = refs[:n], refs[n:2 * n]
        send_sems, recv_sems, loc_sems = refs[2 * n:]
        x, y, c = _place()
        me = 2 * x + y
        others = [(1 - x, y), (x, 1 - y), (1 - x, 1 - y)]
        copies = []
        for t in range(n):
            src_me = ins[t] if gather else ins[t].at[me]
            loc = pltpu.make_async_copy(src_me, outs[t].at[me], loc_sems.at[t])
            loc.start()
            copies.append(loc)
            for r, (px, py) in enumerate(others):
                src = ins[t] if gather else ins[t].at[2 * px + py]
                cp = pltpu.make_async_remote_copy(
                    src_ref=src, dst_ref=outs[t].at[me], send_sem=send_sems.at[3 * t + r],
                    recv_sem=recv_sems.at[3 * t + r], device_id=(px, py, c), device_id_type=MESH)
                cp.start()
                copies.append(cp)
        for cp in copies:
            cp.wait()

    out_shape = [jax.ShapeDtypeStruct(((N_CHIPS,) + a.shape) if gather else a.shape, a.dtype) for a in arrays]
    return pl.pallas_call(
        body, name=name, in_specs=[hbm] * n, out_specs=[hbm] * n, out_shape=out_shape,
        scratch_shapes=[pltpu.SemaphoreType.DMA((3 * n,)), pltpu.SemaphoreType.DMA((3 * n,)),
                        pltpu.SemaphoreType.DMA((n,))],
        compiler_params=_cp(has_side_effects=True),
    )(*arrays)


def _core_exchange(arrays, name):
    n = len(arrays)
    hbm = pl.BlockSpec(memory_space=pltpu.HBM)

    def body(*refs):
        ins, outs = refs[:n], refs[n:2 * n]
        send_sems, recv_sems = refs[2 * n:]
        x, y, c = _place()
        copies = []
        for t in range(n):
            cp = pltpu.make_async_remote_copy(
                src_ref=ins[t], dst_ref=outs[t], send_sem=send_sems.at[t], recv_sem=recv_sems.at[t],
                device_id=(x, y, 1 - c), device_id_type=MESH)
            cp.start()
            copies.append(cp)
        for cp in copies:
            cp.wait()

    return pl.pallas_call(
        body, name=name, in_specs=[hbm] * n, out_specs=[hbm] * n,
        out_shape=[jax.ShapeDtypeStruct(a.shape, a.dtype) for a in arrays],
        scratch_shapes=[pltpu.SemaphoreType.DMA((n,)), pltpu.SemaphoreType.DMA((n,))],
        compiler_params=_cp(has_side_effects=True),
    )(*arrays)


def _all_reduce_packed(pack, name):
    r, c = pack.shape
    vm = pl.BlockSpec(memory_space=pltpu.VMEM)

    def body(p_ref, sum_ref, slots, send_sems, recv_sems):
        x, y, cc = _place()
        me = 4 * x + 2 * y + cc
        slots[me] = p_ref[...]
        copies = []
        for rel in range(1, 8):
            px = 1 - x if rel & 4 else x
            py = 1 - y if rel & 2 else y
            pc = 1 - cc if rel & 1 else cc
            cp = pltpu.make_async_remote_copy(
                src_ref=p_ref, dst_ref=slots.at[me], send_sem=send_sems.at[rel - 1], recv_sem=recv_sems.at[rel - 1],
                device_id=(px, py, pc), device_id_type=MESH)
            cp.start()
            copies.append(cp)
        for cp in copies:
            cp.wait()
        total = slots[0]
        for k in range(1, 8):
            total = total + slots[k]
        sum_ref[...] = total

    return pl.pallas_call(
        body, name=name, in_specs=[vm], out_specs=vm, out_shape=jax.ShapeDtypeStruct((r, c), F32),
        scratch_shapes=[pltpu.VMEM((8, r, c), F32), pltpu.SemaphoreType.DMA((7,)), pltpu.SemaphoreType.DMA((7,))],
        compiler_params=_cp(has_side_effects=True),
    )(pack)


def _pack(items):
    rows = []
    for a in items:
        flat = a.astype(F32).reshape(-1)
        pad = (-flat.shape[0]) % PACK_W
        rows.append(jnp.pad(flat, (0, pad)).reshape(-1, PACK_W))
    out = jnp.concatenate(rows, axis=0)
    pad_r = (-out.shape[0]) % 8
    return jnp.pad(out, ((0, pad_r), (0, 0)))


def _unpack(pack, shapes):
    outs, row = [], 0
    for s in shapes:
        n = int(np.prod(s))
        nr = -(-n // PACK_W)
        outs.append(pack[row:row + nr].reshape(-1)[:n].reshape(s))
        row += nr
    return outs


def _heads_to_member_major(w, axis):
    shp = w.shape
    pre, post = shp[:axis], shp[axis + 1:]
    w4 = w.reshape(pre + (N_KV_HEADS, GROUP, HEAD_DIM) + post)
    w4 = jnp.swapaxes(w4, len(pre), len(pre) + 1)
    return w4.reshape(shp)


def _heads_to_kv_major(w, axis):
    shp = w.shape
    pre, post = shp[:axis], shp[axis + 1:]
    w4 = w.reshape(pre + (GROUP, N_KV_HEADS, HEAD_DIM) + post)
    w4 = jnp.swapaxes(w4, len(pre), len(pre) + 1)
    return w4.reshape(shp)


def _cols_full(g):
    k, l, r, cq = g.shape
    return jnp.transpose(g, (1, 2, 0, 3)).reshape(l, r, k * cq)


def _rows_full(g):
    k, l, rq, c = g.shape
    return jnp.transpose(g, (1, 0, 2, 3)).reshape(l, k * rq, c)


def _cols_pieces(g):
    l, r, c = g.shape
    return jnp.transpose(g.astype(BF16).reshape(l, r, N_CHIPS, c // N_CHIPS), (2, 0, 1, 3))


def _rows_pieces(g):
    l, r, c = g.shape
    return jnp.transpose(g.astype(BF16).reshape(l, N_CHIPS, r // N_CHIPS, c), (1, 0, 2, 3))


def kernel(x, mem, norm_mix, norm_ffn, a_w_in, a_conv_w, a_w_out, kv_norm, w_kv, b_w_q, b_sinks, b_w_out, rel_bias, mem_norm, w_mem_kv, w_gate, w_up, w_down, final_norm, loss_target, m_norm_mix, m_norm_ffn, m_a_w_in, m_a_conv_w, m_a_w_out, m_kv_norm, m_w_kv, m_b_w_q, m_b_sinks, m_b_w_out, m_rel_bias, m_mem_norm, m_w_mem_kv, m_w_gate, m_w_up, m_w_down, m_final_norm, v_norm_mix, v_norm_ffn, v_a_w_in, v_a_conv_w, v_a_w_out, v_kv_norm, v_w_kv, v_b_w_q, v_b_sinks, v_b_w_out, v_rel_bias, v_mem_norm, v_w_mem_kv, v_w_gate, v_w_up, v_w_down, v_final_norm):
    t, d = x.shape[1], x.shape[2]
    tm = 512 if t % 512 == 0 and t >= 2048 else 256
    x0 = x.reshape(t, d)
    target = loss_target.reshape(t, d)
    mem2 = mem.reshape(mem.shape[1], d)
    n_mem = mem2.shape[0]
    ax, ay, ac = _place()
    chip = 2 * ax + ay
    cwid = a_conv_w.shape[2] * N_CHIPS
    qw = N_Q_HEADS * HEAD_DIM

    sharded = dict(a_w_in=a_w_in, a_w_out=a_w_out, w_kv=w_kv[None], b_w_q=b_w_q, b_w_out=b_w_out,
                   w_mem_kv=w_mem_kv, w_gate=w_gate, w_up=w_up, w_down=w_down)
    col_cut = ("a_w_in", "w_gate", "w_up")
    names = list(sharded)
    gathered = _chip_exchange([sharded[k].astype(BF16) for k in names], True, "gather_weights")
    full = {k: (_cols_full(g) if k in col_cut else _rows_full(g)) for k, g in zip(names, gathered)}
    conv_place = jnp.zeros((N_CHIPS,) + a_conv_w.shape, F32).at[chip].set(a_conv_w) * (ac == 0).astype(F32)
    conv_all = _all_reduce_packed(_pack([conv_place]), "gather_conv")
    conv_full = _unpack(conv_all, [conv_place.shape])[0]
    conv_full = jnp.transpose(conv_full, (1, 2, 0, 3)).reshape(N_A, 3, cwid)

    w_in, w_out_a, wkv = full["a_w_in"], full["a_w_out"], full["w_kv"][0]
    wq = jnp.concatenate([_heads_to_member_major(full["b_w_q"][:, :, :qw], 2), full["b_w_q"][:, :, qw:]], axis=2)
    w_out_b = jnp.concatenate([_heads_to_member_major(full["b_w_out"][:, :qw, :], 1), full["b_w_out"][:, qw:, :]], axis=1)
    wmem = jnp.transpose(full["w_mem_kv"], (1, 0, 2)).reshape(d, -1)
    wg, wu, wd = full["w_gate"], full["w_up"], full["w_down"]

    memkv = _norm_mm(mem2, mem_norm.reshape(1, d), wmem, n_mem, "mem_kv")
    bias, sinkt = [], []
    for j in range(2):
        bj, sj = _bias_tables(rel_bias, b_sinks[j], "bias_tables")
        bias.append(bj)
        sinkt.append(sj)

    xs, xmids, projs, gates, ups = [x0], [], [], [], []
    kv = None
    for i in range(DEPTH):
        xin = xs[-1]
        gm = norm_mix[i].reshape(1, d)
        if i < N_A:
            proj = _norm_mm(xin, gm, w_in[i], tm, "proj_a")
            xmid = _mix_a_fwd(xin, proj, conv_full[i], memkv, i, w_out_a[i], tm, "mix_a_fwd")
        else:
            j = i - N_A
            if kv is None:
                kv = _norm_mm(xin, kv_norm.reshape(1, d), wkv, tm, "proj_kv")
            proj = _norm_mm(xin, gm, wq[j], tm, "proj_b")
            xmid = _mix_b_fwd(xin, proj, kv, bias[j], sinkt[j], memkv, i, w_out_b[j], tm, "mix_b_fwd")
        xout, gate, up = _ffn_fwd(xmid, norm_ffn[i].reshape(1, d), wg[i], wu[i], wd[i], tm, "ffn_fwd")
        projs.append(proj)
        xmids.append(xmid)
        gates.append(gate)
        ups.append(up)
        xs.append(xout)

    loss_part, dx, dg_final = _final_loss(xs[-1], final_norm.reshape(1, d), target, tm, "final_loss")

    g_w_in, g_w_out_a, g_wq, g_w_out_b = [None] * 2, [None] * 2, [None] * 2, [None] * 2
    g_wg, g_wu, g_wd = [None] * DEPTH, [None] * DEPTH, [None] * DEPTH
    g_norm_mix, g_norm_ffn = [None] * DEPTH, [None] * DEPTH
    g_conv, g_sinks = [None] * 2, [None] * 2
    dmemkv = [None] * DEPTH
    dbias, dkv_main, dkv_halo = [None] * 2, [None] * 2, [None] * 2
    g_wkv = g_kv_norm = None
    for i in reversed(range(DEPTH)):
        dxm, dgate, dup, h2, dgf = _ffn_bwd(dx, xmids[i], norm_ffn[i].reshape(1, d), gates[i], ups[i],
                                            wg[i], wu[i], wd[i], tm // 2, "ffn_bwd")
        g_norm_ffn[i] = dgf
        g_wd[i] = _wgrad_act(gates[i], ups[i], dx, tm, "wgrad_down")
        g_wg[i] = _wgrad(h2, dgate, tm, "wgrad_gate")
        g_wu[i] = _wgrad(h2, dup, tm, "wgrad_up")
        gm = norm_mix[i].reshape(1, d)
        if i < N_A:
            dproj, cat, dcw, dmemkv[i] = _mix_a_bwd(dxm, projs[i], conv_full[i], memkv, i, w_out_a[i], tm, "mix_a_bwd")
            g_conv[i] = dcw[0:3]
            g_w_out_a[i] = _wgrad(cat, dxm, tm, "wgrad_out")
            dx, g_norm_mix[i], h = _mm_nt_normbwd(dproj, w_in[i], xs[i], gm, dxm, tm, "proj_a_bwd")
            g_w_in[i] = _wgrad(h, dproj, tm, "wgrad_in_a")
        else:
            j = i - N_A
            dqp, cat, dkv_main[j], dkv_halo[j], dbias[j], dsk, dmemkv[i] = _mix_b_bwd(
                dxm, projs[i], kv, bias[j], sinkt[j], memkv, i, w_out_b[j], tm, "mix_b_bwd")
            g_sinks[j] = dsk[0, 0:N_Q_HEADS].reshape(GROUP, N_KV_HEADS).T.reshape(N_Q_HEADS)
            g_w_out_b[j] = _wgrad(cat, dxm, tm, "wgrad_out")
            dx, g_norm_mix[i], h = _mm_nt_normbwd(dqp, wq[j], xs[i], gm, dxm, tm, "proj_b_bwd")
            g_wq[j] = _wgrad(h, dqp, tm, "wgrad_in_b")
            if j == 0:
                dkv = _kv_assemble(dkv_main[0], dkv_halo[0], dkv_main[1], dkv_halo[1], tm, "kv_assemble")
                dx, g_kv_norm, hkv = _mm_nt_normbwd(dkv, wkv, xs[i], kv_norm.reshape(1, d), dx, tm, "proj_kv_bwd")
                g_wkv = _wgrad(hkv, dkv, tm, "wgrad_kv")
    grad_x = dx.reshape(x.shape)

    dmemkv_all = jnp.concatenate([a.astype(BF16) for a in dmemkv], axis=1)
    _, g_mem_norm, hmem = _mm_nt_normbwd(dmemkv_all, wmem, mem2, mem_norm.reshape(1, d),
                                         jnp.zeros((n_mem, d), F32), n_mem, "mem_kv_bwd")
    g_wmem = _wgrad(hmem, dmemkv_all, n_mem, "wgrad_mem")
    g_rel = _bias_bwd(dbias[0], dbias[1], "bias_bwd")[:, 0:N_Q_HEADS]

    g_wq_full = jnp.stack(g_wq)
    g_wq_full = jnp.concatenate([_heads_to_kv_major(g_wq_full[:, :, :qw], 2), g_wq_full[:, :, qw:]], axis=2)
    g_wob_full = jnp.stack(g_w_out_b)
    g_wob_full = jnp.concatenate([_heads_to_kv_major(g_wob_full[:, :qw, :], 1), g_wob_full[:, qw:, :]], axis=1)
    g_wmem_full = jnp.transpose(g_wmem.reshape(d, DEPTH, -1), (1, 0, 2))
    big = dict(a_w_in=jnp.stack(g_w_in), a_w_out=jnp.stack(g_w_out_a), w_kv=g_wkv[None], b_w_q=g_wq_full,
               b_w_out=g_wob_full, w_mem_kv=g_wmem_full, w_gate=jnp.stack(g_wg), w_up=jnp.stack(g_wu),
               w_down=jnp.stack(g_wd))
    pieces = [(_cols_pieces(big[k]) if k in col_cut else _rows_pieces(big[k])) for k in names]
    mine = _chip_exchange(pieces, False, "scatter_grads")
    theirs = _core_exchange(mine, "swap_cores")

    small_shapes = [(DEPTH, d), (DEPTH, d), (d,), (d,), (d,), (2, N_Q_HEADS), (REL_BUCKETS, N_Q_HEADS),
                    (N_A, 3, cwid), ()]
    small = _pack([jnp.concatenate(g_norm_mix, axis=0), jnp.concatenate(g_norm_ffn, axis=0), g_kv_norm, g_mem_norm,
                   dg_final, jnp.stack(g_sinks), g_rel, jnp.stack(g_conv), loss_part[0, 0]])
    small_sum = _all_reduce_packed(small, "reduce_small")
    (gs_norm_mix, gs_norm_ffn, gs_kv_norm, gs_mem_norm, gs_final, gs_sinks, gs_rel, gs_conv_full, loss) = _unpack(
        small_sum, small_shapes)
    cq = cwid // N_CHIPS
    gs_conv = lax.dynamic_slice_in_dim(gs_conv_full, chip * cq, cq, axis=2)

    weights = dict(norm_mix=norm_mix, norm_ffn=norm_ffn, a_w_in=a_w_in, a_conv_w=a_conv_w, a_w_out=a_w_out,
                   kv_norm=kv_norm, w_kv=w_kv, b_w_q=b_w_q, b_sinks=b_sinks, b_w_out=b_w_out, rel_bias=rel_bias,
                   mem_norm=mem_norm, w_mem_kv=w_mem_kv, w_gate=w_gate, w_up=w_up, w_down=w_down,
                   final_norm=final_norm)
    moms = dict(norm_mix=m_norm_mix, norm_ffn=m_norm_ffn, a_w_in=m_a_w_in, a_conv_w=m_a_conv_w, a_w_out=m_a_w_out,
                kv_norm=m_kv_norm, w_kv=m_w_kv, b_w_q=m_b_w_q, b_sinks=m_b_sinks, b_w_out=m_b_w_out,
                rel_bias=m_rel_bias, mem_norm=m_mem_norm, w_mem_kv=m_w_mem_kv, w_gate=m_w_gate, w_up=m_w_up,
                w_down=m_w_down, final_norm=m_final_norm)
    vars_ = dict(norm_mix=v_norm_mix, norm_ffn=v_norm_ffn, a_w_in=v_a_w_in, a_conv_w=v_a_conv_w, a_w_out=v_a_w_out,
                 kv_norm=v_kv_norm, w_kv=v_w_kv, b_w_q=v_b_w_q, b_sinks=v_b_sinks, b_w_out=v_b_w_out,
                 rel_bias=v_rel_bias, mem_norm=v_mem_norm, w_mem_kv=v_w_mem_kv, w_gate=v_w_gate, w_up=v_w_up,
                 w_down=v_w_down, final_norm=v_final_norm)
    order = list(weights)
    grads, deltas, new_m, new_v = {}, {}, {}, {}
    for k, mi, th in zip(names, mine, theirs):
        shp = weights[k].shape
        g, dl, m2, v2 = _adamw_sharded(weights[k].reshape(mi.shape[1:]), mi, th, moms[k].reshape(mi.shape[1:]),
                                       vars_[k].reshape(mi.shape[1:]), "adamw_" + k)
        grads[k], deltas[k], new_m[k], new_v[k] = g.reshape(shp), dl.reshape(shp), m2.reshape(shp), v2.reshape(shp)
    small_names = ["norm_mix", "norm_ffn", "kv_norm", "mem_norm", "final_norm", "b_sinks", "rel_bias", "a_conv_w"]
    small_g = [gs_norm_mix, gs_norm_ffn, gs_kv_norm, gs_mem_norm, gs_final, gs_sinks, gs_rel, gs_conv]
    shapes = [weights[k].shape for k in small_names]
    dl_p, m_p, v_p = _adamw_packed(_pack([weights[k] for k in small_names]), _pack(small_g),
                                   _pack([moms[k] for k in small_names]), _pack([vars_[k] for k in small_names]),
                                   "adamw_small")
    for k, g, dl, m2, v2 in zip(small_names, small_g, _unpack(dl_p, shapes), _unpack(m_p, shapes), _unpack(v_p, shapes)):
        grads[k], deltas[k], new_m[k], new_v[k] = g.reshape(weights[k].shape), dl, m2, v2

    return (loss, grad_x, *[grads[k] for k in order], *[deltas[k] for k in order],
            *[new_m[k] for k in order], *[new_v[k] for k in order])
```

```python
import functools
import math

import numpy as np
import jax
import jax.numpy as jnp
from jax import lax
from jax.experimental import pallas as pl
from jax.experimental.pallas import tpu as pltpu

F32 = jnp.float32
BF16 = jnp.bfloat16
MESH = pl.DeviceIdType.MESH

EPS = 1e-5
HEAD_DIM = 64
N_MEM_HEADS = 4
N_KV_HEADS = 4
GROUP = 3
N_Q_HEADS = N_KV_HEADS * GROUP
BLOCK = 128
REL_BUCKETS = 32
REL_MAX_DIST = 128
SCALE = HEAD_DIM ** -0.5
NEG = -1e30
N_CHIPS = 4
N_A = 2
DEPTH = 4

ADAM_LR = 0.001
ADAM_B1 = 0.9
ADAM_B2 = 0.999
ADAM_EPS = 1e-08
ADAM_WD = 0.01
ADAM_STEP = 10

VMEM_BIG = 56 * 1024 * 1024
PACK_W = 1024

NT = (((1,), (1,)), ((), ()))
TN = (((0,), (0,)), ((), ()))


def _cp(sem=None, vmem=None, **kw):
    return pltpu.CompilerParams(dimension_semantics=sem, vmem_limit_bytes=vmem, **kw)


def _const_spec(shape):
    nd = len(shape)
    return pl.BlockSpec(shape, lambda i, _n=nd: (0,) * _n, pipeline_mode=pl.Buffered(1))


def _row_spec(tm, n):
    return pl.BlockSpec((tm, n), lambda i: (i, 0))


def _rms_parts(xv):
    r = lax.rsqrt(jnp.mean(xv * xv, axis=-1, keepdims=True) + EPS)
    return xv * r, r


def _sigmoid(z):
    return 1.0 / (1.0 + jnp.exp(-z))


def _ff_chunks(f):
    if f % 512 == 0 or f % 256 != 0:
        return [(0, f)] if f <= 1536 else [(0, f // 2), (f // 2, f - f // 2)]
    n = f // 256
    a = (n + 1) // 2 * 256
    return [(0, a), (a, f - a)]


def _norm_mm(x, g, w, tm, name):
    t, d = x.shape
    n = w.shape[1]

    def body(x_ref, g_ref, w_ref, o_ref):
        xhat, _ = _rms_parts(x_ref[...])
        h = (xhat * g_ref[...]).astype(BF16)
        o_ref[...] = jnp.dot(h, w_ref[...], preferred_element_type=F32).astype(BF16)

    return pl.pallas_call(
        body, name=name, grid=(t // tm,),
        in_specs=[_row_spec(tm, d), _const_spec((1, d)), _const_spec((d, n))],
        out_specs=_row_spec(tm, n),
        out_shape=jax.ShapeDtypeStruct((t, n), BF16),
        compiler_params=_cp(("parallel",), VMEM_BIG),
    )(x, g, w)


def _mm_nt_normbwd(dproj, w, x_in, g, dres, tm, name):
    t, d = x_in.shape
    n = w.shape[1]

    def body(dp_ref, w_ref, x_ref, g_ref, dr_ref, dx_ref, dg_ref, h_ref):
        i = pl.program_id(0)
        xhat, r = _rms_parts(x_ref[...])
        gv = g_ref[...]
        h_ref[...] = (xhat * gv).astype(BF16)
        dh = lax.dot_general(dp_ref[...], w_ref[...], NT, preferred_element_type=F32)
        dxhat = dh * gv
        dx = r * (dxhat - xhat * jnp.mean(dxhat * xhat, axis=-1, keepdims=True))
        dx_ref[...] = dr_ref[...] + dx

        @pl.when(i == 0)
        def _():
            dg_ref[...] = jnp.zeros_like(dg_ref)

        dg_ref[...] += jnp.sum(dh * xhat, axis=0, keepdims=True)

    return pl.pallas_call(
        body, name=name, grid=(t // tm,),
        in_specs=[_row_spec(tm, n), _const_spec((d, n)), _row_spec(tm, d), _const_spec((1, d)), _row_spec(tm, d)],
        out_specs=[_row_spec(tm, d), pl.BlockSpec((1, d), lambda i: (0, 0)), _row_spec(tm, d)],
        out_shape=[jax.ShapeDtypeStruct((t, d), F32), jax.ShapeDtypeStruct((1, d), F32),
                   jax.ShapeDtypeStruct((t, d), BF16)],
        compiler_params=_cp(("arbitrary",), VMEM_BIG),
    )(dproj, w, x_in, g, dres)


def _ffn_fwd(x, g, wg, wu, wd, tm, name):
    t, d = x.shape
    f = wg.shape[1]
    chunks = _ff_chunks(f)

    def body(x_ref, g_ref, wg_ref, wu_ref, wd_ref, xo_ref, gate_ref, up_ref):
        xv = x_ref[...]
        xhat, _ = _rms_parts(xv)
        h = (xhat * g_ref[...]).astype(BF16)
        acc = xv
        for c0, cw in chunks:
            gt = jnp.dot(h, wg_ref[:, c0:c0 + cw], preferred_element_type=F32)
            ut = jnp.dot(h, wu_ref[:, c0:c0 + cw], preferred_element_type=F32)
            gate_ref[:, c0:c0 + cw] = gt.astype(BF16)
            up_ref[:, c0:c0 + cw] = ut.astype(BF16)
            a = (gt * _sigmoid(gt) * ut).astype(BF16)
            acc = acc + jnp.dot(a, wd_ref[c0:c0 + cw, :], preferred_element_type=F32)
        xo_ref[...] = acc

    return pl.pallas_call(
        body, name=name, grid=(t // tm,),
        in_specs=[_row_spec(tm, d), _const_spec((1, d)), _const_spec((d, f)), _const_spec((d, f)), _const_spec((f, d))],
        out_specs=[_row_spec(tm, d), _row_spec(tm, f), _row_spec(tm, f)],
        out_shape=[jax.ShapeDtypeStruct((t, d), F32), jax.ShapeDtypeStruct((t, f), BF16),
                   jax.ShapeDtypeStruct((t, f), BF16)],
        compiler_params=_cp(("parallel",), VMEM_BIG),
    )(x, g, wg, wu, wd)


def _ffn_bwd(dxo, xm, g, gate, up, wg, wu, wd, tm, name):
    t, d = xm.shape
    f = wg.shape[1]
    chunks = _ff_chunks(f)

    def body(dxo_ref, xm_ref, g_ref, gate_ref, up_ref, wg_ref, wu_ref, wd_ref,
             dxm_ref, dgate_ref, dup_ref, h2_ref, dg_ref):
        i = pl.program_id(0)
        dxo_v = dxo_ref[...]
        dxo_b = dxo_v.astype(BF16)
        xhat, r = _rms_parts(xm_ref[...])
        gv = g_ref[...]
        h2_ref[...] = (xhat * gv).astype(BF16)
        dh = jnp.zeros((tm, d), F32)
        for c0, cw in chunks:
            dact = lax.dot_general(dxo_b, wd_ref[c0:c0 + cw, :], NT, preferred_element_type=F32)
            gt = gate_ref[:, c0:c0 + cw].astype(F32)
            ut = up_ref[:, c0:c0 + cw].astype(F32)
            sg = _sigmoid(gt)
            sl = gt * sg
            dgt = (dact * ut * (sg * (1.0 + gt * (1.0 - sg)))).astype(BF16)
            dut = (dact * sl).astype(BF16)
            dgate_ref[:, c0:c0 + cw] = dgt
            dup_ref[:, c0:c0 + cw] = dut
            dh = dh + lax.dot_general(dgt, wg_ref[:, c0:c0 + cw], NT, preferred_element_type=F32)
            dh = dh + lax.dot_general(dut, wu_ref[:, c0:c0 + cw], NT, preferred_element_type=F32)
        dxhat = dh * gv
        dx = r * (dxhat - xhat * jnp.mean(dxhat * xhat, axis=-1, keepdims=True))
        dxm_ref[...] = dxo_v + dx

        @pl.when(i == 0)
        def _():
            dg_ref[...] = jnp.zeros_like(dg_ref)

        dg_ref[...] += jnp.sum(dh * xhat, axis=0, keepdims=True)

    return pl.pallas_call(
        body, name=name, grid=(t // tm,),
        in_specs=[_row_spec(tm, d), _row_spec(tm, d), _const_spec((1, d)), _row_spec(tm, f), _row_spec(tm, f),
                  _const_spec((d, f)), _const_spec((d, f)), _const_spec((f, d))],
        out_specs=[_row_spec(tm, d), _row_spec(tm, f), _row_spec(tm, f), _row_spec(tm, d),
                   pl.BlockSpec((1, d), lambda i: (0, 0))],
        out_shape=[jax.ShapeDtypeStruct((t, d), F32), jax.ShapeDtypeStruct((t, f), BF16),
                   jax.ShapeDtypeStruct((t, f), BF16),
                   jax.ShapeDtypeStruct((t, d), BF16), jax.ShapeDtypeStruct((1, d), F32)],
        compiler_params=_cp(("arbitrary",), VMEM_BIG),
    )(dxo, xm, g, gate, up, wg, wu, wd)


def _wgrad(a, b, tt, name):
    t, k = a.shape
    n = b.shape[1]

    def body(a_ref, b_ref, o_ref):
        i = pl.program_id(0)

        @pl.when(i == 0)
        def _():
            o_ref[...] = jnp.zeros_like(o_ref)

        o_ref[...] += lax.dot_general(a_ref[...].astype(BF16), b_ref[...].astype(BF16), TN,
                                      preferred_element_type=F32)

    return pl.pallas_call(
        body, name=name, grid=(t // tt,),
        in_specs=[_row_spec(tt, k), _row_spec(tt, n)],
        out_specs=pl.BlockSpec((k, n), lambda i: (0, 0)),
        out_shape=jax.ShapeDtypeStruct((k, n), F32),
        compiler_params=_cp(("arbitrary",), VMEM_BIG),
    )(a, b)


def _wgrad_act(gate, up, b, tt, name):
    t, f = gate.shape
    n = b.shape[1]

    def body(g_ref, u_ref, b_ref, o_ref):
        i = pl.program_id(0)

        @pl.when(i == 0)
        def _():
            o_ref[...] = jnp.zeros_like(o_ref)

        gt = g_ref[...].astype(F32)
        a = (gt * _sigmoid(gt) * u_ref[...].astype(F32)).astype(BF16)
        o_ref[...] += lax.dot_general(a, b_ref[...].astype(BF16), TN, preferred_element_type=F32)

    return pl.pallas_call(
        body, name=name, grid=(t // tt,),
        in_specs=[_row_spec(tt, f), _row_spec(tt, f), _row_spec(tt, n)],
        out_specs=pl.BlockSpec((f, n), lambda i: (0, 0)),
        out_shape=jax.ShapeDtypeStruct((f, n), F32),
        compiler_params=_cp(("arbitrary",), VMEM_BIG),
    )(gate, up, b)


def _final_loss(x, g, target, tm, name):
    t, d = x.shape

    def body(x_ref, g_ref, t_ref, loss_ref, dx_ref, dg_ref):
        i = pl.program_id(0)
        xhat, r = _rms_parts(x_ref[...])
        gv = g_ref[...]
        err = xhat * gv - t_ref[...]
        dy = err * (1.0 / d)
        dxhat = dy * gv
        dx_ref[...] = r * (dxhat - xhat * jnp.mean(dxhat * xhat, axis=-1, keepdims=True))

        @pl.when(i == 0)
        def _():
            dg_ref[...] = jnp.zeros_like(dg_ref)
            loss_ref[...] = jnp.zeros_like(loss_ref)

        dg_ref[...] += jnp.sum(dy * xhat, axis=0, keepdims=True)
        part = jnp.sum(jnp.sum(err * err, axis=-1, keepdims=True), axis=0, keepdims=True) * (0.5 / d)
        loss_ref[...] += jnp.broadcast_to(part, loss_ref.shape)

    return pl.pallas_call(
        body, name=name, grid=(t // tm,),
        in_specs=[_row_spec(tm, d), _const_spec((1, d)), _row_spec(tm, d)],
        out_specs=[pl.BlockSpec((8, 128), lambda i: (0, 0)), _row_spec(tm, d), pl.BlockSpec((1, d), lambda i: (0, 0))],
        out_shape=[jax.ShapeDtypeStruct((8, 128), F32), jax.ShapeDtypeStruct((t, d), F32),
                   jax.ShapeDtypeStruct((1, d), F32)],
        compiler_params=_cp(("arbitrary",)),
    )(x, g, target)


def _col_head(width):
    return lax.broadcasted_iota(jnp.int32, (1, width), 1) // HEAD_DIM


def _keep_head(a, colh, h):
    return jnp.where(colh == h, a, jnp.zeros_like(a))


def _softmax_cols(s, sink=None):
    m = jnp.max(s, axis=0, keepdims=True)
    if sink is not None:
        m = jnp.maximum(m, sink)
    p = jnp.exp(s - m)
    l = jnp.sum(p, axis=0, keepdims=True)
    if sink is None:
        return p * (1.0 / l), None
    es = jnp.exp(sink - m)
    inv = 1.0 / (l + es)
    return p * inv, es * inv


def _add4(v):
    return (v[0] + v[1]) + (v[2] + v[3])


def _mem_attn_fwd(qm, mk, mv):
    colh = _col_head(mk.shape[1])
    mks = mk * SCALE
    heads = range(N_MEM_HEADS)
    ss = [lax.dot_general(_keep_head(mks, colh, h), qm, NT, preferred_element_type=F32) for h in heads]
    ps = [_softmax_cols(s)[0].astype(BF16) for s in ss]
    return _add4([lax.dot_general(ps[h], _keep_head(mv, colh, h), TN, preferred_element_type=F32) for h in heads])


def _mem_attn_bwd(qm, dy_b, mk, mv):
    colh = _col_head(mk.shape[1])
    mks = mk * SCALE
    heads = range(N_MEM_HEADS)
    khs = [_keep_head(mks, colh, h) for h in heads]
    vhs = [_keep_head(mv, colh, h) for h in heads]
    ss = [lax.dot_general(khs[h], qm, NT, preferred_element_type=F32) for h in heads]
    dps = [lax.dot_general(vhs[h], dy_b, NT, preferred_element_type=F32) for h in heads]
    pbs, dsbs = [], []
    for h in heads:
        p, _ = _softmax_cols(ss[h])
        ds = p * (dps[h] - jnp.sum(p * dps[h], axis=0, keepdims=True))
        pbs.append(p.astype(BF16))
        dsbs.append(ds.astype(BF16))
    y = _add4([lax.dot_general(pbs[h], vhs[h], TN, preferred_element_type=F32) for h in heads])
    dq = _add4([lax.dot_general(dsbs[h], khs[h], TN, preferred_element_type=F32) for h in heads])
    dmk = _add4([jnp.where(colh == h, jnp.dot(dsbs[h], qm, preferred_element_type=F32) * SCALE, 0.0) for h in heads])
    dmv = _add4([jnp.where(colh == h, jnp.dot(pbs[h], dy_b, preferred_element_type=F32), 0.0) for h in heads])
    return y, dq, dmk, dmv


def _shift_down(v, halo, k):
    rolled = pltpu.roll(v, k, 0)
    hrolled = pltpu.roll(halo, k, 0)[0:8]
    rows = lax.broadcasted_iota(jnp.int32, (8, v.shape[1]), 0)
    first = jnp.where(rows < k, hrolled, rolled[0:8])
    return jnp.concatenate([first, rolled[8:]], axis=0)


def _shift_up(v, halo, k):
    n = v.shape[0]
    rolled = pltpu.roll(v, n - k, 0)
    hrolled = pltpu.roll(halo, 8 - k, 0)[0:8]
    rows = lax.broadcasted_iota(jnp.int32, (8, v.shape[1]), 0)
    last = jnp.where(rows >= 8 - k, hrolled, rolled[n - 8:])
    return jnp.concatenate([rolled[:n - 8], last], axis=0)


def _conv_parts(p, ph, cw, first_tile, cwid):
    u = p[:, 0:cwid].astype(F32)
    bg = p[:, cwid:2 * cwid].astype(F32)
    cg = p[:, 2 * cwid:3 * cwid].astype(F32)
    v = cg * u
    vh = ph[:, 2 * cwid:3 * cwid].astype(F32) * ph[:, 0:cwid].astype(F32)
    vh = jnp.where(first_tile, 0.0, vh)
    v1 = _shift_down(v, vh, 1)
    v2 = _shift_down(v, vh, 2)
    conv = cw[0:1, :] * v2 + cw[1:2, :] * v1 + cw[2:3, :] * v
    return u, bg, cg, v, v1, v2, conv


def _halo_prev_spec(rows, n, tm):
    per = tm // rows
    return pl.BlockSpec((rows, n), lambda i: (jnp.maximum(i * per - 1, 0), 0))


def _halo_next_spec(rows, n, tm, t):
    per = tm // rows
    last = t // rows - 1
    return pl.BlockSpec((rows, n), lambda i: (jnp.minimum((i + 1) * per, last), 0))


def _mix_a_fwd(x, proj, convw, memkv, layer, wout, tm, name):
    t, d = x.shape
    n_mem = memkv.shape[0]
    mw = N_MEM_HEADS * HEAD_DIM
    cwid = d - mw
    pw = proj.shape[1]

    def body(x_ref, p_ref, ph_ref, cw_ref, mkv_ref, wo_ref, xo_ref):
        i = pl.program_id(0)
        p = p_ref[...]
        _, bg, _, _, _, _, conv = _conv_parts(p, ph_ref[...], cw_ref[...], i == 0, cwid)
        ytok = (bg * conv).astype(BF16)
        mkv = mkv_ref[...]
        ymem = _mem_attn_fwd(p[:, 3 * cwid:3 * cwid + mw], mkv[:, 0:mw], mkv[:, mw:2 * mw])
        cat = jnp.concatenate([ytok, ymem.astype(BF16)], axis=1)
        xo_ref[...] = x_ref[...] + jnp.dot(cat, wo_ref[...], preferred_element_type=F32)

    return pl.pallas_call(
        body, name=name, grid=(t // tm,),
        in_specs=[_row_spec(tm, d), _row_spec(tm, pw), _halo_prev_spec(16, pw, tm), _const_spec((3, cwid)),
                  pl.BlockSpec((n_mem, 2 * mw), lambda i: (0, layer)), _const_spec((d, d))],
        out_specs=_row_spec(tm, d),
        out_shape=jax.ShapeDtypeStruct((t, d), F32),
        compiler_params=_cp(("parallel",), VMEM_BIG),
    )(x, proj, proj, convw, memkv, wout)


def _mix_a_bwd(dxm, proj, convw, memkv, layer, wout, tm, name):
    t, d = dxm.shape
    n_mem = memkv.shape[0]
    mw = N_MEM_HEADS * HEAD_DIM
    cwid = d - mw
    pw = proj.shape[1]
    nt = t // tm

    def body(dx_ref, dxn_ref, p_ref, ph_ref, pn_ref, cw_ref, mkv_ref, wo_ref,
             dp_ref, cat_ref, dcw_ref, dmkv_ref, dmk_acc, dmv_acc):
        i = pl.program_id(0)
        p = p_ref[...]
        cw = cw_ref[...]
        wo = wo_ref[...]
        u, bg, cg, v, v1, v2, conv = _conv_parts(p, ph_ref[...], cw, i == 0, cwid)
        dcat = lax.dot_general(dx_ref[...].astype(BF16), wo, NT, preferred_element_type=F32)
        dytok = dcat[:, 0:cwid]
        dymem_b = dcat[:, cwid:d].astype(BF16)
        pn = pn_ref[...]
        dcat_n = lax.dot_general(dxn_ref[...].astype(BF16), wo[0:cwid, :], NT, preferred_element_type=F32)
        dconv_n = jnp.where(i == nt - 1, 0.0, dcat_n * pn[:, cwid:2 * cwid].astype(F32))
        dbg = dytok * conv
        dconv = dytok * bg
        dv = cw[2:3, :] * dconv + cw[1:2, :] * _shift_up(dconv, dconv_n, 1) + cw[0:1, :] * _shift_up(dconv, dconv_n, 2)
        du = dv * cg
        dcg = dv * u
        rows8 = lax.broadcasted_iota(jnp.int32, (8, cwid), 0)
        dcw = (jnp.where(rows8 == 0, jnp.sum(dconv * v2, axis=0, keepdims=True), 0.0)
               + jnp.where(rows8 == 1, jnp.sum(dconv * v1, axis=0, keepdims=True), 0.0)
               + jnp.where(rows8 == 2, jnp.sum(dconv * v, axis=0, keepdims=True), 0.0))
        mkv = mkv_ref[...]
        qm = p[:, 3 * cwid:3 * cwid + mw]
        ymem, dqm, dmk, dmv = _mem_attn_bwd(qm, dymem_b, mkv[:, 0:mw], mkv[:, mw:2 * mw])
        cat_ref[...] = jnp.concatenate([(bg * conv).astype(BF16), ymem.astype(BF16)], axis=1)
        dp_ref[...] = jnp.concatenate([du.astype(BF16), dbg.astype(BF16), dcg.astype(BF16), dqm.astype(BF16)], axis=1)

        @pl.when(i == 0)
        def _():
            dcw_ref[...] = jnp.zeros_like(dcw_ref)
            dmk_acc[...] = jnp.zeros_like(dmk_acc)
            dmv_acc[...] = jnp.zeros_like(dmv_acc)

        dcw_ref[...] += dcw
        dmk_acc[...] += dmk
        dmv_acc[...] += dmv

        @pl.when(i == nt - 1)
        def _():
            dmkv_ref[...] = jnp.concatenate([dmk_acc[...], dmv_acc[...]], axis=1)

    return pl.pallas_call(
        body, name=name, grid=(nt,),
        in_specs=[_row_spec(tm, d), _halo_next_spec(16, d, tm, t), _row_spec(tm, pw), _halo_prev_spec(16, pw, tm),
                  _halo_next_spec(16, pw, tm, t), _const_spec((3, cwid)),
                  pl.BlockSpec((n_mem, 2 * mw), lambda i: (0, layer)), _const_spec((d, d))],
        out_specs=[_row_spec(tm, pw), _row_spec(tm, d), pl.BlockSpec((8, cwid), lambda i: (0, 0)),
                   pl.BlockSpec((n_mem, 2 * mw), lambda i: (0, 0))],
        out_shape=[jax.ShapeDtypeStruct((t, pw), BF16), jax.ShapeDtypeStruct((t, d), BF16),
                   jax.ShapeDtypeStruct((8, cwid), F32), jax.ShapeDtypeStruct((n_mem, 2 * mw), F32)],
        scratch_shapes=[pltpu.VMEM((n_mem, mw), F32), pltpu.VMEM((n_mem, mw), F32)],
        compiler_params=_cp(("arbitrary",), VMEM_BIG),
    )(dxm, dxm, proj, proj, proj, convw, memkv, wout)


def _rel_tables():
    qi = np.arange(BLOCK, dtype=np.int32)[:, None]
    kj = np.arange(2 * BLOCK, dtype=np.int32)[None, :]
    dist = qi + BLOCK - kj
    inw = (dist >= 0) & (dist < BLOCK)
    max_exact = REL_BUCKETS // 2
    dd = np.maximum(np.maximum(dist, 0), 1).astype(np.float32)
    large = max_exact + (np.log(dd / np.float32(max_exact)) / np.float32(math.log(REL_MAX_DIST / max_exact))
                         * np.float32(REL_BUCKETS - max_exact)).astype(np.int32)
    large = np.minimum(large, REL_BUCKETS - 1)
    bucket = np.where(np.maximum(dist, 0) < max_exact, np.maximum(dist, 0), large)
    return np.where(inw, bucket, -1).astype(np.int32)


def _bias_tables(rel_bias, sinks, name):
    bucket_t = jnp.asarray(_rel_tables().T)

    def body(rb_ref, sk_ref, bk_ref, bias_ref, sink_ref):
        bk = bk_ref[...]
        prev = lax.broadcasted_iota(jnp.int32, bk.shape, 0) < BLOCK
        for h in range(N_KV_HEADS):
            for j in range(GROUP):
                head = GROUP * h + j
                acc = jnp.full(bk.shape, NEG, F32)
                for b in range(REL_BUCKETS):
                    acc = jnp.where(bk == b, rb_ref[b, head], acc)
                bias_ref[h, :, j * BLOCK:(j + 1) * BLOCK] = acc
                bias_ref[N_KV_HEADS + h, :, j * BLOCK:(j + 1) * BLOCK] = jnp.where(prev, NEG, acc)
                sink_ref[h, :, j * BLOCK:(j + 1) * BLOCK] = jnp.full((8, BLOCK), sk_ref[0, head], F32)

    smem = pl.BlockSpec(memory_space=pltpu.SMEM)
    return pl.pallas_call(
        body, name=name,
        in_specs=[smem, smem, pl.BlockSpec(memory_space=pltpu.VMEM)],
        out_specs=[pl.BlockSpec(memory_space=pltpu.VMEM), pl.BlockSpec(memory_space=pltpu.VMEM)],
        out_shape=[jax.ShapeDtypeStruct((2 * N_KV_HEADS, 2 * BLOCK, GROUP * BLOCK), F32),
                   jax.ShapeDtypeStruct((N_KV_HEADS, 8, GROUP * BLOCK), F32)],
    )(rel_bias, sinks.reshape(1, N_Q_HEADS), bucket_t)


def _bias_bwd(dbias_a, dbias_b, name):
    bucket_t = jnp.asarray(_rel_tables().T)

    def body(da_ref, db_ref, bk_ref, o_ref):
        bk = bk_ref[...]
        ri = lax.broadcasted_iota(jnp.int32, (REL_BUCKETS, 128), 0)
        ci = lax.broadcasted_iota(jnp.int32, (REL_BUCKETS, 128), 1)
        out = jnp.zeros((REL_BUCKETS, 128), F32)
        for h in range(N_KV_HEADS):
            dsum = da_ref[h] + db_ref[h]
            for j in range(GROUP):
                head = GROUP * h + j
                seg = dsum[:, j * BLOCK:(j + 1) * BLOCK]
                for b in range(REL_BUCKETS):
                    val = jnp.sum(jnp.sum(jnp.where(bk == b, seg, 0.0), axis=0, keepdims=True), axis=1, keepdims=True)
                    out = out + jnp.where((ri == b) & (ci == head), val, 0.0)
        o_ref[...] = out

    vm = pl.BlockSpec(memory_space=pltpu.VMEM)
    return pl.pallas_call(
        body, name=name, in_specs=[vm, vm, vm], out_specs=vm,
        out_shape=jax.ShapeDtypeStruct((REL_BUCKETS, 128), F32),
    )(dbias_a, dbias_b, bucket_t)


def _stack_members(ref, r0, width):
    blk = ref[pl.ds(r0, BLOCK), 0:GROUP * width]
    return jnp.concatenate([blk[:, j * width:(j + 1) * width] for j in range(GROUP)], axis=0)


def _mix_b_fwd(x, qp, kv, bias, sinkt, memkv, layer, wout, tm, name):
    t, d = x.shape
    n_mem = memkv.shape[0]
    mw = N_MEM_HEADS * HEAD_DIM
    qw = d - mw
    kw = N_KV_HEADS * HEAD_DIM
    nb = tm // BLOCK
    rows = GROUP * BLOCK

    def body(x_ref, q_ref, kv_ref, kvh_ref, bias_ref, sink_ref, mkv_ref, wo_ref, xo_ref, kvx, ytok):
        i = pl.program_id(0)
        kvx[0:BLOCK, :] = kvh_ref[...]
        kvx[BLOCK:BLOCK + tm, :] = kv_ref[...]
        colh = _col_head(kw)

        def blk(b, carry):
            r0 = pl.multiple_of(b * BLOCK, BLOCK)
            win = kvx[pl.ds(r0, 2 * BLOCK), :]
            kwin = win[:, 0:kw] * SCALE
            vwin = win[:, kw:2 * kw]
            qs = _stack_members(q_ref, r0, kw)
            first = ((i == 0) & (b == 0)).astype(jnp.int32) * N_KV_HEADS
            heads = range(N_KV_HEADS)
            ss = [lax.dot_general(_keep_head(kwin, colh, h), qs, NT, preferred_element_type=F32) for h in heads]
            ps = [_softmax_cols(ss[h] + bias_ref[first + h], sink_ref[h][0:1, :])[0].astype(BF16) for h in heads]
            o = _add4([lax.dot_general(ps[h], _keep_head(vwin, colh, h), TN, preferred_element_type=F32)
                       for h in heads])
            for j in range(GROUP):
                ytok[pl.ds(r0, BLOCK), j * kw:(j + 1) * kw] = o[j * BLOCK:(j + 1) * BLOCK].astype(BF16)
            return carry

        lax.fori_loop(0, nb, blk, 0)
        mkv = mkv_ref[...]
        ymem = _mem_attn_fwd(q_ref[:, qw:d], mkv[:, 0:mw], mkv[:, mw:2 * mw])
        cat = jnp.concatenate([ytok[...], ymem.astype(BF16)], axis=1)
        xo_ref[...] = x_ref[...] + jnp.dot(cat, wo_ref[...], preferred_element_type=F32)

    return pl.pallas_call(
        body, name=name, grid=(t // tm,),
        in_specs=[_row_spec(tm, d), _row_spec(tm, d), _row_spec(tm, 2 * kw), _halo_prev_spec(BLOCK, 2 * kw, tm),
                  _const_spec((2 * N_KV_HEADS, 2 * BLOCK, rows)), _const_spec((N_KV_HEADS, 8, rows)),
                  pl.BlockSpec((n_mem, 2 * mw), lambda i: (0, layer)), _const_spec((d, d))],
        out_specs=_row_spec(tm, d),
        out_shape=jax.ShapeDtypeStruct((t, d), F32),
        scratch_shapes=[pltpu.VMEM((tm + BLOCK, 2 * kw), BF16), pltpu.VMEM((tm, qw), BF16)],
        compiler_params=_cp(("parallel",), VMEM_BIG),
    )(x, qp, kv, kv, bias, sinkt, memkv, wout)


def _mix_b_bwd(dxm, qp, kv, bias, sinkt, memkv, layer, wout, tm, name):
    t, d = dxm.shape
    n_mem = memkv.shape[0]
    mw = N_MEM_HEADS * HEAD_DIM
    qw = d - mw
    kw = N_KV_HEADS * HEAD_DIM
    nb = tm // BLOCK
    nt = t // tm
    rows = GROUP * BLOCK

    def body(dx_ref, q_ref, kv_ref, kvh_ref, bias_ref, sink_ref, mkv_ref, wo_ref,
             dq_ref, cat_ref, dkv_ref, dkvh_ref, dbias_ref, dsink_ref, dmkv_ref,
             kvx, dkvx, dcat_s, dmk_acc, dmv_acc):
        i = pl.program_id(0)

        @pl.when(i == 0)
        def _():
            dbias_ref[...] = jnp.zeros_like(dbias_ref)
            dsink_ref[...] = jnp.zeros_like(dsink_ref)
            dmk_acc[...] = jnp.zeros_like(dmk_acc)
            dmv_acc[...] = jnp.zeros_like(dmv_acc)

        kvx[0:BLOCK, :] = kvh_ref[...]
        kvx[BLOCK:BLOCK + tm, :] = kv_ref[...]
        dkvx[...] = jnp.zeros_like(dkvx)
        dcat_s[...] = lax.dot_general(dx_ref[...].astype(BF16), wo_ref[...], NT,
                                      preferred_element_type=F32).astype(BF16)
        colh = _col_head(kw)
        lane8 = lax.broadcasted_iota(jnp.int32, (8, 128), 1)

        def blk(b, carry):
            r0 = pl.multiple_of(b * BLOCK, BLOCK)
            win = kvx[pl.ds(r0, 2 * BLOCK), :]
            kwin = win[:, 0:kw] * SCALE
            vwin = win[:, kw:2 * kw]
            qs = _stack_members(q_ref, r0, kw)
            dos = _stack_members(dcat_s, r0, kw)
            first = ((i == 0) & (b == 0)).astype(jnp.int32) * N_KV_HEADS
            heads = range(N_KV_HEADS)
            khs = [_keep_head(kwin, colh, h) for h in heads]
            vhs = [_keep_head(vwin, colh, h) for h in heads]
            ss = [lax.dot_general(khs[h], qs, NT, preferred_element_type=F32) for h in heads]
            dps = [lax.dot_general(vhs[h], dos, NT, preferred_element_type=F32) for h in heads]
            dsink = jnp.zeros((8, 128), F32)
            pbs, dsbs = [], []
            for h in heads:
                p, sinkp = _softmax_cols(ss[h] + bias_ref[first + h], sink_ref[h][0:1, :])
                delta = jnp.sum(p * dps[h], axis=0, keepdims=True)
                ds = p * (dps[h] - delta)
                dbias_ref[h] += ds
                sd = sinkp * delta
                for j in range(GROUP):
                    val = -jnp.sum(sd[:, j * BLOCK:(j + 1) * BLOCK], axis=1, keepdims=True)
                    dsink = dsink + jnp.where(lane8 == 4 * j + h, val, 0.0)
                pbs.append(p.astype(BF16))
                dsbs.append(ds.astype(BF16))
            y = _add4([lax.dot_general(pbs[h], vhs[h], TN, preferred_element_type=F32) for h in heads])
            dq = _add4([lax.dot_general(dsbs[h], khs[h], TN, preferred_element_type=F32) for h in heads])
            dk = _add4([jnp.where(colh == h, jnp.dot(dsbs[h], qs, preferred_element_type=F32) * SCALE, 0.0)
                        for h in heads])
            dv = _add4([jnp.where(colh == h, jnp.dot(pbs[h], dos, preferred_element_type=F32), 0.0) for h in heads])
            for j in range(GROUP):
                cat_ref[pl.ds(r0, BLOCK), j * kw:(j + 1) * kw] = y[j * BLOCK:(j + 1) * BLOCK].astype(BF16)
                dq_ref[pl.ds(r0, BLOCK), j * kw:(j + 1) * kw] = dq[j * BLOCK:(j + 1) * BLOCK].astype(BF16)
            dsink_ref[...] += dsink
            dkvx[pl.ds(r0, 2 * BLOCK), :] += jnp.concatenate([dk, dv], axis=1)
            return carry

        lax.fori_loop(0, nb, blk, 0)
        dkvh_ref[0] = dkvx[0:BLOCK, :]
        dkv_ref[...] = dkvx[BLOCK:BLOCK + tm, :]

        mkv = mkv_ref[...]
        ymem, dqm, dmk, dmv = _mem_attn_bwd(q_ref[:, qw:d], dcat_s[:, qw:d], mkv[:, 0:mw], mkv[:, mw:2 * mw])
        cat_ref[:, qw:d] = ymem.astype(BF16)
        dq_ref[:, qw:d] = dqm.astype(BF16)
        dmk_acc[...] += dmk
        dmv_acc[...] += dmv

        @pl.when(i == nt - 1)
        def _():
            dmkv_ref[...] = jnp.concatenate([dmk_acc[...], dmv_acc[...]], axis=1)

    return pl.pallas_call(
        body, name=name, grid=(nt,),
        in_specs=[_row_spec(tm, d), _row_spec(tm, d), _row_spec(tm, 2 * kw), _halo_prev_spec(BLOCK, 2 * kw, tm),
                  _const_spec((2 * N_KV_HEADS, 2 * BLOCK, rows)), _const_spec((N_KV_HEADS, 8, rows)),
                  pl.BlockSpec((n_mem, 2 * mw), lambda i: (0, layer)), _const_spec((d, d))],
        out_specs=[_row_spec(tm, d), _row_spec(tm, d), _row_spec(tm, 2 * kw),
                   pl.BlockSpec((1, BLOCK, 2 * kw), lambda i: (i, 0, 0)),
                   pl.BlockSpec((N_KV_HEADS, 2 * BLOCK, rows), lambda i: (0, 0, 0)),
                   pl.BlockSpec((8, 128), lambda i: (0, 0)),
                   pl.BlockSpec((n_mem, 2 * mw), lambda i: (0, 0))],
        out_shape=[jax.ShapeDtypeStruct((t, d), BF16), jax.ShapeDtypeStruct((t, d), BF16),
                   jax.ShapeDtypeStruct((t, 2 * kw), F32), jax.ShapeDtypeStruct((nt, BLOCK, 2 * kw), F32),
                   jax.ShapeDtypeStruct((N_KV_HEADS, 2 * BLOCK, rows), F32), jax.ShapeDtypeStruct((8, 128), F32),
                   jax.ShapeDtypeStruct((n_mem, 2 * mw), F32)],
        scratch_shapes=[pltpu.VMEM((tm + BLOCK, 2 * kw), BF16), pltpu.VMEM((tm + BLOCK, 2 * kw), F32),
                        pltpu.VMEM((tm, d), BF16), pltpu.VMEM((n_mem, mw), F32), pltpu.VMEM((n_mem, mw), F32)],
        compiler_params=_cp(("arbitrary",), VMEM_BIG),
    )(dxm, qp, kv, kv, bias, sinkt, memkv, wout)


def _kv_assemble(main_a, halo_a, main_b, halo_b, tm, name):
    t, n = main_a.shape
    nt = t // tm

    def body(ma_ref, ha_ref, mb_ref, hb_ref, o_ref):
        i = pl.program_id(0)
        s = ma_ref[...] + mb_ref[...]
        tail = jnp.where(i == nt - 1, 0.0, ha_ref[0] + hb_ref[0])
        o_ref[...] = jnp.concatenate([s[0:tm - BLOCK], s[tm - BLOCK:] + tail], axis=0).astype(BF16)

    halo_spec = pl.BlockSpec((1, BLOCK, n), lambda i: (jnp.minimum(i + 1, nt - 1), 0, 0))
    return pl.pallas_call(
        body, name=name, grid=(nt,),
        in_specs=[_row_spec(tm, n), halo_spec, _row_spec(tm, n), halo_spec],
        out_specs=_row_spec(tm, n),
        out_shape=jax.ShapeDtypeStruct((t, n), BF16),
        compiler_params=_cp(("parallel",)),
    )(main_a, halo_a, main_b, halo_b)


def _adam_math(w, g, m, v):
    m2 = ADAM_B1 * m + (1.0 - ADAM_B1) * g
    v2 = ADAM_B2 * v + (1.0 - ADAM_B2) * (g * g)
    m_hat = m2 / (1.0 - ADAM_B1 ** ADAM_STEP)
    v_hat = v2 / (1.0 - ADAM_B2 ** ADAM_STEP)
    delta = -ADAM_LR * (m_hat / (jnp.sqrt(v_hat) + ADAM_EPS) + ADAM_WD * w)
    return delta, m2, v2


def _adamw_sharded(w, mine, theirs, m, v, name):
    shape = w.shape
    c = shape[-1]
    r = int(np.prod(shape[:-1]))
    tr = r
    for cand in (512, 256, 128, 64, 32, 16):
        if r % cand == 0:
            tr = cand
            break

    def body(w_ref, a_ref, b_ref, m_ref, v_ref, g_ref, d_ref, mo_ref, vo_ref):
        g = a_ref[0].astype(F32) + b_ref[0].astype(F32)
        for k in range(1, N_CHIPS):
            g = g + (a_ref[k].astype(F32) + b_ref[k].astype(F32))
        delta, m2, v2 = _adam_math(w_ref[...], g, m_ref[...], v_ref[...])
        g_ref[...] = g
        d_ref[...] = delta
        mo_ref[...] = m2
        vo_ref[...] = v2

    rs = pl.BlockSpec((tr, c), lambda i: (i, 0))
    ps = pl.BlockSpec((N_CHIPS, tr, c), lambda i: (0, i, 0))
    sd = jax.ShapeDtypeStruct((r, c), F32)
    outs = pl.pallas_call(
        body, name=name, grid=(r // tr,),
        in_specs=[rs, ps, ps, rs, rs], out_specs=[rs, rs, rs, rs], out_shape=[sd, sd, sd, sd],
        compiler_params=_cp(("parallel",)),
    )(w.reshape(r, c), mine.reshape(N_CHIPS, r, c), theirs.reshape(N_CHIPS, r, c), m.reshape(r, c), v.reshape(r, c))
    return [o.reshape(shape) for o in outs]


def _adamw_packed(w, g, m, v, name):
    def body(w_ref, g_ref, m_ref, v_ref, d_ref, mo_ref, vo_ref):
        delta, m2, v2 = _adam_math(w_ref[...], g_ref[...], m_ref[...], v_ref[...])
        d_ref[...] = delta
        mo_ref[...] = m2
        vo_ref[...] = v2

    vm = pl.BlockSpec(memory_space=pltpu.VMEM)
    sd = jax.ShapeDtypeStruct(w.shape, F32)
    return pl.pallas_call(body, name=name, in_specs=[vm] * 4, out_specs=[vm] * 3, out_shape=[sd] * 3)(w, g, m, v)


def _place():
    return lax.axis_index("x"), lax.axis_index("y"), lax.axis_index("c")


def _hbm(a):
    return pltpu.with_memory_space_constraint(a, pltpu.HBM)


def _other_chips(x, y):
    return [(1 - x, y), (x, 1 - y), (1 - x, 1 - y)]


def _chip_copy(src, land, gather, layer, chip_src, chip_dst, send_sem, recv_sem, peer):
    s = src if gather else src.at[chip_src]
    d = land.at[chip_dst] if layer is None else land.at[chip_dst, layer]
    return pltpu.make_async_remote_copy(src_ref=s, dst_ref=d, send_sem=send_sem, recv_sem=recv_sem,
                                        device_id=peer, device_id_type=MESH)


def _exchange_start(srcs, lands, gather, layers, name):
    n = len(srcs)
    hbm = pl.BlockSpec(memory_space=pltpu.HBM)
    sem = pl.BlockSpec(memory_space=pltpu.SEMAPHORE)

    def body(*refs):
        ins, lds = refs[:n], refs[n:2 * n]
        send_sems, recv_sems, token = refs[2 * n], refs[2 * n + 1], refs[-1]
        x, y, c = _place()
        me = 2 * x + y
        for t in range(n):
            for r, (px, py) in enumerate(_other_chips(x, y)):
                _chip_copy(ins[t], lds[t], gather, layers[t], 2 * px + py, me, send_sems.at[3 * t + r],
                           recv_sems.at[3 * t + r], (px, py, c)).start()
        token[...] = jnp.zeros_like(token)

    both = list(srcs) + list(lands)
    outs = pl.pallas_call(
        body, name=name, in_specs=[hbm] * (2 * n),
        out_specs=(sem, sem, *([hbm] * (2 * n)), pl.BlockSpec(memory_space=pltpu.VMEM)),
        out_shape=(pltpu.SemaphoreType.DMA((3 * n,)), pltpu.SemaphoreType.DMA((3 * n,)),
                   *[pltpu.HBM(a.shape, a.dtype) for a in both], jax.ShapeDtypeStruct((8, 128), F32)),
        input_output_aliases={t: 2 + t for t in range(2 * n)},
        compiler_params=_cp(has_side_effects=pltpu.SideEffectType.DATAFLOW_SIDE_EFFECTING),
    )(*[_hbm(a) for a in both])
    return dict(send=outs[0], recv=outs[1], srcs=list(outs[2:2 + n]), lands=list(outs[2 + n:2 + 2 * n]),
                token=outs[-1], gather=gather, layers=list(layers))


def _exchange_wait(groups, lands, land_ids, after, name):
    flat = [s for g in groups for s in g["srcs"]]
    ns, nl, ng, na = len(flat), len(lands), len(groups), len(after)
    hbm = pl.BlockSpec(memory_space=pltpu.HBM)
    sem = pl.BlockSpec(memory_space=pltpu.SEMAPHORE)

    def body(*refs):
        srcs, lds = refs[:ns], refs[ns:ns + nl]
        sems = refs[ns + nl:ns + nl + 2 * ng]
        x, y, c = _place()
        k = 0
        for gi, g in enumerate(groups):
            for t in range(len(g["srcs"])):
                for r, (px, py) in enumerate(_other_chips(x, y)):
                    cp = _chip_copy(srcs[k], lds[land_ids[gi][t]], g["gather"], g["layers"][t], 0, 0,
                                    sems[2 * gi].at[3 * t + r], sems[2 * gi + 1].at[3 * t + r], (px, py, c))
                    cp.wait_send()
                    cp.wait_recv()
                k += 1

    both = flat + list(lands)
    sem_args = [a for g in groups for a in (g["send"], g["recv"])]
    outs = pl.pallas_call(
        body, name=name,
        in_specs=[hbm] * (ns + nl) + [sem] * (2 * ng) + [pl.BlockSpec(memory_space=pl.ANY)] * na,
        out_specs=[hbm] * (ns + nl),
        out_shape=[pltpu.HBM(a.shape, a.dtype) for a in both],
        input_output_aliases={t: t for t in range(ns + nl)},
        compiler_params=_cp(has_side_effects=pltpu.SideEffectType.DATAFLOW_SIDE_EFFECTING),
    )(*both, *sem_args, *after)
    return list(outs[ns:])


def _core_exchange(arrays, name):
    n = len(arrays)
    hbm = pl.BlockSpec(memory_space=pltpu.HBM)

    def body(*refs):
        ins, outs = refs[:n], refs[n:2 * n]
        send_sems, recv_sems = refs[2 * n:]
        x, y, c = _place()
        copies = []
        for t in range(n):
            cp = pltpu.make_async_remote_copy(
                src_ref=ins[t], dst_ref=outs[t], send_sem=send_sems.at[t], recv_sem=recv_sems.at[t],
                device_id=(x, y, 1 - c), device_id_type=MESH)
            cp.start()
            copies.append(cp)
        for cp in copies:
            cp.wait()

    return pl.pallas_call(
        body, name=name, in_specs=[hbm] * n, out_specs=[hbm] * n,
        out_shape=[jax.ShapeDtypeStruct(a.shape, a.dtype) for a in arrays],
        scratch_shapes=[pltpu.SemaphoreType.DMA((n,)), pltpu.SemaphoreType.DMA((n,))],
        compiler_params=_cp(has_side_effects=True),
    )(*arrays)


def _all_reduce_packed(pack, name):
    r, c = pack.shape
    vm = pl.BlockSpec(memory_space=pltpu.VMEM)

    def body(p_ref, sum_ref, slots, send_sems, recv_sems):
        x, y, cc = _place()
        me = 4 * x + 2 * y + cc
        slots[me] = p_ref[...]
        copies = []
        for rel in range(1, 8):
            px = 1 - x if rel & 4 else x
            py = 1 - y if rel & 2 else y
            pc = 1 - cc if rel & 1 else cc
            cp = pltpu.make_async_remote_copy(
                src_ref=p_ref, dst_ref=slots.at[me], send_sem=send_sems.at[rel - 1], recv_sem=recv_sems.at[rel - 1],
                device_id=(px, py, pc), device_id_type=MESH)
            cp.start()
            copies.append(cp)
        for cp in copies:
            cp.wait()
        total = slots[0]
        for k in range(1, 8):
            total = total + slots[k]
        sum_ref[...] = total

    return pl.pallas_call(
        body, name=name, in_specs=[vm], out_specs=vm, out_shape=jax.ShapeDtypeStruct((r, c), F32),
        scratch_shapes=[pltpu.VMEM((8, r, c), F32), pltpu.SemaphoreType.DMA((7,)), pltpu.SemaphoreType.DMA((7,))],
        compiler_params=_cp(has_side_effects=True),
    )(pack)


def _pack(items):
    rows = []
    for a in items:
        flat = a.astype(F32).reshape(-1)
        pad = (-flat.shape[0]) % PACK_W
        rows.append(jnp.pad(flat, (0, pad)).reshape(-1, PACK_W))
    out = jnp.concatenate(rows, axis=0)
    pad_r = (-out.shape[0]) % 8
    return jnp.pad(out, ((0, pad_r), (0, 0)))


def _unpack(pack, shapes):
    outs, row = [], 0
    for s in shapes:
        n = int(np.prod(s))
        nr = -(-n // PACK_W)
        outs.append(pack[row:row + nr].reshape(-1)[:n].reshape(s))
        row += nr
    return outs


def _heads_to_member_major(w, axis):
    shp = w.shape
    pre, post = shp[:axis], shp[axis + 1:]
    w4 = w.reshape(pre + (N_KV_HEADS, GROUP, HEAD_DIM) + post)
    w4 = jnp.swapaxes(w4, len(pre), len(pre) + 1)
    return w4.reshape(shp)


def _heads_to_kv_major(w, axis):
    shp = w.shape
    pre, post = shp[:axis], shp[axis + 1:]
    w4 = w.reshape(pre + (GROUP, N_KV_HEADS, HEAD_DIM) + post)
    w4 = jnp.swapaxes(w4, len(pre), len(pre) + 1)
    return w4.reshape(shp)


def kernel(x, mem, norm_mix, norm_ffn, a_w_in, a_conv_w, a_w_out, kv_norm, w_kv, b_w_q, b_sinks, b_w_out, rel_bias, mem_norm, w_mem_kv, w_gate, w_up, w_down, final_norm, loss_target, m_norm_mix, m_norm_ffn, m_a_w_in, m_a_conv_w, m_a_w_out, m_kv_norm, m_w_kv, m_b_w_q, m_b_sinks, m_b_w_out, m_rel_bias, m_mem_norm, m_w_mem_kv, m_w_gate, m_w_up, m_w_down, m_final_norm, v_norm_mix, v_norm_ffn, v_a_w_in, v_a_conv_w, v_a_w_out, v_kv_norm, v_w_kv, v_b_w_q, v_b_sinks, v_b_w_out, v_rel_bias, v_mem_norm, v_w_mem_kv, v_w_gate, v_w_up, v_w_down, v_final_norm):
    t, d = x.shape[1], x.shape[2]
    tm = 512 if t % 512 == 0 and t >= 2048 else 256
    x0 = x.reshape(t, d)
    target = loss_target.reshape(t, d)
    mem2 = mem.reshape(mem.shape[1], d)
    n_mem = mem2.shape[0]
    ax, ay, ac = _place()
    chip = 2 * ax + ay
    cwid = a_conv_w.shape[2] * N_CHIPS
    qw = N_Q_HEADS * HEAD_DIM
    nq = N_CHIPS

    def own_slot(piece):
        return lax.dynamic_update_slice(lax.empty((nq,) + piece.shape, piece.dtype), piece[None],
                                        (chip,) + (0,) * piece.ndim)

    def layer_shards(i):
        if i < N_A:
            shards = [a_w_in[i], a_w_out[i], w_gate[i], w_up[i], w_down[i]] + ([w_mem_kv] if i == 0 else [])
        else:
            j = i - N_A
            shards = [b_w_q[j], b_w_out[j], w_gate[i], w_up[i], w_down[i]] + ([w_kv] if j == 0 else [])
        return [a.astype(BF16) for a in shards]

    gathers = []
    for i in range(DEPTH):
        shards = layer_shards(i)
        gathers.append(_exchange_start(shards, [own_slot(a) for a in shards], True, [None] * len(shards),
                                       "gather_start_%d" % i))

    def rows_full(g):
        return g.reshape((-1,) + g.shape[2:])

    def cols_full(g):
        return jnp.transpose(g, (1, 0, 2)).reshape(g.shape[1], -1)

    def layer_weights(i, after):
        g = gathers[i]
        got = _exchange_wait([g], g["lands"], [list(range(len(g["lands"])))], after, "gather_wait_%d" % i)
        w = dict(w_first=cols_full(got[0]) if i < N_A else rows_full(got[0]), w_out=rows_full(got[1]),
                 wg=cols_full(got[2]), wu=cols_full(got[3]), wd=rows_full(got[4]))
        if i >= N_A:
            w["w_first"] = jnp.concatenate([_heads_to_member_major(w["w_first"][:, :qw], 1), w["w_first"][:, qw:]], axis=1)
            w["w_out"] = jnp.concatenate([_heads_to_member_major(w["w_out"][:qw, :], 0), w["w_out"][qw:, :]], axis=0)
        if len(got) > 5:
            w["extra"] = got[5]
        return w

    conv_place = jnp.zeros((N_CHIPS,) + a_conv_w.shape, F32).at[chip].set(a_conv_w) * (ac == 0).astype(F32)
    conv_all = _all_reduce_packed(_pack([conv_place]), "gather_conv")
    conv_full = _unpack(conv_all, [conv_place.shape])[0]
    conv_full = jnp.transpose(conv_full, (1, 2, 0, 3)).reshape(N_A, 3, cwid)

    bias, sinkt = [], []
    for j in range(2):
        bj, sj = _bias_tables(rel_bias, b_sinks[j], "bias_tables")
        bias.append(bj)
        sinkt.append(sj)

    ws = []
    xs, xmids, projs, gates, ups = [x0], [], [], [], []
    kv = memkv = wmem = wkv = None
    for i in range(DEPTH):
        xin = xs[-1]
        w = layer_weights(i, [g["token"] for g in gathers] if i == 0 else [xin])
        ws.append(w)
        gm = norm_mix[i].reshape(1, d)
        if i == 0:
            full_mem = jnp.swapaxes(w["extra"], 0, 1).reshape(DEPTH, d, -1)
            wmem = jnp.transpose(full_mem, (1, 0, 2)).reshape(d, -1)
            memkv = _norm_mm(mem2, mem_norm.reshape(1, d), wmem, n_mem, "mem_kv")
        if i < N_A:
            proj = _norm_mm(xin, gm, w["w_first"], tm, "proj_a")
            xmid = _mix_a_fwd(xin, proj, conv_full[i], memkv, i, w["w_out"], tm, "mix_a_fwd")
        else:
            j = i - N_A
            if j == 0:
                wkv = rows_full(w["extra"])
                kv = _norm_mm(xin, kv_norm.reshape(1, d), wkv, tm, "proj_kv")
            proj = _norm_mm(xin, gm, w["w_first"], tm, "proj_b")
            xmid = _mix_b_fwd(xin, proj, kv, bias[j], sinkt[j], memkv, i, w["w_out"], tm, "mix_b_fwd")
        xout, gate, up = _ffn_fwd(xmid, norm_ffn[i].reshape(1, d), w["wg"], w["wu"], w["wd"], tm, "ffn_fwd")
        projs.append(proj)
        xmids.append(xmid)
        gates.append(gate)
        ups.append(up)
        xs.append(xout)

    loss_part, dx, dg_final = _final_loss(xs[-1], final_norm.reshape(1, d), target, tm, "final_loss")

    def rows_pieces(g):
        return g.astype(BF16).reshape((nq, g.shape[0] // nq) + g.shape[1:])

    def cols_pieces(g):
        return jnp.transpose(g.astype(BF16).reshape(g.shape[0], nq, g.shape[1] // nq), (1, 0, 2))

    stacked = dict(a_w_in=a_w_in, a_w_out=a_w_out, w_kv=w_kv[None], b_w_q=b_w_q, b_w_out=b_w_out,
                   w_mem_kv=w_mem_kv, w_gate=w_gate, w_up=w_up, w_down=w_down)
    names = list(stacked)
    land = {k: lax.empty((nq,) + stacked[k].shape, BF16) for k in names}
    own = {k: [None] * stacked[k].shape[0] for k in names}
    scatters, scatter_ids = [], []

    def scatter_start(i, items):
        keys = [k for k, _, _ in items]
        st = _exchange_start([p for _, _, p in items], [land[k] for k in keys], False, [l for _, l, _ in items],
                             "scatter_start_%d" % i)
        for (k, l, p), ld in zip(items, st["lands"]):
            land[k] = ld
            mine_piece = lax.dynamic_index_in_dim(p, chip, 0, keepdims=False)
            if l is None:
                own[k] = [mine_piece[q] for q in range(mine_piece.shape[0])]
            else:
                own[k][l] = mine_piece
        scatters.append(st)
        scatter_ids.append([names.index(k) for k in keys])
        return st["token"][0:1, 0:1]

    g_norm_mix, g_norm_ffn = [None] * DEPTH, [None] * DEPTH
    g_conv, g_sinks = [None] * 2, [None] * 2
    dmemkv = [None] * DEPTH
    dbias, dkv_main, dkv_halo = [None] * 2, [None] * 2, [None] * 2
    g_kv_norm = None
    tok = jnp.zeros((1, 1), F32)
    for i in reversed(range(DEPTH)):
        w = ws[i]
        dxm, dgate, dup, h2, dgf = _ffn_bwd(dx, xmids[i], norm_ffn[i].reshape(1, d) + tok, gates[i], ups[i],
                                            w["wg"], w["wu"], w["wd"], tm // 2, "ffn_bwd")
        g_norm_ffn[i] = dgf
        g_wd = _wgrad_act(gates[i], ups[i], dx, tm, "wgrad_down")
        g_wg = _wgrad(h2, dgate, tm, "wgrad_gate")
        g_wu = _wgrad(h2, dup, tm, "wgrad_up")
        items = [("w_gate", i, cols_pieces(g_wg)), ("w_up", i, cols_pieces(g_wu)), ("w_down", i, rows_pieces(g_wd))]
        gm = norm_mix[i].reshape(1, d)
        if i < N_A:
            dproj, cat, dcw, dmemkv[i] = _mix_a_bwd(dxm, projs[i], conv_full[i], memkv, i, w["w_out"], tm, "mix_a_bwd")
            g_conv[i] = dcw[0:3]
            g_out = _wgrad(cat, dxm, tm, "wgrad_out")
            dx, g_norm_mix[i], h = _mm_nt_normbwd(dproj, w["w_first"], xs[i], gm, dxm, tm, "proj_a_bwd")
            g_in = _wgrad(h, dproj, tm, "wgrad_in_a")
            items += [("a_w_in", i, cols_pieces(g_in)), ("a_w_out", i, rows_pieces(g_out))]
        else:
            j = i - N_A
            dqp, cat, dkv_main[j], dkv_halo[j], dbias[j], dsk, dmemkv[i] = _mix_b_bwd(
                dxm, projs[i], kv, bias[j], sinkt[j], memkv, i, w["w_out"], tm, "mix_b_bwd")
            g_sinks[j] = dsk[0, 0:N_Q_HEADS].reshape(GROUP, N_KV_HEADS).T.reshape(N_Q_HEADS)
            g_out = _wgrad(cat, dxm, tm, "wgrad_out")
            dx, g_norm_mix[i], h = _mm_nt_normbwd(dqp, w["w_first"], xs[i], gm, dxm, tm, "proj_b_bwd")
            g_q = _wgrad(h, dqp, tm, "wgrad_in_b")
            g_q = jnp.concatenate([_heads_to_kv_major(g_q[:, :qw], 1), g_q[:, qw:]], axis=1)
            g_out = jnp.concatenate([_heads_to_kv_major(g_out[:qw, :], 0), g_out[qw:, :]], axis=0)
            items += [("b_w_q", j, rows_pieces(g_q)), ("b_w_out", j, rows_pieces(g_out))]
            if j == 0:
                dkv = _kv_assemble(dkv_main[0], dkv_halo[0], dkv_main[1], dkv_halo[1], tm, "kv_assemble")
                dx, g_kv_norm, hkv = _mm_nt_normbwd(dkv, wkv, xs[i], kv_norm.reshape(1, d), dx, tm, "proj_kv_bwd")
                items.append(("w_kv", 0, rows_pieces(_wgrad(hkv, dkv, tm, "wgrad_kv"))))
        if i == 0:
            dmemkv_all = jnp.concatenate([a.astype(BF16) for a in dmemkv], axis=1)
            _, g_mem_norm, hmem = _mm_nt_normbwd(dmemkv_all, wmem, mem2, mem_norm.reshape(1, d),
                                                 jnp.zeros((n_mem, d), F32), n_mem, "mem_kv_bwd")
            g_wmem = _wgrad(hmem, dmemkv_all, n_mem, "wgrad_mem")
            g_wmem = jnp.transpose(g_wmem.astype(BF16).reshape(nq, d // nq, DEPTH, -1), (0, 2, 1, 3))
            items.append(("w_mem_kv", None, g_wmem))
        tok = scatter_start(i, items)
    grad_x = dx.reshape(x.shape)
    g_rel = _bias_bwd(dbias[0], dbias[1], "bias_bwd")[:, 0:N_Q_HEADS]

    small_shapes = [(DEPTH, d), (DEPTH, d), (d,), (d,), (d,), (2, N_Q_HEADS), (REL_BUCKETS, N_Q_HEADS),
                    (N_A, 3, cwid), ()]
    small = _pack([jnp.concatenate(g_norm_mix, axis=0), jnp.concatenate(g_norm_ffn, axis=0), g_kv_norm, g_mem_norm,
                   dg_final, jnp.stack(g_sinks), g_rel, jnp.stack(g_conv), loss_part[0, 0]])
    small_sum = _all_reduce_packed(small, "reduce_small")
    (gs_norm_mix, gs_norm_ffn, gs_kv_norm, gs_mem_norm, gs_final, gs_sinks, gs_rel, gs_conv_full, loss) = _unpack(
        small_sum, small_shapes)
    cq = cwid // N_CHIPS
    gs_conv = lax.dynamic_slice_in_dim(gs_conv_full, chip * cq, cq, axis=2)

    landed = _exchange_wait(scatters, [land[k] for k in names], scatter_ids, [small_sum], "scatter_wait")
    mine = [lax.dynamic_update_slice(ld, jnp.stack(own[k])[None], (chip,) + (0,) * (ld.ndim - 1))
            for k, ld in zip(names, landed)]
    theirs = _core_exchange(mine, "swap_cores")

    weights = dict(norm_mix=norm_mix, norm_ffn=norm_ffn, a_w_in=a_w_in, a_conv_w=a_conv_w, a_w_out=a_w_out,
                   kv_norm=kv_norm, w_kv=w_kv, b_w_q=b_w_q, b_sinks=b_sinks, b_w_out=b_w_out, rel_bias=rel_bias,
                   mem_norm=mem_norm, w_mem_kv=w_mem_kv, w_gate=w_gate, w_up=w_up, w_down=w_down,
                   final_norm=final_norm)
    moms = dict(norm_mix=m_norm_mix, norm_ffn=m_norm_ffn, a_w_in=m_a_w_in, a_conv_w=m_a_conv_w, a_w_out=m_a_w_out,
                kv_norm=m_kv_norm, w_kv=m_w_kv, b_w_q=m_b_w_q, b_sinks=m_b_sinks, b_w_out=m_b_w_out,
                rel_bias=m_rel_bias, mem_norm=m_mem_norm, w_mem_kv=m_w_mem_kv, w_gate=m_w_gate, w_up=m_w_up,
                w_down=m_w_down, final_norm=m_final_norm)
    vars_ = dict(norm_mix=v_norm_mix, norm_ffn=v_norm_ffn, a_w_in=v_a_w_in, a_conv_w=v_a_conv_w, a_w_out=v_a_w_out,
                 kv_norm=v_kv_norm, w_kv=v_w_kv, b_w_q=v_b_w_q, b_sinks=v_b_sinks, b_w_out=v_b_w_out,
                 rel_bias=v_rel_bias, mem_norm=v_mem_norm, w_mem_kv=v_w_mem_kv, w_gate=v_w_gate, w_up=v_w_up,
                 w_down=v_w_down, final_norm=v_final_norm)
    order = list(weights)
    grads, deltas, new_m, new_v = {}, {}, {}, {}
    for k, mi, th in zip(names, mine, theirs):
        shp = weights[k].shape
        g, dl, m2, v2 = _adamw_sharded(weights[k].reshape(mi.shape[1:]), mi, th, moms[k].reshape(mi.shape[1:]),
                                       vars_[k].reshape(mi.shape[1:]), "adamw_" + k)
        grads[k], deltas[k], new_m[k], new_v[k] = g.reshape(shp), dl.reshape(shp), m2.reshape(shp), v2.reshape(shp)
    small_names = ["norm_mix", "norm_ffn", "kv_norm", "mem_norm", "final_norm", "b_sinks", "rel_bias", "a_conv_w"]
    small_g = [gs_norm_mix, gs_norm_ffn, gs_kv_norm, gs_mem_norm, gs_final, gs_sinks, gs_rel, gs_conv]
    shapes = [weights[k].shape for k in small_names]
    dl_p, m_p, v_p = _adamw_packed(_pack([weights[k] for k in small_names]), _pack(small_g),
                                   _pack([moms[k] for k in small_names]), _pack([vars_[k] for k in small_names]),
                                   "adamw_small")
    for k, g, dl, m2, v2 in zip(small_names, small_g, _unpack(dl_p, shapes), _unpack(m_p, shapes), _unpack(v_p, shapes)):
        grads[k], deltas[k], new_m[k], new_v[k] = g.reshape(weights[k].shape), dl, m2, v2

    return (loss, grad_x, *[grads[k] for k in order], *[deltas[k] for k in order],
            *[new_m[k] for k in order], *[new_v[k] for k in order])
```

```python
import functools
import math

import numpy as np
import jax
import jax.numpy as jnp
from jax import lax
from jax.experimental import pallas as pl
from jax.experimental.pallas import tpu as pltpu

F32 = jnp.float32
BF16 = jnp.bfloat16
MESH = pl.DeviceIdType.MESH

EPS = 1e-5
HEAD_DIM = 64
N_MEM_HEADS = 4
N_KV_HEADS = 4
GROUP = 3
N_Q_HEADS = N_KV_HEADS * GROUP
BLOCK = 128
REL_BUCKETS = 32
REL_MAX_DIST = 128
SCALE = HEAD_DIM ** -0.5
NEG = -1e30
N_CHIPS = 4
N_A = 2
DEPTH = 4

ADAM_LR = 0.001
ADAM_B1 = 0.9
ADAM_B2 = 0.999
ADAM_EPS = 1e-08
ADAM_WD = 0.01
ADAM_STEP = 10

VMEM_BIG = 56 * 1024 * 1024
PACK_W = 1024

NT = (((1,), (1,)), ((), ()))
TN = (((0,), (0,)), ((), ()))


def _cp(sem=None, vmem=None, **kw):
    return pltpu.CompilerParams(dimension_semantics=sem, vmem_limit_bytes=vmem, **kw)


def _const_spec(shape):
    nd = len(shape)
    return pl.BlockSpec(shape, lambda i, _n=nd: (0,) * _n, pipeline_mode=pl.Buffered(1))


def _row_spec(tm, n):
    return pl.BlockSpec((tm, n), lambda i: (i, 0))


def _rms_parts(xv):
    r = lax.rsqrt(jnp.mean(xv * xv, axis=-1, keepdims=True) + EPS)
    return xv * r, r


def _sigmoid(z):
    return 1.0 / (1.0 + jnp.exp(-z))


def _ff_chunks(f):
    if f % 512 == 0 or f % 256 != 0:
        return [(0, f)] if f <= 1536 else [(0, f // 2), (f // 2, f - f // 2)]
    n = f // 256
    a = (n + 1) // 2 * 256
    return [(0, a), (a, f - a)]


def _norm_mm(x, g, w, tm, name):
    t, d = x.shape
    n = w.shape[1]

    def body(x_ref, g_ref, w_ref, o_ref):
        xhat, _ = _rms_parts(x_ref[...])
        h = (xhat * g_ref[...]).astype(BF16)
        o_ref[...] = jnp.dot(h, w_ref[...], preferred_element_type=F32).astype(BF16)

    return pl.pallas_call(
        body, name=name, grid=(t // tm,),
        in_specs=[_row_spec(tm, d), _const_spec((1, d)), _const_spec((d, n))],
        out_specs=_row_spec(tm, n),
        out_shape=jax.ShapeDtypeStruct((t, n), BF16),
        compiler_params=_cp(("parallel",), VMEM_BIG),
    )(x, g, w)


def _mm_nt_normbwd(dproj, w, x_in, g, dres, tm, name):
    t, d = x_in.shape
    n = w.shape[1]

    def body(dp_ref, w_ref, x_ref, g_ref, dr_ref, dx_ref, dg_ref, h_ref):
        i = pl.program_id(0)
        xhat, r = _rms_parts(x_ref[...])
        gv = g_ref[...]
        h_ref[...] = (xhat * gv).astype(BF16)
        dh = lax.dot_general(dp_ref[...], w_ref[...], NT, preferred_element_type=F32)
        dxhat = dh * gv
        dx = r * (dxhat - xhat * jnp.mean(dxhat * xhat, axis=-1, keepdims=True))
        dx_ref[...] = dr_ref[...] + dx

        @pl.when(i == 0)
        def _():
            dg_ref[...] = jnp.zeros_like(dg_ref)

        dg_ref[...] += jnp.sum(dh * xhat, axis=0, keepdims=True)

    return pl.pallas_call(
        body, name=name, grid=(t // tm,),
        in_specs=[_row_spec(tm, n), _const_spec((d, n)), _row_spec(tm, d), _const_spec((1, d)), _row_spec(tm, d)],
        out_specs=[_row_spec(tm, d), pl.BlockSpec((1, d), lambda i: (0, 0)), _row_spec(tm, d)],
        out_shape=[jax.ShapeDtypeStruct((t, d), F32), jax.ShapeDtypeStruct((1, d), F32),
                   jax.ShapeDtypeStruct((t, d), BF16)],
        compiler_params=_cp(("arbitrary",), VMEM_BIG),
    )(dproj, w, x_in, g, dres)


def _ffn_fwd(x, g, wg, wu, wd, tm, name):
    t, d = x.shape
    f = wg.shape[1]
    chunks = _ff_chunks(f)

    def body(x_ref, g_ref, wg_ref, wu_ref, wd_ref, xo_ref, gate_ref, up_ref):
        xv = x_ref[...]
        xhat, _ = _rms_parts(xv)
        h = (xhat * g_ref[...]).astype(BF16)
        acc = xv
        for c0, cw in chunks:
            gt = jnp.dot(h, wg_ref[:, c0:c0 + cw], preferred_element_type=F32)
            ut = jnp.dot(h, wu_ref[:, c0:c0 + cw], preferred_element_type=F32)
            gate_ref[:, c0:c0 + cw] = gt.astype(BF16)
            up_ref[:, c0:c0 + cw] = ut.astype(BF16)
            a = (gt * _sigmoid(gt) * ut).astype(BF16)
            acc = acc + jnp.dot(a, wd_ref[c0:c0 + cw, :], preferred_element_type=F32)
        xo_ref[...] = acc

    return pl.pallas_call(
        body, name=name, grid=(t // tm,),
        in_specs=[_row_spec(tm, d), _const_spec((1, d)), _const_spec((d, f)), _const_spec((d, f)), _const_spec((f, d))],
        out_specs=[_row_spec(tm, d), _row_spec(tm, f), _row_spec(tm, f)],
        out_shape=[jax.ShapeDtypeStruct((t, d), F32), jax.ShapeDtypeStruct((t, f), BF16),
                   jax.ShapeDtypeStruct((t, f), BF16)],
        compiler_params=_cp(("parallel",), VMEM_BIG),
    )(x, g, wg, wu, wd)


def _ffn_bwd(dxo, xm, g, gate, up, wg, wu, wd, tm, name):
    t, d = xm.shape
    f = wg.shape[1]
    chunks = _ff_chunks(f)

    def body(dxo_ref, xm_ref, g_ref, gate_ref, up_ref, wg_ref, wu_ref, wd_ref,
             dxm_ref, dgate_ref, dup_ref, act_ref, h2_ref, dg_ref):
        i = pl.program_id(0)
        dxo_v = dxo_ref[...]
        dxo_b = dxo_v.astype(BF16)
        xhat, r = _rms_parts(xm_ref[...])
        gv = g_ref[...]
        h2_ref[...] = (xhat * gv).astype(BF16)
        dh = jnp.zeros((tm, d), F32)
        for c0, cw in chunks:
            dact = lax.dot_general(dxo_b, wd_ref[c0:c0 + cw, :], NT, preferred_element_type=F32)
            gt = gate_ref[:, c0:c0 + cw].astype(F32)
            ut = up_ref[:, c0:c0 + cw].astype(F32)
            sg = _sigmoid(gt)
            sl = gt * sg
            act_ref[:, c0:c0 + cw] = (sl * ut).astype(BF16)
            dgt = (dact * ut * (sg * (1.0 + gt * (1.0 - sg)))).astype(BF16)
            dut = (dact * sl).astype(BF16)
            dgate_ref[:, c0:c0 + cw] = dgt
            dup_ref[:, c0:c0 + cw] = dut
            dh = dh + lax.dot_general(dgt, wg_ref[:, c0:c0 + cw], NT, preferred_element_type=F32)
            dh = dh + lax.dot_general(dut, wu_ref[:, c0:c0 + cw], NT, preferred_element_type=F32)
        dxhat = dh * gv
        dx = r * (dxhat - xhat * jnp.mean(dxhat * xhat, axis=-1, keepdims=True))
        dxm_ref[...] = dxo_v + dx

        @pl.when(i == 0)
        def _():
            dg_ref[...] = jnp.zeros_like(dg_ref)

        dg_ref[...] += jnp.sum(dh * xhat, axis=0, keepdims=True)

    return pl.pallas_call(
        body, name=name, grid=(t // tm,),
        in_specs=[_row_spec(tm, d), _row_spec(tm, d), _const_spec((1, d)), _row_spec(tm, f), _row_spec(tm, f),
                  _const_spec((d, f)), _const_spec((d, f)), _const_spec((f, d))],
        out_specs=[_row_spec(tm, d), _row_spec(tm, f), _row_spec(tm, f), _row_spec(tm, f), _row_spec(tm, d),
                   pl.BlockSpec((1, d), lambda i: (0, 0))],
        out_shape=[jax.ShapeDtypeStruct((t, d), F32), jax.ShapeDtypeStruct((t, f), BF16),
                   jax.ShapeDtypeStruct((t, f), BF16), jax.ShapeDtypeStruct((t, f), BF16),
                   jax.ShapeDtypeStruct((t, d), BF16), jax.ShapeDtypeStruct((1, d), F32)],
        compiler_params=_cp(("arbitrary",), VMEM_BIG),
    )(dxo, xm, g, gate, up, wg, wu, wd)


def _wgrad(a, b, tt, name):
    t, k = a.shape
    n = b.shape[1]

    def body(a_ref, b_ref, o_ref):
        i = pl.program_id(0)

        @pl.when(i == 0)
        def _():
            o_ref[...] = jnp.zeros_like(o_ref)

        o_ref[...] += lax.dot_general(a_ref[...].astype(BF16), b_ref[...].astype(BF16), TN,
                                      preferred_element_type=F32)

    return pl.pallas_call(
        body, name=name, grid=(t // tt,),
        in_specs=[_row_spec(tt, k), _row_spec(tt, n)],
        out_specs=pl.BlockSpec((k, n), lambda i: (0, 0)),
        out_shape=jax.ShapeDtypeStruct((k, n), F32),
        compiler_params=_cp(("arbitrary",), VMEM_BIG),
    )(a, b)


def _final_loss(x, g, target, tm, name):
    t, d = x.shape

    def body(x_ref, g_ref, t_ref, loss_ref, dx_ref, dg_ref):
        i = pl.program_id(0)
        xhat, r = _rms_parts(x_ref[...])
        gv = g_ref[...]
        err = xhat * gv - t_ref[...]
        dy = err * (1.0 / d)
        dxhat = dy * gv
        dx_ref[...] = r * (dxhat - xhat * jnp.mean(dxhat * xhat, axis=-1, keepdims=True))

        @pl.when(i == 0)
        def _():
            dg_ref[...] = jnp.zeros_like(dg_ref)
            loss_ref[...] = jnp.zeros_like(loss_ref)

        dg_ref[...] += jnp.sum(dy * xhat, axis=0, keepdims=True)
        part = jnp.sum(jnp.sum(err * err, axis=-1, keepdims=True), axis=0, keepdims=True) * (0.5 / d)
        loss_ref[...] += jnp.broadcast_to(part, loss_ref.shape)

    return pl.pallas_call(
        body, name=name, grid=(t // tm,),
        in_specs=[_row_spec(tm, d), _const_spec((1, d)), _row_spec(tm, d)],
        out_specs=[pl.BlockSpec((8, 128), lambda i: (0, 0)), _row_spec(tm, d), pl.BlockSpec((1, d), lambda i: (0, 0))],
        out_shape=[jax.ShapeDtypeStruct((8, 128), F32), jax.ShapeDtypeStruct((t, d), F32),
                   jax.ShapeDtypeStruct((1, d), F32)],
        compiler_params=_cp(("arbitrary",)),
    )(x, g, target)


def _col_head(width):
    return lax.broadcasted_iota(jnp.int32, (1, width), 1) // HEAD_DIM


def _keep_head(a, colh, h):
    return jnp.where(colh == h, a, jnp.zeros_like(a))


def _softmax_cols(s, sink=None):
    m = jnp.max(s, axis=0, keepdims=True)
    if sink is not None:
        m = jnp.maximum(m, sink)
    p = jnp.exp(s - m)
    l = jnp.sum(p, axis=0, keepdims=True)
    if sink is None:
        return p * (1.0 / l), None
    es = jnp.exp(sink - m)
    inv = 1.0 / (l + es)
    return p * inv, es * inv


def _add4(v):
    return (v[0] + v[1]) + (v[2] + v[3])


def _mem_attn_fwd(qm, mk, mv):
    colh = _col_head(mk.shape[1])
    mks = mk * SCALE
    heads = range(N_MEM_HEADS)
    ss = [lax.dot_general(_keep_head(mks, colh, h), qm, NT, preferred_element_type=F32) for h in heads]
    ps = [_softmax_cols(s)[0].astype(BF16) for s in ss]
    return _add4([lax.dot_general(ps[h], _keep_head(mv, colh, h), TN, preferred_element_type=F32) for h in heads])


def _mem_attn_bwd(qm, dy_b, mk, mv):
    colh = _col_head(mk.shape[1])
    mks = mk * SCALE
    heads = range(N_MEM_HEADS)
    khs = [_keep_head(mks, colh, h) for h in heads]
    vhs = [_keep_head(mv, colh, h) for h in heads]
    ss = [lax.dot_general(khs[h], qm, NT, preferred_element_type=F32) for h in heads]
    dps = [lax.dot_general(vhs[h], dy_b, NT, preferred_element_type=F32) for h in heads]
    pbs, dsbs = [], []
    for h in heads:
        p, _ = _softmax_cols(ss[h])
        ds = p * (dps[h] - jnp.sum(p * dps[h], axis=0, keepdims=True))
        pbs.append(p.astype(BF16))
        dsbs.append(ds.astype(BF16))
    y = _add4([lax.dot_general(pbs[h], vhs[h], TN, preferred_element_type=F32) for h in heads])
    dq = _add4([lax.dot_general(dsbs[h], khs[h], TN, preferred_element_type=F32) for h in heads])
    dmk = _add4([jnp.where(colh == h, jnp.dot(dsbs[h], qm, preferred_element_type=F32) * SCALE, 0.0) for h in heads])
    dmv = _add4([jnp.where(colh == h, jnp.dot(pbs[h], dy_b, preferred_element_type=F32), 0.0) for h in heads])
    return y, dq, dmk, dmv


def _shift_down(v, halo, k):
    rolled = pltpu.roll(v, k, 0)
    hrolled = pltpu.roll(halo, k, 0)[0:8]
    rows = lax.broadcasted_iota(jnp.int32, (8, v.shape[1]), 0)
    first = jnp.where(rows < k, hrolled, rolled[0:8])
    return jnp.concatenate([first, rolled[8:]], axis=0)


def _shift_up(v, halo, k):
    n = v.shape[0]
    rolled = pltpu.roll(v, n - k, 0)
    hrolled = pltpu.roll(halo, 8 - k, 0)[0:8]
    rows = lax.broadcasted_iota(jnp.int32, (8, v.shape[1]), 0)
    last = jnp.where(rows >= 8 - k, hrolled, rolled[n - 8:])
    return jnp.concatenate([rolled[:n - 8], last], axis=0)


def _conv_parts(p, ph, cw, first_tile, cwid):
    u = p[:, 0:cwid].astype(F32)
    bg = p[:, cwid:2 * cwid].astype(F32)
    cg = p[:, 2 * cwid:3 * cwid].astype(F32)
    v = cg * u
    vh = ph[:, 2 * cwid:3 * cwid].astype(F32) * ph[:, 0:cwid].astype(F32)
    vh = jnp.where(first_tile, 0.0, vh)
    v1 = _shift_down(v, vh, 1)
    v2 = _shift_down(v, vh, 2)
    conv = cw[0:1, :] * v2 + cw[1:2, :] * v1 + cw[2:3, :] * v
    return u, bg, cg, v, v1, v2, conv


def _halo_prev_spec(rows, n, tm):
    per = tm // rows
    return pl.BlockSpec((rows, n), lambda i: (jnp.maximum(i * per - 1, 0), 0))


def _halo_next_spec(rows, n, tm, t):
    per = tm // rows
    last = t // rows - 1
    return pl.BlockSpec((rows, n), lambda i: (jnp.minimum((i + 1) * per, last), 0))


def _mix_a_fwd(x, proj, convw, memkv, layer, wout, tm, name):
    t, d = x.shape
    n_mem = memkv.shape[0]
    mw = N_MEM_HEADS * HEAD_DIM
    cwid = d - mw
    pw = proj.shape[1]

    def body(x_ref, p_ref, ph_ref, cw_ref, mkv_ref, wo_ref, xo_ref):
        i = pl.program_id(0)
        p = p_ref[...]
        _, bg, _, _, _, _, conv = _conv_parts(p, ph_ref[...], cw_ref[...], i == 0, cwid)
        ytok = (bg * conv).astype(BF16)
        mkv = mkv_ref[...]
        ymem = _mem_attn_fwd(p[:, 3 * cwid:3 * cwid + mw], mkv[:, 0:mw], mkv[:, mw:2 * mw])
        cat = jnp.concatenate([ytok, ymem.astype(BF16)], axis=1)
        xo_ref[...] = x_ref[...] + jnp.dot(cat, wo_ref[...], preferred_element_type=F32)

    return pl.pallas_call(
        body, name=name, grid=(t // tm,),
        in_specs=[_row_spec(tm, d), _row_spec(tm, pw), _halo_prev_spec(16, pw, tm), _const_spec((3, cwid)),
                  pl.BlockSpec((n_mem, 2 * mw), lambda i: (0, layer)), _const_spec((d, d))],
        out_specs=_row_spec(tm, d),
        out_shape=jax.ShapeDtypeStruct((t, d), F32),
        compiler_params=_cp(("parallel",), VMEM_BIG),
    )(x, proj, proj, convw, memkv, wout)


def _mix_a_bwd(dxm, proj, convw, memkv, layer, wout, tm, name):
    t, d = dxm.shape
    n_mem = memkv.shape[0]
    mw = N_MEM_HEADS * HEAD_DIM
    cwid = d - mw
    pw = proj.shape[1]
    nt = t // tm

    def body(dx_ref, dxn_ref, p_ref, ph_ref, pn_ref, cw_ref, mkv_ref, wo_ref,
             dp_ref, cat_ref, dcw_ref, dmkv_ref, dmk_acc, dmv_acc):
        i = pl.program_id(0)
        p = p_ref[...]
        cw = cw_ref[...]
        wo = wo_ref[...]
        u, bg, cg, v, v1, v2, conv = _conv_parts(p, ph_ref[...], cw, i == 0, cwid)
        dcat = lax.dot_general(dx_ref[...].astype(BF16), wo, NT, preferred_element_type=F32)
        dytok = dcat[:, 0:cwid]
        dymem_b = dcat[:, cwid:d].astype(BF16)
        pn = pn_ref[...]
        dcat_n = lax.dot_general(dxn_ref[...].astype(BF16), wo[0:cwid, :], NT, preferred_element_type=F32)
        dconv_n = jnp.where(i == nt - 1, 0.0, dcat_n * pn[:, cwid:2 * cwid].astype(F32))
        dbg = dytok * conv
        dconv = dytok * bg
        dv = cw[2:3, :] * dconv + cw[1:2, :] * _shift_up(dconv, dconv_n, 1) + cw[0:1, :] * _shift_up(dconv, dconv_n, 2)
        du = dv * cg
        dcg = dv * u
        rows8 = lax.broadcasted_iota(jnp.int32, (8, cwid), 0)
        dcw = (jnp.where(rows8 == 0, jnp.sum(dconv * v2, axis=0, keepdims=True), 0.0)
               + jnp.where(rows8 == 1, jnp.sum(dconv * v1, axis=0, keepdims=True), 0.0)
               + jnp.where(rows8 == 2, jnp.sum(dconv * v, axis=0, keepdims=True), 0.0))
        mkv = mkv_ref[...]
        qm = p[:, 3 * cwid:3 * cwid + mw]
        ymem, dqm, dmk, dmv = _mem_attn_bwd(qm, dymem_b, mkv[:, 0:mw], mkv[:, mw:2 * mw])
        cat_ref[...] = jnp.concatenate([(bg * conv).astype(BF16), ymem.astype(BF16)], axis=1)
        dp_ref[...] = jnp.concatenate([du.astype(BF16), dbg.astype(BF16), dcg.astype(BF16), dqm.astype(BF16)], axis=1)

        @pl.when(i == 0)
        def _():
            dcw_ref[...] = jnp.zeros_like(dcw_ref)
            dmk_acc[...] = jnp.zeros_like(dmk_acc)
            dmv_acc[...] = jnp.zeros_like(dmv_acc)

        dcw_ref[...] += dcw
        dmk_acc[...] += dmk
        dmv_acc[...] += dmv

        @pl.when(i == nt - 1)
        def _():
            dmkv_ref[...] = jnp.concatenate([dmk_acc[...], dmv_acc[...]], axis=1)

    return pl.pallas_call(
        body, name=name, grid=(nt,),
        in_specs=[_row_spec(tm, d), _halo_next_spec(16, d, tm, t), _row_spec(tm, pw), _halo_prev_spec(16, pw, tm),
                  _halo_next_spec(16, pw, tm, t), _const_spec((3, cwid)),
                  pl.BlockSpec((n_mem, 2 * mw), lambda i: (0, layer)), _const_spec((d, d))],
        out_specs=[_row_spec(tm, pw), _row_spec(tm, d), pl.BlockSpec((8, cwid), lambda i: (0, 0)),
                   pl.BlockSpec((n_mem, 2 * mw), lambda i: (0, 0))],
        out_shape=[jax.ShapeDtypeStruct((t, pw), BF16), jax.ShapeDtypeStruct((t, d), BF16),
                   jax.ShapeDtypeStruct((8, cwid), F32), jax.ShapeDtypeStruct((n_mem, 2 * mw), F32)],
        scratch_shapes=[pltpu.VMEM((n_mem, mw), F32), pltpu.VMEM((n_mem, mw), F32)],
        compiler_params=_cp(("arbitrary",), VMEM_BIG),
    )(dxm, dxm, proj, proj, proj, convw, memkv, wout)


def _rel_tables():
    qi = np.arange(BLOCK, dtype=np.int32)[:, None]
    kj = np.arange(2 * BLOCK, dtype=np.int32)[None, :]
    dist = qi + BLOCK - kj
    inw = (dist >= 0) & (dist < BLOCK)
    max_exact = REL_BUCKETS // 2
    dd = np.maximum(np.maximum(dist, 0), 1).astype(np.float32)
    large = max_exact + (np.log(dd / np.float32(max_exact)) / np.float32(math.log(REL_MAX_DIST / max_exact))
                         * np.float32(REL_BUCKETS - max_exact)).astype(np.int32)
    large = np.minimum(large, REL_BUCKETS - 1)
    bucket = np.where(np.maximum(dist, 0) < max_exact, np.maximum(dist, 0), large)
    return np.where(inw, bucket, -1).astype(np.int32)


def _bias_tables(rel_bias, sinks, name):
    bucket_t = jnp.asarray(_rel_tables().T)

    def body(rb_ref, sk_ref, bk_ref, bias_ref, sink_ref):
        bk = bk_ref[...]
        prev = lax.broadcasted_iota(jnp.int32, bk.shape, 0) < BLOCK
        for h in range(N_KV_HEADS):
            for j in range(GROUP):
                head = GROUP * h + j
                acc = jnp.full(bk.shape, NEG, F32)
                for b in range(REL_BUCKETS):
                    acc = jnp.where(bk == b, rb_ref[b, head], acc)
                bias_ref[h, :, j * BLOCK:(j + 1) * BLOCK] = acc
                bias_ref[N_KV_HEADS + h, :, j * BLOCK:(j + 1) * BLOCK] = jnp.where(prev, NEG, acc)
                sink_ref[h, :, j * BLOCK:(j + 1) * BLOCK] = jnp.full((8, BLOCK), sk_ref[0, head], F32)

    smem = pl.BlockSpec(memory_space=pltpu.SMEM)
    return pl.pallas_call(
        body, name=name,
        in_specs=[smem, smem, pl.BlockSpec(memory_space=pltpu.VMEM)],
        out_specs=[pl.BlockSpec(memory_space=pltpu.VMEM), pl.BlockSpec(memory_space=pltpu.VMEM)],
        out_shape=[jax.ShapeDtypeStruct((2 * N_KV_HEADS, 2 * BLOCK, GROUP * BLOCK), F32),
                   jax.ShapeDtypeStruct((N_KV_HEADS, 8, GROUP * BLOCK), F32)],
    )(rel_bias, sinks.reshape(1, N_Q_HEADS), bucket_t)


def _bias_bwd(dbias_a, dbias_b, name):
    bucket_t = jnp.asarray(_rel_tables().T)

    def body(da_ref, db_ref, bk_ref, o_ref):
        bk = bk_ref[...]
        ri = lax.broadcasted_iota(jnp.int32, (REL_BUCKETS, 128), 0)
        ci = lax.broadcasted_iota(jnp.int32, (REL_BUCKETS, 128), 1)
        out = jnp.zeros((REL_BUCKETS, 128), F32)
        for h in range(N_KV_HEADS):
            dsum = da_ref[h] + db_ref[h]
            for j in range(GROUP):
                head = GROUP * h + j
                seg = dsum[:, j * BLOCK:(j + 1) * BLOCK]
                for b in range(REL_BUCKETS):
                    val = jnp.sum(jnp.sum(jnp.where(bk == b, seg, 0.0), axis=0, keepdims=True), axis=1, keepdims=True)
                    out = out + jnp.where((ri == b) & (ci == head), val, 0.0)
        o_ref[...] = out

    vm = pl.BlockSpec(memory_space=pltpu.VMEM)
    return pl.pallas_call(
        body, name=name, in_specs=[vm, vm, vm], out_specs=vm,
        out_shape=jax.ShapeDtypeStruct((REL_BUCKETS, 128), F32),
    )(dbias_a, dbias_b, bucket_t)


def _stack_members(ref, r0, width):
    blk = ref[pl.ds(r0, BLOCK), 0:GROUP * width]
    return jnp.concatenate([blk[:, j * width:(j + 1) * width] for j in range(GROUP)], axis=0)


def _mix_b_fwd(x, qp, kv, bias, sinkt, memkv, layer, wout, tm, name):
    t, d = x.shape
    n_mem = memkv.shape[0]
    mw = N_MEM_HEADS * HEAD_DIM
    qw = d - mw
    kw = N_KV_HEADS * HEAD_DIM
    nb = tm // BLOCK
    rows = GROUP * BLOCK

    def body(x_ref, q_ref, kv_ref, kvh_ref, bias_ref, sink_ref, mkv_ref, wo_ref, xo_ref, kvx, ytok):
        i = pl.program_id(0)
        kvx[0:BLOCK, :] = kvh_ref[...]
        kvx[BLOCK:BLOCK + tm, :] = kv_ref[...]
        colh = _col_head(kw)

        def blk(b, carry):
            r0 = pl.multiple_of(b * BLOCK, BLOCK)
            win = kvx[pl.ds(r0, 2 * BLOCK), :]
            kwin = win[:, 0:kw] * SCALE
            vwin = win[:, kw:2 * kw]
            qs = _stack_members(q_ref, r0, kw)
            first = ((i == 0) & (b == 0)).astype(jnp.int32) * N_KV_HEADS
            heads = range(N_KV_HEADS)
            ss = [lax.dot_general(_keep_head(kwin, colh, h), qs, NT, preferred_element_type=F32) for h in heads]
            ps = [_softmax_cols(ss[h] + bias_ref[first + h], sink_ref[h][0:1, :])[0].astype(BF16) for h in heads]
            o = _add4([lax.dot_general(ps[h], _keep_head(vwin, colh, h), TN, preferred_element_type=F32)
                       for h in heads])
            for j in range(GROUP):
                ytok[pl.ds(r0, BLOCK), j * kw:(j + 1) * kw] = o[j * BLOCK:(j + 1) * BLOCK].astype(BF16)
            return carry

        lax.fori_loop(0, nb, blk, 0)
        mkv = mkv_ref[...]
        ymem = _mem_attn_fwd(q_ref[:, qw:d], mkv[:, 0:mw], mkv[:, mw:2 * mw])
        cat = jnp.concatenate([ytok[...], ymem.astype(BF16)], axis=1)
        xo_ref[...] = x_ref[...] + jnp.dot(cat, wo_ref[...], preferred_element_type=F32)

    return pl.pallas_call(
        body, name=name, grid=(t // tm,),
        in_specs=[_row_spec(tm, d), _row_spec(tm, d), _row_spec(tm, 2 * kw), _halo_prev_spec(BLOCK, 2 * kw, tm),
                  _const_spec((2 * N_KV_HEADS, 2 * BLOCK, rows)), _const_spec((N_KV_HEADS, 8, rows)),
                  pl.BlockSpec((n_mem, 2 * mw), lambda i: (0, layer)), _const_spec((d, d))],
        out_specs=_row_spec(tm, d),
        out_shape=jax.ShapeDtypeStruct((t, d), F32),
        scratch_shapes=[pltpu.VMEM((tm + BLOCK, 2 * kw), BF16), pltpu.VMEM((tm, qw), BF16)],
        compiler_params=_cp(("parallel",), VMEM_BIG),
    )(x, qp, kv, kv, bias, sinkt, memkv, wout)


def _mix_b_bwd(dxm, qp, kv, bias, sinkt, memkv, layer, wout, tm, name):
    t, d = dxm.shape
    n_mem = memkv.shape[0]
    mw = N_MEM_HEADS * HEAD_DIM
    qw = d - mw
    kw = N_KV_HEADS * HEAD_DIM
    nb = tm // BLOCK
    nt = t // tm
    rows = GROUP * BLOCK

    def body(dx_ref, q_ref, kv_ref, kvh_ref, bias_ref, sink_ref, mkv_ref, wo_ref,
             dq_ref, cat_ref, dkv_ref, dkvh_ref, dbias_ref, dsink_ref, dmkv_ref,
             kvx, dkvx, dcat_s, dmk_acc, dmv_acc):
        i = pl.program_id(0)

        @pl.when(i == 0)
        def _():
            dbias_ref[...] = jnp.zeros_like(dbias_ref)
            dsink_ref[...] = jnp.zeros_like(dsink_ref)
            dmk_acc[...] = jnp.zeros_like(dmk_acc)
            dmv_acc[...] = jnp.zeros_like(dmv_acc)

        kvx[0:BLOCK, :] = kvh_ref[...]
        kvx[BLOCK:BLOCK + tm, :] = kv_ref[...]
        dkvx[...] = jnp.zeros_like(dkvx)
        dcat_s[...] = lax.dot_general(dx_ref[...].astype(BF16), wo_ref[...], NT,
                                      preferred_element_type=F32).astype(BF16)
        colh = _col_head(kw)
        lane8 = lax.broadcasted_iota(jnp.int32, (8, 128), 1)

        def blk(b, carry):
            r0 = pl.multiple_of(b * BLOCK, BLOCK)
            win = kvx[pl.ds(r0, 2 * BLOCK), :]
            kwin = win[:, 0:kw] * SCALE
            vwin = win[:, kw:2 * kw]
            qs = _stack_members(q_ref, r0, kw)
            dos = _stack_members(dcat_s, r0, kw)
            first = ((i == 0) & (b == 0)).astype(jnp.int32) * N_KV_HEADS
            heads = range(N_KV_HEADS)
            khs = [_keep_head(kwin, colh, h) for h in heads]
            vhs = [_keep_head(vwin, colh, h) for h in heads]
            ss = [lax.dot_general(khs[h], qs, NT, preferred_element_type=F32) for h in heads]
            dps = [lax.dot_general(vhs[h], dos, NT, preferred_element_type=F32) for h in heads]
            dsink = jnp.zeros((8, 128), F32)
            pbs, dsbs = [], []
            for h in heads:
                p, sinkp = _softmax_cols(ss[h] + bias_ref[first + h], sink_ref[h][0:1, :])
                delta = jnp.sum(p * dps[h], axis=0, keepdims=True)
                ds = p * (dps[h] - delta)
                dbias_ref[h] += ds
                sd = sinkp * delta
                for j in range(GROUP):
                    val = -jnp.sum(sd[:, j * BLOCK:(j + 1) * BLOCK], axis=1, keepdims=True)
                    dsink = dsink + jnp.where(lane8 == 4 * j + h, val, 0.0)
                pbs.append(p.astype(BF16))
                dsbs.append(ds.astype(BF16))
            y = _add4([lax.dot_general(pbs[h], vhs[h], TN, preferred_element_type=F32) for h in heads])
            dq = _add4([lax.dot_general(dsbs[h], khs[h], TN, preferred_element_type=F32) for h in heads])
            dk = _add4([jnp.where(colh == h, jnp.dot(dsbs[h], qs, preferred_element_type=F32) * SCALE, 0.0)
                        for h in heads])
            dv = _add4([jnp.where(colh == h, jnp.dot(pbs[h], dos, preferred_element_type=F32), 0.0) for h in heads])
            for j in range(GROUP):
                cat_ref[pl.ds(r0, BLOCK), j * kw:(j + 1) * kw] = y[j * BLOCK:(j + 1) * BLOCK].astype(BF16)
                dq_ref[pl.ds(r0, BLOCK), j * kw:(j + 1) * kw] = dq[j * BLOCK:(j + 1) * BLOCK].astype(BF16)
            dsink_ref[...] += dsink
            dkvx[pl.ds(r0, 2 * BLOCK), :] += jnp.concatenate([dk, dv], axis=1)
            return carry

        lax.fori_loop(0, nb, blk, 0)
        dkvh_ref[0] = dkvx[0:BLOCK, :]
        dkv_ref[...] = dkvx[BLOCK:BLOCK + tm, :]

        mkv = mkv_ref[...]
        ymem, dqm, dmk, dmv = _mem_attn_bwd(q_ref[:, qw:d], dcat_s[:, qw:d], mkv[:, 0:mw], mkv[:, mw:2 * mw])
        cat_ref[:, qw:d] = ymem.astype(BF16)
        dq_ref[:, qw:d] = dqm.astype(BF16)
        dmk_acc[...] += dmk
        dmv_acc[...] += dmv

        @pl.when(i == nt - 1)
        def _():
            dmkv_ref[...] = jnp.concatenate([dmk_acc[...], dmv_acc[...]], axis=1)

    return pl.pallas_call(
        body, name=name, grid=(nt,),
        in_specs=[_row_spec(tm, d), _row_spec(tm, d), _row_spec(tm, 2 * kw), _halo_prev_spec(BLOCK, 2 * kw, tm),
                  _const_spec((2 * N_KV_HEADS, 2 * BLOCK, rows)), _const_spec((N_KV_HEADS, 8, rows)),
                  pl.BlockSpec((n_mem, 2 * mw), lambda i: (0, layer)), _const_spec((d, d))],
        out_specs=[_row_spec(tm, d), _row_spec(tm, d), _row_spec(tm, 2 * kw),
                   pl.BlockSpec((1, BLOCK, 2 * kw), lambda i: (i, 0, 0)),
                   pl.BlockSpec((N_KV_HEADS, 2 * BLOCK, rows), lambda i: (0, 0, 0)),
                   pl.BlockSpec((8, 128), lambda i: (0, 0)),
                   pl.BlockSpec((n_mem, 2 * mw), lambda i: (0, 0))],
        out_shape=[jax.ShapeDtypeStruct((t, d), BF16), jax.ShapeDtypeStruct((t, d), BF16),
                   jax.ShapeDtypeStruct((t, 2 * kw), F32), jax.ShapeDtypeStruct((nt, BLOCK, 2 * kw), F32),
                   jax.ShapeDtypeStruct((N_KV_HEADS, 2 * BLOCK, rows), F32), jax.ShapeDtypeStruct((8, 128), F32),
                   jax.ShapeDtypeStruct((n_mem, 2 * mw), F32)],
        scratch_shapes=[pltpu.VMEM((tm + BLOCK, 2 * kw), BF16), pltpu.VMEM((tm + BLOCK, 2 * kw), F32),
                        pltpu.VMEM((tm, d), BF16), pltpu.VMEM((n_mem, mw), F32), pltpu.VMEM((n_mem, mw), F32)],
        compiler_params=_cp(("arbitrary",), VMEM_BIG),
    )(dxm, qp, kv, kv, bias, sinkt, memkv, wout)


def _kv_assemble(main_a, halo_a, main_b, halo_b, tm, name):
    t, n = main_a.shape
    nt = t // tm

    def body(ma_ref, ha_ref, mb_ref, hb_ref, o_ref):
        i = pl.program_id(0)
        s = ma_ref[...] + mb_ref[...]
        tail = jnp.where(i == nt - 1, 0.0, ha_ref[0] + hb_ref[0])
        o_ref[...] = jnp.concatenate([s[0:tm - BLOCK], s[tm - BLOCK:] + tail], axis=0).astype(BF16)

    halo_spec = pl.BlockSpec((1, BLOCK, n), lambda i: (jnp.minimum(i + 1, nt - 1), 0, 0))
    return pl.pallas_call(
        body, name=name, grid=(nt,),
        in_specs=[_row_spec(tm, n), halo_spec, _row_spec(tm, n), halo_spec],
        out_specs=_row_spec(tm, n),
        out_shape=jax.ShapeDtypeStruct((t, n), BF16),
        compiler_params=_cp(("parallel",)),
    )(main_a, halo_a, main_b, halo_b)


def _adam_math(w, g, m, v):
    m2 = ADAM_B1 * m + (1.0 - ADAM_B1) * g
    v2 = ADAM_B2 * v + (1.0 - ADAM_B2) * (g * g)
    m_hat = m2 / (1.0 - ADAM_B1 ** ADAM_STEP)
    v_hat = v2 / (1.0 - ADAM_B2 ** ADAM_STEP)
    delta = -ADAM_LR * (m_hat / (jnp.sqrt(v_hat) + ADAM_EPS) + ADAM_WD * w)
    return delta, m2, v2


def _adamw_sharded(w, mine, theirs, m, v, name):
    shape = w.shape
    c = shape[-1]
    r = int(np.prod(shape[:-1]))
    tr = r
    for cand in (512, 256, 128, 64, 32, 16):
        if r % cand == 0:
            tr = cand
            break

    def body(w_ref, a_ref, b_ref, m_ref, v_ref, g_ref, d_ref, mo_ref, vo_ref):
        g = a_ref[0].astype(F32) + b_ref[0].astype(F32)
        for k in range(1, N_CHIPS):
            g = g + (a_ref[k].astype(F32) + b_ref[k].astype(F32))
        delta, m2, v2 = _adam_math(w_ref[...], g, m_ref[...], v_ref[...])
        g_ref[...] = g
        d_ref[...] = delta
        mo_ref[...] = m2
        vo_ref[...] = v2

    rs = pl.BlockSpec((tr, c), lambda i: (i, 0))
    ps = pl.BlockSpec((N_CHIPS, tr, c), lambda i: (0, i, 0))
    sd = jax.ShapeDtypeStruct((r, c), F32)
    outs = pl.pallas_call(
        body, name=name, grid=(r // tr,),
        in_specs=[rs, ps, ps, rs, rs], out_specs=[rs, rs, rs, rs], out_shape=[sd, sd, sd, sd],
        compiler_params=_cp(("parallel",)),
    )(w.reshape(r, c), mine.reshape(N_CHIPS, r, c), theirs.reshape(N_CHIPS, r, c), m.reshape(r, c), v.reshape(r, c))
    return [o.reshape(shape) for o in outs]


def _adamw_packed(w, g, m, v, name):
    def body(w_ref, g_ref, m_ref, v_ref, d_ref, mo_ref, vo_ref):
        delta, m2, v2 = _adam_math(w_ref[...], g_ref[...], m_ref[...], v_ref[...])
        d_ref[...] = delta
        mo_ref[...] = m2
        vo_ref[...] = v2

    vm = pl.BlockSpec(memory_space=pltpu.VMEM)
    sd = jax.ShapeDtypeStruct(w.shape, F32)
    return pl.pallas_call(body, name=name, in_specs=[vm] * 4, out_specs=[vm] * 3, out_shape=[sd] * 3)(w, g, m, v)


def _place():
    return lax.axis_index("x"), lax.axis_index("y"), lax.axis_index("c")


def _hbm(a):
    return pltpu.with_memory_space_constraint(a, pltpu.HBM)


def _other_chips(x, y):
    return [(1 - x, y), (x, 1 - y), (1 - x, 1 - y)]


def _chip_copy(src, land, gather, layer, chip_src, chip_dst, send_sem, recv_sem, peer):
    s = src if gather else src.at[chip_src]
    d = land.at[chip_dst] if layer is None else land.at[chip_dst, layer]
    return pltpu.make_async_remote_copy(src_ref=s, dst_ref=d, send_sem=send_sem, recv_sem=recv_sem,
                                        device_id=peer, device_id_type=MESH)


def _exchange_start(srcs, lands, gather, layers, after, name):
    n = len(srcs)
    hbm = pl.BlockSpec(memory_space=pltpu.HBM)
    sem = pl.BlockSpec(memory_space=pltpu.SEMAPHORE)

    def body(*refs):
        ins, lds = refs[:n], refs[n:2 * n]
        first_out = 2 * n + len(after)
        send_sems, recv_sems, token = refs[first_out], refs[first_out + 1], refs[-1]
        x, y, c = _place()
        me = 2 * x + y
        for t in range(n):
            for r, (px, py) in enumerate(_other_chips(x, y)):
                _chip_copy(ins[t], lds[t], gather, layers[t], 2 * px + py, me, send_sems.at[3 * t + r],
                           recv_sems.at[3 * t + r], (px, py, c)).start()
        token[...] = jnp.zeros_like(token)

    both = list(srcs) + list(lands)
    outs = pl.pallas_call(
        body, name=name, in_specs=[hbm] * (2 * n) + [pl.BlockSpec(memory_space=pl.ANY)] * len(after),
        out_specs=(sem, sem, *([hbm] * (2 * n)), pl.BlockSpec(memory_space=pltpu.VMEM)),
        out_shape=(pltpu.SemaphoreType.DMA((3 * n,)), pltpu.SemaphoreType.DMA((3 * n,)),
                   *[pltpu.HBM(a.shape, a.dtype) for a in both], jax.ShapeDtypeStruct((8, 128), F32)),
        input_output_aliases={t: 2 + t for t in range(2 * n)},
        compiler_params=_cp(has_side_effects=pltpu.SideEffectType.DATAFLOW_SIDE_EFFECTING),
    )(*[_hbm(a) for a in both], *after)
    return dict(send=outs[0], recv=outs[1], srcs=list(outs[2:2 + n]), lands=list(outs[2 + n:2 + 2 * n]),
                token=outs[-1], gather=gather, layers=list(layers))


def _exchange_wait(groups, lands, land_ids, after, name):
    flat = [s for g in groups for s in g["srcs"]]
    ns, nl, ng, na = len(flat), len(lands), len(groups), len(after)
    hbm = pl.BlockSpec(memory_space=pltpu.HBM)
    sem = pl.BlockSpec(memory_space=pltpu.SEMAPHORE)

    def body(*refs):
        srcs, lds = refs[:ns], refs[ns:ns + nl]
        sems = refs[ns + nl:ns + nl + 2 * ng]
        x, y, c = _place()
        k = 0
        for gi, g in enumerate(groups):
            for t in range(len(g["srcs"])):
                for r, (px, py) in enumerate(_other_chips(x, y)):
                    cp = _chip_copy(srcs[k], lds[land_ids[gi][t]], g["gather"], g["layers"][t], 0, 0,
                                    sems[2 * gi].at[3 * t + r], sems[2 * gi + 1].at[3 * t + r], (px, py, c))
                    cp.wait_send()
                    cp.wait_recv()
                k += 1

    both = flat + list(lands)
    sem_args = [a for g in groups for a in (g["send"], g["recv"])]
    outs = pl.pallas_call(
        body, name=name,
        in_specs=[hbm] * (ns + nl) + [sem] * (2 * ng) + [pl.BlockSpec(memory_space=pl.ANY)] * na,
        out_specs=[hbm] * (ns + nl),
        out_shape=[pltpu.HBM(a.shape, a.dtype) for a in both],
        input_output_aliases={t: t for t in range(ns + nl)},
        compiler_params=_cp(has_side_effects=pltpu.SideEffectType.DATAFLOW_SIDE_EFFECTING),
    )(*both, *sem_args, *after)
    return list(outs[ns:])


def _core_exchange(arrays, name):
    n = len(arrays)
    hbm = pl.BlockSpec(memory_space=pltpu.HBM)

    def body(*refs):
        ins, outs = refs[:n], refs[n:2 * n]
        send_sems, recv_sems = refs[2 * n:]
        x, y, c = _place()
        copies = []
        for t in range(n):
            cp = pltpu.make_async_remote_copy(
                src_ref=ins[t], dst_ref=outs[t], send_sem=send_sems.at[t], recv_sem=recv_sems.at[t],
                device_id=(x, y, 1 - c), device_id_type=MESH)
            cp.start()
            copies.append(cp)
        for cp in copies:
            cp.wait()

    return pl.pallas_call(
        body, name=name, in_specs=[hbm] * n, out_specs=[hbm] * n,
        out_shape=[jax.ShapeDtypeStruct(a.shape, a.dtype) for a in arrays],
        scratch_shapes=[pltpu.SemaphoreType.DMA((n,)), pltpu.SemaphoreType.DMA((n,))],
        compiler_params=_cp(has_side_effects=True),
    )(*arrays)


def _all_reduce_packed(pack, name):
    r, c = pack.shape
    vm = pl.BlockSpec(memory_space=pltpu.VMEM)

    def body(p_ref, sum_ref, slots, send_sems, recv_sems):
        x, y, cc = _place()
        me = 4 * x + 2 * y + cc
        slots[me] = p_ref[...]
        copies = []
        for rel in range(1, 8):
            px = 1 - x if rel & 4 else x
            py = 1 - y if rel & 2 else y
            pc = 1 - cc if rel & 1 else cc
            cp = pltpu.make_async_remote_copy(
                src_ref=p_ref, dst_ref=slots.at[me], send_sem=send_sems.at[rel - 1], recv_sem=recv_sems.at[rel - 1],
                device_id=(px, py, pc), device_id_type=MESH)
            cp.start()
            copies.append(cp)
        for cp in copies:
            cp.wait()
        total = slots[0]
        for k in range(1, 8):
            total = total + slots[k]
        sum_ref[...] = total

    return pl.pallas_call(
        body, name=name, in_specs=[vm], out_specs=vm, out_shape=jax.ShapeDtypeStruct((r, c), F32),
        scratch_shapes=[pltpu.VMEM((8, r, c), F32), pltpu.SemaphoreType.DMA((7,)), pltpu.SemaphoreType.DMA((7,))],
        compiler_params=_cp(has_side_effects=True),
    )(pack)


def _pack(items):
    rows = []
    for a in items:
        flat = a.astype(F32).reshape(-1)
        pad = (-flat.shape[0]) % PACK_W
        rows.append(jnp.pad(flat, (0, pad)).reshape(-1, PACK_W))
    out = jnp.concatenate(rows, axis=0)
    pad_r = (-out.shape[0]) % 8
    return jnp.pad(out, ((0, pad_r), (0, 0)))


def _unpack(pack, shapes):
    outs, row = [], 0
    for s in shapes:
        n = int(np.prod(s))
        nr = -(-n // PACK_W)
        outs.append(pack[row:row + nr].reshape(-1)[:n].reshape(s))
        row += nr
    return outs


def _heads_to_member_major(w, axis):
    shp = w.shape
    pre, post = shp[:axis], shp[axis + 1:]
    w4 = w.reshape(pre + (N_KV_HEADS, GROUP, HEAD_DIM) + post)
    w4 = jnp.swapaxes(w4, len(pre), len(pre) + 1)
    return w4.reshape(shp)


def _heads_to_kv_major(w, axis):
    shp = w.shape
    pre, post = shp[:axis], shp[axis + 1:]
    w4 = w.reshape(pre + (GROUP, N_KV_HEADS, HEAD_DIM) + post)
    w4 = jnp.swapaxes(w4, len(pre), len(pre) + 1)
    return w4.reshape(shp)


def kernel(x, mem, norm_mix, norm_ffn, a_w_in, a_conv_w, a_w_out, kv_norm, w_kv, b_w_q, b_sinks, b_w_out, rel_bias, mem_norm, w_mem_kv, w_gate, w_up, w_down, final_norm, loss_target, m_norm_mix, m_norm_ffn, m_a_w_in, m_a_conv_w, m_a_w_out, m_kv_norm, m_w_kv, m_b_w_q, m_b_sinks, m_b_w_out, m_rel_bias, m_mem_norm, m_w_mem_kv, m_w_gate, m_w_up, m_w_down, m_final_norm, v_norm_mix, v_norm_ffn, v_a_w_in, v_a_conv_w, v_a_w_out, v_kv_norm, v_w_kv, v_b_w_q, v_b_sinks, v_b_w_out, v_rel_bias, v_mem_norm, v_w_mem_kv, v_w_gate, v_w_up, v_w_down, v_final_norm):
    t, d = x.shape[1], x.shape[2]
    tm = 512 if t % 512 == 0 and t >= 2048 else 256
    x0 = x.reshape(t, d)
    target = loss_target.reshape(t, d)
    mem2 = mem.reshape(mem.shape[1], d)
    n_mem = mem2.shape[0]
    ax, ay, ac = _place()
    chip = 2 * ax + ay
    cwid = a_conv_w.shape[2] * N_CHIPS
    qw = N_Q_HEADS * HEAD_DIM
    nq = N_CHIPS

    def own_slot(piece):
        return lax.dynamic_update_slice(lax.empty((nq,) + piece.shape, piece.dtype), piece[None],
                                        (chip,) + (0,) * piece.ndim)

    def mixer_shards(i):
        if i < N_A:
            return [a_w_in[i], a_w_out[i]] + ([w_mem_kv] if i == 0 else [])
        j = i - N_A
        return [b_w_q[j], b_w_out[j]] + ([w_kv] if j == 0 else [])

    def ffn_shards(i):
        return [w_gate[i], w_up[i], w_down[i]]

    group_shards = {"0m": mixer_shards(0), "0f": ffn_shards(0)}
    for i in range(1, DEPTH):
        group_shards[str(i)] = ffn_shards(i) + mixer_shards(i)
    gathers, prev_tok = {}, []
    for key, shards in group_shards.items():
        shards = [a.astype(BF16) for a in shards]
        gathers[key] = _exchange_start(shards, [own_slot(a) for a in shards], True, [None] * len(shards),
                                       prev_tok, "gather_start_" + key)
        prev_tok = [gathers[key]["token"]]

    def rows_full(g):
        return g.reshape((-1,) + g.shape[2:])

    def cols_full(g):
        return jnp.transpose(g, (1, 0, 2)).reshape(g.shape[1], -1)

    def landed_weights(key, after):
        g = gathers[key]
        return _exchange_wait([g], g["lands"], [list(range(len(g["lands"])))], after, "gather_wait_" + key)

    def mixer_weights(i, got):
        w_first, w_out = (cols_full(got[0]) if i < N_A else rows_full(got[0])), rows_full(got[1])
        if i >= N_A:
            w_first = jnp.concatenate([_heads_to_member_major(w_first[:, :qw], 1), w_first[:, qw:]], axis=1)
            w_out = jnp.concatenate([_heads_to_member_major(w_out[:qw, :], 0), w_out[qw:, :]], axis=0)
        return dict(w_first=w_first, w_out=w_out, extra=got[2] if len(got) > 2 else None)

    def ffn_weights(got):
        return dict(wg=cols_full(got[0]), wu=cols_full(got[1]), wd=rows_full(got[2]))

    conv_place = jnp.zeros((N_CHIPS,) + a_conv_w.shape, F32).at[chip].set(a_conv_w) * (ac == 0).astype(F32)
    conv_all = _all_reduce_packed(_pack([conv_place]), "gather_conv")
    conv_full = _unpack(conv_all, [conv_place.shape])[0]
    conv_full = jnp.transpose(conv_full, (1, 2, 0, 3)).reshape(N_A, 3, cwid)

    bias, sinkt = [], []
    for j in range(2):
        bj, sj = _bias_tables(rel_bias, b_sinks[j], "bias_tables")
        bias.append(bj)
        sinkt.append(sj)

    ws = []
    xs, xmids, projs, gates, ups = [x0], [], [], [], []
    kv = memkv = wmem = wkv = None
    for i in range(DEPTH):
        xin = xs[-1]
        if i == 0:
            w = mixer_weights(0, landed_weights("0m", prev_tok))
        else:
            got = landed_weights(str(i), [xin])
            w = dict(mixer_weights(i, got[3:]), **ffn_weights(got[0:3]))
        ws.append(w)
        gm = norm_mix[i].reshape(1, d)
        if i == 0:
            full_mem = jnp.swapaxes(w["extra"], 0, 1).reshape(DEPTH, d, -1)
            wmem = jnp.transpose(full_mem, (1, 0, 2)).reshape(d, -1)
            memkv = _norm_mm(mem2, mem_norm.reshape(1, d), wmem, n_mem, "mem_kv")
        if i < N_A:
            proj = _norm_mm(xin, gm, w["w_first"], tm, "proj_a")
            xmid = _mix_a_fwd(xin, proj, conv_full[i], memkv, i, w["w_out"], tm, "mix_a_fwd")
        else:
            j = i - N_A
            if j == 0:
                wkv = rows_full(w["extra"])
                kv = _norm_mm(xin, kv_norm.reshape(1, d), wkv, tm, "proj_kv")
            proj = _norm_mm(xin, gm, w["w_first"], tm, "proj_b")
            xmid = _mix_b_fwd(xin, proj, kv, bias[j], sinkt[j], memkv, i, w["w_out"], tm, "mix_b_fwd")
        if i == 0:
            w.update(ffn_weights(landed_weights("0f", [xmid])))
        xout, gate, up = _ffn_fwd(xmid, norm_ffn[i].reshape(1, d), w["wg"], w["wu"], w["wd"], tm, "ffn_fwd")
        projs.append(proj)
        xmids.append(xmid)
        gates.append(gate)
        ups.append(up)
        xs.append(xout)

    loss_part, dx, dg_final = _final_loss(xs[-1], final_norm.reshape(1, d), target, tm, "final_loss")

    def rows_pieces(g):
        return g.astype(BF16).reshape((nq, g.shape[0] // nq) + g.shape[1:])

    def cols_pieces(g):
        return jnp.transpose(g.astype(BF16).reshape(g.shape[0], nq, g.shape[1] // nq), (1, 0, 2))

    stacked = dict(a_w_in=a_w_in, a_w_out=a_w_out, w_kv=w_kv[None], b_w_q=b_w_q, b_w_out=b_w_out,
                   w_mem_kv=w_mem_kv, w_gate=w_gate, w_up=w_up, w_down=w_down)
    names = list(stacked)
    land = {k: lax.empty((nq,) + stacked[k].shape, BF16) for k in names}
    own = {k: [None] * stacked[k].shape[0] for k in names}
    scatters, scatter_ids = [], []

    def scatter_start(key, items):
        keys = [k for k, _, _ in items]
        st = _exchange_start([p for _, _, p in items], [land[k] for k in keys], False, [l for _, l, _ in items],
                             [], "scatter_start_" + key)
        for (k, l, p), ld in zip(items, st["lands"]):
            land[k] = ld
            mine_piece = lax.dynamic_index_in_dim(p, chip, 0, keepdims=False)
            if l is None:
                own[k] = [mine_piece[q] for q in range(mine_piece.shape[0])]
            else:
                own[k][l] = mine_piece
        scatters.append(st)
        scatter_ids.append([names.index(k) for k in keys])
        return st["token"][0:1, 0:1]

    g_norm_mix, g_norm_ffn = [None] * DEPTH, [None] * DEPTH
    g_conv, g_sinks = [None] * 2, [None] * 2
    dmemkv = [None] * DEPTH
    dbias, dkv_main, dkv_halo = [None] * 2, [None] * 2, [None] * 2
    g_kv_norm = None
    tok = jnp.zeros((1, 1), F32)
    for i in reversed(range(DEPTH)):
        w = ws[i]
        dxm, dgate, dup, act, h2, dgf = _ffn_bwd(dx, xmids[i], norm_ffn[i].reshape(1, d) + tok, gates[i], ups[i],
                                            w["wg"], w["wu"], w["wd"], tm // 2, "ffn_bwd")
        g_norm_ffn[i] = dgf
        g_wd = _wgrad(act, dx, tm, "wgrad_down")
        g_wg = _wgrad(h2, dgate, tm, "wgrad_gate")
        g_wu = _wgrad(h2, dup, tm, "wgrad_up")
        items = [("w_gate", i, cols_pieces(g_wg)), ("w_up", i, cols_pieces(g_wu)), ("w_down", i, rows_pieces(g_wd))]
        if i == 0:
            tok = scatter_start("0f", items)
            items = []
        gm = norm_mix[i].reshape(1, d)
        if i < N_A:
            dproj, cat, dcw, dmemkv[i] = _mix_a_bwd(dxm, projs[i], conv_full[i] + (tok if i == 0 else 0.0), memkv, i,
                                                    w["w_out"], tm, "mix_a_bwd")
            g_conv[i] = dcw[0:3]
            g_out = _wgrad(cat, dxm, tm, "wgrad_out")
            dx, g_norm_mix[i], h = _mm_nt_normbwd(dproj, w["w_first"], xs[i], gm, dxm, tm, "proj_a_bwd")
            g_in = _wgrad(h, dproj, tm, "wgrad_in_a")
            items += [("a_w_in", i, cols_pieces(g_in)), ("a_w_out", i, rows_pieces(g_out))]
        else:
            j = i - N_A
            dqp, cat, dkv_main[j], dkv_halo[j], dbias[j], dsk, dmemkv[i] = _mix_b_bwd(
                dxm, projs[i], kv, bias[j], sinkt[j], memkv, i, w["w_out"], tm, "mix_b_bwd")
            g_sinks[j] = dsk[0, 0:N_Q_HEADS].reshape(GROUP, N_KV_HEADS).T.reshape(N_Q_HEADS)
            g_out = _wgrad(cat, dxm, tm, "wgrad_out")
            dx, g_norm_mix[i], h = _mm_nt_normbwd(dqp, w["w_first"], xs[i], gm, dxm, tm, "proj_b_bwd")
            g_q = _wgrad(h, dqp, tm, "wgrad_in_b")
            g_q = jnp.concatenate([_heads_to_kv_major(g_q[:, :qw], 1), g_q[:, qw:]], axis=1)
            g_out = jnp.concatenate([_heads_to_kv_major(g_out[:qw, :], 0), g_out[qw:, :]], axis=0)
            items += [("b_w_q", j, rows_pieces(g_q)), ("b_w_out", j, rows_pieces(g_out))]
            if j == 0:
                dkv = _kv_assemble(dkv_main[0], dkv_halo[0], dkv_main[1], dkv_halo[1], tm, "kv_assemble")
                dx, g_kv_norm, hkv = _mm_nt_normbwd(dkv, wkv, xs[i], kv_norm.reshape(1, d), dx, tm, "proj_kv_bwd")
                items.append(("w_kv", 0, rows_pieces(_wgrad(hkv, dkv, tm, "wgrad_kv"))))
        if i == 0:
            dmemkv_all = jnp.concatenate([a.astype(BF16) for a in dmemkv], axis=1)
            _, g_mem_norm, hmem = _mm_nt_normbwd(dmemkv_all, wmem, mem2, mem_norm.reshape(1, d),
                                                 jnp.zeros((n_mem, d), F32), n_mem, "mem_kv_bwd")
            g_wmem = _wgrad(hmem, dmemkv_all, n_mem, "wgrad_mem")
            g_wmem = jnp.transpose(g_wmem.astype(BF16).reshape(nq, d // nq, DEPTH, -1), (0, 2, 1, 3))
            items.append(("w_mem_kv", None, g_wmem))
        tok = scatter_start(str(i) if i else "0m", items)
    grad_x = dx.reshape(x.shape)
    g_rel = _bias_bwd(dbias[0], dbias[1], "bias_bwd")[:, 0:N_Q_HEADS]

    small_shapes = [(DEPTH, d), (DEPTH, d), (d,), (d,), (d,), (2, N_Q_HEADS), (REL_BUCKETS, N_Q_HEADS),
                    (N_A, 3, cwid), ()]
    small = _pack([jnp.concatenate(g_norm_mix, axis=0), jnp.concatenate(g_norm_ffn, axis=0), g_kv_norm, g_mem_norm,
                   dg_final, jnp.stack(g_sinks), g_rel, jnp.stack(g_conv), loss_part[0, 0]])
    small_sum = _all_reduce_packed(small, "reduce_small")
    (gs_norm_mix, gs_norm_ffn, gs_kv_norm, gs_mem_norm, gs_final, gs_sinks, gs_rel, gs_conv_full, loss) = _unpack(
        small_sum, small_shapes)
    cq = cwid // N_CHIPS
    gs_conv = lax.dynamic_slice_in_dim(gs_conv_full, chip * cq, cq, axis=2)

    landed = _exchange_wait(scatters, [land[k] for k in names], scatter_ids, [small_sum], "scatter_wait")
    mine = [lax.dynamic_update_slice(ld, jnp.stack(own[k])[None], (chip,) + (0,) * (ld.ndim - 1))
            for k, ld in zip(names, landed)]
    theirs = _core_exchange(mine, "swap_cores")

    weights = dict(norm_mix=norm_mix, norm_ffn=norm_ffn, a_w_in=a_w_in, a_conv_w=a_conv_w, a_w_out=a_w_out,
                   kv_norm=kv_norm, w_kv=w_kv, b_w_q=b_w_q, b_sinks=b_sinks, b_w_out=b_w_out, rel_bias=rel_bias,
                   mem_norm=mem_norm, w_mem_kv=w_mem_kv, w_gate=w_gate, w_up=w_up, w_down=w_down,
                   final_norm=final_norm)
    moms = dict(norm_mix=m_norm_mix, norm_ffn=m_norm_ffn, a_w_in=m_a_w_in, a_conv_w=m_a_conv_w, a_w_out=m_a_w_out,
                kv_norm=m_kv_norm, w_kv=m_w_kv, b_w_q=m_b_w_q, b_sinks=m_b_sinks, b_w_out=m_b_w_out,
                rel_bias=m_rel_bias, mem_norm=m_mem_norm, w_mem_kv=m_w_mem_kv, w_gate=m_w_gate, w_up=m_w_up,
                w_down=m_w_down, final_norm=m_final_norm)
    vars_ = dict(norm_mix=v_norm_mix, norm_ffn=v_norm_ffn, a_w_in=v_a_w_in, a_conv_w=v_a_conv_w, a_w_out=v_a_w_out,
                 kv_norm=v_kv_norm, w_kv=v_w_kv, b_w_q=v_b_w_q, b_sinks=v_b_sinks, b_w_out=v_b_w_out,
                 rel_bias=v_rel_bias, mem_norm=v_mem_norm, w_mem_kv=v_w_mem_kv, w_gate=v_w_gate, w_up=v_w_up,
                 w_down=v_w_down, final_norm=v_final_norm)
    order = list(weights)
    grads, deltas, new_m, new_v = {}, {}, {}, {}
    for k, mi, th in zip(names, mine, theirs):
        shp = weights[k].shape
        g, dl, m2, v2 = _adamw_sharded(weights[k].reshape(mi.shape[1:]), mi, th, moms[k].reshape(mi.shape[1:]),
                                       vars_[k].reshape(mi.shape[1:]), "adamw_" + k)
        grads[k], deltas[k], new_m[k], new_v[k] = g.reshape(shp), dl.reshape(shp), m2.reshape(shp), v2.reshape(shp)
    small_names = ["norm_mix", "norm_ffn", "kv_norm", "mem_norm", "final_norm", "b_sinks", "rel_bias", "a_conv_w"]
    small_g = [gs_norm_mix, gs_norm_ffn, gs_kv_norm, gs_mem_norm, gs_final, gs_sinks, gs_rel, gs_conv]
    shapes = [weights[k].shape for k in small_names]
    dl_p, m_p, v_p = _adamw_packed(_pack([weights[k] for k in small_names]), _pack(small_g),
                                   _pack([moms[k] for k in small_names]), _pack([vars_[k] for k in small_names]),
                                   "adamw_small")
    for k, g, dl, m2, v2 in zip(small_names, small_g, _unpack(dl_p, shapes), _unpack(m_p, shapes), _unpack(v_p, shapes)):
        grads[k], deltas[k], new_m[k], new_v[k] = g.reshape(weights[k].shape), dl, m2, v2

    return (loss, grad_x, *[grads[k] for k in order], *[deltas[k] for k in order],
            *[new_m[k] for k in order], *[new_v[k] for k in order])
```

```python
import functools
import math

import numpy as np
import jax
import jax.numpy as jnp
from jax import lax
from jax.experimental import pallas as pl
from jax.experimental.pallas import tpu as pltpu

F32 = jnp.float32
BF16 = jnp.bfloat16
MESH = pl.DeviceIdType.MESH

EPS = 1e-5
HEAD_DIM = 64
N_MEM_HEADS = 4
N_KV_HEADS = 4
GROUP = 3
N_Q_HEADS = N_KV_HEADS * GROUP
BLOCK = 128
REL_BUCKETS = 32
REL_MAX_DIST = 128
SCALE = HEAD_DIM ** -0.5
NEG = -1e30
N_CHIPS = 4
N_A = 2
DEPTH = 4

ADAM_LR = 0.001
ADAM_B1 = 0.9
ADAM_B2 = 0.999
ADAM_EPS = 1e-08
ADAM_WD = 0.01
ADAM_STEP = 10

VMEM_BIG = 56 * 1024 * 1024
PACK_W = 1024

NT = (((1,), (1,)), ((), ()))
TN = (((0,), (0,)), ((), ()))


def _cp(sem=None, vmem=None, **kw):
    return pltpu.CompilerParams(dimension_semantics=sem, vmem_limit_bytes=vmem, **kw)


def _const_spec(shape):
    nd = len(shape)
    return pl.BlockSpec(shape, lambda i, _n=nd: (0,) * _n, pipeline_mode=pl.Buffered(1))


def _row_spec(tm, n):
    return pl.BlockSpec((tm, n), lambda i: (i, 0))


def _rms_parts(xv):
    r = lax.rsqrt(jnp.mean(xv * xv, axis=-1, keepdims=True) + EPS)
    return xv * r, r


def _sigmoid(z):
    return 1.0 / (1.0 + jnp.exp(-z))


def _ff_chunks(f):
    if f % 512 == 0 or f % 256 != 0:
        return [(0, f)] if f <= 1536 else [(0, f // 2), (f // 2, f - f // 2)]
    n = f // 256
    a = (n + 1) // 2 * 256
    return [(0, a), (a, f - a)]


def _norm_mm(x, g, w, tm, name):
    t, d = x.shape
    n = w.shape[1]

    def body(x_ref, g_ref, w_ref, o_ref):
        xhat, _ = _rms_parts(x_ref[...])
        h = (xhat * g_ref[...]).astype(BF16)
        o_ref[...] = jnp.dot(h, w_ref[...], preferred_element_type=F32).astype(BF16)

    return pl.pallas_call(
        body, name=name, grid=(t // tm,),
        in_specs=[_row_spec(tm, d), _const_spec((1, d)), _const_spec((d, n))],
        out_specs=_row_spec(tm, n),
        out_shape=jax.ShapeDtypeStruct((t, n), BF16),
        compiler_params=_cp(("parallel",), VMEM_BIG),
    )(x, g, w)


def _mm_nt_normbwd(dproj, w, x_in, g, dres, tm, name):
    t, d = x_in.shape
    n = w.shape[1]

    def body(dp_ref, w_ref, x_ref, g_ref, dr_ref, dx_ref, dg_ref, h_ref):
        i = pl.program_id(0)
        xhat, r = _rms_parts(x_ref[...])
        gv = g_ref[...]
        h_ref[...] = (xhat * gv).astype(BF16)
        dh = lax.dot_general(dp_ref[...], w_ref[...], NT, preferred_element_type=F32)
        dxhat = dh * gv
        dx = r * (dxhat - xhat * jnp.mean(dxhat * xhat, axis=-1, keepdims=True))
        dx_ref[...] = dr_ref[...] + dx

        @pl.when(i == 0)
        def _():
            dg_ref[...] = jnp.zeros_like(dg_ref)

        dg_ref[...] += jnp.sum(dh * xhat, axis=0, keepdims=True)

    return pl.pallas_call(
        body, name=name, grid=(t // tm,),
        in_specs=[_row_spec(tm, n), _const_spec((d, n)), _row_spec(tm, d), _const_spec((1, d)), _row_spec(tm, d)],
        out_specs=[_row_spec(tm, d), pl.BlockSpec((1, d), lambda i: (0, 0)), _row_spec(tm, d)],
        out_shape=[jax.ShapeDtypeStruct((t, d), F32), jax.ShapeDtypeStruct((1, d), F32),
                   jax.ShapeDtypeStruct((t, d), BF16)],
        compiler_params=_cp(("arbitrary",), VMEM_BIG),
    )(dproj, w, x_in, g, dres)


def _ffn_fwd(x, g, wg, wu, wd, tm, name):
    t, d = x.shape
    f = wg.shape[1]
    chunks = _ff_chunks(f)

    def body(x_ref, g_ref, wg_ref, wu_ref, wd_ref, xo_ref, gate_ref, up_ref):
        xv = x_ref[...]
        xhat, _ = _rms_parts(xv)
        h = (xhat * g_ref[...]).astype(BF16)
        acc = xv
        for c0, cw in chunks:
            gt = jnp.dot(h, wg_ref[:, c0:c0 + cw], preferred_element_type=F32)
            ut = jnp.dot(h, wu_ref[:, c0:c0 + cw], preferred_element_type=F32)
            gate_ref[:, c0:c0 + cw] = gt.astype(BF16)
            up_ref[:, c0:c0 + cw] = ut.astype(BF16)
            a = (gt * _sigmoid(gt) * ut).astype(BF16)
            acc = acc + jnp.dot(a, wd_ref[c0:c0 + cw, :], preferred_element_type=F32)
        xo_ref[...] = acc

    return pl.pallas_call(
        body, name=name, grid=(t // tm,),
        in_specs=[_row_spec(tm, d), _const_spec((1, d)), _const_spec((d, f)), _const_spec((d, f)), _const_spec((f, d))],
        out_specs=[_row_spec(tm, d), _row_spec(tm, f), _row_spec(tm, f)],
        out_shape=[jax.ShapeDtypeStruct((t, d), F32), jax.ShapeDtypeStruct((t, f), BF16),
                   jax.ShapeDtypeStruct((t, f), BF16)],
        compiler_params=_cp(("parallel",), VMEM_BIG),
    )(x, g, wg, wu, wd)


def _ffn_bwd(dxo, xm, g, gate, up, wg, wu, wd, tm, name):
    t, d = xm.shape
    f = wg.shape[1]
    chunks = _ff_chunks(f)

    def body(dxo_ref, xm_ref, g_ref, gate_ref, up_ref, wg_ref, wu_ref, wd_ref,
             dxm_ref, dgate_ref, dup_ref, act_ref, h2_ref, dg_ref):
        i = pl.program_id(0)
        dxo_v = dxo_ref[...]
        dxo_b = dxo_v.astype(BF16)
        xhat, r = _rms_parts(xm_ref[...])
        gv = g_ref[...]
        h2_ref[...] = (xhat * gv).astype(BF16)
        dh = jnp.zeros((tm, d), F32)
        for c0, cw in chunks:
            dact = lax.dot_general(dxo_b, wd_ref[c0:c0 + cw, :], NT, preferred_element_type=F32)
            gt = gate_ref[:, c0:c0 + cw].astype(F32)
            ut = up_ref[:, c0:c0 + cw].astype(F32)
            sg = _sigmoid(gt)
            sl = gt * sg
            act_ref[:, c0:c0 + cw] = (sl * ut).astype(BF16)
            dgt = (dact * ut * (sg * (1.0 + gt * (1.0 - sg)))).astype(BF16)
            dut = (dact * sl).astype(BF16)
            dgate_ref[:, c0:c0 + cw] = dgt
            dup_ref[:, c0:c0 + cw] = dut
            dh = dh + lax.dot_general(dgt, wg_ref[:, c0:c0 + cw], NT, preferred_element_type=F32)
            dh = dh + lax.dot_general(dut, wu_ref[:, c0:c0 + cw], NT, preferred_element_type=F32)
        dxhat = dh * gv
        dx = r * (dxhat - xhat * jnp.mean(dxhat * xhat, axis=-1, keepdims=True))
        dxm_ref[...] = dxo_v + dx

        @pl.when(i == 0)
        def _():
            dg_ref[...] = jnp.zeros_like(dg_ref)

        dg_ref[...] += jnp.sum(dh * xhat, axis=0, keepdims=True)

    return pl.pallas_call(
        body, name=name, grid=(t // tm,),
        in_specs=[_row_spec(tm, d), _row_spec(tm, d), _const_spec((1, d)), _row_spec(tm, f), _row_spec(tm, f),
                  _const_spec((d, f)), _const_spec((d, f)), _const_spec((f, d))],
        out_specs=[_row_spec(tm, d), _row_spec(tm, f), _row_spec(tm, f), _row_spec(tm, f), _row_spec(tm, d),
                   pl.BlockSpec((1, d), lambda i: (0, 0))],
        out_shape=[jax.ShapeDtypeStruct((t, d), F32), jax.ShapeDtypeStruct((t, f), BF16),
                   jax.ShapeDtypeStruct((t, f), BF16), jax.ShapeDtypeStruct((t, f), BF16),
                   jax.ShapeDtypeStruct((t, d), BF16), jax.ShapeDtypeStruct((1, d), F32)],
        compiler_params=_cp(("arbitrary",), VMEM_BIG),
    )(dxo, xm, g, gate, up, wg, wu, wd)


def _wgrad(a, b, tt, name):
    t, k = a.shape
    n = b.shape[1]

    def body(a_ref, b_ref, o_ref):
        i = pl.program_id(0)

        @pl.when(i == 0)
        def _():
            o_ref[...] = jnp.zeros_like(o_ref)

        o_ref[...] += lax.dot_general(a_ref[...].astype(BF16), b_ref[...].astype(BF16), TN,
                                      preferred_element_type=F32)

    return pl.pallas_call(
        body, name=name, grid=(t // tt,),
        in_specs=[_row_spec(tt, k), _row_spec(tt, n)],
        out_specs=pl.BlockSpec((k, n), lambda i: (0, 0)),
        out_shape=jax.ShapeDtypeStruct((k, n), F32),
        compiler_params=_cp(("arbitrary",), VMEM_BIG),
    )(a, b)


def _final_loss(x, g, target, tm, name):
    t, d = x.shape

    def body(x_ref, g_ref, t_ref, loss_ref, dx_ref, dg_ref):
        i = pl.program_id(0)
        xhat, r = _rms_parts(x_ref[...])
        gv = g_ref[...]
        err = xhat * gv - t_ref[...]
        dy = err * (1.0 / d)
        dxhat = dy * gv
        dx_ref[...] = r * (dxhat - xhat * jnp.mean(dxhat * xhat, axis=-1, keepdims=True))

        @pl.when(i == 0)
        def _():
            dg_ref[...] = jnp.zeros_like(dg_ref)
            loss_ref[...] = jnp.zeros_like(loss_ref)

        dg_ref[...] += jnp.sum(dy * xhat, axis=0, keepdims=True)
        part = jnp.sum(jnp.sum(err * err, axis=-1, keepdims=True), axis=0, keepdims=True) * (0.5 / d)
        loss_ref[...] += jnp.broadcast_to(part, loss_ref.shape)

    return pl.pallas_call(
        body, name=name, grid=(t // tm,),
        in_specs=[_row_spec(tm, d), _const_spec((1, d)), _row_spec(tm, d)],
        out_specs=[pl.BlockSpec((8, 128), lambda i: (0, 0)), _row_spec(tm, d), pl.BlockSpec((1, d), lambda i: (0, 0))],
        out_shape=[jax.ShapeDtypeStruct((8, 128), F32), jax.ShapeDtypeStruct((t, d), F32),
                   jax.ShapeDtypeStruct((1, d), F32)],
        compiler_params=_cp(("arbitrary",)),
    )(x, g, target)


def _col_head(width):
    return lax.broadcasted_iota(jnp.int32, (1, width), 1) // HEAD_DIM


def _keep_head(a, colh, h):
    return jnp.where(colh == h, a, jnp.zeros_like(a))


def _softmax_cols(s, sink=None):
    m = jnp.max(s, axis=0, keepdims=True)
    if sink is not None:
        m = jnp.maximum(m, sink)
    p = jnp.exp(s - m)
    l = jnp.sum(p, axis=0, keepdims=True)
    if sink is None:
        return p * (1.0 / l), None
    es = jnp.exp(sink - m)
    inv = 1.0 / (l + es)
    return p * inv, es * inv


def _add4(v):
    return (v[0] + v[1]) + (v[2] + v[3])


def _mem_attn_fwd(qm, mk, mv):
    colh = _col_head(mk.shape[1])
    mks = mk * SCALE
    heads = range(N_MEM_HEADS)
    ss = [lax.dot_general(_keep_head(mks, colh, h), qm, NT, preferred_element_type=F32) for h in heads]
    ps = [_softmax_cols(s)[0].astype(BF16) for s in ss]
    return _add4([lax.dot_general(ps[h], _keep_head(mv, colh, h), TN, preferred_element_type=F32) for h in heads])


def _mem_attn_bwd(qm, dy_b, mk, mv):
    colh = _col_head(mk.shape[1])
    mks = mk * SCALE
    heads = range(N_MEM_HEADS)
    khs = [_keep_head(mks, colh, h) for h in heads]
    vhs = [_keep_head(mv, colh, h) for h in heads]
    ss = [lax.dot_general(khs[h], qm, NT, preferred_element_type=F32) for h in heads]
    dps = [lax.dot_general(vhs[h], dy_b, NT, preferred_element_type=F32) for h in heads]
    pbs, dsbs = [], []
    for h in heads:
        p, _ = _softmax_cols(ss[h])
        ds = p * (dps[h] - jnp.sum(p * dps[h], axis=0, keepdims=True))
        pbs.append(p.astype(BF16))
        dsbs.append(ds.astype(BF16))
    y = _add4([lax.dot_general(pbs[h], vhs[h], TN, preferred_element_type=F32) for h in heads])
    dq = _add4([lax.dot_general(dsbs[h], khs[h], TN, preferred_element_type=F32) for h in heads])
    dmk = _add4([jnp.where(colh == h, jnp.dot(dsbs[h], qm, preferred_element_type=F32) * SCALE, 0.0) for h in heads])
    dmv = _add4([jnp.where(colh == h, jnp.dot(pbs[h], dy_b, preferred_element_type=F32), 0.0) for h in heads])
    return y, dq, dmk, dmv


def _shift_down(v, halo, k):
    rolled = pltpu.roll(v, k, 0)
    hrolled = pltpu.roll(halo, k, 0)[0:8]
    rows = lax.broadcasted_iota(jnp.int32, (8, v.shape[1]), 0)
    first = jnp.where(rows < k, hrolled, rolled[0:8])
    return jnp.concatenate([first, rolled[8:]], axis=0)


def _shift_up(v, halo, k):
    n = v.shape[0]
    rolled = pltpu.roll(v, n - k, 0)
    hrolled = pltpu.roll(halo, 8 - k, 0)[0:8]
    rows = lax.broadcasted_iota(jnp.int32, (8, v.shape[1]), 0)
    last = jnp.where(rows >= 8 - k, hrolled, rolled[n - 8:])
    return jnp.concatenate([rolled[:n - 8], last], axis=0)


def _conv_parts(p, ph, cw, first_tile, cwid):
    u = p[:, 0:cwid].astype(F32)
    bg = p[:, cwid:2 * cwid].astype(F32)
    cg = p[:, 2 * cwid:3 * cwid].astype(F32)
    v = cg * u
    vh = ph[:, 2 * cwid:3 * cwid].astype(F32) * ph[:, 0:cwid].astype(F32)
    vh = jnp.where(first_tile, 0.0, vh)
    v1 = _shift_down(v, vh, 1)
    v2 = _shift_down(v, vh, 2)
    conv = cw[0:1, :] * v2 + cw[1:2, :] * v1 + cw[2:3, :] * v
    return u, bg, cg, v, v1, v2, conv


def _halo_prev_spec(rows, n, tm):
    per = tm // rows
    return pl.BlockSpec((rows, n), lambda i: (jnp.maximum(i * per - 1, 0), 0))


def _halo_next_spec(rows, n, tm, t):
    per = tm // rows
    last = t // rows - 1
    return pl.BlockSpec((rows, n), lambda i: (jnp.minimum((i + 1) * per, last), 0))


def _mix_a_fwd(x, proj, convw, memkv, layer, wout, tm, name):
    t, d = x.shape
    n_mem = memkv.shape[0]
    mw = N_MEM_HEADS * HEAD_DIM
    cwid = d - mw
    pw = proj.shape[1]

    def body(x_ref, p_ref, ph_ref, cw_ref, mkv_ref, wo_ref, xo_ref):
        i = pl.program_id(0)
        p = p_ref[...]
        _, bg, _, _, _, _, conv = _conv_parts(p, ph_ref[...], cw_ref[...], i == 0, cwid)
        ytok = (bg * conv).astype(BF16)
        mkv = mkv_ref[...]
        ymem = _mem_attn_fwd(p[:, 3 * cwid:3 * cwid + mw], mkv[:, 0:mw], mkv[:, mw:2 * mw])
        cat = jnp.concatenate([ytok, ymem.astype(BF16)], axis=1)
        xo_ref[...] = x_ref[...] + jnp.dot(cat, wo_ref[...], preferred_element_type=F32)

    return pl.pallas_call(
        body, name=name, grid=(t // tm,),
        in_specs=[_row_spec(tm, d), _row_spec(tm, pw), _halo_prev_spec(16, pw, tm), _const_spec((3, cwid)),
                  pl.BlockSpec((n_mem, 2 * mw), lambda i: (0, layer)), _const_spec((d, d))],
        out_specs=_row_spec(tm, d),
        out_shape=jax.ShapeDtypeStruct((t, d), F32),
        compiler_params=_cp(("parallel",), VMEM_BIG),
    )(x, proj, proj, convw, memkv, wout)


def _mix_a_bwd(dxm, proj, convw, memkv, layer, wout, tm, name):
    t, d = dxm.shape
    n_mem = memkv.shape[0]
    mw = N_MEM_HEADS * HEAD_DIM
    cwid = d - mw
    pw = proj.shape[1]
    nt = t // tm

    def body(dx_ref, dxn_ref, p_ref, ph_ref, pn_ref, cw_ref, mkv_ref, wo_ref,
             dp_ref, cat_ref, dcw_ref, dmkv_ref, dmk_acc, dmv_acc):
        i = pl.program_id(0)
        p = p_ref[...]
        cw = cw_ref[...]
        wo = wo_ref[...]
        u, bg, cg, v, v1, v2, conv = _conv_parts(p, ph_ref[...], cw, i == 0, cwid)
        dcat = lax.dot_general(dx_ref[...].astype(BF16), wo, NT, preferred_element_type=F32)
        dytok = dcat[:, 0:cwid]
        dymem_b = dcat[:, cwid:d].astype(BF16)
        pn = pn_ref[...]
        dcat_n = lax.dot_general(dxn_ref[...].astype(BF16), wo[0:cwid, :], NT, preferred_element_type=F32)
        dconv_n = jnp.where(i == nt - 1, 0.0, dcat_n * pn[:, cwid:2 * cwid].astype(F32))
        dbg = dytok * conv
        dconv = dytok * bg
        dv = cw[2:3, :] * dconv + cw[1:2, :] * _shift_up(dconv, dconv_n, 1) + cw[0:1, :] * _shift_up(dconv, dconv_n, 2)
        du = dv * cg
        dcg = dv * u
        rows8 = lax.broadcasted_iota(jnp.int32, (8, cwid), 0)
        dcw = (jnp.where(rows8 == 0, jnp.sum(dconv * v2, axis=0, keepdims=True), 0.0)
               + jnp.where(rows8 == 1, jnp.sum(dconv * v1, axis=0, keepdims=True), 0.0)
               + jnp.where(rows8 == 2, jnp.sum(dconv * v, axis=0, keepdims=True), 0.0))
        mkv = mkv_ref[...]
        qm = p[:, 3 * cwid:3 * cwid + mw]
        ymem, dqm, dmk, dmv = _mem_attn_bwd(qm, dymem_b, mkv[:, 0:mw], mkv[:, mw:2 * mw])
        cat_ref[...] = jnp.concatenate([(bg * conv).astype(BF16), ymem.astype(BF16)], axis=1)
        dp_ref[...] = jnp.concatenate([du.astype(BF16), dbg.astype(BF16), dcg.astype(BF16), dqm.astype(BF16)], axis=1)

        @pl.when(i == 0)
        def _():
            dcw_ref[...] = jnp.zeros_like(dcw_ref)
            dmk_acc[...] = jnp.zeros_like(dmk_acc)
            dmv_acc[...] = jnp.zeros_like(dmv_acc)

        dcw_ref[...] += dcw
        dmk_acc[...] += dmk
        dmv_acc[...] += dmv

        @pl.when(i == nt - 1)
        def _():
            dmkv_ref[...] = jnp.concatenate([dmk_acc[...], dmv_acc[...]], axis=1)

    return pl.pallas_call(
        body, name=name, grid=(nt,),
        in_specs=[_row_spec(tm, d), _halo_next_spec(16, d, tm, t), _row_spec(tm, pw), _halo_prev_spec(16, pw, tm),
                  _halo_next_spec(16, pw, tm, t), _const_spec((3, cwid)),
                  pl.BlockSpec((n_mem, 2 * mw), lambda i: (0, layer)), _const_spec((d, d))],
        out_specs=[_row_spec(tm, pw), _row_spec(tm, d), pl.BlockSpec((8, cwid), lambda i: (0, 0)),
                   pl.BlockSpec((n_mem, 2 * mw), lambda i: (0, 0))],
        out_shape=[jax.ShapeDtypeStruct((t, pw), BF16), jax.ShapeDtypeStruct((t, d), BF16),
                   jax.ShapeDtypeStruct((8, cwid), F32), jax.ShapeDtypeStruct((n_mem, 2 * mw), F32)],
        scratch_shapes=[pltpu.VMEM((n_mem, mw), F32), pltpu.VMEM((n_mem, mw), F32)],
        compiler_params=_cp(("arbitrary",), VMEM_BIG),
    )(dxm, dxm, proj, proj, proj, convw, memkv, wout)


def _rel_tables():
    qi = np.arange(BLOCK, dtype=np.int32)[:, None]
    kj = np.arange(2 * BLOCK, dtype=np.int32)[None, :]
    dist = qi + BLOCK - kj
    inw = (dist >= 0) & (dist < BLOCK)
    max_exact = REL_BUCKETS // 2
    dd = np.maximum(np.maximum(dist, 0), 1).astype(np.float32)
    large = max_exact + (np.log(dd / np.float32(max_exact)) / np.float32(math.log(REL_MAX_DIST / max_exact))
                         * np.float32(REL_BUCKETS - max_exact)).astype(np.int32)
    large = np.minimum(large, REL_BUCKETS - 1)
    bucket = np.where(np.maximum(dist, 0) < max_exact, np.maximum(dist, 0), large)
    return np.where(inw, bucket, -1).astype(np.int32)


def _bias_tables(rel_bias, sinks, name):
    bucket_t = jnp.asarray(_rel_tables().T)

    def body(rb_ref, sk_ref, bk_ref, bias_ref, sink_ref):
        bk = bk_ref[...]
        prev = lax.broadcasted_iota(jnp.int32, bk.shape, 0) < BLOCK
        for h in range(N_KV_HEADS):
            for j in range(GROUP):
                head = GROUP * h + j
                acc = jnp.full(bk.shape, NEG, F32)
                for b in range(REL_BUCKETS):
                    acc = jnp.where(bk == b, rb_ref[b, head], acc)
                bias_ref[h, :, j * BLOCK:(j + 1) * BLOCK] = acc
                bias_ref[N_KV_HEADS + h, :, j * BLOCK:(j + 1) * BLOCK] = jnp.where(prev, NEG, acc)
                sink_ref[h, :, j * BLOCK:(j + 1) * BLOCK] = jnp.full((8, BLOCK), sk_ref[0, head], F32)

    smem = pl.BlockSpec(memory_space=pltpu.SMEM)
    return pl.pallas_call(
        body, name=name,
        in_specs=[smem, smem, pl.BlockSpec(memory_space=pltpu.VMEM)],
        out_specs=[pl.BlockSpec(memory_space=pltpu.VMEM), pl.BlockSpec(memory_space=pltpu.VMEM)],
        out_shape=[jax.ShapeDtypeStruct((2 * N_KV_HEADS, 2 * BLOCK, GROUP * BLOCK), F32),
                   jax.ShapeDtypeStruct((N_KV_HEADS, 8, GROUP * BLOCK), F32)],
    )(rel_bias, sinks.reshape(1, N_Q_HEADS), bucket_t)


def _bias_bwd(dbias_a, dbias_b, name):
    bucket_t = jnp.asarray(_rel_tables().T)

    def body(da_ref, db_ref, bk_ref, o_ref):
        bk = bk_ref[...]
        ri = lax.broadcasted_iota(jnp.int32, (REL_BUCKETS, 128), 0)
        ci = lax.broadcasted_iota(jnp.int32, (REL_BUCKETS, 128), 1)
        out = jnp.zeros((REL_BUCKETS, 128), F32)
        for h in range(N_KV_HEADS):
            dsum = da_ref[h] + db_ref[h]
            for j in range(GROUP):
                head = GROUP * h + j
                seg = dsum[:, j * BLOCK:(j + 1) * BLOCK]
                for b in range(REL_BUCKETS):
                    val = jnp.sum(jnp.sum(jnp.where(bk == b, seg, 0.0), axis=0, keepdims=True), axis=1, keepdims=True)
                    out = out + jnp.where((ri == b) & (ci == head), val, 0.0)
        o_ref[...] = out

    vm = pl.BlockSpec(memory_space=pltpu.VMEM)
    return pl.pallas_call(
        body, name=name, in_specs=[vm, vm, vm], out_specs=vm,
        out_shape=jax.ShapeDtypeStruct((REL_BUCKETS, 128), F32),
    )(dbias_a, dbias_b, bucket_t)


def _stack_members(ref, r0, width):
    blk = ref[pl.ds(r0, BLOCK), 0:GROUP * width]
    return jnp.concatenate([blk[:, j * width:(j + 1) * width] for j in range(GROUP)], axis=0)


def _mix_b_fwd(x, qp, kv, bias, sinkt, memkv, layer, wout, tm, name):
    t, d = x.shape
    n_mem = memkv.shape[0]
    mw = N_MEM_HEADS * HEAD_DIM
    qw = d - mw
    kw = N_KV_HEADS * HEAD_DIM
    nb = tm // BLOCK
    rows = GROUP * BLOCK

    def body(x_ref, q_ref, kv_ref, kvh_ref, bias_ref, sink_ref, mkv_ref, wo_ref, xo_ref, kvx, ytok):
        i = pl.program_id(0)
        kvx[0:BLOCK, :] = kvh_ref[...]
        kvx[BLOCK:BLOCK + tm, :] = kv_ref[...]
        colh = _col_head(kw)

        def blk(b, carry):
            r0 = pl.multiple_of(b * BLOCK, BLOCK)
            win = kvx[pl.ds(r0, 2 * BLOCK), :]
            kwin = win[:, 0:kw] * SCALE
            vwin = win[:, kw:2 * kw]
            qs = _stack_members(q_ref, r0, kw)
            first = ((i == 0) & (b == 0)).astype(jnp.int32) * N_KV_HEADS
            heads = range(N_KV_HEADS)
            ss = [lax.dot_general(_keep_head(kwin, colh, h), qs, NT, preferred_element_type=F32) for h in heads]
            ps = [_softmax_cols(ss[h] + bias_ref[first + h], sink_ref[h][0:1, :])[0].astype(BF16) for h in heads]
            o = _add4([lax.dot_general(ps[h], _keep_head(vwin, colh, h), TN, preferred_element_type=F32)
                       for h in heads])
            for j in range(GROUP):
                ytok[pl.ds(r0, BLOCK), j * kw:(j + 1) * kw] = o[j * BLOCK:(j + 1) * BLOCK].astype(BF16)
            return carry

        for b_static in range(nb):
            blk(b_static, 0)
        mkv = mkv_ref[...]
        ymem = _mem_attn_fwd(q_ref[:, qw:d], mkv[:, 0:mw], mkv[:, mw:2 * mw])
        cat = jnp.concatenate([ytok[...], ymem.astype(BF16)], axis=1)
        xo_ref[...] = x_ref[...] + jnp.dot(cat, wo_ref[...], preferred_element_type=F32)

    return pl.pallas_call(
        body, name=name, grid=(t // tm,),
        in_specs=[_row_spec(tm, d), _row_spec(tm, d), _row_spec(tm, 2 * kw), _halo_prev_spec(BLOCK, 2 * kw, tm),
                  _const_spec((2 * N_KV_HEADS, 2 * BLOCK, rows)), _const_spec((N_KV_HEADS, 8, rows)),
                  pl.BlockSpec((n_mem, 2 * mw), lambda i: (0, layer)), _const_spec((d, d))],
        out_specs=_row_spec(tm, d),
        out_shape=jax.ShapeDtypeStruct((t, d), F32),
        scratch_shapes=[pltpu.VMEM((tm + BLOCK, 2 * kw), BF16), pltpu.VMEM((tm, qw), BF16)],
        compiler_params=_cp(("parallel",), VMEM_BIG),
    )(x, qp, kv, kv, bias, sinkt, memkv, wout)


def _mix_b_bwd(dxm, qp, kv, bias, sinkt, memkv, layer, wout, tm, name):
    t, d = dxm.shape
    n_mem = memkv.shape[0]
    mw = N_MEM_HEADS * HEAD_DIM
    qw = d - mw
    kw = N_KV_HEADS * HEAD_DIM
    nb = tm // BLOCK
    nt = t // tm
    rows = GROUP * BLOCK

    def body(dx_ref, q_ref, kv_ref, kvh_ref, bias_ref, sink_ref, mkv_ref, wo_ref,
             dq_ref, cat_ref, dkv_ref, dkvh_ref, dbias_ref, dsink_ref, dmkv_ref,
             kvx, dkvx, dcat_s, dmk_acc, dmv_acc):
        i = pl.program_id(0)

        @pl.when(i == 0)
        def _():
            dbias_ref[...] = jnp.zeros_like(dbias_ref)
            dsink_ref[...] = jnp.zeros_like(dsink_ref)
            dmk_acc[...] = jnp.zeros_like(dmk_acc)
            dmv_acc[...] = jnp.zeros_like(dmv_acc)

        kvx[0:BLOCK, :] = kvh_ref[...]
        kvx[BLOCK:BLOCK + tm, :] = kv_ref[...]
        dkvx[...] = jnp.zeros_like(dkvx)
        dcat_s[...] = lax.dot_general(dx_ref[...].astype(BF16), wo_ref[...], NT,
                                      preferred_element_type=F32).astype(BF16)
        colh = _col_head(kw)
        lane8 = lax.broadcasted_iota(jnp.int32, (8, 128), 1)

        def blk(b, carry):
            r0 = pl.multiple_of(b * BLOCK, BLOCK)
            win = kvx[pl.ds(r0, 2 * BLOCK), :]
            kwin = win[:, 0:kw] * SCALE
            vwin = win[:, kw:2 * kw]
            qs = _stack_members(q_ref, r0, kw)
            dos = _stack_members(dcat_s, r0, kw)
            first = ((i == 0) & (b == 0)).astype(jnp.int32) * N_KV_HEADS
            heads = range(N_KV_HEADS)
            khs = [_keep_head(kwin, colh, h) for h in heads]
            vhs = [_keep_head(vwin, colh, h) for h in heads]
            ss = [lax.dot_general(khs[h], qs, NT, preferred_element_type=F32) for h in heads]
            dps = [lax.dot_general(vhs[h], dos, NT, preferred_element_type=F32) for h in heads]
            dsink = jnp.zeros((8, 128), F32)
            pbs, dsbs = [], []
            for h in heads:
                p, sinkp = _softmax_cols(ss[h] + bias_ref[first + h], sink_ref[h][0:1, :])
                delta = jnp.sum(p * dps[h], axis=0, keepdims=True)
                ds = p * (dps[h] - delta)
                dbias_ref[h] += ds
                sd = sinkp * delta
                for j in range(GROUP):
                    val = -jnp.sum(sd[:, j * BLOCK:(j + 1) * BLOCK], axis=1, keepdims=True)
                    dsink = dsink + jnp.where(lane8 == 4 * j + h, val, 0.0)
                pbs.append(p.astype(BF16))
                dsbs.append(ds.astype(BF16))
            y = _add4([lax.dot_general(pbs[h], vhs[h], TN, preferred_element_type=F32) for h in heads])
            dq = _add4([lax.dot_general(dsbs[h], khs[h], TN, preferred_element_type=F32) for h in heads])
            dk = _add4([jnp.where(colh == h, jnp.dot(dsbs[h], qs, preferred_element_type=F32) * SCALE, 0.0)
                        for h in heads])
            dv = _add4([jnp.where(colh == h, jnp.dot(pbs[h], dos, preferred_element_type=F32), 0.0) for h in heads])
            for j in range(GROUP):
                cat_ref[pl.ds(r0, BLOCK), j * kw:(j + 1) * kw] = y[j * BLOCK:(j + 1) * BLOCK].astype(BF16)
                dq_ref[pl.ds(r0, BLOCK), j * kw:(j + 1) * kw] = dq[j * BLOCK:(j + 1) * BLOCK].astype(BF16)
            dsink_ref[...] += dsink
            dkvx[pl.ds(r0, 2 * BLOCK), :] += jnp.concatenate([dk, dv], axis=1)
            return carry

        for b_static in range(nb):
            blk(b_static, 0)
        dkvh_ref[0] = dkvx[0:BLOCK, :]
        dkv_ref[...] = dkvx[BLOCK:BLOCK + tm, :]

        mkv = mkv_ref[...]
        ymem, dqm, dmk, dmv = _mem_attn_bwd(q_ref[:, qw:d], dcat_s[:, qw:d], mkv[:, 0:mw], mkv[:, mw:2 * mw])
        cat_ref[:, qw:d] = ymem.astype(BF16)
        dq_ref[:, qw:d] = dqm.astype(BF16)
        dmk_acc[...] += dmk
        dmv_acc[...] += dmv

        @pl.when(i == nt - 1)
        def _():
            dmkv_ref[...] = jnp.concatenate([dmk_acc[...], dmv_acc[...]], axis=1)

    return pl.pallas_call(
        body, name=name, grid=(nt,),
        in_specs=[_row_spec(tm, d), _row_spec(tm, d), _row_spec(tm, 2 * kw), _halo_prev_spec(BLOCK, 2 * kw, tm),
                  _const_spec((2 * N_KV_HEADS, 2 * BLOCK, rows)), _const_spec((N_KV_HEADS, 8, rows)),
                  pl.BlockSpec((n_mem, 2 * mw), lambda i: (0, layer)), _const_spec((d, d))],
        out_specs=[_row_spec(tm, d), _row_spec(tm, d), _row_spec(tm, 2 * kw),
                   pl.BlockSpec((1, BLOCK, 2 * kw), lambda i: (i, 0, 0)),
                   pl.BlockSpec((N_KV_HEADS, 2 * BLOCK, rows), lambda i: (0, 0, 0)),
                   pl.BlockSpec((8, 128), lambda i: (0, 0)),
                   pl.BlockSpec((n_mem, 2 * mw), lambda i: (0, 0))],
        out_shape=[jax.ShapeDtypeStruct((t, d), BF16), jax.ShapeDtypeStruct((t, d), BF16),
                   jax.ShapeDtypeStruct((t, 2 * kw), F32), jax.ShapeDtypeStruct((nt, BLOCK, 2 * kw), F32),
                   jax.ShapeDtypeStruct((N_KV_HEADS, 2 * BLOCK, rows), F32), jax.ShapeDtypeStruct((8, 128), F32),
                   jax.ShapeDtypeStruct((n_mem, 2 * mw), F32)],
        scratch_shapes=[pltpu.VMEM((tm + BLOCK, 2 * kw), BF16), pltpu.VMEM((tm + BLOCK, 2 * kw), F32),
                        pltpu.VMEM((tm, d), BF16), pltpu.VMEM((n_mem, mw), F32), pltpu.VMEM((n_mem, mw), F32)],
        compiler_params=_cp(("arbitrary",), VMEM_BIG),
    )(dxm, qp, kv, kv, bias, sinkt, memkv, wout)


def _kv_assemble(main_a, halo_a, main_b, halo_b, tm, name):
    t, n = main_a.shape
    nt = t // tm

    def body(ma_ref, ha_ref, mb_ref, hb_ref, o_ref):
        i = pl.program_id(0)
        s = ma_ref[...] + mb_ref[...]
        tail = jnp.where(i == nt - 1, 0.0, ha_ref[0] + hb_ref[0])
        o_ref[...] = jnp.concatenate([s[0:tm - BLOCK], s[tm - BLOCK:] + tail], axis=0).astype(BF16)

    halo_spec = pl.BlockSpec((1, BLOCK, n), lambda i: (jnp.minimum(i + 1, nt - 1), 0, 0))
    return pl.pallas_call(
        body, name=name, grid=(nt,),
        in_specs=[_row_spec(tm, n), halo_spec, _row_spec(tm, n), halo_spec],
        out_specs=_row_spec(tm, n),
        out_shape=jax.ShapeDtypeStruct((t, n), BF16),
        compiler_params=_cp(("parallel",)),
    )(main_a, halo_a, main_b, halo_b)


def _adam_math(w, g, m, v):
    m2 = ADAM_B1 * m + (1.0 - ADAM_B1) * g
    v2 = ADAM_B2 * v + (1.0 - ADAM_B2) * (g * g)
    m_hat = m2 / (1.0 - ADAM_B1 ** ADAM_STEP)
    v_hat = v2 / (1.0 - ADAM_B2 ** ADAM_STEP)
    delta = -ADAM_LR * (m_hat / (jnp.sqrt(v_hat) + ADAM_EPS) + ADAM_WD * w)
    return delta, m2, v2


def _adamw_sharded(w, mine, theirs, m, v, name):
    shape = w.shape
    c = shape[-1]
    r = int(np.prod(shape[:-1]))
    tr = r
    for cand in (512, 256, 128, 64, 32, 16):
        if r % cand == 0:
            tr = cand
            break

    def body(w_ref, a_ref, b_ref, m_ref, v_ref, g_ref, d_ref, mo_ref, vo_ref):
        g = a_ref[0].astype(F32) + b_ref[0].astype(F32)
        for k in range(1, N_CHIPS):
            g = g + (a_ref[k].astype(F32) + b_ref[k].astype(F32))
        delta, m2, v2 = _adam_math(w_ref[...], g, m_ref[...], v_ref[...])
        g_ref[...] = g
        d_ref[...] = delta
        mo_ref[...] = m2
        vo_ref[...] = v2

    rs = pl.BlockSpec((tr, c), lambda i: (i, 0))
    ps = pl.BlockSpec((N_CHIPS, tr, c), lambda i: (0, i, 0))
    sd = jax.ShapeDtypeStruct((r, c), F32)
    outs = pl.pallas_call(
        body, name=name, grid=(r // tr,),
        in_specs=[rs, ps, ps, rs, rs], out_specs=[rs, rs, rs, rs], out_shape=[sd, sd, sd, sd],
        compiler_params=_cp(("parallel",)),
    )(w.reshape(r, c), mine.reshape(N_CHIPS, r, c), theirs.reshape(N_CHIPS, r, c), m.reshape(r, c), v.reshape(r, c))
    return [o.reshape(shape) for o in outs]


def _adamw_packed(w, g, m, v, name):
    def body(w_ref, g_ref, m_ref, v_ref, d_ref, mo_ref, vo_ref):
        delta, m2, v2 = _adam_math(w_ref[...], g_ref[...], m_ref[...], v_ref[...])
        d_ref[...] = delta
        mo_ref[...] = m2
        vo_ref[...] = v2

    vm = pl.BlockSpec(memory_space=pltpu.VMEM)
    sd = jax.ShapeDtypeStruct(w.shape, F32)
    return pl.pallas_call(body, name=name, in_specs=[vm] * 4, out_specs=[vm] * 3, out_shape=[sd] * 3)(w, g, m, v)


def _place():
    return lax.axis_index("x"), lax.axis_index("y"), lax.axis_index("c")


def _hbm(a):
    return pltpu.with_memory_space_constraint(a, pltpu.HBM)


def _other_chips(x, y):
    return [(1 - x, y), (x, 1 - y), (1 - x, 1 - y)]


def _chip_copy(src, land, gather, layer, chip_src, chip_dst, send_sem, recv_sem, peer):
    s = src if gather else src.at[chip_src]
    d = land.at[chip_dst] if layer is None else land.at[chip_dst, layer]
    return pltpu.make_async_remote_copy(src_ref=s, dst_ref=d, send_sem=send_sem, recv_sem=recv_sem,
                                        device_id=peer, device_id_type=MESH)


def _exchange_start(srcs, lands, gather, layers, after, name):
    n = len(srcs)
    hbm = pl.BlockSpec(memory_space=pltpu.HBM)
    sem = pl.BlockSpec(memory_space=pltpu.SEMAPHORE)

    def body(*refs):
        ins, lds = refs[:n], refs[n:2 * n]
        first_out = 2 * n + len(after)
        send_sems, recv_sems, token = refs[first_out], refs[first_out + 1], refs[-1]
        x, y, c = _place()
        me = 2 * x + y
        for t in range(n):
            for r, (px, py) in enumerate(_other_chips(x, y)):
                _chip_copy(ins[t], lds[t], gather, layers[t], 2 * px + py, me, send_sems.at[3 * t + r],
                           recv_sems.at[3 * t + r], (px, py, c)).start()
        token[...] = jnp.zeros_like(token)

    both = list(srcs) + list(lands)
    outs = pl.pallas_call(
        body, name=name, in_specs=[hbm] * (2 * n) + [pl.BlockSpec(memory_space=pl.ANY)] * len(after),
        out_specs=(sem, sem, *([hbm] * (2 * n)), pl.BlockSpec(memory_space=pltpu.VMEM)),
        out_shape=(pltpu.SemaphoreType.DMA((3 * n,)), pltpu.SemaphoreType.DMA((3 * n,)),
                   *[pltpu.HBM(a.shape, a.dtype) for a in both], jax.ShapeDtypeStruct((8, 128), F32)),
        input_output_aliases={t: 2 + t for t in range(2 * n)},
        compiler_params=_cp(has_side_effects=pltpu.SideEffectType.DATAFLOW_SIDE_EFFECTING),
    )(*[_hbm(a) for a in both], *after)
    return dict(send=outs[0], recv=outs[1], srcs=list(outs[2:2 + n]), lands=list(outs[2 + n:2 + 2 * n]),
                token=outs[-1], gather=gather, layers=list(layers))


def _exchange_wait(groups, lands, land_ids, after, name):
    flat = [s for g in groups for s in g["srcs"]]
    ns, nl, ng, na = len(flat), len(lands), len(groups), len(after)
    hbm = pl.BlockSpec(memory_space=pltpu.HBM)
    sem = pl.BlockSpec(memory_space=pltpu.SEMAPHORE)

    def body(*refs):
        srcs, lds = refs[:ns], refs[ns:ns + nl]
        sems = refs[ns + nl:ns + nl + 2 * ng]
        x, y, c = _place()
        k = 0
        for gi, g in enumerate(groups):
            for t in range(len(g["srcs"])):
                for r, (px, py) in enumerate(_other_chips(x, y)):
                    cp = _chip_copy(srcs[k], lds[land_ids[gi][t]], g["gather"], g["layers"][t], 0, 0,
                                    sems[2 * gi].at[3 * t + r], sems[2 * gi + 1].at[3 * t + r], (px, py, c))
                    cp.wait_send()
                    cp.wait_recv()
                k += 1

    both = flat + list(lands)
    sem_args = [a for g in groups for a in (g["send"], g["recv"])]
    outs = pl.pallas_call(
        body, name=name,
        in_specs=[hbm] * (ns + nl) + [sem] * (2 * ng) + [pl.BlockSpec(memory_space=pl.ANY)] * na,
        out_specs=[hbm] * (ns + nl),
        out_shape=[pltpu.HBM(a.shape, a.dtype) for a in both],
        input_output_aliases={t: t for t in range(ns + nl)},
        compiler_params=_cp(has_side_effects=pltpu.SideEffectType.DATAFLOW_SIDE_EFFECTING),
    )(*both, *sem_args, *after)
    return list(outs[ns:])


def _core_exchange(arrays, name):
    n = len(arrays)
    hbm = pl.BlockSpec(memory_space=pltpu.HBM)

    def body(*refs):
        ins, outs = refs[:n], refs[n:2 * n]
        send_sems, recv_sems = refs[2 * n:]
        x, y, c = _place()
        copies = []
        for t in range(n):
            cp = pltpu.make_async_remote_copy(
                src_ref=ins[t], dst_ref=outs[t], send_sem=send_sems.at[t], recv_sem=recv_sems.at[t],
                device_id=(x, y, 1 - c), device_id_type=MESH)
            cp.start()
            copies.append(cp)
        for cp in copies:
            cp.wait()

    return pl.pallas_call(
        body, name=name, in_specs=[hbm] * n, out_specs=[hbm] * n,
        out_shape=[jax.ShapeDtypeStruct(a.shape, a.dtype) for a in arrays],
        scratch_shapes=[pltpu.SemaphoreType.DMA((n,)), pltpu.SemaphoreType.DMA((n,))],
        compiler_params=_cp(has_side_effects=True),
    )(*arrays)


def _all_reduce_packed(pack, name):
    r, c = pack.shape
    vm = pl.BlockSpec(memory_space=pltpu.VMEM)

    def body(p_ref, sum_ref, slots, send_sems, recv_sems):
        x, y, cc = _place()
        me = 4 * x + 2 * y + cc
        slots[me] = p_ref[...]
        copies = []
        for rel in range(1, 8):
            px = 1 - x if rel & 4 else x
            py = 1 - y if rel & 2 else y
            pc = 1 - cc if rel & 1 else cc
            cp = pltpu.make_async_remote_copy(
                src_ref=p_ref, dst_ref=slots.at[me], send_sem=send_sems.at[rel - 1], recv_sem=recv_sems.at[rel - 1],
                device_id=(px, py, pc), device_id_type=MESH)
            cp.start()
            copies.append(cp)
        for cp in copies:
            cp.wait()
        total = slots[0]
        for k in range(1, 8):
            total = total + slots[k]
        sum_ref[...] = total

    return pl.pallas_call(
        body, name=name, in_specs=[vm], out_specs=vm, out_shape=jax.ShapeDtypeStruct((r, c), F32),
        scratch_shapes=[pltpu.VMEM((8, r, c), F32), pltpu.SemaphoreType.DMA((7,)), pltpu.SemaphoreType.DMA((7,))],
        compiler_params=_cp(has_side_effects=True),
    )(pack)


def _pack(items):
    rows = []
    for a in items:
        flat = a.astype(F32).reshape(-1)
        pad = (-flat.shape[0]) % PACK_W
        rows.append(jnp.pad(flat, (0, pad)).reshape(-1, PACK_W))
    out = jnp.concatenate(rows, axis=0)
    pad_r = (-out.shape[0]) % 8
    return jnp.pad(out, ((0, pad_r), (0, 0)))


def _unpack(pack, shapes):
    outs, row = [], 0
    for s in shapes:
        n = int(np.prod(s))
        nr = -(-n // PACK_W)
        outs.append(pack[row:row + nr].reshape(-1)[:n].reshape(s))
        row += nr
    return outs


def _heads_to_member_major(w, axis):
    shp = w.shape
    pre, post = shp[:axis], shp[axis + 1:]
    w4 = w.reshape(pre + (N_KV_HEADS, GROUP, HEAD_DIM) + post)
    w4 = jnp.swapaxes(w4, len(pre), len(pre) + 1)
    return w4.reshape(shp)


def _heads_to_kv_major(w, axis):
    shp = w.shape
    pre, post = shp[:axis], shp[axis + 1:]
    w4 = w.reshape(pre + (GROUP, N_KV_HEADS, HEAD_DIM) + post)
    w4 = jnp.swapaxes(w4, len(pre), len(pre) + 1)
    return w4.reshape(shp)


def kernel(x, mem, norm_mix, norm_ffn, a_w_in, a_conv_w, a_w_out, kv_norm, w_kv, b_w_q, b_sinks, b_w_out, rel_bias, mem_norm, w_mem_kv, w_gate, w_up, w_down, final_norm, loss_target, m_norm_mix, m_norm_ffn, m_a_w_in, m_a_conv_w, m_a_w_out, m_kv_norm, m_w_kv, m_b_w_q, m_b_sinks, m_b_w_out, m_rel_bias, m_mem_norm, m_w_mem_kv, m_w_gate, m_w_up, m_w_down, m_final_norm, v_norm_mix, v_norm_ffn, v_a_w_in, v_a_conv_w, v_a_w_out, v_kv_norm, v_w_kv, v_b_w_q, v_b_sinks, v_b_w_out, v_rel_bias, v_mem_norm, v_w_mem_kv, v_w_gate, v_w_up, v_w_down, v_final_norm):
    t, d = x.shape[1], x.shape[2]
    tm = 512 if t % 512 == 0 and t >= 2048 else 256
    x0 = x.reshape(t, d)
    target = loss_target.reshape(t, d)
    mem2 = mem.reshape(mem.shape[1], d)
    n_mem = mem2.shape[0]
    ax, ay, ac = _place()
    chip = 2 * ax + ay
    cwid = a_conv_w.shape[2] * N_CHIPS
    qw = N_Q_HEADS * HEAD_DIM
    nq = N_CHIPS

    def own_slot(piece):
        return lax.dynamic_update_slice(lax.empty((nq,) + piece.shape, piece.dtype), piece[None],
                                        (chip,) + (0,) * piece.ndim)

    def mixer_shards(i):
        if i < N_A:
            shards = [a_w_in[i], a_w_out[i]] + ([w_mem_kv] if i == 0 else [])
        else:
            j = i - N_A
            shards = [b_w_q[j], b_w_out[j]] + ([w_kv] if j == 0 else [])
        return [a.astype(BF16) for a in shards]

    def ffn_shards(i):
        return [w_gate[i].astype(BF16), w_up[i].astype(BF16), w_down[i].astype(BF16)]

    conv_pad = jnp.pad(a_conv_w, ((0, 0), (0, 8 - a_conv_w.shape[1]), (0, (-a_conv_w.shape[2]) % 128)))
    group_shards = {"0m": mixer_shards(0) + [conv_pad], "0f": ffn_shards(0)}
    for i in range(1, DEPTH):
        group_shards[str(i)] = ffn_shards(i) + mixer_shards(i)
    gathers, prev_tok = {}, []
    for key, shards in group_shards.items():
        gathers[key] = _exchange_start(shards, [own_slot(a) for a in shards], True, [None] * len(shards),
                                       prev_tok, "gather_start_" + key)
        prev_tok = [gathers[key]["token"]]

    def rows_full(g):
        return g.reshape((-1,) + g.shape[2:])

    def cols_full(g):
        return jnp.transpose(g, (1, 0, 2)).reshape(g.shape[1], -1)

    def landed_weights(key, after):
        g = gathers[key]
        return _exchange_wait([g], g["lands"], [list(range(len(g["lands"])))], after, "gather_wait_" + key)

    def mixer_weights(i, got):
        w_first, w_out = (cols_full(got[0]) if i < N_A else rows_full(got[0])), rows_full(got[1])
        if i >= N_A:
            w_first = jnp.concatenate([_heads_to_member_major(w_first[:, :qw], 1), w_first[:, qw:]], axis=1)
            w_out = jnp.concatenate([_heads_to_member_major(w_out[:qw, :], 0), w_out[qw:, :]], axis=0)
        return dict(w_first=w_first, w_out=w_out, extra=got[2] if len(got) > 2 else None)

    def ffn_weights(got):
        return dict(wg=cols_full(got[0]), wu=cols_full(got[1]), wd=rows_full(got[2]))

    bias, sinkt = [], []
    for j in range(2):
        bj, sj = _bias_tables(rel_bias, b_sinks[j], "bias_tables")
        bias.append(bj)
        sinkt.append(sj)

    ws = []
    xs, xmids, projs, gates, ups = [x0], [], [], [], []
    kv = memkv = wmem = wkv = None
    for i in range(DEPTH):
        xin = xs[-1]
        if i == 0:
            got = landed_weights("0m", prev_tok)
            w = mixer_weights(0, got)
            taps = got[3][:, :, 0:3, 0:a_conv_w.shape[2]]
            conv_full = jnp.transpose(taps, (1, 2, 0, 3)).reshape(N_A, 3, cwid)
        else:
            got = landed_weights(str(i), [xin])
            w = dict(mixer_weights(i, got[3:]), **ffn_weights(got[0:3]))
        ws.append(w)
        gm = norm_mix[i].reshape(1, d)
        if i == 0:
            full_mem = jnp.swapaxes(w["extra"], 0, 1).reshape(DEPTH, d, -1)
            wmem = jnp.transpose(full_mem, (1, 0, 2)).reshape(d, -1)
            memkv = _norm_mm(mem2, mem_norm.reshape(1, d), wmem, n_mem, "mem_kv")
        if i < N_A:
            proj = _norm_mm(xin, gm, w["w_first"], tm, "proj_a")
            xmid = _mix_a_fwd(xin, proj, conv_full[i], memkv, i, w["w_out"], tm, "mix_a_fwd")
        else:
            j = i - N_A
            if j == 0:
                wkv = rows_full(w["extra"])
                kv = _norm_mm(xin, kv_norm.reshape(1, d), wkv, tm, "proj_kv")
            proj = _norm_mm(xin, gm, w["w_first"], tm, "proj_b")
            xmid = _mix_b_fwd(xin, proj, kv, bias[j], sinkt[j], memkv, i, w["w_out"], tm, "mix_b_fwd")
        if i == 0:
            w.update(ffn_weights(landed_weights("0f", [xmid])))
        xout, gate, up = _ffn_fwd(xmid, norm_ffn[i].reshape(1, d), w["wg"], w["wu"], w["wd"], tm, "ffn_fwd")
        projs.append(proj)
        xmids.append(xmid)
        gates.append(gate)
        ups.append(up)
        xs.append(xout)

    loss_part, dx, dg_final = _final_loss(xs[-1], final_norm.reshape(1, d), target, tm, "final_loss")

    def rows_pieces(g):
        return g.astype(BF16).reshape((nq, g.shape[0] // nq) + g.shape[1:])

    def cols_pieces(g):
        return jnp.transpose(g.astype(BF16).reshape(g.shape[0], nq, g.shape[1] // nq), (1, 0, 2))

    stacked = dict(a_w_in=a_w_in, a_w_out=a_w_out, w_kv=w_kv[None], b_w_q=b_w_q, b_w_out=b_w_out,
                   w_mem_kv=w_mem_kv, w_gate=w_gate, w_up=w_up, w_down=w_down)
    names = list(stacked)
    land = {k: lax.empty((nq,) + stacked[k].shape, BF16) for k in names}
    own = {k: [None] * stacked[k].shape[0] for k in names}
    scatters, scatter_ids = [], []

    def scatter_start(key, items):
        keys = [k for k, _, _ in items]
        st = _exchange_start([p for _, _, p in items], [land[k] for k in keys], False, [l for _, l, _ in items],
                             [], "scatter_start_" + key)
        for (k, l, p), ld in zip(items, st["lands"]):
            land[k] = ld
            mine_piece = lax.dynamic_index_in_dim(p, chip, 0, keepdims=False)
            if l is None:
                own[k] = [mine_piece[q] for q in range(mine_piece.shape[0])]
            else:
                own[k][l] = mine_piece
        scatters.append(st)
        scatter_ids.append([names.index(k) for k in keys])
        return st["token"][0:1, 0:1]

    g_norm_mix, g_norm_ffn = [None] * DEPTH, [None] * DEPTH
    g_conv, g_sinks = [None] * 2, [None] * 2
    dmemkv = [None] * DEPTH
    dbias, dkv_main, dkv_halo = [None] * 2, [None] * 2, [None] * 2
    g_kv_norm = None
    tok = jnp.zeros((1, 1), F32)
    for i in reversed(range(DEPTH)):
        w = ws[i]
        dxm, dgate, dup, act, h2, dgf = _ffn_bwd(dx, xmids[i], norm_ffn[i].reshape(1, d) + tok, gates[i], ups[i],
                                            w["wg"], w["wu"], w["wd"], tm // 2, "ffn_bwd")
        g_norm_ffn[i] = dgf
        g_wd = _wgrad(act, dx, 2 * tm, "wgrad_down")
        g_wg = _wgrad(h2, dgate, 2 * tm, "wgrad_gate")
        g_wu = _wgrad(h2, dup, 2 * tm, "wgrad_up")
        items = [("w_gate", i, cols_pieces(g_wg)), ("w_up", i, cols_pieces(g_wu)), ("w_down", i, rows_pieces(g_wd))]
        if i == 0:
            tok = scatter_start("0f", items)
            items = []
        gm = norm_mix[i].reshape(1, d)
        if i < N_A:
            dproj, cat, dcw, dmemkv[i] = _mix_a_bwd(dxm, projs[i], conv_full[i] + (tok if i == 0 else 0.0), memkv, i,
                                                    w["w_out"], tm, "mix_a_bwd")
            g_conv[i] = dcw[0:3]
            g_out = _wgrad(cat, dxm, 2 * tm, "wgrad_out")
            dx, g_norm_mix[i], h = _mm_nt_normbwd(dproj, w["w_first"], xs[i], gm, dxm, tm, "proj_a_bwd")
            g_in = _wgrad(h, dproj, 2 * tm, "wgrad_in_a")
            items += [("a_w_in", i, cols_pieces(g_in)), ("a_w_out", i, rows_pieces(g_out))]
        else:
            j = i - N_A
            dqp, cat, dkv_main[j], dkv_halo[j], dbias[j], dsk, dmemkv[i] = _mix_b_bwd(
                dxm, projs[i], kv, bias[j], sinkt[j], memkv, i, w["w_out"], tm, "mix_b_bwd")
            g_sinks[j] = dsk[0, 0:N_Q_HEADS].reshape(GROUP, N_KV_HEADS).T.reshape(N_Q_HEADS)
            g_out = _wgrad(cat, dxm, 2 * tm, "wgrad_out")
            dx, g_norm_mix[i], h = _mm_nt_normbwd(dqp, w["w_first"], xs[i], gm, dxm, tm, "proj_b_bwd")
            g_q = _wgrad(h, dqp, 2 * tm, "wgrad_in_b")
            g_q = jnp.concatenate([_heads_to_kv_major(g_q[:, :qw], 1), g_q[:, qw:]], axis=1)
            g_out = jnp.concatenate([_heads_to_kv_major(g_out[:qw, :], 0), g_out[qw:, :]], axis=0)
            items += [("b_w_q", j, rows_pieces(g_q)), ("b_w_out", j, rows_pieces(g_out))]
            if j == 0:
                dkv = _kv_assemble(dkv_main[0], dkv_halo[0], dkv_main[1], dkv_halo[1], tm, "kv_assemble")
                dx, g_kv_norm, hkv = _mm_nt_normbwd(dkv, wkv, xs[i], kv_norm.reshape(1, d), dx, tm, "proj_kv_bwd")
                items.append(("w_kv", 0, rows_pieces(_wgrad(hkv, dkv, 2 * tm, "wgrad_kv"))))
        if i == 0:
            dmemkv_all = jnp.concatenate([a.astype(BF16) for a in dmemkv], axis=1)
            _, g_mem_norm, hmem = _mm_nt_normbwd(dmemkv_all, wmem, mem2, mem_norm.reshape(1, d),
                                                 jnp.zeros((n_mem, d), F32), n_mem, "mem_kv_bwd")
            g_wmem = _wgrad(hmem, dmemkv_all, n_mem, "wgrad_mem")
            g_wmem = jnp.transpose(g_wmem.astype(BF16).reshape(nq, d // nq, DEPTH, -1), (0, 2, 1, 3))
            items.append(("w_mem_kv", None, g_wmem))
        tok = scatter_start(str(i) if i else "0m", items)
    grad_x = dx.reshape(x.shape)
    g_rel = _bias_bwd(dbias[0], dbias[1], "bias_bwd")[:, 0:N_Q_HEADS]

    small_shapes = [(DEPTH, d), (DEPTH, d), (d,), (d,), (d,), (2, N_Q_HEADS), (REL_BUCKETS, N_Q_HEADS),
                    (N_A, 3, cwid), ()]
    small = _pack([jnp.concatenate(g_norm_mix, axis=0), jnp.concatenate(g_norm_ffn, axis=0), g_kv_norm, g_mem_norm,
                   dg_final, jnp.stack(g_sinks), g_rel, jnp.stack(g_conv), loss_part[0, 0]])
    small_sum = _all_reduce_packed(small, "reduce_small")
    (gs_norm_mix, gs_norm_ffn, gs_kv_norm, gs_mem_norm, gs_final, gs_sinks, gs_rel, gs_conv_full, loss) = _unpack(
        small_sum, small_shapes)
    cq = cwid // N_CHIPS
    gs_conv = lax.dynamic_slice_in_dim(gs_conv_full, chip * cq, cq, axis=2)

    landed = _exchange_wait(scatters, [land[k] for k in names], scatter_ids, [small_sum], "scatter_wait")
    mine = [lax.dynamic_update_slice(ld, jnp.stack(own[k])[None], (chip,) + (0,) * (ld.ndim - 1))
            for k, ld in zip(names, landed)]
    theirs = _core_exchange(mine, "swap_cores")

    weights = dict(norm_mix=norm_mix, norm_ffn=norm_ffn, a_w_in=a_w_in, a_conv_w=a_conv_w, a_w_out=a_w_out,
                   kv_norm=kv_norm, w_kv=w_kv, b_w_q=b_w_q, b_sinks=b_sinks, b_w_out=b_w_out, rel_bias=rel_bias,
                   mem_norm=mem_norm, w_mem_kv=w_mem_kv, w_gate=w_gate, w_up=w_up, w_down=w_down,
                   final_norm=final_norm)
    moms = dict(norm_mix=m_norm_mix, norm_ffn=m_norm_ffn, a_w_in=m_a_w_in, a_conv_w=m_a_conv_w, a_w_out=m_a_w_out,
                kv_norm=m_kv_norm, w_kv=m_w_kv, b_w_q=m_b_w_q, b_sinks=m_b_sinks, b_w_out=m_b_w_out,
                rel_bias=m_rel_bias, mem_norm=m_mem_norm, w_mem_kv=m_w_mem_kv, w_gate=m_w_gate, w_up=m_w_up,
                w_down=m_w_down, final_norm=m_final_norm)
    vars_ = dict(norm_mix=v_norm_mix, norm_ffn=v_norm_ffn, a_w_in=v_a_w_in, a_conv_w=v_a_conv_w, a_w_out=v_a_w_out,
                 kv_norm=v_kv_norm, w_kv=v_w_kv, b_w_q=v_b_w_q, b_sinks=v_b_sinks, b_w_out=v_b_w_out,
                 rel_bias=v_rel_bias, mem_norm=v_mem_norm, w_mem_kv=v_w_mem_kv, w_gate=v_w_gate, w_up=v_w_up,
                 w_down=v_w_down, final_norm=v_final_norm)
    order = list(weights)
    grads, deltas, new_m, new_v = {}, {}, {}, {}
    for k, mi, th in zip(names, mine, theirs):
        shp = weights[k].shape
        g, dl, m2, v2 = _adamw_sharded(weights[k].reshape(mi.shape[1:]), mi, th, moms[k].reshape(mi.shape[1:]),
                                       vars_[k].reshape(mi.shape[1:]), "adamw_" + k)
        grads[k], deltas[k], new_m[k], new_v[k] = g.reshape(shp), dl.reshape(shp), m2.reshape(shp), v2.reshape(shp)
    small_names = ["norm_mix", "norm_ffn", "kv_norm", "mem_norm", "final_norm", "b_sinks", "rel_bias", "a_conv_w"]
    small_g = [gs_norm_mix, gs_norm_ffn, gs_kv_norm, gs_mem_norm, gs_final, gs_sinks, gs_rel, gs_conv]
    shapes = [weights[k].shape for k in small_names]
    dl_p, m_p, v_p = _adamw_packed(_pack([weights[k] for k in small_names]), _pack(small_g),
                                   _pack([moms[k] for k in small_names]), _pack([vars_[k] for k in small_names]),
                                   "adamw_small")
    for k, g, dl, m2, v2 in zip(small_names, small_g, _unpack(dl_p, shapes), _unpack(m_p, shapes), _unpack(v_p, shapes)):
        grads[k], deltas[k], new_m[k], new_v[k] = g.reshape(weights[k].shape), dl, m2, v2

    return (loss, grad_x, *[grads[k] for k in order], *[deltas[k] for k in order],
            *[new_m[k] for k in order], *[new_v[k] for k in order])
```

```python
import functools
import math

import numpy as np
import jax
import jax.numpy as jnp
from jax import lax
from jax.experimental import pallas as pl
from jax.experimental.pallas import tpu as pltpu

F32 = jnp.float32
BF16 = jnp.bfloat16
MESH = pl.DeviceIdType.MESH

EPS = 1e-5
HEAD_DIM = 64
N_MEM_HEADS = 4
N_KV_HEADS = 4
GROUP = 3
N_Q_HEADS = N_KV_HEADS * GROUP
BLOCK = 128
REL_BUCKETS = 32
REL_MAX_DIST = 128
SCALE = HEAD_DIM ** -0.5
NEG = -1e30
N_CHIPS = 4
N_A = 2
DEPTH = 4

ADAM_LR = 0.001
ADAM_B1 = 0.9
ADAM_B2 = 0.999
ADAM_EPS = 1e-08
ADAM_WD = 0.01
ADAM_STEP = 10

VMEM_BIG = 56 * 1024 * 1024
PACK_W = 1024

NT = (((1,), (1,)), ((), ()))
TN = (((0,), (0,)), ((), ()))


def _cp(sem=None, vmem=None, **kw):
    return pltpu.CompilerParams(dimension_semantics=sem, vmem_limit_bytes=vmem, **kw)


def _const_spec(shape):
    nd = len(shape)
    return pl.BlockSpec(shape, lambda i, _n=nd: (0,) * _n, pipeline_mode=pl.Buffered(1))


def _row_spec(tm, n):
    return pl.BlockSpec((tm, n), lambda i: (i, 0))


def _rms_parts(xv):
    r = lax.rsqrt(jnp.mean(xv * xv, axis=-1, keepdims=True) + EPS)
    return xv * r, r


def _sigmoid(z):
    return 1.0 / (1.0 + jnp.exp(-z))


def _ff_chunks(f):
    if f % 512 == 0 or f % 256 != 0:
        return [(0, f)] if f <= 1536 else [(0, f // 2), (f // 2, f - f // 2)]
    n = f // 256
    a = (n + 1) // 2 * 256
    return [(0, a), (a, f - a)]


def _norm_mm(x, g, w, tm, name):
    t, d = x.shape
    n = w.shape[1]

    def body(x_ref, g_ref, w_ref, o_ref):
        xhat, _ = _rms_parts(x_ref[...])
        h = (xhat * g_ref[...]).astype(BF16)
        o_ref[...] = jnp.dot(h, w_ref[...], preferred_element_type=F32).astype(BF16)

    return pl.pallas_call(
        body, name=name, grid=(t // tm,),
        in_specs=[_row_spec(tm, d), _const_spec((1, d)), _const_spec((d, n))],
        out_specs=_row_spec(tm, n),
        out_shape=jax.ShapeDtypeStruct((t, n), BF16),
        compiler_params=_cp(("parallel",), VMEM_BIG),
    )(x, g, w)


def _mm_nt_normbwd(dproj, w, x_in, g, dres, tm, name):
    t, d = x_in.shape
    n = w.shape[1]

    def body(dp_ref, w_ref, x_ref, g_ref, dr_ref, dx_ref, dg_ref, h_ref):
        i = pl.program_id(0)
        xhat, r = _rms_parts(x_ref[...])
        gv = g_ref[...]
        h_ref[...] = (xhat * gv).astype(BF16)
        dh = lax.dot_general(dp_ref[...], w_ref[...], NT, preferred_element_type=F32)
        dxhat = dh * gv
        dx = r * (dxhat - xhat * jnp.mean(dxhat * xhat, axis=-1, keepdims=True))
        dx_ref[...] = dr_ref[...] + dx

        @pl.when(i == 0)
        def _():
            dg_ref[...] = jnp.zeros_like(dg_ref)

        dg_ref[...] += jnp.sum(dh * xhat, axis=0, keepdims=True)

    return pl.pallas_call(
        body, name=name, grid=(t // tm,),
        in_specs=[_row_spec(tm, n), _const_spec((d, n)), _row_spec(tm, d), _const_spec((1, d)), _row_spec(tm, d)],
        out_specs=[_row_spec(tm, d), pl.BlockSpec((1, d), lambda i: (0, 0)), _row_spec(tm, d)],
        out_shape=[jax.ShapeDtypeStruct((t, d), F32), jax.ShapeDtypeStruct((1, d), F32),
                   jax.ShapeDtypeStruct((t, d), BF16)],
        compiler_params=_cp(("arbitrary",), VMEM_BIG),
    )(dproj, w, x_in, g, dres)


def _ffn_fwd(x, g, wg, wu, wd, tm, name):
    t, d = x.shape
    f = wg.shape[1]
    chunks = _ff_chunks(f)

    def body(x_ref, g_ref, wg_ref, wu_ref, wd_ref, xo_ref, gate_ref, up_ref):
        xv = x_ref[...]
        xhat, _ = _rms_parts(xv)
        h = (xhat * g_ref[...]).astype(BF16)
        acc = xv
        for c0, cw in chunks:
            gt = jnp.dot(h, wg_ref[:, c0:c0 + cw], preferred_element_type=F32)
            ut = jnp.dot(h, wu_ref[:, c0:c0 + cw], preferred_element_type=F32)
            gate_ref[:, c0:c0 + cw] = gt.astype(BF16)
            up_ref[:, c0:c0 + cw] = ut.astype(BF16)
            a = (gt * _sigmoid(gt) * ut).astype(BF16)
            acc = acc + jnp.dot(a, wd_ref[c0:c0 + cw, :], preferred_element_type=F32)
        xo_ref[...] = acc

    return pl.pallas_call(
        body, name=name, grid=(t // tm,),
        in_specs=[_row_spec(tm, d), _const_spec((1, d)), _const_spec((d, f)), _const_spec((d, f)), _const_spec((f, d))],
        out_specs=[_row_spec(tm, d), _row_spec(tm, f), _row_spec(tm, f)],
        out_shape=[jax.ShapeDtypeStruct((t, d), F32), jax.ShapeDtypeStruct((t, f), BF16),
                   jax.ShapeDtypeStruct((t, f), BF16)],
        compiler_params=_cp(("parallel",), VMEM_BIG),
    )(x, g, wg, wu, wd)


def _ffn_bwd(dxo, xm, g, gate, up, wg, wu, wd, tm, name):
    t, d = xm.shape
    f = wg.shape[1]
    chunks = _ff_chunks(f)

    def body(dxo_ref, xm_ref, g_ref, gate_ref, up_ref, wg_ref, wu_ref, wd_ref,
             dxm_ref, dgate_ref, dup_ref, act_ref, h2_ref, dg_ref):
        i = pl.program_id(0)
        dxo_v = dxo_ref[...]
        dxo_b = dxo_v.astype(BF16)
        xhat, r = _rms_parts(xm_ref[...])
        gv = g_ref[...]
        h2_ref[...] = (xhat * gv).astype(BF16)
        dh = jnp.zeros((tm, d), F32)
        for c0, cw in chunks:
            dact = lax.dot_general(dxo_b, wd_ref[c0:c0 + cw, :], NT, preferred_element_type=F32)
            gt = gate_ref[:, c0:c0 + cw].astype(F32)
            ut = up_ref[:, c0:c0 + cw].astype(F32)
            sg = _sigmoid(gt)
            sl = gt * sg
            act_ref[:, c0:c0 + cw] = (sl * ut).astype(BF16)
            dgt = (dact * ut * (sg * (1.0 + gt * (1.0 - sg)))).astype(BF16)
            dut = (dact * sl).astype(BF16)
            dgate_ref[:, c0:c0 + cw] = dgt
            dup_ref[:, c0:c0 + cw] = dut
            dh = dh + lax.dot_general(dgt, wg_ref[:, c0:c0 + cw], NT, preferred_element_type=F32)
            dh = dh + lax.dot_general(dut, wu_ref[:, c0:c0 + cw], NT, preferred_element_type=F32)
        dxhat = dh * gv
        dx = r * (dxhat - xhat * jnp.mean(dxhat * xhat, axis=-1, keepdims=True))
        dxm_ref[...] = dxo_v + dx

        @pl.when(i == 0)
        def _():
            dg_ref[...] = jnp.zeros_like(dg_ref)

        dg_ref[...] += jnp.sum(dh * xhat, axis=0, keepdims=True)

    return pl.pallas_call(
        body, name=name, grid=(t // tm,),
        in_specs=[_row_spec(tm, d), _row_spec(tm, d), _const_spec((1, d)), _row_spec(tm, f), _row_spec(tm, f),
                  _const_spec((d, f)), _const_spec((d, f)), _const_spec((f, d))],
        out_specs=[_row_spec(tm, d), _row_spec(tm, f), _row_spec(tm, f), _row_spec(tm, f), _row_spec(tm, d),
                   pl.BlockSpec((1, d), lambda i: (0, 0))],
        out_shape=[jax.ShapeDtypeStruct((t, d), F32), jax.ShapeDtypeStruct((t, f), BF16),
                   jax.ShapeDtypeStruct((t, f), BF16), jax.ShapeDtypeStruct((t, f), BF16),
                   jax.ShapeDtypeStruct((t, d), BF16), jax.ShapeDtypeStruct((1, d), F32)],
        compiler_params=_cp(("arbitrary",), VMEM_BIG),
    )(dxo, xm, g, gate, up, wg, wu, wd)


def _wgrad(a, b, tt, name):
    t, k = a.shape
    n = b.shape[1]
    nt = t // tt

    def body(a_ref, b_ref, o_ref, acc):
        i = pl.program_id(0)

        @pl.when(i == 0)
        def _():
            acc[...] = jnp.zeros_like(acc)

        acc[...] += lax.dot_general(a_ref[...].astype(BF16), b_ref[...].astype(BF16), TN,
                                    preferred_element_type=F32)

        @pl.when(i == nt - 1)
        def _():
            o_ref[...] = acc[...].astype(BF16)

    return pl.pallas_call(
        body, name=name, grid=(nt,),
        in_specs=[_row_spec(tt, k), _row_spec(tt, n)],
        out_specs=pl.BlockSpec((k, n), lambda i: (0, 0)),
        out_shape=jax.ShapeDtypeStruct((k, n), BF16),
        scratch_shapes=[pltpu.VMEM((k, n), F32)],
        compiler_params=_cp(("arbitrary",), VMEM_BIG),
    )(a, b)


def _final_loss(x, g, target, tm, name):
    t, d = x.shape

    def body(x_ref, g_ref, t_ref, loss_ref, dx_ref, dg_ref):
        i = pl.program_id(0)
        xhat, r = _rms_parts(x_ref[...])
        gv = g_ref[...]
        err = xhat * gv - t_ref[...]
        dy = err * (1.0 / d)
        dxhat = dy * gv
        dx_ref[...] = r * (dxhat - xhat * jnp.mean(dxhat * xhat, axis=-1, keepdims=True))

        @pl.when(i == 0)
        def _():
            dg_ref[...] = jnp.zeros_like(dg_ref)
            loss_ref[...] = jnp.zeros_like(loss_ref)

        dg_ref[...] += jnp.sum(dy * xhat, axis=0, keepdims=True)
        part = jnp.sum(jnp.sum(err * err, axis=-1, keepdims=True), axis=0, keepdims=True) * (0.5 / d)
        loss_ref[...] += jnp.broadcast_to(part, loss_ref.shape)

    return pl.pallas_call(
        body, name=name, grid=(t // tm,),
        in_specs=[_row_spec(tm, d), _const_spec((1, d)), _row_spec(tm, d)],
        out_specs=[pl.BlockSpec((8, 128), lambda i: (0, 0)), _row_spec(tm, d), pl.BlockSpec((1, d), lambda i: (0, 0))],
        out_shape=[jax.ShapeDtypeStruct((8, 128), F32), jax.ShapeDtypeStruct((t, d), F32),
                   jax.ShapeDtypeStruct((1, d), F32)],
        compiler_params=_cp(("arbitrary",)),
    )(x, g, target)


def _col_head(width):
    return lax.broadcasted_iota(jnp.int32, (1, width), 1) // HEAD_DIM


def _keep_head(a, colh, h):
    return jnp.where(colh == h, a, jnp.zeros_like(a))


def _softmax_cols(s, sink=None):
    m = jnp.max(s, axis=0, keepdims=True)
    if sink is not None:
        m = jnp.maximum(m, sink)
    p = jnp.exp(s - m)
    l = jnp.sum(p, axis=0, keepdims=True)
    if sink is None:
        return p * (1.0 / l), None
    es = jnp.exp(sink - m)
    inv = 1.0 / (l + es)
    return p * inv, es * inv


def _add4(v):
    return (v[0] + v[1]) + (v[2] + v[3])


def _mem_attn_fwd(qm, mk, mv):
    colh = _col_head(mk.shape[1])
    mks = mk * SCALE
    heads = range(N_MEM_HEADS)
    ss = [lax.dot_general(_keep_head(mks, colh, h), qm, NT, preferred_element_type=F32) for h in heads]
    ps = [_softmax_cols(s)[0].astype(BF16) for s in ss]
    return _add4([lax.dot_general(ps[h], _keep_head(mv, colh, h), TN, preferred_element_type=F32) for h in heads])


def _mem_attn_bwd(qm, dy_b, mk, mv):
    colh = _col_head(mk.shape[1])
    mks = mk * SCALE
    heads = range(N_MEM_HEADS)
    khs = [_keep_head(mks, colh, h) for h in heads]
    vhs = [_keep_head(mv, colh, h) for h in heads]
    ss = [lax.dot_general(khs[h], qm, NT, preferred_element_type=F32) for h in heads]
    dps = [lax.dot_general(vhs[h], dy_b, NT, preferred_element_type=F32) for h in heads]
    pbs, dsbs = [], []
    for h in heads:
        p, _ = _softmax_cols(ss[h])
        ds = p * (dps[h] - jnp.sum(p * dps[h], axis=0, keepdims=True))
        pbs.append(p.astype(BF16))
        dsbs.append(ds.astype(BF16))
    y = _add4([lax.dot_general(pbs[h], vhs[h], TN, preferred_element_type=F32) for h in heads])
    dq = _add4([lax.dot_general(dsbs[h], khs[h], TN, preferred_element_type=F32) for h in heads])
    dmk = _add4([jnp.where(colh == h, jnp.dot(dsbs[h], qm, preferred_element_type=F32) * SCALE, 0.0) for h in heads])
    dmv = _add4([jnp.where(colh == h, jnp.dot(pbs[h], dy_b, preferred_element_type=F32), 0.0) for h in heads])
    return y, dq, dmk, dmv


def _shift_down(v, halo, k):
    rolled = pltpu.roll(v, k, 0)
    hrolled = pltpu.roll(halo, k, 0)[0:8]
    rows = lax.broadcasted_iota(jnp.int32, (8, v.shape[1]), 0)
    first = jnp.where(rows < k, hrolled, rolled[0:8])
    return jnp.concatenate([first, rolled[8:]], axis=0)


def _shift_up(v, halo, k):
    n = v.shape[0]
    rolled = pltpu.roll(v, n - k, 0)
    hrolled = pltpu.roll(halo, 8 - k, 0)[0:8]
    rows = lax.broadcasted_iota(jnp.int32, (8, v.shape[1]), 0)
    last = jnp.where(rows >= 8 - k, hrolled, rolled[n - 8:])
    return jnp.concatenate([rolled[:n - 8], last], axis=0)


def _conv_parts(p, ph, cw, first_tile, cwid):
    u = p[:, 0:cwid].astype(F32)
    bg = p[:, cwid:2 * cwid].astype(F32)
    cg = p[:, 2 * cwid:3 * cwid].astype(F32)
    v = cg * u
    vh = ph[:, 2 * cwid:3 * cwid].astype(F32) * ph[:, 0:cwid].astype(F32)
    vh = jnp.where(first_tile, 0.0, vh)
    v1 = _shift_down(v, vh, 1)
    v2 = _shift_down(v, vh, 2)
    conv = cw[0:1, :] * v2 + cw[1:2, :] * v1 + cw[2:3, :] * v
    return u, bg, cg, v, v1, v2, conv


def _halo_prev_spec(rows, n, tm):
    per = tm // rows
    return pl.BlockSpec((rows, n), lambda i: (jnp.maximum(i * per - 1, 0), 0))


def _halo_next_spec(rows, n, tm, t):
    per = tm // rows
    last = t // rows - 1
    return pl.BlockSpec((rows, n), lambda i: (jnp.minimum((i + 1) * per, last), 0))


def _mix_a_fwd(x, proj, convw, memkv, layer, wout, tm, name):
    t, d = x.shape
    n_mem = memkv.shape[0]
    mw = N_MEM_HEADS * HEAD_DIM
    cwid = d - mw
    pw = proj.shape[1]

    def body(x_ref, p_ref, ph_ref, cw_ref, mkv_ref, wo_ref, xo_ref):
        i = pl.program_id(0)
        p = p_ref[...]
        _, bg, _, _, _, _, conv = _conv_parts(p, ph_ref[...], cw_ref[...], i == 0, cwid)
        ytok = (bg * conv).astype(BF16)
        mkv = mkv_ref[...]
        ymem = _mem_attn_fwd(p[:, 3 * cwid:3 * cwid + mw], mkv[:, 0:mw], mkv[:, mw:2 * mw])
        cat = jnp.concatenate([ytok, ymem.astype(BF16)], axis=1)
        xo_ref[...] = x_ref[...] + jnp.dot(cat, wo_ref[...], preferred_element_type=F32)

    return pl.pallas_call(
        body, name=name, grid=(t // tm,),
        in_specs=[_row_spec(tm, d), _row_spec(tm, pw), _halo_prev_spec(16, pw, tm), _const_spec((3, cwid)),
                  pl.BlockSpec((n_mem, 2 * mw), lambda i: (0, layer)), _const_spec((d, d))],
        out_specs=_row_spec(tm, d),
        out_shape=jax.ShapeDtypeStruct((t, d), F32),
        compiler_params=_cp(("parallel",), VMEM_BIG),
    )(x, proj, proj, convw, memkv, wout)


def _mix_a_bwd(dxm, proj, convw, memkv, layer, wout, tm, name):
    t, d = dxm.shape
    n_mem = memkv.shape[0]
    mw = N_MEM_HEADS * HEAD_DIM
    cwid = d - mw
    pw = proj.shape[1]
    nt = t // tm

    def body(dx_ref, dxn_ref, p_ref, ph_ref, pn_ref, cw_ref, mkv_ref, wo_ref,
             dp_ref, cat_ref, dcw_ref, dmkv_ref, dmk_acc, dmv_acc):
        i = pl.program_id(0)
        p = p_ref[...]
        cw = cw_ref[...]
        wo = wo_ref[...]
        u, bg, cg, v, v1, v2, conv = _conv_parts(p, ph_ref[...], cw, i == 0, cwid)
        dcat = lax.dot_general(dx_ref[...].astype(BF16), wo, NT, preferred_element_type=F32)
        dytok = dcat[:, 0:cwid]
        dymem_b = dcat[:, cwid:d].astype(BF16)
        pn = pn_ref[...]
        dcat_n = lax.dot_general(dxn_ref[...].astype(BF16), wo[0:cwid, :], NT, preferred_element_type=F32)
        dconv_n = jnp.where(i == nt - 1, 0.0, dcat_n * pn[:, cwid:2 * cwid].astype(F32))
        dbg = dytok * conv
        dconv = dytok * bg
        dv = cw[2:3, :] * dconv + cw[1:2, :] * _shift_up(dconv, dconv_n, 1) + cw[0:1, :] * _shift_up(dconv, dconv_n, 2)
        du = dv * cg
        dcg = dv * u
        rows8 = lax.broadcasted_iota(jnp.int32, (8, cwid), 0)
        dcw = (jnp.where(rows8 == 0, jnp.sum(dconv * v2, axis=0, keepdims=True), 0.0)
               + jnp.where(rows8 == 1, jnp.sum(dconv * v1, axis=0, keepdims=True), 0.0)
               + jnp.where(rows8 == 2, jnp.sum(dconv * v, axis=0, keepdims=True), 0.0))
        mkv = mkv_ref[...]
        qm = p[:, 3 * cwid:3 * cwid + mw]
        ymem, dqm, dmk, dmv = _mem_attn_bwd(qm, dymem_b, mkv[:, 0:mw], mkv[:, mw:2 * mw])
        cat_ref[...] = jnp.concatenate([(bg * conv).astype(BF16), ymem.astype(BF16)], axis=1)
        dp_ref[...] = jnp.concatenate([du.astype(BF16), dbg.astype(BF16), dcg.astype(BF16), dqm.astype(BF16)], axis=1)

        @pl.when(i == 0)
        def _():
            dcw_ref[...] = jnp.zeros_like(dcw_ref)
            dmk_acc[...] = jnp.zeros_like(dmk_acc)
            dmv_acc[...] = jnp.zeros_like(dmv_acc)

        dcw_ref[...] += dcw
        dmk_acc[...] += dmk
        dmv_acc[...] += dmv

        @pl.when(i == nt - 1)
        def _():
            dmkv_ref[...] = jnp.concatenate([dmk_acc[...], dmv_acc[...]], axis=1)

    return pl.pallas_call(
        body, name=name, grid=(nt,),
        in_specs=[_row_spec(tm, d), _halo_next_spec(16, d, tm, t), _row_spec(tm, pw), _halo_prev_spec(16, pw, tm),
                  _halo_next_spec(16, pw, tm, t), _const_spec((3, cwid)),
                  pl.BlockSpec((n_mem, 2 * mw), lambda i: (0, layer)), _const_spec((d, d))],
        out_specs=[_row_spec(tm, pw), _row_spec(tm, d), pl.BlockSpec((8, cwid), lambda i: (0, 0)),
                   pl.BlockSpec((n_mem, 2 * mw), lambda i: (0, 0))],
        out_shape=[jax.ShapeDtypeStruct((t, pw), BF16), jax.ShapeDtypeStruct((t, d), BF16),
                   jax.ShapeDtypeStruct((8, cwid), F32), jax.ShapeDtypeStruct((n_mem, 2 * mw), F32)],
        scratch_shapes=[pltpu.VMEM((n_mem, mw), F32), pltpu.VMEM((n_mem, mw), F32)],
        compiler_params=_cp(("arbitrary",), VMEM_BIG),
    )(dxm, dxm, proj, proj, proj, convw, memkv, wout)


def _rel_tables():
    qi = np.arange(BLOCK, dtype=np.int32)[:, None]
    kj = np.arange(2 * BLOCK, dtype=np.int32)[None, :]
    dist = qi + BLOCK - kj
    inw = (dist >= 0) & (dist < BLOCK)
    max_exact = REL_BUCKETS // 2
    dd = np.maximum(np.maximum(dist, 0), 1).astype(np.float32)
    large = max_exact + (np.log(dd / np.float32(max_exact)) / np.float32(math.log(REL_MAX_DIST / max_exact))
                         * np.float32(REL_BUCKETS - max_exact)).astype(np.int32)
    large = np.minimum(large, REL_BUCKETS - 1)
    bucket = np.where(np.maximum(dist, 0) < max_exact, np.maximum(dist, 0), large)
    return np.where(inw, bucket, -1).astype(np.int32)


def _bias_tables(rel_bias, sinks, name):
    bucket_t = jnp.asarray(_rel_tables().T)

    def body(rb_ref, sk_ref, bk_ref, bias_ref, sink_ref):
        bk = bk_ref[...]
        prev = lax.broadcasted_iota(jnp.int32, bk.shape, 0) < BLOCK
        for h in range(N_KV_HEADS):
            for j in range(GROUP):
                head = GROUP * h + j
                acc = jnp.full(bk.shape, NEG, F32)
                for b in range(REL_BUCKETS):
                    acc = jnp.where(bk == b, rb_ref[b, head], acc)
                bias_ref[h, :, j * BLOCK:(j + 1) * BLOCK] = acc
                bias_ref[N_KV_HEADS + h, :, j * BLOCK:(j + 1) * BLOCK] = jnp.where(prev, NEG, acc)
                sink_ref[h, :, j * BLOCK:(j + 1) * BLOCK] = jnp.full((8, BLOCK), sk_ref[0, head], F32)

    smem = pl.BlockSpec(memory_space=pltpu.SMEM)
    return pl.pallas_call(
        body, name=name,
        in_specs=[smem, smem, pl.BlockSpec(memory_space=pltpu.VMEM)],
        out_specs=[pl.BlockSpec(memory_space=pltpu.VMEM), pl.BlockSpec(memory_space=pltpu.VMEM)],
        out_shape=[jax.ShapeDtypeStruct((2 * N_KV_HEADS, 2 * BLOCK, GROUP * BLOCK), F32),
                   jax.ShapeDtypeStruct((N_KV_HEADS, 8, GROUP * BLOCK), F32)],
    )(rel_bias, sinks.reshape(1, N_Q_HEADS), bucket_t)


def _bias_bwd(dbias_a, dbias_b, name):
    bucket_t = jnp.asarray(_rel_tables().T)

    def body(da_ref, db_ref, bk_ref, o_ref):
        bk = bk_ref[...]
        ri = lax.broadcasted_iota(jnp.int32, (REL_BUCKETS, 128), 0)
        ci = lax.broadcasted_iota(jnp.int32, (REL_BUCKETS, 128), 1)
        out = jnp.zeros((REL_BUCKETS, 128), F32)
        for h in range(N_KV_HEADS):
            dsum = da_ref[h] + db_ref[h]
            for j in range(GROUP):
                head = GROUP * h + j
                seg = dsum[:, j * BLOCK:(j + 1) * BLOCK]
                for b in range(REL_BUCKETS):
                    val = jnp.sum(jnp.sum(jnp.where(bk == b, seg, 0.0), axis=0, keepdims=True), axis=1, keepdims=True)
                    out = out + jnp.where((ri == b) & (ci == head), val, 0.0)
        o_ref[...] = out

    vm = pl.BlockSpec(memory_space=pltpu.VMEM)
    return pl.pallas_call(
        body, name=name, in_specs=[vm, vm, vm], out_specs=vm,
        out_shape=jax.ShapeDtypeStruct((REL_BUCKETS, 128), F32),
    )(dbias_a, dbias_b, bucket_t)


def _stack_members(ref, r0, width):
    blk = ref[pl.ds(r0, BLOCK), 0:GROUP * width]
    return jnp.concatenate([blk[:, j * width:(j + 1) * width] for j in range(GROUP)], axis=0)


def _mix_b_fwd(x, qp, kv, bias, sinkt, memkv, layer, wout, tm, name):
    t, d = x.shape
    n_mem = memkv.shape[0]
    mw = N_MEM_HEADS * HEAD_DIM
    qw = d - mw
    kw = N_KV_HEADS * HEAD_DIM
    nb = tm // BLOCK
    rows = GROUP * BLOCK

    def body(x_ref, q_ref, kv_ref, kvh_ref, bias_ref, sink_ref, mkv_ref, wo_ref, xo_ref, kvx, ytok):
        i = pl.program_id(0)
        kvx[0:BLOCK, :] = kvh_ref[...]
        kvx[BLOCK:BLOCK + tm, :] = kv_ref[...]
        colh = _col_head(kw)

        def blk(b, carry):
            r0 = pl.multiple_of(b * BLOCK, BLOCK)
            win = kvx[pl.ds(r0, 2 * BLOCK), :]
            kwin = win[:, 0:kw] * SCALE
            vwin = win[:, kw:2 * kw]
            qs = _stack_members(q_ref, r0, kw)
            first = ((i == 0) & (b == 0)).astype(jnp.int32) * N_KV_HEADS
            heads = range(N_KV_HEADS)
            ss = [lax.dot_general(_keep_head(kwin, colh, h), qs, NT, preferred_element_type=F32) for h in heads]
            ps = [_softmax_cols(ss[h] + bias_ref[first + h], sink_ref[h][0:1, :])[0].astype(BF16) for h in heads]
            o = _add4([lax.dot_general(ps[h], _keep_head(vwin, colh, h), TN, preferred_element_type=F32)
                       for h in heads])
            for j in range(GROUP):
                ytok[pl.ds(r0, BLOCK), j * kw:(j + 1) * kw] = o[j * BLOCK:(j + 1) * BLOCK].astype(BF16)
            return carry

        for b_static in range(nb):
            blk(b_static, 0)
        mkv = mkv_ref[...]
        ymem = _mem_attn_fwd(q_ref[:, qw:d], mkv[:, 0:mw], mkv[:, mw:2 * mw])
        cat = jnp.concatenate([ytok[...], ymem.astype(BF16)], axis=1)
        xo_ref[...] = x_ref[...] + jnp.dot(cat, wo_ref[...], preferred_element_type=F32)

    return pl.pallas_call(
        body, name=name, grid=(t // tm,),
        in_specs=[_row_spec(tm, d), _row_spec(tm, d), _row_spec(tm, 2 * kw), _halo_prev_spec(BLOCK, 2 * kw, tm),
                  _const_spec((2 * N_KV_HEADS, 2 * BLOCK, rows)), _const_spec((N_KV_HEADS, 8, rows)),
                  pl.BlockSpec((n_mem, 2 * mw), lambda i: (0, layer)), _const_spec((d, d))],
        out_specs=_row_spec(tm, d),
        out_shape=jax.ShapeDtypeStruct((t, d), F32),
        scratch_shapes=[pltpu.VMEM((tm + BLOCK, 2 * kw), BF16), pltpu.VMEM((tm, qw), BF16)],
        compiler_params=_cp(("parallel",), VMEM_BIG),
    )(x, qp, kv, kv, bias, sinkt, memkv, wout)


def _mix_b_bwd(dxm, qp, kv, bias, sinkt, memkv, layer, wout, tm, name):
    t, d = dxm.shape
    n_mem = memkv.shape[0]
    mw = N_MEM_HEADS * HEAD_DIM
    qw = d - mw
    kw = N_KV_HEADS * HEAD_DIM
    nb = tm // BLOCK
    nt = t // tm
    rows = GROUP * BLOCK

    def body(dx_ref, q_ref, kv_ref, kvh_ref, bias_ref, sink_ref, mkv_ref, wo_ref,
             dq_ref, cat_ref, dkv_ref, dkvh_ref, dbias_ref, dsink_ref, dmkv_ref,
             kvx, dkvx, dcat_s, dmk_acc, dmv_acc):
        i = pl.program_id(0)

        @pl.when(i == 0)
        def _():
            dbias_ref[...] = jnp.zeros_like(dbias_ref)
            dsink_ref[...] = jnp.zeros_like(dsink_ref)
            dmk_acc[...] = jnp.zeros_like(dmk_acc)
            dmv_acc[...] = jnp.zeros_like(dmv_acc)

        kvx[0:BLOCK, :] = kvh_ref[...]
        kvx[BLOCK:BLOCK + tm, :] = kv_ref[...]
        dkvx[...] = jnp.zeros_like(dkvx)
        dcat_s[...] = lax.dot_general(dx_ref[...].astype(BF16), wo_ref[...], NT,
                                      preferred_element_type=F32).astype(BF16)
        colh = _col_head(kw)
        lane8 = lax.broadcasted_iota(jnp.int32, (8, 128), 1)

        def blk(b, carry):
            r0 = pl.multiple_of(b * BLOCK, BLOCK)
            win = kvx[pl.ds(r0, 2 * BLOCK), :]
            kwin = win[:, 0:kw] * SCALE
            vwin = win[:, kw:2 * kw]
            qs = _stack_members(q_ref, r0, kw)
            dos = _stack_members(dcat_s, r0, kw)
            first = ((i == 0) & (b == 0)).astype(jnp.int32) * N_KV_HEADS
            heads = range(N_KV_HEADS)
            khs = [_keep_head(kwin, colh, h) for h in heads]
            vhs = [_keep_head(vwin, colh, h) for h in heads]
            ss = [lax.dot_general(khs[h], qs, NT, preferred_element_type=F32) for h in heads]
            dps = [lax.dot_general(vhs[h], dos, NT, preferred_element_type=F32) for h in heads]
            dsink = jnp.zeros((8, 128), F32)
            pbs, dsbs = [], []
            for h in heads:
                p, sinkp = _softmax_cols(ss[h] + bias_ref[first + h], sink_ref[h][0:1, :])
                delta = jnp.sum(p * dps[h], axis=0, keepdims=True)
                ds = p * (dps[h] - delta)
                dbias_ref[h] += ds
                sd = sinkp * delta
                for j in range(GROUP):
                    val = -jnp.sum(sd[:, j * BLOCK:(j + 1) * BLOCK], axis=1, keepdims=True)
                    dsink = dsink + jnp.where(lane8 == 4 * j + h, val, 0.0)
                pbs.append(p.astype(BF16))
                dsbs.append(ds.astype(BF16))
            y = _add4([lax.dot_general(pbs[h], vhs[h], TN, preferred_element_type=F32) for h in heads])
            dq = _add4([lax.dot_general(dsbs[h], khs[h], TN, preferred_element_type=F32) for h in heads])
            dk = _add4([jnp.where(colh == h, jnp.dot(dsbs[h], qs, preferred_element_type=F32) * SCALE, 0.0)
                        for h in heads])
            dv = _add4([jnp.where(colh == h, jnp.dot(pbs[h], dos, preferred_element_type=F32), 0.0) for h in heads])
            for j in range(GROUP):
                cat_ref[pl.ds(r0, BLOCK), j * kw:(j + 1) * kw] = y[j * BLOCK:(j + 1) * BLOCK].astype(BF16)
                dq_ref[pl.ds(r0, BLOCK), j * kw:(j + 1) * kw] = dq[j * BLOCK:(j + 1) * BLOCK].astype(BF16)
            dsink_ref[...] += dsink
            dkvx[pl.ds(r0, 2 * BLOCK), :] += jnp.concatenate([dk, dv], axis=1)
            return carry

        for b_static in range(nb):
            blk(b_static, 0)
        dkvh_ref[0] = dkvx[0:BLOCK, :]
        dkv_ref[...] = dkvx[BLOCK:BLOCK + tm, :]

        mkv = mkv_ref[...]
        ymem, dqm, dmk, dmv = _mem_attn_bwd(q_ref[:, qw:d], dcat_s[:, qw:d], mkv[:, 0:mw], mkv[:, mw:2 * mw])
        cat_ref[:, qw:d] = ymem.astype(BF16)
        dq_ref[:, qw:d] = dqm.astype(BF16)
        dmk_acc[...] += dmk
        dmv_acc[...] += dmv

        @pl.when(i == nt - 1)
        def _():
            dmkv_ref[...] = jnp.concatenate([dmk_acc[...], dmv_acc[...]], axis=1)

    return pl.pallas_call(
        body, name=name, grid=(nt,),
        in_specs=[_row_spec(tm, d), _row_spec(tm, d), _row_spec(tm, 2 * kw), _halo_prev_spec(BLOCK, 2 * kw, tm),
                  _const_spec((2 * N_KV_HEADS, 2 * BLOCK, rows)), _const_spec((N_KV_HEADS, 8, rows)),
                  pl.BlockSpec((n_mem, 2 * mw), lambda i: (0, layer)), _const_spec((d, d))],
        out_specs=[_row_spec(tm, d), _row_spec(tm, d), _row_spec(tm, 2 * kw),
                   pl.BlockSpec((1, BLOCK, 2 * kw), lambda i: (i, 0, 0)),
                   pl.BlockSpec((N_KV_HEADS, 2 * BLOCK, rows), lambda i: (0, 0, 0)),
                   pl.BlockSpec((8, 128), lambda i: (0, 0)),
                   pl.BlockSpec((n_mem, 2 * mw), lambda i: (0, 0))],
        out_shape=[jax.ShapeDtypeStruct((t, d), BF16), jax.ShapeDtypeStruct((t, d), BF16),
                   jax.ShapeDtypeStruct((t, 2 * kw), F32), jax.ShapeDtypeStruct((nt, BLOCK, 2 * kw), F32),
                   jax.ShapeDtypeStruct((N_KV_HEADS, 2 * BLOCK, rows), F32), jax.ShapeDtypeStruct((8, 128), F32),
                   jax.ShapeDtypeStruct((n_mem, 2 * mw), F32)],
        scratch_shapes=[pltpu.VMEM((tm + BLOCK, 2 * kw), BF16), pltpu.VMEM((tm + BLOCK, 2 * kw), F32),
                        pltpu.VMEM((tm, d), BF16), pltpu.VMEM((n_mem, mw), F32), pltpu.VMEM((n_mem, mw), F32)],
        compiler_params=_cp(("arbitrary",), VMEM_BIG),
    )(dxm, qp, kv, kv, bias, sinkt, memkv, wout)


def _kv_assemble(main_a, halo_a, main_b, halo_b, tm, name):
    t, n = main_a.shape
    nt = t // tm

    def body(ma_ref, ha_ref, mb_ref, hb_ref, o_ref):
        i = pl.program_id(0)
        s = ma_ref[...] + mb_ref[...]
        tail = jnp.where(i == nt - 1, 0.0, ha_ref[0] + hb_ref[0])
        o_ref[...] = jnp.concatenate([s[0:tm - BLOCK], s[tm - BLOCK:] + tail], axis=0).astype(BF16)

    halo_spec = pl.BlockSpec((1, BLOCK, n), lambda i: (jnp.minimum(i + 1, nt - 1), 0, 0))
    return pl.pallas_call(
        body, name=name, grid=(nt,),
        in_specs=[_row_spec(tm, n), halo_spec, _row_spec(tm, n), halo_spec],
        out_specs=_row_spec(tm, n),
        out_shape=jax.ShapeDtypeStruct((t, n), BF16),
        compiler_params=_cp(("parallel",)),
    )(main_a, halo_a, main_b, halo_b)


def _adam_math(w, g, m, v):
    m2 = ADAM_B1 * m + (1.0 - ADAM_B1) * g
    v2 = ADAM_B2 * v + (1.0 - ADAM_B2) * (g * g)
    m_hat = m2 / (1.0 - ADAM_B1 ** ADAM_STEP)
    v_hat = v2 / (1.0 - ADAM_B2 ** ADAM_STEP)
    delta = -ADAM_LR * (m_hat / (jnp.sqrt(v_hat) + ADAM_EPS) + ADAM_WD * w)
    return delta, m2, v2


def _adamw_sharded(w, mine, theirs, m, v, name):
    shape = w.shape
    c = shape[-1]
    r = int(np.prod(shape[:-1]))
    tr = r
    for cand in (512, 256, 128, 64, 32, 16):
        if r % cand == 0:
            tr = cand
            break

    def body(w_ref, a_ref, b_ref, m_ref, v_ref, g_ref, d_ref, mo_ref, vo_ref):
        g = a_ref[0].astype(F32) + b_ref[0].astype(F32)
        for k in range(1, N_CHIPS):
            g = g + (a_ref[k].astype(F32) + b_ref[k].astype(F32))
        delta, m2, v2 = _adam_math(w_ref[...], g, m_ref[...], v_ref[...])
        g_ref[...] = g
        d_ref[...] = delta
        mo_ref[...] = m2
        vo_ref[...] = v2

    rs = pl.BlockSpec((tr, c), lambda i: (i, 0))
    ps = pl.BlockSpec((N_CHIPS, tr, c), lambda i: (0, i, 0))
    sd = jax.ShapeDtypeStruct((r, c), F32)
    outs = pl.pallas_call(
        body, name=name, grid=(r // tr,),
        in_specs=[rs, ps, ps, rs, rs], out_specs=[rs, rs, rs, rs], out_shape=[sd, sd, sd, sd],
        compiler_params=_cp(("parallel",)),
    )(w.reshape(r, c), mine.reshape(N_CHIPS, r, c), theirs.reshape(N_CHIPS, r, c), m.reshape(r, c), v.reshape(r, c))
    return [o.reshape(shape) for o in outs]


def _adamw_packed(w, g, m, v, name):
    def body(w_ref, g_ref, m_ref, v_ref, d_ref, mo_ref, vo_ref):
        delta, m2, v2 = _adam_math(w_ref[...], g_ref[...], m_ref[...], v_ref[...])
        d_ref[...] = delta
        mo_ref[...] = m2
        vo_ref[...] = v2

    vm = pl.BlockSpec(memory_space=pltpu.VMEM)
    sd = jax.ShapeDtypeStruct(w.shape, F32)
    return pl.pallas_call(body, name=name, in_specs=[vm] * 4, out_specs=[vm] * 3, out_shape=[sd] * 3)(w, g, m, v)


def _place():
    return lax.axis_index("x"), lax.axis_index("y"), lax.axis_index("c")


def _hbm(a):
    return pltpu.with_memory_space_constraint(a, pltpu.HBM)


def _other_chips(x, y):
    return [(1 - x, y), (x, 1 - y), (1 - x, 1 - y)]


def _chip_copy(src, land, gather, layer, chip_src, chip_dst, send_sem, recv_sem, peer):
    s = src if gather else src.at[chip_src]
    d = land.at[chip_dst] if layer is None else land.at[chip_dst, layer]
    return pltpu.make_async_remote_copy(src_ref=s, dst_ref=d, send_sem=send_sem, recv_sem=recv_sem,
                                        device_id=peer, device_id_type=MESH)


def _exchange_start(srcs, lands, gather, layers, after, name):
    n = len(srcs)
    hbm = pl.BlockSpec(memory_space=pltpu.HBM)
    sem = pl.BlockSpec(memory_space=pltpu.SEMAPHORE)

    def body(*refs):
        ins, lds = refs[:n], refs[n:2 * n]
        first_out = 2 * n + len(after)
        send_sems, recv_sems, token = refs[first_out], refs[first_out + 1], refs[-1]
        x, y, c = _place()
        me = 2 * x + y
        for t in range(n):
            for r, (px, py) in enumerate(_other_chips(x, y)):
                _chip_copy(ins[t], lds[t], gather, layers[t], 2 * px + py, me, send_sems.at[3 * t + r],
                           recv_sems.at[3 * t + r], (px, py, c)).start()
        token[...] = jnp.zeros_like(token)

    both = list(srcs) + list(lands)
    outs = pl.pallas_call(
        body, name=name, in_specs=[hbm] * (2 * n) + [pl.BlockSpec(memory_space=pl.ANY)] * len(after),
        out_specs=(sem, sem, *([hbm] * (2 * n)), pl.BlockSpec(memory_space=pltpu.VMEM)),
        out_shape=(pltpu.SemaphoreType.DMA((3 * n,)), pltpu.SemaphoreType.DMA((3 * n,)),
                   *[pltpu.HBM(a.shape, a.dtype) for a in both], jax.ShapeDtypeStruct((8, 128), F32)),
        input_output_aliases={t: 2 + t for t in range(2 * n)},
        compiler_params=_cp(has_side_effects=pltpu.SideEffectType.DATAFLOW_SIDE_EFFECTING),
    )(*[_hbm(a) for a in both], *after)
    return dict(send=outs[0], recv=outs[1], srcs=list(outs[2:2 + n]), lands=list(outs[2 + n:2 + 2 * n]),
                token=outs[-1], gather=gather, layers=list(layers))


def _exchange_wait(groups, lands, land_ids, after, name):
    flat = [s for g in groups for s in g["srcs"]]
    ns, nl, ng, na = len(flat), len(lands), len(groups), len(after)
    hbm = pl.BlockSpec(memory_space=pltpu.HBM)
    sem = pl.BlockSpec(memory_space=pltpu.SEMAPHORE)

    def body(*refs):
        srcs, lds = refs[:ns], refs[ns:ns + nl]
        sems = refs[ns + nl:ns + nl + 2 * ng]
        x, y, c = _place()
        k = 0
        for gi, g in enumerate(groups):
            for t in range(len(g["srcs"])):
                for r, (px, py) in enumerate(_other_chips(x, y)):
                    cp = _chip_copy(srcs[k], lds[land_ids[gi][t]], g["gather"], g["layers"][t], 0, 0,
                                    sems[2 * gi].at[3 * t + r], sems[2 * gi + 1].at[3 * t + r], (px, py, c))
                    cp.wait_send()
                    cp.wait_recv()
                k += 1

    both = flat + list(lands)
    sem_args = [a for g in groups for a in (g["send"], g["recv"])]
    outs = pl.pallas_call(
        body, name=name,
        in_specs=[hbm] * (ns + nl) + [sem] * (2 * ng) + [pl.BlockSpec(memory_space=pl.ANY)] * na,
        out_specs=[hbm] * (ns + nl),
        out_shape=[pltpu.HBM(a.shape, a.dtype) for a in both],
        input_output_aliases={t: t for t in range(ns + nl)},
        compiler_params=_cp(has_side_effects=pltpu.SideEffectType.DATAFLOW_SIDE_EFFECTING),
    )(*both, *sem_args, *after)
    return list(outs[ns:])


def _core_exchange(arrays, name):
    n = len(arrays)
    hbm = pl.BlockSpec(memory_space=pltpu.HBM)

    def body(*refs):
        ins, outs = refs[:n], refs[n:2 * n]
        send_sems, recv_sems = refs[2 * n:]
        x, y, c = _place()
        copies = []
        for t in range(n):
            cp = pltpu.make_async_remote_copy(
                src_ref=ins[t], dst_ref=outs[t], send_sem=send_sems.at[t], recv_sem=recv_sems.at[t],
                device_id=(x, y, 1 - c), device_id_type=MESH)
            cp.start()
            copies.append(cp)
        for cp in copies:
            cp.wait()

    return pl.pallas_call(
        body, name=name, in_specs=[hbm] * n, out_specs=[hbm] * n,
        out_shape=[jax.ShapeDtypeStruct(a.shape, a.dtype) for a in arrays],
        scratch_shapes=[pltpu.SemaphoreType.DMA((n,)), pltpu.SemaphoreType.DMA((n,))],
        compiler_params=_cp(has_side_effects=True),
    )(*arrays)


def _all_reduce_packed(pack, name):
    r, c = pack.shape
    vm = pl.BlockSpec(memory_space=pltpu.VMEM)

    def body(p_ref, sum_ref, slots, send_sems, recv_sems):
        x, y, cc = _place()
        me = 4 * x + 2 * y + cc
        slots[me] = p_ref[...]
        copies = []
        for rel in range(1, 8):
            px = 1 - x if rel & 4 else x
            py = 1 - y if rel & 2 else y
            pc = 1 - cc if rel & 1 else cc
            cp = pltpu.make_async_remote_copy(
                src_ref=p_ref, dst_ref=slots.at[me], send_sem=send_sems.at[rel - 1], recv_sem=recv_sems.at[rel - 1],
                device_id=(px, py, pc), device_id_type=MESH)
            cp.start()
            copies.append(cp)
        for cp in copies:
            cp.wait()
        total = slots[0]
        for k in range(1, 8):
            total = total + slots[k]
        sum_ref[...] = total

    return pl.pallas_call(
        body, name=name, in_specs=[vm], out_specs=vm, out_shape=jax.ShapeDtypeStruct((r, c), F32),
        scratch_shapes=[pltpu.VMEM((8, r, c), F32), pltpu.SemaphoreType.DMA((7,)), pltpu.SemaphoreType.DMA((7,))],
        compiler_params=_cp(has_side_effects=True),
    )(pack)


def _pack(items):
    rows = []
    for a in items:
        flat = a.astype(F32).reshape(-1)
        pad = (-flat.shape[0]) % PACK_W
        rows.append(jnp.pad(flat, (0, pad)).reshape(-1, PACK_W))
    out = jnp.concatenate(rows, axis=0)
    pad_r = (-out.shape[0]) % 8
    return jnp.pad(out, ((0, pad_r), (0, 0)))


def _unpack(pack, shapes):
    outs, row = [], 0
    for s in shapes:
        n = int(np.prod(s))
        nr = -(-n // PACK_W)
        outs.append(pack[row:row + nr].reshape(-1)[:n].reshape(s))
        row += nr
    return outs


def _heads_to_member_major(w, axis):
    shp = w.shape
    pre, post = shp[:axis], shp[axis + 1:]
    w4 = w.reshape(pre + (N_KV_HEADS, GROUP, HEAD_DIM) + post)
    w4 = jnp.swapaxes(w4, len(pre), len(pre) + 1)
    return w4.reshape(shp)


def _heads_to_kv_major(w, axis):
    shp = w.shape
    pre, post = shp[:axis], shp[axis + 1:]
    w4 = w.reshape(pre + (GROUP, N_KV_HEADS, HEAD_DIM) + post)
    w4 = jnp.swapaxes(w4, len(pre), len(pre) + 1)
    return w4.reshape(shp)


def kernel(x, mem, norm_mix, norm_ffn, a_w_in, a_conv_w, a_w_out, kv_norm, w_kv, b_w_q, b_sinks, b_w_out, rel_bias, mem_norm, w_mem_kv, w_gate, w_up, w_down, final_norm, loss_target, m_norm_mix, m_norm_ffn, m_a_w_in, m_a_conv_w, m_a_w_out, m_kv_norm, m_w_kv, m_b_w_q, m_b_sinks, m_b_w_out, m_rel_bias, m_mem_norm, m_w_mem_kv, m_w_gate, m_w_up, m_w_down, m_final_norm, v_norm_mix, v_norm_ffn, v_a_w_in, v_a_conv_w, v_a_w_out, v_kv_norm, v_w_kv, v_b_w_q, v_b_sinks, v_b_w_out, v_rel_bias, v_mem_norm, v_w_mem_kv, v_w_gate, v_w_up, v_w_down, v_final_norm):
    t, d = x.shape[1], x.shape[2]
    tm = 512 if t % 512 == 0 and t >= 2048 else 256
    x0 = x.reshape(t, d)
    target = loss_target.reshape(t, d)
    mem2 = mem.reshape(mem.shape[1], d)
    n_mem = mem2.shape[0]
    ax, ay, ac = _place()
    chip = 2 * ax + ay
    cwid = a_conv_w.shape[2] * N_CHIPS
    qw = N_Q_HEADS * HEAD_DIM
    nq = N_CHIPS

    def own_slot(piece):
        return lax.dynamic_update_slice(lax.empty((nq,) + piece.shape, piece.dtype), piece[None],
                                        (chip,) + (0,) * piece.ndim)

    def mixer_shards(i):
        if i < N_A:
            shards = [a_w_in[i], a_w_out[i]] + ([w_mem_kv] if i == 0 else [])
        else:
            j = i - N_A
            shards = [b_w_q[j], b_w_out[j]] + ([w_kv] if j == 0 else [])
        return [a.astype(BF16) for a in shards]

    def ffn_shards(i):
        return [w_gate[i].astype(BF16), w_up[i].astype(BF16), w_down[i].astype(BF16)]

    conv_pad = jnp.pad(a_conv_w, ((0, 0), (0, 8 - a_conv_w.shape[1]), (0, (-a_conv_w.shape[2]) % 128)))
    first = mixer_shards(0)
    group_shards = {"0a": first[0:1], "0b": first[1:] + [conv_pad], "0f": ffn_shards(0)}
    for i in range(1, DEPTH):
        group_shards[str(i)] = ffn_shards(i) + mixer_shards(i)
    gathers, prev_tok = {}, []
    for key, shards in group_shards.items():
        gathers[key] = _exchange_start(shards, [own_slot(a) for a in shards], True, [None] * len(shards),
                                       prev_tok, "gather_start_" + key)
        prev_tok = [gathers[key]["token"]]

    def rows_full(g):
        return g.reshape((-1,) + g.shape[2:])

    def cols_full(g):
        return jnp.transpose(g, (1, 0, 2)).reshape(g.shape[1], -1)

    def landed_weights(key, after):
        g = gathers[key]
        return _exchange_wait([g], g["lands"], [list(range(len(g["lands"])))], after, "gather_wait_" + key)

    def mixer_weights(i, got):
        w_first, w_out = (cols_full(got[0]) if i < N_A else rows_full(got[0])), rows_full(got[1])
        if i >= N_A:
            w_first = jnp.concatenate([_heads_to_member_major(w_first[:, :qw], 1), w_first[:, qw:]], axis=1)
            w_out = jnp.concatenate([_heads_to_member_major(w_out[:qw, :], 0), w_out[qw:, :]], axis=0)
        return dict(w_first=w_first, w_out=w_out, extra=got[2] if len(got) > 2 else None)

    def ffn_weights(got):
        return dict(wg=cols_full(got[0]), wu=cols_full(got[1]), wd=rows_full(got[2]))

    bias, sinkt = [], []
    for j in range(2):
        bj, sj = _bias_tables(rel_bias, b_sinks[j], "bias_tables")
        bias.append(bj)
        sinkt.append(sj)

    ws = []
    xs, xmids, projs, gates, ups = [x0], [], [], [], []
    kv = memkv = wmem = wkv = None
    for i in range(DEPTH):
        xin = xs[-1]
        if i == 0:
            w = dict(w_first=cols_full(landed_weights("0a", prev_tok)[0]))
        else:
            got = landed_weights(str(i), [xin])
            w = dict(mixer_weights(i, got[3:]), **ffn_weights(got[0:3]))
        ws.append(w)
        gm = norm_mix[i].reshape(1, d)
        if i < N_A:
            proj = _norm_mm(xin, gm, w["w_first"], tm, "proj_a")
            if i == 0:
                got = landed_weights("0b", [proj])
                w["w_out"] = rows_full(got[0])
                full_mem = jnp.swapaxes(got[1], 0, 1).reshape(DEPTH, d, -1)
                wmem = jnp.transpose(full_mem, (1, 0, 2)).reshape(d, -1)
                memkv = _norm_mm(mem2, mem_norm.reshape(1, d), wmem, n_mem, "mem_kv")
                taps = got[2][:, :, 0:3, 0:a_conv_w.shape[2]]
                conv_full = jnp.transpose(taps, (1, 2, 0, 3)).reshape(N_A, 3, cwid)
            xmid = _mix_a_fwd(xin, proj, conv_full[i], memkv, i, w["w_out"], tm, "mix_a_fwd")
        else:
            j = i - N_A
            if j == 0:
                wkv = rows_full(w["extra"])
                kv = _norm_mm(xin, kv_norm.reshape(1, d), wkv, tm, "proj_kv")
            proj = _norm_mm(xin, gm, w["w_first"], tm, "proj_b")
            xmid = _mix_b_fwd(xin, proj, kv, bias[j], sinkt[j], memkv, i, w["w_out"], tm, "mix_b_fwd")
        if i == 0:
            w.update(ffn_weights(landed_weights("0f", [xmid])))
        xout, gate, up = _ffn_fwd(xmid, norm_ffn[i].reshape(1, d), w["wg"], w["wu"], w["wd"], tm, "ffn_fwd")
        projs.append(proj)
        xmids.append(xmid)
        gates.append(gate)
        ups.append(up)
        xs.append(xout)

    loss_part, dx, dg_final = _final_loss(xs[-1], final_norm.reshape(1, d), target, tm, "final_loss")

    def rows_pieces(g):
        return g.astype(BF16).reshape((nq, g.shape[0] // nq) + g.shape[1:])

    def cols_pieces(g):
        return jnp.transpose(g.astype(BF16).reshape(g.shape[0], nq, g.shape[1] // nq), (1, 0, 2))

    stacked = dict(a_w_in=a_w_in, a_w_out=a_w_out, w_kv=w_kv[None], b_w_q=b_w_q, b_w_out=b_w_out,
                   w_mem_kv=w_mem_kv, w_gate=w_gate, w_up=w_up, w_down=w_down)
    names = list(stacked)
    land = {k: lax.empty((nq,) + stacked[k].shape, BF16) for k in names}
    own = {k: [None] * stacked[k].shape[0] for k in names}
    scatters, scatter_ids = [], []

    def scatter_start(key, items):
        keys = [k for k, _, _ in items]
        st = _exchange_start([p for _, _, p in items], [land[k] for k in keys], False, [l for _, l, _ in items],
                             [], "scatter_start_" + key)
        for (k, l, p), ld in zip(items, st["lands"]):
            land[k] = ld
            mine_piece = lax.dynamic_index_in_dim(p, chip, 0, keepdims=False)
            if l is None:
                own[k] = [mine_piece[q] for q in range(mine_piece.shape[0])]
            else:
                own[k][l] = mine_piece
        scatters.append(st)
        scatter_ids.append([names.index(k) for k in keys])
        return st["token"][0:1, 0:1]

    g_norm_mix, g_norm_ffn = [None] * DEPTH, [None] * DEPTH
    g_conv, g_sinks = [None] * 2, [None] * 2
    dmemkv = [None] * DEPTH
    dbias, dkv_main, dkv_halo = [None] * 2, [None] * 2, [None] * 2
    g_kv_norm = None
    tok = jnp.zeros((1, 1), F32)
    for i in reversed(range(DEPTH)):
        w = ws[i]
        dxm, dgate, dup, act, h2, dgf = _ffn_bwd(dx, xmids[i], norm_ffn[i].reshape(1, d) + tok, gates[i], ups[i],
                                            w["wg"], w["wu"], w["wd"], tm // 2, "ffn_bwd")
        g_norm_ffn[i] = dgf
        g_wd = _wgrad(act, dx, 2 * tm, "wgrad_down")
        g_wg = _wgrad(h2, dgate, 2 * tm, "wgrad_gate")
        g_wu = _wgrad(h2, dup, 2 * tm, "wgrad_up")
        items = [("w_gate", i, cols_pieces(g_wg)), ("w_up", i, cols_pieces(g_wu)), ("w_down", i, rows_pieces(g_wd))]
        if i == 0:
            tok = scatter_start("0f", items)
            items = []
        gm = norm_mix[i].reshape(1, d)
        if i < N_A:
            dproj, cat, dcw, dmemkv[i] = _mix_a_bwd(dxm, projs[i], conv_full[i] + (tok if i == 0 else 0.0), memkv, i,
                                                    w["w_out"], tm, "mix_a_bwd")
            g_conv[i] = dcw[0:3]
            g_out = _wgrad(cat, dxm, 2 * tm, "wgrad_out")
            if i == 0:
                dmemkv_all = jnp.concatenate([a.astype(BF16) for a in dmemkv], axis=1)
                _, g_mem_norm, hmem = _mm_nt_normbwd(dmemkv_all, wmem, mem2, mem_norm.reshape(1, d),
                                                     jnp.zeros((n_mem, d), F32), n_mem, "mem_kv_bwd")
                g_wmem = _wgrad(hmem, dmemkv_all, n_mem, "wgrad_mem")
                g_wmem = jnp.transpose(g_wmem.reshape(nq, d // nq, DEPTH, -1), (0, 2, 1, 3))
                gm = gm + scatter_start("0o", [("a_w_out", 0, rows_pieces(g_out)), ("w_mem_kv", None, g_wmem)])
            dx, g_norm_mix[i], h = _mm_nt_normbwd(dproj, w["w_first"], xs[i], gm, dxm, tm, "proj_a_bwd")
            g_in = _wgrad(h, dproj, 2 * tm, "wgrad_in_a")
            items.append(("a_w_in", i, cols_pieces(g_in)))
            if i > 0:
                items.append(("a_w_out", i, rows_pieces(g_out)))
        else:
            j = i - N_A
            dqp, cat, dkv_main[j], dkv_halo[j], dbias[j], dsk, dmemkv[i] = _mix_b_bwd(
                dxm, projs[i], kv, bias[j], sinkt[j], memkv, i, w["w_out"], tm, "mix_b_bwd")
            g_sinks[j] = dsk[0, 0:N_Q_HEADS].reshape(GROUP, N_KV_HEADS).T.reshape(N_Q_HEADS)
            g_out = _wgrad(cat, dxm, 2 * tm, "wgrad_out")
            dx, g_norm_mix[i], h = _mm_nt_normbwd(dqp, w["w_first"], xs[i], gm, dxm, tm, "proj_b_bwd")
            g_q = _wgrad(h, dqp, 2 * tm, "wgrad_in_b")
            g_q = jnp.concatenate([_heads_to_kv_major(g_q[:, :qw], 1), g_q[:, qw:]], axis=1)
            g_out = jnp.concatenate([_heads_to_kv_major(g_out[:qw, :], 0), g_out[qw:, :]], axis=0)
            items += [("b_w_q", j, rows_pieces(g_q)), ("b_w_out", j, rows_pieces(g_out))]
            if j == 0:
                dkv = _kv_assemble(dkv_main[0], dkv_halo[0], dkv_main[1], dkv_halo[1], tm, "kv_assemble")
                dx, g_kv_norm, hkv = _mm_nt_normbwd(dkv, wkv, xs[i], kv_norm.reshape(1, d), dx, tm, "proj_kv_bwd")
                items.append(("w_kv", 0, rows_pieces(_wgrad(hkv, dkv, 2 * tm, "wgrad_kv"))))
        tok = scatter_start(str(i) if i else "0i", items)
    grad_x = dx.reshape(x.shape)
    g_rel = _bias_bwd(dbias[0], dbias[1], "bias_bwd")[:, 0:N_Q_HEADS]

    small_shapes = [(DEPTH, d), (DEPTH, d), (d,), (d,), (d,), (2, N_Q_HEADS), (REL_BUCKETS, N_Q_HEADS),
                    (N_A, 3, cwid), ()]
    small = _pack([jnp.concatenate(g_norm_mix, axis=0), jnp.concatenate(g_norm_ffn, axis=0), g_kv_norm, g_mem_norm,
                   dg_final, jnp.stack(g_sinks), g_rel, jnp.stack(g_conv), loss_part[0, 0]])
    small_sum = _all_reduce_packed(small, "reduce_small")
    (gs_norm_mix, gs_norm_ffn, gs_kv_norm, gs_mem_norm, gs_final, gs_sinks, gs_rel, gs_conv_full, loss) = _unpack(
        small_sum, small_shapes)
    cq = cwid // N_CHIPS
    gs_conv = lax.dynamic_slice_in_dim(gs_conv_full, chip * cq, cq, axis=2)

    landed = _exchange_wait(scatters, [land[k] for k in names], scatter_ids, [small_sum], "scatter_wait")
    mine = [lax.dynamic_update_slice(ld, jnp.stack(own[k])[None], (chip,) + (0,) * (ld.ndim - 1))
            for k, ld in zip(names, landed)]
    theirs = _core_exchange(mine, "swap_cores")

    weights = dict(norm_mix=norm_mix, norm_ffn=norm_ffn, a_w_in=a_w_in, a_conv_w=a_conv_w, a_w_out=a_w_out,
                   kv_norm=kv_norm, w_kv=w_kv, b_w_q=b_w_q, b_sinks=b_sinks, b_w_out=b_w_out, rel_bias=rel_bias,
                   mem_norm=mem_norm, w_mem_kv=w_mem_kv, w_gate=w_gate, w_up=w_up, w_down=w_down,
                   final_norm=final_norm)
    moms = dict(norm_mix=m_norm_mix, norm_ffn=m_norm_ffn, a_w_in=m_a_w_in, a_conv_w=m_a_conv_w, a_w_out=m_a_w_out,
                kv_norm=m_kv_norm, w_kv=m_w_kv, b_w_q=m_b_w_q, b_sinks=m_b_sinks, b_w_out=m_b_w_out,
                rel_bias=m_rel_bias, mem_norm=m_mem_norm, w_mem_kv=m_w_mem_kv, w_gate=m_w_gate, w_up=m_w_up,
                w_down=m_w_down, final_norm=m_final_norm)
    vars_ = dict(norm_mix=v_norm_mix, norm_ffn=v_norm_ffn, a_w_in=v_a_w_in, a_conv_w=v_a_conv_w, a_w_out=v_a_w_out,
                 kv_norm=v_kv_norm, w_kv=v_w_kv, b_w_q=v_b_w_q, b_sinks=v_b_sinks, b_w_out=v_b_w_out,
                 rel_bias=v_rel_bias, mem_norm=v_mem_norm, w_mem_kv=v_w_mem_kv, w_gate=v_w_gate, w_up=v_w_up,
                 w_down=v_w_down, final_norm=v_final_norm)
    order = list(weights)
    grads, deltas, new_m, new_v = {}, {}, {}, {}
    for k, mi, th in zip(names, mine, theirs):
        shp = weights[k].shape
        g, dl, m2, v2 = _adamw_sharded(weights[k].reshape(mi.shape[1:]), mi, th, moms[k].reshape(mi.shape[1:]),
                                       vars_[k].reshape(mi.shape[1:]), "adamw_" + k)
        grads[k], deltas[k], new_m[k], new_v[k] = g.reshape(shp), dl.reshape(shp), m2.reshape(shp), v2.reshape(shp)
    small_names = ["norm_mix", "norm_ffn", "kv_norm", "mem_norm", "final_norm", "b_sinks", "rel_bias", "a_conv_w"]
    small_g = [gs_norm_mix, gs_norm_ffn, gs_kv_norm, gs_mem_norm, gs_final, gs_sinks, gs_rel, gs_conv]
    shapes = [weights[k].shape for k in small_names]
    dl_p, m_p, v_p = _adamw_packed(_pack([weights[k] for k in small_names]), _pack(small_g),
                                   _pack([moms[k] for k in small_names]), _pack([vars_[k] for k in small_names]),
                                   "adamw_small")
    for k, g, dl, m2, v2 in zip(small_names, small_g, _unpack(dl_p, shapes), _unpack(m_p, shapes), _unpack(v_p, shapes)):
        grads[k], deltas[k], new_m[k], new_v[k] = g.reshape(weights[k].shape), dl, m2, v2

    return (loss, grad_x, *[grads[k] for k in order], *[deltas[k] for k in order],
            *[new_m[k] for k in order], *[new_v[k] for k in order])
```

```python
import functools
import math

import numpy as np
import jax
import jax.numpy as jnp
from jax import lax
from jax.experimental import pallas as pl
from jax.experimental.pallas import tpu as pltpu

F32 = jnp.float32
BF16 = jnp.bfloat16
MESH = pl.DeviceIdType.MESH

EPS = 1e-5
HEAD_DIM = 64
N_MEM_HEADS = 4
N_KV_HEADS = 4
GROUP = 3
N_Q_HEADS = N_KV_HEADS * GROUP
BLOCK = 128
REL_BUCKETS = 32
REL_MAX_DIST = 128
SCALE = HEAD_DIM ** -0.5
NEG = -1e30
N_CHIPS = 4
N_A = 2
DEPTH = 4

ADAM_LR = 0.001
ADAM_B1 = 0.9
ADAM_B2 = 0.999
ADAM_EPS = 1e-08
ADAM_WD = 0.01
ADAM_STEP = 10

VMEM_BIG = 56 * 1024 * 1024
PACK_W = 1024

NT = (((1,), (1,)), ((), ()))
TN = (((0,), (0,)), ((), ()))


def _cp(sem=None, vmem=None, **kw):
    return pltpu.CompilerParams(dimension_semantics=sem, vmem_limit_bytes=vmem, **kw)


def _const_spec(shape):
    nd = len(shape)
    return pl.BlockSpec(shape, lambda i, _n=nd: (0,) * _n, pipeline_mode=pl.Buffered(1))


def _row_spec(tm, n):
    return pl.BlockSpec((tm, n), lambda i: (i, 0))


def _rms_parts(xv):
    r = lax.rsqrt(jnp.mean(xv * xv, axis=-1, keepdims=True) + EPS)
    return xv * r, r


def _sigmoid(z):
    return 1.0 / (1.0 + jnp.exp(-z))


def _ff_chunks(f):
    if f % 512 == 0 or f % 256 != 0:
        return [(0, f)] if f <= 1536 else [(0, f // 2), (f // 2, f - f // 2)]
    n = f // 256
    a = (n + 1) // 2 * 256
    return [(0, a), (a, f - a)]


def _norm_mm(x, g, w, tm, name):
    t, d = x.shape
    n = w.shape[1]

    def body(x_ref, g_ref, w_ref, o_ref):
        xhat, _ = _rms_parts(x_ref[...])
        h = (xhat * g_ref[...]).astype(BF16)
        o_ref[...] = jnp.dot(h, w_ref[...], preferred_element_type=F32).astype(BF16)

    return pl.pallas_call(
        body, name=name, grid=(t // tm,),
        in_specs=[_row_spec(tm, d), _const_spec((1, d)), _const_spec((d, n))],
        out_specs=_row_spec(tm, n),
        out_shape=jax.ShapeDtypeStruct((t, n), BF16),
        compiler_params=_cp(("parallel",), VMEM_BIG),
    )(x, g, w)


def _mm_nt_normbwd(dproj, w, x_in, g, dres, tm, name):
    t, d = x_in.shape
    n = w.shape[1]

    def body(dp_ref, w_ref, x_ref, g_ref, dr_ref, dx_ref, dg_ref, h_ref):
        i = pl.program_id(0)
        xhat, r = _rms_parts(x_ref[...])
        gv = g_ref[...]
        h_ref[...] = (xhat * gv).astype(BF16)
        dh = lax.dot_general(dp_ref[...], w_ref[...], NT, preferred_element_type=F32)
        dxhat = dh * gv
        dx = r * (dxhat - xhat * jnp.mean(dxhat * xhat, axis=-1, keepdims=True))
        dx_ref[...] = dr_ref[...] + dx

        @pl.when(i == 0)
        def _():
            dg_ref[...] = jnp.zeros_like(dg_ref)

        dg_ref[...] += jnp.sum(dh * xhat, axis=0, keepdims=True)

    return pl.pallas_call(
        body, name=name, grid=(t // tm,),
        in_specs=[_row_spec(tm, n), _const_spec((d, n)), _row_spec(tm, d), _const_spec((1, d)), _row_spec(tm, d)],
        out_specs=[_row_spec(tm, d), pl.BlockSpec((1, d), lambda i: (0, 0)), _row_spec(tm, d)],
        out_shape=[jax.ShapeDtypeStruct((t, d), F32), jax.ShapeDtypeStruct((1, d), F32),
                   jax.ShapeDtypeStruct((t, d), BF16)],
        compiler_params=_cp(("arbitrary",), VMEM_BIG),
    )(dproj, w, x_in, g, dres)


def _ffn_fwd(x, g, wg, wu, wd, tm, name):
    t, d = x.shape
    f = wg.shape[1]
    chunks = _ff_chunks(f)

    def body(x_ref, g_ref, wg_ref, wu_ref, wd_ref, xo_ref, gate_ref, up_ref):
        xv = x_ref[...]
        xhat, _ = _rms_parts(xv)
        h = (xhat * g_ref[...]).astype(BF16)
        acc = xv
        for c0, cw in chunks:
            gt = jnp.dot(h, wg_ref[:, c0:c0 + cw], preferred_element_type=F32)
            ut = jnp.dot(h, wu_ref[:, c0:c0 + cw], preferred_element_type=F32)
            gate_ref[:, c0:c0 + cw] = gt.astype(BF16)
            up_ref[:, c0:c0 + cw] = ut.astype(BF16)
            a = (gt * _sigmoid(gt) * ut).astype(BF16)
            acc = acc + jnp.dot(a, wd_ref[c0:c0 + cw, :], preferred_element_type=F32)
        xo_ref[...] = acc

    return pl.pallas_call(
        body, name=name, grid=(t // tm,),
        in_specs=[_row_spec(tm, d), _const_spec((1, d)), _const_spec((d, f)), _const_spec((d, f)), _const_spec((f, d))],
        out_specs=[_row_spec(tm, d), _row_spec(tm, f), _row_spec(tm, f)],
        out_shape=[jax.ShapeDtypeStruct((t, d), F32), jax.ShapeDtypeStruct((t, f), BF16),
                   jax.ShapeDtypeStruct((t, f), BF16)],
        compiler_params=_cp(("parallel",), VMEM_BIG),
    )(x, g, wg, wu, wd)


def _ffn_bwd(dxo, xm, g, gate, up, wg, wu, wd, tm, name):
    t, d = xm.shape
    f = wg.shape[1]
    chunks = _ff_chunks(f)

    def body(dxo_ref, xm_ref, g_ref, gate_ref, up_ref, wg_ref, wu_ref, wd_ref,
             dxm_ref, dgate_ref, dup_ref, act_ref, h2_ref, dg_ref):
        i = pl.program_id(0)
        dxo_v = dxo_ref[...]
        dxo_b = dxo_v.astype(BF16)
        xhat, r = _rms_parts(xm_ref[...])
        gv = g_ref[...]
        h2_ref[...] = (xhat * gv).astype(BF16)
        dh = jnp.zeros((tm, d), F32)
        for c0, cw in chunks:
            dact = lax.dot_general(dxo_b, wd_ref[c0:c0 + cw, :], NT, preferred_element_type=F32)
            gt = gate_ref[:, c0:c0 + cw].astype(F32)
            ut = up_ref[:, c0:c0 + cw].astype(F32)
            sg = _sigmoid(gt)
            sl = gt * sg
            act_ref[:, c0:c0 + cw] = (sl * ut).astype(BF16)
            dgt = (dact * ut * (sg * (1.0 + gt * (1.0 - sg)))).astype(BF16)
            dut = (dact * sl).astype(BF16)
            dgate_ref[:, c0:c0 + cw] = dgt
            dup_ref[:, c0:c0 + cw] = dut
            dh = dh + lax.dot_general(dgt, wg_ref[:, c0:c0 + cw], NT, preferred_element_type=F32)
            dh = dh + lax.dot_general(dut, wu_ref[:, c0:c0 + cw], NT, preferred_element_type=F32)
        dxhat = dh * gv
        dx = r * (dxhat - xhat * jnp.mean(dxhat * xhat, axis=-1, keepdims=True))
        dxm_ref[...] = dxo_v + dx

        @pl.when(i == 0)
        def _():
            dg_ref[...] = jnp.zeros_like(dg_ref)

        dg_ref[...] += jnp.sum(dh * xhat, axis=0, keepdims=True)

    return pl.pallas_call(
        body, name=name, grid=(t // tm,),
        in_specs=[_row_spec(tm, d), _row_spec(tm, d), _const_spec((1, d)), _row_spec(tm, f), _row_spec(tm, f),
                  _const_spec((d, f)), _const_spec((d, f)), _const_spec((f, d))],
        out_specs=[_row_spec(tm, d), _row_spec(tm, f), _row_spec(tm, f), _row_spec(tm, f), _row_spec(tm, d),
                   pl.BlockSpec((1, d), lambda i: (0, 0))],
        out_shape=[jax.ShapeDtypeStruct((t, d), F32), jax.ShapeDtypeStruct((t, f), BF16),
                   jax.ShapeDtypeStruct((t, f), BF16), jax.ShapeDtypeStruct((t, f), BF16),
                   jax.ShapeDtypeStruct((t, d), BF16), jax.ShapeDtypeStruct((1, d), F32)],
        compiler_params=_cp(("arbitrary",), VMEM_BIG),
    )(dxo, xm, g, gate, up, wg, wu, wd)


def _wgrad(a, b, tt, name):
    t, k = a.shape
    n = b.shape[1]
    nt = t // tt

    def body(a_ref, b_ref, o_ref, acc):
        i = pl.program_id(0)

        @pl.when(i == 0)
        def _():
            acc[...] = jnp.zeros_like(acc)

        acc[...] += lax.dot_general(a_ref[...].astype(BF16), b_ref[...].astype(BF16), TN,
                                    preferred_element_type=F32)

        @pl.when(i == nt - 1)
        def _():
            o_ref[...] = acc[...].astype(BF16)

    return pl.pallas_call(
        body, name=name, grid=(nt,),
        in_specs=[_row_spec(tt, k), _row_spec(tt, n)],
        out_specs=pl.BlockSpec((k, n), lambda i: (0, 0)),
        out_shape=jax.ShapeDtypeStruct((k, n), BF16),
        scratch_shapes=[pltpu.VMEM((k, n), F32)],
        compiler_params=_cp(("arbitrary",), VMEM_BIG),
    )(a, b)


def _final_loss(x, g, target, tm, name):
    t, d = x.shape

    def body(x_ref, g_ref, t_ref, loss_ref, dx_ref, dg_ref):
        i = pl.program_id(0)
        xhat, r = _rms_parts(x_ref[...])
        gv = g_ref[...]
        err = xhat * gv - t_ref[...]
        dy = err * (1.0 / d)
        dxhat = dy * gv
        dx_ref[...] = r * (dxhat - xhat * jnp.mean(dxhat * xhat, axis=-1, keepdims=True))

        @pl.when(i == 0)
        def _():
            dg_ref[...] = jnp.zeros_like(dg_ref)
            loss_ref[...] = jnp.zeros_like(loss_ref)

        dg_ref[...] += jnp.sum(dy * xhat, axis=0, keepdims=True)
        part = jnp.sum(jnp.sum(err * err, axis=-1, keepdims=True), axis=0, keepdims=True) * (0.5 / d)
        loss_ref[...] += jnp.broadcast_to(part, loss_ref.shape)

    return pl.pallas_call(
        body, name=name, grid=(t // tm,),
        in_specs=[_row_spec(tm, d), _const_spec((1, d)), _row_spec(tm, d)],
        out_specs=[pl.BlockSpec((8, 128), lambda i: (0, 0)), _row_spec(tm, d), pl.BlockSpec((1, d), lambda i: (0, 0))],
        out_shape=[jax.ShapeDtypeStruct((8, 128), F32), jax.ShapeDtypeStruct((t, d), F32),
                   jax.ShapeDtypeStruct((1, d), F32)],
        compiler_params=_cp(("arbitrary",)),
    )(x, g, target)


def _col_head(width):
    return lax.broadcasted_iota(jnp.int32, (1, width), 1) // HEAD_DIM


def _keep_head(a, colh, h):
    return jnp.where(colh == h, a, jnp.zeros_like(a))


def _softmax_cols(s, sink=None):
    m = jnp.max(s, axis=0, keepdims=True)
    if sink is not None:
        m = jnp.maximum(m, sink)
    p = jnp.exp(s - m)
    l = jnp.sum(p, axis=0, keepdims=True)
    if sink is None:
        return p * (1.0 / l), None
    es = jnp.exp(sink - m)
    inv = 1.0 / (l + es)
    return p * inv, es * inv


def _add4(v):
    return (v[0] + v[1]) + (v[2] + v[3])


def _mem_attn_fwd(qm, mk, mv):
    colh = _col_head(mk.shape[1])
    mks = mk * SCALE
    heads = range(N_MEM_HEADS)
    ss = [lax.dot_general(_keep_head(mks, colh, h), qm, NT, preferred_element_type=F32) for h in heads]
    ps = [_softmax_cols(s)[0].astype(BF16) for s in ss]
    return _add4([lax.dot_general(ps[h], _keep_head(mv, colh, h), TN, preferred_element_type=F32) for h in heads])


def _mem_attn_bwd(qm, dy_b, mk, mv):
    colh = _col_head(mk.shape[1])
    mks = mk * SCALE
    heads = range(N_MEM_HEADS)
    khs = [_keep_head(mks, colh, h) for h in heads]
    vhs = [_keep_head(mv, colh, h) for h in heads]
    ss = [lax.dot_general(khs[h], qm, NT, preferred_element_type=F32) for h in heads]
    dps = [lax.dot_general(vhs[h], dy_b, NT, preferred_element_type=F32) for h in heads]
    pbs, dsbs = [], []
    for h in heads:
        p, _ = _softmax_cols(ss[h])
        ds = p * (dps[h] - jnp.sum(p * dps[h], axis=0, keepdims=True))
        pbs.append(p.astype(BF16))
        dsbs.append(ds.astype(BF16))
    y = _add4([lax.dot_general(pbs[h], vhs[h], TN, preferred_element_type=F32) for h in heads])
    dq = _add4([lax.dot_general(dsbs[h], khs[h], TN, preferred_element_type=F32) for h in heads])
    dmk = _add4([jnp.where(colh == h, jnp.dot(dsbs[h], qm, preferred_element_type=F32) * SCALE, 0.0) for h in heads])
    dmv = _add4([jnp.where(colh == h, jnp.dot(pbs[h], dy_b, preferred_element_type=F32), 0.0) for h in heads])
    return y, dq, dmk, dmv


def _shift_down(v, halo, k):
    rolled = pltpu.roll(v, k, 0)
    hrolled = pltpu.roll(halo, k, 0)[0:8]
    rows = lax.broadcasted_iota(jnp.int32, (8, v.shape[1]), 0)
    first = jnp.where(rows < k, hrolled, rolled[0:8])
    return jnp.concatenate([first, rolled[8:]], axis=0)


def _shift_up(v, halo, k):
    n = v.shape[0]
    rolled = pltpu.roll(v, n - k, 0)
    hrolled = pltpu.roll(halo, 8 - k, 0)[0:8]
    rows = lax.broadcasted_iota(jnp.int32, (8, v.shape[1]), 0)
    last = jnp.where(rows >= 8 - k, hrolled, rolled[n - 8:])
    return jnp.concatenate([rolled[:n - 8], last], axis=0)


def _conv_parts(p, ph, cw, first_tile, cwid):
    u = p[:, 0:cwid].astype(F32)
    bg = p[:, cwid:2 * cwid].astype(F32)
    cg = p[:, 2 * cwid:3 * cwid].astype(F32)
    v = cg * u
    vh = ph[:, 2 * cwid:3 * cwid].astype(F32) * ph[:, 0:cwid].astype(F32)
    vh = jnp.where(first_tile, 0.0, vh)
    v1 = _shift_down(v, vh, 1)
    v2 = _shift_down(v, vh, 2)
    conv = cw[0:1, :] * v2 + cw[1:2, :] * v1 + cw[2:3, :] * v
    return u, bg, cg, v, v1, v2, conv


def _halo_prev_spec(rows, n, tm):
    per = tm // rows
    return pl.BlockSpec((rows, n), lambda i: (jnp.maximum(i * per - 1, 0), 0))


def _halo_next_spec(rows, n, tm, t):
    per = tm // rows
    last = t // rows - 1
    return pl.BlockSpec((rows, n), lambda i: (jnp.minimum((i + 1) * per, last), 0))


def _mix_a_fwd(x, proj, convw, memkv, layer, wout, tm, name):
    t, d = x.shape
    n_mem = memkv.shape[0]
    mw = N_MEM_HEADS * HEAD_DIM
    cwid = d - mw
    pw = proj.shape[1]

    def body(x_ref, p_ref, ph_ref, cw_ref, mkv_ref, wo_ref, xo_ref):
        i = pl.program_id(0)
        p = p_ref[...]
        _, bg, _, _, _, _, conv = _conv_parts(p, ph_ref[...], cw_ref[...], i == 0, cwid)
        ytok = (bg * conv).astype(BF16)
        mkv = mkv_ref[...]
        ymem = _mem_attn_fwd(p[:, 3 * cwid:3 * cwid + mw], mkv[:, 0:mw], mkv[:, mw:2 * mw])
        cat = jnp.concatenate([ytok, ymem.astype(BF16)], axis=1)
        xo_ref[...] = x_ref[...] + jnp.dot(cat, wo_ref[...], preferred_element_type=F32)

    return pl.pallas_call(
        body, name=name, grid=(t // tm,),
        in_specs=[_row_spec(tm, d), _row_spec(tm, pw), _halo_prev_spec(16, pw, tm), _const_spec((3, cwid)),
                  pl.BlockSpec((n_mem, 2 * mw), lambda i: (0, layer)), _const_spec((d, d))],
        out_specs=_row_spec(tm, d),
        out_shape=jax.ShapeDtypeStruct((t, d), F32),
        compiler_params=_cp(("parallel",), VMEM_BIG),
    )(x, proj, proj, convw, memkv, wout)


def _mix_a_bwd(dxm, proj, convw, memkv, layer, wout, tm, name):
    t, d = dxm.shape
    n_mem = memkv.shape[0]
    mw = N_MEM_HEADS * HEAD_DIM
    cwid = d - mw
    pw = proj.shape[1]
    nt = t // tm

    def body(dx_ref, dxn_ref, p_ref, ph_ref, pn_ref, cw_ref, mkv_ref, wo_ref,
             dp_ref, cat_ref, dcw_ref, dmkv_ref, dmk_acc, dmv_acc):
        i = pl.program_id(0)
        p = p_ref[...]
        cw = cw_ref[...]
        wo = wo_ref[...]
        u, bg, cg, v, v1, v2, conv = _conv_parts(p, ph_ref[...], cw, i == 0, cwid)
        dcat = lax.dot_general(dx_ref[...].astype(BF16), wo, NT, preferred_element_type=F32)
        dytok = dcat[:, 0:cwid]
        dymem_b = dcat[:, cwid:d].astype(BF16)
        pn = pn_ref[...]
        dcat_n = lax.dot_general(dxn_ref[...].astype(BF16), wo[0:cwid, :], NT, preferred_element_type=F32)
        dconv_n = jnp.where(i == nt - 1, 0.0, dcat_n * pn[:, cwid:2 * cwid].astype(F32))
        dbg = dytok * conv
        dconv = dytok * bg
        dv = cw[2:3, :] * dconv + cw[1:2, :] * _shift_up(dconv, dconv_n, 1) + cw[0:1, :] * _shift_up(dconv, dconv_n, 2)
        du = dv * cg
        dcg = dv * u
        rows8 = lax.broadcasted_iota(jnp.int32, (8, cwid), 0)
        dcw = (jnp.where(rows8 == 0, jnp.sum(dconv * v2, axis=0, keepdims=True), 0.0)
               + jnp.where(rows8 == 1, jnp.sum(dconv * v1, axis=0, keepdims=True), 0.0)
               + jnp.where(rows8 == 2, jnp.sum(dconv * v, axis=0, keepdims=True), 0.0))
        mkv = mkv_ref[...]
        qm = p[:, 3 * cwid:3 * cwid + mw]
        ymem, dqm, dmk, dmv = _mem_attn_bwd(qm, dymem_b, mkv[:, 0:mw], mkv[:, mw:2 * mw])
        cat_ref[...] = jnp.concatenate([(bg * conv).astype(BF16), ymem.astype(BF16)], axis=1)
        dp_ref[...] = jnp.concatenate([du.astype(BF16), dbg.astype(BF16), dcg.astype(BF16), dqm.astype(BF16)], axis=1)

        @pl.when(i == 0)
        def _():
            dcw_ref[...] = jnp.zeros_like(dcw_ref)
            dmk_acc[...] = jnp.zeros_like(dmk_acc)
            dmv_acc[...] = jnp.zeros_like(dmv_acc)

        dcw_ref[...] += dcw
        dmk_acc[...] += dmk
        dmv_acc[...] += dmv

        @pl.when(i == nt - 1)
        def _():
            dmkv_ref[...] = jnp.concatenate([dmk_acc[...], dmv_acc[...]], axis=1)

    return pl.pallas_call(
        body, name=name, grid=(nt,),
        in_specs=[_row_spec(tm, d), _halo_next_spec(16, d, tm, t), _row_spec(tm, pw), _halo_prev_spec(16, pw, tm),
                  _halo_next_spec(16, pw, tm, t), _const_spec((3, cwid)),
                  pl.BlockSpec((n_mem, 2 * mw), lambda i: (0, layer)), _const_spec((d, d))],
        out_specs=[_row_spec(tm, pw), _row_spec(tm, d), pl.BlockSpec((8, cwid), lambda i: (0, 0)),
                   pl.BlockSpec((n_mem, 2 * mw), lambda i: (0, 0))],
        out_shape=[jax.ShapeDtypeStruct((t, pw), BF16), jax.ShapeDtypeStruct((t, d), BF16),
                   jax.ShapeDtypeStruct((8, cwid), F32), jax.ShapeDtypeStruct((n_mem, 2 * mw), F32)],
        scratch_shapes=[pltpu.VMEM((n_mem, mw), F32), pltpu.VMEM((n_mem, mw), F32)],
        compiler_params=_cp(("arbitrary",), VMEM_BIG),
    )(dxm, dxm, proj, proj, proj, convw, memkv, wout)


def _rel_tables():
    qi = np.arange(BLOCK, dtype=np.int32)[:, None]
    kj = np.arange(2 * BLOCK, dtype=np.int32)[None, :]
    dist = qi + BLOCK - kj
    inw = (dist >= 0) & (dist < BLOCK)
    max_exact = REL_BUCKETS // 2
    dd = np.maximum(np.maximum(dist, 0), 1).astype(np.float32)
    large = max_exact + (np.log(dd / np.float32(max_exact)) / np.float32(math.log(REL_MAX_DIST / max_exact))
                         * np.float32(REL_BUCKETS - max_exact)).astype(np.int32)
    large = np.minimum(large, REL_BUCKETS - 1)
    bucket = np.where(np.maximum(dist, 0) < max_exact, np.maximum(dist, 0), large)
    return np.where(inw, bucket, -1).astype(np.int32)


def _bias_tables(rel_bias, sinks, name):
    bucket_t = jnp.asarray(_rel_tables().T)

    def body(rb_ref, sk_ref, bk_ref, bias_ref, sink_ref):
        bk = bk_ref[...]
        prev = lax.broadcasted_iota(jnp.int32, bk.shape, 0) < BLOCK
        for h in range(N_KV_HEADS):
            for j in range(GROUP):
                head = GROUP * h + j
                acc = jnp.full(bk.shape, NEG, F32)
                for b in range(REL_BUCKETS):
                    acc = jnp.where(bk == b, rb_ref[b, head], acc)
                bias_ref[h, :, j * BLOCK:(j + 1) * BLOCK] = acc
                bias_ref[N_KV_HEADS + h, :, j * BLOCK:(j + 1) * BLOCK] = jnp.where(prev, NEG, acc)
                sink_ref[h, :, j * BLOCK:(j + 1) * BLOCK] = jnp.full((8, BLOCK), sk_ref[0, head], F32)

    smem = pl.BlockSpec(memory_space=pltpu.SMEM)
    return pl.pallas_call(
        body, name=name,
        in_specs=[smem, smem, pl.BlockSpec(memory_space=pltpu.VMEM)],
        out_specs=[pl.BlockSpec(memory_space=pltpu.VMEM), pl.BlockSpec(memory_space=pltpu.VMEM)],
        out_shape=[jax.ShapeDtypeStruct((2 * N_KV_HEADS, 2 * BLOCK, GROUP * BLOCK), F32),
                   jax.ShapeDtypeStruct((N_KV_HEADS, 8, GROUP * BLOCK), F32)],
    )(rel_bias, sinks.reshape(1, N_Q_HEADS), bucket_t)


def _bias_bwd(dbias_a, dbias_b, name):
    bucket_t = jnp.asarray(_rel_tables().T)

    def body(da_ref, db_ref, bk_ref, o_ref):
        bk = bk_ref[...]
        ri = lax.broadcasted_iota(jnp.int32, (REL_BUCKETS, 128), 0)
        ci = lax.broadcasted_iota(jnp.int32, (REL_BUCKETS, 128), 1)
        out = jnp.zeros((REL_BUCKETS, 128), F32)
        for h in range(N_KV_HEADS):
            dsum = da_ref[h] + db_ref[h]
            for j in range(GROUP):
                head = GROUP * h + j
                seg = dsum[:, j * BLOCK:(j + 1) * BLOCK]
                for b in range(REL_BUCKETS):
                    val = jnp.sum(jnp.sum(jnp.where(bk == b, seg, 0.0), axis=0, keepdims=True), axis=1, keepdims=True)
                    out = out + jnp.where((ri == b) & (ci == head), val, 0.0)
        o_ref[...] = out

    vm = pl.BlockSpec(memory_space=pltpu.VMEM)
    return pl.pallas_call(
        body, name=name, in_specs=[vm, vm, vm], out_specs=vm,
        out_shape=jax.ShapeDtypeStruct((REL_BUCKETS, 128), F32),
    )(dbias_a, dbias_b, bucket_t)


def _stack_members(ref, r0, width):
    blk = ref[pl.ds(r0, BLOCK), 0:GROUP * width]
    return jnp.concatenate([blk[:, j * width:(j + 1) * width] for j in range(GROUP)], axis=0)


def _mix_b_fwd(x, qp, kv, bias, sinkt, memkv, layer, wout, tm, name):
    t, d = x.shape
    n_mem = memkv.shape[0]
    mw = N_MEM_HEADS * HEAD_DIM
    qw = d - mw
    kw = N_KV_HEADS * HEAD_DIM
    nb = tm // BLOCK
    rows = GROUP * BLOCK

    def body(x_ref, q_ref, kv_ref, kvh_ref, bias_ref, sink_ref, mkv_ref, wo_ref, xo_ref, kvx, ytok):
        i = pl.program_id(0)
        kvx[0:BLOCK, :] = kvh_ref[...]
        kvx[BLOCK:BLOCK + tm, :] = kv_ref[...]
        colh = _col_head(kw)

        def blk(b, carry):
            r0 = pl.multiple_of(b * BLOCK, BLOCK)
            win = kvx[pl.ds(r0, 2 * BLOCK), :]
            kwin = win[:, 0:kw] * SCALE
            vwin = win[:, kw:2 * kw]
            qs = _stack_members(q_ref, r0, kw)
            first = ((i == 0) & (b == 0)).astype(jnp.int32) * N_KV_HEADS
            heads = range(N_KV_HEADS)
            ss = [lax.dot_general(_keep_head(kwin, colh, h), qs, NT, preferred_element_type=F32) for h in heads]
            ps = [_softmax_cols(ss[h] + bias_ref[first + h], sink_ref[h][0:1, :])[0].astype(BF16) for h in heads]
            o = _add4([lax.dot_general(ps[h], _keep_head(vwin, colh, h), TN, preferred_element_type=F32)
                       for h in heads])
            for j in range(GROUP):
                ytok[pl.ds(r0, BLOCK), j * kw:(j + 1) * kw] = o[j * BLOCK:(j + 1) * BLOCK].astype(BF16)
            return carry

        for b_static in range(nb):
            blk(b_static, 0)
        mkv = mkv_ref[...]
        ymem = _mem_attn_fwd(q_ref[:, qw:d], mkv[:, 0:mw], mkv[:, mw:2 * mw])
        cat = jnp.concatenate([ytok[...], ymem.astype(BF16)], axis=1)
        xo_ref[...] = x_ref[...] + jnp.dot(cat, wo_ref[...], preferred_element_type=F32)

    return pl.pallas_call(
        body, name=name, grid=(t // tm,),
        in_specs=[_row_spec(tm, d), _row_spec(tm, d), _row_spec(tm, 2 * kw), _halo_prev_spec(BLOCK, 2 * kw, tm),
                  _const_spec((2 * N_KV_HEADS, 2 * BLOCK, rows)), _const_spec((N_KV_HEADS, 8, rows)),
                  pl.BlockSpec((n_mem, 2 * mw), lambda i: (0, layer)), _const_spec((d, d))],
        out_specs=_row_spec(tm, d),
        out_shape=jax.ShapeDtypeStruct((t, d), F32),
        scratch_shapes=[pltpu.VMEM((tm + BLOCK, 2 * kw), BF16), pltpu.VMEM((tm, qw), BF16)],
        compiler_params=_cp(("parallel",), VMEM_BIG),
    )(x, qp, kv, kv, bias, sinkt, memkv, wout)


def _mix_b_bwd(dxm, qp, kv, bias, sinkt, memkv, layer, wout, tm, name):
    t, d = dxm.shape
    n_mem = memkv.shape[0]
    mw = N_MEM_HEADS * HEAD_DIM
    qw = d - mw
    kw = N_KV_HEADS * HEAD_DIM
    nb = tm // BLOCK
    nt = t // tm
    rows = GROUP * BLOCK

    def body(dx_ref, q_ref, kv_ref, kvh_ref, bias_ref, sink_ref, mkv_ref, wo_ref,
             dq_ref, cat_ref, dkv_ref, dkvh_ref, dbias_ref, dsink_ref, dmkv_ref,
             kvx, dkvx, dcat_s, dmk_acc, dmv_acc):
        i = pl.program_id(0)

        @pl.when(i == 0)
        def _():
            dbias_ref[...] = jnp.zeros_like(dbias_ref)
            dsink_ref[...] = jnp.zeros_like(dsink_ref)
            dmk_acc[...] = jnp.zeros_like(dmk_acc)
            dmv_acc[...] = jnp.zeros_like(dmv_acc)

        kvx[0:BLOCK, :] = kvh_ref[...]
        kvx[BLOCK:BLOCK + tm, :] = kv_ref[...]
        dkvx[...] = jnp.zeros_like(dkvx)
        dcat_s[...] = lax.dot_general(dx_ref[...].astype(BF16), wo_ref[...], NT,
                                      preferred_element_type=F32).astype(BF16)
        colh = _col_head(kw)
        lane8 = lax.broadcasted_iota(jnp.int32, (8, 128), 1)

        def blk(b, carry):
            r0 = pl.multiple_of(b * BLOCK, BLOCK)
            win = kvx[pl.ds(r0, 2 * BLOCK), :]
            kwin = win[:, 0:kw] * SCALE
            vwin = win[:, kw:2 * kw]
            qs = _stack_members(q_ref, r0, kw)
            dos = _stack_members(dcat_s, r0, kw)
            first = ((i == 0) & (b == 0)).astype(jnp.int32) * N_KV_HEADS
            heads = range(N_KV_HEADS)
            khs = [_keep_head(kwin, colh, h) for h in heads]
            vhs = [_keep_head(vwin, colh, h) for h in heads]
            ss = [lax.dot_general(khs[h], qs, NT, preferred_element_type=F32) for h in heads]
            dps = [lax.dot_general(vhs[h], dos, NT, preferred_element_type=F32) for h in heads]
            dsink = jnp.zeros((8, 128), F32)
            pbs, dsbs = [], []
            for h in heads:
                p, sinkp = _softmax_cols(ss[h] + bias_ref[first + h], sink_ref[h][0:1, :])
                delta = jnp.sum(p * dps[h], axis=0, keepdims=True)
                ds = p * (dps[h] - delta)
                dbias_ref[h] += ds
                sd = sinkp * delta
                for j in range(GROUP):
                    val = -jnp.sum(sd[:, j * BLOCK:(j + 1) * BLOCK], axis=1, keepdims=True)
                    dsink = dsink + jnp.where(lane8 == 4 * j + h, val, 0.0)
                pbs.append(p.astype(BF16))
                dsbs.append(ds.astype(BF16))
            y = _add4([lax.dot_general(pbs[h], vhs[h], TN, preferred_element_type=F32) for h in heads])
            dq = _add4([lax.dot_general(dsbs[h], khs[h], TN, preferred_element_type=F32) for h in heads])
            dk = _add4([jnp.where(colh == h, jnp.dot(dsbs[h], qs, preferred_element_type=F32) * SCALE, 0.0)
                        for h in heads])
            dv = _add4([jnp.where(colh == h, jnp.dot(pbs[h], dos, preferred_element_type=F32), 0.0) for h in heads])
            for j in range(GROUP):
                cat_ref[pl.ds(r0, BLOCK), j * kw:(j + 1) * kw] = y[j * BLOCK:(j + 1) * BLOCK].astype(BF16)
                dq_ref[pl.ds(r0, BLOCK), j * kw:(j + 1) * kw] = dq[j * BLOCK:(j + 1) * BLOCK].astype(BF16)
            dsink_ref[...] += dsink
            dkvx[pl.ds(r0, 2 * BLOCK), :] += jnp.concatenate([dk, dv], axis=1)
            return carry

        for b_static in range(nb):
            blk(b_static, 0)
        dkvh_ref[0] = dkvx[0:BLOCK, :]
        dkv_ref[...] = dkvx[BLOCK:BLOCK + tm, :]

        mkv = mkv_ref[...]
        ymem, dqm, dmk, dmv = _mem_attn_bwd(q_ref[:, qw:d], dcat_s[:, qw:d], mkv[:, 0:mw], mkv[:, mw:2 * mw])
        cat_ref[:, qw:d] = ymem.astype(BF16)
        dq_ref[:, qw:d] = dqm.astype(BF16)
        dmk_acc[...] += dmk
        dmv_acc[...] += dmv

        @pl.when(i == nt - 1)
        def _():
            dmkv_ref[...] = jnp.concatenate([dmk_acc[...], dmv_acc[...]], axis=1)

    return pl.pallas_call(
        body, name=name, grid=(nt,),
        in_specs=[_row_spec(tm, d), _row_spec(tm, d), _row_spec(tm, 2 * kw), _halo_prev_spec(BLOCK, 2 * kw, tm),
                  _const_spec((2 * N_KV_HEADS, 2 * BLOCK, rows)), _const_spec((N_KV_HEADS, 8, rows)),
                  pl.BlockSpec((n_mem, 2 * mw), lambda i: (0, layer)), _const_spec((d, d))],
        out_specs=[_row_spec(tm, d), _row_spec(tm, d), _row_spec(tm, 2 * kw),
                   pl.BlockSpec((1, BLOCK, 2 * kw), lambda i: (i, 0, 0)),
                   pl.BlockSpec((N_KV_HEADS, 2 * BLOCK, rows), lambda i: (0, 0, 0)),
                   pl.BlockSpec((8, 128), lambda i: (0, 0)),
                   pl.BlockSpec((n_mem, 2 * mw), lambda i: (0, 0))],
        out_shape=[jax.ShapeDtypeStruct((t, d), BF16), jax.ShapeDtypeStruct((t, d), BF16),
                   jax.ShapeDtypeStruct((t, 2 * kw), F32), jax.ShapeDtypeStruct((nt, BLOCK, 2 * kw), F32),
                   jax.ShapeDtypeStruct((N_KV_HEADS, 2 * BLOCK, rows), F32), jax.ShapeDtypeStruct((8, 128), F32),
                   jax.ShapeDtypeStruct((n_mem, 2 * mw), F32)],
        scratch_shapes=[pltpu.VMEM((tm + BLOCK, 2 * kw), BF16), pltpu.VMEM((tm + BLOCK, 2 * kw), F32),
                        pltpu.VMEM((tm, d), BF16), pltpu.VMEM((n_mem, mw), F32), pltpu.VMEM((n_mem, mw), F32)],
        compiler_params=_cp(("arbitrary",), VMEM_BIG),
    )(dxm, qp, kv, kv, bias, sinkt, memkv, wout)


def _kv_assemble(main_a, halo_a, main_b, halo_b, tm, name):
    t, n = main_a.shape
    nt = t // tm

    def body(ma_ref, ha_ref, mb_ref, hb_ref, o_ref):
        i = pl.program_id(0)
        s = ma_ref[...] + mb_ref[...]
        tail = jnp.where(i == nt - 1, 0.0, ha_ref[0] + hb_ref[0])
        o_ref[...] = jnp.concatenate([s[0:tm - BLOCK], s[tm - BLOCK:] + tail], axis=0).astype(BF16)

    halo_spec = pl.BlockSpec((1, BLOCK, n), lambda i: (jnp.minimum(i + 1, nt - 1), 0, 0))
    return pl.pallas_call(
        body, name=name, grid=(nt,),
        in_specs=[_row_spec(tm, n), halo_spec, _row_spec(tm, n), halo_spec],
        out_specs=_row_spec(tm, n),
        out_shape=jax.ShapeDtypeStruct((t, n), BF16),
        compiler_params=_cp(("parallel",)),
    )(main_a, halo_a, main_b, halo_b)


def _adam_math(w, g, m, v):
    m2 = ADAM_B1 * m + (1.0 - ADAM_B1) * g
    v2 = ADAM_B2 * v + (1.0 - ADAM_B2) * (g * g)
    m_hat = m2 / (1.0 - ADAM_B1 ** ADAM_STEP)
    v_hat = v2 / (1.0 - ADAM_B2 ** ADAM_STEP)
    delta = -ADAM_LR * (m_hat / (jnp.sqrt(v_hat) + ADAM_EPS) + ADAM_WD * w)
    return delta, m2, v2


def _adamw_sharded(w, land, m, v, name):
    nl, r, c = w.shape
    tr = r
    for cand in (512, 256, 128, 64, 32, 16):
        if r % cand == 0:
            tr = cand
            break

    def body(w_ref, a_ref, m_ref, v_ref, g_ref, d_ref, mo_ref, vo_ref):
        g = a_ref[0, 0].astype(F32) + a_ref[1, 0].astype(F32)
        for k in range(1, N_CHIPS):
            g = g + (a_ref[2 * k, 0].astype(F32) + a_ref[2 * k + 1, 0].astype(F32))
        delta, m2, v2 = _adam_math(w_ref[0], g, m_ref[0], v_ref[0])
        g_ref[0] = g
        d_ref[0] = delta
        mo_ref[0] = m2
        vo_ref[0] = v2

    rs = pl.BlockSpec((1, tr, c), lambda l, i: (l, i, 0))
    ps = pl.BlockSpec((2 * N_CHIPS, 1, tr, c), lambda l, i: (0, l, i, 0))
    sd = jax.ShapeDtypeStruct((nl, r, c), F32)
    return pl.pallas_call(
        body, name=name, grid=(nl, r // tr),
        in_specs=[rs, ps, rs, rs], out_specs=[rs, rs, rs, rs], out_shape=[sd, sd, sd, sd],
        compiler_params=_cp(("parallel", "parallel")),
    )(w, land, m, v)


def _adamw_packed(w, g, m, v, name):
    def body(w_ref, g_ref, m_ref, v_ref, d_ref, mo_ref, vo_ref):
        delta, m2, v2 = _adam_math(w_ref[...], g_ref[...], m_ref[...], v_ref[...])
        d_ref[...] = delta
        mo_ref[...] = m2
        vo_ref[...] = v2

    vm = pl.BlockSpec(memory_space=pltpu.VMEM)
    sd = jax.ShapeDtypeStruct(w.shape, F32)
    return pl.pallas_call(body, name=name, in_specs=[vm] * 4, out_specs=[vm] * 3, out_shape=[sd] * 3)(w, g, m, v)


def _place():
    return lax.axis_index("x"), lax.axis_index("y"), lax.axis_index("c")


def _hbm(a):
    return pltpu.with_memory_space_constraint(a, pltpu.HBM)


def _peers(x, y, c, both_cores):
    chips = [(1 - x, y), (x, 1 - y), (1 - x, 1 - y)]
    if not both_cores:
        return [(px, py, c) for px, py in chips]
    return [(px, py, pc) for px, py in chips for pc in (c, 1 - c)] + [(x, y, 1 - c)]


def _chip_copy(src, land, gather, layer, chip_src, slot, send_sem, recv_sem, peer):
    s = src if gather else src.at[chip_src]
    d = land.at[slot] if layer is None else land.at[slot, layer]
    return pltpu.make_async_remote_copy(src_ref=s, dst_ref=d, send_sem=send_sem, recv_sem=recv_sem,
                                        device_id=peer, device_id_type=MESH)


def _exchange_start(srcs, lands, gather, layers, both_cores, after, name):
    n = len(srcs)
    npeer = 7 if both_cores else 3
    hbm = pl.BlockSpec(memory_space=pltpu.HBM)
    sem = pl.BlockSpec(memory_space=pltpu.SEMAPHORE)

    def body(*refs):
        ins, lds = refs[:n], refs[n:2 * n]
        first_out = 2 * n + len(after)
        send_sems, recv_sems, token = refs[first_out], refs[first_out + 1], refs[-1]
        x, y, c = _place()
        slot = 2 * x + y if gather else 2 * (2 * x + y) + c
        for t in range(n):
            for r, peer in enumerate(_peers(x, y, c, both_cores)):
                _chip_copy(ins[t], lds[t], gather, layers[t], 2 * peer[0] + peer[1], slot,
                           send_sems.at[npeer * t + r], recv_sems.at[npeer * t + r], peer).start()
        token[...] = jnp.zeros_like(token)

    both = list(srcs) + list(lands)
    outs = pl.pallas_call(
        body, name=name, in_specs=[hbm] * (2 * n) + [pl.BlockSpec(memory_space=pl.ANY)] * len(after),
        out_specs=(sem, sem, *([hbm] * (2 * n)), pl.BlockSpec(memory_space=pltpu.VMEM)),
        out_shape=(pltpu.SemaphoreType.DMA((npeer * n,)), pltpu.SemaphoreType.DMA((npeer * n,)),
                   *[pltpu.HBM(a.shape, a.dtype) for a in both], jax.ShapeDtypeStruct((8, 128), F32)),
        input_output_aliases={t: 2 + t for t in range(2 * n)},
        compiler_params=_cp(has_side_effects=pltpu.SideEffectType.DATAFLOW_SIDE_EFFECTING),
    )(*[_hbm(a) for a in both], *after)
    return dict(send=outs[0], recv=outs[1], srcs=list(outs[2:2 + n]), lands=list(outs[2 + n:2 + 2 * n]),
                token=outs[-1], gather=gather, layers=list(layers), both_cores=both_cores)


def _exchange_wait(groups, lands, land_ids, after, name):
    flat = [s for g in groups for s in g["srcs"]]
    ns, nl, ng, na = len(flat), len(lands), len(groups), len(after)
    hbm = pl.BlockSpec(memory_space=pltpu.HBM)
    sem = pl.BlockSpec(memory_space=pltpu.SEMAPHORE)

    def body(*refs):
        srcs, lds = refs[:ns], refs[ns:ns + nl]
        sems = refs[ns + nl:ns + nl + 2 * ng]
        x, y, c = _place()
        k = 0
        for gi, g in enumerate(groups):
            peers = _peers(x, y, c, g["both_cores"])
            for t in range(len(g["srcs"])):
                for r, peer in enumerate(peers):
                    cp = _chip_copy(srcs[k], lds[land_ids[gi][t]], g["gather"], g["layers"][t], 0, 0,
                                    sems[2 * gi].at[len(peers) * t + r], sems[2 * gi + 1].at[len(peers) * t + r], peer)
                    cp.wait_send()
                    cp.wait_recv()
                k += 1

    both = flat + list(lands)
    sem_args = [a for g in groups for a in (g["send"], g["recv"])]
    outs = pl.pallas_call(
        body, name=name,
        in_specs=[hbm] * (ns + nl) + [sem] * (2 * ng) + [pl.BlockSpec(memory_space=pl.ANY)] * na,
        out_specs=[hbm] * (ns + nl),
        out_shape=[pltpu.HBM(a.shape, a.dtype) for a in both],
        input_output_aliases={t: t for t in range(ns + nl)},
        compiler_params=_cp(has_side_effects=pltpu.SideEffectType.DATAFLOW_SIDE_EFFECTING),
    )(*both, *sem_args, *after)
    return list(outs[ns:])


def _core_fill(lands, layers, name):
    n = len(lands)
    hbm = pl.BlockSpec(memory_space=pltpu.HBM)

    def body(*refs):
        ins = refs[:n]
        send_sems, recv_sems = refs[2 * n:]
        x, y, c = _place()
        copies = []
        for t in range(n):
            for k in range(N_CHIPS):
                mine = ins[t].at[2 * k + c] if layers[t] is None else ins[t].at[2 * k + c, layers[t]]
                cp = pltpu.make_async_remote_copy(
                    src_ref=mine, dst_ref=mine, send_sem=send_sems.at[N_CHIPS * t + k],
                    recv_sem=recv_sems.at[N_CHIPS * t + k], device_id=(x, y, 1 - c), device_id_type=MESH)
                cp.start()
                copies.append(cp)
        for cp in copies:
            cp.wait()

    return pl.pallas_call(
        body, name=name, in_specs=[hbm] * n, out_specs=[hbm] * n,
        out_shape=[jax.ShapeDtypeStruct(a.shape, a.dtype) for a in lands],
        input_output_aliases={t: t for t in range(n)},
        scratch_shapes=[pltpu.SemaphoreType.DMA((N_CHIPS * n,)), pltpu.SemaphoreType.DMA((N_CHIPS * n,))],
        compiler_params=_cp(has_side_effects=True),
    )(*lands)


def _all_reduce_packed(pack, name):
    r, c = pack.shape
    vm = pl.BlockSpec(memory_space=pltpu.VMEM)

    def body(p_ref, sum_ref, slots, send_sems, recv_sems):
        x, y, cc = _place()
        me = 4 * x + 2 * y + cc
        slots[me] = p_ref[...]
        copies = []
        for rel in range(1, 8):
            px = 1 - x if rel & 4 else x
            py = 1 - y if rel & 2 else y
            pc = 1 - cc if rel & 1 else cc
            cp = pltpu.make_async_remote_copy(
                src_ref=p_ref, dst_ref=slots.at[me], send_sem=send_sems.at[rel - 1], recv_sem=recv_sems.at[rel - 1],
                device_id=(px, py, pc), device_id_type=MESH)
            cp.start()
            copies.append(cp)
        for cp in copies:
            cp.wait()
        total = slots[0]
        for k in range(1, 8):
            total = total + slots[k]
        sum_ref[...] = total

    return pl.pallas_call(
        body, name=name, in_specs=[vm], out_specs=vm, out_shape=jax.ShapeDtypeStruct((r, c), F32),
        scratch_shapes=[pltpu.VMEM((8, r, c), F32), pltpu.SemaphoreType.DMA((7,)), pltpu.SemaphoreType.DMA((7,))],
        compiler_params=_cp(has_side_effects=True),
    )(pack)


def _pack(items):
    rows = []
    for a in items:
        flat = a.astype(F32).reshape(-1)
        pad = (-flat.shape[0]) % PACK_W
        rows.append(jnp.pad(flat, (0, pad)).reshape(-1, PACK_W))
    out = jnp.concatenate(rows, axis=0)
    pad_r = (-out.shape[0]) % 8
    return jnp.pad(out, ((0, pad_r), (0, 0)))


def _unpack(pack, shapes):
    outs, row = [], 0
    for s in shapes:
        n = int(np.prod(s))
        nr = -(-n // PACK_W)
        outs.append(pack[row:row + nr].reshape(-1)[:n].reshape(s))
        row += nr
    return outs


def _heads_to_member_major(w, axis):
    shp = w.shape
    pre, post = shp[:axis], shp[axis + 1:]
    w4 = w.reshape(pre + (N_KV_HEADS, GROUP, HEAD_DIM) + post)
    w4 = jnp.swapaxes(w4, len(pre), len(pre) + 1)
    return w4.reshape(shp)


def _heads_to_kv_major(w, axis):
    shp = w.shape
    pre, post = shp[:axis], shp[axis + 1:]
    w4 = w.reshape(pre + (GROUP, N_KV_HEADS, HEAD_DIM) + post)
    w4 = jnp.swapaxes(w4, len(pre), len(pre) + 1)
    return w4.reshape(shp)


def kernel(x, mem, norm_mix, norm_ffn, a_w_in, a_conv_w, a_w_out, kv_norm, w_kv, b_w_q, b_sinks, b_w_out, rel_bias, mem_norm, w_mem_kv, w_gate, w_up, w_down, final_norm, loss_target, m_norm_mix, m_norm_ffn, m_a_w_in, m_a_conv_w, m_a_w_out, m_kv_norm, m_w_kv, m_b_w_q, m_b_sinks, m_b_w_out, m_rel_bias, m_mem_norm, m_w_mem_kv, m_w_gate, m_w_up, m_w_down, m_final_norm, v_norm_mix, v_norm_ffn, v_a_w_in, v_a_conv_w, v_a_w_out, v_kv_norm, v_w_kv, v_b_w_q, v_b_sinks, v_b_w_out, v_rel_bias, v_mem_norm, v_w_mem_kv, v_w_gate, v_w_up, v_w_down, v_final_norm):
    t, d = x.shape[1], x.shape[2]
    tm = 512 if t % 512 == 0 and t >= 2048 else 256
    x0 = x.reshape(t, d)
    target = loss_target.reshape(t, d)
    mem2 = mem.reshape(mem.shape[1], d)
    n_mem = mem2.shape[0]
    ax, ay, ac = _place()
    chip = 2 * ax + ay
    cwid = a_conv_w.shape[2] * N_CHIPS
    qw = N_Q_HEADS * HEAD_DIM
    nq = N_CHIPS

    def own_slot(piece):
        return lax.dynamic_update_slice(lax.empty((nq,) + piece.shape, piece.dtype), piece[None],
                                        (chip,) + (0,) * piece.ndim)

    def mixer_shards(i):
        if i < N_A:
            shards = [a_w_in[i], a_w_out[i]] + ([w_mem_kv] if i == 0 else [])
        else:
            j = i - N_A
            shards = [b_w_q[j], b_w_out[j]] + ([w_kv] if j == 0 else [])
        return [a.astype(BF16) for a in shards]

    def ffn_shards(i):
        return [w_gate[i].astype(BF16), w_up[i].astype(BF16), w_down[i].astype(BF16)]

    conv_pad = jnp.pad(a_conv_w, ((0, 0), (0, 8 - a_conv_w.shape[1]), (0, (-a_conv_w.shape[2]) % 128)))
    first = mixer_shards(0)
    group_shards = {"0a": first[0:1], "0b": first[1:] + [conv_pad], "0f": ffn_shards(0)}
    for i in range(1, DEPTH):
        group_shards[str(i)] = ffn_shards(i) + mixer_shards(i)
    gathers, prev_tok = {}, []
    for key, shards in group_shards.items():
        gathers[key] = _exchange_start(shards, [own_slot(a) for a in shards], True, [None] * len(shards), False,
                                       prev_tok, "gather_start_" + key)
        prev_tok = [gathers[key]["token"]]

    def rows_full(g):
        return g.reshape((-1,) + g.shape[2:])

    def cols_full(g):
        return jnp.transpose(g, (1, 0, 2)).reshape(g.shape[1], -1)

    def landed_weights(key, after):
        g = gathers[key]
        return _exchange_wait([g], g["lands"], [list(range(len(g["lands"])))], after, "gather_wait_" + key)

    def mixer_weights(i, got):
        w_first, w_out = (cols_full(got[0]) if i < N_A else rows_full(got[0])), rows_full(got[1])
        if i >= N_A:
            w_first = jnp.concatenate([_heads_to_member_major(w_first[:, :qw], 1), w_first[:, qw:]], axis=1)
            w_out = jnp.concatenate([_heads_to_member_major(w_out[:qw, :], 0), w_out[qw:, :]], axis=0)
        return dict(w_first=w_first, w_out=w_out, extra=got[2] if len(got) > 2 else None)

    def ffn_weights(got):
        return dict(wg=cols_full(got[0]), wu=cols_full(got[1]), wd=rows_full(got[2]))

    bias, sinkt = [], []
    for j in range(2):
        bj, sj = _bias_tables(rel_bias, b_sinks[j], "bias_tables")
        bias.append(bj)
        sinkt.append(sj)

    ws = []
    xs, xmids, projs, gates, ups = [x0], [], [], [], []
    kv = memkv = wmem = wkv = None
    for i in range(DEPTH):
        xin = xs[-1]
        if i == 0:
            w = dict(w_first=cols_full(landed_weights("0a", prev_tok)[0]))
        else:
            got = landed_weights(str(i), [xin])
            w = dict(mixer_weights(i, got[3:]), **ffn_weights(got[0:3]))
        ws.append(w)
        gm = norm_mix[i].reshape(1, d)
        if i < N_A:
            proj = _norm_mm(xin, gm, w["w_first"], tm, "proj_a")
            if i == 0:
                got = landed_weights("0b", [proj])
                w["w_out"] = rows_full(got[0])
                full_mem = jnp.swapaxes(got[1], 0, 1).reshape(DEPTH, d, -1)
                wmem = jnp.transpose(full_mem, (1, 0, 2)).reshape(d, -1)
                memkv = _norm_mm(mem2, mem_norm.reshape(1, d), wmem, n_mem, "mem_kv")
                taps = got[2][:, :, 0:3, 0:a_conv_w.shape[2]]
                conv_full = jnp.transpose(taps, (1, 2, 0, 3)).reshape(N_A, 3, cwid)
            xmid = _mix_a_fwd(xin, proj, conv_full[i], memkv, i, w["w_out"], tm, "mix_a_fwd")
        else:
            j = i - N_A
            if j == 0:
                wkv = rows_full(w["extra"])
                kv = _norm_mm(xin, kv_norm.reshape(1, d), wkv, tm, "proj_kv")
            proj = _norm_mm(xin, gm, w["w_first"], tm, "proj_b")
            xmid = _mix_b_fwd(xin, proj, kv, bias[j], sinkt[j], memkv, i, w["w_out"], tm, "mix_b_fwd")
        if i == 0:
            w.update(ffn_weights(landed_weights("0f", [xmid])))
        xout, gate, up = _ffn_fwd(xmid, norm_ffn[i].reshape(1, d), w["wg"], w["wu"], w["wd"], tm, "ffn_fwd")
        projs.append(proj)
        xmids.append(xmid)
        gates.append(gate)
        ups.append(up)
        xs.append(xout)

    loss_part, dx, dg_final = _final_loss(xs[-1], final_norm.reshape(1, d), target, tm, "final_loss")

    def rows_pieces(g):
        return g.astype(BF16).reshape((nq, g.shape[0] // nq) + g.shape[1:])

    def cols_pieces(g):
        return jnp.transpose(g.astype(BF16).reshape(g.shape[0], nq, g.shape[1] // nq), (1, 0, 2))

    stacked = dict(a_w_in=a_w_in, a_w_out=a_w_out, w_kv=w_kv[None], b_w_q=b_w_q, b_w_out=b_w_out,
                   w_mem_kv=w_mem_kv, w_gate=w_gate, w_up=w_up, w_down=w_down)
    names = list(stacked)
    land = {k: lax.empty((2 * nq,) + stacked[k].shape, BF16) for k in names}
    own = {k: [None] * stacked[k].shape[0] for k in names}
    scatters, scatter_ids = [], []

    def scatter_start(key, items, both_cores):
        keys = [k for k, _, _ in items]
        st = _exchange_start([p for _, _, p in items], [land[k] for k in keys], False, [l for _, l, _ in items],
                             both_cores, [], "scatter_start_" + key)
        for (k, l, p), ld in zip(items, st["lands"]):
            land[k] = ld
            mine_piece = lax.dynamic_index_in_dim(p, chip, 0, keepdims=False)
            if l is None:
                own[k] = [mine_piece[q] for q in range(mine_piece.shape[0])]
            else:
                own[k][l] = mine_piece
        scatters.append(st)
        scatter_ids.append([names.index(k) for k in keys])
        return st["token"][0:1, 0:1]

    g_norm_mix, g_norm_ffn = [None] * DEPTH, [None] * DEPTH
    g_conv, g_sinks = [None] * 2, [None] * 2
    dmemkv = [None] * DEPTH
    dbias, dkv_main, dkv_halo = [None] * 2, [None] * 2, [None] * 2
    g_kv_norm = None
    tok = jnp.zeros((1, 1), F32)
    for i in reversed(range(DEPTH)):
        w = ws[i]
        dxm, dgate, dup, act, h2, dgf = _ffn_bwd(dx, xmids[i], norm_ffn[i].reshape(1, d) + tok, gates[i], ups[i],
                                            w["wg"], w["wu"], w["wd"], tm // 2, "ffn_bwd")
        g_norm_ffn[i] = dgf
        g_wd = _wgrad(act, dx, 2 * tm, "wgrad_down")
        g_wg = _wgrad(h2, dgate, 2 * tm, "wgrad_gate")
        g_wu = _wgrad(h2, dup, 2 * tm, "wgrad_up")
        items = [("w_gate", i, cols_pieces(g_wg)), ("w_up", i, cols_pieces(g_wu)), ("w_down", i, rows_pieces(g_wd))]
        if i == 0:
            tok = scatter_start("0f", items, True)
            items = []
        gm = norm_mix[i].reshape(1, d)
        if i < N_A:
            dproj, cat, dcw, dmemkv[i] = _mix_a_bwd(dxm, projs[i], conv_full[i] + (tok if i == 0 else 0.0), memkv, i,
                                                    w["w_out"], tm, "mix_a_bwd")
            g_conv[i] = dcw[0:3]
            g_out = _wgrad(cat, dxm, 2 * tm, "wgrad_out")
            if i == 0:
                dmemkv_all = jnp.concatenate([a.astype(BF16) for a in dmemkv], axis=1)
                _, g_mem_norm, hmem = _mm_nt_normbwd(dmemkv_all, wmem, mem2, mem_norm.reshape(1, d),
                                                     jnp.zeros((n_mem, d), F32), n_mem, "mem_kv_bwd")
                g_wmem = _wgrad(hmem, dmemkv_all, n_mem, "wgrad_mem")
                g_wmem = jnp.transpose(g_wmem.reshape(nq, d // nq, DEPTH, -1), (0, 2, 1, 3))
                gm = gm + scatter_start("0o", [("a_w_out", 0, rows_pieces(g_out)), ("w_mem_kv", None, g_wmem)], False)
            dx, g_norm_mix[i], h = _mm_nt_normbwd(dproj, w["w_first"], xs[i], gm, dxm, tm, "proj_a_bwd")
            g_in = _wgrad(h, dproj, 2 * tm, "wgrad_in_a")
            items.append(("a_w_in", i, cols_pieces(g_in)))
            if i > 0:
                items.append(("a_w_out", i, rows_pieces(g_out)))
        else:
            j = i - N_A
            dqp, cat, dkv_main[j], dkv_halo[j], dbias[j], dsk, dmemkv[i] = _mix_b_bwd(
                dxm, projs[i], kv, bias[j], sinkt[j], memkv, i, w["w_out"], tm, "mix_b_bwd")
            g_sinks[j] = dsk[0, 0:N_Q_HEADS].reshape(GROUP, N_KV_HEADS).T.reshape(N_Q_HEADS)
            g_out = _wgrad(cat, dxm, 2 * tm, "wgrad_out")
            dx, g_norm_mix[i], h = _mm_nt_normbwd(dqp, w["w_first"], xs[i], gm, dxm, tm, "proj_b_bwd")
            g_q = _wgrad(h, dqp, 2 * tm, "wgrad_in_b")
            g_q = jnp.concatenate([_heads_to_kv_major(g_q[:, :qw], 1), g_q[:, qw:]], axis=1)
            g_out = jnp.concatenate([_heads_to_kv_major(g_out[:qw, :], 0), g_out[qw:, :]], axis=0)
            items += [("b_w_q", j, rows_pieces(g_q)), ("b_w_out", j, rows_pieces(g_out))]
            if j == 0:
                dkv = _kv_assemble(dkv_main[0], dkv_halo[0], dkv_main[1], dkv_halo[1], tm, "kv_assemble")
                dx, g_kv_norm, hkv = _mm_nt_normbwd(dkv, wkv, xs[i], kv_norm.reshape(1, d), dx, tm, "proj_kv_bwd")
                items.append(("w_kv", 0, rows_pieces(_wgrad(hkv, dkv, 2 * tm, "wgrad_kv"))))
        tok = scatter_start(str(i) if i else "0i", items, i > 0)
    grad_x = dx.reshape(x.shape)
    g_rel = _bias_bwd(dbias[0], dbias[1], "bias_bwd")[:, 0:N_Q_HEADS]

    small_shapes = [(DEPTH, d), (DEPTH, d), (d,), (d,), (d,), (2, N_Q_HEADS), (REL_BUCKETS, N_Q_HEADS),
                    (N_A, 3, cwid), ()]
    small = _pack([jnp.concatenate(g_norm_mix, axis=0), jnp.concatenate(g_norm_ffn, axis=0), g_kv_norm, g_mem_norm,
                   dg_final, jnp.stack(g_sinks), g_rel, jnp.stack(g_conv), loss_part[0, 0]])
    small_sum = _all_reduce_packed(small, "reduce_small")
    (gs_norm_mix, gs_norm_ffn, gs_kv_norm, gs_mem_norm, gs_final, gs_sinks, gs_rel, gs_conv_full, loss) = _unpack(
        small_sum, small_shapes)
    cq = cwid // N_CHIPS
    gs_conv = lax.dynamic_slice_in_dim(gs_conv_full, chip * cq, cq, axis=2)

    late = ("a_w_in", "a_w_out", "w_mem_kv")
    landed = dict(zip(names, _exchange_wait(scatters[:-2], [land[k] for k in names], scatter_ids[:-2], [small_sum],
                                            "scatter_wait_a")))

    def with_own(k, ld):
        return lax.dynamic_update_slice(ld, jnp.stack(own[k])[None], (2 * chip + ac,) + (0,) * (ld.ndim - 1))

    weights = dict(norm_mix=norm_mix, norm_ffn=norm_ffn, a_w_in=a_w_in, a_conv_w=a_conv_w, a_w_out=a_w_out,
                   kv_norm=kv_norm, w_kv=w_kv, b_w_q=b_w_q, b_sinks=b_sinks, b_w_out=b_w_out, rel_bias=rel_bias,
                   mem_norm=mem_norm, w_mem_kv=w_mem_kv, w_gate=w_gate, w_up=w_up, w_down=w_down,
                   final_norm=final_norm)
    moms = dict(norm_mix=m_norm_mix, norm_ffn=m_norm_ffn, a_w_in=m_a_w_in, a_conv_w=m_a_conv_w, a_w_out=m_a_w_out,
                kv_norm=m_kv_norm, w_kv=m_w_kv, b_w_q=m_b_w_q, b_sinks=m_b_sinks, b_w_out=m_b_w_out,
                rel_bias=m_rel_bias, mem_norm=m_mem_norm, w_mem_kv=m_w_mem_kv, w_gate=m_w_gate, w_up=m_w_up,
                w_down=m_w_down, final_norm=m_final_norm)
    vars_ = dict(norm_mix=v_norm_mix, norm_ffn=v_norm_ffn, a_w_in=v_a_w_in, a_conv_w=v_a_conv_w, a_w_out=v_a_w_out,
                 kv_norm=v_kv_norm, w_kv=v_w_kv, b_w_q=v_b_w_q, b_sinks=v_b_sinks, b_w_out=v_b_w_out,
                 rel_bias=v_rel_bias, mem_norm=v_mem_norm, w_mem_kv=v_w_mem_kv, w_gate=v_w_gate, w_up=v_w_up,
                 w_down=v_w_down, final_norm=v_final_norm)
    order = list(weights)
    grads, deltas, new_m, new_v = {}, {}, {}, {}
    def adamw(k, ld):
        shp, stk = weights[k].shape, ld.shape[1:]
        g, dl, m2, v2 = _adamw_sharded(weights[k].reshape(stk), ld, moms[k].reshape(stk), vars_[k].reshape(stk),
                                       "adamw_" + k)
        grads[k], deltas[k], new_m[k], new_v[k] = g.reshape(shp), dl.reshape(shp), m2.reshape(shp), v2.reshape(shp)

    for k in names:
        if k not in late:
            adamw(k, with_own(k, landed[k]))
    late_ids = [[late.index(names[t]) for t in ids] for ids in scatter_ids[-2:]]
    late_landed = _exchange_wait(scatters[-2:], [landed[k] for k in late], late_ids, [deltas["w_down"]], "scatter_wait_b")
    filled = _core_fill([with_own(k, ld) for k, ld in zip(late, late_landed)], [0, 0, None], "fill_cores")
    for k, ld in zip(late, filled):
        adamw(k, ld)
    small_names = ["norm_mix", "norm_ffn", "kv_norm", "mem_norm", "final_norm", "b_sinks", "rel_bias", "a_conv_w"]
    small_g = [gs_norm_mix, gs_norm_ffn, gs_kv_norm, gs_mem_norm, gs_final, gs_sinks, gs_rel, gs_conv]
    shapes = [weights[k].shape for k in small_names]
    dl_p, m_p, v_p = _adamw_packed(_pack([weights[k] for k in small_names]), _pack(small_g),
                                   _pack([moms[k] for k in small_names]), _pack([vars_[k] for k in small_names]),
                                   "adamw_small")
    for k, g, dl, m2, v2 in zip(small_names, small_g, _unpack(dl_p, shapes), _unpack(m_p, shapes), _unpack(v_p, shapes)):
        grads[k], deltas[k], new_m[k], new_v[k] = g.reshape(weights[k].shape), dl, m2, v2

    return (loss, grad_x, *[grads[k] for k in order], *[deltas[k] for k in order],
            *[new_m[k] for k in order], *[new_v[k] for k in order])
```

```python
import functools
import math

import numpy as np
import jax
import jax.numpy as jnp
from jax import lax
from jax.experimental import pallas as pl
from jax.experimental.pallas import tpu as pltpu

F32 = jnp.float32
BF16 = jnp.bfloat16
MESH = pl.DeviceIdType.MESH

EPS = 1e-5
HEAD_DIM = 64
N_MEM_HEADS = 4
N_KV_HEADS = 4
GROUP = 3
N_Q_HEADS = N_KV_HEADS * GROUP
BLOCK = 128
REL_BUCKETS = 32
REL_MAX_DIST = 128
SCALE = HEAD_DIM ** -0.5
NEG = -1e30
N_CHIPS = 4
N_A = 2
DEPTH = 4

ADAM_LR = 0.001
ADAM_B1 = 0.9
ADAM_B2 = 0.999
ADAM_EPS = 1e-08
ADAM_WD = 0.01
ADAM_STEP = 10

VMEM_BIG = 56 * 1024 * 1024
PACK_W = 1024

NT = (((1,), (1,)), ((), ()))
TN = (((0,), (0,)), ((), ()))


def _cp(sem=None, vmem=None, **kw):
    return pltpu.CompilerParams(dimension_semantics=sem, vmem_limit_bytes=vmem, **kw)


def _const_spec(shape):
    nd = len(shape)
    return pl.BlockSpec(shape, lambda i, _n=nd: (0,) * _n, pipeline_mode=pl.Buffered(1))


def _row_spec(tm, n):
    return pl.BlockSpec((tm, n), lambda i: (i, 0))


def _rms_parts(xv):
    r = lax.rsqrt(jnp.mean(xv * xv, axis=-1, keepdims=True) + EPS)
    return xv * r, r


def _sigmoid(z):
    return 1.0 / (1.0 + jnp.exp(-z))


def _ff_chunks(f):
    if f % 512 == 0 or f % 256 != 0:
        return [(0, f)] if f <= 1536 else [(0, f // 2), (f // 2, f - f // 2)]
    n = f // 256
    a = (n + 1) // 2 * 256
    return [(0, a), (a, f - a)]


def _norm_mm(x, g, w, tm, name):
    t, d = x.shape
    n = w.shape[1]

    def body(x_ref, g_ref, w_ref, o_ref):
        xhat, _ = _rms_parts(x_ref[...])
        h = (xhat * g_ref[...]).astype(BF16)
        o_ref[...] = jnp.dot(h, w_ref[...], preferred_element_type=F32).astype(BF16)

    return pl.pallas_call(
        body, name=name, grid=(t // tm,),
        in_specs=[_row_spec(tm, d), _const_spec((1, d)), _const_spec((d, n))],
        out_specs=_row_spec(tm, n),
        out_shape=jax.ShapeDtypeStruct((t, n), BF16),
        compiler_params=_cp(("parallel",), VMEM_BIG),
    )(x, g, w)


def _mm_nt_normbwd(dproj, w, x_in, g, dres, tm, name):
    t, d = x_in.shape
    n = w.shape[1]

    def body(dp_ref, w_ref, x_ref, g_ref, dr_ref, dx_ref, dg_ref, h_ref):
        i = pl.program_id(0)
        xhat, r = _rms_parts(x_ref[...])
        gv = g_ref[...]
        h_ref[...] = (xhat * gv).astype(BF16)
        dh = lax.dot_general(dp_ref[...], w_ref[...], NT, preferred_element_type=F32)
        dxhat = dh * gv
        dx = r * (dxhat - xhat * jnp.mean(dxhat * xhat, axis=-1, keepdims=True))
        dx_ref[...] = dr_ref[...] + dx

        @pl.when(i == 0)
        def _():
            dg_ref[...] = jnp.zeros_like(dg_ref)

        dg_ref[...] += jnp.sum(dh * xhat, axis=0, keepdims=True)

    return pl.pallas_call(
        body, name=name, grid=(t // tm,),
        in_specs=[_row_spec(tm, n), _const_spec((d, n)), _row_spec(tm, d), _const_spec((1, d)), _row_spec(tm, d)],
        out_specs=[_row_spec(tm, d), pl.BlockSpec((1, d), lambda i: (0, 0)), _row_spec(tm, d)],
        out_shape=[jax.ShapeDtypeStruct((t, d), F32), jax.ShapeDtypeStruct((1, d), F32),
                   jax.ShapeDtypeStruct((t, d), BF16)],
        compiler_params=_cp(("arbitrary",), VMEM_BIG),
    )(dproj, w, x_in, g, dres)


def _ffn_fwd(x, g, wg, wu, wd, tm, name):
    t, d = x.shape
    f = wg.shape[0]
    chunks = _ff_chunks(f)

    def body(x_ref, g_ref, wg_ref, wu_ref, wd_ref, xo_ref, gate_ref, up_ref):
        xv = x_ref[...]
        xhat, _ = _rms_parts(xv)
        h = (xhat * g_ref[...]).astype(BF16)
        acc = xv
        for c0, cw in chunks:
            gt = lax.dot_general(h, wg_ref[c0:c0 + cw, :], NT, preferred_element_type=F32)
            ut = lax.dot_general(h, wu_ref[c0:c0 + cw, :], NT, preferred_element_type=F32)
            gate_ref[:, c0:c0 + cw] = gt.astype(BF16)
            up_ref[:, c0:c0 + cw] = ut.astype(BF16)
            a = (gt * _sigmoid(gt) * ut).astype(BF16)
            acc = acc + jnp.dot(a, wd_ref[c0:c0 + cw, :], preferred_element_type=F32)
        xo_ref[...] = acc

    return pl.pallas_call(
        body, name=name, grid=(t // tm,),
        in_specs=[_row_spec(tm, d), _const_spec((1, d)), _const_spec((f, d)), _const_spec((f, d)), _const_spec((f, d))],
        out_specs=[_row_spec(tm, d), _row_spec(tm, f), _row_spec(tm, f)],
        out_shape=[jax.ShapeDtypeStruct((t, d), F32), jax.ShapeDtypeStruct((t, f), BF16),
                   jax.ShapeDtypeStruct((t, f), BF16)],
        compiler_params=_cp(("parallel",), VMEM_BIG),
    )(x, g, wg, wu, wd)


def _ffn_bwd(dxo, xm, g, gate, up, wg, wu, wd, tm, name):
    t, d = xm.shape
    f = wg.shape[0]
    chunks = _ff_chunks(f)

    def body(dxo_ref, xm_ref, g_ref, gate_ref, up_ref, wg_ref, wu_ref, wd_ref,
             dxm_ref, dgate_ref, dup_ref, act_ref, h2_ref, dg_ref):
        i = pl.program_id(0)
        dxo_v = dxo_ref[...]
        dxo_b = dxo_v.astype(BF16)
        xhat, r = _rms_parts(xm_ref[...])
        gv = g_ref[...]
        h2_ref[...] = (xhat * gv).astype(BF16)
        dh = jnp.zeros((tm, d), F32)
        for c0, cw in chunks:
            dact = lax.dot_general(dxo_b, wd_ref[c0:c0 + cw, :], NT, preferred_element_type=F32)
            gt = gate_ref[:, c0:c0 + cw].astype(F32)
            ut = up_ref[:, c0:c0 + cw].astype(F32)
            sg = _sigmoid(gt)
            sl = gt * sg
            act_ref[:, c0:c0 + cw] = (sl * ut).astype(BF16)
            dgt = (dact * ut * (sg * (1.0 + gt * (1.0 - sg)))).astype(BF16)
            dut = (dact * sl).astype(BF16)
            dgate_ref[:, c0:c0 + cw] = dgt
            dup_ref[:, c0:c0 + cw] = dut
            dh = dh + jnp.dot(dgt, wg_ref[c0:c0 + cw, :], preferred_element_type=F32)
            dh = dh + jnp.dot(dut, wu_ref[c0:c0 + cw, :], preferred_element_type=F32)
        dxhat = dh * gv
        dx = r * (dxhat - xhat * jnp.mean(dxhat * xhat, axis=-1, keepdims=True))
        dxm_ref[...] = dxo_v + dx

        @pl.when(i == 0)
        def _():
            dg_ref[...] = jnp.zeros_like(dg_ref)

        dg_ref[...] += jnp.sum(dh * xhat, axis=0, keepdims=True)

    return pl.pallas_call(
        body, name=name, grid=(t // tm,),
        in_specs=[_row_spec(tm, d), _row_spec(tm, d), _const_spec((1, d)), _row_spec(tm, f), _row_spec(tm, f),
                  _const_spec((f, d)), _const_spec((f, d)), _const_spec((f, d))],
        out_specs=[_row_spec(tm, d), _row_spec(tm, f), _row_spec(tm, f), _row_spec(tm, f), _row_spec(tm, d),
                   pl.BlockSpec((1, d), lambda i: (0, 0))],
        out_shape=[jax.ShapeDtypeStruct((t, d), F32), jax.ShapeDtypeStruct((t, f), BF16),
                   jax.ShapeDtypeStruct((t, f), BF16), jax.ShapeDtypeStruct((t, f), BF16),
                   jax.ShapeDtypeStruct((t, d), BF16), jax.ShapeDtypeStruct((1, d), F32)],
        compiler_params=_cp(("arbitrary",), VMEM_BIG),
    )(dxo, xm, g, gate, up, wg, wu, wd)


def _wgrad(a, b, tt, name):
    t, k = a.shape
    n = b.shape[1]
    nt = t // tt

    def body(a_ref, b_ref, o_ref, acc):
        i = pl.program_id(0)

        @pl.when(i == 0)
        def _():
            acc[...] = jnp.zeros_like(acc)

        acc[...] += lax.dot_general(a_ref[...].astype(BF16), b_ref[...].astype(BF16), TN,
                                    preferred_element_type=F32)

        @pl.when(i == nt - 1)
        def _():
            o_ref[...] = acc[...].astype(BF16)

    return pl.pallas_call(
        body, name=name, grid=(nt,),
        in_specs=[_row_spec(tt, k), _row_spec(tt, n)],
        out_specs=pl.BlockSpec((k, n), lambda i: (0, 0)),
        out_shape=jax.ShapeDtypeStruct((k, n), BF16),
        scratch_shapes=[pltpu.VMEM((k, n), F32)],
        compiler_params=_cp(("arbitrary",), VMEM_BIG),
    )(a, b)


def _final_loss(x, g, target, tm, name):
    t, d = x.shape

    def body(x_ref, g_ref, t_ref, loss_ref, dx_ref, dg_ref):
        i = pl.program_id(0)
        xhat, r = _rms_parts(x_ref[...])
        gv = g_ref[...]
        err = xhat * gv - t_ref[...]
        dy = err * (1.0 / d)
        dxhat = dy * gv
        dx_ref[...] = r * (dxhat - xhat * jnp.mean(dxhat * xhat, axis=-1, keepdims=True))

        @pl.when(i == 0)
        def _():
            dg_ref[...] = jnp.zeros_like(dg_ref)
            loss_ref[...] = jnp.zeros_like(loss_ref)

        dg_ref[...] += jnp.sum(dy * xhat, axis=0, keepdims=True)
        part = jnp.sum(jnp.sum(err * err, axis=-1, keepdims=True), axis=0, keepdims=True) * (0.5 / d)
        loss_ref[...] += jnp.broadcast_to(part, loss_ref.shape)

    return pl.pallas_call(
        body, name=name, grid=(t // tm,),
        in_specs=[_row_spec(tm, d), _const_spec((1, d)), _row_spec(tm, d)],
        out_specs=[pl.BlockSpec((8, 128), lambda i: (0, 0)), _row_spec(tm, d), pl.BlockSpec((1, d), lambda i: (0, 0))],
        out_shape=[jax.ShapeDtypeStruct((8, 128), F32), jax.ShapeDtypeStruct((t, d), F32),
                   jax.ShapeDtypeStruct((1, d), F32)],
        compiler_params=_cp(("arbitrary",)),
    )(x, g, target)


def _col_head(width):
    return lax.broadcasted_iota(jnp.int32, (1, width), 1) // HEAD_DIM


def _keep_head(a, colh, h):
    return jnp.where(colh == h, a, jnp.zeros_like(a))


def _softmax_cols(s, sink=None):
    m = jnp.max(s, axis=0, keepdims=True)
    if sink is not None:
        m = jnp.maximum(m, sink)
    p = jnp.exp(s - m)
    l = jnp.sum(p, axis=0, keepdims=True)
    if sink is None:
        return p * (1.0 / l), None
    es = jnp.exp(sink - m)
    inv = 1.0 / (l + es)
    return p * inv, es * inv


def _add4(v):
    return (v[0] + v[1]) + (v[2] + v[3])


def _mem_attn_fwd(qm, mk, mv):
    colh = _col_head(mk.shape[1])
    mks = mk * SCALE
    heads = range(N_MEM_HEADS)
    ss = [lax.dot_general(_keep_head(mks, colh, h), qm, NT, preferred_element_type=F32) for h in heads]
    ps = [_softmax_cols(s)[0].astype(BF16) for s in ss]
    return _add4([lax.dot_general(ps[h], _keep_head(mv, colh, h), TN, preferred_element_type=F32) for h in heads])


def _mem_attn_bwd(qm, dy_b, mk, mv):
    colh = _col_head(mk.shape[1])
    mks = mk * SCALE
    heads = range(N_MEM_HEADS)
    khs = [_keep_head(mks, colh, h) for h in heads]
    vhs = [_keep_head(mv, colh, h) for h in heads]
    ss = [lax.dot_general(khs[h], qm, NT, preferred_element_type=F32) for h in heads]
    dps = [lax.dot_general(vhs[h], dy_b, NT, preferred_element_type=F32) for h in heads]
    pbs, dsbs = [], []
    for h in heads:
        p, _ = _softmax_cols(ss[h])
        ds = p * (dps[h] - jnp.sum(p * dps[h], axis=0, keepdims=True))
        pbs.append(p.astype(BF16))
        dsbs.append(ds.astype(BF16))
    y = _add4([lax.dot_general(pbs[h], vhs[h], TN, preferred_element_type=F32) for h in heads])
    dq = _add4([lax.dot_general(dsbs[h], khs[h], TN, preferred_element_type=F32) for h in heads])
    dmk = _add4([jnp.where(colh == h, jnp.dot(dsbs[h], qm, preferred_element_type=F32) * SCALE, 0.0) for h in heads])
    dmv = _add4([jnp.where(colh == h, jnp.dot(pbs[h], dy_b, preferred_element_type=F32), 0.0) for h in heads])
    return y, dq, dmk, dmv


def _shift_down(v, halo, k):
    rolled = pltpu.roll(v, k, 0)
    hrolled = pltpu.roll(halo, k, 0)[0:8]
    rows = lax.broadcasted_iota(jnp.int32, (8, v.shape[1]), 0)
    first = jnp.where(rows < k, hrolled, rolled[0:8])
    return jnp.concatenate([first, rolled[8:]], axis=0)


def _shift_up(v, halo, k):
    n = v.shape[0]
    rolled = pltpu.roll(v, n - k, 0)
    hrolled = pltpu.roll(halo, 8 - k, 0)[0:8]
    rows = lax.broadcasted_iota(jnp.int32, (8, v.shape[1]), 0)
    last = jnp.where(rows >= 8 - k, hrolled, rolled[n - 8:])
    return jnp.concatenate([rolled[:n - 8], last], axis=0)


def _conv_parts(p, ph, cw, first_tile, cwid):
    u = p[:, 0:cwid].astype(F32)
    bg = p[:, cwid:2 * cwid].astype(F32)
    cg = p[:, 2 * cwid:3 * cwid].astype(F32)
    v = cg * u
    vh = ph[:, 2 * cwid:3 * cwid].astype(F32) * ph[:, 0:cwid].astype(F32)
    vh = jnp.where(first_tile, 0.0, vh)
    v1 = _shift_down(v, vh, 1)
    v2 = _shift_down(v, vh, 2)
    conv = cw[0:1, :] * v2 + cw[1:2, :] * v1 + cw[2:3, :] * v
    return u, bg, cg, v, v1, v2, conv


def _halo_prev_spec(rows, n, tm):
    per = tm // rows
    return pl.BlockSpec((rows, n), lambda i: (jnp.maximum(i * per - 1, 0), 0))


def _halo_next_spec(rows, n, tm, t):
    per = tm // rows
    last = t // rows - 1
    return pl.BlockSpec((rows, n), lambda i: (jnp.minimum((i + 1) * per, last), 0))


def _mix_a_fwd(x, proj, convw, memkv, layer, wout, tm, name):
    t, d = x.shape
    n_mem = memkv.shape[0]
    mw = N_MEM_HEADS * HEAD_DIM
    cwid = d - mw
    pw = proj.shape[1]

    def body(x_ref, p_ref, ph_ref, cw_ref, mkv_ref, wo_ref, xo_ref):
        i = pl.program_id(0)
        p = p_ref[...]
        _, bg, _, _, _, _, conv = _conv_parts(p, ph_ref[...], cw_ref[...], i == 0, cwid)
        ytok = (bg * conv).astype(BF16)
        mkv = mkv_ref[...]
        ymem = _mem_attn_fwd(p[:, 3 * cwid:3 * cwid + mw], mkv[:, 0:mw], mkv[:, mw:2 * mw])
        cat = jnp.concatenate([ytok, ymem.astype(BF16)], axis=1)
        xo_ref[...] = x_ref[...] + jnp.dot(cat, wo_ref[...], preferred_element_type=F32)

    return pl.pallas_call(
        body, name=name, grid=(t // tm,),
        in_specs=[_row_spec(tm, d), _row_spec(tm, pw), _halo_prev_spec(16, pw, tm), _const_spec((3, cwid)),
                  pl.BlockSpec((n_mem, 2 * mw), lambda i: (0, layer)), _const_spec((d, d))],
        out_specs=_row_spec(tm, d),
        out_shape=jax.ShapeDtypeStruct((t, d), F32),
        compiler_params=_cp(("parallel",), VMEM_BIG),
    )(x, proj, proj, convw, memkv, wout)


def _mix_a_bwd(dxm, proj, convw, memkv, layer, wout, tm, name):
    t, d = dxm.shape
    n_mem = memkv.shape[0]
    mw = N_MEM_HEADS * HEAD_DIM
    cwid = d - mw
    pw = proj.shape[1]
    nt = t // tm

    def body(dx_ref, dxn_ref, p_ref, ph_ref, pn_ref, cw_ref, mkv_ref, wo_ref,
             dp_ref, cat_ref, dcw_ref, dmkv_ref, dmk_acc, dmv_acc):
        i = pl.program_id(0)
        p = p_ref[...]
        cw = cw_ref[...]
        wo = wo_ref[...]
        u, bg, cg, v, v1, v2, conv = _conv_parts(p, ph_ref[...], cw, i == 0, cwid)
        dcat = lax.dot_general(dx_ref[...].astype(BF16), wo, NT, preferred_element_type=F32)
        dytok = dcat[:, 0:cwid]
        dymem_b = dcat[:, cwid:d].astype(BF16)
        pn = pn_ref[...]
        dcat_n = lax.dot_general(dxn_ref[...].astype(BF16), wo[0:cwid, :], NT, preferred_element_type=F32)
        dconv_n = jnp.where(i == nt - 1, 0.0, dcat_n * pn[:, cwid:2 * cwid].astype(F32))
        dbg = dytok * conv
        dconv = dytok * bg
        dv = cw[2:3, :] * dconv + cw[1:2, :] * _shift_up(dconv, dconv_n, 1) + cw[0:1, :] * _shift_up(dconv, dconv_n, 2)
        du = dv * cg
        dcg = dv * u
        rows8 = lax.broadcasted_iota(jnp.int32, (8, cwid), 0)
        dcw = (jnp.where(rows8 == 0, jnp.sum(dconv * v2, axis=0, keepdims=True), 0.0)
               + jnp.where(rows8 == 1, jnp.sum(dconv * v1, axis=0, keepdims=True), 0.0)
               + jnp.where(rows8 == 2, jnp.sum(dconv * v, axis=0, keepdims=True), 0.0))
        mkv = mkv_ref[...]
        qm = p[:, 3 * cwid:3 * cwid + mw]
        ymem, dqm, dmk, dmv = _mem_attn_bwd(qm, dymem_b, mkv[:, 0:mw], mkv[:, mw:2 * mw])
        cat_ref[...] = jnp.concatenate([(bg * conv).astype(BF16), ymem.astype(BF16)], axis=1)
        dp_ref[...] = jnp.concatenate([du.astype(BF16), dbg.astype(BF16), dcg.astype(BF16), dqm.astype(BF16)], axis=1)

        @pl.when(i == 0)
        def _():
            dcw_ref[...] = jnp.zeros_like(dcw_ref)
            dmk_acc[...] = jnp.zeros_like(dmk_acc)
            dmv_acc[...] = jnp.zeros_like(dmv_acc)

        dcw_ref[...] += dcw
        dmk_acc[...] += dmk
        dmv_acc[...] += dmv

        @pl.when(i == nt - 1)
        def _():
            dmkv_ref[...] = jnp.concatenate([dmk_acc[...], dmv_acc[...]], axis=1)

    return pl.pallas_call(
        body, name=name, grid=(nt,),
        in_specs=[_row_spec(tm, d), _halo_next_spec(16, d, tm, t), _row_spec(tm, pw), _halo_prev_spec(16, pw, tm),
                  _halo_next_spec(16, pw, tm, t), _const_spec((3, cwid)),
                  pl.BlockSpec((n_mem, 2 * mw), lambda i: (0, layer)), _const_spec((d, d))],
        out_specs=[_row_spec(tm, pw), _row_spec(tm, d), pl.BlockSpec((8, cwid), lambda i: (0, 0)),
                   pl.BlockSpec((n_mem, 2 * mw), lambda i: (0, 0))],
        out_shape=[jax.ShapeDtypeStruct((t, pw), BF16), jax.ShapeDtypeStruct((t, d), BF16),
                   jax.ShapeDtypeStruct((8, cwid), F32), jax.ShapeDtypeStruct((n_mem, 2 * mw), F32)],
        scratch_shapes=[pltpu.VMEM((n_mem, mw), F32), pltpu.VMEM((n_mem, mw), F32)],
        compiler_params=_cp(("arbitrary",), VMEM_BIG),
    )(dxm, dxm, proj, proj, proj, convw, memkv, wout)


def _rel_tables():
    qi = np.arange(BLOCK, dtype=np.int32)[:, None]
    kj = np.arange(2 * BLOCK, dtype=np.int32)[None, :]
    dist = qi + BLOCK - kj
    inw = (dist >= 0) & (dist < BLOCK)
    max_exact = REL_BUCKETS // 2
    dd = np.maximum(np.maximum(dist, 0), 1).astype(np.float32)
    large = max_exact + (np.log(dd / np.float32(max_exact)) / np.float32(math.log(REL_MAX_DIST / max_exact))
                         * np.float32(REL_BUCKETS - max_exact)).astype(np.int32)
    large = np.minimum(large, REL_BUCKETS - 1)
    bucket = np.where(np.maximum(dist, 0) < max_exact, np.maximum(dist, 0), large)
    return np.where(inw, bucket, -1).astype(np.int32)


def _bias_tables(rel_bias, sinks, name):
    bucket_t = jnp.asarray(_rel_tables().T)

    def body(rb_ref, sk_ref, bk_ref, bias_ref, sink_ref):
        bk = bk_ref[...]
        prev = lax.broadcasted_iota(jnp.int32, bk.shape, 0) < BLOCK
        for h in range(N_KV_HEADS):
            for j in range(GROUP):
                head = GROUP * h + j
                acc = jnp.full(bk.shape, NEG, F32)
                for b in range(REL_BUCKETS):
                    acc = jnp.where(bk == b, rb_ref[b, head], acc)
                bias_ref[h, :, j * BLOCK:(j + 1) * BLOCK] = acc
                bias_ref[N_KV_HEADS + h, :, j * BLOCK:(j + 1) * BLOCK] = jnp.where(prev, NEG, acc)
                sink_ref[h, :, j * BLOCK:(j + 1) * BLOCK] = jnp.full((8, BLOCK), sk_ref[0, head], F32)

    smem = pl.BlockSpec(memory_space=pltpu.SMEM)
    return pl.pallas_call(
        body, name=name,
        in_specs=[smem, smem, pl.BlockSpec(memory_space=pltpu.VMEM)],
        out_specs=[pl.BlockSpec(memory_space=pltpu.VMEM), pl.BlockSpec(memory_space=pltpu.VMEM)],
        out_shape=[jax.ShapeDtypeStruct((2 * N_KV_HEADS, 2 * BLOCK, GROUP * BLOCK), F32),
                   jax.ShapeDtypeStruct((N_KV_HEADS, 8, GROUP * BLOCK), F32)],
    )(rel_bias, sinks.reshape(1, N_Q_HEADS), bucket_t)


def _bias_bwd(dbias_a, dbias_b, name):
    bucket_t = jnp.asarray(_rel_tables().T)

    def body(da_ref, db_ref, bk_ref, o_ref):
        bk = bk_ref[...]
        ri = lax.broadcasted_iota(jnp.int32, (REL_BUCKETS, 128), 0)
        ci = lax.broadcasted_iota(jnp.int32, (REL_BUCKETS, 128), 1)
        out = jnp.zeros((REL_BUCKETS, 128), F32)
        for h in range(N_KV_HEADS):
            dsum = da_ref[h] + db_ref[h]
            for j in range(GROUP):
                head = GROUP * h + j
                seg = dsum[:, j * BLOCK:(j + 1) * BLOCK]
                for b in range(REL_BUCKETS):
                    val = jnp.sum(jnp.sum(jnp.where(bk == b, seg, 0.0), axis=0, keepdims=True), axis=1, keepdims=True)
                    out = out + jnp.where((ri == b) & (ci == head), val, 0.0)
        o_ref[...] = out

    vm = pl.BlockSpec(memory_space=pltpu.VMEM)
    return pl.pallas_call(
        body, name=name, in_specs=[vm, vm, vm], out_specs=vm,
        out_shape=jax.ShapeDtypeStruct((REL_BUCKETS, 128), F32),
    )(dbias_a, dbias_b, bucket_t)


def _stack_members(ref, r0, width):
    blk = ref[pl.ds(r0, BLOCK), 0:GROUP * width]
    return jnp.concatenate([blk[:, j * width:(j + 1) * width] for j in range(GROUP)], axis=0)


def _mix_b_fwd(x, qp, kv, bias, sinkt, memkv, layer, wout, tm, name):
    t, d = x.shape
    n_mem = memkv.shape[0]
    mw = N_MEM_HEADS * HEAD_DIM
    qw = d - mw
    kw = N_KV_HEADS * HEAD_DIM
    nb = tm // BLOCK
    rows = GROUP * BLOCK

    def body(x_ref, q_ref, kv_ref, kvh_ref, bias_ref, sink_ref, mkv_ref, wo_ref, xo_ref, kvx, ytok):
        i = pl.program_id(0)
        kvx[0:BLOCK, :] = kvh_ref[...]
        kvx[BLOCK:BLOCK + tm, :] = kv_ref[...]
        colh = _col_head(kw)

        def blk(b, carry):
            r0 = pl.multiple_of(b * BLOCK, BLOCK)
            win = kvx[pl.ds(r0, 2 * BLOCK), :]
            kwin = win[:, 0:kw] * SCALE
            vwin = win[:, kw:2 * kw]
            qs = _stack_members(q_ref, r0, kw)
            first = ((i == 0) & (b == 0)).astype(jnp.int32) * N_KV_HEADS
            heads = range(N_KV_HEADS)
            ss = [lax.dot_general(_keep_head(kwin, colh, h), qs, NT, preferred_element_type=F32) for h in heads]
            ps = [_softmax_cols(ss[h] + bias_ref[first + h], sink_ref[h][0:1, :])[0].astype(BF16) for h in heads]
            o = _add4([lax.dot_general(ps[h], _keep_head(vwin, colh, h), TN, preferred_element_type=F32)
                       for h in heads])
            for j in range(GROUP):
                ytok[pl.ds(r0, BLOCK), j * kw:(j + 1) * kw] = o[j * BLOCK:(j + 1) * BLOCK].astype(BF16)
            return carry

        for b_static in range(nb):
            blk(b_static, 0)
        mkv = mkv_ref[...]
        ymem = _mem_attn_fwd(q_ref[:, qw:d], mkv[:, 0:mw], mkv[:, mw:2 * mw])
        cat = jnp.concatenate([ytok[...], ymem.astype(BF16)], axis=1)
        xo_ref[...] = x_ref[...] + jnp.dot(cat, wo_ref[...], preferred_element_type=F32)

    return pl.pallas_call(
        body, name=name, grid=(t // tm,),
        in_specs=[_row_spec(tm, d), _row_spec(tm, d), _row_spec(tm, 2 * kw), _halo_prev_spec(BLOCK, 2 * kw, tm),
                  _const_spec((2 * N_KV_HEADS, 2 * BLOCK, rows)), _const_spec((N_KV_HEADS, 8, rows)),
                  pl.BlockSpec((n_mem, 2 * mw), lambda i: (0, layer)), _const_spec((d, d))],
        out_specs=_row_spec(tm, d),
        out_shape=jax.ShapeDtypeStruct((t, d), F32),
        scratch_shapes=[pltpu.VMEM((tm + BLOCK, 2 * kw), BF16), pltpu.VMEM((tm, qw), BF16)],
        compiler_params=_cp(("parallel",), VMEM_BIG),
    )(x, qp, kv, kv, bias, sinkt, memkv, wout)


def _mix_b_bwd(dxm, qp, kv, bias, sinkt, memkv, layer, wout, tm, name):
    t, d = dxm.shape
    n_mem = memkv.shape[0]
    mw = N_MEM_HEADS * HEAD_DIM
    qw = d - mw
    kw = N_KV_HEADS * HEAD_DIM
    nb = tm // BLOCK
    nt = t // tm
    rows = GROUP * BLOCK

    def body(dx_ref, q_ref, kv_ref, kvh_ref, bias_ref, sink_ref, mkv_ref, wo_ref,
             dq_ref, cat_ref, dkv_ref, dkvh_ref, dbias_ref, dsink_ref, dmkv_ref,
             kvx, dkvx, dcat_s, dmk_acc, dmv_acc):
        i = pl.program_id(0)

        @pl.when(i == 0)
        def _():
            dbias_ref[...] = jnp.zeros_like(dbias_ref)
            dsink_ref[...] = jnp.zeros_like(dsink_ref)
            dmk_acc[...] = jnp.zeros_like(dmk_acc)
            dmv_acc[...] = jnp.zeros_like(dmv_acc)

        kvx[0:BLOCK, :] = kvh_ref[...]
        kvx[BLOCK:BLOCK + tm, :] = kv_ref[...]
        dkvx[...] = jnp.zeros_like(dkvx)
        dcat_s[...] = lax.dot_general(dx_ref[...].astype(BF16), wo_ref[...], NT,
                                      preferred_element_type=F32).astype(BF16)
        colh = _col_head(kw)
        lane8 = lax.broadcasted_iota(jnp.int32, (8, 128), 1)

        def blk(b, carry):
            r0 = pl.multiple_of(b * BLOCK, BLOCK)
            win = kvx[pl.ds(r0, 2 * BLOCK), :]
            kwin = win[:, 0:kw] * SCALE
            vwin = win[:, kw:2 * kw]
            qs = _stack_members(q_ref, r0, kw)
            dos = _stack_members(dcat_s, r0, kw)
            first = ((i == 0) & (b == 0)).astype(jnp.int32) * N_KV_HEADS
            heads = range(N_KV_HEADS)
            khs = [_keep_head(kwin, colh, h) for h in heads]
            vhs = [_keep_head(vwin, colh, h) for h in heads]
            ss = [lax.dot_general(khs[h], qs, NT, preferred_element_type=F32) for h in heads]
            dps = [lax.dot_general(vhs[h], dos, NT, preferred_element_type=F32) for h in heads]
            dsink = jnp.zeros((8, 128), F32)
            pbs, dsbs = [], []
            for h in heads:
                p, sinkp = _softmax_cols(ss[h] + bias_ref[first + h], sink_ref[h][0:1, :])
                delta = jnp.sum(p * dps[h], axis=0, keepdims=True)
                ds = p * (dps[h] - delta)
                dbias_ref[h] += ds
                sd = sinkp * delta
                for j in range(GROUP):
                    val = -jnp.sum(sd[:, j * BLOCK:(j + 1) * BLOCK], axis=1, keepdims=True)
                    dsink = dsink + jnp.where(lane8 == 4 * j + h, val, 0.0)
                pbs.append(p.astype(BF16))
                dsbs.append(ds.astype(BF16))
            y = _add4([lax.dot_general(pbs[h], vhs[h], TN, preferred_element_type=F32) for h in heads])
            dq = _add4([lax.dot_general(dsbs[h], khs[h], TN, preferred_element_type=F32) for h in heads])
            dk = _add4([jnp.where(colh == h, jnp.dot(dsbs[h], qs, preferred_element_type=F32) * SCALE, 0.0)
                        for h in heads])
            dv = _add4([jnp.where(colh == h, jnp.dot(pbs[h], dos, preferred_element_type=F32), 0.0) for h in heads])
            for j in range(GROUP):
                cat_ref[pl.ds(r0, BLOCK), j * kw:(j + 1) * kw] = y[j * BLOCK:(j + 1) * BLOCK].astype(BF16)
                dq_ref[pl.ds(r0, BLOCK), j * kw:(j + 1) * kw] = dq[j * BLOCK:(j + 1) * BLOCK].astype(BF16)
            dsink_ref[...] += dsink
            dkvx[pl.ds(r0, 2 * BLOCK), :] += jnp.concatenate([dk, dv], axis=1)
            return carry

        for b_static in range(nb):
            blk(b_static, 0)
        dkvh_ref[0] = dkvx[0:BLOCK, :]
        dkv_ref[...] = dkvx[BLOCK:BLOCK + tm, :]

        mkv = mkv_ref[...]
        ymem, dqm, dmk, dmv = _mem_attn_bwd(q_ref[:, qw:d], dcat_s[:, qw:d], mkv[:, 0:mw], mkv[:, mw:2 * mw])
        cat_ref[:, qw:d] = ymem.astype(BF16)
        dq_ref[:, qw:d] = dqm.astype(BF16)
        dmk_acc[...] += dmk
        dmv_acc[...] += dmv

        @pl.when(i == nt - 1)
        def _():
            dmkv_ref[...] = jnp.concatenate([dmk_acc[...], dmv_acc[...]], axis=1)

    return pl.pallas_call(
        body, name=name, grid=(nt,),
        in_specs=[_row_spec(tm, d), _row_spec(tm, d), _row_spec(tm, 2 * kw), _halo_prev_spec(BLOCK, 2 * kw, tm),
                  _const_spec((2 * N_KV_HEADS, 2 * BLOCK, rows)), _const_spec((N_KV_HEADS, 8, rows)),
                  pl.BlockSpec((n_mem, 2 * mw), lambda i: (0, layer)), _const_spec((d, d))],
        out_specs=[_row_spec(tm, d), _row_spec(tm, d), _row_spec(tm, 2 * kw),
                   pl.BlockSpec((1, BLOCK, 2 * kw), lambda i: (i, 0, 0)),
                   pl.BlockSpec((N_KV_HEADS, 2 * BLOCK, rows), lambda i: (0, 0, 0)),
                   pl.BlockSpec((8, 128), lambda i: (0, 0)),
                   pl.BlockSpec((n_mem, 2 * mw), lambda i: (0, 0))],
        out_shape=[jax.ShapeDtypeStruct((t, d), BF16), jax.ShapeDtypeStruct((t, d), BF16),
                   jax.ShapeDtypeStruct((t, 2 * kw), F32), jax.ShapeDtypeStruct((nt, BLOCK, 2 * kw), F32),
                   jax.ShapeDtypeStruct((N_KV_HEADS, 2 * BLOCK, rows), F32), jax.ShapeDtypeStruct((8, 128), F32),
                   jax.ShapeDtypeStruct((n_mem, 2 * mw), F32)],
        scratch_shapes=[pltpu.VMEM((tm + BLOCK, 2 * kw), BF16), pltpu.VMEM((tm + BLOCK, 2 * kw), F32),
                        pltpu.VMEM((tm, d), BF16), pltpu.VMEM((n_mem, mw), F32), pltpu.VMEM((n_mem, mw), F32)],
        compiler_params=_cp(("arbitrary",), VMEM_BIG),
    )(dxm, qp, kv, kv, bias, sinkt, memkv, wout)


def _kv_assemble(main_a, halo_a, main_b, halo_b, tm, name):
    t, n = main_a.shape
    nt = t // tm

    def body(ma_ref, ha_ref, mb_ref, hb_ref, o_ref):
        i = pl.program_id(0)
        s = ma_ref[...] + mb_ref[...]
        tail = jnp.where(i == nt - 1, 0.0, ha_ref[0] + hb_ref[0])
        o_ref[...] = jnp.concatenate([s[0:tm - BLOCK], s[tm - BLOCK:] + tail], axis=0).astype(BF16)

    halo_spec = pl.BlockSpec((1, BLOCK, n), lambda i: (jnp.minimum(i + 1, nt - 1), 0, 0))
    return pl.pallas_call(
        body, name=name, grid=(nt,),
        in_specs=[_row_spec(tm, n), halo_spec, _row_spec(tm, n), halo_spec],
        out_specs=_row_spec(tm, n),
        out_shape=jax.ShapeDtypeStruct((t, n), BF16),
        compiler_params=_cp(("parallel",)),
    )(main_a, halo_a, main_b, halo_b)


def _adam_math(w, g, m, v):
    m2 = ADAM_B1 * m + (1.0 - ADAM_B1) * g
    v2 = ADAM_B2 * v + (1.0 - ADAM_B2) * (g * g)
    m_hat = m2 / (1.0 - ADAM_B1 ** ADAM_STEP)
    v_hat = v2 / (1.0 - ADAM_B2 ** ADAM_STEP)
    delta = -ADAM_LR * (m_hat / (jnp.sqrt(v_hat) + ADAM_EPS) + ADAM_WD * w)
    return delta, m2, v2


def _adamw_sharded(w, land, m, v, name):
    nl, r, c = w.shape
    tr = r
    for cand in (512, 256, 128, 64, 32, 16):
        if r % cand == 0:
            tr = cand
            break

    def body(w_ref, a_ref, m_ref, v_ref, g_ref, d_ref, mo_ref, vo_ref):
        g = a_ref[0, 0].astype(F32) + a_ref[1, 0].astype(F32)
        for k in range(1, N_CHIPS):
            g = g + (a_ref[2 * k, 0].astype(F32) + a_ref[2 * k + 1, 0].astype(F32))
        delta, m2, v2 = _adam_math(w_ref[0], g, m_ref[0], v_ref[0])
        g_ref[0] = g
        d_ref[0] = delta
        mo_ref[0] = m2
        vo_ref[0] = v2

    rs = pl.BlockSpec((1, tr, c), lambda l, i: (l, i, 0))
    ps = pl.BlockSpec((2 * N_CHIPS, 1, tr, c), lambda l, i: (0, l, i, 0))
    sd = jax.ShapeDtypeStruct((nl, r, c), F32)
    return pl.pallas_call(
        body, name=name, grid=(nl, r // tr),
        in_specs=[rs, ps, rs, rs], out_specs=[rs, rs, rs, rs], out_shape=[sd, sd, sd, sd],
        compiler_params=_cp(("parallel", "parallel")),
    )(w, land, m, v)


def _adamw_packed(w, g, m, v, name):
    def body(w_ref, g_ref, m_ref, v_ref, d_ref, mo_ref, vo_ref):
        delta, m2, v2 = _adam_math(w_ref[...], g_ref[...], m_ref[...], v_ref[...])
        d_ref[...] = delta
        mo_ref[...] = m2
        vo_ref[...] = v2

    vm = pl.BlockSpec(memory_space=pltpu.VMEM)
    sd = jax.ShapeDtypeStruct(w.shape, F32)
    return pl.pallas_call(body, name=name, in_specs=[vm] * 4, out_specs=[vm] * 3, out_shape=[sd] * 3)(w, g, m, v)


def _place():
    return lax.axis_index("x"), lax.axis_index("y"), lax.axis_index("c")


def _hbm(a):
    return pltpu.with_memory_space_constraint(a, pltpu.HBM)


def _peers(x, y, c, both_cores):
    chips = [(1 - x, y), (x, 1 - y), (1 - x, 1 - y)]
    if not both_cores:
        return [(px, py, c) for px, py in chips]
    return [(px, py, pc) for px, py in chips for pc in (c, 1 - c)] + [(x, y, 1 - c)]


def _chip_copy(src, land, gather, layer, chip_src, slot, send_sem, recv_sem, peer):
    s = src if gather else src.at[chip_src]
    d = land.at[slot] if layer is None else land.at[slot, layer]
    return pltpu.make_async_remote_copy(src_ref=s, dst_ref=d, send_sem=send_sem, recv_sem=recv_sem,
                                        device_id=peer, device_id_type=MESH)


def _exchange_start(srcs, lands, gather, layers, both_cores, after, name):
    n = len(srcs)
    npeer = 7 if both_cores else 3
    hbm = pl.BlockSpec(memory_space=pltpu.HBM)
    sem = pl.BlockSpec(memory_space=pltpu.SEMAPHORE)

    def body(*refs):
        ins, lds = refs[:n], refs[n:2 * n]
        first_out = 2 * n + len(after)
        send_sems, recv_sems, token = refs[first_out], refs[first_out + 1], refs[-1]
        x, y, c = _place()
        slot = 2 * x + y if gather else 2 * (2 * x + y) + c
        for t in range(n):
            for r, peer in enumerate(_peers(x, y, c, both_cores)):
                _chip_copy(ins[t], lds[t], gather, layers[t], 2 * peer[0] + peer[1], slot,
                           send_sems.at[npeer * t + r], recv_sems.at[npeer * t + r], peer).start()
        token[...] = jnp.zeros_like(token)

    both = list(srcs) + list(lands)
    outs = pl.pallas_call(
        body, name=name, in_specs=[hbm] * (2 * n) + [pl.BlockSpec(memory_space=pl.ANY)] * len(after),
        out_specs=(sem, sem, *([hbm] * (2 * n)), pl.BlockSpec(memory_space=pltpu.VMEM)),
        out_shape=(pltpu.SemaphoreType.DMA((npeer * n,)), pltpu.SemaphoreType.DMA((npeer * n,)),
                   *[pltpu.HBM(a.shape, a.dtype) for a in both], jax.ShapeDtypeStruct((8, 128), F32)),
        input_output_aliases={t: 2 + t for t in range(2 * n)},
        compiler_params=_cp(has_side_effects=pltpu.SideEffectType.DATAFLOW_SIDE_EFFECTING),
    )(*[_hbm(a) for a in both], *after)
    return dict(send=outs[0], recv=outs[1], srcs=list(outs[2:2 + n]), lands=list(outs[2 + n:2 + 2 * n]),
                token=outs[-1], gather=gather, layers=list(layers), both_cores=both_cores)


def _exchange_wait(groups, lands, land_ids, after, name):
    flat = [s for g in groups for s in g["srcs"]]
    ns, nl, ng, na = len(flat), len(lands), len(groups), len(after)
    hbm = pl.BlockSpec(memory_space=pltpu.HBM)
    sem = pl.BlockSpec(memory_space=pltpu.SEMAPHORE)

    def body(*refs):
        srcs, lds = refs[:ns], refs[ns:ns + nl]
        sems = refs[ns + nl:ns + nl + 2 * ng]
        x, y, c = _place()
        k = 0
        for gi, g in enumerate(groups):
            peers = _peers(x, y, c, g["both_cores"])
            for t in range(len(g["srcs"])):
                for r, peer in enumerate(peers):
                    cp = _chip_copy(srcs[k], lds[land_ids[gi][t]], g["gather"], g["layers"][t], 0, 0,
                                    sems[2 * gi].at[len(peers) * t + r], sems[2 * gi + 1].at[len(peers) * t + r], peer)
                    cp.wait_send()
                    cp.wait_recv()
                k += 1

    both = flat + list(lands)
    sem_args = [a for g in groups for a in (g["send"], g["recv"])]
    outs = pl.pallas_call(
        body, name=name,
        in_specs=[hbm] * (ns + nl) + [sem] * (2 * ng) + [pl.BlockSpec(memory_space=pl.ANY)] * na,
        out_specs=[hbm] * (ns + nl),
        out_shape=[pltpu.HBM(a.shape, a.dtype) for a in both],
        input_output_aliases={t: t for t in range(ns + nl)},
        compiler_params=_cp(has_side_effects=pltpu.SideEffectType.DATAFLOW_SIDE_EFFECTING),
    )(*both, *sem_args, *after)
    return list(outs[ns:])


def _core_fill(lands, layers, name):
    n = len(lands)
    hbm = pl.BlockSpec(memory_space=pltpu.HBM)

    def body(*refs):
        ins = refs[:n]
        send_sems, recv_sems = refs[2 * n:]
        x, y, c = _place()
        copies = []
        for t in range(n):
            for k in range(N_CHIPS):
                mine = ins[t].at[2 * k + c] if layers[t] is None else ins[t].at[2 * k + c, layers[t]]
                cp = pltpu.make_async_remote_copy(
                    src_ref=mine, dst_ref=mine, send_sem=send_sems.at[N_CHIPS * t + k],
                    recv_sem=recv_sems.at[N_CHIPS * t + k], device_id=(x, y, 1 - c), device_id_type=MESH)
                cp.start()
                copies.append(cp)
        for cp in copies:
            cp.wait()

    return pl.pallas_call(
        body, name=name, in_specs=[hbm] * n, out_specs=[hbm] * n,
        out_shape=[jax.ShapeDtypeStruct(a.shape, a.dtype) for a in lands],
        input_output_aliases={t: t for t in range(n)},
        scratch_shapes=[pltpu.SemaphoreType.DMA((N_CHIPS * n,)), pltpu.SemaphoreType.DMA((N_CHIPS * n,))],
        compiler_params=_cp(has_side_effects=True),
    )(*lands)


def _all_reduce_packed(pack, name):
    r, c = pack.shape
    vm = pl.BlockSpec(memory_space=pltpu.VMEM)

    def body(p_ref, sum_ref, slots, send_sems, recv_sems):
        x, y, cc = _place()
        me = 4 * x + 2 * y + cc
        slots[me] = p_ref[...]
        copies = []
        for rel in range(1, 8):
            px = 1 - x if rel & 4 else x
            py = 1 - y if rel & 2 else y
            pc = 1 - cc if rel & 1 else cc
            cp = pltpu.make_async_remote_copy(
                src_ref=p_ref, dst_ref=slots.at[me], send_sem=send_sems.at[rel - 1], recv_sem=recv_sems.at[rel - 1],
                device_id=(px, py, pc), device_id_type=MESH)
            cp.start()
            copies.append(cp)
        for cp in copies:
            cp.wait()
        total = slots[0]
        for k in range(1, 8):
            total = total + slots[k]
        sum_ref[...] = total

    return pl.pallas_call(
        body, name=name, in_specs=[vm], out_specs=vm, out_shape=jax.ShapeDtypeStruct((r, c), F32),
        scratch_shapes=[pltpu.VMEM((8, r, c), F32), pltpu.SemaphoreType.DMA((7,)), pltpu.SemaphoreType.DMA((7,))],
        compiler_params=_cp(has_side_effects=True),
    )(pack)


def _pack(items):
    rows = []
    for a in items:
        flat = a.astype(F32).reshape(-1)
        pad = (-flat.shape[0]) % PACK_W
        rows.append(jnp.pad(flat, (0, pad)).reshape(-1, PACK_W))
    out = jnp.concatenate(rows, axis=0)
    pad_r = (-out.shape[0]) % 8
    return jnp.pad(out, ((0, pad_r), (0, 0)))


def _unpack(pack, shapes):
    outs, row = [], 0
    for s in shapes:
        n = int(np.prod(s))
        nr = -(-n // PACK_W)
        outs.append(pack[row:row + nr].reshape(-1)[:n].reshape(s))
        row += nr
    return outs


def _heads_to_member_major(w, axis):
    shp = w.shape
    pre, post = shp[:axis], shp[axis + 1:]
    w4 = w.reshape(pre + (N_KV_HEADS, GROUP, HEAD_DIM) + post)
    w4 = jnp.swapaxes(w4, len(pre), len(pre) + 1)
    return w4.reshape(shp)


def _heads_to_kv_major(w, axis):
    shp = w.shape
    pre, post = shp[:axis], shp[axis + 1:]
    w4 = w.reshape(pre + (GROUP, N_KV_HEADS, HEAD_DIM) + post)
    w4 = jnp.swapaxes(w4, len(pre), len(pre) + 1)
    return w4.reshape(shp)


def kernel(x, mem, norm_mix, norm_ffn, a_w_in, a_conv_w, a_w_out, kv_norm, w_kv, b_w_q, b_sinks, b_w_out, rel_bias, mem_norm, w_mem_kv, w_gate, w_up, w_down, final_norm, loss_target, m_norm_mix, m_norm_ffn, m_a_w_in, m_a_conv_w, m_a_w_out, m_kv_norm, m_w_kv, m_b_w_q, m_b_sinks, m_b_w_out, m_rel_bias, m_mem_norm, m_w_mem_kv, m_w_gate, m_w_up, m_w_down, m_final_norm, v_norm_mix, v_norm_ffn, v_a_w_in, v_a_conv_w, v_a_w_out, v_kv_norm, v_w_kv, v_b_w_q, v_b_sinks, v_b_w_out, v_rel_bias, v_mem_norm, v_w_mem_kv, v_w_gate, v_w_up, v_w_down, v_final_norm):
    t, d = x.shape[1], x.shape[2]
    tm = 512 if t % 512 == 0 and t >= 2048 else 256
    x0 = x.reshape(t, d)
    target = loss_target.reshape(t, d)
    mem2 = mem.reshape(mem.shape[1], d)
    n_mem = mem2.shape[0]
    ax, ay, ac = _place()
    chip = 2 * ax + ay
    cwid = a_conv_w.shape[2] * N_CHIPS
    qw = N_Q_HEADS * HEAD_DIM
    nq = N_CHIPS

    def own_slot(piece):
        return lax.dynamic_update_slice(lax.empty((nq,) + piece.shape, piece.dtype), piece[None],
                                        (chip,) + (0,) * piece.ndim)

    def mixer_shards(i):
        if i < N_A:
            shards = [a_w_in[i], a_w_out[i]] + ([w_mem_kv] if i == 0 else [])
        else:
            j = i - N_A
            shards = [b_w_q[j], b_w_out[j]] + ([w_kv] if j == 0 else [])
        return [a.astype(BF16) for a in shards]

    def ffn_shards(i):
        return [w_gate[i].T.astype(BF16), w_up[i].T.astype(BF16), w_down[i].astype(BF16)]

    conv_pad = jnp.pad(a_conv_w, ((0, 0), (0, 8 - a_conv_w.shape[1]), (0, (-a_conv_w.shape[2]) % 128)))
    first = mixer_shards(0)
    group_shards = {"0a": first[0:1], "0b": first[1:] + [conv_pad], "0f": ffn_shards(0)}
    for i in range(1, DEPTH):
        group_shards[str(i)] = ffn_shards(i) + mixer_shards(i)
    gathers, prev_tok = {}, []
    for key, shards in group_shards.items():
        gathers[key] = _exchange_start(shards, [own_slot(a) for a in shards], True, [None] * len(shards), False,
                                       prev_tok, "gather_start_" + key)
        prev_tok = [gathers[key]["token"]]

    def rows_full(g):
        return g.reshape((-1,) + g.shape[2:])

    def cols_full(g):
        return jnp.transpose(g, (1, 0, 2)).reshape(g.shape[1], -1)

    def landed_weights(key, after):
        g = gathers[key]
        return _exchange_wait([g], g["lands"], [list(range(len(g["lands"])))], after, "gather_wait_" + key)

    def mixer_weights(i, got):
        w_first, w_out = (cols_full(got[0]) if i < N_A else rows_full(got[0])), rows_full(got[1])
        if i >= N_A:
            w_first = jnp.concatenate([_heads_to_member_major(w_first[:, :qw], 1), w_first[:, qw:]], axis=1)
            w_out = jnp.concatenate([_heads_to_member_major(w_out[:qw, :], 0), w_out[qw:, :]], axis=0)
        return dict(w_first=w_first, w_out=w_out, extra=got[2] if len(got) > 2 else None)

    def ffn_weights(got):
        return dict(wg=rows_full(got[0]), wu=rows_full(got[1]), wd=rows_full(got[2]))

    bias, sinkt = [], []
    for j in range(2):
        bj, sj = _bias_tables(rel_bias, b_sinks[j], "bias_tables")
        bias.append(bj)
        sinkt.append(sj)

    ws = []
    xs, xmids, projs, gates, ups = [x0], [], [], [], []
    kv = memkv = wmem = wkv = None
    for i in range(DEPTH):
        xin = xs[-1]
        if i == 0:
            w = dict(w_first=cols_full(landed_weights("0a", prev_tok)[0]))
        else:
            got = landed_weights(str(i), [xin])
            w = dict(mixer_weights(i, got[3:]), **ffn_weights(got[0:3]))
        ws.append(w)
        gm = norm_mix[i].reshape(1, d)
        if i < N_A:
            proj = _norm_mm(xin, gm, w["w_first"], tm, "proj_a")
            if i == 0:
                got = landed_weights("0b", [proj])
                w["w_out"] = rows_full(got[0])
                full_mem = jnp.swapaxes(got[1], 0, 1).reshape(DEPTH, d, -1)
                wmem = jnp.transpose(full_mem, (1, 0, 2)).reshape(d, -1)
                memkv = _norm_mm(mem2, mem_norm.reshape(1, d), wmem, n_mem, "mem_kv")
                taps = got[2][:, :, 0:3, 0:a_conv_w.shape[2]]
                conv_full = jnp.transpose(taps, (1, 2, 0, 3)).reshape(N_A, 3, cwid)
            xmid = _mix_a_fwd(xin, proj, conv_full[i], memkv, i, w["w_out"], tm, "mix_a_fwd")
        else:
            j = i - N_A
            if j == 0:
                wkv = rows_full(w["extra"])
                kv = _norm_mm(xin, kv_norm.reshape(1, d), wkv, tm, "proj_kv")
            proj = _norm_mm(xin, gm, w["w_first"], tm, "proj_b")
            xmid = _mix_b_fwd(xin, proj, kv, bias[j], sinkt[j], memkv, i, w["w_out"], tm, "mix_b_fwd")
        if i == 0:
            w.update(ffn_weights(landed_weights("0f", [xmid])))
        xout, gate, up = _ffn_fwd(xmid, norm_ffn[i].reshape(1, d), w["wg"], w["wu"], w["wd"], tm, "ffn_fwd")
        projs.append(proj)
        xmids.append(xmid)
        gates.append(gate)
        ups.append(up)
        xs.append(xout)

    loss_part, dx, dg_final = _final_loss(xs[-1], final_norm.reshape(1, d), target, tm, "final_loss")

    def rows_pieces(g):
        return g.astype(BF16).reshape((nq, g.shape[0] // nq) + g.shape[1:])

    def cols_pieces(g):
        return jnp.transpose(g.astype(BF16).reshape(g.shape[0], nq, g.shape[1] // nq), (1, 0, 2))

    swapped = ("w_gate", "w_up")
    stacked = dict(a_w_in=a_w_in, a_w_out=a_w_out, w_kv=w_kv[None], b_w_q=b_w_q, b_w_out=b_w_out,
                   w_mem_kv=w_mem_kv, w_gate=jnp.swapaxes(w_gate, 1, 2), w_up=jnp.swapaxes(w_up, 1, 2), w_down=w_down)
    names = list(stacked)
    land = {k: lax.empty((2 * nq,) + stacked[k].shape, BF16) for k in names}
    own = {k: [None] * stacked[k].shape[0] for k in names}
    scatters, scatter_ids = [], []

    def scatter_start(key, items, both_cores):
        keys = [k for k, _, _ in items]
        st = _exchange_start([p for _, _, p in items], [land[k] for k in keys], False, [l for _, l, _ in items],
                             both_cores, [], "scatter_start_" + key)
        for (k, l, p), ld in zip(items, st["lands"]):
            land[k] = ld
            mine_piece = lax.dynamic_index_in_dim(p, chip, 0, keepdims=False)
            if l is None:
                own[k] = [mine_piece[q] for q in range(mine_piece.shape[0])]
            else:
                own[k][l] = mine_piece
        scatters.append(st)
        scatter_ids.append([names.index(k) for k in keys])
        return st["token"][0:1, 0:1]

    g_norm_mix, g_norm_ffn = [None] * DEPTH, [None] * DEPTH
    g_conv, g_sinks = [None] * 2, [None] * 2
    dmemkv = [None] * DEPTH
    dbias, dkv_main, dkv_halo = [None] * 2, [None] * 2, [None] * 2
    g_kv_norm = None
    tok = jnp.zeros((1, 1), F32)
    for i in reversed(range(DEPTH)):
        w = ws[i]
        dxm, dgate, dup, act, h2, dgf = _ffn_bwd(dx, xmids[i], norm_ffn[i].reshape(1, d) + tok, gates[i], ups[i],
                                            w["wg"], w["wu"], w["wd"], tm // 2, "ffn_bwd")
        g_norm_ffn[i] = dgf
        g_wd = _wgrad(act, dx, 2 * tm, "wgrad_down")
        g_wg = _wgrad(dgate, h2, 2 * tm, "wgrad_gate")
        g_wu = _wgrad(dup, h2, 2 * tm, "wgrad_up")
        items = [("w_gate", i, rows_pieces(g_wg)), ("w_up", i, rows_pieces(g_wu)), ("w_down", i, rows_pieces(g_wd))]
        if i == 0:
            tok = scatter_start("0f", items, True)
            items = []
        gm = norm_mix[i].reshape(1, d)
        if i < N_A:
            dproj, cat, dcw, dmemkv[i] = _mix_a_bwd(dxm, projs[i], conv_full[i] + (tok if i == 0 else 0.0), memkv, i,
                                                    w["w_out"], tm, "mix_a_bwd")
            g_conv[i] = dcw[0:3]
            g_out = _wgrad(cat, dxm, 2 * tm, "wgrad_out")
            if i == 0:
                dmemkv_all = jnp.concatenate([a.astype(BF16) for a in dmemkv], axis=1)
                _, g_mem_norm, hmem = _mm_nt_normbwd(dmemkv_all, wmem, mem2, mem_norm.reshape(1, d),
                                                     jnp.zeros((n_mem, d), F32), n_mem, "mem_kv_bwd")
                g_wmem = _wgrad(hmem, dmemkv_all, n_mem, "wgrad_mem")
                g_wmem = jnp.transpose(g_wmem.reshape(nq, d // nq, DEPTH, -1), (0, 2, 1, 3))
                gm = gm + scatter_start("0o", [("a_w_out", 0, rows_pieces(g_out)), ("w_mem_kv", None, g_wmem)], False)
            dx, g_norm_mix[i], h = _mm_nt_normbwd(dproj, w["w_first"], xs[i], gm, dxm, tm, "proj_a_bwd")
            g_in = _wgrad(h, dproj, 2 * tm, "wgrad_in_a")
            items.append(("a_w_in", i, cols_pieces(g_in)))
            if i > 0:
                items.append(("a_w_out", i, rows_pieces(g_out)))
        else:
            j = i - N_A
            dqp, cat, dkv_main[j], dkv_halo[j], dbias[j], dsk, dmemkv[i] = _mix_b_bwd(
                dxm, projs[i], kv, bias[j], sinkt[j], memkv, i, w["w_out"], tm, "mix_b_bwd")
            g_sinks[j] = dsk[0, 0:N_Q_HEADS].reshape(GROUP, N_KV_HEADS).T.reshape(N_Q_HEADS)
            g_out = _wgrad(cat, dxm, 2 * tm, "wgrad_out")
            dx, g_norm_mix[i], h = _mm_nt_normbwd(dqp, w["w_first"], xs[i], gm, dxm, tm, "proj_b_bwd")
            g_q = _wgrad(h, dqp, 2 * tm, "wgrad_in_b")
            g_q = jnp.concatenate([_heads_to_kv_major(g_q[:, :qw], 1), g_q[:, qw:]], axis=1)
            g_out = jnp.concatenate([_heads_to_kv_major(g_out[:qw, :], 0), g_out[qw:, :]], axis=0)
            items += [("b_w_q", j, rows_pieces(g_q)), ("b_w_out", j, rows_pieces(g_out))]
            if j == 0:
                dkv = _kv_assemble(dkv_main[0], dkv_halo[0], dkv_main[1], dkv_halo[1], tm, "kv_assemble")
                dx, g_kv_norm, hkv = _mm_nt_normbwd(dkv, wkv, xs[i], kv_norm.reshape(1, d), dx, tm, "proj_kv_bwd")
                items.append(("w_kv", 0, rows_pieces(_wgrad(hkv, dkv, 2 * tm, "wgrad_kv"))))
        tok = scatter_start(str(i) if i else "0i", items, i > 0)
    grad_x = dx.reshape(x.shape)
    g_rel = _bias_bwd(dbias[0], dbias[1], "bias_bwd")[:, 0:N_Q_HEADS]

    small_shapes = [(DEPTH, d), (DEPTH, d), (d,), (d,), (d,), (2, N_Q_HEADS), (REL_BUCKETS, N_Q_HEADS),
                    (N_A, 3, cwid), ()]
    small = _pack([jnp.concatenate(g_norm_mix, axis=0), jnp.concatenate(g_norm_ffn, axis=0), g_kv_norm, g_mem_norm,
                   dg_final, jnp.stack(g_sinks), g_rel, jnp.stack(g_conv), loss_part[0, 0]])
    small_sum = _all_reduce_packed(small, "reduce_small")
    (gs_norm_mix, gs_norm_ffn, gs_kv_norm, gs_mem_norm, gs_final, gs_sinks, gs_rel, gs_conv_full, loss) = _unpack(
        small_sum, small_shapes)
    cq = cwid // N_CHIPS
    gs_conv = lax.dynamic_slice_in_dim(gs_conv_full, chip * cq, cq, axis=2)

    late = ("a_w_in", "a_w_out", "w_mem_kv")
    landed = dict(zip(names, _exchange_wait(scatters[:-2], [land[k] for k in names], scatter_ids[:-2], [small_sum],
                                            "scatter_wait_a")))

    def with_own(k, ld):
        return lax.dynamic_update_slice(ld, jnp.stack(own[k])[None], (2 * chip + ac,) + (0,) * (ld.ndim - 1))

    weights = dict(norm_mix=norm_mix, norm_ffn=norm_ffn, a_w_in=a_w_in, a_conv_w=a_conv_w, a_w_out=a_w_out,
                   kv_norm=kv_norm, w_kv=w_kv, b_w_q=b_w_q, b_sinks=b_sinks, b_w_out=b_w_out, rel_bias=rel_bias,
                   mem_norm=mem_norm, w_mem_kv=w_mem_kv, w_gate=w_gate, w_up=w_up, w_down=w_down,
                   final_norm=final_norm)
    moms = dict(norm_mix=m_norm_mix, norm_ffn=m_norm_ffn, a_w_in=m_a_w_in, a_conv_w=m_a_conv_w, a_w_out=m_a_w_out,
                kv_norm=m_kv_norm, w_kv=m_w_kv, b_w_q=m_b_w_q, b_sinks=m_b_sinks, b_w_out=m_b_w_out,
                rel_bias=m_rel_bias, mem_norm=m_mem_norm, w_mem_kv=m_w_mem_kv, w_gate=m_w_gate, w_up=m_w_up,
                w_down=m_w_down, final_norm=m_final_norm)
    vars_ = dict(norm_mix=v_norm_mix, norm_ffn=v_norm_ffn, a_w_in=v_a_w_in, a_conv_w=v_a_conv_w, a_w_out=v_a_w_out,
                 kv_norm=v_kv_norm, w_kv=v_w_kv, b_w_q=v_b_w_q, b_sinks=v_b_sinks, b_w_out=v_b_w_out,
                 rel_bias=v_rel_bias, mem_norm=v_mem_norm, w_mem_kv=v_w_mem_kv, w_gate=v_w_gate, w_up=v_w_up,
                 w_down=v_w_down, final_norm=v_final_norm)
    order = list(weights)
    grads, deltas, new_m, new_v = {}, {}, {}, {}
    def adamw(k, ld):
        shp, stk = weights[k].shape, ld.shape[1:]
        view = (lambda a: jnp.swapaxes(a, 1, 2)) if k in swapped else (lambda a: a.reshape(stk))
        back = (lambda a: jnp.swapaxes(a, 1, 2)) if k in swapped else (lambda a: a.reshape(shp))
        outs = _adamw_sharded(view(weights[k]), ld, view(moms[k]), view(vars_[k]), "adamw_" + k)
        grads[k], deltas[k], new_m[k], new_v[k] = [back(o) for o in outs]

    for k in names:
        if k not in late:
            adamw(k, with_own(k, landed[k]))
    late_ids = [[late.index(names[t]) for t in ids] for ids in scatter_ids[-2:]]
    late_landed = _exchange_wait(scatters[-2:], [landed[k] for k in late], late_ids, [deltas["w_down"]], "scatter_wait_b")
    filled = _core_fill([with_own(k, ld) for k, ld in zip(late, late_landed)], [0, 0, None], "fill_cores")
    for k, ld in zip(late, filled):
        adamw(k, ld)
    small_names = ["norm_mix", "norm_ffn", "kv_norm", "mem_norm", "final_norm", "b_sinks", "rel_bias", "a_conv_w"]
    small_g = [gs_norm_mix, gs_norm_ffn, gs_kv_norm, gs_mem_norm, gs_final, gs_sinks, gs_rel, gs_conv]
    shapes = [weights[k].shape for k in small_names]
    dl_p, m_p, v_p = _adamw_packed(_pack([weights[k] for k in small_names]), _pack(small_g),
                                   _pack([moms[k] for k in small_names]), _pack([vars_[k] for k in small_names]),
                                   "adamw_small")
    for k, g, dl, m2, v2 in zip(small_names, small_g, _unpack(dl_p, shapes), _unpack(m_p, shapes), _unpack(v_p, shapes)):
        grads[k], deltas[k], new_m[k], new_v[k] = g.reshape(weights[k].shape), dl, m2, v2

    return (loss, grad_x, *[grads[k] for k in order], *[deltas[k] for k in order],
            *[new_m[k] for k in order], *[new_v[k] for k in order])
```

```python
import functools
import math

import numpy as np
import jax
import jax.numpy as jnp
from jax import lax
from jax.experimental import pallas as pl
from jax.experimental.pallas import tpu as pltpu

F32 = jnp.float32
BF16 = jnp.bfloat16
MESH = pl.DeviceIdType.MESH

EPS = 1e-5
HEAD_DIM = 64
N_MEM_HEADS = 4
N_KV_HEADS = 4
GROUP = 3
N_Q_HEADS = N_KV_HEADS * GROUP
BLOCK = 128
REL_BUCKETS = 32
REL_MAX_DIST = 128
SCALE = HEAD_DIM ** -0.5
NEG = -1e30
N_CHIPS = 4
N_A = 2
DEPTH = 4

ADAM_LR = 0.001
ADAM_B1 = 0.9
ADAM_B2 = 0.999
ADAM_EPS = 1e-08
ADAM_WD = 0.01
ADAM_STEP = 10

VMEM_BIG = 56 * 1024 * 1024
PACK_W = 1024

NT = (((1,), (1,)), ((), ()))
TN = (((0,), (0,)), ((), ()))


def _cp(sem=None, vmem=None, **kw):
    return pltpu.CompilerParams(dimension_semantics=sem, vmem_limit_bytes=vmem, **kw)


def _const_spec(shape):
    nd = len(shape)
    return pl.BlockSpec(shape, lambda i, _n=nd: (0,) * _n, pipeline_mode=pl.Buffered(1))


def _row_spec(tm, n):
    return pl.BlockSpec((tm, n), lambda i: (i, 0))


def _rms_parts(xv):
    r = lax.rsqrt(jnp.mean(xv * xv, axis=-1, keepdims=True) + EPS)
    return xv * r, r


def _sigmoid(z):
    return 1.0 / (1.0 + jnp.exp(-z))


def _ff_chunks(f):
    if f % 512 == 0 or f % 256 != 0:
        return [(0, f)] if f <= 1536 else [(0, f // 2), (f // 2, f - f // 2)]
    n = f // 256
    a = (n + 1) // 2 * 256
    return [(0, a), (a, f - a)]


def _norm_mm(x, g, w, tm, name):
    t, d = x.shape
    n = w.shape[1]

    def body(x_ref, g_ref, w_ref, o_ref):
        xhat, _ = _rms_parts(x_ref[...])
        h = (xhat * g_ref[...]).astype(BF16)
        o_ref[...] = jnp.dot(h, w_ref[...], preferred_element_type=F32).astype(BF16)

    return pl.pallas_call(
        body, name=name, grid=(t // tm,),
        in_specs=[_row_spec(tm, d), _const_spec((1, d)), _const_spec((d, n))],
        out_specs=_row_spec(tm, n),
        out_shape=jax.ShapeDtypeStruct((t, n), BF16),
        compiler_params=_cp(("parallel",), VMEM_BIG),
    )(x, g, w)


def _mm_nt_normbwd(dproj, w, x_in, g, dres, tm, name):
    t, d = x_in.shape
    n = w.shape[1]

    def body(dp_ref, w_ref, x_ref, g_ref, dr_ref, dx_ref, dg_ref, h_ref):
        i = pl.program_id(0)
        xhat, r = _rms_parts(x_ref[...])
        gv = g_ref[...]
        h_ref[...] = (xhat * gv).astype(BF16)
        dh = lax.dot_general(dp_ref[...], w_ref[...], NT, preferred_element_type=F32)
        dxhat = dh * gv
        dx = r * (dxhat - xhat * jnp.mean(dxhat * xhat, axis=-1, keepdims=True))
        dx_ref[...] = dr_ref[...] + dx

        @pl.when(i == 0)
        def _():
            dg_ref[...] = jnp.zeros_like(dg_ref)

        dg_ref[...] += jnp.sum(dh * xhat, axis=0, keepdims=True)

    return pl.pallas_call(
        body, name=name, grid=(t // tm,),
        in_specs=[_row_spec(tm, n), _const_spec((d, n)), _row_spec(tm, d), _const_spec((1, d)), _row_spec(tm, d)],
        out_specs=[_row_spec(tm, d), pl.BlockSpec((1, d), lambda i: (0, 0)), _row_spec(tm, d)],
        out_shape=[jax.ShapeDtypeStruct((t, d), F32), jax.ShapeDtypeStruct((1, d), F32),
                   jax.ShapeDtypeStruct((t, d), BF16)],
        compiler_params=_cp(("arbitrary",), VMEM_BIG),
    )(dproj, w, x_in, g, dres)


def _ffn_fwd(x, g, wg, wu, wd, tm, name):
    t, d = x.shape
    f = wg.shape[0]
    chunks = _ff_chunks(f)

    def body(x_ref, g_ref, wg_ref, wu_ref, wd_ref, xo_ref, gate_ref, up_ref):
        xv = x_ref[...]
        xhat, _ = _rms_parts(xv)
        h = (xhat * g_ref[...]).astype(BF16)
        acc = xv
        for c0, cw in chunks:
            gt = lax.dot_general(h, wg_ref[c0:c0 + cw, :], NT, preferred_element_type=F32)
            ut = lax.dot_general(h, wu_ref[c0:c0 + cw, :], NT, preferred_element_type=F32)
            gate_ref[:, c0:c0 + cw] = gt.astype(BF16)
            up_ref[:, c0:c0 + cw] = ut.astype(BF16)
            a = (gt * _sigmoid(gt) * ut).astype(BF16)
            acc = acc + jnp.dot(a, wd_ref[c0:c0 + cw, :], preferred_element_type=F32)
        xo_ref[...] = acc

    return pl.pallas_call(
        body, name=name, grid=(t // tm,),
        in_specs=[_row_spec(tm, d), _const_spec((1, d)), _const_spec((f, d)), _const_spec((f, d)), _const_spec((f, d))],
        out_specs=[_row_spec(tm, d), _row_spec(tm, f), _row_spec(tm, f)],
        out_shape=[jax.ShapeDtypeStruct((t, d), F32), jax.ShapeDtypeStruct((t, f), BF16),
                   jax.ShapeDtypeStruct((t, f), BF16)],
        compiler_params=_cp(("parallel",), VMEM_BIG),
    )(x, g, wg, wu, wd)


def _ffn_bwd(dxo, xm, g, gate, up, wg, wu, wd, tm, name):
    t, d = xm.shape
    f = wg.shape[0]
    chunks = _ff_chunks(f)

    def body(dxo_ref, xm_ref, g_ref, gate_ref, up_ref, wg_ref, wu_ref, wd_ref,
             dxm_ref, dgate_ref, dup_ref, act_ref, h2_ref, dg_ref):
        i = pl.program_id(0)
        dxo_v = dxo_ref[...]
        dxo_b = dxo_v.astype(BF16)
        xhat, r = _rms_parts(xm_ref[...])
        gv = g_ref[...]
        h2_ref[...] = (xhat * gv).astype(BF16)
        dh = jnp.zeros((tm, d), F32)
        for c0, cw in chunks:
            dact = lax.dot_general(dxo_b, wd_ref[c0:c0 + cw, :], NT, preferred_element_type=F32)
            gt = gate_ref[:, c0:c0 + cw].astype(F32)
            ut = up_ref[:, c0:c0 + cw].astype(F32)
            sg = _sigmoid(gt)
            sl = gt * sg
            act_ref[:, c0:c0 + cw] = (sl * ut).astype(BF16)
            dgt = (dact * ut * (sg * (1.0 + gt * (1.0 - sg)))).astype(BF16)
            dut = (dact * sl).astype(BF16)
            dgate_ref[:, c0:c0 + cw] = dgt
            dup_ref[:, c0:c0 + cw] = dut
            dh = dh + jnp.dot(dgt, wg_ref[c0:c0 + cw, :], preferred_element_type=F32)
            dh = dh + jnp.dot(dut, wu_ref[c0:c0 + cw, :], preferred_element_type=F32)
        dxhat = dh * gv
        dx = r * (dxhat - xhat * jnp.mean(dxhat * xhat, axis=-1, keepdims=True))
        dxm_ref[...] = dxo_v + dx

        @pl.when(i == 0)
        def _():
            dg_ref[...] = jnp.zeros_like(dg_ref)

        dg_ref[...] += jnp.sum(dh * xhat, axis=0, keepdims=True)

    return pl.pallas_call(
        body, name=name, grid=(t // tm,),
        in_specs=[_row_spec(tm, d), _row_spec(tm, d), _const_spec((1, d)), _row_spec(tm, f), _row_spec(tm, f),
                  _const_spec((f, d)), _const_spec((f, d)), _const_spec((f, d))],
        out_specs=[_row_spec(tm, d), _row_spec(tm, f), _row_spec(tm, f), _row_spec(tm, f), _row_spec(tm, d),
                   pl.BlockSpec((1, d), lambda i: (0, 0))],
        out_shape=[jax.ShapeDtypeStruct((t, d), F32), jax.ShapeDtypeStruct((t, f), BF16),
                   jax.ShapeDtypeStruct((t, f), BF16), jax.ShapeDtypeStruct((t, f), BF16),
                   jax.ShapeDtypeStruct((t, d), BF16), jax.ShapeDtypeStruct((1, d), F32)],
        compiler_params=_cp(("arbitrary",), VMEM_BIG),
    )(dxo, xm, g, gate, up, wg, wu, wd)


def _wgrad(a, b, tt, name):
    t, k = a.shape
    n = b.shape[1]
    nt = t // tt

    def body(a_ref, b_ref, o_ref, acc):
        i = pl.program_id(0)

        @pl.when(i == 0)
        def _():
            acc[...] = jnp.zeros_like(acc)

        acc[...] += lax.dot_general(a_ref[...].astype(BF16), b_ref[...].astype(BF16), TN,
                                    preferred_element_type=F32)

        @pl.when(i == nt - 1)
        def _():
            o_ref[...] = acc[...].astype(BF16)

    return pl.pallas_call(
        body, name=name, grid=(nt,),
        in_specs=[_row_spec(tt, k), _row_spec(tt, n)],
        out_specs=pl.BlockSpec((k, n), lambda i: (0, 0)),
        out_shape=jax.ShapeDtypeStruct((k, n), BF16),
        scratch_shapes=[pltpu.VMEM((k, n), F32)],
        compiler_params=_cp(("arbitrary",), VMEM_BIG),
    )(a, b)


def _final_loss(x, g, target, tm, name):
    t, d = x.shape

    def body(x_ref, g_ref, t_ref, loss_ref, dx_ref, dg_ref):
        i = pl.program_id(0)
        xhat, r = _rms_parts(x_ref[...])
        gv = g_ref[...]
        err = xhat * gv - t_ref[...]
        dy = err * (1.0 / d)
        dxhat = dy * gv
        dx_ref[...] = r * (dxhat - xhat * jnp.mean(dxhat * xhat, axis=-1, keepdims=True))

        @pl.when(i == 0)
        def _():
            dg_ref[...] = jnp.zeros_like(dg_ref)
            loss_ref[...] = jnp.zeros_like(loss_ref)

        dg_ref[...] += jnp.sum(dy * xhat, axis=0, keepdims=True)
        part = jnp.sum(jnp.sum(err * err, axis=-1, keepdims=True), axis=0, keepdims=True) * (0.5 / d)
        loss_ref[...] += jnp.broadcast_to(part, loss_ref.shape)

    return pl.pallas_call(
        body, name=name, grid=(t // tm,),
        in_specs=[_row_spec(tm, d), _const_spec((1, d)), _row_spec(tm, d)],
        out_specs=[pl.BlockSpec((8, 128), lambda i: (0, 0)), _row_spec(tm, d), pl.BlockSpec((1, d), lambda i: (0, 0))],
        out_shape=[jax.ShapeDtypeStruct((8, 128), F32), jax.ShapeDtypeStruct((t, d), F32),
                   jax.ShapeDtypeStruct((1, d), F32)],
        compiler_params=_cp(("arbitrary",)),
    )(x, g, target)


def _col_head(width):
    return lax.broadcasted_iota(jnp.int32, (1, width), 1) // HEAD_DIM


def _keep_head(a, colh, h):
    return jnp.where(colh == h, a, jnp.zeros_like(a))


def _softmax_cols(s, sink=None):
    m = jnp.max(s, axis=0, keepdims=True)
    if sink is not None:
        m = jnp.maximum(m, sink)
    p = jnp.exp(s - m)
    l = jnp.sum(p, axis=0, keepdims=True)
    if sink is None:
        return p * (1.0 / l), None
    es = jnp.exp(sink - m)
    inv = 1.0 / (l + es)
    return p * inv, es * inv


def _add4(v):
    return (v[0] + v[1]) + (v[2] + v[3])


def _mem_attn_fwd(qm, mk, mv):
    colh = _col_head(mk.shape[1])
    mks = mk * SCALE
    heads = range(N_MEM_HEADS)
    ss = [lax.dot_general(_keep_head(mks, colh, h), qm, NT, preferred_element_type=F32) for h in heads]
    ps = [_softmax_cols(s)[0].astype(BF16) for s in ss]
    return _add4([lax.dot_general(ps[h], _keep_head(mv, colh, h), TN, preferred_element_type=F32) for h in heads])


def _mem_attn_bwd(qm, dy_b, mk, mv):
    colh = _col_head(mk.shape[1])
    mks = mk * SCALE
    heads = range(N_MEM_HEADS)
    khs = [_keep_head(mks, colh, h) for h in heads]
    vhs = [_keep_head(mv, colh, h) for h in heads]
    ss = [lax.dot_general(khs[h], qm, NT, preferred_element_type=F32) for h in heads]
    dps = [lax.dot_general(vhs[h], dy_b, NT, preferred_element_type=F32) for h in heads]
    pbs, dsbs = [], []
    for h in heads:
        p, _ = _softmax_cols(ss[h])
        ds = p * (dps[h] - jnp.sum(p * dps[h], axis=0, keepdims=True))
        pbs.append(p.astype(BF16))
        dsbs.append(ds.astype(BF16))
    y = _add4([lax.dot_general(pbs[h], vhs[h], TN, preferred_element_type=F32) for h in heads])
    dq = _add4([lax.dot_general(dsbs[h], khs[h], TN, preferred_element_type=F32) for h in heads])
    dmk = _add4([jnp.where(colh == h, jnp.dot(dsbs[h], qm, preferred_element_type=F32) * SCALE, 0.0) for h in heads])
    dmv = _add4([jnp.where(colh == h, jnp.dot(pbs[h], dy_b, preferred_element_type=F32), 0.0) for h in heads])
    return y, dq, dmk, dmv


def _shift_down(v, halo, k):
    rolled = pltpu.roll(v, k, 0)
    hrolled = pltpu.roll(halo, k, 0)[0:8]
    rows = lax.broadcasted_iota(jnp.int32, (8, v.shape[1]), 0)
    first = jnp.where(rows < k, hrolled, rolled[0:8])
    return jnp.concatenate([first, rolled[8:]], axis=0)


def _shift_up(v, halo, k):
    n = v.shape[0]
    rolled = pltpu.roll(v, n - k, 0)
    hrolled = pltpu.roll(halo, 8 - k, 0)[0:8]
    rows = lax.broadcasted_iota(jnp.int32, (8, v.shape[1]), 0)
    last = jnp.where(rows >= 8 - k, hrolled, rolled[n - 8:])
    return jnp.concatenate([rolled[:n - 8], last], axis=0)


def _conv_parts(p, ph, cw, first_tile, cwid):
    u = p[:, 0:cwid].astype(F32)
    bg = p[:, cwid:2 * cwid].astype(F32)
    cg = p[:, 2 * cwid:3 * cwid].astype(F32)
    v = cg * u
    vh = ph[:, 2 * cwid:3 * cwid].astype(F32) * ph[:, 0:cwid].astype(F32)
    vh = jnp.where(first_tile, 0.0, vh)
    v1 = _shift_down(v, vh, 1)
    v2 = _shift_down(v, vh, 2)
    conv = cw[0:1, :] * v2 + cw[1:2, :] * v1 + cw[2:3, :] * v
    return u, bg, cg, v, v1, v2, conv


def _halo_prev_spec(rows, n, tm):
    per = tm // rows
    return pl.BlockSpec((rows, n), lambda i: (jnp.maximum(i * per - 1, 0), 0))


def _halo_next_spec(rows, n, tm, t):
    per = tm // rows
    last = t // rows - 1
    return pl.BlockSpec((rows, n), lambda i: (jnp.minimum((i + 1) * per, last), 0))


def _mix_a_fwd(x, proj, convw, memkv, layer, wout, tm, name):
    t, d = x.shape
    n_mem = memkv.shape[0]
    mw = N_MEM_HEADS * HEAD_DIM
    cwid = d - mw
    pw = proj.shape[1]

    def body(x_ref, p_ref, ph_ref, cw_ref, mkv_ref, wo_ref, xo_ref):
        i = pl.program_id(0)
        p = p_ref[...]
        _, bg, _, _, _, _, conv = _conv_parts(p, ph_ref[...], cw_ref[...], i == 0, cwid)
        ytok = (bg * conv).astype(BF16)
        mkv = mkv_ref[...]
        ymem = _mem_attn_fwd(p[:, 3 * cwid:3 * cwid + mw], mkv[:, 0:mw], mkv[:, mw:2 * mw])
        cat = jnp.concatenate([ytok, ymem.astype(BF16)], axis=1)
        xo_ref[...] = x_ref[...] + jnp.dot(cat, wo_ref[...], preferred_element_type=F32)

    return pl.pallas_call(
        body, name=name, grid=(t // tm,),
        in_specs=[_row_spec(tm, d), _row_spec(tm, pw), _halo_prev_spec(16, pw, tm), _const_spec((3, cwid)),
                  pl.BlockSpec((n_mem, 2 * mw), lambda i: (0, layer)), _const_spec((d, d))],
        out_specs=_row_spec(tm, d),
        out_shape=jax.ShapeDtypeStruct((t, d), F32),
        compiler_params=_cp(("parallel",), VMEM_BIG),
    )(x, proj, proj, convw, memkv, wout)


def _mix_a_bwd(dxm, proj, convw, memkv, layer, wout, tm, name):
    t, d = dxm.shape
    n_mem = memkv.shape[0]
    mw = N_MEM_HEADS * HEAD_DIM
    cwid = d - mw
    pw = proj.shape[1]
    nt = t // tm

    def body(dx_ref, dxn_ref, p_ref, ph_ref, pn_ref, cw_ref, mkv_ref, wo_ref,
             dp_ref, cat_ref, dcw_ref, dmkv_ref, dmk_acc, dmv_acc):
        i = pl.program_id(0)
        p = p_ref[...]
        cw = cw_ref[...]
        wo = wo_ref[...]
        u, bg, cg, v, v1, v2, conv = _conv_parts(p, ph_ref[...], cw, i == 0, cwid)
        dcat = lax.dot_general(dx_ref[...].astype(BF16), wo, NT, preferred_element_type=F32)
        dytok = dcat[:, 0:cwid]
        dymem_b = dcat[:, cwid:d].astype(BF16)
        pn = pn_ref[...]
        dcat_n = lax.dot_general(dxn_ref[...].astype(BF16), wo[0:cwid, :], NT, preferred_element_type=F32)
        dconv_n = jnp.where(i == nt - 1, 0.0, dcat_n * pn[:, cwid:2 * cwid].astype(F32))
        dbg = dytok * conv
        dconv = dytok * bg
        dv = cw[2:3, :] * dconv + cw[1:2, :] * _shift_up(dconv, dconv_n, 1) + cw[0:1, :] * _shift_up(dconv, dconv_n, 2)
        du = dv * cg
        dcg = dv * u
        rows8 = lax.broadcasted_iota(jnp.int32, (8, cwid), 0)
        dcw = (jnp.where(rows8 == 0, jnp.sum(dconv * v2, axis=0, keepdims=True), 0.0)
               + jnp.where(rows8 == 1, jnp.sum(dconv * v1, axis=0, keepdims=True), 0.0)
               + jnp.where(rows8 == 2, jnp.sum(dconv * v, axis=0, keepdims=True), 0.0))
        mkv = mkv_ref[...]
        qm = p[:, 3 * cwid:3 * cwid + mw]
        ymem, dqm, dmk, dmv = _mem_attn_bwd(qm, dymem_b, mkv[:, 0:mw], mkv[:, mw:2 * mw])
        cat_ref[...] = jnp.concatenate([(bg * conv).astype(BF16), ymem.astype(BF16)], axis=1)
        dp_ref[...] = jnp.concatenate([du.astype(BF16), dbg.astype(BF16), dcg.astype(BF16), dqm.astype(BF16)], axis=1)

        @pl.when(i == 0)
        def _():
            dcw_ref[...] = jnp.zeros_like(dcw_ref)
            dmk_acc[...] = jnp.zeros_like(dmk_acc)
            dmv_acc[...] = jnp.zeros_like(dmv_acc)

        dcw_ref[...] += dcw
        dmk_acc[...] += dmk
        dmv_acc[...] += dmv

        @pl.when(i == nt - 1)
        def _():
            dmkv_ref[...] = jnp.concatenate([dmk_acc[...], dmv_acc[...]], axis=1)

    return pl.pallas_call(
        body, name=name, grid=(nt,),
        in_specs=[_row_spec(tm, d), _halo_next_spec(16, d, tm, t), _row_spec(tm, pw), _halo_prev_spec(16, pw, tm),
                  _halo_next_spec(16, pw, tm, t), _const_spec((3, cwid)),
                  pl.BlockSpec((n_mem, 2 * mw), lambda i: (0, layer)), _const_spec((d, d))],
        out_specs=[_row_spec(tm, pw), _row_spec(tm, d), pl.BlockSpec((8, cwid), lambda i: (0, 0)),
                   pl.BlockSpec((n_mem, 2 * mw), lambda i: (0, 0))],
        out_shape=[jax.ShapeDtypeStruct((t, pw), BF16), jax.ShapeDtypeStruct((t, d), BF16),
                   jax.ShapeDtypeStruct((8, cwid), F32), jax.ShapeDtypeStruct((n_mem, 2 * mw), F32)],
        scratch_shapes=[pltpu.VMEM((n_mem, mw), F32), pltpu.VMEM((n_mem, mw), F32)],
        compiler_params=_cp(("arbitrary",), VMEM_BIG),
    )(dxm, dxm, proj, proj, proj, convw, memkv, wout)


def _rel_tables():
    qi = np.arange(BLOCK, dtype=np.int32)[:, None]
    kj = np.arange(2 * BLOCK, dtype=np.int32)[None, :]
    dist = qi + BLOCK - kj
    inw = (dist >= 0) & (dist < BLOCK)
    max_exact = REL_BUCKETS // 2
    dd = np.maximum(np.maximum(dist, 0), 1).astype(np.float32)
    large = max_exact + (np.log(dd / np.float32(max_exact)) / np.float32(math.log(REL_MAX_DIST / max_exact))
                         * np.float32(REL_BUCKETS - max_exact)).astype(np.int32)
    large = np.minimum(large, REL_BUCKETS - 1)
    bucket = np.where(np.maximum(dist, 0) < max_exact, np.maximum(dist, 0), large)
    return np.where(inw, bucket, -1).astype(np.int32)


def _bias_tables(rel_bias, sinks, name):
    bucket_t = jnp.asarray(_rel_tables().T)

    def body(rb_ref, sk_ref, bk_ref, bias_ref, sink_ref):
        bk = bk_ref[...]
        prev = lax.broadcasted_iota(jnp.int32, bk.shape, 0) < BLOCK
        for h in range(N_KV_HEADS):
            for j in range(GROUP):
                head = GROUP * h + j
                acc = jnp.full(bk.shape, NEG, F32)
                for b in range(REL_BUCKETS):
                    acc = jnp.where(bk == b, rb_ref[b, head], acc)
                bias_ref[h, :, j * BLOCK:(j + 1) * BLOCK] = acc
                bias_ref[N_KV_HEADS + h, :, j * BLOCK:(j + 1) * BLOCK] = jnp.where(prev, NEG, acc)
                sink_ref[h, :, j * BLOCK:(j + 1) * BLOCK] = jnp.full((8, BLOCK), sk_ref[0, head], F32)

    smem = pl.BlockSpec(memory_space=pltpu.SMEM)
    return pl.pallas_call(
        body, name=name,
        in_specs=[smem, smem, pl.BlockSpec(memory_space=pltpu.VMEM)],
        out_specs=[pl.BlockSpec(memory_space=pltpu.VMEM), pl.BlockSpec(memory_space=pltpu.VMEM)],
        out_shape=[jax.ShapeDtypeStruct((2 * N_KV_HEADS, 2 * BLOCK, GROUP * BLOCK), F32),
                   jax.ShapeDtypeStruct((N_KV_HEADS, 8, GROUP * BLOCK), F32)],
    )(rel_bias, sinks.reshape(1, N_Q_HEADS), bucket_t)


def _bias_bwd(dbias_a, dbias_b, name):
    bucket_t = jnp.asarray(_rel_tables().T)

    def body(da_ref, db_ref, bk_ref, o_ref):
        bk = bk_ref[...]
        ri = lax.broadcasted_iota(jnp.int32, (REL_BUCKETS, 128), 0)
        ci = lax.broadcasted_iota(jnp.int32, (REL_BUCKETS, 128), 1)
        out = jnp.zeros((REL_BUCKETS, 128), F32)
        for h in range(N_KV_HEADS):
            dsum = da_ref[h] + db_ref[h]
            for j in range(GROUP):
                head = GROUP * h + j
                seg = dsum[:, j * BLOCK:(j + 1) * BLOCK]
                for b in range(REL_BUCKETS):
                    val = jnp.sum(jnp.sum(jnp.where(bk == b, seg, 0.0), axis=0, keepdims=True), axis=1, keepdims=True)
                    out = out + jnp.where((ri == b) & (ci == head), val, 0.0)
        o_ref[...] = out

    vm = pl.BlockSpec(memory_space=pltpu.VMEM)
    return pl.pallas_call(
        body, name=name, in_specs=[vm, vm, vm], out_specs=vm,
        out_shape=jax.ShapeDtypeStruct((REL_BUCKETS, 128), F32),
    )(dbias_a, dbias_b, bucket_t)


def _stack_members(ref, r0, width):
    blk = ref[pl.ds(r0, BLOCK), 0:GROUP * width]
    return jnp.concatenate([blk[:, j * width:(j + 1) * width] for j in range(GROUP)], axis=0)


def _mix_b_fwd(x, qp, kv, bias, sinkt, memkv, layer, wout, tm, name):
    t, d = x.shape
    n_mem = memkv.shape[0]
    mw = N_MEM_HEADS * HEAD_DIM
    qw = d - mw
    kw = N_KV_HEADS * HEAD_DIM
    nb = tm // BLOCK
    rows = GROUP * BLOCK

    def body(x_ref, q_ref, kv_ref, kvh_ref, bias_ref, sink_ref, mkv_ref, wo_ref, xo_ref, kvx, ytok):
        i = pl.program_id(0)
        kvx[0:BLOCK, :] = kvh_ref[...]
        kvx[BLOCK:BLOCK + tm, :] = kv_ref[...]
        colh = _col_head(kw)

        def blk(b, carry):
            r0 = pl.multiple_of(b * BLOCK, BLOCK)
            win = kvx[pl.ds(r0, 2 * BLOCK), :]
            kwin = win[:, 0:kw] * SCALE
            vwin = win[:, kw:2 * kw]
            qs = _stack_members(q_ref, r0, kw)
            first = ((i == 0) & (b == 0)).astype(jnp.int32) * N_KV_HEADS
            heads = range(N_KV_HEADS)
            ss = [lax.dot_general(_keep_head(kwin, colh, h), qs, NT, preferred_element_type=F32) for h in heads]
            ps = [_softmax_cols(ss[h] + bias_ref[first + h], sink_ref[h][0:1, :])[0].astype(BF16) for h in heads]
            o = _add4([lax.dot_general(ps[h], _keep_head(vwin, colh, h), TN, preferred_element_type=F32)
                       for h in heads])
            for j in range(GROUP):
                ytok[pl.ds(r0, BLOCK), j * kw:(j + 1) * kw] = o[j * BLOCK:(j + 1) * BLOCK].astype(BF16)
            return carry

        for b_static in range(nb):
            blk(b_static, 0)
        mkv = mkv_ref[...]
        ymem = _mem_attn_fwd(q_ref[:, qw:d], mkv[:, 0:mw], mkv[:, mw:2 * mw])
        cat = jnp.concatenate([ytok[...], ymem.astype(BF16)], axis=1)
        xo_ref[...] = x_ref[...] + jnp.dot(cat, wo_ref[...], preferred_element_type=F32)

    return pl.pallas_call(
        body, name=name, grid=(t // tm,),
        in_specs=[_row_spec(tm, d), _row_spec(tm, d), _row_spec(tm, 2 * kw), _halo_prev_spec(BLOCK, 2 * kw, tm),
                  _const_spec((2 * N_KV_HEADS, 2 * BLOCK, rows)), _const_spec((N_KV_HEADS, 8, rows)),
                  pl.BlockSpec((n_mem, 2 * mw), lambda i: (0, layer)), _const_spec((d, d))],
        out_specs=_row_spec(tm, d),
        out_shape=jax.ShapeDtypeStruct((t, d), F32),
        scratch_shapes=[pltpu.VMEM((tm + BLOCK, 2 * kw), BF16), pltpu.VMEM((tm, qw), BF16)],
        compiler_params=_cp(("parallel",), VMEM_BIG),
    )(x, qp, kv, kv, bias, sinkt, memkv, wout)


def _mix_b_bwd(dxm, qp, kv, bias, sinkt, memkv, layer, wout, tm, name):
    t, d = dxm.shape
    n_mem = memkv.shape[0]
    mw = N_MEM_HEADS * HEAD_DIM
    qw = d - mw
    kw = N_KV_HEADS * HEAD_DIM
    nb = tm // BLOCK
    nt = t // tm
    rows = GROUP * BLOCK

    def body(dx_ref, q_ref, kv_ref, kvh_ref, bias_ref, sink_ref, mkv_ref, wo_ref,
             dq_ref, cat_ref, dkv_ref, dkvh_ref, dbias_ref, dsink_ref, dmkv_ref,
             kvx, dkvx, dcat_s, dmk_acc, dmv_acc):
        i = pl.program_id(0)

        @pl.when(i == 0)
        def _():
            dbias_ref[...] = jnp.zeros_like(dbias_ref)
            dsink_ref[...] = jnp.zeros_like(dsink_ref)
            dmk_acc[...] = jnp.zeros_like(dmk_acc)
            dmv_acc[...] = jnp.zeros_like(dmv_acc)

        kvx[0:BLOCK, :] = kvh_ref[...]
        kvx[BLOCK:BLOCK + tm, :] = kv_ref[...]
        dkvx[...] = jnp.zeros_like(dkvx)
        dcat_s[...] = lax.dot_general(dx_ref[...].astype(BF16), wo_ref[...], NT,
                                      preferred_element_type=F32).astype(BF16)
        colh = _col_head(kw)
        lane8 = lax.broadcasted_iota(jnp.int32, (8, 128), 1)

        def blk(b, carry):
            r0 = pl.multiple_of(b * BLOCK, BLOCK)
            win = kvx[pl.ds(r0, 2 * BLOCK), :]
            kwin = win[:, 0:kw] * SCALE
            vwin = win[:, kw:2 * kw]
            qs = _stack_members(q_ref, r0, kw)
            dos = _stack_members(dcat_s, r0, kw)
            first = ((i == 0) & (b == 0)).astype(jnp.int32) * N_KV_HEADS
            heads = range(N_KV_HEADS)
            khs = [_keep_head(kwin, colh, h) for h in heads]
            vhs = [_keep_head(vwin, colh, h) for h in heads]
            ss = [lax.dot_general(khs[h], qs, NT, preferred_element_type=F32) for h in heads]
            dps = [lax.dot_general(vhs[h], dos, NT, preferred_element_type=F32) for h in heads]
            dsink = jnp.zeros((8, 128), F32)
            pbs, dsbs = [], []
            for h in heads:
                p, sinkp = _softmax_cols(ss[h] + bias_ref[first + h], sink_ref[h][0:1, :])
                delta = jnp.sum(p * dps[h], axis=0, keepdims=True)
                ds = p * (dps[h] - delta)
                dbias_ref[h] += ds
                sd = sinkp * delta
                for j in range(GROUP):
                    val = -jnp.sum(sd[:, j * BLOCK:(j + 1) * BLOCK], axis=1, keepdims=True)
                    dsink = dsink + jnp.where(lane8 == 4 * j + h, val, 0.0)
                pbs.append(p.astype(BF16))
                dsbs.append(ds.astype(BF16))
            y = _add4([lax.dot_general(pbs[h], vhs[h], TN, preferred_element_type=F32) for h in heads])
            dq = _add4([lax.dot_general(dsbs[h], khs[h], TN, preferred_element_type=F32) for h in heads])
            dk = _add4([jnp.where(colh == h, jnp.dot(dsbs[h], qs, preferred_element_type=F32) * SCALE, 0.0)
                        for h in heads])
            dv = _add4([jnp.where(colh == h, jnp.dot(pbs[h], dos, preferred_element_type=F32), 0.0) for h in heads])
            for j in range(GROUP):
                cat_ref[pl.ds(r0, BLOCK), j * kw:(j + 1) * kw] = y[j * BLOCK:(j + 1) * BLOCK].astype(BF16)
                dq_ref[pl.ds(r0, BLOCK), j * kw:(j + 1) * kw] = dq[j * BLOCK:(j + 1) * BLOCK].astype(BF16)
            dsink_ref[...] += dsink
            dkvx[pl.ds(r0, 2 * BLOCK), :] += jnp.concatenate([dk, dv], axis=1)
            return carry

        for b_static in range(nb):
            blk(b_static, 0)
        dkvh_ref[0] = dkvx[0:BLOCK, :]
        dkv_ref[...] = dkvx[BLOCK:BLOCK + tm, :]

        mkv = mkv_ref[...]
        ymem, dqm, dmk, dmv = _mem_attn_bwd(q_ref[:, qw:d], dcat_s[:, qw:d], mkv[:, 0:mw], mkv[:, mw:2 * mw])
        cat_ref[:, qw:d] = ymem.astype(BF16)
        dq_ref[:, qw:d] = dqm.astype(BF16)
        dmk_acc[...] += dmk
        dmv_acc[...] += dmv

        @pl.when(i == nt - 1)
        def _():
            dmkv_ref[...] = jnp.concatenate([dmk_acc[...], dmv_acc[...]], axis=1)

    return pl.pallas_call(
        body, name=name, grid=(nt,),
        in_specs=[_row_spec(tm, d), _row_spec(tm, d), _row_spec(tm, 2 * kw), _halo_prev_spec(BLOCK, 2 * kw, tm),
                  _const_spec((2 * N_KV_HEADS, 2 * BLOCK, rows)), _const_spec((N_KV_HEADS, 8, rows)),
                  pl.BlockSpec((n_mem, 2 * mw), lambda i: (0, layer)), _const_spec((d, d))],
        out_specs=[_row_spec(tm, d), _row_spec(tm, d), _row_spec(tm, 2 * kw),
                   pl.BlockSpec((1, BLOCK, 2 * kw), lambda i: (i, 0, 0)),
                   pl.BlockSpec((N_KV_HEADS, 2 * BLOCK, rows), lambda i: (0, 0, 0)),
                   pl.BlockSpec((8, 128), lambda i: (0, 0)),
                   pl.BlockSpec((n_mem, 2 * mw), lambda i: (0, 0))],
        out_shape=[jax.ShapeDtypeStruct((t, d), BF16), jax.ShapeDtypeStruct((t, d), BF16),
                   jax.ShapeDtypeStruct((t, 2 * kw), F32), jax.ShapeDtypeStruct((nt, BLOCK, 2 * kw), F32),
                   jax.ShapeDtypeStruct((N_KV_HEADS, 2 * BLOCK, rows), F32), jax.ShapeDtypeStruct((8, 128), F32),
                   jax.ShapeDtypeStruct((n_mem, 2 * mw), F32)],
        scratch_shapes=[pltpu.VMEM((tm + BLOCK, 2 * kw), BF16), pltpu.VMEM((tm + BLOCK, 2 * kw), F32),
                        pltpu.VMEM((tm, d), BF16), pltpu.VMEM((n_mem, mw), F32), pltpu.VMEM((n_mem, mw), F32)],
        compiler_params=_cp(("arbitrary",), VMEM_BIG),
    )(dxm, qp, kv, kv, bias, sinkt, memkv, wout)


def _kv_assemble(main_a, halo_a, main_b, halo_b, tm, name):
    t, n = main_a.shape
    nt = t // tm

    def body(ma_ref, ha_ref, mb_ref, hb_ref, o_ref):
        i = pl.program_id(0)
        s = ma_ref[...] + mb_ref[...]
        tail = jnp.where(i == nt - 1, 0.0, ha_ref[0] + hb_ref[0])
        o_ref[...] = jnp.concatenate([s[0:tm - BLOCK], s[tm - BLOCK:] + tail], axis=0).astype(BF16)

    halo_spec = pl.BlockSpec((1, BLOCK, n), lambda i: (jnp.minimum(i + 1, nt - 1), 0, 0))
    return pl.pallas_call(
        body, name=name, grid=(nt,),
        in_specs=[_row_spec(tm, n), halo_spec, _row_spec(tm, n), halo_spec],
        out_specs=_row_spec(tm, n),
        out_shape=jax.ShapeDtypeStruct((t, n), BF16),
        compiler_params=_cp(("parallel",)),
    )(main_a, halo_a, main_b, halo_b)


def _adam_math(w, g, m, v):
    m2 = ADAM_B1 * m + (1.0 - ADAM_B1) * g
    v2 = ADAM_B2 * v + (1.0 - ADAM_B2) * (g * g)
    m_hat = m2 / (1.0 - ADAM_B1 ** ADAM_STEP)
    v_hat = v2 / (1.0 - ADAM_B2 ** ADAM_STEP)
    delta = -ADAM_LR * (m_hat / (jnp.sqrt(v_hat) + ADAM_EPS) + ADAM_WD * w)
    return delta, m2, v2


def _adamw_sharded(w, land, m, v, name):
    nl, r, c = w.shape
    tr = r
    for cand in (512, 256, 128, 64, 32, 16):
        if r % cand == 0:
            tr = cand
            break

    def body(w_ref, a_ref, m_ref, v_ref, g_ref, d_ref, mo_ref, vo_ref):
        g = a_ref[0, 0].astype(F32) + a_ref[1, 0].astype(F32)
        for k in range(1, N_CHIPS):
            g = g + (a_ref[2 * k, 0].astype(F32) + a_ref[2 * k + 1, 0].astype(F32))
        delta, m2, v2 = _adam_math(w_ref[0], g, m_ref[0], v_ref[0])
        g_ref[0] = g
        d_ref[0] = delta
        mo_ref[0] = m2
        vo_ref[0] = v2

    rs = pl.BlockSpec((1, tr, c), lambda l, i: (l, i, 0))
    ps = pl.BlockSpec((2 * N_CHIPS, 1, tr, c), lambda l, i: (0, l, i, 0))
    sd = jax.ShapeDtypeStruct((nl, r, c), F32)
    return pl.pallas_call(
        body, name=name, grid=(nl, r // tr),
        in_specs=[rs, ps, rs, rs], out_specs=[rs, rs, rs, rs], out_shape=[sd, sd, sd, sd],
        compiler_params=_cp(("parallel", "parallel")),
    )(w, land, m, v)


def _adamw_packed(w, g, m, v, name):
    def body(w_ref, g_ref, m_ref, v_ref, d_ref, mo_ref, vo_ref):
        delta, m2, v2 = _adam_math(w_ref[...], g_ref[...], m_ref[...], v_ref[...])
        d_ref[...] = delta
        mo_ref[...] = m2
        vo_ref[...] = v2

    vm = pl.BlockSpec(memory_space=pltpu.VMEM)
    sd = jax.ShapeDtypeStruct(w.shape, F32)
    return pl.pallas_call(body, name=name, in_specs=[vm] * 4, out_specs=[vm] * 3, out_shape=[sd] * 3)(w, g, m, v)


def _place():
    return lax.axis_index("x"), lax.axis_index("y"), lax.axis_index("c")


def _hbm(a):
    return pltpu.with_memory_space_constraint(a, pltpu.HBM)


def _peers(x, y, c, both_cores):
    chips = [(1 - x, y), (x, 1 - y), (1 - x, 1 - y)]
    if not both_cores:
        return [(px, py, c) for px, py in chips]
    return [(px, py, pc) for px, py in chips for pc in (c, 1 - c)] + [(x, y, 1 - c)]


def _chip_copy(src, land, gather, layer, chip_src, slot, send_sem, recv_sem, peer):
    s = src if gather else src.at[chip_src]
    d = land.at[slot] if layer is None else land.at[slot, layer]
    return pltpu.make_async_remote_copy(src_ref=s, dst_ref=d, send_sem=send_sem, recv_sem=recv_sem,
                                        device_id=peer, device_id_type=MESH)


def _exchange_start(srcs, lands, gather, layers, both_cores, after, name):
    n = len(srcs)
    npeer = 7 if both_cores else 3
    hbm = pl.BlockSpec(memory_space=pltpu.HBM)
    sem = pl.BlockSpec(memory_space=pltpu.SEMAPHORE)

    def body(*refs):
        ins, lds = refs[:n], refs[n:2 * n]
        first_out = 2 * n + len(after)
        send_sems, recv_sems, token = refs[first_out], refs[first_out + 1], refs[-1]
        x, y, c = _place()
        slot = 2 * x + y if gather else 2 * (2 * x + y) + c
        for t in range(n):
            for r, peer in enumerate(_peers(x, y, c, both_cores)):
                _chip_copy(ins[t], lds[t], gather, layers[t], 2 * peer[0] + peer[1], slot,
                           send_sems.at[npeer * t + r], recv_sems.at[npeer * t + r], peer).start()
        token[...] = jnp.zeros_like(token)

    both = list(srcs) + list(lands)
    outs = pl.pallas_call(
        body, name=name, in_specs=[hbm] * (2 * n) + [pl.BlockSpec(memory_space=pl.ANY)] * len(after),
        out_specs=(sem, sem, *([hbm] * (2 * n)), pl.BlockSpec(memory_space=pltpu.VMEM)),
        out_shape=(pltpu.SemaphoreType.DMA((npeer * n,)), pltpu.SemaphoreType.DMA((npeer * n,)),
                   *[pltpu.HBM(a.shape, a.dtype) for a in both], jax.ShapeDtypeStruct((8, 128), F32)),
        input_output_aliases={t: 2 + t for t in range(2 * n)},
        compiler_params=_cp(has_side_effects=pltpu.SideEffectType.DATAFLOW_SIDE_EFFECTING),
    )(*[_hbm(a) for a in both], *after)
    return dict(send=outs[0], recv=outs[1], srcs=list(outs[2:2 + n]), lands=list(outs[2 + n:2 + 2 * n]),
                token=outs[-1], gather=gather, layers=list(layers), both_cores=both_cores)


def _exchange_wait(groups, lands, land_ids, after, name):
    flat = [s for g in groups for s in g["srcs"]]
    ns, nl, ng, na = len(flat), len(lands), len(groups), len(after)
    hbm = pl.BlockSpec(memory_space=pltpu.HBM)
    sem = pl.BlockSpec(memory_space=pltpu.SEMAPHORE)

    def body(*refs):
        srcs, lds = refs[:ns], refs[ns:ns + nl]
        sems = refs[ns + nl:ns + nl + 2 * ng]
        x, y, c = _place()
        k = 0
        for gi, g in enumerate(groups):
            peers = _peers(x, y, c, g["both_cores"])
            for t in range(len(g["srcs"])):
                for r, peer in enumerate(peers):
                    cp = _chip_copy(srcs[k], lds[land_ids[gi][t]], g["gather"], g["layers"][t], 0, 0,
                                    sems[2 * gi].at[len(peers) * t + r], sems[2 * gi + 1].at[len(peers) * t + r], peer)
                    cp.wait_send()
                    cp.wait_recv()
                k += 1

    both = flat + list(lands)
    sem_args = [a for g in groups for a in (g["send"], g["recv"])]
    outs = pl.pallas_call(
        body, name=name,
        in_specs=[hbm] * (ns + nl) + [sem] * (2 * ng) + [pl.BlockSpec(memory_space=pl.ANY)] * na,
        out_specs=[hbm] * (ns + nl),
        out_shape=[pltpu.HBM(a.shape, a.dtype) for a in both],
        input_output_aliases={t: t for t in range(ns + nl)},
        compiler_params=_cp(has_side_effects=pltpu.SideEffectType.DATAFLOW_SIDE_EFFECTING),
    )(*both, *sem_args, *after)
    return list(outs[ns:])


def _core_fill(lands, layers, name):
    n = len(lands)
    hbm = pl.BlockSpec(memory_space=pltpu.HBM)

    def body(*refs):
        ins = refs[:n]
        send_sems, recv_sems = refs[2 * n:]
        x, y, c = _place()
        copies = []
        for t in range(n):
            for k in range(N_CHIPS):
                mine = ins[t].at[2 * k + c] if layers[t] is None else ins[t].at[2 * k + c, layers[t]]
                cp = pltpu.make_async_remote_copy(
                    src_ref=mine, dst_ref=mine, send_sem=send_sems.at[N_CHIPS * t + k],
                    recv_sem=recv_sems.at[N_CHIPS * t + k], device_id=(x, y, 1 - c), device_id_type=MESH)
                cp.start()
                copies.append(cp)
        for cp in copies:
            cp.wait()

    return pl.pallas_call(
        body, name=name, in_specs=[hbm] * n, out_specs=[hbm] * n,
        out_shape=[jax.ShapeDtypeStruct(a.shape, a.dtype) for a in lands],
        input_output_aliases={t: t for t in range(n)},
        scratch_shapes=[pltpu.SemaphoreType.DMA((N_CHIPS * n,)), pltpu.SemaphoreType.DMA((N_CHIPS * n,))],
        compiler_params=_cp(has_side_effects=True),
    )(*lands)


def _all_reduce_packed(pack, name):
    r, c = pack.shape
    vm = pl.BlockSpec(memory_space=pltpu.VMEM)

    def body(p_ref, sum_ref, slots, send_sems, recv_sems):
        x, y, cc = _place()
        me = 4 * x + 2 * y + cc
        slots[me] = p_ref[...]
        copies = []
        for rel in range(1, 8):
            px = 1 - x if rel & 4 else x
            py = 1 - y if rel & 2 else y
            pc = 1 - cc if rel & 1 else cc
            cp = pltpu.make_async_remote_copy(
                src_ref=p_ref, dst_ref=slots.at[me], send_sem=send_sems.at[rel - 1], recv_sem=recv_sems.at[rel - 1],
                device_id=(px, py, pc), device_id_type=MESH)
            cp.start()
            copies.append(cp)
        for cp in copies:
            cp.wait()
        total = slots[0]
        for k in range(1, 8):
            total = total + slots[k]
        sum_ref[...] = total

    return pl.pallas_call(
        body, name=name, in_specs=[vm], out_specs=vm, out_shape=jax.ShapeDtypeStruct((r, c), F32),
        scratch_shapes=[pltpu.VMEM((8, r, c), F32), pltpu.SemaphoreType.DMA((7,)), pltpu.SemaphoreType.DMA((7,))],
        compiler_params=_cp(has_side_effects=True),
    )(pack)


def _pack(items):
    rows = []
    for a in items:
        flat = a.astype(F32).reshape(-1)
        pad = (-flat.shape[0]) % PACK_W
        rows.append(jnp.pad(flat, (0, pad)).reshape(-1, PACK_W))
    out = jnp.concatenate(rows, axis=0)
    pad_r = (-out.shape[0]) % 8
    return jnp.pad(out, ((0, pad_r), (0, 0)))


def _unpack(pack, shapes):
    outs, row = [], 0
    for s in shapes:
        n = int(np.prod(s))
        nr = -(-n // PACK_W)
        outs.append(pack[row:row + nr].reshape(-1)[:n].reshape(s))
        row += nr
    return outs


def _heads_to_member_major(w, axis):
    shp = w.shape
    pre, post = shp[:axis], shp[axis + 1:]
    w4 = w.reshape(pre + (N_KV_HEADS, GROUP, HEAD_DIM) + post)
    w4 = jnp.swapaxes(w4, len(pre), len(pre) + 1)
    return w4.reshape(shp)


def _heads_to_kv_major(w, axis):
    shp = w.shape
    pre, post = shp[:axis], shp[axis + 1:]
    w4 = w.reshape(pre + (GROUP, N_KV_HEADS, HEAD_DIM) + post)
    w4 = jnp.swapaxes(w4, len(pre), len(pre) + 1)
    return w4.reshape(shp)


def kernel(x, mem, norm_mix, norm_ffn, a_w_in, a_conv_w, a_w_out, kv_norm, w_kv, b_w_q, b_sinks, b_w_out, rel_bias, mem_norm, w_mem_kv, w_gate, w_up, w_down, final_norm, loss_target, m_norm_mix, m_norm_ffn, m_a_w_in, m_a_conv_w, m_a_w_out, m_kv_norm, m_w_kv, m_b_w_q, m_b_sinks, m_b_w_out, m_rel_bias, m_mem_norm, m_w_mem_kv, m_w_gate, m_w_up, m_w_down, m_final_norm, v_norm_mix, v_norm_ffn, v_a_w_in, v_a_conv_w, v_a_w_out, v_kv_norm, v_w_kv, v_b_w_q, v_b_sinks, v_b_w_out, v_rel_bias, v_mem_norm, v_w_mem_kv, v_w_gate, v_w_up, v_w_down, v_final_norm):
    t, d = x.shape[1], x.shape[2]
    tm = 512 if t % 512 == 0 and t >= 2048 else 256
    tl = 2 * tm if t % (2 * tm) == 0 else tm
    x0 = x.reshape(t, d)
    target = loss_target.reshape(t, d)
    mem2 = mem.reshape(mem.shape[1], d)
    n_mem = mem2.shape[0]
    ax, ay, ac = _place()
    chip = 2 * ax + ay
    cwid = a_conv_w.shape[2] * N_CHIPS
    qw = N_Q_HEADS * HEAD_DIM
    nq = N_CHIPS

    def own_slot(piece):
        return lax.dynamic_update_slice(lax.empty((nq,) + piece.shape, piece.dtype), piece[None],
                                        (chip,) + (0,) * piece.ndim)

    def mixer_shards(i):
        if i < N_A:
            shards = [a_w_in[i], a_w_out[i]] + ([w_mem_kv] if i == 0 else [])
        else:
            j = i - N_A
            shards = [b_w_q[j], b_w_out[j]] + ([w_kv] if j == 0 else [])
        return [a.astype(BF16) for a in shards]

    def ffn_shards(i):
        return [w_gate[i].T.astype(BF16), w_up[i].T.astype(BF16), w_down[i].astype(BF16)]

    conv_pad = jnp.pad(a_conv_w, ((0, 0), (0, 8 - a_conv_w.shape[1]), (0, (-a_conv_w.shape[2]) % 128)))
    first = mixer_shards(0)
    group_shards = {"0a": first[0:1], "0b": first[1:] + [conv_pad], "0f": ffn_shards(0)}
    for i in range(1, DEPTH):
        group_shards[str(i)] = ffn_shards(i) + mixer_shards(i)
    gathers, prev_tok = {}, []
    for key, shards in group_shards.items():
        gathers[key] = _exchange_start(shards, [own_slot(a) for a in shards], True, [None] * len(shards), False,
                                       prev_tok, "gather_start_" + key)
        prev_tok = [gathers[key]["token"]]

    def rows_full(g):
        return g.reshape((-1,) + g.shape[2:])

    def cols_full(g):
        return jnp.transpose(g, (1, 0, 2)).reshape(g.shape[1], -1)

    def landed_weights(key, after):
        g = gathers[key]
        return _exchange_wait([g], g["lands"], [list(range(len(g["lands"])))], after, "gather_wait_" + key)

    def mixer_weights(i, got):
        w_first, w_out = (cols_full(got[0]) if i < N_A else rows_full(got[0])), rows_full(got[1])
        if i >= N_A:
            w_first = jnp.concatenate([_heads_to_member_major(w_first[:, :qw], 1), w_first[:, qw:]], axis=1)
            w_out = jnp.concatenate([_heads_to_member_major(w_out[:qw, :], 0), w_out[qw:, :]], axis=0)
        return dict(w_first=w_first, w_out=w_out, extra=got[2] if len(got) > 2 else None)

    def ffn_weights(got):
        return dict(wg=rows_full(got[0]), wu=rows_full(got[1]), wd=rows_full(got[2]))

    bias, sinkt = [], []
    for j in range(2):
        bj, sj = _bias_tables(rel_bias, b_sinks[j], "bias_tables")
        bias.append(bj)
        sinkt.append(sj)

    ws = []
    xs, xmids, projs, gates, ups = [x0], [], [], [], []
    kv = memkv = wmem = wkv = None
    for i in range(DEPTH):
        xin = xs[-1]
        if i == 0:
            w = dict(w_first=cols_full(landed_weights("0a", prev_tok)[0]))
        else:
            got = landed_weights(str(i), [xin])
            w = dict(mixer_weights(i, got[3:]), **ffn_weights(got[0:3]))
        ws.append(w)
        gm = norm_mix[i].reshape(1, d)
        if i < N_A:
            proj = _norm_mm(xin, gm, w["w_first"], tl, "proj_a")
            if i == 0:
                got = landed_weights("0b", [proj])
                w["w_out"] = rows_full(got[0])
                full_mem = jnp.swapaxes(got[1], 0, 1).reshape(DEPTH, d, -1)
                wmem = jnp.transpose(full_mem, (1, 0, 2)).reshape(d, -1)
                memkv = _norm_mm(mem2, mem_norm.reshape(1, d), wmem, n_mem, "mem_kv")
                taps = got[2][:, :, 0:3, 0:a_conv_w.shape[2]]
                conv_full = jnp.transpose(taps, (1, 2, 0, 3)).reshape(N_A, 3, cwid)
            xmid = _mix_a_fwd(xin, proj, conv_full[i], memkv, i, w["w_out"], tm, "mix_a_fwd")
        else:
            j = i - N_A
            if j == 0:
                wkv = rows_full(w["extra"])
                kv = _norm_mm(xin, kv_norm.reshape(1, d), wkv, tl, "proj_kv")
            proj = _norm_mm(xin, gm, w["w_first"], tl, "proj_b")
            xmid = _mix_b_fwd(xin, proj, kv, bias[j], sinkt[j], memkv, i, w["w_out"], tm, "mix_b_fwd")
        if i == 0:
            w.update(ffn_weights(landed_weights("0f", [xmid])))
        xout, gate, up = _ffn_fwd(xmid, norm_ffn[i].reshape(1, d), w["wg"], w["wu"], w["wd"], tm, "ffn_fwd")
        projs.append(proj)
        xmids.append(xmid)
        gates.append(gate)
        ups.append(up)
        xs.append(xout)

    loss_part, dx, dg_final = _final_loss(xs[-1], final_norm.reshape(1, d), target, tl, "final_loss")

    def rows_pieces(g):
        return g.astype(BF16).reshape((nq, g.shape[0] // nq) + g.shape[1:])

    def cols_pieces(g):
        return jnp.transpose(g.astype(BF16).reshape(g.shape[0], nq, g.shape[1] // nq), (1, 0, 2))

    swapped = ("w_gate", "w_up")
    stacked = dict(a_w_in=a_w_in, a_w_out=a_w_out, w_kv=w_kv[None], b_w_q=b_w_q, b_w_out=b_w_out,
                   w_mem_kv=w_mem_kv, w_gate=jnp.swapaxes(w_gate, 1, 2), w_up=jnp.swapaxes(w_up, 1, 2), w_down=w_down)
    names = list(stacked)
    land = {k: lax.empty((2 * nq,) + stacked[k].shape, BF16) for k in names}
    own = {k: [None] * stacked[k].shape[0] for k in names}
    scatters, scatter_ids = [], []

    def scatter_start(key, items, both_cores):
        keys = [k for k, _, _ in items]
        st = _exchange_start([p for _, _, p in items], [land[k] for k in keys], False, [l for _, l, _ in items],
                             both_cores, [], "scatter_start_" + key)
        for (k, l, p), ld in zip(items, st["lands"]):
            land[k] = ld
            mine_piece = lax.dynamic_index_in_dim(p, chip, 0, keepdims=False)
            if l is None:
                own[k] = [mine_piece[q] for q in range(mine_piece.shape[0])]
            else:
                own[k][l] = mine_piece
        scatters.append(st)
        scatter_ids.append([names.index(k) for k in keys])
        return st["token"][0:1, 0:1]

    g_norm_mix, g_norm_ffn = [None] * DEPTH, [None] * DEPTH
    g_conv, g_sinks = [None] * 2, [None] * 2
    dmemkv = [None] * DEPTH
    dbias, dkv_main, dkv_halo = [None] * 2, [None] * 2, [None] * 2
    g_kv_norm = None
    tok = jnp.zeros((1, 1), F32)
    for i in reversed(range(DEPTH)):
        w = ws[i]
        dxm, dgate, dup, act, h2, dgf = _ffn_bwd(dx, xmids[i], norm_ffn[i].reshape(1, d) + tok, gates[i], ups[i],
                                            w["wg"], w["wu"], w["wd"], tm // 2, "ffn_bwd")
        g_norm_ffn[i] = dgf
        g_wd = _wgrad(act, dx, 2 * tm, "wgrad_down")
        g_wg = _wgrad(dgate, h2, 2 * tm, "wgrad_gate")
        g_wu = _wgrad(dup, h2, 2 * tm, "wgrad_up")
        items = [("w_gate", i, rows_pieces(g_wg)), ("w_up", i, rows_pieces(g_wu)), ("w_down", i, rows_pieces(g_wd))]
        if i == 0:
            tok = scatter_start("0f", items, True)
            items = []
        gm = norm_mix[i].reshape(1, d)
        if i < N_A:
            dproj, cat, dcw, dmemkv[i] = _mix_a_bwd(dxm, projs[i], conv_full[i] + (tok if i == 0 else 0.0), memkv, i,
                                                    w["w_out"], tm, "mix_a_bwd")
            g_conv[i] = dcw[0:3]
            g_out = _wgrad(cat, dxm, 2 * tm, "wgrad_out")
            if i == 0:
                dmemkv_all = jnp.concatenate([a.astype(BF16) for a in dmemkv], axis=1)
                _, g_mem_norm, hmem = _mm_nt_normbwd(dmemkv_all, wmem, mem2, mem_norm.reshape(1, d),
                                                     jnp.zeros((n_mem, d), F32), n_mem, "mem_kv_bwd")
                g_wmem = _wgrad(hmem, dmemkv_all, n_mem, "wgrad_mem")
                g_wmem = jnp.transpose(g_wmem.reshape(nq, d // nq, DEPTH, -1), (0, 2, 1, 3))
                gm = gm + scatter_start("0o", [("a_w_out", 0, rows_pieces(g_out)), ("w_mem_kv", None, g_wmem)], False)
            dx, g_norm_mix[i], h = _mm_nt_normbwd(dproj, w["w_first"], xs[i], gm, dxm, tl, "proj_a_bwd")
            g_in = _wgrad(h, dproj, 2 * tm, "wgrad_in_a")
            items.append(("a_w_in", i, cols_pieces(g_in)))
            if i > 0:
                items.append(("a_w_out", i, rows_pieces(g_out)))
        else:
            j = i - N_A
            dqp, cat, dkv_main[j], dkv_halo[j], dbias[j], dsk, dmemkv[i] = _mix_b_bwd(
                dxm, projs[i], kv, bias[j], sinkt[j], memkv, i, w["w_out"], tm, "mix_b_bwd")
            g_sinks[j] = dsk[0, 0:N_Q_HEADS].reshape(GROUP, N_KV_HEADS).T.reshape(N_Q_HEADS)
            g_out = _wgrad(cat, dxm, 2 * tm, "wgrad_out")
            dx, g_norm_mix[i], h = _mm_nt_normbwd(dqp, w["w_first"], xs[i], gm, dxm, tl, "proj_b_bwd")
            g_q = _wgrad(h, dqp, 2 * tm, "wgrad_in_b")
            g_q = jnp.concatenate([_heads_to_kv_major(g_q[:, :qw], 1), g_q[:, qw:]], axis=1)
            g_out = jnp.concatenate([_heads_to_kv_major(g_out[:qw, :], 0), g_out[qw:, :]], axis=0)
            items += [("b_w_q", j, rows_pieces(g_q)), ("b_w_out", j, rows_pieces(g_out))]
            if j == 0:
                dkv = _kv_assemble(dkv_main[0], dkv_halo[0], dkv_main[1], dkv_halo[1], tm, "kv_assemble")
                dx, g_kv_norm, hkv = _mm_nt_normbwd(dkv, wkv, xs[i], kv_norm.reshape(1, d), dx, tl, "proj_kv_bwd")
                items.append(("w_kv", 0, rows_pieces(_wgrad(hkv, dkv, 2 * tm, "wgrad_kv"))))
        tok = scatter_start(str(i) if i else "0i", items, i > 0)
    grad_x = dx.reshape(x.shape)
    g_rel = _bias_bwd(dbias[0], dbias[1], "bias_bwd")[:, 0:N_Q_HEADS]

    small_shapes = [(DEPTH, d), (DEPTH, d), (d,), (d,), (d,), (2, N_Q_HEADS), (REL_BUCKETS, N_Q_HEADS),
                    (N_A, 3, cwid), ()]
    small = _pack([jnp.concatenate(g_norm_mix, axis=0), jnp.concatenate(g_norm_ffn, axis=0), g_kv_norm, g_mem_norm,
                   dg_final, jnp.stack(g_sinks), g_rel, jnp.stack(g_conv), loss_part[0, 0]])
    small_sum = _all_reduce_packed(small, "reduce_small")
    (gs_norm_mix, gs_norm_ffn, gs_kv_norm, gs_mem_norm, gs_final, gs_sinks, gs_rel, gs_conv_full, loss) = _unpack(
        small_sum, small_shapes)
    cq = cwid // N_CHIPS
    gs_conv = lax.dynamic_slice_in_dim(gs_conv_full, chip * cq, cq, axis=2)

    late = ("a_w_in", "a_w_out", "w_mem_kv")
    landed = dict(zip(names, _exchange_wait(scatters[:-2], [land[k] for k in names], scatter_ids[:-2], [small_sum],
                                            "scatter_wait_a")))

    def with_own(k, ld):
        return lax.dynamic_update_slice(ld, jnp.stack(own[k])[None], (2 * chip + ac,) + (0,) * (ld.ndim - 1))

    weights = dict(norm_mix=norm_mix, norm_ffn=norm_ffn, a_w_in=a_w_in, a_conv_w=a_conv_w, a_w_out=a_w_out,
                   kv_norm=kv_norm, w_kv=w_kv, b_w_q=b_w_q, b_sinks=b_sinks, b_w_out=b_w_out, rel_bias=rel_bias,
                   mem_norm=mem_norm, w_mem_kv=w_mem_kv, w_gate=w_gate, w_up=w_up, w_down=w_down,
                   final_norm=final_norm)
    moms = dict(norm_mix=m_norm_mix, norm_ffn=m_norm_ffn, a_w_in=m_a_w_in, a_conv_w=m_a_conv_w, a_w_out=m_a_w_out,
                kv_norm=m_kv_norm, w_kv=m_w_kv, b_w_q=m_b_w_q, b_sinks=m_b_sinks, b_w_out=m_b_w_out,
                rel_bias=m_rel_bias, mem_norm=m_mem_norm, w_mem_kv=m_w_mem_kv, w_gate=m_w_gate, w_up=m_w_up,
                w_down=m_w_down, final_norm=m_final_norm)
    vars_ = dict(norm_mix=v_norm_mix, norm_ffn=v_norm_ffn, a_w_in=v_a_w_in, a_conv_w=v_a_conv_w, a_w_out=v_a_w_out,
                 kv_norm=v_kv_norm, w_kv=v_w_kv, b_w_q=v_b_w_q, b_sinks=v_b_sinks, b_w_out=v_b_w_out,
                 rel_bias=v_rel_bias, mem_norm=v_mem_norm, w_mem_kv=v_w_mem_kv, w_gate=v_w_gate, w_up=v_w_up,
                 w_down=v_w_down, final_norm=v_final_norm)
    order = list(weights)
    grads, deltas, new_m, new_v = {}, {}, {}, {}
    def adamw(k, ld):
        shp, stk = weights[k].shape, ld.shape[1:]
        view = (lambda a: jnp.swapaxes(a, 1, 2)) if k in swapped else (lambda a: a.reshape(stk))
        back = (lambda a: jnp.swapaxes(a, 1, 2)) if k in swapped else (lambda a: a.reshape(shp))
        outs = _adamw_sharded(view(weights[k]), ld, view(moms[k]), view(vars_[k]), "adamw_" + k)
        grads[k], deltas[k], new_m[k], new_v[k] = [back(o) for o in outs]

    for k in names:
        if k not in late:
            adamw(k, with_own(k, landed[k]))
    late_ids = [[late.index(names[t]) for t in ids] for ids in scatter_ids[-2:]]
    late_landed = _exchange_wait(scatters[-2:], [landed[k] for k in late], late_ids, [deltas["w_down"]], "scatter_wait_b")
    filled = _core_fill([with_own(k, ld) for k, ld in zip(late, late_landed)], [0, 0, None], "fill_cores")
    for k, ld in zip(late, filled):
        adamw(k, ld)
    small_names = ["norm_mix", "norm_ffn", "kv_norm", "mem_norm", "final_norm", "b_sinks", "rel_bias", "a_conv_w"]
    small_g = [gs_norm_mix, gs_norm_ffn, gs_kv_norm, gs_mem_norm, gs_final, gs_sinks, gs_rel, gs_conv]
    shapes = [weights[k].shape for k in small_names]
    dl_p, m_p, v_p = _adamw_packed(_pack([weights[k] for k in small_names]), _pack(small_g),
                                   _pack([moms[k] for k in small_names]), _pack([vars_[k] for k in small_names]),
                                   "adamw_small")
    for k, g, dl, m2, v2 in zip(small_names, small_g, _unpack(dl_p, shapes), _unpack(m_p, shapes), _unpack(v_p, shapes)):
        grads[k], deltas[k], new_m[k], new_v[k] = g.reshape(weights[k].shape), dl, m2, v2

    return (loss, grad_x, *[grads[k] for k in order], *[deltas[k] for k in order],
            *[new_m[k] for k in order], *[new_v[k] for k in order])
```

```python
import functools
import math

import numpy as np
import jax
import jax.numpy as jnp
from jax import lax
from jax.experimental import pallas as pl
from jax.experimental.pallas import tpu as pltpu

F32 = jnp.float32
BF16 = jnp.bfloat16
MESH = pl.DeviceIdType.MESH

EPS = 1e-5
HEAD_DIM = 64
N_MEM_HEADS = 4
N_KV_HEADS = 4
GROUP = 3
N_Q_HEADS = N_KV_HEADS * GROUP
BLOCK = 128
REL_BUCKETS = 32
REL_MAX_DIST = 128
SCALE = HEAD_DIM ** -0.5
NEG = -1e30
N_CHIPS = 4
N_A = 2
DEPTH = 4

ADAM_LR = 0.001
ADAM_B1 = 0.9
ADAM_B2 = 0.999
ADAM_EPS = 1e-08
ADAM_WD = 0.01
ADAM_STEP = 10

VMEM_BIG = 56 * 1024 * 1024
PACK_W = 1024

NT = (((1,), (1,)), ((), ()))
TN = (((0,), (0,)), ((), ()))


def _cp(sem=None, vmem=None, **kw):
    return pltpu.CompilerParams(dimension_semantics=sem, vmem_limit_bytes=vmem, **kw)


def _const_spec(shape):
    nd = len(shape)
    return pl.BlockSpec(shape, lambda i, _n=nd: (0,) * _n, pipeline_mode=pl.Buffered(1))


def _row_spec(tm, n):
    return pl.BlockSpec((tm, n), lambda i: (i, 0))


def _rms_parts(xv):
    r = lax.rsqrt(jnp.mean(xv * xv, axis=-1, keepdims=True) + EPS)
    return xv * r, r


def _sigmoid(z):
    return 1.0 / (1.0 + jnp.exp(-z))


def _ff_chunks(f):
    if f % 512 == 0 or f % 256 != 0:
        return [(0, f)] if f <= 1536 else [(0, f // 2), (f // 2, f - f // 2)]
    n = f // 256
    a = (n + 1) // 2 * 256
    return [(0, a), (a, f - a)]


def _norm_mm(x, g, w, tm, name):
    t, d = x.shape
    n = w.shape[1]

    def body(x_ref, g_ref, w_ref, o_ref):
        xhat, _ = _rms_parts(x_ref[...])
        h = (xhat * g_ref[...]).astype(BF16)
        o_ref[...] = jnp.dot(h, w_ref[...], preferred_element_type=F32).astype(BF16)

    return pl.pallas_call(
        body, name=name, grid=(t // tm,),
        in_specs=[_row_spec(tm, d), _const_spec((1, d)), _const_spec((d, n))],
        out_specs=_row_spec(tm, n),
        out_shape=jax.ShapeDtypeStruct((t, n), BF16),
        compiler_params=_cp(("parallel",), VMEM_BIG),
    )(x, g, w)


def _mm_nt_normbwd(dproj, w, x_in, g, dres, tm, name):
    t, d = x_in.shape
    n = w.shape[1]

    def body(dp_ref, w_ref, x_ref, g_ref, dr_ref, dx_ref, dg_ref, h_ref):
        i = pl.program_id(0)
        xhat, r = _rms_parts(x_ref[...])
        gv = g_ref[...]
        h_ref[...] = (xhat * gv).astype(BF16)
        dh = lax.dot_general(dp_ref[...], w_ref[...], NT, preferred_element_type=F32)
        dxhat = dh * gv
        dx = r * (dxhat - xhat * jnp.mean(dxhat * xhat, axis=-1, keepdims=True))
        dx_ref[...] = dr_ref[...] + dx

        @pl.when(i == 0)
        def _():
            dg_ref[...] = jnp.zeros_like(dg_ref)

        dg_ref[...] += jnp.sum(dh * xhat, axis=0, keepdims=True)

    return pl.pallas_call(
        body, name=name, grid=(t // tm,),
        in_specs=[_row_spec(tm, n), _const_spec((d, n)), _row_spec(tm, d), _const_spec((1, d)), _row_spec(tm, d)],
        out_specs=[_row_spec(tm, d), pl.BlockSpec((1, d), lambda i: (0, 0)), _row_spec(tm, d)],
        out_shape=[jax.ShapeDtypeStruct((t, d), F32), jax.ShapeDtypeStruct((1, d), F32),
                   jax.ShapeDtypeStruct((t, d), BF16)],
        compiler_params=_cp(("arbitrary",), VMEM_BIG),
    )(dproj, w, x_in, g, dres)


def _norm_mm_pair(x, g_a, w_a, g_b, w_b, tm, name):
    t, d = x.shape
    na, nb = w_a.shape[1], w_b.shape[1]

    def body(x_ref, ga_ref, wa_ref, gb_ref, wb_ref, oa_ref, ob_ref):
        xhat, _ = _rms_parts(x_ref[...])
        ha = (xhat * ga_ref[...]).astype(BF16)
        hb = (xhat * gb_ref[...]).astype(BF16)
        oa_ref[...] = jnp.dot(ha, wa_ref[...], preferred_element_type=F32).astype(BF16)
        ob_ref[...] = jnp.dot(hb, wb_ref[...], preferred_element_type=F32).astype(BF16)

    return pl.pallas_call(
        body, name=name, grid=(t // tm,),
        in_specs=[_row_spec(tm, d), _const_spec((1, d)), _const_spec((d, na)), _const_spec((1, d)), _const_spec((d, nb))],
        out_specs=[_row_spec(tm, na), _row_spec(tm, nb)],
        out_shape=[jax.ShapeDtypeStruct((t, na), BF16), jax.ShapeDtypeStruct((t, nb), BF16)],
        compiler_params=_cp(("parallel",), VMEM_BIG),
    )(x, g_a, w_a, g_b, w_b)


def _mm_nt_normbwd_pair(dp_a, w_a, g_a, kv_parts, w_b, g_b, x_in, dres, tm, name):
    t, d = x_in.shape
    na, nb = w_a.shape[1], w_b.shape[1]
    nt = t // tm
    main_1, halo_1, main_2, halo_2 = kv_parts

    def body(dpa_ref, wa_ref, ga_ref, m1_ref, h1_ref, m2_ref, h2_ref, wb_ref, gb_ref, x_ref, dr_ref,
             dx_ref, dga_ref, dgb_ref, ha_ref, hb_ref, dkv_ref):
        i = pl.program_id(0)
        s = m1_ref[...] + m2_ref[...]
        tail = jnp.where(i == nt - 1, 0.0, h1_ref[0] + h2_ref[0])
        dkv = jnp.concatenate([s[0:tm - BLOCK], s[tm - BLOCK:] + tail], axis=0).astype(BF16)
        dkv_ref[...] = dkv
        xhat, r = _rms_parts(x_ref[...])
        ga, gb = ga_ref[...], gb_ref[...]
        ha_ref[...] = (xhat * ga).astype(BF16)
        hb_ref[...] = (xhat * gb).astype(BF16)
        dha = lax.dot_general(dpa_ref[...], wa_ref[...], NT, preferred_element_type=F32)
        dhb = lax.dot_general(dkv, wb_ref[...], NT, preferred_element_type=F32)
        dxhat = dha * ga + dhb * gb
        dx_ref[...] = dr_ref[...] + r * (dxhat - xhat * jnp.mean(dxhat * xhat, axis=-1, keepdims=True))

        @pl.when(i == 0)
        def _():
            dga_ref[...] = jnp.zeros_like(dga_ref)
            dgb_ref[...] = jnp.zeros_like(dgb_ref)

        dga_ref[...] += jnp.sum(dha * xhat, axis=0, keepdims=True)
        dgb_ref[...] += jnp.sum(dhb * xhat, axis=0, keepdims=True)

    halo_spec = pl.BlockSpec((1, BLOCK, nb), lambda i: (jnp.minimum(i + 1, nt - 1), 0, 0))
    row1 = pl.BlockSpec((1, d), lambda i: (0, 0))
    return pl.pallas_call(
        body, name=name, grid=(nt,),
        in_specs=[_row_spec(tm, na), _const_spec((d, na)), _const_spec((1, d)), _row_spec(tm, nb), halo_spec,
                  _row_spec(tm, nb), halo_spec, _const_spec((d, nb)), _const_spec((1, d)), _row_spec(tm, d),
                  _row_spec(tm, d)],
        out_specs=[_row_spec(tm, d), row1, row1, _row_spec(tm, d), _row_spec(tm, d), _row_spec(tm, nb)],
        out_shape=[jax.ShapeDtypeStruct((t, d), F32), jax.ShapeDtypeStruct((1, d), F32), jax.ShapeDtypeStruct((1, d), F32),
                   jax.ShapeDtypeStruct((t, d), BF16), jax.ShapeDtypeStruct((t, d), BF16),
                   jax.ShapeDtypeStruct((t, nb), BF16)],
        compiler_params=_cp(("arbitrary",), VMEM_BIG),
    )(dp_a, w_a, g_a, main_1, halo_1, main_2, halo_2, w_b, g_b, x_in, dres)


def _ffn_fwd(x, g, wg, wu, wd, tm, name):
    t, d = x.shape
    f = wg.shape[0]
    chunks = _ff_chunks(f)

    def body(x_ref, g_ref, wg_ref, wu_ref, wd_ref, xo_ref, gate_ref, up_ref):
        xv = x_ref[...]
        xhat, _ = _rms_parts(xv)
        h = (xhat * g_ref[...]).astype(BF16)
        acc = xv
        for c0, cw in chunks:
            gt = lax.dot_general(h, wg_ref[c0:c0 + cw, :], NT, preferred_element_type=F32)
            ut = lax.dot_general(h, wu_ref[c0:c0 + cw, :], NT, preferred_element_type=F32)
            gate_ref[:, c0:c0 + cw] = gt.astype(BF16)
            up_ref[:, c0:c0 + cw] = ut.astype(BF16)
            a = (gt * _sigmoid(gt) * ut).astype(BF16)
            acc = acc + jnp.dot(a, wd_ref[c0:c0 + cw, :], preferred_element_type=F32)
        xo_ref[...] = acc

    return pl.pallas_call(
        body, name=name, grid=(t // tm,),
        in_specs=[_row_spec(tm, d), _const_spec((1, d)), _const_spec((f, d)), _const_spec((f, d)), _const_spec((f, d))],
        out_specs=[_row_spec(tm, d), _row_spec(tm, f), _row_spec(tm, f)],
        out_shape=[jax.ShapeDtypeStruct((t, d), F32), jax.ShapeDtypeStruct((t, f), BF16),
                   jax.ShapeDtypeStruct((t, f), BF16)],
        compiler_params=_cp(("parallel",), VMEM_BIG),
    )(x, g, wg, wu, wd)


def _ffn_bwd(dxo, xm, g, gate, up, wg, wu, wd, tm, name):
    t, d = xm.shape
    f = wg.shape[0]
    chunks = _ff_chunks(f)

    def body(dxo_ref, xm_ref, g_ref, gate_ref, up_ref, wg_ref, wu_ref, wd_ref,
             dxm_ref, dgate_ref, dup_ref, act_ref, h2_ref, dg_ref):
        i = pl.program_id(0)
        dxo_v = dxo_ref[...]
        dxo_b = dxo_v.astype(BF16)
        xhat, r = _rms_parts(xm_ref[...])
        gv = g_ref[...]
        h2_ref[...] = (xhat * gv).astype(BF16)
        dh = jnp.zeros((tm, d), F32)
        for c0, cw in chunks:
            dact = lax.dot_general(dxo_b, wd_ref[c0:c0 + cw, :], NT, preferred_element_type=F32)
            gt = gate_ref[:, c0:c0 + cw].astype(F32)
            ut = up_ref[:, c0:c0 + cw].astype(F32)
            sg = _sigmoid(gt)
            sl = gt * sg
            act_ref[:, c0:c0 + cw] = (sl * ut).astype(BF16)
            dgt = (dact * ut * (sg * (1.0 + gt * (1.0 - sg)))).astype(BF16)
            dut = (dact * sl).astype(BF16)
            dgate_ref[:, c0:c0 + cw] = dgt
            dup_ref[:, c0:c0 + cw] = dut
            dh = dh + jnp.dot(dgt, wg_ref[c0:c0 + cw, :], preferred_element_type=F32)
            dh = dh + jnp.dot(dut, wu_ref[c0:c0 + cw, :], preferred_element_type=F32)
        dxhat = dh * gv
        dx = r * (dxhat - xhat * jnp.mean(dxhat * xhat, axis=-1, keepdims=True))
        dxm_ref[...] = dxo_v + dx

        @pl.when(i == 0)
        def _():
            dg_ref[...] = jnp.zeros_like(dg_ref)

        dg_ref[...] += jnp.sum(dh * xhat, axis=0, keepdims=True)

    return pl.pallas_call(
        body, name=name, grid=(t // tm,),
        in_specs=[_row_spec(tm, d), _row_spec(tm, d), _const_spec((1, d)), _row_spec(tm, f), _row_spec(tm, f),
                  _const_spec((f, d)), _const_spec((f, d)), _const_spec((f, d))],
        out_specs=[_row_spec(tm, d), _row_spec(tm, f), _row_spec(tm, f), _row_spec(tm, f), _row_spec(tm, d),
                   pl.BlockSpec((1, d), lambda i: (0, 0))],
        out_shape=[jax.ShapeDtypeStruct((t, d), F32), jax.ShapeDtypeStruct((t, f), BF16),
                   jax.ShapeDtypeStruct((t, f), BF16), jax.ShapeDtypeStruct((t, f), BF16),
                   jax.ShapeDtypeStruct((t, d), BF16), jax.ShapeDtypeStruct((1, d), F32)],
        compiler_params=_cp(("arbitrary",), VMEM_BIG),
    )(dxo, xm, g, gate, up, wg, wu, wd)


def _wgrad(a, b, tt, name):
    t, k = a.shape
    n = b.shape[1]
    nt = t // tt

    def body(a_ref, b_ref, o_ref, acc):
        i = pl.program_id(0)

        @pl.when(i == 0)
        def _():
            acc[...] = jnp.zeros_like(acc)

        acc[...] += lax.dot_general(a_ref[...].astype(BF16), b_ref[...].astype(BF16), TN,
                                    preferred_element_type=F32)

        @pl.when(i == nt - 1)
        def _():
            o_ref[...] = acc[...].astype(BF16)

    return pl.pallas_call(
        body, name=name, grid=(nt,),
        in_specs=[_row_spec(tt, k), _row_spec(tt, n)],
        out_specs=pl.BlockSpec((k, n), lambda i: (0, 0)),
        out_shape=jax.ShapeDtypeStruct((k, n), BF16),
        scratch_shapes=[pltpu.VMEM((k, n), F32)],
        compiler_params=_cp(("arbitrary",), VMEM_BIG),
    )(a, b)


def _final_loss(x, g, target, tm, name):
    t, d = x.shape

    def body(x_ref, g_ref, t_ref, loss_ref, dx_ref, dg_ref):
        i = pl.program_id(0)
        xhat, r = _rms_parts(x_ref[...])
        gv = g_ref[...]
        err = xhat * gv - t_ref[...]
        dy = err * (1.0 / d)
        dxhat = dy * gv
        dx_ref[...] = r * (dxhat - xhat * jnp.mean(dxhat * xhat, axis=-1, keepdims=True))

        @pl.when(i == 0)
        def _():
            dg_ref[...] = jnp.zeros_like(dg_ref)
            loss_ref[...] = jnp.zeros_like(loss_ref)

        dg_ref[...] += jnp.sum(dy * xhat, axis=0, keepdims=True)
        part = jnp.sum(jnp.sum(err * err, axis=-1, keepdims=True), axis=0, keepdims=True) * (0.5 / d)
        loss_ref[...] += jnp.broadcast_to(part, loss_ref.shape)

    return pl.pallas_call(
        body, name=name, grid=(t // tm,),
        in_specs=[_row_spec(tm, d), _const_spec((1, d)), _row_spec(tm, d)],
        out_specs=[pl.BlockSpec((8, 128), lambda i: (0, 0)), _row_spec(tm, d), pl.BlockSpec((1, d), lambda i: (0, 0))],
        out_shape=[jax.ShapeDtypeStruct((8, 128), F32), jax.ShapeDtypeStruct((t, d), F32),
                   jax.ShapeDtypeStruct((1, d), F32)],
        compiler_params=_cp(("arbitrary",)),
    )(x, g, target)


def _col_head(width):
    return lax.broadcasted_iota(jnp.int32, (1, width), 1) // HEAD_DIM


def _keep_head(a, colh, h):
    return jnp.where(colh == h, a, jnp.zeros_like(a))


def _softmax_cols(s, sink=None):
    m = jnp.max(s, axis=0, keepdims=True)
    if sink is not None:
        m = jnp.maximum(m, sink)
    p = jnp.exp(s - m)
    l = jnp.sum(p, axis=0, keepdims=True)
    if sink is None:
        return p * (1.0 / l), None
    es = jnp.exp(sink - m)
    inv = 1.0 / (l + es)
    return p * inv, es * inv


def _add4(v):
    return (v[0] + v[1]) + (v[2] + v[3])


def _mem_attn_fwd(qm, mk, mv):
    colh = _col_head(mk.shape[1])
    mks = mk * SCALE
    heads = range(N_MEM_HEADS)
    ss = [lax.dot_general(_keep_head(mks, colh, h), qm, NT, preferred_element_type=F32) for h in heads]
    ps = [_softmax_cols(s)[0].astype(BF16) for s in ss]
    return _add4([lax.dot_general(ps[h], _keep_head(mv, colh, h), TN, preferred_element_type=F32) for h in heads])


def _mem_attn_bwd(qm, dy_b, mk, mv):
    colh = _col_head(mk.shape[1])
    mks = mk * SCALE
    heads = range(N_MEM_HEADS)
    khs = [_keep_head(mks, colh, h) for h in heads]
    vhs = [_keep_head(mv, colh, h) for h in heads]
    ss = [lax.dot_general(khs[h], qm, NT, preferred_element_type=F32) for h in heads]
    dps = [lax.dot_general(vhs[h], dy_b, NT, preferred_element_type=F32) for h in heads]
    pbs, dsbs = [], []
    for h in heads:
        p, _ = _softmax_cols(ss[h])
        ds = p * (dps[h] - jnp.sum(p * dps[h], axis=0, keepdims=True))
        pbs.append(p.astype(BF16))
        dsbs.append(ds.astype(BF16))
    y = _add4([lax.dot_general(pbs[h], vhs[h], TN, preferred_element_type=F32) for h in heads])
    dq = _add4([lax.dot_general(dsbs[h], khs[h], TN, preferred_element_type=F32) for h in heads])
    dmk = _add4([jnp.where(colh == h, jnp.dot(dsbs[h], qm, preferred_element_type=F32) * SCALE, 0.0) for h in heads])
    dmv = _add4([jnp.where(colh == h, jnp.dot(pbs[h], dy_b, preferred_element_type=F32), 0.0) for h in heads])
    return y, dq, dmk, dmv


def _shift_down(v, halo, k):
    rolled = pltpu.roll(v, k, 0)
    hrolled = pltpu.roll(halo, k, 0)[0:8]
    rows = lax.broadcasted_iota(jnp.int32, (8, v.shape[1]), 0)
    first = jnp.where(rows < k, hrolled, rolled[0:8])
    return jnp.concatenate([first, rolled[8:]], axis=0)


def _shift_up(v, halo, k):
    n = v.shape[0]
    rolled = pltpu.roll(v, n - k, 0)
    hrolled = pltpu.roll(halo, 8 - k, 0)[0:8]
    rows = lax.broadcasted_iota(jnp.int32, (8, v.shape[1]), 0)
    last = jnp.where(rows >= 8 - k, hrolled, rolled[n - 8:])
    return jnp.concatenate([rolled[:n - 8], last], axis=0)


def _conv_parts(p, ph, cw, first_tile, cwid):
    u = p[:, 0:cwid].astype(F32)
    bg = p[:, cwid:2 * cwid].astype(F32)
    cg = p[:, 2 * cwid:3 * cwid].astype(F32)
    v = cg * u
    vh = ph[:, 2 * cwid:3 * cwid].astype(F32) * ph[:, 0:cwid].astype(F32)
    vh = jnp.where(first_tile, 0.0, vh)
    v1 = _shift_down(v, vh, 1)
    v2 = _shift_down(v, vh, 2)
    conv = cw[0:1, :] * v2 + cw[1:2, :] * v1 + cw[2:3, :] * v
    return u, bg, cg, v, v1, v2, conv


def _halo_prev_spec(rows, n, tm):
    per = tm // rows
    return pl.BlockSpec((rows, n), lambda i: (jnp.maximum(i * per - 1, 0), 0))


def _halo_next_spec(rows, n, tm, t):
    per = tm // rows
    last = t // rows - 1
    return pl.BlockSpec((rows, n), lambda i: (jnp.minimum((i + 1) * per, last), 0))


def _mix_a_fwd(x, proj, convw, memkv, layer, wout, tm, name):
    t, d = x.shape
    n_mem = memkv.shape[0]
    mw = N_MEM_HEADS * HEAD_DIM
    cwid = d - mw
    pw = proj.shape[1]

    def body(x_ref, p_ref, ph_ref, cw_ref, mkv_ref, wo_ref, xo_ref):
        i = pl.program_id(0)
        p = p_ref[...]
        _, bg, _, _, _, _, conv = _conv_parts(p, ph_ref[...], cw_ref[...], i == 0, cwid)
        ytok = (bg * conv).astype(BF16)
        mkv = mkv_ref[...]
        ymem = _mem_attn_fwd(p[:, 3 * cwid:3 * cwid + mw], mkv[:, 0:mw], mkv[:, mw:2 * mw])
        cat = jnp.concatenate([ytok, ymem.astype(BF16)], axis=1)
        xo_ref[...] = x_ref[...] + jnp.dot(cat, wo_ref[...], preferred_element_type=F32)

    return pl.pallas_call(
        body, name=name, grid=(t // tm,),
        in_specs=[_row_spec(tm, d), _row_spec(tm, pw), _halo_prev_spec(16, pw, tm), _const_spec((3, cwid)),
                  pl.BlockSpec((n_mem, 2 * mw), lambda i: (0, layer)), _const_spec((d, d))],
        out_specs=_row_spec(tm, d),
        out_shape=jax.ShapeDtypeStruct((t, d), F32),
        compiler_params=_cp(("parallel",), VMEM_BIG),
    )(x, proj, proj, convw, memkv, wout)


def _mix_a_bwd(dxm, proj, convw, memkv, layer, wout, tm, name):
    t, d = dxm.shape
    n_mem = memkv.shape[0]
    mw = N_MEM_HEADS * HEAD_DIM
    cwid = d - mw
    pw = proj.shape[1]
    nt = t // tm

    def body(dx_ref, dxn_ref, p_ref, ph_ref, pn_ref, cw_ref, mkv_ref, wo_ref,
             dp_ref, cat_ref, dcw_ref, dmkv_ref, dmk_acc, dmv_acc):
        i = pl.program_id(0)
        p = p_ref[...]
        cw = cw_ref[...]
        wo = wo_ref[...]
        u, bg, cg, v, v1, v2, conv = _conv_parts(p, ph_ref[...], cw, i == 0, cwid)
        dcat = lax.dot_general(dx_ref[...].astype(BF16), wo, NT, preferred_element_type=F32)
        dytok = dcat[:, 0:cwid]
        dymem_b = dcat[:, cwid:d].astype(BF16)
        pn = pn_ref[...]
        dcat_n = lax.dot_general(dxn_ref[...].astype(BF16), wo[0:cwid, :], NT, preferred_element_type=F32)
        dconv_n = jnp.where(i == nt - 1, 0.0, dcat_n * pn[:, cwid:2 * cwid].astype(F32))
        dbg = dytok * conv
        dconv = dytok * bg
        dv = cw[2:3, :] * dconv + cw[1:2, :] * _shift_up(dconv, dconv_n, 1) + cw[0:1, :] * _shift_up(dconv, dconv_n, 2)
        du = dv * cg
        dcg = dv * u
        rows8 = lax.broadcasted_iota(jnp.int32, (8, cwid), 0)
        dcw = (jnp.where(rows8 == 0, jnp.sum(dconv * v2, axis=0, keepdims=True), 0.0)
               + jnp.where(rows8 == 1, jnp.sum(dconv * v1, axis=0, keepdims=True), 0.0)
               + jnp.where(rows8 == 2, jnp.sum(dconv * v, axis=0, keepdims=True), 0.0))
        mkv = mkv_ref[...]
        qm = p[:, 3 * cwid:3 * cwid + mw]
        ymem, dqm, dmk, dmv = _mem_attn_bwd(qm, dymem_b, mkv[:, 0:mw], mkv[:, mw:2 * mw])
        cat_ref[...] = jnp.concatenate([(bg * conv).astype(BF16), ymem.astype(BF16)], axis=1)
        dp_ref[...] = jnp.concatenate([du.astype(BF16), dbg.astype(BF16), dcg.astype(BF16), dqm.astype(BF16)], axis=1)

        @pl.when(i == 0)
        def _():
            dcw_ref[...] = jnp.zeros_like(dcw_ref)
            dmk_acc[...] = jnp.zeros_like(dmk_acc)
            dmv_acc[...] = jnp.zeros_like(dmv_acc)

        dcw_ref[...] += dcw
        dmk_acc[...] += dmk
        dmv_acc[...] += dmv

        @pl.when(i == nt - 1)
        def _():
            dmkv_ref[...] = jnp.concatenate([dmk_acc[...], dmv_acc[...]], axis=1)

    return pl.pallas_call(
        body, name=name, grid=(nt,),
        in_specs=[_row_spec(tm, d), _halo_next_spec(16, d, tm, t), _row_spec(tm, pw), _halo_prev_spec(16, pw, tm),
                  _halo_next_spec(16, pw, tm, t), _const_spec((3, cwid)),
                  pl.BlockSpec((n_mem, 2 * mw), lambda i: (0, layer)), _const_spec((d, d))],
        out_specs=[_row_spec(tm, pw), _row_spec(tm, d), pl.BlockSpec((8, cwid), lambda i: (0, 0)),
                   pl.BlockSpec((n_mem, 2 * mw), lambda i: (0, 0))],
        out_shape=[jax.ShapeDtypeStruct((t, pw), BF16), jax.ShapeDtypeStruct((t, d), BF16),
                   jax.ShapeDtypeStruct((8, cwid), F32), jax.ShapeDtypeStruct((n_mem, 2 * mw), F32)],
        scratch_shapes=[pltpu.VMEM((n_mem, mw), F32), pltpu.VMEM((n_mem, mw), F32)],
        compiler_params=_cp(("arbitrary",), VMEM_BIG),
    )(dxm, dxm, proj, proj, proj, convw, memkv, wout)


def _rel_tables():
    qi = np.arange(BLOCK, dtype=np.int32)[:, None]
    kj = np.arange(2 * BLOCK, dtype=np.int32)[None, :]
    dist = qi + BLOCK - kj
    inw = (dist >= 0) & (dist < BLOCK)
    max_exact = REL_BUCKETS // 2
    dd = np.maximum(np.maximum(dist, 0), 1).astype(np.float32)
    large = max_exact + (np.log(dd / np.float32(max_exact)) / np.float32(math.log(REL_MAX_DIST / max_exact))
                         * np.float32(REL_BUCKETS - max_exact)).astype(np.int32)
    large = np.minimum(large, REL_BUCKETS - 1)
    bucket = np.where(np.maximum(dist, 0) < max_exact, np.maximum(dist, 0), large)
    return np.where(inw, bucket, -1).astype(np.int32)


def _bias_tables(rel_bias, sinks, name):
    bucket_t = jnp.asarray(_rel_tables().T)

    def body(rb_ref, sk_ref, bk_ref, bias_ref, sink_ref):
        bk = bk_ref[...]
        prev = lax.broadcasted_iota(jnp.int32, bk.shape, 0) < BLOCK
        for h in range(N_KV_HEADS):
            for j in range(GROUP):
                head = GROUP * h + j
                acc = jnp.full(bk.shape, NEG, F32)
                for b in range(REL_BUCKETS):
                    acc = jnp.where(bk == b, rb_ref[b, head], acc)
                bias_ref[h, :, j * BLOCK:(j + 1) * BLOCK] = acc
                bias_ref[N_KV_HEADS + h, :, j * BLOCK:(j + 1) * BLOCK] = jnp.where(prev, NEG, acc)
                sink_ref[h, :, j * BLOCK:(j + 1) * BLOCK] = jnp.full((8, BLOCK), sk_ref[0, head], F32)

    smem = pl.BlockSpec(memory_space=pltpu.SMEM)
    return pl.pallas_call(
        body, name=name,
        in_specs=[smem, smem, pl.BlockSpec(memory_space=pltpu.VMEM)],
        out_specs=[pl.BlockSpec(memory_space=pltpu.VMEM), pl.BlockSpec(memory_space=pltpu.VMEM)],
        out_shape=[jax.ShapeDtypeStruct((2 * N_KV_HEADS, 2 * BLOCK, GROUP * BLOCK), F32),
                   jax.ShapeDtypeStruct((N_KV_HEADS, 8, GROUP * BLOCK), F32)],
    )(rel_bias, sinks.reshape(1, N_Q_HEADS), bucket_t)


def _bias_bwd(dbias_a, dbias_b, name):
    bucket_t = jnp.asarray(_rel_tables().T)

    def body(da_ref, db_ref, bk_ref, o_ref):
        bk = bk_ref[...]
        ri = lax.broadcasted_iota(jnp.int32, (REL_BUCKETS, 128), 0)
        ci = lax.broadcasted_iota(jnp.int32, (REL_BUCKETS, 128), 1)
        out = jnp.zeros((REL_BUCKETS, 128), F32)
        for h in range(N_KV_HEADS):
            dsum = da_ref[h] + db_ref[h]
            for j in range(GROUP):
                head = GROUP * h + j
                seg = dsum[:, j * BLOCK:(j + 1) * BLOCK]
                for b in range(REL_BUCKETS):
                    val = jnp.sum(jnp.sum(jnp.where(bk == b, seg, 0.0), axis=0, keepdims=True), axis=1, keepdims=True)
                    out = out + jnp.where((ri == b) & (ci == head), val, 0.0)
        o_ref[...] = out

    vm = pl.BlockSpec(memory_space=pltpu.VMEM)
    return pl.pallas_call(
        body, name=name, in_specs=[vm, vm, vm], out_specs=vm,
        out_shape=jax.ShapeDtypeStruct((REL_BUCKETS, 128), F32),
    )(dbias_a, dbias_b, bucket_t)


def _stack_members(ref, r0, width):
    blk = ref[pl.ds(r0, BLOCK), 0:GROUP * width]
    return jnp.concatenate([blk[:, j * width:(j + 1) * width] for j in range(GROUP)], axis=0)


def _mix_b_fwd(x, qp, kv, bias, sinkt, memkv, layer, wout, tm, name):
    t, d = x.shape
    n_mem = memkv.shape[0]
    mw = N_MEM_HEADS * HEAD_DIM
    qw = d - mw
    kw = N_KV_HEADS * HEAD_DIM
    nb = tm // BLOCK
    rows = GROUP * BLOCK

    def body(x_ref, q_ref, kv_ref, kvh_ref, bias_ref, sink_ref, mkv_ref, wo_ref, xo_ref, kvx, ytok):
        i = pl.program_id(0)
        kvx[0:BLOCK, :] = kvh_ref[...]
        kvx[BLOCK:BLOCK + tm, :] = kv_ref[...]
        colh = _col_head(kw)

        def blk(b, carry):
            r0 = pl.multiple_of(b * BLOCK, BLOCK)
            win = kvx[pl.ds(r0, 2 * BLOCK), :]
            kwin = win[:, 0:kw] * SCALE
            vwin = win[:, kw:2 * kw]
            qs = _stack_members(q_ref, r0, kw)
            first = ((i == 0) & (b == 0)).astype(jnp.int32) * N_KV_HEADS
            heads = range(N_KV_HEADS)
            ss = [lax.dot_general(_keep_head(kwin, colh, h), qs, NT, preferred_element_type=F32) for h in heads]
            ps = [_softmax_cols(ss[h] + bias_ref[first + h], sink_ref[h][0:1, :])[0].astype(BF16) for h in heads]
            o = _add4([lax.dot_general(ps[h], _keep_head(vwin, colh, h), TN, preferred_element_type=F32)
                       for h in heads])
            for j in range(GROUP):
                ytok[pl.ds(r0, BLOCK), j * kw:(j + 1) * kw] = o[j * BLOCK:(j + 1) * BLOCK].astype(BF16)
            return carry

        for b_static in range(nb):
            blk(b_static, 0)
        mkv = mkv_ref[...]
        ymem = _mem_attn_fwd(q_ref[:, qw:d], mkv[:, 0:mw], mkv[:, mw:2 * mw])
        cat = jnp.concatenate([ytok[...], ymem.astype(BF16)], axis=1)
        xo_ref[...] = x_ref[...] + jnp.dot(cat, wo_ref[...], preferred_element_type=F32)

    return pl.pallas_call(
        body, name=name, grid=(t // tm,),
        in_specs=[_row_spec(tm, d), _row_spec(tm, d), _row_spec(tm, 2 * kw), _halo_prev_spec(BLOCK, 2 * kw, tm),
                  _const_spec((2 * N_KV_HEADS, 2 * BLOCK, rows)), _const_spec((N_KV_HEADS, 8, rows)),
                  pl.BlockSpec((n_mem, 2 * mw), lambda i: (0, layer)), _const_spec((d, d))],
        out_specs=_row_spec(tm, d),
        out_shape=jax.ShapeDtypeStruct((t, d), F32),
        scratch_shapes=[pltpu.VMEM((tm + BLOCK, 2 * kw), BF16), pltpu.VMEM((tm, qw), BF16)],
        compiler_params=_cp(("parallel",), VMEM_BIG),
    )(x, qp, kv, kv, bias, sinkt, memkv, wout)


def _mix_b_bwd(dxm, qp, kv, bias, sinkt, memkv, layer, wout, tm, name):
    t, d = dxm.shape
    n_mem = memkv.shape[0]
    mw = N_MEM_HEADS * HEAD_DIM
    qw = d - mw
    kw = N_KV_HEADS * HEAD_DIM
    nb = tm // BLOCK
    nt = t // tm
    rows = GROUP * BLOCK

    def body(dx_ref, q_ref, kv_ref, kvh_ref, bias_ref, sink_ref, mkv_ref, wo_ref,
             dq_ref, cat_ref, dkv_ref, dkvh_ref, dbias_ref, dsink_ref, dmkv_ref,
             kvx, dkvx, dcat_s, dmk_acc, dmv_acc):
        i = pl.program_id(0)

        @pl.when(i == 0)
        def _():
            dbias_ref[...] = jnp.zeros_like(dbias_ref)
            dsink_ref[...] = jnp.zeros_like(dsink_ref)
            dmk_acc[...] = jnp.zeros_like(dmk_acc)
            dmv_acc[...] = jnp.zeros_like(dmv_acc)

        kvx[0:BLOCK, :] = kvh_ref[...]
        kvx[BLOCK:BLOCK + tm, :] = kv_ref[...]
        dkvx[...] = jnp.zeros_like(dkvx)
        dcat_s[...] = lax.dot_general(dx_ref[...].astype(BF16), wo_ref[...], NT,
                                      preferred_element_type=F32).astype(BF16)
        colh = _col_head(kw)
        lane8 = lax.broadcasted_iota(jnp.int32, (8, 128), 1)

        def blk(b, carry):
            r0 = pl.multiple_of(b * BLOCK, BLOCK)
            win = kvx[pl.ds(r0, 2 * BLOCK), :]
            kwin = win[:, 0:kw] * SCALE
            vwin = win[:, kw:2 * kw]
            qs = _stack_members(q_ref, r0, kw)
            dos = _stack_members(dcat_s, r0, kw)
            first = ((i == 0) & (b == 0)).astype(jnp.int32) * N_KV_HEADS
            heads = range(N_KV_HEADS)
            khs = [_keep_head(kwin, colh, h) for h in heads]
            vhs = [_keep_head(vwin, colh, h) for h in heads]
            ss = [lax.dot_general(khs[h], qs, NT, preferred_element_type=F32) for h in heads]
            dps = [lax.dot_general(vhs[h], dos, NT, preferred_element_type=F32) for h in heads]
            dsink = jnp.zeros((8, 128), F32)
            pbs, dsbs = [], []
            for h in heads:
                p, sinkp = _softmax_cols(ss[h] + bias_ref[first + h], sink_ref[h][0:1, :])
                delta = jnp.sum(p * dps[h], axis=0, keepdims=True)
                ds = p * (dps[h] - delta)
                dbias_ref[h] += ds
                sd = sinkp * delta
                for j in range(GROUP):
                    val = -jnp.sum(sd[:, j * BLOCK:(j + 1) * BLOCK], axis=1, keepdims=True)
                    dsink = dsink + jnp.where(lane8 == 4 * j + h, val, 0.0)
                pbs.append(p.astype(BF16))
                dsbs.append(ds.astype(BF16))
            y = _add4([lax.dot_general(pbs[h], vhs[h], TN, preferred_element_type=F32) for h in heads])
            dq = _add4([lax.dot_general(dsbs[h], khs[h], TN, preferred_element_type=F32) for h in heads])
            dk = _add4([jnp.where(colh == h, jnp.dot(dsbs[h], qs, preferred_element_type=F32) * SCALE, 0.0)
                        for h in heads])
            dv = _add4([jnp.where(colh == h, jnp.dot(pbs[h], dos, preferred_element_type=F32), 0.0) for h in heads])
            for j in range(GROUP):
                cat_ref[pl.ds(r0, BLOCK), j * kw:(j + 1) * kw] = y[j * BLOCK:(j + 1) * BLOCK].astype(BF16)
                dq_ref[pl.ds(r0, BLOCK), j * kw:(j + 1) * kw] = dq[j * BLOCK:(j + 1) * BLOCK].astype(BF16)
            dsink_ref[...] += dsink
            dkvx[pl.ds(r0, 2 * BLOCK), :] += jnp.concatenate([dk, dv], axis=1)
            return carry

        for b_static in range(nb):
            blk(b_static, 0)
        dkvh_ref[0] = dkvx[0:BLOCK, :]
        dkv_ref[...] = dkvx[BLOCK:BLOCK + tm, :]

        mkv = mkv_ref[...]
        ymem, dqm, dmk, dmv = _mem_attn_bwd(q_ref[:, qw:d], dcat_s[:, qw:d], mkv[:, 0:mw], mkv[:, mw:2 * mw])
        cat_ref[:, qw:d] = ymem.astype(BF16)
        dq_ref[:, qw:d] = dqm.astype(BF16)
        dmk_acc[...] += dmk
        dmv_acc[...] += dmv

        @pl.when(i == nt - 1)
        def _():
            dmkv_ref[...] = jnp.concatenate([dmk_acc[...], dmv_acc[...]], axis=1)

    return pl.pallas_call(
        body, name=name, grid=(nt,),
        in_specs=[_row_spec(tm, d), _row_spec(tm, d), _row_spec(tm, 2 * kw), _halo_prev_spec(BLOCK, 2 * kw, tm),
                  _const_spec((2 * N_KV_HEADS, 2 * BLOCK, rows)), _const_spec((N_KV_HEADS, 8, rows)),
                  pl.BlockSpec((n_mem, 2 * mw), lambda i: (0, layer)), _const_spec((d, d))],
        out_specs=[_row_spec(tm, d), _row_spec(tm, d), _row_spec(tm, 2 * kw),
                   pl.BlockSpec((1, BLOCK, 2 * kw), lambda i: (i, 0, 0)),
                   pl.BlockSpec((N_KV_HEADS, 2 * BLOCK, rows), lambda i: (0, 0, 0)),
                   pl.BlockSpec((8, 128), lambda i: (0, 0)),
                   pl.BlockSpec((n_mem, 2 * mw), lambda i: (0, 0))],
        out_shape=[jax.ShapeDtypeStruct((t, d), BF16), jax.ShapeDtypeStruct((t, d), BF16),
                   jax.ShapeDtypeStruct((t, 2 * kw), F32), jax.ShapeDtypeStruct((nt, BLOCK, 2 * kw), F32),
                   jax.ShapeDtypeStruct((N_KV_HEADS, 2 * BLOCK, rows), F32), jax.ShapeDtypeStruct((8, 128), F32),
                   jax.ShapeDtypeStruct((n_mem, 2 * mw), F32)],
        scratch_shapes=[pltpu.VMEM((tm + BLOCK, 2 * kw), BF16), pltpu.VMEM((tm + BLOCK, 2 * kw), F32),
                        pltpu.VMEM((tm, d), BF16), pltpu.VMEM((n_mem, mw), F32), pltpu.VMEM((n_mem, mw), F32)],
        compiler_params=_cp(("arbitrary",), VMEM_BIG),
    )(dxm, qp, kv, kv, bias, sinkt, memkv, wout)


def _adam_math(w, g, m, v):
    m2 = ADAM_B1 * m + (1.0 - ADAM_B1) * g
    v2 = ADAM_B2 * v + (1.0 - ADAM_B2) * (g * g)
    m_hat = m2 / (1.0 - ADAM_B1 ** ADAM_STEP)
    v_hat = v2 / (1.0 - ADAM_B2 ** ADAM_STEP)
    delta = -ADAM_LR * (m_hat / (jnp.sqrt(v_hat) + ADAM_EPS) + ADAM_WD * w)
    return delta, m2, v2


def _adamw_sharded(w, land, m, v, name):
    nl, r, c = w.shape
    tr = max(cand for cand in range(16, r + 1, 16) if r % cand == 0 and cand * c <= 512 * 1024)

    def body(w_ref, a_ref, m_ref, v_ref, g_ref, d_ref, mo_ref, vo_ref):
        g = a_ref[0, 0].astype(F32) + a_ref[1, 0].astype(F32)
        for k in range(1, N_CHIPS):
            g = g + (a_ref[2 * k, 0].astype(F32) + a_ref[2 * k + 1, 0].astype(F32))
        delta, m2, v2 = _adam_math(w_ref[0], g, m_ref[0], v_ref[0])
        g_ref[0] = g
        d_ref[0] = delta
        mo_ref[0] = m2
        vo_ref[0] = v2

    rs = pl.BlockSpec((1, tr, c), lambda l, i: (l, i, 0))
    ps = pl.BlockSpec((2 * N_CHIPS, 1, tr, c), lambda l, i: (0, l, i, 0))
    sd = jax.ShapeDtypeStruct((nl, r, c), F32)
    return pl.pallas_call(
        body, name=name, grid=(nl, r // tr),
        in_specs=[rs, ps, rs, rs], out_specs=[rs, rs, rs, rs], out_shape=[sd, sd, sd, sd],
        compiler_params=_cp(("parallel", "parallel"), VMEM_BIG),
    )(w, land, m, v)


def _adamw_packed(w, g, m, v, name):
    def body(w_ref, g_ref, m_ref, v_ref, d_ref, mo_ref, vo_ref):
        delta, m2, v2 = _adam_math(w_ref[...], g_ref[...], m_ref[...], v_ref[...])
        d_ref[...] = delta
        mo_ref[...] = m2
        vo_ref[...] = v2

    vm = pl.BlockSpec(memory_space=pltpu.VMEM)
    sd = jax.ShapeDtypeStruct(w.shape, F32)
    return pl.pallas_call(body, name=name, in_specs=[vm] * 4, out_specs=[vm] * 3, out_shape=[sd] * 3)(w, g, m, v)


def _place():
    return lax.axis_index("x"), lax.axis_index("y"), lax.axis_index("c")


def _hbm(a):
    return pltpu.with_memory_space_constraint(a, pltpu.HBM)


def _peers(x, y, c, both_cores):
    chips = [(1 - x, y), (x, 1 - y), (1 - x, 1 - y)]
    if not both_cores:
        return [(px, py, c) for px, py in chips]
    return [(px, py, pc) for px, py in chips for pc in (c, 1 - c)] + [(x, y, 1 - c)]


def _chip_copy(src, land, gather, layer, chip_src, slot, send_sem, recv_sem, peer):
    s = src if gather else src.at[chip_src]
    d = land.at[slot] if layer is None else land.at[slot, layer]
    return pltpu.make_async_remote_copy(src_ref=s, dst_ref=d, send_sem=send_sem, recv_sem=recv_sem,
                                        device_id=peer, device_id_type=MESH)


def _exchange_start(srcs, lands, gather, layers, both_cores, after, name):
    n = len(srcs)
    npeer = 7 if both_cores else 3
    hbm = pl.BlockSpec(memory_space=pltpu.HBM)
    sem = pl.BlockSpec(memory_space=pltpu.SEMAPHORE)

    def body(*refs):
        ins, lds = refs[:n], refs[n:2 * n]
        first_out = 2 * n + len(after)
        send_sems, recv_sems, token = refs[first_out], refs[first_out + 1], refs[-1]
        x, y, c = _place()
        slot = 2 * x + y if gather else 2 * (2 * x + y) + c
        for t in range(n):
            for r, peer in enumerate(_peers(x, y, c, both_cores)):
                _chip_copy(ins[t], lds[t], gather, layers[t], 2 * peer[0] + peer[1], slot,
                           send_sems.at[npeer * t + r], recv_sems.at[npeer * t + r], peer).start()
        token[...] = jnp.zeros_like(token)

    both = list(srcs) + list(lands)
    outs = pl.pallas_call(
        body, name=name, in_specs=[hbm] * (2 * n) + [pl.BlockSpec(memory_space=pl.ANY)] * len(after),
        out_specs=(sem, sem, *([hbm] * (2 * n)), pl.BlockSpec(memory_space=pltpu.VMEM)),
        out_shape=(pltpu.SemaphoreType.DMA((npeer * n,)), pltpu.SemaphoreType.DMA((npeer * n,)),
                   *[pltpu.HBM(a.shape, a.dtype) for a in both], jax.ShapeDtypeStruct((8, 128), F32)),
        input_output_aliases={t: 2 + t for t in range(2 * n)},
        compiler_params=_cp(has_side_effects=pltpu.SideEffectType.DATAFLOW_SIDE_EFFECTING),
    )(*[_hbm(a) for a in both], *after)
    return dict(send=outs[0], recv=outs[1], srcs=list(outs[2:2 + n]), lands=list(outs[2 + n:2 + 2 * n]),
                token=outs[-1], gather=gather, layers=list(layers), both_cores=both_cores)


def _exchange_wait(groups, lands, land_ids, after, name):
    flat = [s for g in groups for s in g["srcs"]]
    ns, nl, ng, na = len(flat), len(lands), len(groups), len(after)
    hbm = pl.BlockSpec(memory_space=pltpu.HBM)
    sem = pl.BlockSpec(memory_space=pltpu.SEMAPHORE)

    def body(*refs):
        srcs, lds = refs[:ns], refs[ns:ns + nl]
        sems = refs[ns + nl:ns + nl + 2 * ng]
        x, y, c = _place()
        k = 0
        for gi, g in enumerate(groups):
            peers = _peers(x, y, c, g["both_cores"])
            for t in range(len(g["srcs"])):
                for r, peer in enumerate(peers):
                    cp = _chip_copy(srcs[k], lds[land_ids[gi][t]], g["gather"], g["layers"][t], 0, 0,
                                    sems[2 * gi].at[len(peers) * t + r], sems[2 * gi + 1].at[len(peers) * t + r], peer)
                    cp.wait_send()
                    cp.wait_recv()
                k += 1

    both = flat + list(lands)
    sem_args = [a for g in groups for a in (g["send"], g["recv"])]
    outs = pl.pallas_call(
        body, name=name,
        in_specs=[hbm] * (ns + nl) + [sem] * (2 * ng) + [pl.BlockSpec(memory_space=pl.ANY)] * na,
        out_specs=[hbm] * (ns + nl),
        out_shape=[pltpu.HBM(a.shape, a.dtype) for a in both],
        input_output_aliases={t: t for t in range(ns + nl)},
        compiler_params=_cp(has_side_effects=pltpu.SideEffectType.DATAFLOW_SIDE_EFFECTING),
    )(*both, *sem_args, *after)
    return list(outs[ns:])


def _core_fill(lands, layers, name):
    n = len(lands)
    hbm = pl.BlockSpec(memory_space=pltpu.HBM)

    def body(*refs):
        ins = refs[:n]
        send_sems, recv_sems = refs[2 * n:]
        x, y, c = _place()
        copies = []
        for t in range(n):
            for k in range(N_CHIPS):
                mine = ins[t].at[2 * k + c] if layers[t] is None else ins[t].at[2 * k + c, layers[t]]
                cp = pltpu.make_async_remote_copy(
                    src_ref=mine, dst_ref=mine, send_sem=send_sems.at[N_CHIPS * t + k],
                    recv_sem=recv_sems.at[N_CHIPS * t + k], device_id=(x, y, 1 - c), device_id_type=MESH)
                cp.start()
                copies.append(cp)
        for cp in copies:
            cp.wait()

    return pl.pallas_call(
        body, name=name, in_specs=[hbm] * n, out_specs=[hbm] * n,
        out_shape=[jax.ShapeDtypeStruct(a.shape, a.dtype) for a in lands],
        input_output_aliases={t: t for t in range(n)},
        scratch_shapes=[pltpu.SemaphoreType.DMA((N_CHIPS * n,)), pltpu.SemaphoreType.DMA((N_CHIPS * n,))],
        compiler_params=_cp(has_side_effects=True),
    )(*lands)


def _all_reduce_packed(pack, name):
    r, c = pack.shape
    vm = pl.BlockSpec(memory_space=pltpu.VMEM)

    def body(p_ref, sum_ref, slots, send_sems, recv_sems):
        x, y, cc = _place()
        me = 4 * x + 2 * y + cc
        slots[me] = p_ref[...]
        copies = []
        for rel in range(1, 8):
            px = 1 - x if rel & 4 else x
            py = 1 - y if rel & 2 else y
            pc = 1 - cc if rel & 1 else cc
            cp = pltpu.make_async_remote_copy(
                src_ref=p_ref, dst_ref=slots.at[me], send_sem=send_sems.at[rel - 1], recv_sem=recv_sems.at[rel - 1],
                device_id=(px, py, pc), device_id_type=MESH)
            cp.start()
            copies.append(cp)
        for cp in copies:
            cp.wait()
        total = slots[0]
        for k in range(1, 8):
            total = total + slots[k]
        sum_ref[...] = total

    return pl.pallas_call(
        body, name=name, in_specs=[vm], out_specs=vm, out_shape=jax.ShapeDtypeStruct((r, c), F32),
        scratch_shapes=[pltpu.VMEM((8, r, c), F32), pltpu.SemaphoreType.DMA((7,)), pltpu.SemaphoreType.DMA((7,))],
        compiler_params=_cp(has_side_effects=True),
    )(pack)


def _pack(items):
    rows = []
    for a in items:
        flat = a.astype(F32).reshape(-1)
        pad = (-flat.shape[0]) % PACK_W
        rows.append(jnp.pad(flat, (0, pad)).reshape(-1, PACK_W))
    out = jnp.concatenate(rows, axis=0)
    pad_r = (-out.shape[0]) % 8
    return jnp.pad(out, ((0, pad_r), (0, 0)))


def _unpack(pack, shapes):
    outs, row = [], 0
    for s in shapes:
        n = int(np.prod(s))
        nr = -(-n // PACK_W)
        outs.append(pack[row:row + nr].reshape(-1)[:n].reshape(s))
        row += nr
    return outs


def _heads_to_member_major(w, axis):
    shp = w.shape
    pre, post = shp[:axis], shp[axis + 1:]
    w4 = w.reshape(pre + (N_KV_HEADS, GROUP, HEAD_DIM) + post)
    w4 = jnp.swapaxes(w4, len(pre), len(pre) + 1)
    return w4.reshape(shp)


def _heads_to_kv_major(w, axis):
    shp = w.shape
    pre, post = shp[:axis], shp[axis + 1:]
    w4 = w.reshape(pre + (GROUP, N_KV_HEADS, HEAD_DIM) + post)
    w4 = jnp.swapaxes(w4, len(pre), len(pre) + 1)
    return w4.reshape(shp)


def kernel(x, mem, norm_mix, norm_ffn, a_w_in, a_conv_w, a_w_out, kv_norm, w_kv, b_w_q, b_sinks, b_w_out, rel_bias, mem_norm, w_mem_kv, w_gate, w_up, w_down, final_norm, loss_target, m_norm_mix, m_norm_ffn, m_a_w_in, m_a_conv_w, m_a_w_out, m_kv_norm, m_w_kv, m_b_w_q, m_b_sinks, m_b_w_out, m_rel_bias, m_mem_norm, m_w_mem_kv, m_w_gate, m_w_up, m_w_down, m_final_norm, v_norm_mix, v_norm_ffn, v_a_w_in, v_a_conv_w, v_a_w_out, v_kv_norm, v_w_kv, v_b_w_q, v_b_sinks, v_b_w_out, v_rel_bias, v_mem_norm, v_w_mem_kv, v_w_gate, v_w_up, v_w_down, v_final_norm):
    t, d = x.shape[1], x.shape[2]
    tm = 512 if t % 512 == 0 and t >= 2048 else 256
    tl = 2 * tm if t % (2 * tm) == 0 else tm
    x0 = x.reshape(t, d)
    target = loss_target.reshape(t, d)
    mem2 = mem.reshape(mem.shape[1], d)
    n_mem = mem2.shape[0]
    ax, ay, ac = _place()
    chip = 2 * ax + ay
    cwid = a_conv_w.shape[2] * N_CHIPS
    qw = N_Q_HEADS * HEAD_DIM
    nq = N_CHIPS

    def own_slot(piece):
        return lax.dynamic_update_slice(lax.empty((nq,) + piece.shape, piece.dtype), piece[None],
                                        (chip,) + (0,) * piece.ndim)

    def mixer_shards(i):
        if i < N_A:
            shards = [a_w_in[i], a_w_out[i]] + ([w_mem_kv] if i == 0 else [])
        else:
            j = i - N_A
            shards = [b_w_q[j], b_w_out[j]] + ([w_kv] if j == 0 else [])
        return [a.astype(BF16) for a in shards]

    def ffn_shards(i):
        return [w_gate[i].T.astype(BF16), w_up[i].T.astype(BF16), w_down[i].astype(BF16)]

    conv_pad = jnp.pad(a_conv_w, ((0, 0), (0, 8 - a_conv_w.shape[1]), (0, (-a_conv_w.shape[2]) % 128)))
    first = mixer_shards(0)
    group_shards = {"0a": first[0:1], "0b": first[1:] + [conv_pad], "0f": ffn_shards(0)}
    for i in range(1, DEPTH):
        group_shards[str(i)] = ffn_shards(i) + mixer_shards(i)
    gathers, prev_tok = {}, []
    for key, shards in group_shards.items():
        gathers[key] = _exchange_start(shards, [own_slot(a) for a in shards], True, [None] * len(shards), False,
                                       prev_tok, "gather_start_" + key)
        prev_tok = [gathers[key]["token"]]

    def rows_full(g):
        return g.reshape((-1,) + g.shape[2:])

    def cols_full(g):
        return jnp.transpose(g, (1, 0, 2)).reshape(g.shape[1], -1)

    def landed_weights(key, after):
        g = gathers[key]
        return _exchange_wait([g], g["lands"], [list(range(len(g["lands"])))], after, "gather_wait_" + key)

    def mixer_weights(i, got):
        w_first, w_out = (cols_full(got[0]) if i < N_A else rows_full(got[0])), rows_full(got[1])
        if i >= N_A:
            w_first = jnp.concatenate([_heads_to_member_major(w_first[:, :qw], 1), w_first[:, qw:]], axis=1)
            w_out = jnp.concatenate([_heads_to_member_major(w_out[:qw, :], 0), w_out[qw:, :]], axis=0)
        return dict(w_first=w_first, w_out=w_out, extra=got[2] if len(got) > 2 else None)

    def ffn_weights(got):
        return dict(wg=rows_full(got[0]), wu=rows_full(got[1]), wd=rows_full(got[2]))

    bias, sinkt = [], []
    for j in range(2):
        bj, sj = _bias_tables(rel_bias, b_sinks[j], "bias_tables")
        bias.append(bj)
        sinkt.append(sj)

    ws = []
    xs, xmids, projs, gates, ups = [x0], [], [], [], []
    kv = memkv = wmem = wkv = None
    for i in range(DEPTH):
        xin = xs[-1]
        if i == 0:
            w = dict(w_first=cols_full(landed_weights("0a", prev_tok)[0]))
        else:
            got = landed_weights(str(i), [xin])
            w = dict(mixer_weights(i, got[3:]), **ffn_weights(got[0:3]))
        ws.append(w)
        gm = norm_mix[i].reshape(1, d)
        if i < N_A:
            proj = _norm_mm(xin, gm, w["w_first"], tl, "proj_a")
            if i == 0:
                got = landed_weights("0b", [proj])
                w["w_out"] = rows_full(got[0])
                full_mem = jnp.swapaxes(got[1], 0, 1).reshape(DEPTH, d, -1)
                wmem = jnp.transpose(full_mem, (1, 0, 2)).reshape(d, -1)
                memkv = _norm_mm(mem2, mem_norm.reshape(1, d), wmem, n_mem, "mem_kv")
                taps = got[2][:, :, 0:3, 0:a_conv_w.shape[2]]
                conv_full = jnp.transpose(taps, (1, 2, 0, 3)).reshape(N_A, 3, cwid)
            xmid = _mix_a_fwd(xin, proj, conv_full[i], memkv, i, w["w_out"], tm, "mix_a_fwd")
        else:
            j = i - N_A
            if j == 0:
                wkv = rows_full(w["extra"])
                proj, kv = _norm_mm_pair(xin, gm, w["w_first"], kv_norm.reshape(1, d), wkv, tl, "proj_b_kv")
            else:
                proj = _norm_mm(xin, gm, w["w_first"], tl, "proj_b")
            xmid = _mix_b_fwd(xin, proj, kv, bias[j], sinkt[j], memkv, i, w["w_out"], tm, "mix_b_fwd")
        if i == 0:
            w.update(ffn_weights(landed_weights("0f", [xmid])))
        xout, gate, up = _ffn_fwd(xmid, norm_ffn[i].reshape(1, d), w["wg"], w["wu"], w["wd"], tm, "ffn_fwd")
        projs.append(proj)
        xmids.append(xmid)
        gates.append(gate)
        ups.append(up)
        xs.append(xout)

    loss_part, dx, dg_final = _final_loss(xs[-1], final_norm.reshape(1, d), target, tl, "final_loss")

    def rows_pieces(g):
        return g.astype(BF16).reshape((nq, g.shape[0] // nq) + g.shape[1:])

    def cols_pieces(g):
        return jnp.transpose(g.astype(BF16).reshape(g.shape[0], nq, g.shape[1] // nq), (1, 0, 2))

    swapped = ("w_gate", "w_up")
    stacked = dict(a_w_in=a_w_in, a_w_out=a_w_out, w_kv=w_kv[None], b_w_q=b_w_q, b_w_out=b_w_out,
                   w_mem_kv=w_mem_kv, w_gate=jnp.swapaxes(w_gate, 1, 2), w_up=jnp.swapaxes(w_up, 1, 2), w_down=w_down)
    names = list(stacked)
    land = {k: lax.empty((2 * nq,) + stacked[k].shape, BF16) for k in names}
    own = {k: [None] * stacked[k].shape[0] for k in names}
    scatters, scatter_ids = [], []

    def scatter_start(key, items, both_cores):
        keys = [k for k, _, _ in items]
        st = _exchange_start([p for _, _, p in items], [land[k] for k in keys], False, [l for _, l, _ in items],
                             both_cores, [], "scatter_start_" + key)
        for (k, l, p), ld in zip(items, st["lands"]):
            land[k] = ld
            mine_piece = lax.dynamic_index_in_dim(p, chip, 0, keepdims=False)
            if l is None:
                own[k] = [mine_piece[q] for q in range(mine_piece.shape[0])]
            else:
                own[k][l] = mine_piece
        scatters.append(st)
        scatter_ids.append([names.index(k) for k in keys])
        return st["token"][0:1, 0:1]

    g_norm_mix, g_norm_ffn = [None] * DEPTH, [None] * DEPTH
    g_conv, g_sinks = [None] * 2, [None] * 2
    dmemkv = [None] * DEPTH
    dbias, dkv_main, dkv_halo = [None] * 2, [None] * 2, [None] * 2
    g_kv_norm = None
    tok = jnp.zeros((1, 1), F32)
    for i in reversed(range(DEPTH)):
        w = ws[i]
        dxm, dgate, dup, act, h2, dgf = _ffn_bwd(dx, xmids[i], norm_ffn[i].reshape(1, d) + tok, gates[i], ups[i],
                                            w["wg"], w["wu"], w["wd"], tm // 2, "ffn_bwd")
        g_norm_ffn[i] = dgf
        g_wd = _wgrad(act, dx, 2 * tm, "wgrad_down")
        g_wg = _wgrad(dgate, h2, 2 * tm, "wgrad_gate")
        g_wu = _wgrad(dup, h2, 2 * tm, "wgrad_up")
        items = [("w_gate", i, rows_pieces(g_wg)), ("w_up", i, rows_pieces(g_wu)), ("w_down", i, rows_pieces(g_wd))]
        if i == 0:
            tok = scatter_start("0f", items, True)
            items = []
        gm = norm_mix[i].reshape(1, d)
        if i < N_A:
            dproj, cat, dcw, dmemkv[i] = _mix_a_bwd(dxm, projs[i], conv_full[i] + (tok if i == 0 else 0.0), memkv, i,
                                                    w["w_out"], tm, "mix_a_bwd")
            g_conv[i] = dcw[0:3]
            g_out = _wgrad(cat, dxm, 2 * tm, "wgrad_out")
            if i == 0:
                dmemkv_all = jnp.concatenate([a.astype(BF16) for a in dmemkv], axis=1)
                _, g_mem_norm, hmem = _mm_nt_normbwd(dmemkv_all, wmem, mem2, mem_norm.reshape(1, d),
                                                     jnp.zeros((n_mem, d), F32), n_mem, "mem_kv_bwd")
                g_wmem = _wgrad(hmem, dmemkv_all, n_mem, "wgrad_mem")
                g_wmem = jnp.transpose(g_wmem.reshape(nq, d // nq, DEPTH, -1), (0, 2, 1, 3))
                gm = gm + scatter_start("0o", [("a_w_out", 0, rows_pieces(g_out)), ("w_mem_kv", None, g_wmem)], False)
            dx, g_norm_mix[i], h = _mm_nt_normbwd(dproj, w["w_first"], xs[i], gm, dxm, tl, "proj_a_bwd")
            g_in = _wgrad(h, dproj, 2 * tm, "wgrad_in_a")
            items.append(("a_w_in", i, cols_pieces(g_in)))
            if i > 0:
                items.append(("a_w_out", i, rows_pieces(g_out)))
        else:
            j = i - N_A
            dqp, cat, dkv_main[j], dkv_halo[j], dbias[j], dsk, dmemkv[i] = _mix_b_bwd(
                dxm, projs[i], kv, bias[j], sinkt[j], memkv, i, w["w_out"], tm, "mix_b_bwd")
            g_sinks[j] = dsk[0, 0:N_Q_HEADS].reshape(GROUP, N_KV_HEADS).T.reshape(N_Q_HEADS)
            g_out = _wgrad(cat, dxm, 2 * tm, "wgrad_out")
            if j == 0:
                dx, g_norm_mix[i], g_kv_norm, h, hkv, dkv = _mm_nt_normbwd_pair(
                    dqp, w["w_first"], gm, (dkv_main[0], dkv_halo[0], dkv_main[1], dkv_halo[1]), wkv,
                    kv_norm.reshape(1, d), xs[i], dxm, tm, "proj_b_kv_bwd")
            else:
                dx, g_norm_mix[i], h = _mm_nt_normbwd(dqp, w["w_first"], xs[i], gm, dxm, tl, "proj_b_bwd")
            g_q = _wgrad(h, dqp, 2 * tm, "wgrad_in_b")
            g_q = jnp.concatenate([_heads_to_kv_major(g_q[:, :qw], 1), g_q[:, qw:]], axis=1)
            g_out = jnp.concatenate([_heads_to_kv_major(g_out[:qw, :], 0), g_out[qw:, :]], axis=0)
            items += [("b_w_q", j, rows_pieces(g_q)), ("b_w_out", j, rows_pieces(g_out))]
            if j == 0:
                items.append(("w_kv", 0, rows_pieces(_wgrad(hkv, dkv, 2 * tm, "wgrad_kv"))))
        tok = scatter_start(str(i) if i else "0i", items, i > 0)
    grad_x = dx.reshape(x.shape)
    g_rel = _bias_bwd(dbias[0], dbias[1], "bias_bwd")[:, 0:N_Q_HEADS]

    small_shapes = [(DEPTH, d), (DEPTH, d), (d,), (d,), (d,), (2, N_Q_HEADS), (REL_BUCKETS, N_Q_HEADS),
                    (N_A, 3, cwid), ()]
    small = _pack([jnp.concatenate(g_norm_mix, axis=0), jnp.concatenate(g_norm_ffn, axis=0), g_kv_norm, g_mem_norm,
                   dg_final, jnp.stack(g_sinks), g_rel, jnp.stack(g_conv), loss_part[0, 0]])
    small_sum = _all_reduce_packed(small, "reduce_small")
    (gs_norm_mix, gs_norm_ffn, gs_kv_norm, gs_mem_norm, gs_final, gs_sinks, gs_rel, gs_conv_full, loss) = _unpack(
        small_sum, small_shapes)
    cq = cwid // N_CHIPS
    gs_conv = lax.dynamic_slice_in_dim(gs_conv_full, chip * cq, cq, axis=2)

    late = ("a_w_in", "a_w_out", "w_mem_kv")
    landed = dict(zip(names, _exchange_wait(scatters[:-2], [land[k] for k in names], scatter_ids[:-2], [small_sum],
                                            "scatter_wait_a")))

    def with_own(k, ld):
        return lax.dynamic_update_slice(ld, jnp.stack(own[k])[None], (2 * chip + ac,) + (0,) * (ld.ndim - 1))

    weights = dict(norm_mix=norm_mix, norm_ffn=norm_ffn, a_w_in=a_w_in, a_conv_w=a_conv_w, a_w_out=a_w_out,
                   kv_norm=kv_norm, w_kv=w_kv, b_w_q=b_w_q, b_sinks=b_sinks, b_w_out=b_w_out, rel_bias=rel_bias,
                   mem_norm=mem_norm, w_mem_kv=w_mem_kv, w_gate=w_gate, w_up=w_up, w_down=w_down,
                   final_norm=final_norm)
    moms = dict(norm_mix=m_norm_mix, norm_ffn=m_norm_ffn, a_w_in=m_a_w_in, a_conv_w=m_a_conv_w, a_w_out=m_a_w_out,
                kv_norm=m_kv_norm, w_kv=m_w_kv, b_w_q=m_b_w_q, b_sinks=m_b_sinks, b_w_out=m_b_w_out,
                rel_bias=m_rel_bias, mem_norm=m_mem_norm, w_mem_kv=m_w_mem_kv, w_gate=m_w_gate, w_up=m_w_up,
                w_down=m_w_down, final_norm=m_final_norm)
    vars_ = dict(norm_mix=v_norm_mix, norm_ffn=v_norm_ffn, a_w_in=v_a_w_in, a_conv_w=v_a_conv_w, a_w_out=v_a_w_out,
                 kv_norm=v_kv_norm, w_kv=v_w_kv, b_w_q=v_b_w_q, b_sinks=v_b_sinks, b_w_out=v_b_w_out,
                 rel_bias=v_rel_bias, mem_norm=v_mem_norm, w_mem_kv=v_w_mem_kv, w_gate=v_w_gate, w_up=v_w_up,
                 w_down=v_w_down, final_norm=v_final_norm)
    order = list(weights)
    grads, deltas, new_m, new_v = {}, {}, {}, {}
    def adamw(k, ld):
        shp, stk = weights[k].shape, ld.shape[1:]
        view = (lambda a: jnp.swapaxes(a, 1, 2)) if k in swapped else (lambda a: a.reshape(stk))
        back = (lambda a: jnp.swapaxes(a, 1, 2)) if k in swapped else (lambda a: a.reshape(shp))
        outs = _adamw_sharded(view(weights[k]), ld, view(moms[k]), view(vars_[k]), "adamw_" + k)
        grads[k], deltas[k], new_m[k], new_v[k] = [back(o) for o in outs]

    for k in names:
        if k not in late:
            adamw(k, with_own(k, landed[k]))
    late_ids = [[late.index(names[t]) for t in ids] for ids in scatter_ids[-2:]]
    late_landed = _exchange_wait(scatters[-2:], [landed[k] for k in late], late_ids, [deltas["w_down"]], "scatter_wait_b")
    filled = _core_fill([with_own(k, ld) for k, ld in zip(late, late_landed)], [0, 0, None], "fill_cores")
    for k, ld in zip(late, filled):
        adamw(k, ld)
    small_names = ["norm_mix", "norm_ffn", "kv_norm", "mem_norm", "final_norm", "b_sinks", "rel_bias", "a_conv_w"]
    small_g = [gs_norm_mix, gs_norm_ffn, gs_kv_norm, gs_mem_norm, gs_final, gs_sinks, gs_rel, gs_conv]
    shapes = [weights[k].shape for k in small_names]
    dl_p, m_p, v_p = _adamw_packed(_pack([weights[k] for k in small_names]), _pack(small_g),
                                   _pack([moms[k] for k in small_names]), _pack([vars_[k] for k in small_names]),
                                   "adamw_small")
    for k, g, dl, m2, v2 in zip(small_names, small_g, _unpack(dl_p, shapes), _unpack(m_p, shapes), _unpack(v_p, shapes)):
        grads[k], deltas[k], new_m[k], new_v[k] = g.reshape(weights[k].shape), dl, m2, v2

    return (loss, grad_x, *[grads[k] for k in order], *[deltas[k] for k in order],
            *[new_m[k] for k in order], *[new_v[k] for k in order])
```

```python
import functools
import math

import numpy as np
import jax
import jax.numpy as jnp
from jax import lax
from jax.experimental import pallas as pl
from jax.experimental.pallas import tpu as pltpu

F32 = jnp.float32
BF16 = jnp.bfloat16
MESH = pl.DeviceIdType.MESH

EPS = 1e-5
HEAD_DIM = 64
N_MEM_HEADS = 4
N_KV_HEADS = 4
GROUP = 3
N_Q_HEADS = N_KV_HEADS * GROUP
BLOCK = 128
REL_BUCKETS = 32
REL_MAX_DIST = 128
SCALE = HEAD_DIM ** -0.5
NEG = -1e30
N_CHIPS = 4
N_A = 2
DEPTH = 4

ADAM_LR = 0.001
ADAM_B1 = 0.9
ADAM_B2 = 0.999
ADAM_EPS = 1e-08
ADAM_WD = 0.01
ADAM_STEP = 10

VMEM_BIG = 56 * 1024 * 1024
PACK_W = 1024

NT = (((1,), (1,)), ((), ()))
TN = (((0,), (0,)), ((), ()))


def _cp(sem=None, vmem=None, **kw):
    return pltpu.CompilerParams(dimension_semantics=sem, vmem_limit_bytes=vmem, **kw)


def _const_spec(shape):
    nd = len(shape)
    return pl.BlockSpec(shape, lambda i, _n=nd: (0,) * _n, pipeline_mode=pl.Buffered(1))


def _row_spec(tm, n):
    return pl.BlockSpec((tm, n), lambda i: (i, 0))


def _rms_parts(xv):
    r = lax.rsqrt(jnp.mean(xv * xv, axis=-1, keepdims=True) + EPS)
    return xv * r, r


def _sigmoid(z):
    return 1.0 / (1.0 + jnp.exp(-z))


def _ff_chunks(f):
    if f % 512 == 0 or f % 256 != 0:
        return [(0, f)] if f <= 1536 else [(0, f // 2), (f // 2, f - f // 2)]
    n = f // 256
    a = (n + 1) // 2 * 256
    return [(0, a), (a, f - a)]


def _norm_mm(x, g, w, tm, name):
    t, d = x.shape
    n = w.shape[1]

    def body(x_ref, g_ref, w_ref, o_ref):
        xhat, _ = _rms_parts(x_ref[...])
        h = (xhat * g_ref[...]).astype(BF16)
        o_ref[...] = jnp.dot(h, w_ref[...], preferred_element_type=F32).astype(BF16)

    return pl.pallas_call(
        body, name=name, grid=(t // tm,),
        in_specs=[_row_spec(tm, d), _const_spec((1, d)), _const_spec((d, n))],
        out_specs=_row_spec(tm, n),
        out_shape=jax.ShapeDtypeStruct((t, n), BF16),
        compiler_params=_cp(("parallel",), VMEM_BIG),
    )(x, g, w)


def _mm_nt_normbwd(dproj, w, x_in, g, dres, tm, name):
    t, d = x_in.shape
    n = w.shape[1]

    def body(dp_ref, w_ref, x_ref, g_ref, dr_ref, dx_ref, dg_ref, h_ref):
        i = pl.program_id(0)
        xhat, r = _rms_parts(x_ref[...])
        gv = g_ref[...]
        h_ref[...] = (xhat * gv).astype(BF16)
        dh = lax.dot_general(dp_ref[...], w_ref[...], NT, preferred_element_type=F32)
        dxhat = dh * gv
        dx = r * (dxhat - xhat * jnp.mean(dxhat * xhat, axis=-1, keepdims=True))
        dx_ref[...] = dr_ref[...] + dx

        @pl.when(i == 0)
        def _():
            dg_ref[...] = jnp.zeros_like(dg_ref)

        dg_ref[...] += jnp.sum(dh * xhat, axis=0, keepdims=True)

    return pl.pallas_call(
        body, name=name, grid=(t // tm,),
        in_specs=[_row_spec(tm, n), _const_spec((d, n)), _row_spec(tm, d), _const_spec((1, d)), _row_spec(tm, d)],
        out_specs=[_row_spec(tm, d), pl.BlockSpec((1, d), lambda i: (0, 0)), _row_spec(tm, d)],
        out_shape=[jax.ShapeDtypeStruct((t, d), F32), jax.ShapeDtypeStruct((1, d), F32),
                   jax.ShapeDtypeStruct((t, d), BF16)],
        compiler_params=_cp(("arbitrary",), VMEM_BIG),
    )(dproj, w, x_in, g, dres)


def _norm_mm_pair(x, g_a, w_a, g_b, w_b, tm, name):
    t, d = x.shape
    na, nb = w_a.shape[1], w_b.shape[1]

    def body(x_ref, ga_ref, wa_ref, gb_ref, wb_ref, oa_ref, ob_ref):
        xhat, _ = _rms_parts(x_ref[...])
        ha = (xhat * ga_ref[...]).astype(BF16)
        hb = (xhat * gb_ref[...]).astype(BF16)
        oa_ref[...] = jnp.dot(ha, wa_ref[...], preferred_element_type=F32).astype(BF16)
        ob_ref[...] = jnp.dot(hb, wb_ref[...], preferred_element_type=F32).astype(BF16)

    return pl.pallas_call(
        body, name=name, grid=(t // tm,),
        in_specs=[_row_spec(tm, d), _const_spec((1, d)), _const_spec((d, na)), _const_spec((1, d)), _const_spec((d, nb))],
        out_specs=[_row_spec(tm, na), _row_spec(tm, nb)],
        out_shape=[jax.ShapeDtypeStruct((t, na), BF16), jax.ShapeDtypeStruct((t, nb), BF16)],
        compiler_params=_cp(("parallel",), VMEM_BIG),
    )(x, g_a, w_a, g_b, w_b)


def _mm_nt_normbwd_pair(dp_a, w_a, g_a, kv_parts, w_b, g_b, x_in, dres, tm, name):
    t, d = x_in.shape
    na, nb = w_a.shape[1], w_b.shape[1]
    nt = t // tm
    main_1, halo_1, main_2, halo_2 = kv_parts

    def body(dpa_ref, wa_ref, ga_ref, m1_ref, h1_ref, m2_ref, h2_ref, wb_ref, gb_ref, x_ref, dr_ref,
             dx_ref, dga_ref, dgb_ref, ha_ref, hb_ref, dkv_ref):
        i = pl.program_id(0)
        s = m1_ref[...] + m2_ref[...]
        tail = jnp.where(i == nt - 1, 0.0, h1_ref[0] + h2_ref[0])
        dkv = jnp.concatenate([s[0:tm - BLOCK], s[tm - BLOCK:] + tail], axis=0).astype(BF16)
        dkv_ref[...] = dkv
        xhat, r = _rms_parts(x_ref[...])
        ga, gb = ga_ref[...], gb_ref[...]
        ha_ref[...] = (xhat * ga).astype(BF16)
        hb_ref[...] = (xhat * gb).astype(BF16)
        dha = lax.dot_general(dpa_ref[...], wa_ref[...], NT, preferred_element_type=F32)
        dhb = lax.dot_general(dkv, wb_ref[...], NT, preferred_element_type=F32)
        dxhat = dha * ga + dhb * gb
        dx_ref[...] = dr_ref[...] + r * (dxhat - xhat * jnp.mean(dxhat * xhat, axis=-1, keepdims=True))

        @pl.when(i == 0)
        def _():
            dga_ref[...] = jnp.zeros_like(dga_ref)
            dgb_ref[...] = jnp.zeros_like(dgb_ref)

        dga_ref[...] += jnp.sum(dha * xhat, axis=0, keepdims=True)
        dgb_ref[...] += jnp.sum(dhb * xhat, axis=0, keepdims=True)

    halo_spec = pl.BlockSpec((1, BLOCK, nb), lambda i: (jnp.minimum(i + 1, nt - 1), 0, 0))
    row1 = pl.BlockSpec((1, d), lambda i: (0, 0))
    return pl.pallas_call(
        body, name=name, grid=(nt,),
        in_specs=[_row_spec(tm, na), _const_spec((d, na)), _const_spec((1, d)), _row_spec(tm, nb), halo_spec,
                  _row_spec(tm, nb), halo_spec, _const_spec((d, nb)), _const_spec((1, d)), _row_spec(tm, d),
                  _row_spec(tm, d)],
        out_specs=[_row_spec(tm, d), row1, row1, _row_spec(tm, d), _row_spec(tm, d), _row_spec(tm, nb)],
        out_shape=[jax.ShapeDtypeStruct((t, d), F32), jax.ShapeDtypeStruct((1, d), F32), jax.ShapeDtypeStruct((1, d), F32),
                   jax.ShapeDtypeStruct((t, d), BF16), jax.ShapeDtypeStruct((t, d), BF16),
                   jax.ShapeDtypeStruct((t, nb), BF16)],
        compiler_params=_cp(("arbitrary",), VMEM_BIG),
    )(dp_a, w_a, g_a, main_1, halo_1, main_2, halo_2, w_b, g_b, x_in, dres)


def _ffn_fwd(x, g, wg, wu, wd, tm, name):
    t, d = x.shape
    f = wg.shape[0]
    chunks = _ff_chunks(f)

    def body(x_ref, g_ref, wg_ref, wu_ref, wd_ref, xo_ref, gate_ref, up_ref):
        xv = x_ref[...]
        xhat, _ = _rms_parts(xv)
        h = (xhat * g_ref[...]).astype(BF16)
        acc = xv
        for c0, cw in chunks:
            gt = lax.dot_general(h, wg_ref[c0:c0 + cw, :], NT, preferred_element_type=F32)
            ut = lax.dot_general(h, wu_ref[c0:c0 + cw, :], NT, preferred_element_type=F32)
            gate_ref[:, c0:c0 + cw] = gt.astype(BF16)
            up_ref[:, c0:c0 + cw] = ut.astype(BF16)
            a = (gt * _sigmoid(gt) * ut).astype(BF16)
            acc = acc + jnp.dot(a, wd_ref[c0:c0 + cw, :], preferred_element_type=F32)
        xo_ref[...] = acc

    return pl.pallas_call(
        body, name=name, grid=(t // tm,),
        in_specs=[_row_spec(tm, d), _const_spec((1, d)), _const_spec((f, d)), _const_spec((f, d)), _const_spec((f, d))],
        out_specs=[_row_spec(tm, d), _row_spec(tm, f), _row_spec(tm, f)],
        out_shape=[jax.ShapeDtypeStruct((t, d), F32), jax.ShapeDtypeStruct((t, f), BF16),
                   jax.ShapeDtypeStruct((t, f), BF16)],
        compiler_params=_cp(("parallel",), VMEM_BIG),
    )(x, g, wg, wu, wd)


def _ffn_bwd(dxo, xm, g, gate, up, wg, wu, wd, tm, name):
    t, d = xm.shape
    f = wg.shape[0]
    chunks = _ff_chunks(f)

    def body(dxo_ref, xm_ref, g_ref, gate_ref, up_ref, wg_ref, wu_ref, wd_ref,
             dxm_ref, dgate_ref, dup_ref, act_ref, h2_ref, dg_ref):
        i = pl.program_id(0)
        dxo_v = dxo_ref[...]
        dxo_b = dxo_v.astype(BF16)
        xhat, r = _rms_parts(xm_ref[...])
        gv = g_ref[...]
        h2_ref[...] = (xhat * gv).astype(BF16)
        dh = jnp.zeros((tm, d), F32)
        for c0, cw in chunks:
            dact = lax.dot_general(dxo_b, wd_ref[c0:c0 + cw, :], NT, preferred_element_type=F32)
            gt = gate_ref[:, c0:c0 + cw].astype(F32)
            ut = up_ref[:, c0:c0 + cw].astype(F32)
            sg = _sigmoid(gt)
            sl = gt * sg
            act_ref[:, c0:c0 + cw] = (sl * ut).astype(BF16)
            dgt = (dact * ut * (sg * (1.0 + gt * (1.0 - sg)))).astype(BF16)
            dut = (dact * sl).astype(BF16)
            dgate_ref[:, c0:c0 + cw] = dgt
            dup_ref[:, c0:c0 + cw] = dut
            dh = dh + jnp.dot(dgt, wg_ref[c0:c0 + cw, :], preferred_element_type=F32)
            dh = dh + jnp.dot(dut, wu_ref[c0:c0 + cw, :], preferred_element_type=F32)
        dxhat = dh * gv
        dx = r * (dxhat - xhat * jnp.mean(dxhat * xhat, axis=-1, keepdims=True))
        dxm_ref[...] = dxo_v + dx

        @pl.when(i == 0)
        def _():
            dg_ref[...] = jnp.zeros_like(dg_ref)

        dg_ref[...] += jnp.sum(dh * xhat, axis=0, keepdims=True)

    return pl.pallas_call(
        body, name=name, grid=(t // tm,),
        in_specs=[_row_spec(tm, d), _row_spec(tm, d), _const_spec((1, d)), _row_spec(tm, f), _row_spec(tm, f),
                  _const_spec((f, d)), _const_spec((f, d)), _const_spec((f, d))],
        out_specs=[_row_spec(tm, d), _row_spec(tm, f), _row_spec(tm, f), _row_spec(tm, f), _row_spec(tm, d),
                   pl.BlockSpec((1, d), lambda i: (0, 0))],
        out_shape=[jax.ShapeDtypeStruct((t, d), F32), jax.ShapeDtypeStruct((t, f), BF16),
                   jax.ShapeDtypeStruct((t, f), BF16), jax.ShapeDtypeStruct((t, f), BF16),
                   jax.ShapeDtypeStruct((t, d), BF16), jax.ShapeDtypeStruct((1, d), F32)],
        compiler_params=_cp(("arbitrary",), VMEM_BIG),
    )(dxo, xm, g, gate, up, wg, wu, wd)


def _wgrad(a, b, tt, name):
    t, k = a.shape
    n = b.shape[1]
    nt = t // tt

    def body(a_ref, b_ref, o_ref, acc):
        i = pl.program_id(0)

        @pl.when(i == 0)
        def _():
            acc[...] = jnp.zeros_like(acc)

        acc[...] += lax.dot_general(a_ref[...].astype(BF16), b_ref[...].astype(BF16), TN,
                                    preferred_element_type=F32)

        @pl.when(i == nt - 1)
        def _():
            o_ref[...] = acc[...].astype(BF16)

    return pl.pallas_call(
        body, name=name, grid=(nt,),
        in_specs=[_row_spec(tt, k), _row_spec(tt, n)],
        out_specs=pl.BlockSpec((k, n), lambda i: (0, 0)),
        out_shape=jax.ShapeDtypeStruct((k, n), BF16),
        scratch_shapes=[pltpu.VMEM((k, n), F32)],
        compiler_params=_cp(("arbitrary",), VMEM_BIG),
    )(a, b)


def _final_loss(x, g, target, tm, name):
    t, d = x.shape

    def body(x_ref, g_ref, t_ref, loss_ref, dx_ref, dg_ref):
        i = pl.program_id(0)
        xhat, r = _rms_parts(x_ref[...])
        gv = g_ref[...]
        err = xhat * gv - t_ref[...]
        dy = err * (1.0 / d)
        dxhat = dy * gv
        dx_ref[...] = r * (dxhat - xhat * jnp.mean(dxhat * xhat, axis=-1, keepdims=True))

        @pl.when(i == 0)
        def _():
            dg_ref[...] = jnp.zeros_like(dg_ref)
            loss_ref[...] = jnp.zeros_like(loss_ref)

        dg_ref[...] += jnp.sum(dy * xhat, axis=0, keepdims=True)
        part = jnp.sum(jnp.sum(err * err, axis=-1, keepdims=True), axis=0, keepdims=True) * (0.5 / d)
        loss_ref[...] += jnp.broadcast_to(part, loss_ref.shape)

    return pl.pallas_call(
        body, name=name, grid=(t // tm,),
        in_specs=[_row_spec(tm, d), _const_spec((1, d)), _row_spec(tm, d)],
        out_specs=[pl.BlockSpec((8, 128), lambda i: (0, 0)), _row_spec(tm, d), pl.BlockSpec((1, d), lambda i: (0, 0))],
        out_shape=[jax.ShapeDtypeStruct((8, 128), F32), jax.ShapeDtypeStruct((t, d), F32),
                   jax.ShapeDtypeStruct((1, d), F32)],
        compiler_params=_cp(("arbitrary",)),
    )(x, g, target)


def _col_head(width):
    return lax.broadcasted_iota(jnp.int32, (1, width), 1) // HEAD_DIM


def _keep_head(a, colh, h):
    return jnp.where(colh == h, a, jnp.zeros_like(a))


def _softmax_cols(s, sink=None):
    m = jnp.max(s, axis=0, keepdims=True)
    if sink is not None:
        m = jnp.maximum(m, sink)
    p = jnp.exp(s - m)
    l = jnp.sum(p, axis=0, keepdims=True)
    if sink is None:
        return p * (1.0 / l), None
    es = jnp.exp(sink - m)
    inv = 1.0 / (l + es)
    return p * inv, es * inv


def _add4(v):
    return (v[0] + v[1]) + (v[2] + v[3])


def _mem_attn_fwd(qm, mk, mv):
    colh = _col_head(mk.shape[1])
    mks = mk * SCALE
    heads = range(N_MEM_HEADS)
    ss = [lax.dot_general(_keep_head(mks, colh, h), qm, NT, preferred_element_type=F32) for h in heads]
    ps = [_softmax_cols(s)[0].astype(BF16) for s in ss]
    return _add4([lax.dot_general(ps[h], _keep_head(mv, colh, h), TN, preferred_element_type=F32) for h in heads])


def _mem_attn_bwd(qm, dy_b, mk, mv):
    colh = _col_head(mk.shape[1])
    mks = mk * SCALE
    heads = range(N_MEM_HEADS)
    khs = [_keep_head(mks, colh, h) for h in heads]
    vhs = [_keep_head(mv, colh, h) for h in heads]
    ss = [lax.dot_general(khs[h], qm, NT, preferred_element_type=F32) for h in heads]
    dps = [lax.dot_general(vhs[h], dy_b, NT, preferred_element_type=F32) for h in heads]
    pbs, dsbs = [], []
    for h in heads:
        p, _ = _softmax_cols(ss[h])
        ds = p * (dps[h] - jnp.sum(p * dps[h], axis=0, keepdims=True))
        pbs.append(p.astype(BF16))
        dsbs.append(ds.astype(BF16))
    dq = _add4([lax.dot_general(dsbs[h], khs[h], TN, preferred_element_type=F32) for h in heads])
    dmk = _add4([jnp.where(colh == h, jnp.dot(dsbs[h], qm, preferred_element_type=F32) * SCALE, 0.0) for h in heads])
    dmv = _add4([jnp.where(colh == h, jnp.dot(pbs[h], dy_b, preferred_element_type=F32), 0.0) for h in heads])
    return dq, dmk, dmv


def _shift_down(v, halo, k):
    rolled = pltpu.roll(v, k, 0)
    hrolled = pltpu.roll(halo, k, 0)[0:8]
    rows = lax.broadcasted_iota(jnp.int32, (8, v.shape[1]), 0)
    first = jnp.where(rows < k, hrolled, rolled[0:8])
    return jnp.concatenate([first, rolled[8:]], axis=0)


def _shift_up(v, halo, k):
    n = v.shape[0]
    rolled = pltpu.roll(v, n - k, 0)
    hrolled = pltpu.roll(halo, 8 - k, 0)[0:8]
    rows = lax.broadcasted_iota(jnp.int32, (8, v.shape[1]), 0)
    last = jnp.where(rows >= 8 - k, hrolled, rolled[n - 8:])
    return jnp.concatenate([rolled[:n - 8], last], axis=0)


def _conv_parts(p, ph, cw, first_tile, cwid):
    u = p[:, 0:cwid].astype(F32)
    bg = p[:, cwid:2 * cwid].astype(F32)
    cg = p[:, 2 * cwid:3 * cwid].astype(F32)
    v = cg * u
    vh = ph[:, 2 * cwid:3 * cwid].astype(F32) * ph[:, 0:cwid].astype(F32)
    vh = jnp.where(first_tile, 0.0, vh)
    v1 = _shift_down(v, vh, 1)
    v2 = _shift_down(v, vh, 2)
    conv = cw[0:1, :] * v2 + cw[1:2, :] * v1 + cw[2:3, :] * v
    return u, bg, cg, v, v1, v2, conv


def _halo_prev_spec(rows, n, tm):
    per = tm // rows
    return pl.BlockSpec((rows, n), lambda i: (jnp.maximum(i * per - 1, 0), 0))


def _halo_next_spec(rows, n, tm, t):
    per = tm // rows
    last = t // rows - 1
    return pl.BlockSpec((rows, n), lambda i: (jnp.minimum((i + 1) * per, last), 0))


def _mix_a_fwd(x, proj, convw, memkv, layer, wout, tm, name):
    t, d = x.shape
    n_mem = memkv.shape[0]
    mw = N_MEM_HEADS * HEAD_DIM
    cwid = d - mw
    pw = proj.shape[1]

    def body(x_ref, p_ref, ph_ref, cw_ref, mkv_ref, wo_ref, xo_ref, cat_ref):
        i = pl.program_id(0)
        p = p_ref[...]
        _, bg, _, _, _, _, conv = _conv_parts(p, ph_ref[...], cw_ref[...], i == 0, cwid)
        ytok = (bg * conv).astype(BF16)
        mkv = mkv_ref[...]
        ymem = _mem_attn_fwd(p[:, 3 * cwid:3 * cwid + mw], mkv[:, 0:mw], mkv[:, mw:2 * mw])
        cat = jnp.concatenate([ytok, ymem.astype(BF16)], axis=1)
        cat_ref[...] = cat
        xo_ref[...] = x_ref[...] + jnp.dot(cat, wo_ref[...], preferred_element_type=F32)

    return pl.pallas_call(
        body, name=name, grid=(t // tm,),
        in_specs=[_row_spec(tm, d), _row_spec(tm, pw), _halo_prev_spec(16, pw, tm), _const_spec((3, cwid)),
                  pl.BlockSpec((n_mem, 2 * mw), lambda i: (0, layer)), _const_spec((d, d))],
        out_specs=[_row_spec(tm, d), _row_spec(tm, d)],
        out_shape=[jax.ShapeDtypeStruct((t, d), F32), jax.ShapeDtypeStruct((t, d), BF16)],
        compiler_params=_cp(("parallel",), VMEM_BIG),
    )(x, proj, proj, convw, memkv, wout)


def _mix_a_bwd(dxm, proj, convw, memkv, layer, wout, tm, name):
    t, d = dxm.shape
    n_mem = memkv.shape[0]
    mw = N_MEM_HEADS * HEAD_DIM
    cwid = d - mw
    pw = proj.shape[1]
    nt = t // tm

    def body(dx_ref, dxn_ref, p_ref, ph_ref, pn_ref, cw_ref, mkv_ref, wo_ref,
             dp_ref, dcw_ref, dmkv_ref, dmk_acc, dmv_acc):
        i = pl.program_id(0)
        p = p_ref[...]
        cw = cw_ref[...]
        wo = wo_ref[...]
        u, bg, cg, v, v1, v2, conv = _conv_parts(p, ph_ref[...], cw, i == 0, cwid)
        dcat = lax.dot_general(dx_ref[...].astype(BF16), wo, NT, preferred_element_type=F32)
        dytok = dcat[:, 0:cwid]
        dymem_b = dcat[:, cwid:d].astype(BF16)
        pn = pn_ref[...]
        dcat_n = lax.dot_general(dxn_ref[...].astype(BF16), wo[0:cwid, :], NT, preferred_element_type=F32)
        dconv_n = jnp.where(i == nt - 1, 0.0, dcat_n * pn[:, cwid:2 * cwid].astype(F32))
        dbg = dytok * conv
        dconv = dytok * bg
        dv = cw[2:3, :] * dconv + cw[1:2, :] * _shift_up(dconv, dconv_n, 1) + cw[0:1, :] * _shift_up(dconv, dconv_n, 2)
        du = dv * cg
        dcg = dv * u
        rows8 = lax.broadcasted_iota(jnp.int32, (8, cwid), 0)
        dcw = (jnp.where(rows8 == 0, jnp.sum(dconv * v2, axis=0, keepdims=True), 0.0)
               + jnp.where(rows8 == 1, jnp.sum(dconv * v1, axis=0, keepdims=True), 0.0)
               + jnp.where(rows8 == 2, jnp.sum(dconv * v, axis=0, keepdims=True), 0.0))
        mkv = mkv_ref[...]
        qm = p[:, 3 * cwid:3 * cwid + mw]
        dqm, dmk, dmv = _mem_attn_bwd(qm, dymem_b, mkv[:, 0:mw], mkv[:, mw:2 * mw])
        dp_ref[...] = jnp.concatenate([du.astype(BF16), dbg.astype(BF16), dcg.astype(BF16), dqm.astype(BF16)], axis=1)

        @pl.when(i == 0)
        def _():
            dcw_ref[...] = jnp.zeros_like(dcw_ref)
            dmk_acc[...] = jnp.zeros_like(dmk_acc)
            dmv_acc[...] = jnp.zeros_like(dmv_acc)

        dcw_ref[...] += dcw
        dmk_acc[...] += dmk
        dmv_acc[...] += dmv

        @pl.when(i == nt - 1)
        def _():
            dmkv_ref[...] = jnp.concatenate([dmk_acc[...], dmv_acc[...]], axis=1)

    return pl.pallas_call(
        body, name=name, grid=(nt,),
        in_specs=[_row_spec(tm, d), _halo_next_spec(16, d, tm, t), _row_spec(tm, pw), _halo_prev_spec(16, pw, tm),
                  _halo_next_spec(16, pw, tm, t), _const_spec((3, cwid)),
                  pl.BlockSpec((n_mem, 2 * mw), lambda i: (0, layer)), _const_spec((d, d))],
        out_specs=[_row_spec(tm, pw), pl.BlockSpec((8, cwid), lambda i: (0, 0)),
                   pl.BlockSpec((n_mem, 2 * mw), lambda i: (0, 0))],
        out_shape=[jax.ShapeDtypeStruct((t, pw), BF16),
                   jax.ShapeDtypeStruct((8, cwid), F32), jax.ShapeDtypeStruct((n_mem, 2 * mw), F32)],
        scratch_shapes=[pltpu.VMEM((n_mem, mw), F32), pltpu.VMEM((n_mem, mw), F32)],
        compiler_params=_cp(("arbitrary",), VMEM_BIG),
    )(dxm, dxm, proj, proj, proj, convw, memkv, wout)


def _rel_tables():
    qi = np.arange(BLOCK, dtype=np.int32)[:, None]
    kj = np.arange(2 * BLOCK, dtype=np.int32)[None, :]
    dist = qi + BLOCK - kj
    inw = (dist >= 0) & (dist < BLOCK)
    max_exact = REL_BUCKETS // 2
    dd = np.maximum(np.maximum(dist, 0), 1).astype(np.float32)
    large = max_exact + (np.log(dd / np.float32(max_exact)) / np.float32(math.log(REL_MAX_DIST / max_exact))
                         * np.float32(REL_BUCKETS - max_exact)).astype(np.int32)
    large = np.minimum(large, REL_BUCKETS - 1)
    bucket = np.where(np.maximum(dist, 0) < max_exact, np.maximum(dist, 0), large)
    return np.where(inw, bucket, -1).astype(np.int32)


def _bias_tables(rel_bias, sinks, name):
    bucket_t = jnp.asarray(_rel_tables().T)

    def body(rb_ref, sk_ref, bk_ref, bias_ref, sink_ref):
        bk = bk_ref[...]
        prev = lax.broadcasted_iota(jnp.int32, bk.shape, 0) < BLOCK
        for h in range(N_KV_HEADS):
            for j in range(GROUP):
                head = GROUP * h + j
                acc = jnp.full(bk.shape, NEG, F32)
                for b in range(REL_BUCKETS):
                    acc = jnp.where(bk == b, rb_ref[b, head], acc)
                bias_ref[h, :, j * BLOCK:(j + 1) * BLOCK] = acc
                bias_ref[N_KV_HEADS + h, :, j * BLOCK:(j + 1) * BLOCK] = jnp.where(prev, NEG, acc)
                sink_ref[h, :, j * BLOCK:(j + 1) * BLOCK] = jnp.full((8, BLOCK), sk_ref[0, head], F32)

    smem = pl.BlockSpec(memory_space=pltpu.SMEM)
    return pl.pallas_call(
        body, name=name,
        in_specs=[smem, smem, pl.BlockSpec(memory_space=pltpu.VMEM)],
        out_specs=[pl.BlockSpec(memory_space=pltpu.VMEM), pl.BlockSpec(memory_space=pltpu.VMEM)],
        out_shape=[jax.ShapeDtypeStruct((2 * N_KV_HEADS, 2 * BLOCK, GROUP * BLOCK), F32),
                   jax.ShapeDtypeStruct((N_KV_HEADS, 8, GROUP * BLOCK), F32)],
    )(rel_bias, sinks.reshape(1, N_Q_HEADS), bucket_t)


def _bias_bwd(dbias_a, dbias_b, name):
    bucket_t = jnp.asarray(_rel_tables().T)

    def body(da_ref, db_ref, bk_ref, o_ref):
        bk = bk_ref[...]
        ri = lax.broadcasted_iota(jnp.int32, (REL_BUCKETS, 128), 0)
        ci = lax.broadcasted_iota(jnp.int32, (REL_BUCKETS, 128), 1)
        out = jnp.zeros((REL_BUCKETS, 128), F32)
        for h in range(N_KV_HEADS):
            dsum = da_ref[h] + db_ref[h]
            for j in range(GROUP):
                head = GROUP * h + j
                seg = dsum[:, j * BLOCK:(j + 1) * BLOCK]
                for b in range(REL_BUCKETS):
                    val = jnp.sum(jnp.sum(jnp.where(bk == b, seg, 0.0), axis=0, keepdims=True), axis=1, keepdims=True)
                    out = out + jnp.where((ri == b) & (ci == head), val, 0.0)
        o_ref[...] = out

    vm = pl.BlockSpec(memory_space=pltpu.VMEM)
    return pl.pallas_call(
        body, name=name, in_specs=[vm, vm, vm], out_specs=vm,
        out_shape=jax.ShapeDtypeStruct((REL_BUCKETS, 128), F32),
    )(dbias_a, dbias_b, bucket_t)


def _stack_members(ref, r0, width):
    blk = ref[pl.ds(r0, BLOCK), 0:GROUP * width]
    return jnp.concatenate([blk[:, j * width:(j + 1) * width] for j in range(GROUP)], axis=0)


def _mix_b_fwd(x, qp, kv, bias, sinkt, memkv, layer, wout, tm, name):
    t, d = x.shape
    n_mem = memkv.shape[0]
    mw = N_MEM_HEADS * HEAD_DIM
    qw = d - mw
    kw = N_KV_HEADS * HEAD_DIM
    nb = tm // BLOCK
    rows = GROUP * BLOCK

    def body(x_ref, q_ref, kv_ref, kvh_ref, bias_ref, sink_ref, mkv_ref, wo_ref, xo_ref, cat_ref, kvx, ytok):
        i = pl.program_id(0)
        kvx[0:BLOCK, :] = kvh_ref[...]
        kvx[BLOCK:BLOCK + tm, :] = kv_ref[...]
        colh = _col_head(kw)

        def blk(b, carry):
            r0 = pl.multiple_of(b * BLOCK, BLOCK)
            win = kvx[pl.ds(r0, 2 * BLOCK), :]
            kwin = win[:, 0:kw] * SCALE
            vwin = win[:, kw:2 * kw]
            qs = _stack_members(q_ref, r0, kw)
            first = ((i == 0) & (b == 0)).astype(jnp.int32) * N_KV_HEADS
            heads = range(N_KV_HEADS)
            ss = [lax.dot_general(_keep_head(kwin, colh, h), qs, NT, preferred_element_type=F32) for h in heads]
            ps = [_softmax_cols(ss[h] + bias_ref[first + h], sink_ref[h][0:1, :])[0].astype(BF16) for h in heads]
            o = _add4([lax.dot_general(ps[h], _keep_head(vwin, colh, h), TN, preferred_element_type=F32)
                       for h in heads])
            for j in range(GROUP):
                ytok[pl.ds(r0, BLOCK), j * kw:(j + 1) * kw] = o[j * BLOCK:(j + 1) * BLOCK].astype(BF16)
            return carry

        for b_static in range(nb):
            blk(b_static, 0)
        mkv = mkv_ref[...]
        ymem = _mem_attn_fwd(q_ref[:, qw:d], mkv[:, 0:mw], mkv[:, mw:2 * mw])
        cat = jnp.concatenate([ytok[...], ymem.astype(BF16)], axis=1)
        cat_ref[...] = cat
        xo_ref[...] = x_ref[...] + jnp.dot(cat, wo_ref[...], preferred_element_type=F32)

    return pl.pallas_call(
        body, name=name, grid=(t // tm,),
        in_specs=[_row_spec(tm, d), _row_spec(tm, d), _row_spec(tm, 2 * kw), _halo_prev_spec(BLOCK, 2 * kw, tm),
                  _const_spec((2 * N_KV_HEADS, 2 * BLOCK, rows)), _const_spec((N_KV_HEADS, 8, rows)),
                  pl.BlockSpec((n_mem, 2 * mw), lambda i: (0, layer)), _const_spec((d, d))],
        out_specs=[_row_spec(tm, d), _row_spec(tm, d)],
        out_shape=[jax.ShapeDtypeStruct((t, d), F32), jax.ShapeDtypeStruct((t, d), BF16)],
        scratch_shapes=[pltpu.VMEM((tm + BLOCK, 2 * kw), BF16), pltpu.VMEM((tm, qw), BF16)],
        compiler_params=_cp(("parallel",), VMEM_BIG),
    )(x, qp, kv, kv, bias, sinkt, memkv, wout)


def _mix_b_bwd(dxm, qp, kv, bias, sinkt, memkv, layer, wout, tm, name):
    t, d = dxm.shape
    n_mem = memkv.shape[0]
    mw = N_MEM_HEADS * HEAD_DIM
    qw = d - mw
    kw = N_KV_HEADS * HEAD_DIM
    nb = tm // BLOCK
    nt = t // tm
    rows = GROUP * BLOCK

    def body(dx_ref, q_ref, kv_ref, kvh_ref, bias_ref, sink_ref, mkv_ref, wo_ref,
             dq_ref, dkv_ref, dkvh_ref, dbias_ref, dsink_ref, dmkv_ref,
             kvx, dkvx, dcat_s, dmk_acc, dmv_acc):
        i = pl.program_id(0)

        @pl.when(i == 0)
        def _():
            dbias_ref[...] = jnp.zeros_like(dbias_ref)
            dsink_ref[...] = jnp.zeros_like(dsink_ref)
            dmk_acc[...] = jnp.zeros_like(dmk_acc)
            dmv_acc[...] = jnp.zeros_like(dmv_acc)

        kvx[0:BLOCK, :] = kvh_ref[...]
        kvx[BLOCK:BLOCK + tm, :] = kv_ref[...]
        dkvx[...] = jnp.zeros_like(dkvx)
        dcat_s[...] = lax.dot_general(dx_ref[...].astype(BF16), wo_ref[...], NT,
                                      preferred_element_type=F32).astype(BF16)
        colh = _col_head(kw)
        lane8 = lax.broadcasted_iota(jnp.int32, (8, 128), 1)

        def blk(b, carry):
            r0 = pl.multiple_of(b * BLOCK, BLOCK)
            win = kvx[pl.ds(r0, 2 * BLOCK), :]
            kwin = win[:, 0:kw] * SCALE
            vwin = win[:, kw:2 * kw]
            qs = _stack_members(q_ref, r0, kw)
            dos = _stack_members(dcat_s, r0, kw)
            first = ((i == 0) & (b == 0)).astype(jnp.int32) * N_KV_HEADS
            heads = range(N_KV_HEADS)
            khs = [_keep_head(kwin, colh, h) for h in heads]
            vhs = [_keep_head(vwin, colh, h) for h in heads]
            ss = [lax.dot_general(khs[h], qs, NT, preferred_element_type=F32) for h in heads]
            dps = [lax.dot_general(vhs[h], dos, NT, preferred_element_type=F32) for h in heads]
            dsink = jnp.zeros((8, 128), F32)
            pbs, dsbs = [], []
            for h in heads:
                p, sinkp = _softmax_cols(ss[h] + bias_ref[first + h], sink_ref[h][0:1, :])
                delta = jnp.sum(p * dps[h], axis=0, keepdims=True)
                ds = p * (dps[h] - delta)
                dbias_ref[h] += ds
                sd = sinkp * delta
                for j in range(GROUP):
                    val = -jnp.sum(sd[:, j * BLOCK:(j + 1) * BLOCK], axis=1, keepdims=True)
                    dsink = dsink + jnp.where(lane8 == 4 * j + h, val, 0.0)
                pbs.append(p.astype(BF16))
                dsbs.append(ds.astype(BF16))
            dq = _add4([lax.dot_general(dsbs[h], khs[h], TN, preferred_element_type=F32) for h in heads])
            dk = _add4([jnp.where(colh == h, jnp.dot(dsbs[h], qs, preferred_element_type=F32) * SCALE, 0.0)
                        for h in heads])
            dv = _add4([jnp.where(colh == h, jnp.dot(pbs[h], dos, preferred_element_type=F32), 0.0) for h in heads])
            for j in range(GROUP):
                dq_ref[pl.ds(r0, BLOCK), j * kw:(j + 1) * kw] = dq[j * BLOCK:(j + 1) * BLOCK].astype(BF16)
            dsink_ref[...] += dsink
            dkvx[pl.ds(r0, 2 * BLOCK), :] += jnp.concatenate([dk, dv], axis=1)
            return carry

        for b_static in range(nb):
            blk(b_static, 0)
        dkvh_ref[0] = dkvx[0:BLOCK, :]
        dkv_ref[...] = dkvx[BLOCK:BLOCK + tm, :]

        mkv = mkv_ref[...]
        dqm, dmk, dmv = _mem_attn_bwd(q_ref[:, qw:d], dcat_s[:, qw:d], mkv[:, 0:mw], mkv[:, mw:2 * mw])
        dq_ref[:, qw:d] = dqm.astype(BF16)
        dmk_acc[...] += dmk
        dmv_acc[...] += dmv

        @pl.when(i == nt - 1)
        def _():
            dmkv_ref[...] = jnp.concatenate([dmk_acc[...], dmv_acc[...]], axis=1)

    return pl.pallas_call(
        body, name=name, grid=(nt,),
        in_specs=[_row_spec(tm, d), _row_spec(tm, d), _row_spec(tm, 2 * kw), _halo_prev_spec(BLOCK, 2 * kw, tm),
                  _const_spec((2 * N_KV_HEADS, 2 * BLOCK, rows)), _const_spec((N_KV_HEADS, 8, rows)),
                  pl.BlockSpec((n_mem, 2 * mw), lambda i: (0, layer)), _const_spec((d, d))],
        out_specs=[_row_spec(tm, d), _row_spec(tm, 2 * kw),
                   pl.BlockSpec((1, BLOCK, 2 * kw), lambda i: (i, 0, 0)),
                   pl.BlockSpec((N_KV_HEADS, 2 * BLOCK, rows), lambda i: (0, 0, 0)),
                   pl.BlockSpec((8, 128), lambda i: (0, 0)),
                   pl.BlockSpec((n_mem, 2 * mw), lambda i: (0, 0))],
        out_shape=[jax.ShapeDtypeStruct((t, d), BF16),
                   jax.ShapeDtypeStruct((t, 2 * kw), F32), jax.ShapeDtypeStruct((nt, BLOCK, 2 * kw), F32),
                   jax.ShapeDtypeStruct((N_KV_HEADS, 2 * BLOCK, rows), F32), jax.ShapeDtypeStruct((8, 128), F32),
                   jax.ShapeDtypeStruct((n_mem, 2 * mw), F32)],
        scratch_shapes=[pltpu.VMEM((tm + BLOCK, 2 * kw), BF16), pltpu.VMEM((tm + BLOCK, 2 * kw), F32),
                        pltpu.VMEM((tm, d), BF16), pltpu.VMEM((n_mem, mw), F32), pltpu.VMEM((n_mem, mw), F32)],
        compiler_params=_cp(("arbitrary",), VMEM_BIG),
    )(dxm, qp, kv, kv, bias, sinkt, memkv, wout)


def _adam_math(w, g, m, v):
    m2 = ADAM_B1 * m + (1.0 - ADAM_B1) * g
    v2 = ADAM_B2 * v + (1.0 - ADAM_B2) * (g * g)
    m_hat = m2 / (1.0 - ADAM_B1 ** ADAM_STEP)
    v_hat = v2 / (1.0 - ADAM_B2 ** ADAM_STEP)
    delta = -ADAM_LR * (m_hat / (jnp.sqrt(v_hat) + ADAM_EPS) + ADAM_WD * w)
    return delta, m2, v2


def _adamw_sharded(w, land, m, v, name):
    nl, r, c = w.shape
    tr = max(cand for cand in range(16, r + 1, 16) if r % cand == 0 and cand * c <= 512 * 1024)

    def body(w_ref, a_ref, m_ref, v_ref, g_ref, d_ref, mo_ref, vo_ref):
        g = a_ref[0, 0].astype(F32) + a_ref[1, 0].astype(F32)
        for k in range(1, N_CHIPS):
            g = g + (a_ref[2 * k, 0].astype(F32) + a_ref[2 * k + 1, 0].astype(F32))
        delta, m2, v2 = _adam_math(w_ref[0], g, m_ref[0], v_ref[0])
        g_ref[0] = g
        d_ref[0] = delta
        mo_ref[0] = m2
        vo_ref[0] = v2

    rs = pl.BlockSpec((1, tr, c), lambda l, i: (l, i, 0))
    ps = pl.BlockSpec((2 * N_CHIPS, 1, tr, c), lambda l, i: (0, l, i, 0))
    sd = jax.ShapeDtypeStruct((nl, r, c), F32)
    return pl.pallas_call(
        body, name=name, grid=(nl, r // tr),
        in_specs=[rs, ps, rs, rs], out_specs=[rs, rs, rs, rs], out_shape=[sd, sd, sd, sd],
        compiler_params=_cp(("parallel", "parallel"), VMEM_BIG),
    )(w, land, m, v)


def _adamw_packed(w, g, m, v, name):
    def body(w_ref, g_ref, m_ref, v_ref, d_ref, mo_ref, vo_ref):
        delta, m2, v2 = _adam_math(w_ref[...], g_ref[...], m_ref[...], v_ref[...])
        d_ref[...] = delta
        mo_ref[...] = m2
        vo_ref[...] = v2

    vm = pl.BlockSpec(memory_space=pltpu.VMEM)
    sd = jax.ShapeDtypeStruct(w.shape, F32)
    return pl.pallas_call(body, name=name, in_specs=[vm] * 4, out_specs=[vm] * 3, out_shape=[sd] * 3)(w, g, m, v)


def _place():
    return lax.axis_index("x"), lax.axis_index("y"), lax.axis_index("c")


def _hbm(a):
    return pltpu.with_memory_space_constraint(a, pltpu.HBM)


def _peers(x, y, c, both_cores):
    chips = [(1 - x, y), (x, 1 - y), (1 - x, 1 - y)]
    if not both_cores:
        return [(px, py, c) for px, py in chips]
    return [(px, py, pc) for px, py in chips for pc in (c, 1 - c)] + [(x, y, 1 - c)]


def _chip_copy(src, land, gather, layer, chip_src, slot, send_sem, recv_sem, peer):
    s = src if gather else src.at[chip_src]
    d = land.at[slot] if layer is None else land.at[slot, layer]
    return pltpu.make_async_remote_copy(src_ref=s, dst_ref=d, send_sem=send_sem, recv_sem=recv_sem,
                                        device_id=peer, device_id_type=MESH)


def _exchange_start(srcs, lands, gather, layers, both_cores, after, name):
    n = len(srcs)
    npeer = 7 if both_cores else 3
    hbm = pl.BlockSpec(memory_space=pltpu.HBM)
    sem = pl.BlockSpec(memory_space=pltpu.SEMAPHORE)

    def body(*refs):
        ins, lds = refs[:n], refs[n:2 * n]
        first_out = 2 * n + len(after)
        send_sems, recv_sems, token = refs[first_out], refs[first_out + 1], refs[-1]
        x, y, c = _place()
        slot = 2 * x + y if gather else 2 * (2 * x + y) + c
        for t in range(n):
            for r, peer in enumerate(_peers(x, y, c, both_cores)):
                _chip_copy(ins[t], lds[t], gather, layers[t], 2 * peer[0] + peer[1], slot,
                           send_sems.at[npeer * t + r], recv_sems.at[npeer * t + r], peer).start()
        token[...] = jnp.zeros_like(token)

    both = list(srcs) + list(lands)
    outs = pl.pallas_call(
        body, name=name, in_specs=[hbm] * (2 * n) + [pl.BlockSpec(memory_space=pl.ANY)] * len(after),
        out_specs=(sem, sem, *([hbm] * (2 * n)), pl.BlockSpec(memory_space=pltpu.VMEM)),
        out_shape=(pltpu.SemaphoreType.DMA((npeer * n,)), pltpu.SemaphoreType.DMA((npeer * n,)),
                   *[pltpu.HBM(a.shape, a.dtype) for a in both], jax.ShapeDtypeStruct((8, 128), F32)),
        input_output_aliases={t: 2 + t for t in range(2 * n)},
        compiler_params=_cp(has_side_effects=pltpu.SideEffectType.DATAFLOW_SIDE_EFFECTING),
    )(*[_hbm(a) for a in both], *after)
    return dict(send=outs[0], recv=outs[1], srcs=list(outs[2:2 + n]), lands=list(outs[2 + n:2 + 2 * n]),
                token=outs[-1], gather=gather, layers=list(layers), both_cores=both_cores)


def _exchange_wait(groups, lands, land_ids, after, name):
    flat = [s for g in groups for s in g["srcs"]]
    ns, nl, ng, na = len(flat), len(lands), len(groups), len(after)
    hbm = pl.BlockSpec(memory_space=pltpu.HBM)
    sem = pl.BlockSpec(memory_space=pltpu.SEMAPHORE)

    def body(*refs):
        srcs, lds = refs[:ns], refs[ns:ns + nl]
        sems = refs[ns + nl:ns + nl + 2 * ng]
        x, y, c = _place()
        k = 0
        for gi, g in enumerate(groups):
            peers = _peers(x, y, c, g["both_cores"])
            for t in range(len(g["srcs"])):
                for r, peer in enumerate(peers):
                    cp = _chip_copy(srcs[k], lds[land_ids[gi][t]], g["gather"], g["layers"][t], 0, 0,
                                    sems[2 * gi].at[len(peers) * t + r], sems[2 * gi + 1].at[len(peers) * t + r], peer)
                    cp.wait_send()
                    cp.wait_recv()
                k += 1

    both = flat + list(lands)
    sem_args = [a for g in groups for a in (g["send"], g["recv"])]
    outs = pl.pallas_call(
        body, name=name,
        in_specs=[hbm] * (ns + nl) + [sem] * (2 * ng) + [pl.BlockSpec(memory_space=pl.ANY)] * na,
        out_specs=[hbm] * (ns + nl),
        out_shape=[pltpu.HBM(a.shape, a.dtype) for a in both],
        input_output_aliases={t: t for t in range(ns + nl)},
        compiler_params=_cp(has_side_effects=pltpu.SideEffectType.DATAFLOW_SIDE_EFFECTING),
    )(*both, *sem_args, *after)
    return list(outs[ns:])


def _core_fill(lands, layers, name):
    n = len(lands)
    hbm = pl.BlockSpec(memory_space=pltpu.HBM)

    def body(*refs):
        ins = refs[:n]
        send_sems, recv_sems = refs[2 * n:]
        x, y, c = _place()
        copies = []
        for t in range(n):
            for k in range(N_CHIPS):
                mine = ins[t].at[2 * k + c] if layers[t] is None else ins[t].at[2 * k + c, layers[t]]
                cp = pltpu.make_async_remote_copy(
                    src_ref=mine, dst_ref=mine, send_sem=send_sems.at[N_CHIPS * t + k],
                    recv_sem=recv_sems.at[N_CHIPS * t + k], device_id=(x, y, 1 - c), device_id_type=MESH)
                cp.start()
                copies.append(cp)
        for cp in copies:
            cp.wait()

    return pl.pallas_call(
        body, name=name, in_specs=[hbm] * n, out_specs=[hbm] * n,
        out_shape=[jax.ShapeDtypeStruct(a.shape, a.dtype) for a in lands],
        input_output_aliases={t: t for t in range(n)},
        scratch_shapes=[pltpu.SemaphoreType.DMA((N_CHIPS * n,)), pltpu.SemaphoreType.DMA((N_CHIPS * n,))],
        compiler_params=_cp(has_side_effects=True),
    )(*lands)


def _all_reduce_packed(pack, name):
    r, c = pack.shape
    vm = pl.BlockSpec(memory_space=pltpu.VMEM)

    def body(p_ref, sum_ref, slots, send_sems, recv_sems):
        x, y, cc = _place()
        me = 4 * x + 2 * y + cc
        slots[me] = p_ref[...]
        copies = []
        for rel in range(1, 8):
            px = 1 - x if rel & 4 else x
            py = 1 - y if rel & 2 else y
            pc = 1 - cc if rel & 1 else cc
            cp = pltpu.make_async_remote_copy(
                src_ref=p_ref, dst_ref=slots.at[me], send_sem=send_sems.at[rel - 1], recv_sem=recv_sems.at[rel - 1],
                device_id=(px, py, pc), device_id_type=MESH)
            cp.start()
            copies.append(cp)
        for cp in copies:
            cp.wait()
        total = slots[0]
        for k in range(1, 8):
            total = total + slots[k]
        sum_ref[...] = total

    return pl.pallas_call(
        body, name=name, in_specs=[vm], out_specs=vm, out_shape=jax.ShapeDtypeStruct((r, c), F32),
        scratch_shapes=[pltpu.VMEM((8, r, c), F32), pltpu.SemaphoreType.DMA((7,)), pltpu.SemaphoreType.DMA((7,))],
        compiler_params=_cp(has_side_effects=True),
    )(pack)


def _pack(items):
    rows = []
    for a in items:
        flat = a.astype(F32).reshape(-1)
        pad = (-flat.shape[0]) % PACK_W
        rows.append(jnp.pad(flat, (0, pad)).reshape(-1, PACK_W))
    out = jnp.concatenate(rows, axis=0)
    pad_r = (-out.shape[0]) % 8
    return jnp.pad(out, ((0, pad_r), (0, 0)))


def _unpack(pack, shapes):
    outs, row = [], 0
    for s in shapes:
        n = int(np.prod(s))
        nr = -(-n // PACK_W)
        outs.append(pack[row:row + nr].reshape(-1)[:n].reshape(s))
        row += nr
    return outs


def _heads_to_member_major(w, axis):
    shp = w.shape
    pre, post = shp[:axis], shp[axis + 1:]
    w4 = w.reshape(pre + (N_KV_HEADS, GROUP, HEAD_DIM) + post)
    w4 = jnp.swapaxes(w4, len(pre), len(pre) + 1)
    return w4.reshape(shp)


def _heads_to_kv_major(w, axis):
    shp = w.shape
    pre, post = shp[:axis], shp[axis + 1:]
    w4 = w.reshape(pre + (GROUP, N_KV_HEADS, HEAD_DIM) + post)
    w4 = jnp.swapaxes(w4, len(pre), len(pre) + 1)
    return w4.reshape(shp)


def kernel(x, mem, norm_mix, norm_ffn, a_w_in, a_conv_w, a_w_out, kv_norm, w_kv, b_w_q, b_sinks, b_w_out, rel_bias, mem_norm, w_mem_kv, w_gate, w_up, w_down, final_norm, loss_target, m_norm_mix, m_norm_ffn, m_a_w_in, m_a_conv_w, m_a_w_out, m_kv_norm, m_w_kv, m_b_w_q, m_b_sinks, m_b_w_out, m_rel_bias, m_mem_norm, m_w_mem_kv, m_w_gate, m_w_up, m_w_down, m_final_norm, v_norm_mix, v_norm_ffn, v_a_w_in, v_a_conv_w, v_a_w_out, v_kv_norm, v_w_kv, v_b_w_q, v_b_sinks, v_b_w_out, v_rel_bias, v_mem_norm, v_w_mem_kv, v_w_gate, v_w_up, v_w_down, v_final_norm):
    t, d = x.shape[1], x.shape[2]
    tm = 512 if t % 512 == 0 and t >= 2048 else 256
    tl = 2 * tm if t % (2 * tm) == 0 else tm
    x0 = x.reshape(t, d)
    target = loss_target.reshape(t, d)
    mem2 = mem.reshape(mem.shape[1], d)
    n_mem = mem2.shape[0]
    ax, ay, ac = _place()
    chip = 2 * ax + ay
    cwid = a_conv_w.shape[2] * N_CHIPS
    qw = N_Q_HEADS * HEAD_DIM
    nq = N_CHIPS

    def own_slot(piece):
        return lax.dynamic_update_slice(lax.empty((nq,) + piece.shape, piece.dtype), piece[None],
                                        (chip,) + (0,) * piece.ndim)

    def mixer_shards(i):
        if i < N_A:
            shards = [a_w_in[i], a_w_out[i]] + ([w_mem_kv] if i == 0 else [])
        else:
            j = i - N_A
            shards = [b_w_q[j], b_w_out[j]] + ([w_kv] if j == 0 else [])
        return [a.astype(BF16) for a in shards]

    def ffn_shards(i):
        return [w_gate[i].T.astype(BF16), w_up[i].T.astype(BF16), w_down[i].astype(BF16)]

    conv_pad = jnp.pad(a_conv_w, ((0, 0), (0, 8 - a_conv_w.shape[1]), (0, (-a_conv_w.shape[2]) % 128)))
    first = mixer_shards(0)
    group_shards = {"0a": first[0:1], "0b": first[1:] + [conv_pad], "0f": ffn_shards(0)}
    for i in range(1, DEPTH):
        group_shards[str(i)] = ffn_shards(i) + mixer_shards(i)
    gathers, prev_tok = {}, []
    for key, shards in group_shards.items():
        gathers[key] = _exchange_start(shards, [own_slot(a) for a in shards], True, [None] * len(shards), False,
                                       prev_tok, "gather_start_" + key)
        prev_tok = [gathers[key]["token"]]

    def rows_full(g):
        return g.reshape((-1,) + g.shape[2:])

    def cols_full(g):
        return jnp.transpose(g, (1, 0, 2)).reshape(g.shape[1], -1)

    def landed_weights(key, after):
        g = gathers[key]
        return _exchange_wait([g], g["lands"], [list(range(len(g["lands"])))], after, "gather_wait_" + key)

    def mixer_weights(i, got):
        w_first, w_out = (cols_full(got[0]) if i < N_A else rows_full(got[0])), rows_full(got[1])
        if i >= N_A:
            w_first = jnp.concatenate([_heads_to_member_major(w_first[:, :qw], 1), w_first[:, qw:]], axis=1)
            w_out = jnp.concatenate([_heads_to_member_major(w_out[:qw, :], 0), w_out[qw:, :]], axis=0)
        return dict(w_first=w_first, w_out=w_out, extra=got[2] if len(got) > 2 else None)

    def ffn_weights(got):
        return dict(wg=rows_full(got[0]), wu=rows_full(got[1]), wd=rows_full(got[2]))

    bias, sinkt = [], []
    for j in range(2):
        bj, sj = _bias_tables(rel_bias, b_sinks[j], "bias_tables")
        bias.append(bj)
        sinkt.append(sj)

    ws = []
    xs, xmids, projs, cats, gates, ups = [x0], [], [], [], [], []
    kv = memkv = wmem = wkv = None
    for i in range(DEPTH):
        xin = xs[-1]
        if i == 0:
            w = dict(w_first=cols_full(landed_weights("0a", prev_tok)[0]))
        else:
            got = landed_weights(str(i), [xin])
            w = dict(mixer_weights(i, got[3:]), **ffn_weights(got[0:3]))
        ws.append(w)
        gm = norm_mix[i].reshape(1, d)
        if i < N_A:
            proj = _norm_mm(xin, gm, w["w_first"], tl, "proj_a")
            if i == 0:
                got = landed_weights("0b", [proj])
                w["w_out"] = rows_full(got[0])
                full_mem = jnp.swapaxes(got[1], 0, 1).reshape(DEPTH, d, -1)
                wmem = jnp.transpose(full_mem, (1, 0, 2)).reshape(d, -1)
                memkv = _norm_mm(mem2, mem_norm.reshape(1, d), wmem, n_mem, "mem_kv")
                taps = got[2][:, :, 0:3, 0:a_conv_w.shape[2]]
                conv_full = jnp.transpose(taps, (1, 2, 0, 3)).reshape(N_A, 3, cwid)
            xmid, cat = _mix_a_fwd(xin, proj, conv_full[i], memkv, i, w["w_out"], tm, "mix_a_fwd")
        else:
            j = i - N_A
            if j == 0:
                wkv = rows_full(w["extra"])
                proj, kv = _norm_mm_pair(xin, gm, w["w_first"], kv_norm.reshape(1, d), wkv, tl, "proj_b_kv")
            else:
                proj = _norm_mm(xin, gm, w["w_first"], tl, "proj_b")
            xmid, cat = _mix_b_fwd(xin, proj, kv, bias[j], sinkt[j], memkv, i, w["w_out"], tm, "mix_b_fwd")
        if i == 0:
            w.update(ffn_weights(landed_weights("0f", [xmid])))
        xout, gate, up = _ffn_fwd(xmid, norm_ffn[i].reshape(1, d), w["wg"], w["wu"], w["wd"], tm, "ffn_fwd")
        projs.append(proj)
        cats.append(cat)
        xmids.append(xmid)
        gates.append(gate)
        ups.append(up)
        xs.append(xout)

    loss_part, dx, dg_final = _final_loss(xs[-1], final_norm.reshape(1, d), target, tl, "final_loss")

    def rows_pieces(g):
        return g.astype(BF16).reshape((nq, g.shape[0] // nq) + g.shape[1:])

    def cols_pieces(g):
        return jnp.transpose(g.astype(BF16).reshape(g.shape[0], nq, g.shape[1] // nq), (1, 0, 2))

    swapped = ("w_gate", "w_up")
    stacked = dict(a_w_in=a_w_in, a_w_out=a_w_out, w_kv=w_kv[None], b_w_q=b_w_q, b_w_out=b_w_out,
                   w_mem_kv=w_mem_kv, w_gate=jnp.swapaxes(w_gate, 1, 2), w_up=jnp.swapaxes(w_up, 1, 2), w_down=w_down)
    names = list(stacked)
    land = {k: lax.empty((2 * nq,) + stacked[k].shape, BF16) for k in names}
    own = {k: [None] * stacked[k].shape[0] for k in names}
    scatters, scatter_ids = [], []

    def scatter_start(key, items, both_cores):
        keys = [k for k, _, _ in items]
        st = _exchange_start([p for _, _, p in items], [land[k] for k in keys], False, [l for _, l, _ in items],
                             both_cores, [], "scatter_start_" + key)
        for (k, l, p), ld in zip(items, st["lands"]):
            land[k] = ld
            mine_piece = lax.dynamic_index_in_dim(p, chip, 0, keepdims=False)
            if l is None:
                own[k] = [mine_piece[q] for q in range(mine_piece.shape[0])]
            else:
                own[k][l] = mine_piece
        scatters.append(st)
        scatter_ids.append([names.index(k) for k in keys])
        return st["token"][0:1, 0:1]

    g_norm_mix, g_norm_ffn = [None] * DEPTH, [None] * DEPTH
    g_conv, g_sinks = [None] * 2, [None] * 2
    dmemkv = [None] * DEPTH
    dbias, dkv_main, dkv_halo = [None] * 2, [None] * 2, [None] * 2
    g_kv_norm = None
    tok = jnp.zeros((1, 1), F32)
    for i in reversed(range(DEPTH)):
        w = ws[i]
        dxm, dgate, dup, act, h2, dgf = _ffn_bwd(dx, xmids[i], norm_ffn[i].reshape(1, d) + tok, gates[i], ups[i],
                                            w["wg"], w["wu"], w["wd"], tm // 2, "ffn_bwd")
        g_norm_ffn[i] = dgf
        g_wd = _wgrad(act, dx, 2 * tm, "wgrad_down")
        g_wg = _wgrad(dgate, h2, 2 * tm, "wgrad_gate")
        g_wu = _wgrad(dup, h2, 2 * tm, "wgrad_up")
        items = [("w_gate", i, rows_pieces(g_wg)), ("w_up", i, rows_pieces(g_wu)), ("w_down", i, rows_pieces(g_wd))]
        if i == 0:
            tok = scatter_start("0f", items, True)
            items = []
        gm = norm_mix[i].reshape(1, d)
        if i < N_A:
            dproj, dcw, dmemkv[i] = _mix_a_bwd(dxm, projs[i], conv_full[i] + (tok if i == 0 else 0.0), memkv, i,
                                                    w["w_out"], tm, "mix_a_bwd")
            g_conv[i] = dcw[0:3]
            g_out = _wgrad(cats[i], dxm, 2 * tm, "wgrad_out")
            if i == 0:
                dmemkv_all = jnp.concatenate([a.astype(BF16) for a in dmemkv], axis=1)
                _, g_mem_norm, hmem = _mm_nt_normbwd(dmemkv_all, wmem, mem2, mem_norm.reshape(1, d),
                                                     jnp.zeros((n_mem, d), F32), n_mem, "mem_kv_bwd")
                g_wmem = _wgrad(hmem, dmemkv_all, n_mem, "wgrad_mem")
                g_wmem = jnp.transpose(g_wmem.reshape(nq, d // nq, DEPTH, -1), (0, 2, 1, 3))
                gm = gm + scatter_start("0o", [("a_w_out", 0, rows_pieces(g_out)), ("w_mem_kv", None, g_wmem)], False)
            dx, g_norm_mix[i], h = _mm_nt_normbwd(dproj, w["w_first"], xs[i], gm, dxm, tl, "proj_a_bwd")
            g_in = _wgrad(h, dproj, 2 * tm, "wgrad_in_a")
            items.append(("a_w_in", i, cols_pieces(g_in)))
            if i > 0:
                items.append(("a_w_out", i, rows_pieces(g_out)))
        else:
            j = i - N_A
            dqp, dkv_main[j], dkv_halo[j], dbias[j], dsk, dmemkv[i] = _mix_b_bwd(
                dxm, projs[i], kv, bias[j], sinkt[j], memkv, i, w["w_out"], tm, "mix_b_bwd")
            g_sinks[j] = dsk[0, 0:N_Q_HEADS].reshape(GROUP, N_KV_HEADS).T.reshape(N_Q_HEADS)
            g_out = _wgrad(cats[i], dxm, 2 * tm, "wgrad_out")
            if j == 0:
                dx, g_norm_mix[i], g_kv_norm, h, hkv, dkv = _mm_nt_normbwd_pair(
                    dqp, w["w_first"], gm, (dkv_main[0], dkv_halo[0], dkv_main[1], dkv_halo[1]), wkv,
                    kv_norm.reshape(1, d), xs[i], dxm, tm, "proj_b_kv_bwd")
            else:
                dx, g_norm_mix[i], h = _mm_nt_normbwd(dqp, w["w_first"], xs[i], gm, dxm, tl, "proj_b_bwd")
            g_q = _wgrad(h, dqp, 2 * tm, "wgrad_in_b")
            g_q = jnp.concatenate([_heads_to_kv_major(g_q[:, :qw], 1), g_q[:, qw:]], axis=1)
            g_out = jnp.concatenate([_heads_to_kv_major(g_out[:qw, :], 0), g_out[qw:, :]], axis=0)
            items += [("b_w_q", j, rows_pieces(g_q)), ("b_w_out", j, rows_pieces(g_out))]
            if j == 0:
                items.append(("w_kv", 0, rows_pieces(_wgrad(hkv, dkv, 2 * tm, "wgrad_kv"))))
        tok = scatter_start(str(i) if i else "0i", items, i > 0)
    grad_x = dx.reshape(x.shape)
    g_rel = _bias_bwd(dbias[0], dbias[1], "bias_bwd")[:, 0:N_Q_HEADS]

    small_shapes = [(DEPTH, d), (DEPTH, d), (d,), (d,), (d,), (2, N_Q_HEADS), (REL_BUCKETS, N_Q_HEADS),
                    (N_A, 3, cwid), ()]
    small = _pack([jnp.concatenate(g_norm_mix, axis=0), jnp.concatenate(g_norm_ffn, axis=0), g_kv_norm, g_mem_norm,
                   dg_final, jnp.stack(g_sinks), g_rel, jnp.stack(g_conv), loss_part[0, 0]])
    small_sum = _all_reduce_packed(small, "reduce_small")
    (gs_norm_mix, gs_norm_ffn, gs_kv_norm, gs_mem_norm, gs_final, gs_sinks, gs_rel, gs_conv_full, loss) = _unpack(
        small_sum, small_shapes)
    cq = cwid // N_CHIPS
    gs_conv = lax.dynamic_slice_in_dim(gs_conv_full, chip * cq, cq, axis=2)

    late = ("a_w_in", "a_w_out", "w_mem_kv")
    landed = dict(zip(names, _exchange_wait(scatters[:-2], [land[k] for k in names], scatter_ids[:-2], [small_sum],
                                            "scatter_wait_a")))

    def with_own(k, ld):
        return lax.dynamic_update_slice(ld, jnp.stack(own[k])[None], (2 * chip + ac,) + (0,) * (ld.ndim - 1))

    weights = dict(norm_mix=norm_mix, norm_ffn=norm_ffn, a_w_in=a_w_in, a_conv_w=a_conv_w, a_w_out=a_w_out,
                   kv_norm=kv_norm, w_kv=w_kv, b_w_q=b_w_q, b_sinks=b_sinks, b_w_out=b_w_out, rel_bias=rel_bias,
                   mem_norm=mem_norm, w_mem_kv=w_mem_kv, w_gate=w_gate, w_up=w_up, w_down=w_down,
                   final_norm=final_norm)
    moms = dict(norm_mix=m_norm_mix, norm_ffn=m_norm_ffn, a_w_in=m_a_w_in, a_conv_w=m_a_conv_w, a_w_out=m_a_w_out,
                kv_norm=m_kv_norm, w_kv=m_w_kv, b_w_q=m_b_w_q, b_sinks=m_b_sinks, b_w_out=m_b_w_out,
                rel_bias=m_rel_bias, mem_norm=m_mem_norm, w_mem_kv=m_w_mem_kv, w_gate=m_w_gate, w_up=m_w_up,
                w_down=m_w_down, final_norm=m_final_norm)
    vars_ = dict(norm_mix=v_norm_mix, norm_ffn=v_norm_ffn, a_w_in=v_a_w_in, a_conv_w=v_a_conv_w, a_w_out=v_a_w_out,
                 kv_norm=v_kv_norm, w_kv=v_w_kv, b_w_q=v_b_w_q, b_sinks=v_b_sinks, b_w_out=v_b_w_out,
                 rel_bias=v_rel_bias, mem_norm=v_mem_norm, w_mem_kv=v_w_mem_kv, w_gate=v_w_gate, w_up=v_w_up,
                 w_down=v_w_down, final_norm=v_final_norm)
    order = list(weights)
    grads, deltas, new_m, new_v = {}, {}, {}, {}
    def adamw(k, ld):
        shp, stk = weights[k].shape, ld.shape[1:]
        view = (lambda a: jnp.swapaxes(a, 1, 2)) if k in swapped else (lambda a: a.reshape(stk))
        back = (lambda a: jnp.swapaxes(a, 1, 2)) if k in swapped else (lambda a: a.reshape(shp))
        outs = _adamw_sharded(view(weights[k]), ld, view(moms[k]), view(vars_[k]), "adamw_" + k)
        grads[k], deltas[k], new_m[k], new_v[k] = [back(o) for o in outs]

    for k in names:
        if k not in late:
            adamw(k, with_own(k, landed[k]))
    late_ids = [[late.index(names[t]) for t in ids] for ids in scatter_ids[-2:]]
    late_landed = _exchange_wait(scatters[-2:], [landed[k] for k in late], late_ids, [deltas["w_down"]], "scatter_wait_b")
    filled = _core_fill([with_own(k, ld) for k, ld in zip(late, late_landed)], [0, 0, None], "fill_cores")
    for k, ld in zip(late, filled):
        adamw(k, ld)
    small_names = ["norm_mix", "norm_ffn", "kv_norm", "mem_norm", "final_norm", "b_sinks", "rel_bias", "a_conv_w"]
    small_g = [gs_norm_mix, gs_norm_ffn, gs_kv_norm, gs_mem_norm, gs_final, gs_sinks, gs_rel, gs_conv]
    shapes = [weights[k].shape for k in small_names]
    dl_p, m_p, v_p = _adamw_packed(_pack([weights[k] for k in small_names]), _pack(small_g),
                                   _pack([moms[k] for k in small_names]), _pack([vars_[k] for k in small_names]),
                                   "adamw_small")
    for k, g, dl, m2, v2 in zip(small_names, small_g, _unpack(dl_p, shapes), _unpack(m_p, shapes), _unpack(v_p, shapes)):
        grads[k], deltas[k], new_m[k], new_v[k] = g.reshape(weights[k].shape), dl, m2, v2

    return (loss, grad_x, *[grads[k] for k in order], *[deltas[k] for k in order],
            *[new_m[k] for k in order], *[new_v[k] for k in order])
```

```python
import functools
import math

import numpy as np
import jax
import jax.numpy as jnp
from jax import lax
from jax.experimental import pallas as pl
from jax.experimental.pallas import tpu as pltpu

F32 = jnp.float32
BF16 = jnp.bfloat16
MESH = pl.DeviceIdType.MESH

EPS = 1e-5
HEAD_DIM = 64
N_MEM_HEADS = 4
N_KV_HEADS = 4
GROUP = 3
N_Q_HEADS = N_KV_HEADS * GROUP
BLOCK = 128
REL_BUCKETS = 32
REL_MAX_DIST = 128
SCALE = HEAD_DIM ** -0.5
NEG = -1e30
N_CHIPS = 4
N_A = 2
DEPTH = 4

ADAM_LR = 0.001
ADAM_B1 = 0.9
ADAM_B2 = 0.999
ADAM_EPS = 1e-08
ADAM_WD = 0.01
ADAM_STEP = 10

VMEM_BIG = 56 * 1024 * 1024
PACK_W = 1024

NT = (((1,), (1,)), ((), ()))
TN = (((0,), (0,)), ((), ()))


def _cp(sem=None, vmem=None, **kw):
    return pltpu.CompilerParams(dimension_semantics=sem, vmem_limit_bytes=vmem, **kw)


def _const_spec(shape):
    nd = len(shape)
    return pl.BlockSpec(shape, lambda i, _n=nd: (0,) * _n, pipeline_mode=pl.Buffered(1))


def _row_spec(tm, n):
    return pl.BlockSpec((tm, n), lambda i: (i, 0))


def _rms_parts(xv):
    r = lax.rsqrt(jnp.mean(xv * xv, axis=-1, keepdims=True) + EPS)
    return xv * r, r


def _sigmoid(z):
    return 1.0 / (1.0 + jnp.exp(-z))


def _ff_chunks(f):
    if f % 512 == 0 or f % 256 != 0:
        return [(0, f)] if f <= 1536 else [(0, f // 2), (f // 2, f - f // 2)]
    n = f // 256
    a = (n + 1) // 2 * 256
    return [(0, a), (a, f - a)]


def _norm_mm(x, g, w, tm, name):
    t, d = x.shape
    n = w.shape[1]

    def body(x_ref, g_ref, w_ref, o_ref):
        xhat, _ = _rms_parts(x_ref[...])
        h = (xhat * g_ref[...]).astype(BF16)
        o_ref[...] = jnp.dot(h, w_ref[...], preferred_element_type=F32).astype(BF16)

    return pl.pallas_call(
        body, name=name, grid=(t // tm,),
        in_specs=[_row_spec(tm, d), _const_spec((1, d)), _const_spec((d, n))],
        out_specs=_row_spec(tm, n),
        out_shape=jax.ShapeDtypeStruct((t, n), BF16),
        compiler_params=_cp(("parallel",), VMEM_BIG),
    )(x, g, w)


def _mm_nt_normbwd(dproj, w, x_in, g, dres, tm, name):
    t, d = x_in.shape
    n = w.shape[1]

    def body(dp_ref, w_ref, x_ref, g_ref, dr_ref, dx_ref, dg_ref, h_ref):
        i = pl.program_id(0)
        xhat, r = _rms_parts(x_ref[...])
        gv = g_ref[...]
        h_ref[...] = (xhat * gv).astype(BF16)
        dh = lax.dot_general(dp_ref[...], w_ref[...], NT, preferred_element_type=F32)
        dxhat = dh * gv
        dx = r * (dxhat - xhat * jnp.mean(dxhat * xhat, axis=-1, keepdims=True))
        dx_ref[...] = dr_ref[...] + dx

        @pl.when(i == 0)
        def _():
            dg_ref[...] = jnp.zeros_like(dg_ref)

        dg_ref[...] += jnp.sum(dh * xhat, axis=0, keepdims=True)

    return pl.pallas_call(
        body, name=name, grid=(t // tm,),
        in_specs=[_row_spec(tm, n), _const_spec((d, n)), _row_spec(tm, d), _const_spec((1, d)), _row_spec(tm, d)],
        out_specs=[_row_spec(tm, d), pl.BlockSpec((1, d), lambda i: (0, 0)), _row_spec(tm, d)],
        out_shape=[jax.ShapeDtypeStruct((t, d), F32), jax.ShapeDtypeStruct((1, d), F32),
                   jax.ShapeDtypeStruct((t, d), BF16)],
        compiler_params=_cp(("arbitrary",), VMEM_BIG),
    )(dproj, w, x_in, g, dres)


def _norm_mm_pair(x, g_a, w_a, g_b, w_b, tm, name):
    t, d = x.shape
    na, nb = w_a.shape[1], w_b.shape[1]

    def body(x_ref, ga_ref, wa_ref, gb_ref, wb_ref, oa_ref, ob_ref):
        xhat, _ = _rms_parts(x_ref[...])
        ha = (xhat * ga_ref[...]).astype(BF16)
        hb = (xhat * gb_ref[...]).astype(BF16)
        oa_ref[...] = jnp.dot(ha, wa_ref[...], preferred_element_type=F32).astype(BF16)
        ob_ref[...] = jnp.dot(hb, wb_ref[...], preferred_element_type=F32).astype(BF16)

    return pl.pallas_call(
        body, name=name, grid=(t // tm,),
        in_specs=[_row_spec(tm, d), _const_spec((1, d)), _const_spec((d, na)), _const_spec((1, d)), _const_spec((d, nb))],
        out_specs=[_row_spec(tm, na), _row_spec(tm, nb)],
        out_shape=[jax.ShapeDtypeStruct((t, na), BF16), jax.ShapeDtypeStruct((t, nb), BF16)],
        compiler_params=_cp(("parallel",), VMEM_BIG),
    )(x, g_a, w_a, g_b, w_b)


def _mm_nt_normbwd_pair(dp_a, w_a, g_a, kv_parts, w_b, g_b, x_in, dres, tm, name):
    t, d = x_in.shape
    na, nb = w_a.shape[1], w_b.shape[1]
    nt = t // tm
    main_1, halo_1, main_2, halo_2 = kv_parts

    def body(dpa_ref, wa_ref, ga_ref, m1_ref, h1_ref, m2_ref, h2_ref, wb_ref, gb_ref, x_ref, dr_ref,
             dx_ref, dga_ref, dgb_ref, ha_ref, hb_ref, dkv_ref):
        i = pl.program_id(0)
        s = m1_ref[...] + m2_ref[...]
        tail = jnp.where(i == nt - 1, 0.0, h1_ref[0] + h2_ref[0])
        dkv = jnp.concatenate([s[0:tm - BLOCK], s[tm - BLOCK:] + tail], axis=0).astype(BF16)
        dkv_ref[...] = dkv
        xhat, r = _rms_parts(x_ref[...])
        ga, gb = ga_ref[...], gb_ref[...]
        ha_ref[...] = (xhat * ga).astype(BF16)
        hb_ref[...] = (xhat * gb).astype(BF16)
        dha = lax.dot_general(dpa_ref[...], wa_ref[...], NT, preferred_element_type=F32)
        dhb = lax.dot_general(dkv, wb_ref[...], NT, preferred_element_type=F32)
        dxhat = dha * ga + dhb * gb
        dx_ref[...] = dr_ref[...] + r * (dxhat - xhat * jnp.mean(dxhat * xhat, axis=-1, keepdims=True))

        @pl.when(i == 0)
        def _():
            dga_ref[...] = jnp.zeros_like(dga_ref)
            dgb_ref[...] = jnp.zeros_like(dgb_ref)

        dga_ref[...] += jnp.sum(dha * xhat, axis=0, keepdims=True)
        dgb_ref[...] += jnp.sum(dhb * xhat, axis=0, keepdims=True)

    halo_spec = pl.BlockSpec((1, BLOCK, nb), lambda i: (jnp.minimum(i + 1, nt - 1), 0, 0))
    row1 = pl.BlockSpec((1, d), lambda i: (0, 0))
    return pl.pallas_call(
        body, name=name, grid=(nt,),
        in_specs=[_row_spec(tm, na), _const_spec((d, na)), _const_spec((1, d)), _row_spec(tm, nb), halo_spec,
                  _row_spec(tm, nb), halo_spec, _const_spec((d, nb)), _const_spec((1, d)), _row_spec(tm, d),
                  _row_spec(tm, d)],
        out_specs=[_row_spec(tm, d), row1, row1, _row_spec(tm, d), _row_spec(tm, d), _row_spec(tm, nb)],
        out_shape=[jax.ShapeDtypeStruct((t, d), F32), jax.ShapeDtypeStruct((1, d), F32), jax.ShapeDtypeStruct((1, d), F32),
                   jax.ShapeDtypeStruct((t, d), BF16), jax.ShapeDtypeStruct((t, d), BF16),
                   jax.ShapeDtypeStruct((t, nb), BF16)],
        compiler_params=_cp(("arbitrary",), VMEM_BIG),
    )(dp_a, w_a, g_a, main_1, halo_1, main_2, halo_2, w_b, g_b, x_in, dres)


def _ffn_fwd(x, g, wg, wu, wd, tm, name):
    t, d = x.shape
    f = wg.shape[0]
    chunks = _ff_chunks(f)

    def body(x_ref, g_ref, wg_ref, wu_ref, wd_ref, xo_ref, gate_ref, up_ref):
        xv = x_ref[...]
        xhat, _ = _rms_parts(xv)
        h = (xhat * g_ref[...]).astype(BF16)
        acc = xv
        for c0, cw in chunks:
            gt = lax.dot_general(h, wg_ref[c0:c0 + cw, :], NT, preferred_element_type=F32)
            ut = lax.dot_general(h, wu_ref[c0:c0 + cw, :], NT, preferred_element_type=F32)
            gate_ref[:, c0:c0 + cw] = gt.astype(BF16)
            up_ref[:, c0:c0 + cw] = ut.astype(BF16)
            a = (gt * _sigmoid(gt) * ut).astype(BF16)
            acc = acc + jnp.dot(a, wd_ref[c0:c0 + cw, :], preferred_element_type=F32)
        xo_ref[...] = acc

    return pl.pallas_call(
        body, name=name, grid=(t // tm,),
        in_specs=[_row_spec(tm, d), _const_spec((1, d)), _const_spec((f, d)), _const_spec((f, d)), _const_spec((f, d))],
        out_specs=[_row_spec(tm, d), _row_spec(tm, f), _row_spec(tm, f)],
        out_shape=[jax.ShapeDtypeStruct((t, d), F32), jax.ShapeDtypeStruct((t, f), BF16),
                   jax.ShapeDtypeStruct((t, f), BF16)],
        compiler_params=_cp(("parallel",), VMEM_BIG),
    )(x, g, wg, wu, wd)


def _ffn_fwd_loss(x, g, wg, wu, wd, g_final, target, tm, name):
    t, d = x.shape
    f = wg.shape[0]
    chunks = _ff_chunks(f)

    def body(x_ref, g_ref, wg_ref, wu_ref, wd_ref, gf_ref, t_ref, gate_ref, up_ref, loss_ref, dx_ref, dgf_ref):
        i = pl.program_id(0)
        xv = x_ref[...]
        xhat, _ = _rms_parts(xv)
        h = (xhat * g_ref[...]).astype(BF16)
        acc = xv
        for c0, cw in chunks:
            gt = lax.dot_general(h, wg_ref[c0:c0 + cw, :], NT, preferred_element_type=F32)
            ut = lax.dot_general(h, wu_ref[c0:c0 + cw, :], NT, preferred_element_type=F32)
            gate_ref[:, c0:c0 + cw] = gt.astype(BF16)
            up_ref[:, c0:c0 + cw] = ut.astype(BF16)
            a = (gt * _sigmoid(gt) * ut).astype(BF16)
            acc = acc + jnp.dot(a, wd_ref[c0:c0 + cw, :], preferred_element_type=F32)
        xhat_o, r_o = _rms_parts(acc)
        gf = gf_ref[...]
        err = xhat_o * gf - t_ref[...]
        dy = err * (1.0 / d)
        dxhat = dy * gf
        dx_ref[...] = r_o * (dxhat - xhat_o * jnp.mean(dxhat * xhat_o, axis=-1, keepdims=True))

        @pl.when(i == 0)
        def _():
            dgf_ref[...] = jnp.zeros_like(dgf_ref)
            loss_ref[...] = jnp.zeros_like(loss_ref)

        dgf_ref[...] += jnp.sum(dy * xhat_o, axis=0, keepdims=True)
        part = jnp.sum(jnp.sum(err * err, axis=-1, keepdims=True), axis=0, keepdims=True) * (0.5 / d)
        loss_ref[...] += jnp.broadcast_to(part, loss_ref.shape)

    return pl.pallas_call(
        body, name=name, grid=(t // tm,),
        in_specs=[_row_spec(tm, d), _const_spec((1, d)), _const_spec((f, d)), _const_spec((f, d)), _const_spec((f, d)),
                  _const_spec((1, d)), _row_spec(tm, d)],
        out_specs=[_row_spec(tm, f), _row_spec(tm, f), pl.BlockSpec((8, 128), lambda i: (0, 0)), _row_spec(tm, d),
                   pl.BlockSpec((1, d), lambda i: (0, 0))],
        out_shape=[jax.ShapeDtypeStruct((t, f), BF16), jax.ShapeDtypeStruct((t, f), BF16),
                   jax.ShapeDtypeStruct((8, 128), F32), jax.ShapeDtypeStruct((t, d), F32),
                   jax.ShapeDtypeStruct((1, d), F32)],
        compiler_params=_cp(("arbitrary",), VMEM_BIG),
    )(x, g, wg, wu, wd, g_final, target)


def _ffn_bwd(dxo, xm, g, gate, up, wg, wu, wd, tm, name):
    t, d = xm.shape
    f = wg.shape[0]
    chunks = _ff_chunks(f)

    def body(dxo_ref, xm_ref, g_ref, gate_ref, up_ref, wg_ref, wu_ref, wd_ref,
             dxm_ref, dgate_ref, dup_ref, act_ref, h2_ref, dg_ref):
        i = pl.program_id(0)
        dxo_v = dxo_ref[...]
        dxo_b = dxo_v.astype(BF16)
        xhat, r = _rms_parts(xm_ref[...])
        gv = g_ref[...]
        h2_ref[...] = (xhat * gv).astype(BF16)
        dh = jnp.zeros((tm, d), F32)
        for c0, cw in chunks:
            dact = lax.dot_general(dxo_b, wd_ref[c0:c0 + cw, :], NT, preferred_element_type=F32)
            gt = gate_ref[:, c0:c0 + cw].astype(F32)
            ut = up_ref[:, c0:c0 + cw].astype(F32)
            sg = _sigmoid(gt)
            sl = gt * sg
            act_ref[:, c0:c0 + cw] = (sl * ut).astype(BF16)
            dgt = (dact * ut * (sg * (1.0 + gt * (1.0 - sg)))).astype(BF16)
            dut = (dact * sl).astype(BF16)
            dgate_ref[:, c0:c0 + cw] = dgt
            dup_ref[:, c0:c0 + cw] = dut
            dh = dh + jnp.dot(dgt, wg_ref[c0:c0 + cw, :], preferred_element_type=F32)
            dh = dh + jnp.dot(dut, wu_ref[c0:c0 + cw, :], preferred_element_type=F32)
        dxhat = dh * gv
        dx = r * (dxhat - xhat * jnp.mean(dxhat * xhat, axis=-1, keepdims=True))
        dxm_ref[...] = dxo_v + dx

        @pl.when(i == 0)
        def _():
            dg_ref[...] = jnp.zeros_like(dg_ref)

        dg_ref[...] += jnp.sum(dh * xhat, axis=0, keepdims=True)

    return pl.pallas_call(
        body, name=name, grid=(t // tm,),
        in_specs=[_row_spec(tm, d), _row_spec(tm, d), _const_spec((1, d)), _row_spec(tm, f), _row_spec(tm, f),
                  _const_spec((f, d)), _const_spec((f, d)), _const_spec((f, d))],
        out_specs=[_row_spec(tm, d), _row_spec(tm, f), _row_spec(tm, f), _row_spec(tm, f), _row_spec(tm, d),
                   pl.BlockSpec((1, d), lambda i: (0, 0))],
        out_shape=[jax.ShapeDtypeStruct((t, d), F32), jax.ShapeDtypeStruct((t, f), BF16),
                   jax.ShapeDtypeStruct((t, f), BF16), jax.ShapeDtypeStruct((t, f), BF16),
                   jax.ShapeDtypeStruct((t, d), BF16), jax.ShapeDtypeStruct((1, d), F32)],
        compiler_params=_cp(("arbitrary",), VMEM_BIG),
    )(dxo, xm, g, gate, up, wg, wu, wd)


def _wgrad(a, b, tt, name):
    t, k = a.shape
    n = b.shape[1]
    nt = t // tt

    def body(a_ref, b_ref, o_ref, acc):
        i = pl.program_id(0)

        @pl.when(i == 0)
        def _():
            acc[...] = jnp.zeros_like(acc)

        acc[...] += lax.dot_general(a_ref[...].astype(BF16), b_ref[...].astype(BF16), TN,
                                    preferred_element_type=F32)

        @pl.when(i == nt - 1)
        def _():
            o_ref[...] = acc[...].astype(BF16)

    return pl.pallas_call(
        body, name=name, grid=(nt,),
        in_specs=[_row_spec(tt, k), _row_spec(tt, n)],
        out_specs=pl.BlockSpec((k, n), lambda i: (0, 0)),
        out_shape=jax.ShapeDtypeStruct((k, n), BF16),
        scratch_shapes=[pltpu.VMEM((k, n), F32)],
        compiler_params=_cp(("arbitrary",), VMEM_BIG),
    )(a, b)


def _col_head(width):
    return lax.broadcasted_iota(jnp.int32, (1, width), 1) // HEAD_DIM


def _keep_head(a, colh, h):
    return jnp.where(colh == h, a, jnp.zeros_like(a))


def _softmax_cols(s, sink=None):
    m = jnp.max(s, axis=0, keepdims=True)
    if sink is not None:
        m = jnp.maximum(m, sink)
    p = jnp.exp(s - m)
    l = jnp.sum(p, axis=0, keepdims=True)
    if sink is None:
        return p * (1.0 / l), None
    es = jnp.exp(sink - m)
    inv = 1.0 / (l + es)
    return p * inv, es * inv


def _add4(v):
    return (v[0] + v[1]) + (v[2] + v[3])


def _mem_attn_fwd(qm, mk, mv):
    colh = _col_head(mk.shape[1])
    mks = mk * SCALE
    heads = range(N_MEM_HEADS)
    ss = [lax.dot_general(_keep_head(mks, colh, h), qm, NT, preferred_element_type=F32) for h in heads]
    ps = [_softmax_cols(s)[0].astype(BF16) for s in ss]
    return _add4([lax.dot_general(ps[h], _keep_head(mv, colh, h), TN, preferred_element_type=F32) for h in heads])


def _mem_attn_bwd(qm, dy_b, mk, mv):
    colh = _col_head(mk.shape[1])
    mks = mk * SCALE
    heads = range(N_MEM_HEADS)
    khs = [_keep_head(mks, colh, h) for h in heads]
    vhs = [_keep_head(mv, colh, h) for h in heads]
    ss = [lax.dot_general(khs[h], qm, NT, preferred_element_type=F32) for h in heads]
    dps = [lax.dot_general(vhs[h], dy_b, NT, preferred_element_type=F32) for h in heads]
    pbs, dsbs = [], []
    for h in heads:
        p, _ = _softmax_cols(ss[h])
        ds = p * (dps[h] - jnp.sum(p * dps[h], axis=0, keepdims=True))
        pbs.append(p.astype(BF16))
        dsbs.append(ds.astype(BF16))
    dq = _add4([lax.dot_general(dsbs[h], khs[h], TN, preferred_element_type=F32) for h in heads])
    dmk = _add4([jnp.where(colh == h, jnp.dot(dsbs[h], qm, preferred_element_type=F32) * SCALE, 0.0) for h in heads])
    dmv = _add4([jnp.where(colh == h, jnp.dot(pbs[h], dy_b, preferred_element_type=F32), 0.0) for h in heads])
    return dq, dmk, dmv


def _shift_down(v, halo, k):
    rolled = pltpu.roll(v, k, 0)
    hrolled = pltpu.roll(halo, k, 0)[0:8]
    rows = lax.broadcasted_iota(jnp.int32, (8, v.shape[1]), 0)
    first = jnp.where(rows < k, hrolled, rolled[0:8])
    return jnp.concatenate([first, rolled[8:]], axis=0)


def _shift_up(v, halo, k):
    n = v.shape[0]
    rolled = pltpu.roll(v, n - k, 0)
    hrolled = pltpu.roll(halo, 8 - k, 0)[0:8]
    rows = lax.broadcasted_iota(jnp.int32, (8, v.shape[1]), 0)
    last = jnp.where(rows >= 8 - k, hrolled, rolled[n - 8:])
    return jnp.concatenate([rolled[:n - 8], last], axis=0)


def _conv_parts(p, ph, cw, first_tile, cwid):
    u = p[:, 0:cwid].astype(F32)
    bg = p[:, cwid:2 * cwid].astype(F32)
    cg = p[:, 2 * cwid:3 * cwid].astype(F32)
    v = cg * u
    vh = ph[:, 2 * cwid:3 * cwid].astype(F32) * ph[:, 0:cwid].astype(F32)
    vh = jnp.where(first_tile, 0.0, vh)
    v1 = _shift_down(v, vh, 1)
    v2 = _shift_down(v, vh, 2)
    conv = cw[0:1, :] * v2 + cw[1:2, :] * v1 + cw[2:3, :] * v
    return u, bg, cg, v, v1, v2, conv


def _halo_prev_spec(rows, n, tm):
    per = tm // rows
    return pl.BlockSpec((rows, n), lambda i: (jnp.maximum(i * per - 1, 0), 0))


def _halo_next_spec(rows, n, tm, t):
    per = tm // rows
    last = t // rows - 1
    return pl.BlockSpec((rows, n), lambda i: (jnp.minimum((i + 1) * per, last), 0))


def _mix_a_fwd(x, proj, convw, memkv, layer, wout, tm, name):
    t, d = x.shape
    n_mem = memkv.shape[0]
    mw = N_MEM_HEADS * HEAD_DIM
    cwid = d - mw
    pw = proj.shape[1]

    def body(x_ref, p_ref, ph_ref, cw_ref, mkv_ref, wo_ref, xo_ref, cat_ref):
        i = pl.program_id(0)
        p = p_ref[...]
        _, bg, _, _, _, _, conv = _conv_parts(p, ph_ref[...], cw_ref[...], i == 0, cwid)
        ytok = (bg * conv).astype(BF16)
        mkv = mkv_ref[...]
        ymem = _mem_attn_fwd(p[:, 3 * cwid:3 * cwid + mw], mkv[:, 0:mw], mkv[:, mw:2 * mw])
        cat = jnp.concatenate([ytok, ymem.astype(BF16)], axis=1)
        cat_ref[...] = cat
        xo_ref[...] = x_ref[...] + jnp.dot(cat, wo_ref[...], preferred_element_type=F32)

    return pl.pallas_call(
        body, name=name, grid=(t // tm,),
        in_specs=[_row_spec(tm, d), _row_spec(tm, pw), _halo_prev_spec(16, pw, tm), _const_spec((3, cwid)),
                  pl.BlockSpec((n_mem, 2 * mw), lambda i: (0, layer)), _const_spec((d, d))],
        out_specs=[_row_spec(tm, d), _row_spec(tm, d)],
        out_shape=[jax.ShapeDtypeStruct((t, d), F32), jax.ShapeDtypeStruct((t, d), BF16)],
        compiler_params=_cp(("parallel",), VMEM_BIG),
    )(x, proj, proj, convw, memkv, wout)


def _mix_a_bwd(dxm, proj, convw, memkv, layer, wout, tm, name):
    t, d = dxm.shape
    n_mem = memkv.shape[0]
    mw = N_MEM_HEADS * HEAD_DIM
    cwid = d - mw
    pw = proj.shape[1]
    nt = t // tm

    def body(dx_ref, dxn_ref, p_ref, ph_ref, pn_ref, cw_ref, mkv_ref, wo_ref,
             dp_ref, dcw_ref, dmkv_ref, dmk_acc, dmv_acc):
        i = pl.program_id(0)
        p = p_ref[...]
        cw = cw_ref[...]
        wo = wo_ref[...]
        u, bg, cg, v, v1, v2, conv = _conv_parts(p, ph_ref[...], cw, i == 0, cwid)
        dcat = lax.dot_general(dx_ref[...].astype(BF16), wo, NT, preferred_element_type=F32)
        dytok = dcat[:, 0:cwid]
        dymem_b = dcat[:, cwid:d].astype(BF16)
        pn = pn_ref[...]
        dcat_n = lax.dot_general(dxn_ref[...].astype(BF16), wo[0:cwid, :], NT, preferred_element_type=F32)
        dconv_n = jnp.where(i == nt - 1, 0.0, dcat_n * pn[:, cwid:2 * cwid].astype(F32))
        dbg = dytok * conv
        dconv = dytok * bg
        dv = cw[2:3, :] * dconv + cw[1:2, :] * _shift_up(dconv, dconv_n, 1) + cw[0:1, :] * _shift_up(dconv, dconv_n, 2)
        du = dv * cg
        dcg = dv * u
        rows8 = lax.broadcasted_iota(jnp.int32, (8, cwid), 0)
        dcw = (jnp.where(rows8 == 0, jnp.sum(dconv * v2, axis=0, keepdims=True), 0.0)
               + jnp.where(rows8 == 1, jnp.sum(dconv * v1, axis=0, keepdims=True), 0.0)
               + jnp.where(rows8 == 2, jnp.sum(dconv * v, axis=0, keepdims=True), 0.0))
        mkv = mkv_ref[...]
        qm = p[:, 3 * cwid:3 * cwid + mw]
        dqm, dmk, dmv = _mem_attn_bwd(qm, dymem_b, mkv[:, 0:mw], mkv[:, mw:2 * mw])
        dp_ref[...] = jnp.concatenate([du.astype(BF16), dbg.astype(BF16), dcg.astype(BF16), dqm.astype(BF16)], axis=1)

        @pl.when(i == 0)
        def _():
            dcw_ref[...] = jnp.zeros_like(dcw_ref)
            dmk_acc[...] = jnp.zeros_like(dmk_acc)
            dmv_acc[...] = jnp.zeros_like(dmv_acc)

        dcw_ref[...] += dcw
        dmk_acc[...] += dmk
        dmv_acc[...] += dmv

        @pl.when(i == nt - 1)
        def _():
            dmkv_ref[...] = jnp.concatenate([dmk_acc[...], dmv_acc[...]], axis=1)

    return pl.pallas_call(
        body, name=name, grid=(nt,),
        in_specs=[_row_spec(tm, d), _halo_next_spec(16, d, tm, t), _row_spec(tm, pw), _halo_prev_spec(16, pw, tm),
                  _halo_next_spec(16, pw, tm, t), _const_spec((3, cwid)),
                  pl.BlockSpec((n_mem, 2 * mw), lambda i: (0, layer)), _const_spec((d, d))],
        out_specs=[_row_spec(tm, pw), pl.BlockSpec((8, cwid), lambda i: (0, 0)),
                   pl.BlockSpec((n_mem, 2 * mw), lambda i: (0, 0))],
        out_shape=[jax.ShapeDtypeStruct((t, pw), BF16),
                   jax.ShapeDtypeStruct((8, cwid), F32), jax.ShapeDtypeStruct((n_mem, 2 * mw), F32)],
        scratch_shapes=[pltpu.VMEM((n_mem, mw), F32), pltpu.VMEM((n_mem, mw), F32)],
        compiler_params=_cp(("arbitrary",), VMEM_BIG),
    )(dxm, dxm, proj, proj, proj, convw, memkv, wout)


def _rel_tables():
    qi = np.arange(BLOCK, dtype=np.int32)[:, None]
    kj = np.arange(2 * BLOCK, dtype=np.int32)[None, :]
    dist = qi + BLOCK - kj
    inw = (dist >= 0) & (dist < BLOCK)
    max_exact = REL_BUCKETS // 2
    dd = np.maximum(np.maximum(dist, 0), 1).astype(np.float32)
    large = max_exact + (np.log(dd / np.float32(max_exact)) / np.float32(math.log(REL_MAX_DIST / max_exact))
                         * np.float32(REL_BUCKETS - max_exact)).astype(np.int32)
    large = np.minimum(large, REL_BUCKETS - 1)
    bucket = np.where(np.maximum(dist, 0) < max_exact, np.maximum(dist, 0), large)
    return np.where(inw, bucket, -1).astype(np.int32)


def _bias_tables(rel_bias, sinks, name):
    bucket_t = jnp.asarray(_rel_tables().T)

    def body(rb_ref, sk_ref, bk_ref, bias_ref, sink_ref):
        bk = bk_ref[...]
        prev = lax.broadcasted_iota(jnp.int32, bk.shape, 0) < BLOCK
        for h in range(N_KV_HEADS):
            for j in range(GROUP):
                head = GROUP * h + j
                acc = jnp.full(bk.shape, NEG, F32)
                for b in range(REL_BUCKETS):
                    acc = jnp.where(bk == b, rb_ref[b, head], acc)
                bias_ref[h, :, j * BLOCK:(j + 1) * BLOCK] = acc
                bias_ref[N_KV_HEADS + h, :, j * BLOCK:(j + 1) * BLOCK] = jnp.where(prev, NEG, acc)
                sink_ref[h, :, j * BLOCK:(j + 1) * BLOCK] = jnp.full((8, BLOCK), sk_ref[0, head], F32)

    smem = pl.BlockSpec(memory_space=pltpu.SMEM)
    return pl.pallas_call(
        body, name=name,
        in_specs=[smem, smem, pl.BlockSpec(memory_space=pltpu.VMEM)],
        out_specs=[pl.BlockSpec(memory_space=pltpu.VMEM), pl.BlockSpec(memory_space=pltpu.VMEM)],
        out_shape=[jax.ShapeDtypeStruct((2 * N_KV_HEADS, 2 * BLOCK, GROUP * BLOCK), F32),
                   jax.ShapeDtypeStruct((N_KV_HEADS, 8, GROUP * BLOCK), F32)],
    )(rel_bias, sinks.reshape(1, N_Q_HEADS), bucket_t)


def _bias_bwd(dbias_a, dbias_b, name):
    bucket_t = jnp.asarray(_rel_tables().T)

    def body(da_ref, db_ref, bk_ref, o_ref):
        bk = bk_ref[...]
        ri = lax.broadcasted_iota(jnp.int32, (REL_BUCKETS, 128), 0)
        ci = lax.broadcasted_iota(jnp.int32, (REL_BUCKETS, 128), 1)
        out = jnp.zeros((REL_BUCKETS, 128), F32)
        for h in range(N_KV_HEADS):
            dsum = da_ref[h] + db_ref[h]
            for j in range(GROUP):
                head = GROUP * h + j
                seg = dsum[:, j * BLOCK:(j + 1) * BLOCK]
                for b in range(REL_BUCKETS):
                    val = jnp.sum(jnp.sum(jnp.where(bk == b, seg, 0.0), axis=0, keepdims=True), axis=1, keepdims=True)
                    out = out + jnp.where((ri == b) & (ci == head), val, 0.0)
        o_ref[...] = out

    vm = pl.BlockSpec(memory_space=pltpu.VMEM)
    return pl.pallas_call(
        body, name=name, in_specs=[vm, vm, vm], out_specs=vm,
        out_shape=jax.ShapeDtypeStruct((REL_BUCKETS, 128), F32),
    )(dbias_a, dbias_b, bucket_t)


def _stack_members(ref, r0, width):
    blk = ref[pl.ds(r0, BLOCK), 0:GROUP * width]
    return jnp.concatenate([blk[:, j * width:(j + 1) * width] for j in range(GROUP)], axis=0)


def _mix_b_fwd(x, qp, kv, bias, sinkt, memkv, layer, wout, tm, name):
    t, d = x.shape
    n_mem = memkv.shape[0]
    mw = N_MEM_HEADS * HEAD_DIM
    qw = d - mw
    kw = N_KV_HEADS * HEAD_DIM
    nb = tm // BLOCK
    rows = GROUP * BLOCK

    def body(x_ref, q_ref, kv_ref, kvh_ref, bias_ref, sink_ref, mkv_ref, wo_ref, xo_ref, cat_ref, kvx, ytok):
        i = pl.program_id(0)
        kvx[0:BLOCK, :] = kvh_ref[...]
        kvx[BLOCK:BLOCK + tm, :] = kv_ref[...]
        colh = _col_head(kw)

        def blk(b, carry):
            r0 = pl.multiple_of(b * BLOCK, BLOCK)
            win = kvx[pl.ds(r0, 2 * BLOCK), :]
            kwin = win[:, 0:kw] * SCALE
            vwin = win[:, kw:2 * kw]
            qs = _stack_members(q_ref, r0, kw)
            first = ((i == 0) & (b == 0)).astype(jnp.int32) * N_KV_HEADS
            heads = range(N_KV_HEADS)
            ss = [lax.dot_general(_keep_head(kwin, colh, h), qs, NT, preferred_element_type=F32) for h in heads]
            ps = [_softmax_cols(ss[h] + bias_ref[first + h], sink_ref[h][0:1, :])[0].astype(BF16) for h in heads]
            o = _add4([lax.dot_general(ps[h], _keep_head(vwin, colh, h), TN, preferred_element_type=F32)
                       for h in heads])
            for j in range(GROUP):
                ytok[pl.ds(r0, BLOCK), j * kw:(j + 1) * kw] = o[j * BLOCK:(j + 1) * BLOCK].astype(BF16)
            return carry

        for b_static in range(nb):
            blk(b_static, 0)
        mkv = mkv_ref[...]
        ymem = _mem_attn_fwd(q_ref[:, qw:d], mkv[:, 0:mw], mkv[:, mw:2 * mw])
        cat = jnp.concatenate([ytok[...], ymem.astype(BF16)], axis=1)
        cat_ref[...] = cat
        xo_ref[...] = x_ref[...] + jnp.dot(cat, wo_ref[...], preferred_element_type=F32)

    return pl.pallas_call(
        body, name=name, grid=(t // tm,),
        in_specs=[_row_spec(tm, d), _row_spec(tm, d), _row_spec(tm, 2 * kw), _halo_prev_spec(BLOCK, 2 * kw, tm),
                  _const_spec((2 * N_KV_HEADS, 2 * BLOCK, rows)), _const_spec((N_KV_HEADS, 8, rows)),
                  pl.BlockSpec((n_mem, 2 * mw), lambda i: (0, layer)), _const_spec((d, d))],
        out_specs=[_row_spec(tm, d), _row_spec(tm, d)],
        out_shape=[jax.ShapeDtypeStruct((t, d), F32), jax.ShapeDtypeStruct((t, d), BF16)],
        scratch_shapes=[pltpu.VMEM((tm + BLOCK, 2 * kw), BF16), pltpu.VMEM((tm, qw), BF16)],
        compiler_params=_cp(("parallel",), VMEM_BIG),
    )(x, qp, kv, kv, bias, sinkt, memkv, wout)


def _mix_b_bwd(dxm, qp, kv, bias, sinkt, memkv, layer, wout, tm, name):
    t, d = dxm.shape
    n_mem = memkv.shape[0]
    mw = N_MEM_HEADS * HEAD_DIM
    qw = d - mw
    kw = N_KV_HEADS * HEAD_DIM
    nb = tm // BLOCK
    nt = t // tm
    rows = GROUP * BLOCK

    def body(dx_ref, q_ref, kv_ref, kvh_ref, bias_ref, sink_ref, mkv_ref, wo_ref,
             dq_ref, dkv_ref, dkvh_ref, dbias_ref, dsink_ref, dmkv_ref,
             kvx, dkvx, dcat_s, dmk_acc, dmv_acc):
        i = pl.program_id(0)

        @pl.when(i == 0)
        def _():
            dbias_ref[...] = jnp.zeros_like(dbias_ref)
            dsink_ref[...] = jnp.zeros_like(dsink_ref)
            dmk_acc[...] = jnp.zeros_like(dmk_acc)
            dmv_acc[...] = jnp.zeros_like(dmv_acc)

        kvx[0:BLOCK, :] = kvh_ref[...]
        kvx[BLOCK:BLOCK + tm, :] = kv_ref[...]
        dkvx[...] = jnp.zeros_like(dkvx)
        dcat_s[...] = lax.dot_general(dx_ref[...].astype(BF16), wo_ref[...], NT,
                                      preferred_element_type=F32).astype(BF16)
        colh = _col_head(kw)
        lane8 = lax.broadcasted_iota(jnp.int32, (8, 128), 1)

        def blk(b, carry):
            r0 = pl.multiple_of(b * BLOCK, BLOCK)
            win = kvx[pl.ds(r0, 2 * BLOCK), :]
            kwin = win[:, 0:kw] * SCALE
            vwin = win[:, kw:2 * kw]
            qs = _stack_members(q_ref, r0, kw)
            dos = _stack_members(dcat_s, r0, kw)
            first = ((i == 0) & (b == 0)).astype(jnp.int32) * N_KV_HEADS
            heads = range(N_KV_HEADS)
            khs = [_keep_head(kwin, colh, h) for h in heads]
            vhs = [_keep_head(vwin, colh, h) for h in heads]
            ss = [lax.dot_general(khs[h], qs, NT, preferred_element_type=F32) for h in heads]
            dps = [lax.dot_general(vhs[h], dos, NT, preferred_element_type=F32) for h in heads]
            dsink = jnp.zeros((8, 128), F32)
            pbs, dsbs = [], []
            for h in heads:
                p, sinkp = _softmax_cols(ss[h] + bias_ref[first + h], sink_ref[h][0:1, :])
                delta = jnp.sum(p * dps[h], axis=0, keepdims=True)
                ds = p * (dps[h] - delta)
                dbias_ref[h] += ds
                sd = sinkp * delta
                for j in range(GROUP):
                    val = -jnp.sum(sd[:, j * BLOCK:(j + 1) * BLOCK], axis=1, keepdims=True)
                    dsink = dsink + jnp.where(lane8 == 4 * j + h, val, 0.0)
                pbs.append(p.astype(BF16))
                dsbs.append(ds.astype(BF16))
            dq = _add4([lax.dot_general(dsbs[h], khs[h], TN, preferred_element_type=F32) for h in heads])
            dk = _add4([jnp.where(colh == h, jnp.dot(dsbs[h], qs, preferred_element_type=F32) * SCALE, 0.0)
                        for h in heads])
            dv = _add4([jnp.where(colh == h, jnp.dot(pbs[h], dos, preferred_element_type=F32), 0.0) for h in heads])
            for j in range(GROUP):
                dq_ref[pl.ds(r0, BLOCK), j * kw:(j + 1) * kw] = dq[j * BLOCK:(j + 1) * BLOCK].astype(BF16)
            dsink_ref[...] += dsink
            dkvx[pl.ds(r0, 2 * BLOCK), :] += jnp.concatenate([dk, dv], axis=1)
            return carry

        for b_static in range(nb):
            blk(b_static, 0)
        dkvh_ref[0] = dkvx[0:BLOCK, :]
        dkv_ref[...] = dkvx[BLOCK:BLOCK + tm, :]

        mkv = mkv_ref[...]
        dqm, dmk, dmv = _mem_attn_bwd(q_ref[:, qw:d], dcat_s[:, qw:d], mkv[:, 0:mw], mkv[:, mw:2 * mw])
        dq_ref[:, qw:d] = dqm.astype(BF16)
        dmk_acc[...] += dmk
        dmv_acc[...] += dmv

        @pl.when(i == nt - 1)
        def _():
            dmkv_ref[...] = jnp.concatenate([dmk_acc[...], dmv_acc[...]], axis=1)

    return pl.pallas_call(
        body, name=name, grid=(nt,),
        in_specs=[_row_spec(tm, d), _row_spec(tm, d), _row_spec(tm, 2 * kw), _halo_prev_spec(BLOCK, 2 * kw, tm),
                  _const_spec((2 * N_KV_HEADS, 2 * BLOCK, rows)), _const_spec((N_KV_HEADS, 8, rows)),
                  pl.BlockSpec((n_mem, 2 * mw), lambda i: (0, layer)), _const_spec((d, d))],
        out_specs=[_row_spec(tm, d), _row_spec(tm, 2 * kw),
                   pl.BlockSpec((1, BLOCK, 2 * kw), lambda i: (i, 0, 0)),
                   pl.BlockSpec((N_KV_HEADS, 2 * BLOCK, rows), lambda i: (0, 0, 0)),
                   pl.BlockSpec((8, 128), lambda i: (0, 0)),
                   pl.BlockSpec((n_mem, 2 * mw), lambda i: (0, 0))],
        out_shape=[jax.ShapeDtypeStruct((t, d), BF16),
                   jax.ShapeDtypeStruct((t, 2 * kw), F32), jax.ShapeDtypeStruct((nt, BLOCK, 2 * kw), F32),
                   jax.ShapeDtypeStruct((N_KV_HEADS, 2 * BLOCK, rows), F32), jax.ShapeDtypeStruct((8, 128), F32),
                   jax.ShapeDtypeStruct((n_mem, 2 * mw), F32)],
        scratch_shapes=[pltpu.VMEM((tm + BLOCK, 2 * kw), BF16), pltpu.VMEM((tm + BLOCK, 2 * kw), F32),
                        pltpu.VMEM((tm, d), BF16), pltpu.VMEM((n_mem, mw), F32), pltpu.VMEM((n_mem, mw), F32)],
        compiler_params=_cp(("arbitrary",), VMEM_BIG),
    )(dxm, qp, kv, kv, bias, sinkt, memkv, wout)


def _adam_math(w, g, m, v):
    m2 = ADAM_B1 * m + (1.0 - ADAM_B1) * g
    v2 = ADAM_B2 * v + (1.0 - ADAM_B2) * (g * g)
    m_hat = m2 / (1.0 - ADAM_B1 ** ADAM_STEP)
    v_hat = v2 / (1.0 - ADAM_B2 ** ADAM_STEP)
    delta = -ADAM_LR * (m_hat / (jnp.sqrt(v_hat) + ADAM_EPS) + ADAM_WD * w)
    return delta, m2, v2


def _adamw_sharded(w, land, m, v, name):
    nl, r, c = w.shape
    tr = max(cand for cand in range(16, r + 1, 16) if r % cand == 0 and cand * c <= 512 * 1024)

    def body(w_ref, a_ref, m_ref, v_ref, g_ref, d_ref, mo_ref, vo_ref):
        g = a_ref[0, 0].astype(F32) + a_ref[1, 0].astype(F32)
        for k in range(1, N_CHIPS):
            g = g + (a_ref[2 * k, 0].astype(F32) + a_ref[2 * k + 1, 0].astype(F32))
        delta, m2, v2 = _adam_math(w_ref[0], g, m_ref[0], v_ref[0])
        g_ref[0] = g
        d_ref[0] = delta
        mo_ref[0] = m2
        vo_ref[0] = v2

    rs = pl.BlockSpec((1, tr, c), lambda l, i: (l, i, 0))
    ps = pl.BlockSpec((2 * N_CHIPS, 1, tr, c), lambda l, i: (0, l, i, 0))
    sd = jax.ShapeDtypeStruct((nl, r, c), F32)
    return pl.pallas_call(
        body, name=name, grid=(nl, r // tr),
        in_specs=[rs, ps, rs, rs], out_specs=[rs, rs, rs, rs], out_shape=[sd, sd, sd, sd],
        compiler_params=_cp(("parallel", "parallel"), VMEM_BIG),
    )(w, land, m, v)


def _adamw_packed(w, g, m, v, name):
    def body(w_ref, g_ref, m_ref, v_ref, d_ref, mo_ref, vo_ref):
        delta, m2, v2 = _adam_math(w_ref[...], g_ref[...], m_ref[...], v_ref[...])
        d_ref[...] = delta
        mo_ref[...] = m2
        vo_ref[...] = v2

    vm = pl.BlockSpec(memory_space=pltpu.VMEM)
    sd = jax.ShapeDtypeStruct(w.shape, F32)
    return pl.pallas_call(body, name=name, in_specs=[vm] * 4, out_specs=[vm] * 3, out_shape=[sd] * 3)(w, g, m, v)


def _place():
    return lax.axis_index("x"), lax.axis_index("y"), lax.axis_index("c")


def _hbm(a):
    return pltpu.with_memory_space_constraint(a, pltpu.HBM)


def _peers(x, y, c, both_cores):
    chips = [(1 - x, y), (x, 1 - y), (1 - x, 1 - y)]
    if not both_cores:
        return [(px, py, c) for px, py in chips]
    return [(px, py, pc) for px, py in chips for pc in (c, 1 - c)] + [(x, y, 1 - c)]


def _chip_copy(src, land, gather, layer, chip_src, slot, send_sem, recv_sem, peer):
    s = src if gather else src.at[chip_src]
    d = land.at[slot] if layer is None else land.at[slot, layer]
    return pltpu.make_async_remote_copy(src_ref=s, dst_ref=d, send_sem=send_sem, recv_sem=recv_sem,
                                        device_id=peer, device_id_type=MESH)


def _exchange_start(srcs, lands, gather, layers, both_cores, after, name):
    n = len(srcs)
    npeer = 7 if both_cores else 3
    hbm = pl.BlockSpec(memory_space=pltpu.HBM)
    sem = pl.BlockSpec(memory_space=pltpu.SEMAPHORE)

    def body(*refs):
        ins, lds = refs[:n], refs[n:2 * n]
        first_out = 2 * n + len(after)
        send_sems, recv_sems, token = refs[first_out], refs[first_out + 1], refs[-1]
        x, y, c = _place()
        slot = 2 * x + y if gather else 2 * (2 * x + y) + c
        for t in range(n):
            for r, peer in enumerate(_peers(x, y, c, both_cores)):
                _chip_copy(ins[t], lds[t], gather, layers[t], 2 * peer[0] + peer[1], slot,
                           send_sems.at[npeer * t + r], recv_sems.at[npeer * t + r], peer).start()
        token[...] = jnp.zeros_like(token)

    both = list(srcs) + list(lands)
    outs = pl.pallas_call(
        body, name=name, in_specs=[hbm] * (2 * n) + [pl.BlockSpec(memory_space=pl.ANY)] * len(after),
        out_specs=(sem, sem, *([hbm] * (2 * n)), pl.BlockSpec(memory_space=pltpu.VMEM)),
        out_shape=(pltpu.SemaphoreType.DMA((npeer * n,)), pltpu.SemaphoreType.DMA((npeer * n,)),
                   *[pltpu.HBM(a.shape, a.dtype) for a in both], jax.ShapeDtypeStruct((8, 128), F32)),
        input_output_aliases={t: 2 + t for t in range(2 * n)},
        compiler_params=_cp(has_side_effects=pltpu.SideEffectType.DATAFLOW_SIDE_EFFECTING),
    )(*[_hbm(a) for a in both], *after)
    return dict(send=outs[0], recv=outs[1], srcs=list(outs[2:2 + n]), lands=list(outs[2 + n:2 + 2 * n]),
                token=outs[-1], gather=gather, layers=list(layers), both_cores=both_cores)


def _exchange_wait(groups, lands, land_ids, after, name):
    flat = [s for g in groups for s in g["srcs"]]
    ns, nl, ng, na = len(flat), len(lands), len(groups), len(after)
    hbm = pl.BlockSpec(memory_space=pltpu.HBM)
    sem = pl.BlockSpec(memory_space=pltpu.SEMAPHORE)

    def body(*refs):
        srcs, lds = refs[:ns], refs[ns:ns + nl]
        sems = refs[ns + nl:ns + nl + 2 * ng]
        x, y, c = _place()
        k = 0
        for gi, g in enumerate(groups):
            peers = _peers(x, y, c, g["both_cores"])
            for t in range(len(g["srcs"])):
                for r, peer in enumerate(peers):
                    cp = _chip_copy(srcs[k], lds[land_ids[gi][t]], g["gather"], g["layers"][t], 0, 0,
                                    sems[2 * gi].at[len(peers) * t + r], sems[2 * gi + 1].at[len(peers) * t + r], peer)
                    cp.wait_send()
                    cp.wait_recv()
                k += 1

    both = flat + list(lands)
    sem_args = [a for g in groups for a in (g["send"], g["recv"])]
    outs = pl.pallas_call(
        body, name=name,
        in_specs=[hbm] * (ns + nl) + [sem] * (2 * ng) + [pl.BlockSpec(memory_space=pl.ANY)] * na,
        out_specs=[hbm] * (ns + nl),
        out_shape=[pltpu.HBM(a.shape, a.dtype) for a in both],
        input_output_aliases={t: t for t in range(ns + nl)},
        compiler_params=_cp(has_side_effects=pltpu.SideEffectType.DATAFLOW_SIDE_EFFECTING),
    )(*both, *sem_args, *after)
    return list(outs[ns:])


def _core_fill(lands, layers, name):
    n = len(lands)
    hbm = pl.BlockSpec(memory_space=pltpu.HBM)

    def body(*refs):
        ins = refs[:n]
        send_sems, recv_sems = refs[2 * n:]
        x, y, c = _place()
        copies = []
        for t in range(n):
            for k in range(N_CHIPS):
                mine = ins[t].at[2 * k + c] if layers[t] is None else ins[t].at[2 * k + c, layers[t]]
                cp = pltpu.make_async_remote_copy(
                    src_ref=mine, dst_ref=mine, send_sem=send_sems.at[N_CHIPS * t + k],
                    recv_sem=recv_sems.at[N_CHIPS * t + k], device_id=(x, y, 1 - c), device_id_type=MESH)
                cp.start()
                copies.append(cp)
        for cp in copies:
            cp.wait()

    return pl.pallas_call(
        body, name=name, in_specs=[hbm] * n, out_specs=[hbm] * n,
        out_shape=[jax.ShapeDtypeStruct(a.shape, a.dtype) for a in lands],
        input_output_aliases={t: t for t in range(n)},
        scratch_shapes=[pltpu.SemaphoreType.DMA((N_CHIPS * n,)), pltpu.SemaphoreType.DMA((N_CHIPS * n,))],
        compiler_params=_cp(has_side_effects=True),
    )(*lands)


def _all_reduce_packed(pack, name):
    r, c = pack.shape
    vm = pl.BlockSpec(memory_space=pltpu.VMEM)

    def body(p_ref, sum_ref, slots, send_sems, recv_sems):
        x, y, cc = _place()
        me = 4 * x + 2 * y + cc
        slots[me] = p_ref[...]
        copies = []
        for rel in range(1, 8):
            px = 1 - x if rel & 4 else x
            py = 1 - y if rel & 2 else y
            pc = 1 - cc if rel & 1 else cc
            cp = pltpu.make_async_remote_copy(
                src_ref=p_ref, dst_ref=slots.at[me], send_sem=send_sems.at[rel - 1], recv_sem=recv_sems.at[rel - 1],
                device_id=(px, py, pc), device_id_type=MESH)
            cp.start()
            copies.append(cp)
        for cp in copies:
            cp.wait()
        total = slots[0]
        for k in range(1, 8):
            total = total + slots[k]
        sum_ref[...] = total

    return pl.pallas_call(
        body, name=name, in_specs=[vm], out_specs=vm, out_shape=jax.ShapeDtypeStruct((r, c), F32),
        scratch_shapes=[pltpu.VMEM((8, r, c), F32), pltpu.SemaphoreType.DMA((7,)), pltpu.SemaphoreType.DMA((7,))],
        compiler_params=_cp(has_side_effects=True),
    )(pack)


def _pack(items):
    rows = []
    for a in items:
        flat = a.astype(F32).reshape(-1)
        pad = (-flat.shape[0]) % PACK_W
        rows.append(jnp.pad(flat, (0, pad)).reshape(-1, PACK_W))
    out = jnp.concatenate(rows, axis=0)
    pad_r = (-out.shape[0]) % 8
    return jnp.pad(out, ((0, pad_r), (0, 0)))


def _unpack(pack, shapes):
    outs, row = [], 0
    for s in shapes:
        n = int(np.prod(s))
        nr = -(-n // PACK_W)
        outs.append(pack[row:row + nr].reshape(-1)[:n].reshape(s))
        row += nr
    return outs


def _heads_to_member_major(w, axis):
    shp = w.shape
    pre, post = shp[:axis], shp[axis + 1:]
    w4 = w.reshape(pre + (N_KV_HEADS, GROUP, HEAD_DIM) + post)
    w4 = jnp.swapaxes(w4, len(pre), len(pre) + 1)
    return w4.reshape(shp)


def _heads_to_kv_major(w, axis):
    shp = w.shape
    pre, post = shp[:axis], shp[axis + 1:]
    w4 = w.reshape(pre + (GROUP, N_KV_HEADS, HEAD_DIM) + post)
    w4 = jnp.swapaxes(w4, len(pre), len(pre) + 1)
    return w4.reshape(shp)


def kernel(x, mem, norm_mix, norm_ffn, a_w_in, a_conv_w, a_w_out, kv_norm, w_kv, b_w_q, b_sinks, b_w_out, rel_bias, mem_norm, w_mem_kv, w_gate, w_up, w_down, final_norm, loss_target, m_norm_mix, m_norm_ffn, m_a_w_in, m_a_conv_w, m_a_w_out, m_kv_norm, m_w_kv, m_b_w_q, m_b_sinks, m_b_w_out, m_rel_bias, m_mem_norm, m_w_mem_kv, m_w_gate, m_w_up, m_w_down, m_final_norm, v_norm_mix, v_norm_ffn, v_a_w_in, v_a_conv_w, v_a_w_out, v_kv_norm, v_w_kv, v_b_w_q, v_b_sinks, v_b_w_out, v_rel_bias, v_mem_norm, v_w_mem_kv, v_w_gate, v_w_up, v_w_down, v_final_norm):
    t, d = x.shape[1], x.shape[2]
    tm = 512 if t % 512 == 0 and t >= 2048 else 256
    tl = 2 * tm if t % (2 * tm) == 0 else tm
    x0 = x.reshape(t, d)
    target = loss_target.reshape(t, d)
    mem2 = mem.reshape(mem.shape[1], d)
    n_mem = mem2.shape[0]
    ax, ay, ac = _place()
    chip = 2 * ax + ay
    cwid = a_conv_w.shape[2] * N_CHIPS
    qw = N_Q_HEADS * HEAD_DIM
    nq = N_CHIPS

    def own_slot(piece):
        return lax.dynamic_update_slice(lax.empty((nq,) + piece.shape, piece.dtype), piece[None],
                                        (chip,) + (0,) * piece.ndim)

    def mixer_shards(i):
        if i < N_A:
            shards = [a_w_in[i], a_w_out[i]] + ([w_mem_kv] if i == 0 else [])
        else:
            j = i - N_A
            shards = [b_w_q[j], b_w_out[j]] + ([w_kv] if j == 0 else [])
        return [a.astype(BF16) for a in shards]

    def ffn_shards(i):
        return [w_gate[i].T.astype(BF16), w_up[i].T.astype(BF16), w_down[i].astype(BF16)]

    conv_pad = jnp.pad(a_conv_w, ((0, 0), (0, 8 - a_conv_w.shape[1]), (0, (-a_conv_w.shape[2]) % 128)))
    first = mixer_shards(0)
    group_shards = {"0a": first[0:1], "0b": first[1:] + [conv_pad], "0f": ffn_shards(0)}
    for i in range(1, DEPTH):
        group_shards[str(i)] = ffn_shards(i) + mixer_shards(i)
    gathers, prev_tok = {}, []

    def start_group(key, after):
        shards = group_shards[key]
        gathers[key] = _exchange_start(shards, [own_slot(a) for a in shards], True, [None] * len(shards), False,
                                       after, "gather_start_" + key)
        return [gathers[key]["token"]]

    for key in ("0a", "0b", "0f"):
        prev_tok = start_group(key, prev_tok)

    def rows_full(g):
        return g.reshape((-1,) + g.shape[2:])

    def cols_full(g):
        return jnp.transpose(g, (1, 0, 2)).reshape(g.shape[1], -1)

    def landed_weights(key, after):
        g = gathers[key]
        return _exchange_wait([g], g["lands"], [list(range(len(g["lands"])))], after, "gather_wait_" + key)

    def mixer_weights(i, got):
        w_first, w_out = (cols_full(got[0]) if i < N_A else rows_full(got[0])), rows_full(got[1])
        if i >= N_A:
            w_first = jnp.concatenate([_heads_to_member_major(w_first[:, :qw], 1), w_first[:, qw:]], axis=1)
            w_out = jnp.concatenate([_heads_to_member_major(w_out[:qw, :], 0), w_out[qw:, :]], axis=0)
        return dict(w_first=w_first, w_out=w_out, extra=got[2] if len(got) > 2 else None)

    def ffn_weights(got):
        return dict(wg=rows_full(got[0]), wu=rows_full(got[1]), wd=rows_full(got[2]))

    bias, sinkt = [], []
    for j in range(2):
        bj, sj = _bias_tables(rel_bias, b_sinks[j], "bias_tables")
        bias.append(bj)
        sinkt.append(sj)

    ws = []
    xs, xmids, projs, cats, gates, ups = [x0], [], [], [], [], []
    kv = memkv = wmem = wkv = None
    for i in range(DEPTH):
        xin = xs[-1]
        if i == 0:
            w = dict(w_first=cols_full(landed_weights("0a", prev_tok)[0]))
        else:
            got = landed_weights(str(i), [xin])
            w = dict(mixer_weights(i, got[3:]), **ffn_weights(got[0:3]))
        ws.append(w)
        gm = norm_mix[i].reshape(1, d)
        if i < N_A:
            proj = _norm_mm(xin, gm, w["w_first"], tl, "proj_a")
            if i == 0:
                for key in ("1", "2", "3"):
                    prev_tok = start_group(key, prev_tok + [proj])
                got = landed_weights("0b", prev_tok)
                w["w_out"] = rows_full(got[0])
                full_mem = jnp.swapaxes(got[1], 0, 1).reshape(DEPTH, d, -1)
                wmem = jnp.transpose(full_mem, (1, 0, 2)).reshape(d, -1)
                memkv = _norm_mm(mem2, mem_norm.reshape(1, d), wmem, n_mem, "mem_kv")
                taps = got[2][:, :, 0:3, 0:a_conv_w.shape[2]]
                conv_full = jnp.transpose(taps, (1, 2, 0, 3)).reshape(N_A, 3, cwid)
            xmid, cat = _mix_a_fwd(xin, proj, conv_full[i], memkv, i, w["w_out"], tm, "mix_a_fwd")
        else:
            j = i - N_A
            if j == 0:
                wkv = rows_full(w["extra"])
                proj, kv = _norm_mm_pair(xin, gm, w["w_first"], kv_norm.reshape(1, d), wkv, tl, "proj_b_kv")
            else:
                proj = _norm_mm(xin, gm, w["w_first"], tl, "proj_b")
            xmid, cat = _mix_b_fwd(xin, proj, kv, bias[j], sinkt[j], memkv, i, w["w_out"], tm, "mix_b_fwd")
        if i == 0:
            w.update(ffn_weights(landed_weights("0f", [xmid])))
        if i < DEPTH - 1:
            xout, gate, up = _ffn_fwd(xmid, norm_ffn[i].reshape(1, d), w["wg"], w["wu"], w["wd"], tm, "ffn_fwd")
        else:
            xout = None
            gate, up, loss_part, dx, dg_final = _ffn_fwd_loss(xmid, norm_ffn[i].reshape(1, d), w["wg"], w["wu"], w["wd"],
                                                              final_norm.reshape(1, d), target, tm, "ffn_fwd_loss")
        projs.append(proj)
        cats.append(cat)
        xmids.append(xmid)
        gates.append(gate)
        ups.append(up)
        xs.append(xout)

    def rows_pieces(g):
        return g.astype(BF16).reshape((nq, g.shape[0] // nq) + g.shape[1:])

    def cols_pieces(g):
        return jnp.transpose(g.astype(BF16).reshape(g.shape[0], nq, g.shape[1] // nq), (1, 0, 2))

    swapped = ("w_gate", "w_up")
    stacked = dict(a_w_in=a_w_in, a_w_out=a_w_out, w_kv=w_kv[None], b_w_q=b_w_q, b_w_out=b_w_out,
                   w_mem_kv=w_mem_kv, w_gate=jnp.swapaxes(w_gate, 1, 2), w_up=jnp.swapaxes(w_up, 1, 2), w_down=w_down)
    names = list(stacked)
    land = {k: lax.empty((2 * nq,) + stacked[k].shape, BF16) for k in names}
    own = {k: [None] * stacked[k].shape[0] for k in names}
    scatters, scatter_ids = [], []

    def scatter_start(key, items, both_cores):
        keys = [k for k, _, _ in items]
        st = _exchange_start([p for _, _, p in items], [land[k] for k in keys], False, [l for _, l, _ in items],
                             both_cores, [], "scatter_start_" + key)
        for (k, l, p), ld in zip(items, st["lands"]):
            land[k] = ld
            mine_piece = lax.dynamic_index_in_dim(p, chip, 0, keepdims=False)
            if l is None:
                own[k] = [mine_piece[q] for q in range(mine_piece.shape[0])]
            else:
                own[k][l] = mine_piece
        scatters.append(st)
        scatter_ids.append([names.index(k) for k in keys])
        return st["token"][0:1, 0:1]

    g_norm_mix, g_norm_ffn = [None] * DEPTH, [None] * DEPTH
    g_conv, g_sinks = [None] * 2, [None] * 2
    dmemkv = [None] * DEPTH
    dbias, dkv_main, dkv_halo = [None] * 2, [None] * 2, [None] * 2
    g_kv_norm = None
    tok = jnp.zeros((1, 1), F32)
    for i in reversed(range(DEPTH)):
        w = ws[i]
        dxm, dgate, dup, act, h2, dgf = _ffn_bwd(dx, xmids[i], norm_ffn[i].reshape(1, d) + tok, gates[i], ups[i],
                                            w["wg"], w["wu"], w["wd"], tm // 2, "ffn_bwd")
        g_norm_ffn[i] = dgf
        g_wd = _wgrad(act, dx, 2 * tm, "wgrad_down")
        g_wg = _wgrad(dgate, h2, 2 * tm, "wgrad_gate")
        g_wu = _wgrad(dup, h2, 2 * tm, "wgrad_up")
        items = [("w_gate", i, rows_pieces(g_wg)), ("w_up", i, rows_pieces(g_wu)), ("w_down", i, rows_pieces(g_wd))]
        if i == 0:
            tok = scatter_start("0f", items, True)
            items = []
        gm = norm_mix[i].reshape(1, d)
        if i < N_A:
            dproj, dcw, dmemkv[i] = _mix_a_bwd(dxm, projs[i], conv_full[i] + (tok if i == 0 else 0.0), memkv, i,
                                                    w["w_out"], tm, "mix_a_bwd")
            g_conv[i] = dcw[0:3]
            g_out = _wgrad(cats[i], dxm, 2 * tm, "wgrad_out")
            if i == 0:
                dmemkv_all = jnp.concatenate([a.astype(BF16) for a in dmemkv], axis=1)
                _, g_mem_norm, hmem = _mm_nt_normbwd(dmemkv_all, wmem, mem2, mem_norm.reshape(1, d),
                                                     jnp.zeros((n_mem, d), F32), n_mem, "mem_kv_bwd")
                g_wmem = _wgrad(hmem, dmemkv_all, n_mem, "wgrad_mem")
                g_wmem = jnp.transpose(g_wmem.reshape(nq, d // nq, DEPTH, -1), (0, 2, 1, 3))
                gm = gm + scatter_start("0o", [("a_w_out", 0, rows_pieces(g_out)), ("w_mem_kv", None, g_wmem)], False)
            dx, g_norm_mix[i], h = _mm_nt_normbwd(dproj, w["w_first"], xs[i], gm, dxm, tl, "proj_a_bwd")
            g_in = _wgrad(h, dproj, 2 * tm, "wgrad_in_a")
            items.append(("a_w_in", i, cols_pieces(g_in)))
            if i > 0:
                items.append(("a_w_out", i, rows_pieces(g_out)))
        else:
            j = i - N_A
            dqp, dkv_main[j], dkv_halo[j], dbias[j], dsk, dmemkv[i] = _mix_b_bwd(
                dxm, projs[i], kv, bias[j], sinkt[j], memkv, i, w["w_out"], tm, "mix_b_bwd")
            g_sinks[j] = dsk[0, 0:N_Q_HEADS].reshape(GROUP, N_KV_HEADS).T.reshape(N_Q_HEADS)
            g_out = _wgrad(cats[i], dxm, 2 * tm, "wgrad_out")
            if j == 0:
                dx, g_norm_mix[i], g_kv_norm, h, hkv, dkv = _mm_nt_normbwd_pair(
                    dqp, w["w_first"], gm, (dkv_main[0], dkv_halo[0], dkv_main[1], dkv_halo[1]), wkv,
                    kv_norm.reshape(1, d), xs[i], dxm, tm, "proj_b_kv_bwd")
            else:
                dx, g_norm_mix[i], h = _mm_nt_normbwd(dqp, w["w_first"], xs[i], gm, dxm, tl, "proj_b_bwd")
            g_q = _wgrad(h, dqp, 2 * tm, "wgrad_in_b")
            g_q = jnp.concatenate([_heads_to_kv_major(g_q[:, :qw], 1), g_q[:, qw:]], axis=1)
            g_out = jnp.concatenate([_heads_to_kv_major(g_out[:qw, :], 0), g_out[qw:, :]], axis=0)
            items += [("b_w_q", j, rows_pieces(g_q)), ("b_w_out", j, rows_pieces(g_out))]
            if j == 0:
                items.append(("w_kv", 0, rows_pieces(_wgrad(hkv, dkv, 2 * tm, "wgrad_kv"))))
        tok = scatter_start(str(i) if i else "0i", items, i > 0)
    grad_x = dx.reshape(x.shape)
    g_rel = _bias_bwd(dbias[0], dbias[1], "bias_bwd")[:, 0:N_Q_HEADS]

    small_shapes = [(DEPTH, d), (DEPTH, d), (d,), (d,), (d,), (2, N_Q_HEADS), (REL_BUCKETS, N_Q_HEADS),
                    (N_A, 3, cwid), ()]
    small = _pack([jnp.concatenate(g_norm_mix, axis=0), jnp.concatenate(g_norm_ffn, axis=0), g_kv_norm, g_mem_norm,
                   dg_final, jnp.stack(g_sinks), g_rel, jnp.stack(g_conv), loss_part[0, 0]])
    small_sum = _all_reduce_packed(small, "reduce_small")
    (gs_norm_mix, gs_norm_ffn, gs_kv_norm, gs_mem_norm, gs_final, gs_sinks, gs_rel, gs_conv_full, loss) = _unpack(
        small_sum, small_shapes)
    cq = cwid // N_CHIPS
    gs_conv = lax.dynamic_slice_in_dim(gs_conv_full, chip * cq, cq, axis=2)

    late = ("a_w_in", "a_w_out", "w_mem_kv")
    landed = dict(zip(names, _exchange_wait(scatters[:-2], [land[k] for k in names], scatter_ids[:-2], [small_sum],
                                            "scatter_wait_a")))

    def with_own(k, ld):
        return lax.dynamic_update_slice(ld, jnp.stack(own[k])[None], (2 * chip + ac,) + (0,) * (ld.ndim - 1))

    weights = dict(norm_mix=norm_mix, norm_ffn=norm_ffn, a_w_in=a_w_in, a_conv_w=a_conv_w, a_w_out=a_w_out,
                   kv_norm=kv_norm, w_kv=w_kv, b_w_q=b_w_q, b_sinks=b_sinks, b_w_out=b_w_out, rel_bias=rel_bias,
                   mem_norm=mem_norm, w_mem_kv=w_mem_kv, w_gate=w_gate, w_up=w_up, w_down=w_down,
                   final_norm=final_norm)
    moms = dict(norm_mix=m_norm_mix, norm_ffn=m_norm_ffn, a_w_in=m_a_w_in, a_conv_w=m_a_conv_w, a_w_out=m_a_w_out,
                kv_norm=m_kv_norm, w_kv=m_w_kv, b_w_q=m_b_w_q, b_sinks=m_b_sinks, b_w_out=m_b_w_out,
                rel_bias=m_rel_bias, mem_norm=m_mem_norm, w_mem_kv=m_w_mem_kv, w_gate=m_w_gate, w_up=m_w_up,
                w_down=m_w_down, final_norm=m_final_norm)
    vars_ = dict(norm_mix=v_norm_mix, norm_ffn=v_norm_ffn, a_w_in=v_a_w_in, a_conv_w=v_a_conv_w, a_w_out=v_a_w_out,
                 kv_norm=v_kv_norm, w_kv=v_w_kv, b_w_q=v_b_w_q, b_sinks=v_b_sinks, b_w_out=v_b_w_out,
                 rel_bias=v_rel_bias, mem_norm=v_mem_norm, w_mem_kv=v_w_mem_kv, w_gate=v_w_gate, w_up=v_w_up,
                 w_down=v_w_down, final_norm=v_final_norm)
    order = list(weights)
    grads, deltas, new_m, new_v = {}, {}, {}, {}
    def adamw(k, ld):
        shp, stk = weights[k].shape, ld.shape[1:]
        view = (lambda a: jnp.swapaxes(a, 1, 2)) if k in swapped else (lambda a: a.reshape(stk))
        back = (lambda a: jnp.swapaxes(a, 1, 2)) if k in swapped else (lambda a: a.reshape(shp))
        outs = _adamw_sharded(view(weights[k]), ld, view(moms[k]), view(vars_[k]), "adamw_" + k)
        grads[k], deltas[k], new_m[k], new_v[k] = [back(o) for o in outs]

    for k in names:
        if k not in late:
            adamw(k, with_own(k, landed[k]))
    late_ids = [[late.index(names[t]) for t in ids] for ids in scatter_ids[-2:]]
    late_landed = _exchange_wait(scatters[-2:], [landed[k] for k in late], late_ids, [deltas["w_down"]], "scatter_wait_b")
    filled = _core_fill([with_own(k, ld) for k, ld in zip(late, late_landed)], [0, 0, None], "fill_cores")
    for k, ld in zip(late, filled):
        adamw(k, ld)
    small_names = ["norm_mix", "norm_ffn", "kv_norm", "mem_norm", "final_norm", "b_sinks", "rel_bias", "a_conv_w"]
    small_g = [gs_norm_mix, gs_norm_ffn, gs_kv_norm, gs_mem_norm, gs_final, gs_sinks, gs_rel, gs_conv]
    shapes = [weights[k].shape for k in small_names]
    dl_p, m_p, v_p = _adamw_packed(_pack([weights[k] for k in small_names]), _pack(small_g),
                                   _pack([moms[k] for k in small_names]), _pack([vars_[k] for k in small_names]),
                                   "adamw_small")
    for k, g, dl, m2, v2 in zip(small_names, small_g, _unpack(dl_p, shapes), _unpack(m_p, shapes), _unpack(v_p, shapes)):
        grads[k], deltas[k], new_m[k], new_v[k] = g.reshape(weights[k].shape), dl, m2, v2

    return (loss, grad_x, *[grads[k] for k in order], *[deltas[k] for k in order],
            *[new_m[k] for k in order], *[new_v[k] for k in order])
```

```python
import functools
import math

import numpy as np
import jax
import jax.numpy as jnp
from jax import lax
from jax.experimental import pallas as pl
from jax.experimental.pallas import tpu as pltpu

F32 = jnp.float32
BF16 = jnp.bfloat16
MESH = pl.DeviceIdType.MESH

EPS = 1e-5
HEAD_DIM = 64
N_MEM_HEADS = 4
N_KV_HEADS = 4
GROUP = 3
N_Q_HEADS = N_KV_HEADS * GROUP
BLOCK = 128
REL_BUCKETS = 32
REL_MAX_DIST = 128
SCALE = HEAD_DIM ** -0.5
NEG = -1e30
N_CHIPS = 4
N_A = 2
DEPTH = 4

ADAM_LR = 0.001
ADAM_B1 = 0.9
ADAM_B2 = 0.999
ADAM_EPS = 1e-08
ADAM_WD = 0.01
ADAM_STEP = 10

VMEM_BIG = 56 * 1024 * 1024
PACK_W = 1024

NT = (((1,), (1,)), ((), ()))
TN = (((0,), (0,)), ((), ()))


def _cp(sem=None, vmem=None, **kw):
    return pltpu.CompilerParams(dimension_semantics=sem, vmem_limit_bytes=vmem, **kw)


def _const_spec(shape):
    nd = len(shape)
    return pl.BlockSpec(shape, lambda i, _n=nd: (0,) * _n, pipeline_mode=pl.Buffered(1))


def _row_spec(tm, n):
    return pl.BlockSpec((tm, n), lambda i: (i, 0))


def _rms_parts(xv):
    r = lax.rsqrt(jnp.mean(xv * xv, axis=-1, keepdims=True) + EPS)
    return xv * r, r


def _sigmoid(z):
    return 1.0 / (1.0 + jnp.exp(-z))


def _ff_chunks(f):
    if f % 512 == 0 or f % 256 != 0:
        return [(0, f)] if f <= 1536 else [(0, f // 2), (f // 2, f - f // 2)]
    n = f // 256
    a = (n + 1) // 2 * 256
    return [(0, a), (a, f - a)]


def _norm_mm(x, g, w, tm, name):
    t, d = x.shape
    n = w.shape[1]

    def body(x_ref, g_ref, w_ref, o_ref):
        xhat, _ = _rms_parts(x_ref[...])
        h = (xhat * g_ref[...]).astype(BF16)
        o_ref[...] = jnp.dot(h, w_ref[...], preferred_element_type=F32).astype(BF16)

    return pl.pallas_call(
        body, name=name, grid=(t // tm,),
        in_specs=[_row_spec(tm, d), _const_spec((1, d)), _const_spec((d, n))],
        out_specs=_row_spec(tm, n),
        out_shape=jax.ShapeDtypeStruct((t, n), BF16),
        compiler_params=_cp(("parallel",), VMEM_BIG),
    )(x, g, w)


def _mm_nt_normbwd(dproj, w, x_in, g, dres, tm, name):
    t, d = x_in.shape
    n = w.shape[1]

    def body(dp_ref, w_ref, x_ref, g_ref, dr_ref, dx_ref, dg_ref, h_ref):
        i = pl.program_id(0)
        xhat, r = _rms_parts(x_ref[...])
        gv = g_ref[...]
        h_ref[...] = (xhat * gv).astype(BF16)
        dh = lax.dot_general(dp_ref[...], w_ref[...], NT, preferred_element_type=F32)
        dxhat = dh * gv
        dx = r * (dxhat - xhat * jnp.mean(dxhat * xhat, axis=-1, keepdims=True))
        dx_ref[...] = dr_ref[...] + dx

        @pl.when(i == 0)
        def _():
            dg_ref[...] = jnp.zeros_like(dg_ref)

        dg_ref[...] += jnp.sum(dh * xhat, axis=0, keepdims=True)

    return pl.pallas_call(
        body, name=name, grid=(t // tm,),
        in_specs=[_row_spec(tm, n), _const_spec((d, n)), _row_spec(tm, d), _const_spec((1, d)), _row_spec(tm, d)],
        out_specs=[_row_spec(tm, d), pl.BlockSpec((1, d), lambda i: (0, 0)), _row_spec(tm, d)],
        out_shape=[jax.ShapeDtypeStruct((t, d), F32), jax.ShapeDtypeStruct((1, d), F32),
                   jax.ShapeDtypeStruct((t, d), BF16)],
        compiler_params=_cp(("arbitrary",), VMEM_BIG),
    )(dproj, w, x_in, g, dres)


def _norm_mm_pair(x, g_a, w_a, g_b, w_b, tm, name):
    t, d = x.shape
    na, nb = w_a.shape[1], w_b.shape[1]

    def body(x_ref, ga_ref, wa_ref, gb_ref, wb_ref, oa_ref, ob_ref):
        xhat, _ = _rms_parts(x_ref[...])
        ha = (xhat * ga_ref[...]).astype(BF16)
        hb = (xhat * gb_ref[...]).astype(BF16)
        oa_ref[...] = jnp.dot(ha, wa_ref[...], preferred_element_type=F32).astype(BF16)
        ob_ref[...] = jnp.dot(hb, wb_ref[...], preferred_element_type=F32).astype(BF16)

    return pl.pallas_call(
        body, name=name, grid=(t // tm,),
        in_specs=[_row_spec(tm, d), _const_spec((1, d)), _const_spec((d, na)), _const_spec((1, d)), _const_spec((d, nb))],
        out_specs=[_row_spec(tm, na), _row_spec(tm, nb)],
        out_shape=[jax.ShapeDtypeStruct((t, na), BF16), jax.ShapeDtypeStruct((t, nb), BF16)],
        compiler_params=_cp(("parallel",), VMEM_BIG),
    )(x, g_a, w_a, g_b, w_b)


def _mm_nt_normbwd_pair(dp_a, w_a, g_a, kv_parts, w_b, g_b, x_in, dres, tm, name):
    t, d = x_in.shape
    na, nb = w_a.shape[1], w_b.shape[1]
    nt = t // tm
    main_1, halo_1, main_2, halo_2 = kv_parts

    def body(dpa_ref, wa_ref, ga_ref, m1_ref, h1_ref, m2_ref, h2_ref, wb_ref, gb_ref, x_ref, dr_ref,
             dx_ref, dga_ref, dgb_ref, ha_ref, hb_ref, dkv_ref):
        i = pl.program_id(0)
        s = m1_ref[...] + m2_ref[...]
        tail = jnp.where(i == nt - 1, 0.0, h1_ref[0] + h2_ref[0])
        dkv = jnp.concatenate([s[0:tm - BLOCK], s[tm - BLOCK:] + tail], axis=0).astype(BF16)
        dkv_ref[...] = dkv
        xhat, r = _rms_parts(x_ref[...])
        ga, gb = ga_ref[...], gb_ref[...]
        ha_ref[...] = (xhat * ga).astype(BF16)
        hb_ref[...] = (xhat * gb).astype(BF16)
        dha = lax.dot_general(dpa_ref[...], wa_ref[...], NT, preferred_element_type=F32)
        dhb = lax.dot_general(dkv, wb_ref[...], NT, preferred_element_type=F32)
        dxhat = dha * ga + dhb * gb
        dx_ref[...] = dr_ref[...] + r * (dxhat - xhat * jnp.mean(dxhat * xhat, axis=-1, keepdims=True))

        @pl.when(i == 0)
        def _():
            dga_ref[...] = jnp.zeros_like(dga_ref)
            dgb_ref[...] = jnp.zeros_like(dgb_ref)

        dga_ref[...] += jnp.sum(dha * xhat, axis=0, keepdims=True)
        dgb_ref[...] += jnp.sum(dhb * xhat, axis=0, keepdims=True)

    halo_spec = pl.BlockSpec((1, BLOCK, nb), lambda i: (jnp.minimum(i + 1, nt - 1), 0, 0))
    row1 = pl.BlockSpec((1, d), lambda i: (0, 0))
    return pl.pallas_call(
        body, name=name, grid=(nt,),
        in_specs=[_row_spec(tm, na), _const_spec((d, na)), _const_spec((1, d)), _row_spec(tm, nb), halo_spec,
                  _row_spec(tm, nb), halo_spec, _const_spec((d, nb)), _const_spec((1, d)), _row_spec(tm, d),
                  _row_spec(tm, d)],
        out_specs=[_row_spec(tm, d), row1, row1, _row_spec(tm, d), _row_spec(tm, d), _row_spec(tm, nb)],
        out_shape=[jax.ShapeDtypeStruct((t, d), F32), jax.ShapeDtypeStruct((1, d), F32), jax.ShapeDtypeStruct((1, d), F32),
                   jax.ShapeDtypeStruct((t, d), BF16), jax.ShapeDtypeStruct((t, d), BF16),
                   jax.ShapeDtypeStruct((t, nb), BF16)],
        compiler_params=_cp(("arbitrary",), VMEM_BIG),
    )(dp_a, w_a, g_a, main_1, halo_1, main_2, halo_2, w_b, g_b, x_in, dres)


def _ffn_fwd(x, g, wg, wu, wd, tm, name):
    t, d = x.shape
    f = wg.shape[0]
    chunks = _ff_chunks(f)

    def body(x_ref, g_ref, wg_ref, wu_ref, wd_ref, xo_ref, gate_ref, up_ref):
        xv = x_ref[...]
        xhat, _ = _rms_parts(xv)
        h = (xhat * g_ref[...]).astype(BF16)
        acc = xv
        for c0, cw in chunks:
            gt = lax.dot_general(h, wg_ref[c0:c0 + cw, :], NT, preferred_element_type=F32)
            ut = lax.dot_general(h, wu_ref[c0:c0 + cw, :], NT, preferred_element_type=F32)
            gate_ref[:, c0:c0 + cw] = gt.astype(BF16)
            up_ref[:, c0:c0 + cw] = ut.astype(BF16)
            a = (gt * _sigmoid(gt) * ut).astype(BF16)
            acc = acc + jnp.dot(a, wd_ref[c0:c0 + cw, :], preferred_element_type=F32)
        xo_ref[...] = acc

    return pl.pallas_call(
        body, name=name, grid=(t // tm,),
        in_specs=[_row_spec(tm, d), _const_spec((1, d)), _const_spec((f, d)), _const_spec((f, d)), _const_spec((f, d))],
        out_specs=[_row_spec(tm, d), _row_spec(tm, f), _row_spec(tm, f)],
        out_shape=[jax.ShapeDtypeStruct((t, d), F32), jax.ShapeDtypeStruct((t, f), BF16),
                   jax.ShapeDtypeStruct((t, f), BF16)],
        compiler_params=_cp(("parallel",), VMEM_BIG),
    )(x, g, wg, wu, wd)


def _ffn_fwd_loss(x, g, wg, wu, wd, g_final, target, tm, name):
    t, d = x.shape
    f = wg.shape[0]
    chunks = _ff_chunks(f)

    def body(x_ref, g_ref, wg_ref, wu_ref, wd_ref, gf_ref, t_ref, gate_ref, up_ref, loss_ref, dx_ref, dgf_ref):
        i = pl.program_id(0)
        xv = x_ref[...]
        xhat, _ = _rms_parts(xv)
        h = (xhat * g_ref[...]).astype(BF16)
        acc = xv
        for c0, cw in chunks:
            gt = lax.dot_general(h, wg_ref[c0:c0 + cw, :], NT, preferred_element_type=F32)
            ut = lax.dot_general(h, wu_ref[c0:c0 + cw, :], NT, preferred_element_type=F32)
            gate_ref[:, c0:c0 + cw] = gt.astype(BF16)
            up_ref[:, c0:c0 + cw] = ut.astype(BF16)
            a = (gt * _sigmoid(gt) * ut).astype(BF16)
            acc = acc + jnp.dot(a, wd_ref[c0:c0 + cw, :], preferred_element_type=F32)
        xhat_o, r_o = _rms_parts(acc)
        gf = gf_ref[...]
        err = xhat_o * gf - t_ref[...]
        dy = err * (1.0 / d)
        dxhat = dy * gf
        dx_ref[...] = r_o * (dxhat - xhat_o * jnp.mean(dxhat * xhat_o, axis=-1, keepdims=True))

        @pl.when(i == 0)
        def _():
            dgf_ref[...] = jnp.zeros_like(dgf_ref)
            loss_ref[...] = jnp.zeros_like(loss_ref)

        dgf_ref[...] += jnp.sum(dy * xhat_o, axis=0, keepdims=True)
        part = jnp.sum(jnp.sum(err * err, axis=-1, keepdims=True), axis=0, keepdims=True) * (0.5 / d)
        loss_ref[...] += jnp.broadcast_to(part, loss_ref.shape)

    return pl.pallas_call(
        body, name=name, grid=(t // tm,),
        in_specs=[_row_spec(tm, d), _const_spec((1, d)), _const_spec((f, d)), _const_spec((f, d)), _const_spec((f, d)),
                  _const_spec((1, d)), _row_spec(tm, d)],
        out_specs=[_row_spec(tm, f), _row_spec(tm, f), pl.BlockSpec((8, 128), lambda i: (0, 0)), _row_spec(tm, d),
                   pl.BlockSpec((1, d), lambda i: (0, 0))],
        out_shape=[jax.ShapeDtypeStruct((t, f), BF16), jax.ShapeDtypeStruct((t, f), BF16),
                   jax.ShapeDtypeStruct((8, 128), F32), jax.ShapeDtypeStruct((t, d), F32),
                   jax.ShapeDtypeStruct((1, d), F32)],
        compiler_params=_cp(("arbitrary",), VMEM_BIG),
    )(x, g, wg, wu, wd, g_final, target)


def _ffn_bwd(dxo, xm, g, gate, up, wg, wu, wd, tm, name):
    t, d = xm.shape
    f = wg.shape[0]
    chunks = _ff_chunks(f)

    def body(dxo_ref, xm_ref, g_ref, gate_ref, up_ref, wg_ref, wu_ref, wd_ref,
             dxm_ref, dgate_ref, dup_ref, act_ref, h2_ref, dg_ref):
        i = pl.program_id(0)
        dxo_v = dxo_ref[...]
        dxo_b = dxo_v.astype(BF16)
        xhat, r = _rms_parts(xm_ref[...])
        gv = g_ref[...]
        h2_ref[...] = (xhat * gv).astype(BF16)
        dh = jnp.zeros((tm, d), F32)
        for c0, cw in chunks:
            dact = lax.dot_general(dxo_b, wd_ref[c0:c0 + cw, :], NT, preferred_element_type=F32)
            gt = gate_ref[:, c0:c0 + cw].astype(F32)
            ut = up_ref[:, c0:c0 + cw].astype(F32)
            sg = _sigmoid(gt)
            sl = gt * sg
            act_ref[:, c0:c0 + cw] = (sl * ut).astype(BF16)
            dgt = (dact * ut * (sg * (1.0 + gt * (1.0 - sg)))).astype(BF16)
            dut = (dact * sl).astype(BF16)
            dgate_ref[:, c0:c0 + cw] = dgt
            dup_ref[:, c0:c0 + cw] = dut
            dh = dh + jnp.dot(dgt, wg_ref[c0:c0 + cw, :], preferred_element_type=F32)
            dh = dh + jnp.dot(dut, wu_ref[c0:c0 + cw, :], preferred_element_type=F32)
        dxhat = dh * gv
        dx = r * (dxhat - xhat * jnp.mean(dxhat * xhat, axis=-1, keepdims=True))
        dxm_ref[...] = dxo_v + dx

        @pl.when(i == 0)
        def _():
            dg_ref[...] = jnp.zeros_like(dg_ref)

        dg_ref[...] += jnp.sum(dh * xhat, axis=0, keepdims=True)

    return pl.pallas_call(
        body, name=name, grid=(t // tm,),
        in_specs=[_row_spec(tm, d), _row_spec(tm, d), _const_spec((1, d)), _row_spec(tm, f), _row_spec(tm, f),
                  _const_spec((f, d)), _const_spec((f, d)), _const_spec((f, d))],
        out_specs=[_row_spec(tm, d), _row_spec(tm, f), _row_spec(tm, f), _row_spec(tm, f), _row_spec(tm, d),
                   pl.BlockSpec((1, d), lambda i: (0, 0))],
        out_shape=[jax.ShapeDtypeStruct((t, d), F32), jax.ShapeDtypeStruct((t, f), BF16),
                   jax.ShapeDtypeStruct((t, f), BF16), jax.ShapeDtypeStruct((t, f), BF16),
                   jax.ShapeDtypeStruct((t, d), BF16), jax.ShapeDtypeStruct((1, d), F32)],
        compiler_params=_cp(("arbitrary",), VMEM_BIG),
    )(dxo, xm, g, gate, up, wg, wu, wd)


def _wgrad(a, b, tt, name):
    t, k = a.shape
    n = b.shape[1]
    nt = t // tt

    def body(a_ref, b_ref, o_ref, acc):
        i = pl.program_id(0)

        @pl.when(i == 0)
        def _():
            acc[...] = jnp.zeros_like(acc)

        acc[...] += lax.dot_general(a_ref[...].astype(BF16), b_ref[...].astype(BF16), TN,
                                    preferred_element_type=F32)

        @pl.when(i == nt - 1)
        def _():
            o_ref[...] = acc[...].astype(BF16)

    return pl.pallas_call(
        body, name=name, grid=(nt,),
        in_specs=[_row_spec(tt, k), _row_spec(tt, n)],
        out_specs=pl.BlockSpec((k, n), lambda i: (0, 0)),
        out_shape=jax.ShapeDtypeStruct((k, n), BF16),
        scratch_shapes=[pltpu.VMEM((k, n), F32)],
        compiler_params=_cp(("arbitrary",), VMEM_BIG),
    )(a, b)


def _col_head(width):
    return lax.broadcasted_iota(jnp.int32, (1, width), 1) // HEAD_DIM


def _keep_head(a, colh, h):
    return jnp.where(colh == h, a, jnp.zeros_like(a))


def _softmax_cols(s, sink=None):
    m = jnp.max(s, axis=0, keepdims=True)
    if sink is not None:
        m = jnp.maximum(m, sink)
    p = jnp.exp(s - m)
    l = jnp.sum(p, axis=0, keepdims=True)
    if sink is None:
        return p * (1.0 / l), None
    es = jnp.exp(sink - m)
    inv = 1.0 / (l + es)
    return p * inv, es * inv


def _add4(v):
    return (v[0] + v[1]) + (v[2] + v[3])


def _mem_attn_fwd(qm, mk, mv):
    colh = _col_head(mk.shape[1])
    mks = mk * SCALE
    heads = range(N_MEM_HEADS)
    ss = [lax.dot_general(_keep_head(mks, colh, h), qm, NT, preferred_element_type=F32) for h in heads]
    ps = [_softmax_cols(s)[0].astype(BF16) for s in ss]
    return _add4([lax.dot_general(ps[h], _keep_head(mv, colh, h), TN, preferred_element_type=F32) for h in heads])


def _mem_attn_bwd(qm, dy_b, mk, mv):
    colh = _col_head(mk.shape[1])
    mks = mk * SCALE
    heads = range(N_MEM_HEADS)
    khs = [_keep_head(mks, colh, h) for h in heads]
    vhs = [_keep_head(mv, colh, h) for h in heads]
    ss = [lax.dot_general(khs[h], qm, NT, preferred_element_type=F32) for h in heads]
    dps = [lax.dot_general(vhs[h], dy_b, NT, preferred_element_type=F32) for h in heads]
    pbs, dsbs = [], []
    for h in heads:
        p, _ = _softmax_cols(ss[h])
        ds = p * (dps[h] - jnp.sum(p * dps[h], axis=0, keepdims=True))
        pbs.append(p.astype(BF16))
        dsbs.append(ds.astype(BF16))
    dq = _add4([lax.dot_general(dsbs[h], khs[h], TN, preferred_element_type=F32) for h in heads])
    dmk = _add4([jnp.where(colh == h, jnp.dot(dsbs[h], qm, preferred_element_type=F32) * SCALE, 0.0) for h in heads])
    dmv = _add4([jnp.where(colh == h, jnp.dot(pbs[h], dy_b, preferred_element_type=F32), 0.0) for h in heads])
    return dq, dmk, dmv


def _shift_down(v, halo, k):
    rolled = pltpu.roll(v, k, 0)
    hrolled = pltpu.roll(halo, k, 0)[0:8]
    rows = lax.broadcasted_iota(jnp.int32, (8, v.shape[1]), 0)
    first = jnp.where(rows < k, hrolled, rolled[0:8])
    return jnp.concatenate([first, rolled[8:]], axis=0)


def _shift_up(v, halo, k):
    n = v.shape[0]
    rolled = pltpu.roll(v, n - k, 0)
    hrolled = pltpu.roll(halo, 8 - k, 0)[0:8]
    rows = lax.broadcasted_iota(jnp.int32, (8, v.shape[1]), 0)
    last = jnp.where(rows >= 8 - k, hrolled, rolled[n - 8:])
    return jnp.concatenate([rolled[:n - 8], last], axis=0)


def _conv_parts(p, ph, cw, first_tile, cwid):
    u = p[:, 0:cwid].astype(F32)
    bg = p[:, cwid:2 * cwid].astype(F32)
    cg = p[:, 2 * cwid:3 * cwid].astype(F32)
    v = cg * u
    vh = ph[:, 2 * cwid:3 * cwid].astype(F32) * ph[:, 0:cwid].astype(F32)
    vh = jnp.where(first_tile, 0.0, vh)
    v1 = _shift_down(v, vh, 1)
    v2 = _shift_down(v, vh, 2)
    conv = cw[0:1, :] * v2 + cw[1:2, :] * v1 + cw[2:3, :] * v
    return u, bg, cg, v, v1, v2, conv


def _halo_prev_spec(rows, n, tm):
    per = tm // rows
    return pl.BlockSpec((rows, n), lambda i: (jnp.maximum(i * per - 1, 0), 0))


def _halo_next_spec(rows, n, tm, t):
    per = tm // rows
    last = t // rows - 1
    return pl.BlockSpec((rows, n), lambda i: (jnp.minimum((i + 1) * per, last), 0))


def _mix_a_fwd(x, proj, convw, memkv, layer, wout, tm, name):
    t, d = x.shape
    n_mem = memkv.shape[0]
    mw = N_MEM_HEADS * HEAD_DIM
    cwid = d - mw
    pw = proj.shape[1]

    def body(x_ref, p_ref, ph_ref, cw_ref, mkv_ref, wo_ref, xo_ref, cat_ref):
        i = pl.program_id(0)
        p = p_ref[...]
        _, bg, _, _, _, _, conv = _conv_parts(p, ph_ref[...], cw_ref[...], i == 0, cwid)
        ytok = (bg * conv).astype(BF16)
        mkv = mkv_ref[...]
        ymem = _mem_attn_fwd(p[:, 3 * cwid:3 * cwid + mw], mkv[:, 0:mw], mkv[:, mw:2 * mw])
        cat = jnp.concatenate([ytok, ymem.astype(BF16)], axis=1)
        cat_ref[...] = cat
        xo_ref[...] = x_ref[...] + jnp.dot(cat, wo_ref[...], preferred_element_type=F32)

    return pl.pallas_call(
        body, name=name, grid=(t // tm,),
        in_specs=[_row_spec(tm, d), _row_spec(tm, pw), _halo_prev_spec(16, pw, tm), _const_spec((3, cwid)),
                  pl.BlockSpec((n_mem, 2 * mw), lambda i: (0, layer)), _const_spec((d, d))],
        out_specs=[_row_spec(tm, d), _row_spec(tm, d)],
        out_shape=[jax.ShapeDtypeStruct((t, d), F32), jax.ShapeDtypeStruct((t, d), BF16)],
        compiler_params=_cp(("parallel",), VMEM_BIG),
    )(x, proj, proj, convw, memkv, wout)


def _mix_a_bwd(dxm, proj, convw, memkv, layer, wout, tm, name):
    t, d = dxm.shape
    n_mem = memkv.shape[0]
    mw = N_MEM_HEADS * HEAD_DIM
    cwid = d - mw
    pw = proj.shape[1]
    nt = t // tm

    def body(dx_ref, dxn_ref, p_ref, ph_ref, pn_ref, cw_ref, mkv_ref, wo_ref,
             dp_ref, dcw_ref, dmkv_ref, dmk_acc, dmv_acc):
        i = pl.program_id(0)
        p = p_ref[...]
        cw = cw_ref[...]
        wo = wo_ref[...]
        u, bg, cg, v, v1, v2, conv = _conv_parts(p, ph_ref[...], cw, i == 0, cwid)
        dcat = lax.dot_general(dx_ref[...].astype(BF16), wo, NT, preferred_element_type=F32)
        dytok = dcat[:, 0:cwid]
        dymem_b = dcat[:, cwid:d].astype(BF16)
        pn = pn_ref[...]
        dcat_n = lax.dot_general(dxn_ref[...].astype(BF16), wo[0:cwid, :], NT, preferred_element_type=F32)
        dconv_n = jnp.where(i == nt - 1, 0.0, dcat_n * pn[:, cwid:2 * cwid].astype(F32))
        dbg = dytok * conv
        dconv = dytok * bg
        dv = cw[2:3, :] * dconv + cw[1:2, :] * _shift_up(dconv, dconv_n, 1) + cw[0:1, :] * _shift_up(dconv, dconv_n, 2)
        du = dv * cg
        dcg = dv * u
        rows8 = lax.broadcasted_iota(jnp.int32, (8, cwid), 0)
        dcw = (jnp.where(rows8 == 0, jnp.sum(dconv * v2, axis=0, keepdims=True), 0.0)
               + jnp.where(rows8 == 1, jnp.sum(dconv * v1, axis=0, keepdims=True), 0.0)
               + jnp.where(rows8 == 2, jnp.sum(dconv * v, axis=0, keepdims=True), 0.0))
        mkv = mkv_ref[...]
        qm = p[:, 3 * cwid:3 * cwid + mw]
        dqm, dmk, dmv = _mem_attn_bwd(qm, dymem_b, mkv[:, 0:mw], mkv[:, mw:2 * mw])
        dp_ref[...] = jnp.concatenate([du.astype(BF16), dbg.astype(BF16), dcg.astype(BF16), dqm.astype(BF16)], axis=1)

        @pl.when(i == 0)
        def _():
            dcw_ref[...] = jnp.zeros_like(dcw_ref)
            dmk_acc[...] = jnp.zeros_like(dmk_acc)
            dmv_acc[...] = jnp.zeros_like(dmv_acc)

        dcw_ref[...] += dcw
        dmk_acc[...] += dmk
        dmv_acc[...] += dmv

        @pl.when(i == nt - 1)
        def _():
            dmkv_ref[...] = jnp.concatenate([dmk_acc[...], dmv_acc[...]], axis=1)

    return pl.pallas_call(
        body, name=name, grid=(nt,),
        in_specs=[_row_spec(tm, d), _halo_next_spec(16, d, tm, t), _row_spec(tm, pw), _halo_prev_spec(16, pw, tm),
                  _halo_next_spec(16, pw, tm, t), _const_spec((3, cwid)),
                  pl.BlockSpec((n_mem, 2 * mw), lambda i: (0, layer)), _const_spec((d, d))],
        out_specs=[_row_spec(tm, pw), pl.BlockSpec((8, cwid), lambda i: (0, 0)),
                   pl.BlockSpec((n_mem, 2 * mw), lambda i: (0, 0))],
        out_shape=[jax.ShapeDtypeStruct((t, pw), BF16),
                   jax.ShapeDtypeStruct((8, cwid), F32), jax.ShapeDtypeStruct((n_mem, 2 * mw), F32)],
        scratch_shapes=[pltpu.VMEM((n_mem, mw), F32), pltpu.VMEM((n_mem, mw), F32)],
        compiler_params=_cp(("arbitrary",), VMEM_BIG),
    )(dxm, dxm, proj, proj, proj, convw, memkv, wout)


def _rel_tables():
    qi = np.arange(BLOCK, dtype=np.int32)[:, None]
    kj = np.arange(2 * BLOCK, dtype=np.int32)[None, :]
    dist = qi + BLOCK - kj
    inw = (dist >= 0) & (dist < BLOCK)
    max_exact = REL_BUCKETS // 2
    dd = np.maximum(np.maximum(dist, 0), 1).astype(np.float32)
    large = max_exact + (np.log(dd / np.float32(max_exact)) / np.float32(math.log(REL_MAX_DIST / max_exact))
                         * np.float32(REL_BUCKETS - max_exact)).astype(np.int32)
    large = np.minimum(large, REL_BUCKETS - 1)
    bucket = np.where(np.maximum(dist, 0) < max_exact, np.maximum(dist, 0), large)
    return np.where(inw, bucket, -1).astype(np.int32)


def _bias_tables(rel_bias, sinks, name):
    bucket_t = jnp.asarray(_rel_tables().T)

    def body(rb_ref, sk_ref, bk_ref, bias_ref, sink_ref):
        bk = bk_ref[...]
        prev = lax.broadcasted_iota(jnp.int32, bk.shape, 0) < BLOCK
        for h in range(N_KV_HEADS):
            for j in range(GROUP):
                head = GROUP * h + j
                acc = jnp.full(bk.shape, NEG, F32)
                for b in range(REL_BUCKETS):
                    acc = jnp.where(bk == b, rb_ref[b, head], acc)
                bias_ref[h, :, j * BLOCK:(j + 1) * BLOCK] = acc
                bias_ref[N_KV_HEADS + h, :, j * BLOCK:(j + 1) * BLOCK] = jnp.where(prev, NEG, acc)
                sink_ref[h, :, j * BLOCK:(j + 1) * BLOCK] = jnp.full((8, BLOCK), sk_ref[0, head], F32)

    smem = pl.BlockSpec(memory_space=pltpu.SMEM)
    return pl.pallas_call(
        body, name=name,
        in_specs=[smem, smem, pl.BlockSpec(memory_space=pltpu.VMEM)],
        out_specs=[pl.BlockSpec(memory_space=pltpu.VMEM), pl.BlockSpec(memory_space=pltpu.VMEM)],
        out_shape=[jax.ShapeDtypeStruct((2 * N_KV_HEADS, 2 * BLOCK, GROUP * BLOCK), F32),
                   jax.ShapeDtypeStruct((N_KV_HEADS, 8, GROUP * BLOCK), F32)],
    )(rel_bias, sinks.reshape(1, N_Q_HEADS), bucket_t)


def _bias_bwd(dbias_a, dbias_b, name):
    bucket_t = jnp.asarray(_rel_tables().T)

    def body(da_ref, db_ref, bk_ref, o_ref):
        bk = bk_ref[...]
        ri = lax.broadcasted_iota(jnp.int32, (REL_BUCKETS, 128), 0)
        ci = lax.broadcasted_iota(jnp.int32, (REL_BUCKETS, 128), 1)
        out = jnp.zeros((REL_BUCKETS, 128), F32)
        for h in range(N_KV_HEADS):
            dsum = da_ref[h] + db_ref[h]
            for j in range(GROUP):
                head = GROUP * h + j
                seg = dsum[:, j * BLOCK:(j + 1) * BLOCK]
                for b in range(REL_BUCKETS):
                    val = jnp.sum(jnp.sum(jnp.where(bk == b, seg, 0.0), axis=0, keepdims=True), axis=1, keepdims=True)
                    out = out + jnp.where((ri == b) & (ci == head), val, 0.0)
        o_ref[...] = out

    vm = pl.BlockSpec(memory_space=pltpu.VMEM)
    return pl.pallas_call(
        body, name=name, in_specs=[vm, vm, vm], out_specs=vm,
        out_shape=jax.ShapeDtypeStruct((REL_BUCKETS, 128), F32),
    )(dbias_a, dbias_b, bucket_t)


def _stack_members(ref, r0, width):
    blk = ref[pl.ds(r0, BLOCK), 0:GROUP * width]
    return jnp.concatenate([blk[:, j * width:(j + 1) * width] for j in range(GROUP)], axis=0)


def _mix_b_fwd(x, qp, kv, bias, sinkt, memkv, layer, wout, tm, name):
    t, d = x.shape
    n_mem = memkv.shape[0]
    mw = N_MEM_HEADS * HEAD_DIM
    qw = d - mw
    kw = N_KV_HEADS * HEAD_DIM
    nb = tm // BLOCK
    rows = GROUP * BLOCK

    def body(x_ref, q_ref, kv_ref, kvh_ref, bias_ref, sink_ref, mkv_ref, wo_ref, xo_ref, cat_ref, kvx, ytok):
        i = pl.program_id(0)
        kvx[0:BLOCK, :] = kvh_ref[...]
        kvx[BLOCK:BLOCK + tm, :] = kv_ref[...]
        colh = _col_head(kw)

        def blk(b, carry):
            r0 = pl.multiple_of(b * BLOCK, BLOCK)
            win = kvx[pl.ds(r0, 2 * BLOCK), :]
            kwin = win[:, 0:kw] * SCALE
            vwin = win[:, kw:2 * kw]
            qs = _stack_members(q_ref, r0, kw)
            first = ((i == 0) & (b == 0)).astype(jnp.int32) * N_KV_HEADS
            heads = range(N_KV_HEADS)
            ss = [lax.dot_general(_keep_head(kwin, colh, h), qs, NT, preferred_element_type=F32) for h in heads]
            ps = [_softmax_cols(ss[h] + bias_ref[first + h], sink_ref[h][0:1, :])[0].astype(BF16) for h in heads]
            o = _add4([lax.dot_general(ps[h], _keep_head(vwin, colh, h), TN, preferred_element_type=F32)
                       for h in heads])
            for j in range(GROUP):
                ytok[pl.ds(r0, BLOCK), j * kw:(j + 1) * kw] = o[j * BLOCK:(j + 1) * BLOCK].astype(BF16)
            return carry

        for b_static in range(nb):
            blk(b_static, 0)
        mkv = mkv_ref[...]
        ymem = _mem_attn_fwd(q_ref[:, qw:d], mkv[:, 0:mw], mkv[:, mw:2 * mw])
        cat = jnp.concatenate([ytok[...], ymem.astype(BF16)], axis=1)
        cat_ref[...] = cat
        xo_ref[...] = x_ref[...] + jnp.dot(cat, wo_ref[...], preferred_element_type=F32)

    return pl.pallas_call(
        body, name=name, grid=(t // tm,),
        in_specs=[_row_spec(tm, d), _row_spec(tm, d), _row_spec(tm, 2 * kw), _halo_prev_spec(BLOCK, 2 * kw, tm),
                  _const_spec((2 * N_KV_HEADS, 2 * BLOCK, rows)), _const_spec((N_KV_HEADS, 8, rows)),
                  pl.BlockSpec((n_mem, 2 * mw), lambda i: (0, layer)), _const_spec((d, d))],
        out_specs=[_row_spec(tm, d), _row_spec(tm, d)],
        out_shape=[jax.ShapeDtypeStruct((t, d), F32), jax.ShapeDtypeStruct((t, d), BF16)],
        scratch_shapes=[pltpu.VMEM((tm + BLOCK, 2 * kw), BF16), pltpu.VMEM((tm, qw), BF16)],
        compiler_params=_cp(("parallel",), VMEM_BIG),
    )(x, qp, kv, kv, bias, sinkt, memkv, wout)


def _mix_b_bwd(dxm, qp, kv, bias, sinkt, memkv, layer, wout, tm, name):
    t, d = dxm.shape
    n_mem = memkv.shape[0]
    mw = N_MEM_HEADS * HEAD_DIM
    qw = d - mw
    kw = N_KV_HEADS * HEAD_DIM
    nb = tm // BLOCK
    nt = t // tm
    rows = GROUP * BLOCK

    def body(dx_ref, q_ref, kv_ref, kvh_ref, bias_ref, sink_ref, mkv_ref, wo_ref,
             dq_ref, dkv_ref, dkvh_ref, dbias_ref, dsink_ref, dmkv_ref,
             kvx, dkvx, dcat_s, dmk_acc, dmv_acc):
        i = pl.program_id(0)

        @pl.when(i == 0)
        def _():
            dbias_ref[...] = jnp.zeros_like(dbias_ref)
            dsink_ref[...] = jnp.zeros_like(dsink_ref)
            dmk_acc[...] = jnp.zeros_like(dmk_acc)
            dmv_acc[...] = jnp.zeros_like(dmv_acc)

        kvx[0:BLOCK, :] = kvh_ref[...]
        kvx[BLOCK:BLOCK + tm, :] = kv_ref[...]
        dkvx[...] = jnp.zeros_like(dkvx)
        dcat_s[...] = lax.dot_general(dx_ref[...].astype(BF16), wo_ref[...], NT,
                                      preferred_element_type=F32).astype(BF16)
        colh = _col_head(kw)
        lane8 = lax.broadcasted_iota(jnp.int32, (8, 128), 1)

        def blk(b, carry):
            r0 = pl.multiple_of(b * BLOCK, BLOCK)
            win = kvx[pl.ds(r0, 2 * BLOCK), :]
            kwin = win[:, 0:kw] * SCALE
            vwin = win[:, kw:2 * kw]
            qs = _stack_members(q_ref, r0, kw)
            dos = _stack_members(dcat_s, r0, kw)
            first = ((i == 0) & (b == 0)).astype(jnp.int32) * N_KV_HEADS
            heads = range(N_KV_HEADS)
            khs = [_keep_head(kwin, colh, h) for h in heads]
            vhs = [_keep_head(vwin, colh, h) for h in heads]
            ss = [lax.dot_general(khs[h], qs, NT, preferred_element_type=F32) for h in heads]
            dps = [lax.dot_general(vhs[h], dos, NT, preferred_element_type=F32) for h in heads]
            dsink = jnp.zeros((8, 128), F32)
            pbs, dsbs = [], []
            for h in heads:
                p, sinkp = _softmax_cols(ss[h] + bias_ref[first + h], sink_ref[h][0:1, :])
                delta = jnp.sum(p * dps[h], axis=0, keepdims=True)
                ds = p * (dps[h] - delta)
                dbias_ref[h] += ds
                sd = sinkp * delta
                for j in range(GROUP):
                    val = -jnp.sum(sd[:, j * BLOCK:(j + 1) * BLOCK], axis=1, keepdims=True)
                    dsink = dsink + jnp.where(lane8 == 4 * j + h, val, 0.0)
                pbs.append(p.astype(BF16))
                dsbs.append(ds.astype(BF16))
            dq = _add4([lax.dot_general(dsbs[h], khs[h], TN, preferred_element_type=F32) for h in heads])
            dk = _add4([jnp.where(colh == h, jnp.dot(dsbs[h], qs, preferred_element_type=F32) * SCALE, 0.0)
                        for h in heads])
            dv = _add4([jnp.where(colh == h, jnp.dot(pbs[h], dos, preferred_element_type=F32), 0.0) for h in heads])
            for j in range(GROUP):
                dq_ref[pl.ds(r0, BLOCK), j * kw:(j + 1) * kw] = dq[j * BLOCK:(j + 1) * BLOCK].astype(BF16)
            dsink_ref[...] += dsink
            dkvx[pl.ds(r0, 2 * BLOCK), :] += jnp.concatenate([dk, dv], axis=1)
            return carry

        for b_static in range(nb):
            blk(b_static, 0)
        dkvh_ref[0] = dkvx[0:BLOCK, :]
        dkv_ref[...] = dkvx[BLOCK:BLOCK + tm, :]

        mkv = mkv_ref[...]
        dqm, dmk, dmv = _mem_attn_bwd(q_ref[:, qw:d], dcat_s[:, qw:d], mkv[:, 0:mw], mkv[:, mw:2 * mw])
        dq_ref[:, qw:d] = dqm.astype(BF16)
        dmk_acc[...] += dmk
        dmv_acc[...] += dmv

        @pl.when(i == nt - 1)
        def _():
            dmkv_ref[...] = jnp.concatenate([dmk_acc[...], dmv_acc[...]], axis=1)

    return pl.pallas_call(
        body, name=name, grid=(nt,),
        in_specs=[_row_spec(tm, d), _row_spec(tm, d), _row_spec(tm, 2 * kw), _halo_prev_spec(BLOCK, 2 * kw, tm),
                  _const_spec((2 * N_KV_HEADS, 2 * BLOCK, rows)), _const_spec((N_KV_HEADS, 8, rows)),
                  pl.BlockSpec((n_mem, 2 * mw), lambda i: (0, layer)), _const_spec((d, d))],
        out_specs=[_row_spec(tm, d), _row_spec(tm, 2 * kw),
                   pl.BlockSpec((1, BLOCK, 2 * kw), lambda i: (i, 0, 0)),
                   pl.BlockSpec((N_KV_HEADS, 2 * BLOCK, rows), lambda i: (0, 0, 0)),
                   pl.BlockSpec((8, 128), lambda i: (0, 0)),
                   pl.BlockSpec((n_mem, 2 * mw), lambda i: (0, 0))],
        out_shape=[jax.ShapeDtypeStruct((t, d), BF16),
                   jax.ShapeDtypeStruct((t, 2 * kw), F32), jax.ShapeDtypeStruct((nt, BLOCK, 2 * kw), F32),
                   jax.ShapeDtypeStruct((N_KV_HEADS, 2 * BLOCK, rows), F32), jax.ShapeDtypeStruct((8, 128), F32),
                   jax.ShapeDtypeStruct((n_mem, 2 * mw), F32)],
        scratch_shapes=[pltpu.VMEM((tm + BLOCK, 2 * kw), BF16), pltpu.VMEM((tm + BLOCK, 2 * kw), F32),
                        pltpu.VMEM((tm, d), BF16), pltpu.VMEM((n_mem, mw), F32), pltpu.VMEM((n_mem, mw), F32)],
        compiler_params=_cp(("arbitrary",), VMEM_BIG),
    )(dxm, qp, kv, kv, bias, sinkt, memkv, wout)


def _adam_math(w, g, m, v):
    m2 = ADAM_B1 * m + (1.0 - ADAM_B1) * g
    v2 = ADAM_B2 * v + (1.0 - ADAM_B2) * (g * g)
    m_hat = m2 / (1.0 - ADAM_B1 ** ADAM_STEP)
    v_hat = v2 / (1.0 - ADAM_B2 ** ADAM_STEP)
    delta = -ADAM_LR * (m_hat / (jnp.sqrt(v_hat) + ADAM_EPS) + ADAM_WD * w)
    return delta, m2, v2


def _adamw_sharded(w, land, m, v, name):
    nl, r, c = w.shape
    tr = max(cand for cand in range(16, r + 1, 16) if r % cand == 0 and cand * c <= 512 * 1024)

    def body(w_ref, a_ref, m_ref, v_ref, g_ref, d_ref, mo_ref, vo_ref):
        g = a_ref[0, 0].astype(F32) + a_ref[1, 0].astype(F32)
        for k in range(1, N_CHIPS):
            g = g + (a_ref[2 * k, 0].astype(F32) + a_ref[2 * k + 1, 0].astype(F32))
        delta, m2, v2 = _adam_math(w_ref[0], g, m_ref[0], v_ref[0])
        g_ref[0] = g
        d_ref[0] = delta
        mo_ref[0] = m2
        vo_ref[0] = v2

    rs = pl.BlockSpec((1, tr, c), lambda l, i: (l, i, 0))
    ps = pl.BlockSpec((2 * N_CHIPS, 1, tr, c), lambda l, i: (0, l, i, 0))
    sd = jax.ShapeDtypeStruct((nl, r, c), F32)
    return pl.pallas_call(
        body, name=name, grid=(nl, r // tr),
        in_specs=[rs, ps, rs, rs], out_specs=[rs, rs, rs, rs], out_shape=[sd, sd, sd, sd],
        compiler_params=_cp(("parallel", "parallel"), VMEM_BIG),
    )(w, land, m, v)


def _adamw_packed(w, g, m, v, name):
    def body(w_ref, g_ref, m_ref, v_ref, d_ref, mo_ref, vo_ref):
        delta, m2, v2 = _adam_math(w_ref[...], g_ref[...], m_ref[...], v_ref[...])
        d_ref[...] = delta
        mo_ref[...] = m2
        vo_ref[...] = v2

    vm = pl.BlockSpec(memory_space=pltpu.VMEM)
    sd = jax.ShapeDtypeStruct(w.shape, F32)
    return pl.pallas_call(body, name=name, in_specs=[vm] * 4, out_specs=[vm] * 3, out_shape=[sd] * 3)(w, g, m, v)


def _place():
    return lax.axis_index("x"), lax.axis_index("y"), lax.axis_index("c")


def _hbm(a):
    return pltpu.with_memory_space_constraint(a, pltpu.HBM)


def _peers(x, y, c, both_cores):
    chips = [(1 - x, y), (x, 1 - y), (1 - x, 1 - y)]
    if not both_cores:
        return [(px, py, c) for px, py in chips]
    return [(px, py, pc) for px, py in chips for pc in (c, 1 - c)] + [(x, y, 1 - c)]


def _chip_copy(src, land, gather, layer, chip_src, slot, send_sem, recv_sem, peer):
    s = src if gather else src.at[chip_src]
    d = land.at[slot] if layer is None else land.at[slot, layer]
    return pltpu.make_async_remote_copy(src_ref=s, dst_ref=d, send_sem=send_sem, recv_sem=recv_sem,
                                        device_id=peer, device_id_type=MESH)


def _own_copy(src, land, gather, layer, chip, slot, own_sem):
    s = src if gather else src.at[chip]
    d = land.at[slot] if layer is None else land.at[slot, layer]
    return pltpu.make_async_copy(s, d, own_sem)


def _exchange_start(srcs, lands, gather, layers, both_cores, after, name):
    n = len(srcs)
    npeer = 7 if both_cores else 3
    hbm = pl.BlockSpec(memory_space=pltpu.HBM)
    sem = pl.BlockSpec(memory_space=pltpu.SEMAPHORE)

    def body(*refs):
        ins, lds = refs[:n], refs[n:2 * n]
        first_out = 2 * n + len(after)
        send_sems, recv_sems, own_sems, token = refs[first_out], refs[first_out + 1], refs[first_out + 2], refs[-1]
        x, y, c = _place()
        slot = 2 * x + y if gather else 2 * (2 * x + y) + c
        for t in range(n):
            _own_copy(ins[t], lds[t], gather, layers[t], 2 * x + y, slot, own_sems.at[t]).start()
            for r, peer in enumerate(_peers(x, y, c, both_cores)):
                _chip_copy(ins[t], lds[t], gather, layers[t], 2 * peer[0] + peer[1], slot,
                           send_sems.at[npeer * t + r], recv_sems.at[npeer * t + r], peer).start()
        token[...] = jnp.zeros_like(token)

    both = list(srcs) + list(lands)
    outs = pl.pallas_call(
        body, name=name, in_specs=[hbm] * (2 * n) + [pl.BlockSpec(memory_space=pl.ANY)] * len(after),
        out_specs=(sem, sem, sem, *([hbm] * (2 * n)), pl.BlockSpec(memory_space=pltpu.VMEM)),
        out_shape=(pltpu.SemaphoreType.DMA((npeer * n,)), pltpu.SemaphoreType.DMA((npeer * n,)),
                   pltpu.SemaphoreType.DMA((n,)),
                   *[pltpu.HBM(a.shape, a.dtype) for a in both], jax.ShapeDtypeStruct((8, 128), F32)),
        input_output_aliases={t: 3 + t for t in range(2 * n)},
        compiler_params=_cp(has_side_effects=pltpu.SideEffectType.DATAFLOW_SIDE_EFFECTING),
    )(*[_hbm(a) for a in both], *after)
    return dict(send=outs[0], recv=outs[1], own=outs[2], srcs=list(outs[3:3 + n]), lands=list(outs[3 + n:3 + 2 * n]),
                token=outs[-1], gather=gather, layers=list(layers), both_cores=both_cores)


def _exchange_wait(groups, lands, land_ids, after, name):
    flat = [s for g in groups for s in g["srcs"]]
    ns, nl, ng, na = len(flat), len(lands), len(groups), len(after)
    hbm = pl.BlockSpec(memory_space=pltpu.HBM)
    sem = pl.BlockSpec(memory_space=pltpu.SEMAPHORE)

    def body(*refs):
        srcs, lds = refs[:ns], refs[ns:ns + nl]
        sems = refs[ns + nl:ns + nl + 3 * ng]
        x, y, c = _place()
        k = 0
        for gi, g in enumerate(groups):
            peers = _peers(x, y, c, g["both_cores"])
            for t in range(len(g["srcs"])):
                _own_copy(srcs[k], lds[land_ids[gi][t]], g["gather"], g["layers"][t], 0, 0, sems[3 * gi + 2].at[t]).wait()
                for r, peer in enumerate(peers):
                    cp = _chip_copy(srcs[k], lds[land_ids[gi][t]], g["gather"], g["layers"][t], 0, 0,
                                    sems[3 * gi].at[len(peers) * t + r], sems[3 * gi + 1].at[len(peers) * t + r], peer)
                    cp.wait_send()
                    cp.wait_recv()
                k += 1

    both = flat + list(lands)
    sem_args = [a for g in groups for a in (g["send"], g["recv"], g["own"])]
    outs = pl.pallas_call(
        body, name=name,
        in_specs=[hbm] * (ns + nl) + [sem] * (3 * ng) + [pl.BlockSpec(memory_space=pl.ANY)] * na,
        out_specs=[hbm] * (ns + nl),
        out_shape=[pltpu.HBM(a.shape, a.dtype) for a in both],
        input_output_aliases={t: t for t in range(ns + nl)},
        compiler_params=_cp(has_side_effects=pltpu.SideEffectType.DATAFLOW_SIDE_EFFECTING),
    )(*both, *sem_args, *after)
    return list(outs[ns:])


def _core_fill(lands, layers, name):
    n = len(lands)
    hbm = pl.BlockSpec(memory_space=pltpu.HBM)

    def body(*refs):
        ins = refs[:n]
        send_sems, recv_sems = refs[2 * n:]
        x, y, c = _place()
        copies = []
        for t in range(n):
            for k in range(N_CHIPS):
                mine = ins[t].at[2 * k + c] if layers[t] is None else ins[t].at[2 * k + c, layers[t]]
                cp = pltpu.make_async_remote_copy(
                    src_ref=mine, dst_ref=mine, send_sem=send_sems.at[N_CHIPS * t + k],
                    recv_sem=recv_sems.at[N_CHIPS * t + k], device_id=(x, y, 1 - c), device_id_type=MESH)
                cp.start()
                copies.append(cp)
        for cp in copies:
            cp.wait()

    return pl.pallas_call(
        body, name=name, in_specs=[hbm] * n, out_specs=[hbm] * n,
        out_shape=[jax.ShapeDtypeStruct(a.shape, a.dtype) for a in lands],
        input_output_aliases={t: t for t in range(n)},
        scratch_shapes=[pltpu.SemaphoreType.DMA((N_CHIPS * n,)), pltpu.SemaphoreType.DMA((N_CHIPS * n,))],
        compiler_params=_cp(has_side_effects=True),
    )(*lands)


def _all_reduce_packed(pack, name):
    r, c = pack.shape
    vm = pl.BlockSpec(memory_space=pltpu.VMEM)

    def body(p_ref, sum_ref, slots, send_sems, recv_sems):
        x, y, cc = _place()
        me = 4 * x + 2 * y + cc
        slots[me] = p_ref[...]
        copies = []
        for rel in range(1, 8):
            px = 1 - x if rel & 4 else x
            py = 1 - y if rel & 2 else y
            pc = 1 - cc if rel & 1 else cc
            cp = pltpu.make_async_remote_copy(
                src_ref=p_ref, dst_ref=slots.at[me], send_sem=send_sems.at[rel - 1], recv_sem=recv_sems.at[rel - 1],
                device_id=(px, py, pc), device_id_type=MESH)
            cp.start()
            copies.append(cp)
        for cp in copies:
            cp.wait()
        total = slots[0]
        for k in range(1, 8):
            total = total + slots[k]
        sum_ref[...] = total

    return pl.pallas_call(
        body, name=name, in_specs=[vm], out_specs=vm, out_shape=jax.ShapeDtypeStruct((r, c), F32),
        scratch_shapes=[pltpu.VMEM((8, r, c), F32), pltpu.SemaphoreType.DMA((7,)), pltpu.SemaphoreType.DMA((7,))],
        compiler_params=_cp(has_side_effects=True),
    )(pack)


def _pack(items):
    rows = []
    for a in items:
        flat = a.astype(F32).reshape(-1)
        pad = (-flat.shape[0]) % PACK_W
        rows.append(jnp.pad(flat, (0, pad)).reshape(-1, PACK_W))
    out = jnp.concatenate(rows, axis=0)
    pad_r = (-out.shape[0]) % 8
    return jnp.pad(out, ((0, pad_r), (0, 0)))


def _unpack(pack, shapes):
    outs, row = [], 0
    for s in shapes:
        n = int(np.prod(s))
        nr = -(-n // PACK_W)
        outs.append(pack[row:row + nr].reshape(-1)[:n].reshape(s))
        row += nr
    return outs


def _heads_to_member_major(w, axis):
    shp = w.shape
    pre, post = shp[:axis], shp[axis + 1:]
    w4 = w.reshape(pre + (N_KV_HEADS, GROUP, HEAD_DIM) + post)
    w4 = jnp.swapaxes(w4, len(pre), len(pre) + 1)
    return w4.reshape(shp)


def _heads_to_kv_major(w, axis):
    shp = w.shape
    pre, post = shp[:axis], shp[axis + 1:]
    w4 = w.reshape(pre + (GROUP, N_KV_HEADS, HEAD_DIM) + post)
    w4 = jnp.swapaxes(w4, len(pre), len(pre) + 1)
    return w4.reshape(shp)


def kernel(x, mem, norm_mix, norm_ffn, a_w_in, a_conv_w, a_w_out, kv_norm, w_kv, b_w_q, b_sinks, b_w_out, rel_bias, mem_norm, w_mem_kv, w_gate, w_up, w_down, final_norm, loss_target, m_norm_mix, m_norm_ffn, m_a_w_in, m_a_conv_w, m_a_w_out, m_kv_norm, m_w_kv, m_b_w_q, m_b_sinks, m_b_w_out, m_rel_bias, m_mem_norm, m_w_mem_kv, m_w_gate, m_w_up, m_w_down, m_final_norm, v_norm_mix, v_norm_ffn, v_a_w_in, v_a_conv_w, v_a_w_out, v_kv_norm, v_w_kv, v_b_w_q, v_b_sinks, v_b_w_out, v_rel_bias, v_mem_norm, v_w_mem_kv, v_w_gate, v_w_up, v_w_down, v_final_norm):
    t, d = x.shape[1], x.shape[2]
    tm = 512 if t % 512 == 0 and t >= 2048 else 256
    tl = 2 * tm if t % (2 * tm) == 0 else tm
    x0 = x.reshape(t, d)
    target = loss_target.reshape(t, d)
    mem2 = mem.reshape(mem.shape[1], d)
    n_mem = mem2.shape[0]
    ax, ay, ac = _place()
    chip = 2 * ax + ay
    cwid = a_conv_w.shape[2] * N_CHIPS
    qw = N_Q_HEADS * HEAD_DIM
    nq = N_CHIPS

    def landing(piece):
        return lax.empty((nq,) + piece.shape, piece.dtype)

    def mixer_shards(i):
        if i < N_A:
            shards = [a_w_in[i], a_w_out[i]] + ([w_mem_kv] if i == 0 else [])
        else:
            j = i - N_A
            shards = [b_w_q[j], b_w_out[j]] + ([w_kv] if j == 0 else [])
        return [a.astype(BF16) for a in shards]

    def ffn_shards(i):
        return [w_gate[i].T.astype(BF16), w_up[i].T.astype(BF16), w_down[i].astype(BF16)]

    conv_pad = jnp.pad(a_conv_w, ((0, 0), (0, 8 - a_conv_w.shape[1]), (0, (-a_conv_w.shape[2]) % 128)))
    first = mixer_shards(0)
    group_shards = {"0a": first[0:1], "0b": first[1:] + [conv_pad], "0f": ffn_shards(0)}
    for i in range(1, DEPTH):
        group_shards[str(i)] = ffn_shards(i) + mixer_shards(i)
    gathers, prev_tok = {}, []

    def start_group(key, after):
        shards = group_shards[key]
        gathers[key] = _exchange_start(shards, [landing(a) for a in shards], True, [None] * len(shards), False,
                                       after, "gather_start_" + key)
        return [gathers[key]["token"]]

    for key in ("0a", "0b", "0f"):
        prev_tok = start_group(key, prev_tok)

    def rows_full(g):
        return g.reshape((-1,) + g.shape[2:])

    def cols_full(g):
        return jnp.transpose(g, (1, 0, 2)).reshape(g.shape[1], -1)

    def landed_weights(key, after):
        g = gathers[key]
        return _exchange_wait([g], g["lands"], [list(range(len(g["lands"])))], after, "gather_wait_" + key)

    def mixer_weights(i, got):
        w_first, w_out = (cols_full(got[0]) if i < N_A else rows_full(got[0])), rows_full(got[1])
        if i >= N_A:
            w_first = jnp.concatenate([_heads_to_member_major(w_first[:, :qw], 1), w_first[:, qw:]], axis=1)
            w_out = jnp.concatenate([_heads_to_member_major(w_out[:qw, :], 0), w_out[qw:, :]], axis=0)
        return dict(w_first=w_first, w_out=w_out, extra=got[2] if len(got) > 2 else None)

    def ffn_weights(got):
        return dict(wg=rows_full(got[0]), wu=rows_full(got[1]), wd=rows_full(got[2]))

    bias, sinkt = [], []
    for j in range(2):
        bj, sj = _bias_tables(rel_bias, b_sinks[j], "bias_tables")
        bias.append(bj)
        sinkt.append(sj)

    ws = []
    xs, xmids, projs, cats, gates, ups = [x0], [], [], [], [], []
    kv = memkv = wmem = wkv = None
    for i in range(DEPTH):
        xin = xs[-1]
        if i == 0:
            w = dict(w_first=cols_full(landed_weights("0a", prev_tok)[0]))
        else:
            got = landed_weights(str(i), [xin])
            w = dict(mixer_weights(i, got[3:]), **ffn_weights(got[0:3]))
        ws.append(w)
        gm = norm_mix[i].reshape(1, d)
        if i < N_A:
            proj = _norm_mm(xin, gm, w["w_first"], tl, "proj_a")
            if i == 0:
                for key in ("1", "2", "3"):
                    prev_tok = start_group(key, prev_tok + [proj])
                got = landed_weights("0b", prev_tok)
                w["w_out"] = rows_full(got[0])
                full_mem = jnp.swapaxes(got[1], 0, 1).reshape(DEPTH, d, -1)
                wmem = jnp.transpose(full_mem, (1, 0, 2)).reshape(d, -1)
                memkv = _norm_mm(mem2, mem_norm.reshape(1, d), wmem, n_mem, "mem_kv")
                taps = got[2][:, :, 0:3, 0:a_conv_w.shape[2]]
                conv_full = jnp.transpose(taps, (1, 2, 0, 3)).reshape(N_A, 3, cwid)
            xmid, cat = _mix_a_fwd(xin, proj, conv_full[i], memkv, i, w["w_out"], tm, "mix_a_fwd")
        else:
            j = i - N_A
            if j == 0:
                wkv = rows_full(w["extra"])
                proj, kv = _norm_mm_pair(xin, gm, w["w_first"], kv_norm.reshape(1, d), wkv, tl, "proj_b_kv")
            else:
                proj = _norm_mm(xin, gm, w["w_first"], tl, "proj_b")
            xmid, cat = _mix_b_fwd(xin, proj, kv, bias[j], sinkt[j], memkv, i, w["w_out"], tm, "mix_b_fwd")
        if i == 0:
            w.update(ffn_weights(landed_weights("0f", [xmid])))
        if i < DEPTH - 1:
            xout, gate, up = _ffn_fwd(xmid, norm_ffn[i].reshape(1, d), w["wg"], w["wu"], w["wd"], tm, "ffn_fwd")
        else:
            xout = None
            gate, up, loss_part, dx, dg_final = _ffn_fwd_loss(xmid, norm_ffn[i].reshape(1, d), w["wg"], w["wu"], w["wd"],
                                                              final_norm.reshape(1, d), target, tm, "ffn_fwd_loss")
        projs.append(proj)
        cats.append(cat)
        xmids.append(xmid)
        gates.append(gate)
        ups.append(up)
        xs.append(xout)

    def rows_pieces(g):
        return g.astype(BF16).reshape((nq, g.shape[0] // nq) + g.shape[1:])

    def cols_pieces(g):
        return jnp.transpose(g.astype(BF16).reshape(g.shape[0], nq, g.shape[1] // nq), (1, 0, 2))

    swapped = ("w_gate", "w_up")
    stacked = dict(a_w_in=a_w_in, a_w_out=a_w_out, w_kv=w_kv[None], b_w_q=b_w_q, b_w_out=b_w_out,
                   w_mem_kv=w_mem_kv, w_gate=jnp.swapaxes(w_gate, 1, 2), w_up=jnp.swapaxes(w_up, 1, 2), w_down=w_down)
    names = list(stacked)
    land = {k: lax.empty((2 * nq,) + stacked[k].shape, BF16) for k in names}
    scatters, scatter_ids = [], []

    def scatter_start(key, items, both_cores):
        keys = [k for k, _, _ in items]
        st = _exchange_start([p for _, _, p in items], [land[k] for k in keys], False, [l for _, l, _ in items],
                             both_cores, [], "scatter_start_" + key)
        for k, ld in zip(keys, st["lands"]):
            land[k] = ld
        scatters.append(st)
        scatter_ids.append([names.index(k) for k in keys])
        return st["token"][0:1, 0:1]

    g_norm_mix, g_norm_ffn = [None] * DEPTH, [None] * DEPTH
    g_conv, g_sinks = [None] * 2, [None] * 2
    dmemkv = [None] * DEPTH
    dbias, dkv_main, dkv_halo = [None] * 2, [None] * 2, [None] * 2
    g_kv_norm = None
    tok = jnp.zeros((1, 1), F32)
    for i in reversed(range(DEPTH)):
        w = ws[i]
        dxm, dgate, dup, act, h2, dgf = _ffn_bwd(dx, xmids[i], norm_ffn[i].reshape(1, d) + tok, gates[i], ups[i],
                                            w["wg"], w["wu"], w["wd"], tm // 2, "ffn_bwd")
        g_norm_ffn[i] = dgf
        g_wd = _wgrad(act, dx, 2 * tm, "wgrad_down")
        g_wg = _wgrad(dgate, h2, 2 * tm, "wgrad_gate")
        g_wu = _wgrad(dup, h2, 2 * tm, "wgrad_up")
        items = [("w_gate", i, rows_pieces(g_wg)), ("w_up", i, rows_pieces(g_wu)), ("w_down", i, rows_pieces(g_wd))]
        if i == 0:
            tok = scatter_start("0f", items, True)
            items = []
        gm = norm_mix[i].reshape(1, d)
        if i < N_A:
            dproj, dcw, dmemkv[i] = _mix_a_bwd(dxm, projs[i], conv_full[i] + (tok if i == 0 else 0.0), memkv, i,
                                                    w["w_out"], tm, "mix_a_bwd")
            g_conv[i] = dcw[0:3]
            g_out = _wgrad(cats[i], dxm, 2 * tm, "wgrad_out")
            if i == 0:
                dmemkv_all = jnp.concatenate([a.astype(BF16) for a in dmemkv], axis=1)
                _, g_mem_norm, hmem = _mm_nt_normbwd(dmemkv_all, wmem, mem2, mem_norm.reshape(1, d),
                                                     jnp.zeros((n_mem, d), F32), n_mem, "mem_kv_bwd")
                g_wmem = _wgrad(hmem, dmemkv_all, n_mem, "wgrad_mem")
                g_wmem = jnp.transpose(g_wmem.reshape(nq, d // nq, DEPTH, -1), (0, 2, 1, 3))
                gm = gm + scatter_start("0o", [("a_w_out", 0, rows_pieces(g_out)), ("w_mem_kv", None, g_wmem)], False)
            dx, g_norm_mix[i], h = _mm_nt_normbwd(dproj, w["w_first"], xs[i], gm, dxm, tl, "proj_a_bwd")
            g_in = _wgrad(h, dproj, 2 * tm, "wgrad_in_a")
            items.append(("a_w_in", i, cols_pieces(g_in)))
            if i > 0:
                items.append(("a_w_out", i, rows_pieces(g_out)))
        else:
            j = i - N_A
            dqp, dkv_main[j], dkv_halo[j], dbias[j], dsk, dmemkv[i] = _mix_b_bwd(
                dxm, projs[i], kv, bias[j], sinkt[j], memkv, i, w["w_out"], tm, "mix_b_bwd")
            g_sinks[j] = dsk[0, 0:N_Q_HEADS].reshape(GROUP, N_KV_HEADS).T.reshape(N_Q_HEADS)
            g_out = _wgrad(cats[i], dxm, 2 * tm, "wgrad_out")
            if j == 0:
                dx, g_norm_mix[i], g_kv_norm, h, hkv, dkv = _mm_nt_normbwd_pair(
                    dqp, w["w_first"], gm, (dkv_main[0], dkv_halo[0], dkv_main[1], dkv_halo[1]), wkv,
                    kv_norm.reshape(1, d), xs[i], dxm, tm, "proj_b_kv_bwd")
            else:
                dx, g_norm_mix[i], h = _mm_nt_normbwd(dqp, w["w_first"], xs[i], gm, dxm, tl, "proj_b_bwd")
            g_q = _wgrad(h, dqp, 2 * tm, "wgrad_in_b")
            g_q = jnp.concatenate([_heads_to_kv_major(g_q[:, :qw], 1), g_q[:, qw:]], axis=1)
            g_out = jnp.concatenate([_heads_to_kv_major(g_out[:qw, :], 0), g_out[qw:, :]], axis=0)
            items += [("b_w_q", j, rows_pieces(g_q)), ("b_w_out", j, rows_pieces(g_out))]
            if j == 0:
                items.append(("w_kv", 0, rows_pieces(_wgrad(hkv, dkv, 2 * tm, "wgrad_kv"))))
        tok = scatter_start(str(i) if i else "0i", items, i > 0)
    grad_x = dx.reshape(x.shape)
    g_rel = _bias_bwd(dbias[0], dbias[1], "bias_bwd")[:, 0:N_Q_HEADS]

    small_shapes = [(DEPTH, d), (DEPTH, d), (d,), (d,), (d,), (2, N_Q_HEADS), (REL_BUCKETS, N_Q_HEADS),
                    (N_A, 3, cwid), ()]
    small = _pack([jnp.concatenate(g_norm_mix, axis=0), jnp.concatenate(g_norm_ffn, axis=0), g_kv_norm, g_mem_norm,
                   dg_final, jnp.stack(g_sinks), g_rel, jnp.stack(g_conv), loss_part[0, 0]])
    small_sum = _all_reduce_packed(small, "reduce_small")
    (gs_norm_mix, gs_norm_ffn, gs_kv_norm, gs_mem_norm, gs_final, gs_sinks, gs_rel, gs_conv_full, loss) = _unpack(
        small_sum, small_shapes)
    cq = cwid // N_CHIPS
    gs_conv = lax.dynamic_slice_in_dim(gs_conv_full, chip * cq, cq, axis=2)

    late = ("a_w_in", "a_w_out", "w_mem_kv")
    landed = dict(zip(names, _exchange_wait(scatters[:-2], [land[k] for k in names], scatter_ids[:-2], [small_sum],
                                            "scatter_wait_a")))

    weights = dict(norm_mix=norm_mix, norm_ffn=norm_ffn, a_w_in=a_w_in, a_conv_w=a_conv_w, a_w_out=a_w_out,
                   kv_norm=kv_norm, w_kv=w_kv, b_w_q=b_w_q, b_sinks=b_sinks, b_w_out=b_w_out, rel_bias=rel_bias,
                   mem_norm=mem_norm, w_mem_kv=w_mem_kv, w_gate=w_gate, w_up=w_up, w_down=w_down,
                   final_norm=final_norm)
    moms = dict(norm_mix=m_norm_mix, norm_ffn=m_norm_ffn, a_w_in=m_a_w_in, a_conv_w=m_a_conv_w, a_w_out=m_a_w_out,
                kv_norm=m_kv_norm, w_kv=m_w_kv, b_w_q=m_b_w_q, b_sinks=m_b_sinks, b_w_out=m_b_w_out,
                rel_bias=m_rel_bias, mem_norm=m_mem_norm, w_mem_kv=m_w_mem_kv, w_gate=m_w_gate, w_up=m_w_up,
                w_down=m_w_down, final_norm=m_final_norm)
    vars_ = dict(norm_mix=v_norm_mix, norm_ffn=v_norm_ffn, a_w_in=v_a_w_in, a_conv_w=v_a_conv_w, a_w_out=v_a_w_out,
                 kv_norm=v_kv_norm, w_kv=v_w_kv, b_w_q=v_b_w_q, b_sinks=v_b_sinks, b_w_out=v_b_w_out,
                 rel_bias=v_rel_bias, mem_norm=v_mem_norm, w_mem_kv=v_w_mem_kv, w_gate=v_w_gate, w_up=v_w_up,
                 w_down=v_w_down, final_norm=v_final_norm)
    order = list(weights)
    grads, deltas, new_m, new_v = {}, {}, {}, {}
    def adamw(k, ld):
        shp, stk = weights[k].shape, ld.shape[1:]
        view = (lambda a: jnp.swapaxes(a, 1, 2)) if k in swapped else (lambda a: a.reshape(stk))
        back = (lambda a: jnp.swapaxes(a, 1, 2)) if k in swapped else (lambda a: a.reshape(shp))
        outs = _adamw_sharded(view(weights[k]), ld, view(moms[k]), view(vars_[k]), "adamw_" + k)
        grads[k], deltas[k], new_m[k], new_v[k] = [back(o) for o in outs]

    for k in names:
        if k not in late:
            adamw(k, landed[k])
    late_ids = [[late.index(names[t]) for t in ids] for ids in scatter_ids[-2:]]
    late_landed = _exchange_wait(scatters[-2:], [landed[k] for k in late], late_ids, [deltas["w_down"]], "scatter_wait_b")
    filled = _core_fill(late_landed, [0, 0, None], "fill_cores")
    for k, ld in zip(late, filled):
        adamw(k, ld)
    small_names = ["norm_mix", "norm_ffn", "kv_norm", "mem_norm", "final_norm", "b_sinks", "rel_bias", "a_conv_w"]
    small_g = [gs_norm_mix, gs_norm_ffn, gs_kv_norm, gs_mem_norm, gs_final, gs_sinks, gs_rel, gs_conv]
    shapes = [weights[k].shape for k in small_names]
    dl_p, m_p, v_p = _adamw_packed(_pack([weights[k] for k in small_names]), _pack(small_g),
                                   _pack([moms[k] for k in small_names]), _pack([vars_[k] for k in small_names]),
                                   "adamw_small")
    for k, g, dl, m2, v2 in zip(small_names, small_g, _unpack(dl_p, shapes), _unpack(m_p, shapes), _unpack(v_p, shapes)):
        grads[k], deltas[k], new_m[k], new_v[k] = g.reshape(weights[k].shape), dl, m2, v2

    return (loss, grad_x, *[grads[k] for k in order], *[deltas[k] for k in order],
            *[new_m[k] for k in order], *[new_v[k] for k in order])
```

```python
import functools
import math

import numpy as np
import jax
import jax.numpy as jnp
from jax import lax
from jax.experimental import pallas as pl
from jax.experimental.pallas import tpu as pltpu

F32 = jnp.float32
BF16 = jnp.bfloat16
MESH = pl.DeviceIdType.MESH

EPS = 1e-5
HEAD_DIM = 64
N_MEM_HEADS = 4
N_KV_HEADS = 4
GROUP = 3
N_Q_HEADS = N_KV_HEADS * GROUP
BLOCK = 128
REL_BUCKETS = 32
REL_MAX_DIST = 128
SCALE = HEAD_DIM ** -0.5
NEG = -1e30
N_CHIPS = 4
N_A = 2
DEPTH = 4

ADAM_LR = 0.001
ADAM_B1 = 0.9
ADAM_B2 = 0.999
ADAM_EPS = 1e-08
ADAM_WD = 0.01
ADAM_STEP = 10

VMEM_BIG = 56 * 1024 * 1024
PACK_W = 1024

NT = (((1,), (1,)), ((), ()))
TN = (((0,), (0,)), ((), ()))


def _cp(sem=None, vmem=None, **kw):
    return pltpu.CompilerParams(dimension_semantics=sem, vmem_limit_bytes=vmem, **kw)


def _const_spec(shape):
    nd = len(shape)
    return pl.BlockSpec(shape, lambda i, _n=nd: (0,) * _n, pipeline_mode=pl.Buffered(1))


def _row_spec(tm, n):
    return pl.BlockSpec((tm, n), lambda i: (i, 0))


def _rms_parts(xv):
    r = lax.rsqrt(jnp.mean(xv * xv, axis=-1, keepdims=True) + EPS)
    return xv * r, r


def _sigmoid(z):
    return 1.0 / (1.0 + jnp.exp(-z))


def _ff_chunks(f):
    if f % 512 == 0 or f % 256 != 0:
        return [(0, f)] if f <= 1536 else [(0, f // 2), (f // 2, f - f // 2)]
    n = f // 256
    a = (n + 1) // 2 * 256
    return [(0, a), (a, f - a)]


def _norm_mm(x, g, w, tm, name):
    t, d = x.shape
    n = w.shape[1]

    def body(x_ref, g_ref, w_ref, o_ref):
        xhat, _ = _rms_parts(x_ref[...])
        h = (xhat * g_ref[...]).astype(BF16)
        o_ref[...] = jnp.dot(h, w_ref[...], preferred_element_type=F32).astype(BF16)

    return pl.pallas_call(
        body, name=name, grid=(t // tm,),
        in_specs=[_row_spec(tm, d), _const_spec((1, d)), _const_spec((d, n))],
        out_specs=_row_spec(tm, n),
        out_shape=jax.ShapeDtypeStruct((t, n), BF16),
        compiler_params=_cp(("parallel",), VMEM_BIG),
    )(x, g, w)


def _mm_nt_normbwd(dproj, w, x_in, g, dres, tm, name):
    t, d = x_in.shape
    n = w.shape[1]

    def body(dp_ref, w_ref, x_ref, g_ref, dr_ref, dx_ref, dg_ref, h_ref):
        i = pl.program_id(0)
        xhat, r = _rms_parts(x_ref[...])
        gv = g_ref[...]
        h_ref[...] = (xhat * gv).astype(BF16)
        dh = lax.dot_general(dp_ref[...], w_ref[...], NT, preferred_element_type=F32)
        dxhat = dh * gv
        dx = r * (dxhat - xhat * jnp.mean(dxhat * xhat, axis=-1, keepdims=True))
        dx_ref[...] = dr_ref[...] + dx

        @pl.when(i == 0)
        def _():
            dg_ref[...] = jnp.zeros_like(dg_ref)

        dg_ref[...] += jnp.sum(dh * xhat, axis=0, keepdims=True)

    return pl.pallas_call(
        body, name=name, grid=(t // tm,),
        in_specs=[_row_spec(tm, n), _const_spec((d, n)), _row_spec(tm, d), _const_spec((1, d)), _row_spec(tm, d)],
        out_specs=[_row_spec(tm, d), pl.BlockSpec((1, d), lambda i: (0, 0)), _row_spec(tm, d)],
        out_shape=[jax.ShapeDtypeStruct((t, d), F32), jax.ShapeDtypeStruct((1, d), F32),
                   jax.ShapeDtypeStruct((t, d), BF16)],
        compiler_params=_cp(("arbitrary",), VMEM_BIG),
    )(dproj, w, x_in, g, dres)


def _norm_mm_pair(x, g_a, w_a, g_b, w_b, tm, name):
    t, d = x.shape
    na, nb = w_a.shape[1], w_b.shape[1]

    def body(x_ref, ga_ref, wa_ref, gb_ref, wb_ref, oa_ref, ob_ref):
        xhat, _ = _rms_parts(x_ref[...])
        ha = (xhat * ga_ref[...]).astype(BF16)
        hb = (xhat * gb_ref[...]).astype(BF16)
        oa_ref[...] = jnp.dot(ha, wa_ref[...], preferred_element_type=F32).astype(BF16)
        ob_ref[...] = jnp.dot(hb, wb_ref[...], preferred_element_type=F32).astype(BF16)

    return pl.pallas_call(
        body, name=name, grid=(t // tm,),
        in_specs=[_row_spec(tm, d), _const_spec((1, d)), _const_spec((d, na)), _const_spec((1, d)), _const_spec((d, nb))],
        out_specs=[_row_spec(tm, na), _row_spec(tm, nb)],
        out_shape=[jax.ShapeDtypeStruct((t, na), BF16), jax.ShapeDtypeStruct((t, nb), BF16)],
        compiler_params=_cp(("parallel",), VMEM_BIG),
    )(x, g_a, w_a, g_b, w_b)


def _mm_nt_normbwd_pair(dp_a, w_a, g_a, kv_parts, w_b, g_b, x_in, dres, tm, name):
    t, d = x_in.shape
    na, nb = w_a.shape[1], w_b.shape[1]
    nt = t // tm
    main_1, halo_1, main_2, halo_2 = kv_parts

    def body(dpa_ref, wa_ref, ga_ref, m1_ref, h1_ref, m2_ref, h2_ref, wb_ref, gb_ref, x_ref, dr_ref,
             dx_ref, dga_ref, dgb_ref, ha_ref, hb_ref, dkv_ref):
        i = pl.program_id(0)
        s = m1_ref[...] + m2_ref[...]
        tail = jnp.where(i == nt - 1, 0.0, h1_ref[0] + h2_ref[0])
        dkv = jnp.concatenate([s[0:tm - BLOCK], s[tm - BLOCK:] + tail], axis=0).astype(BF16)
        dkv_ref[...] = dkv
        xhat, r = _rms_parts(x_ref[...])
        ga, gb = ga_ref[...], gb_ref[...]
        ha_ref[...] = (xhat * ga).astype(BF16)
        hb_ref[...] = (xhat * gb).astype(BF16)
        dha = lax.dot_general(dpa_ref[...], wa_ref[...], NT, preferred_element_type=F32)
        dhb = lax.dot_general(dkv, wb_ref[...], NT, preferred_element_type=F32)
        dxhat = dha * ga + dhb * gb
        dx_ref[...] = dr_ref[...] + r * (dxhat - xhat * jnp.mean(dxhat * xhat, axis=-1, keepdims=True))

        @pl.when(i == 0)
        def _():
            dga_ref[...] = jnp.zeros_like(dga_ref)
            dgb_ref[...] = jnp.zeros_like(dgb_ref)

        dga_ref[...] += jnp.sum(dha * xhat, axis=0, keepdims=True)
        dgb_ref[...] += jnp.sum(dhb * xhat, axis=0, keepdims=True)

    halo_spec = pl.BlockSpec((1, BLOCK, nb), lambda i: (jnp.minimum(i + 1, nt - 1), 0, 0))
    row1 = pl.BlockSpec((1, d), lambda i: (0, 0))
    return pl.pallas_call(
        body, name=name, grid=(nt,),
        in_specs=[_row_spec(tm, na), _const_spec((d, na)), _const_spec((1, d)), _row_spec(tm, nb), halo_spec,
                  _row_spec(tm, nb), halo_spec, _const_spec((d, nb)), _const_spec((1, d)), _row_spec(tm, d),
                  _row_spec(tm, d)],
        out_specs=[_row_spec(tm, d), row1, row1, _row_spec(tm, d), _row_spec(tm, d), _row_spec(tm, nb)],
        out_shape=[jax.ShapeDtypeStruct((t, d), F32), jax.ShapeDtypeStruct((1, d), F32), jax.ShapeDtypeStruct((1, d), F32),
                   jax.ShapeDtypeStruct((t, d), BF16), jax.ShapeDtypeStruct((t, d), BF16),
                   jax.ShapeDtypeStruct((t, nb), BF16)],
        compiler_params=_cp(("arbitrary",), VMEM_BIG),
    )(dp_a, w_a, g_a, main_1, halo_1, main_2, halo_2, w_b, g_b, x_in, dres)


def _ffn_fwd(x, g, wg, wu, wd, tm, name):
    t, d = x.shape
    f = wg.shape[0]
    chunks = _ff_chunks(f)

    def body(x_ref, g_ref, wg_ref, wu_ref, wd_ref, xo_ref, gate_ref, up_ref):
        xv = x_ref[...]
        xhat, _ = _rms_parts(xv)
        h = (xhat * g_ref[...]).astype(BF16)
        acc = xv
        for c0, cw in chunks:
            gt = lax.dot_general(h, wg_ref[c0:c0 + cw, :], NT, preferred_element_type=F32)
            ut = lax.dot_general(h, wu_ref[c0:c0 + cw, :], NT, preferred_element_type=F32)
            gate_ref[:, c0:c0 + cw] = gt.astype(BF16)
            up_ref[:, c0:c0 + cw] = ut.astype(BF16)
            a = (gt * _sigmoid(gt) * ut).astype(BF16)
            acc = acc + jnp.dot(a, wd_ref[c0:c0 + cw, :], preferred_element_type=F32)
        xo_ref[...] = acc

    return pl.pallas_call(
        body, name=name, grid=(t // tm,),
        in_specs=[_row_spec(tm, d), _const_spec((1, d)), _const_spec((f, d)), _const_spec((f, d)), _const_spec((f, d))],
        out_specs=[_row_spec(tm, d), _row_spec(tm, f), _row_spec(tm, f)],
        out_shape=[jax.ShapeDtypeStruct((t, d), F32), jax.ShapeDtypeStruct((t, f), BF16),
                   jax.ShapeDtypeStruct((t, f), BF16)],
        compiler_params=_cp(("parallel",), VMEM_BIG),
    )(x, g, wg, wu, wd)


def _ffn_fwd_loss(x, g, wg, wu, wd, g_final, target, tm, name):
    t, d = x.shape
    f = wg.shape[0]
    chunks = _ff_chunks(f)

    def body(x_ref, g_ref, wg_ref, wu_ref, wd_ref, gf_ref, t_ref, gate_ref, up_ref, loss_ref, dx_ref, dgf_ref):
        i = pl.program_id(0)
        xv = x_ref[...]
        xhat, _ = _rms_parts(xv)
        h = (xhat * g_ref[...]).astype(BF16)
        acc = xv
        for c0, cw in chunks:
            gt = lax.dot_general(h, wg_ref[c0:c0 + cw, :], NT, preferred_element_type=F32)
            ut = lax.dot_general(h, wu_ref[c0:c0 + cw, :], NT, preferred_element_type=F32)
            gate_ref[:, c0:c0 + cw] = gt.astype(BF16)
            up_ref[:, c0:c0 + cw] = ut.astype(BF16)
            a = (gt * _sigmoid(gt) * ut).astype(BF16)
            acc = acc + jnp.dot(a, wd_ref[c0:c0 + cw, :], preferred_element_type=F32)
        xhat_o, r_o = _rms_parts(acc)
        gf = gf_ref[...]
        err = xhat_o * gf - t_ref[...]
        dy = err * (1.0 / d)
        dxhat = dy * gf
        dx_ref[...] = r_o * (dxhat - xhat_o * jnp.mean(dxhat * xhat_o, axis=-1, keepdims=True))

        @pl.when(i == 0)
        def _():
            dgf_ref[...] = jnp.zeros_like(dgf_ref)
            loss_ref[...] = jnp.zeros_like(loss_ref)

        dgf_ref[...] += jnp.sum(dy * xhat_o, axis=0, keepdims=True)
        part = jnp.sum(jnp.sum(err * err, axis=-1, keepdims=True), axis=0, keepdims=True) * (0.5 / d)
        loss_ref[...] += jnp.broadcast_to(part, loss_ref.shape)

    return pl.pallas_call(
        body, name=name, grid=(t // tm,),
        in_specs=[_row_spec(tm, d), _const_spec((1, d)), _const_spec((f, d)), _const_spec((f, d)), _const_spec((f, d)),
                  _const_spec((1, d)), _row_spec(tm, d)],
        out_specs=[_row_spec(tm, f), _row_spec(tm, f), pl.BlockSpec((8, 128), lambda i: (0, 0)), _row_spec(tm, d),
                   pl.BlockSpec((1, d), lambda i: (0, 0))],
        out_shape=[jax.ShapeDtypeStruct((t, f), BF16), jax.ShapeDtypeStruct((t, f), BF16),
                   jax.ShapeDtypeStruct((8, 128), F32), jax.ShapeDtypeStruct((t, d), F32),
                   jax.ShapeDtypeStruct((1, d), F32)],
        compiler_params=_cp(("arbitrary",), VMEM_BIG),
    )(x, g, wg, wu, wd, g_final, target)


def _ffn_bwd(dxo, xm, g, gate, up, wg, wu, wd, tm, name):
    t, d = xm.shape
    f = wg.shape[0]
    chunks = _ff_chunks(f)

    def body(dxo_ref, xm_ref, g_ref, gate_ref, up_ref, wg_ref, wu_ref, wd_ref,
             dxm_ref, dgate_ref, dup_ref, act_ref, h2_ref, dg_ref):
        i = pl.program_id(0)
        dxo_v = dxo_ref[...]
        dxo_b = dxo_v.astype(BF16)
        xhat, r = _rms_parts(xm_ref[...])
        gv = g_ref[...]
        h2_ref[...] = (xhat * gv).astype(BF16)
        dh = jnp.zeros((tm, d), F32)
        for c0, cw in chunks:
            dact = lax.dot_general(dxo_b, wd_ref[c0:c0 + cw, :], NT, preferred_element_type=F32)
            gt = gate_ref[:, c0:c0 + cw].astype(F32)
            ut = up_ref[:, c0:c0 + cw].astype(F32)
            sg = _sigmoid(gt)
            sl = gt * sg
            act_ref[:, c0:c0 + cw] = (sl * ut).astype(BF16)
            dgt = (dact * ut * (sg * (1.0 + gt * (1.0 - sg)))).astype(BF16)
            dut = (dact * sl).astype(BF16)
            dgate_ref[:, c0:c0 + cw] = dgt
            dup_ref[:, c0:c0 + cw] = dut
            dh = dh + jnp.dot(dgt, wg_ref[c0:c0 + cw, :], preferred_element_type=F32)
            dh = dh + jnp.dot(dut, wu_ref[c0:c0 + cw, :], preferred_element_type=F32)
        dxhat = dh * gv
        dx = r * (dxhat - xhat * jnp.mean(dxhat * xhat, axis=-1, keepdims=True))
        dxm_ref[...] = dxo_v + dx

        @pl.when(i == 0)
        def _():
            dg_ref[...] = jnp.zeros_like(dg_ref)

        dg_ref[...] += jnp.sum(dh * xhat, axis=0, keepdims=True)

    return pl.pallas_call(
        body, name=name, grid=(t // tm,),
        in_specs=[_row_spec(tm, d), _row_spec(tm, d), _const_spec((1, d)), _row_spec(tm, f), _row_spec(tm, f),
                  _const_spec((f, d)), _const_spec((f, d)), _const_spec((f, d))],
        out_specs=[_row_spec(tm, d), _row_spec(tm, f), _row_spec(tm, f), _row_spec(tm, f), _row_spec(tm, d),
                   pl.BlockSpec((1, d), lambda i: (0, 0))],
        out_shape=[jax.ShapeDtypeStruct((t, d), F32), jax.ShapeDtypeStruct((t, f), BF16),
                   jax.ShapeDtypeStruct((t, f), BF16), jax.ShapeDtypeStruct((t, f), BF16),
                   jax.ShapeDtypeStruct((t, d), BF16), jax.ShapeDtypeStruct((1, d), F32)],
        compiler_params=_cp(("arbitrary",), VMEM_BIG),
    )(dxo, xm, g, gate, up, wg, wu, wd)


def _wgrad(a, b, tt, name):
    t, k = a.shape
    n = b.shape[1]
    nt = t // tt

    def body(a_ref, b_ref, o_ref, acc):
        i = pl.program_id(0)

        @pl.when(i == 0)
        def _():
            acc[...] = jnp.zeros_like(acc)

        acc[...] += lax.dot_general(a_ref[...].astype(BF16), b_ref[...].astype(BF16), TN,
                                    preferred_element_type=F32)

        @pl.when(i == nt - 1)
        def _():
            o_ref[...] = acc[...].astype(BF16)

    return pl.pallas_call(
        body, name=name, grid=(nt,),
        in_specs=[_row_spec(tt, k), _row_spec(tt, n)],
        out_specs=pl.BlockSpec((k, n), lambda i: (0, 0)),
        out_shape=jax.ShapeDtypeStruct((k, n), BF16),
        scratch_shapes=[pltpu.VMEM((k, n), F32)],
        compiler_params=_cp(("arbitrary",), VMEM_BIG),
    )(a, b)


def _col_head(width):
    return lax.broadcasted_iota(jnp.int32, (1, width), 1) // HEAD_DIM


def _keep_head(a, colh, h):
    return jnp.where(colh == h, a, jnp.zeros_like(a))


def _softmax_cols(s, sink=None):
    m = jnp.max(s, axis=0, keepdims=True)
    if sink is not None:
        m = jnp.maximum(m, sink)
    p = jnp.exp(s - m)
    l = jnp.sum(p, axis=0, keepdims=True)
    if sink is None:
        return p * (1.0 / l), None
    es = jnp.exp(sink - m)
    inv = 1.0 / (l + es)
    return p * inv, es * inv


def _add4(v):
    return (v[0] + v[1]) + (v[2] + v[3])


def _mem_attn_fwd(qm, mk, mv):
    colh = _col_head(mk.shape[1])
    mks = mk * SCALE
    heads = range(N_MEM_HEADS)
    ss = [lax.dot_general(_keep_head(mks, colh, h), qm, NT, preferred_element_type=F32) for h in heads]
    ps = [_softmax_cols(s)[0].astype(BF16) for s in ss]
    return _add4([lax.dot_general(ps[h], _keep_head(mv, colh, h), TN, preferred_element_type=F32) for h in heads])


def _mem_attn_bwd(qm, dy_b, mk, mv):
    colh = _col_head(mk.shape[1])
    mks = mk * SCALE
    heads = range(N_MEM_HEADS)
    khs = [_keep_head(mks, colh, h) for h in heads]
    vhs = [_keep_head(mv, colh, h) for h in heads]
    ss = [lax.dot_general(khs[h], qm, NT, preferred_element_type=F32) for h in heads]
    dps = [lax.dot_general(vhs[h], dy_b, NT, preferred_element_type=F32) for h in heads]
    pbs, dsbs = [], []
    for h in heads:
        p, _ = _softmax_cols(ss[h])
        ds = p * (dps[h] - jnp.sum(p * dps[h], axis=0, keepdims=True))
        pbs.append(p.astype(BF16))
        dsbs.append(ds.astype(BF16))
    dq = _add4([lax.dot_general(dsbs[h], khs[h], TN, preferred_element_type=F32) for h in heads])
    dmk = _add4([jnp.where(colh == h, jnp.dot(dsbs[h], qm, preferred_element_type=F32) * SCALE, 0.0) for h in heads])
    dmv = _add4([jnp.where(colh == h, jnp.dot(pbs[h], dy_b, preferred_element_type=F32), 0.0) for h in heads])
    return dq, dmk, dmv


def _shift_down(v, halo, k):
    rolled = pltpu.roll(v, k, 0)
    hrolled = pltpu.roll(halo, k, 0)[0:8]
    rows = lax.broadcasted_iota(jnp.int32, (8, v.shape[1]), 0)
    first = jnp.where(rows < k, hrolled, rolled[0:8])
    return jnp.concatenate([first, rolled[8:]], axis=0)


def _shift_up(v, halo, k):
    n = v.shape[0]
    rolled = pltpu.roll(v, n - k, 0)
    hrolled = pltpu.roll(halo, 8 - k, 0)[0:8]
    rows = lax.broadcasted_iota(jnp.int32, (8, v.shape[1]), 0)
    last = jnp.where(rows >= 8 - k, hrolled, rolled[n - 8:])
    return jnp.concatenate([rolled[:n - 8], last], axis=0)


def _conv_parts(p, ph, cw, first_tile, cwid):
    u = p[:, 0:cwid].astype(F32)
    bg = p[:, cwid:2 * cwid].astype(F32)
    cg = p[:, 2 * cwid:3 * cwid].astype(F32)
    v = cg * u
    vh = ph[:, 2 * cwid:3 * cwid].astype(F32) * ph[:, 0:cwid].astype(F32)
    vh = jnp.where(first_tile, 0.0, vh)
    v1 = _shift_down(v, vh, 1)
    v2 = _shift_down(v, vh, 2)
    conv = cw[0:1, :] * v2 + cw[1:2, :] * v1 + cw[2:3, :] * v
    return u, bg, cg, v, v1, v2, conv


def _halo_prev_spec(rows, n, tm):
    per = tm // rows
    return pl.BlockSpec((rows, n), lambda i: (jnp.maximum(i * per - 1, 0), 0))


def _halo_next_spec(rows, n, tm, t):
    per = tm // rows
    last = t // rows - 1
    return pl.BlockSpec((rows, n), lambda i: (jnp.minimum((i + 1) * per, last), 0))


def _mix_a_fwd(x, proj, convw, memkv, layer, wout, tm, name):
    t, d = x.shape
    n_mem = memkv.shape[0]
    mw = N_MEM_HEADS * HEAD_DIM
    cwid = d - mw
    pw = proj.shape[1]

    def body(x_ref, p_ref, ph_ref, cw_ref, mkv_ref, wo_ref, xo_ref, cat_ref):
        i = pl.program_id(0)
        p = p_ref[...]
        _, bg, _, _, _, _, conv = _conv_parts(p, ph_ref[...], cw_ref[...], i == 0, cwid)
        ytok = (bg * conv).astype(BF16)
        mkv = mkv_ref[...]
        ymem = _mem_attn_fwd(p[:, 3 * cwid:3 * cwid + mw], mkv[:, 0:mw], mkv[:, mw:2 * mw])
        cat = jnp.concatenate([ytok, ymem.astype(BF16)], axis=1)
        cat_ref[...] = cat
        xo_ref[...] = x_ref[...] + jnp.dot(cat, wo_ref[...], preferred_element_type=F32)

    return pl.pallas_call(
        body, name=name, grid=(t // tm,),
        in_specs=[_row_spec(tm, d), _row_spec(tm, pw), _halo_prev_spec(16, pw, tm), _const_spec((3, cwid)),
                  pl.BlockSpec((n_mem, 2 * mw), lambda i: (0, layer)), _const_spec((d, d))],
        out_specs=[_row_spec(tm, d), _row_spec(tm, d)],
        out_shape=[jax.ShapeDtypeStruct((t, d), F32), jax.ShapeDtypeStruct((t, d), BF16)],
        compiler_params=_cp(("parallel",), VMEM_BIG),
    )(x, proj, proj, convw, memkv, wout)


def _mix_a_bwd(dxm, proj, convw, memkv, layer, wout, tm, name):
    t, d = dxm.shape
    n_mem = memkv.shape[0]
    mw = N_MEM_HEADS * HEAD_DIM
    cwid = d - mw
    pw = proj.shape[1]
    nt = t // tm

    def body(dx_ref, dxn_ref, p_ref, ph_ref, pn_ref, cw_ref, mkv_ref, wo_ref,
             dp_ref, dcw_ref, dmkv_ref, dmk_acc, dmv_acc):
        i = pl.program_id(0)
        p = p_ref[...]
        cw = cw_ref[...]
        wo = wo_ref[...]
        u, bg, cg, v, v1, v2, conv = _conv_parts(p, ph_ref[...], cw, i == 0, cwid)
        dcat = lax.dot_general(dx_ref[...].astype(BF16), wo, NT, preferred_element_type=F32)
        dytok = dcat[:, 0:cwid]
        dymem_b = dcat[:, cwid:d].astype(BF16)
        pn = pn_ref[...]
        dcat_n = lax.dot_general(dxn_ref[...].astype(BF16), wo[0:cwid, :], NT, preferred_element_type=F32)
        dconv_n = jnp.where(i == nt - 1, 0.0, dcat_n * pn[:, cwid:2 * cwid].astype(F32))
        dbg = dytok * conv
        dconv = dytok * bg
        dv = cw[2:3, :] * dconv + cw[1:2, :] * _shift_up(dconv, dconv_n, 1) + cw[0:1, :] * _shift_up(dconv, dconv_n, 2)
        du = dv * cg
        dcg = dv * u
        rows8 = lax.broadcasted_iota(jnp.int32, (8, cwid), 0)
        dcw = (jnp.where(rows8 == 0, jnp.sum(dconv * v2, axis=0, keepdims=True), 0.0)
               + jnp.where(rows8 == 1, jnp.sum(dconv * v1, axis=0, keepdims=True), 0.0)
               + jnp.where(rows8 == 2, jnp.sum(dconv * v, axis=0, keepdims=True), 0.0))
        mkv = mkv_ref[...]
        qm = p[:, 3 * cwid:3 * cwid + mw]
        dqm, dmk, dmv = _mem_attn_bwd(qm, dymem_b, mkv[:, 0:mw], mkv[:, mw:2 * mw])
        dp_ref[...] = jnp.concatenate([du.astype(BF16), dbg.astype(BF16), dcg.astype(BF16), dqm.astype(BF16)], axis=1)

        @pl.when(i == 0)
        def _():
            dcw_ref[...] = jnp.zeros_like(dcw_ref)
            dmk_acc[...] = jnp.zeros_like(dmk_acc)
            dmv_acc[...] = jnp.zeros_like(dmv_acc)

        dcw_ref[...] += dcw
        dmk_acc[...] += dmk
        dmv_acc[...] += dmv

        @pl.when(i == nt - 1)
        def _():
            dmkv_ref[...] = jnp.concatenate([dmk_acc[...], dmv_acc[...]], axis=1)

    return pl.pallas_call(
        body, name=name, grid=(nt,),
        in_specs=[_row_spec(tm, d), _halo_next_spec(16, d, tm, t), _row_spec(tm, pw), _halo_prev_spec(16, pw, tm),
                  _halo_next_spec(16, pw, tm, t), _const_spec((3, cwid)),
                  pl.BlockSpec((n_mem, 2 * mw), lambda i: (0, layer)), _const_spec((d, d))],
        out_specs=[_row_spec(tm, pw), pl.BlockSpec((8, cwid), lambda i: (0, 0)),
                   pl.BlockSpec((n_mem, 2 * mw), lambda i: (0, 0))],
        out_shape=[jax.ShapeDtypeStruct((t, pw), BF16),
                   jax.ShapeDtypeStruct((8, cwid), F32), jax.ShapeDtypeStruct((n_mem, 2 * mw), F32)],
        scratch_shapes=[pltpu.VMEM((n_mem, mw), F32), pltpu.VMEM((n_mem, mw), F32)],
        compiler_params=_cp(("arbitrary",), VMEM_BIG),
    )(dxm, dxm, proj, proj, proj, convw, memkv, wout)


def _rel_tables():
    qi = np.arange(BLOCK, dtype=np.int32)[:, None]
    kj = np.arange(2 * BLOCK, dtype=np.int32)[None, :]
    dist = qi + BLOCK - kj
    inw = (dist >= 0) & (dist < BLOCK)
    max_exact = REL_BUCKETS // 2
    dd = np.maximum(np.maximum(dist, 0), 1).astype(np.float32)
    large = max_exact + (np.log(dd / np.float32(max_exact)) / np.float32(math.log(REL_MAX_DIST / max_exact))
                         * np.float32(REL_BUCKETS - max_exact)).astype(np.int32)
    large = np.minimum(large, REL_BUCKETS - 1)
    bucket = np.where(np.maximum(dist, 0) < max_exact, np.maximum(dist, 0), large)
    return np.where(inw, bucket, -1).astype(np.int32)


def _bias_tables(rel_bias, sinks, name):
    bucket_t = jnp.asarray(_rel_tables().T)

    def body(rb_ref, sk_ref, bk_ref, bias_ref, sink_ref):
        bk = bk_ref[...]
        prev = lax.broadcasted_iota(jnp.int32, bk.shape, 0) < BLOCK
        for h in range(N_KV_HEADS):
            for j in range(GROUP):
                head = GROUP * h + j
                acc = jnp.full(bk.shape, NEG, F32)
                for b in range(REL_BUCKETS):
                    acc = jnp.where(bk == b, rb_ref[b, head], acc)
                bias_ref[h, :, j * BLOCK:(j + 1) * BLOCK] = acc
                bias_ref[N_KV_HEADS + h, :, j * BLOCK:(j + 1) * BLOCK] = jnp.where(prev, NEG, acc)
                sink_ref[h, :, j * BLOCK:(j + 1) * BLOCK] = jnp.full((8, BLOCK), sk_ref[0, head], F32)

    smem = pl.BlockSpec(memory_space=pltpu.SMEM)
    return pl.pallas_call(
        body, name=name,
        in_specs=[smem, smem, pl.BlockSpec(memory_space=pltpu.VMEM)],
        out_specs=[pl.BlockSpec(memory_space=pltpu.VMEM), pl.BlockSpec(memory_space=pltpu.VMEM)],
        out_shape=[jax.ShapeDtypeStruct((2 * N_KV_HEADS, 2 * BLOCK, GROUP * BLOCK), F32),
                   jax.ShapeDtypeStruct((N_KV_HEADS, 8, GROUP * BLOCK), F32)],
    )(rel_bias, sinks.reshape(1, N_Q_HEADS), bucket_t)


def _bias_bwd(dbias_a, dbias_b, name):
    bucket_t = jnp.asarray(_rel_tables().T)

    def body(da_ref, db_ref, bk_ref, o_ref):
        bk = bk_ref[...]
        ri = lax.broadcasted_iota(jnp.int32, (REL_BUCKETS, 128), 0)
        ci = lax.broadcasted_iota(jnp.int32, (REL_BUCKETS, 128), 1)
        out = jnp.zeros((REL_BUCKETS, 128), F32)
        for h in range(N_KV_HEADS):
            dsum = da_ref[h] + db_ref[h]
            for j in range(GROUP):
                head = GROUP * h + j
                seg = dsum[:, j * BLOCK:(j + 1) * BLOCK]
                for b in range(REL_BUCKETS):
                    val = jnp.sum(jnp.sum(jnp.where(bk == b, seg, 0.0), axis=0, keepdims=True), axis=1, keepdims=True)
                    out = out + jnp.where((ri == b) & (ci == head), val, 0.0)
        o_ref[...] = out

    vm = pl.BlockSpec(memory_space=pltpu.VMEM)
    return pl.pallas_call(
        body, name=name, in_specs=[vm, vm, vm], out_specs=vm,
        out_shape=jax.ShapeDtypeStruct((REL_BUCKETS, 128), F32),
    )(dbias_a, dbias_b, bucket_t)


def _stack_members(ref, r0, width):
    blk = ref[pl.ds(r0, BLOCK), 0:GROUP * width]
    return jnp.concatenate([blk[:, j * width:(j + 1) * width] for j in range(GROUP)], axis=0)


def _mix_b_fwd(x, qp, kv, bias, sinkt, memkv, layer, wout, tm, name):
    t, d = x.shape
    n_mem = memkv.shape[0]
    mw = N_MEM_HEADS * HEAD_DIM
    qw = d - mw
    kw = N_KV_HEADS * HEAD_DIM
    nb = tm // BLOCK
    rows = GROUP * BLOCK

    def body(x_ref, q_ref, kv_ref, kvh_ref, bias_ref, sink_ref, mkv_ref, wo_ref, xo_ref, cat_ref, kvx, ytok):
        i = pl.program_id(0)
        kvx[0:BLOCK, :] = kvh_ref[...]
        kvx[BLOCK:BLOCK + tm, :] = kv_ref[...]
        colh = _col_head(kw)

        def blk(b, carry):
            r0 = pl.multiple_of(b * BLOCK, BLOCK)
            win = kvx[pl.ds(r0, 2 * BLOCK), :]
            kwin = win[:, 0:kw] * SCALE
            vwin = win[:, kw:2 * kw]
            qs = _stack_members(q_ref, r0, kw)
            first = ((i == 0) & (b == 0)).astype(jnp.int32) * N_KV_HEADS
            heads = range(N_KV_HEADS)
            ss = [lax.dot_general(_keep_head(kwin, colh, h), qs, NT, preferred_element_type=F32) for h in heads]
            ps = [_softmax_cols(ss[h] + bias_ref[first + h], sink_ref[h][0:1, :])[0].astype(BF16) for h in heads]
            o = _add4([lax.dot_general(ps[h], _keep_head(vwin, colh, h), TN, preferred_element_type=F32)
                       for h in heads])
            for j in range(GROUP):
                ytok[pl.ds(r0, BLOCK), j * kw:(j + 1) * kw] = o[j * BLOCK:(j + 1) * BLOCK].astype(BF16)
            return carry

        for b_static in range(nb):
            blk(b_static, 0)
        mkv = mkv_ref[...]
        ymem = _mem_attn_fwd(q_ref[:, qw:d], mkv[:, 0:mw], mkv[:, mw:2 * mw])
        cat = jnp.concatenate([ytok[...], ymem.astype(BF16)], axis=1)
        cat_ref[...] = cat
        xo_ref[...] = x_ref[...] + jnp.dot(cat, wo_ref[...], preferred_element_type=F32)

    return pl.pallas_call(
        body, name=name, grid=(t // tm,),
        in_specs=[_row_spec(tm, d), _row_spec(tm, d), _row_spec(tm, 2 * kw), _halo_prev_spec(BLOCK, 2 * kw, tm),
                  _const_spec((2 * N_KV_HEADS, 2 * BLOCK, rows)), _const_spec((N_KV_HEADS, 8, rows)),
                  pl.BlockSpec((n_mem, 2 * mw), lambda i: (0, layer)), _const_spec((d, d))],
        out_specs=[_row_spec(tm, d), _row_spec(tm, d)],
        out_shape=[jax.ShapeDtypeStruct((t, d), F32), jax.ShapeDtypeStruct((t, d), BF16)],
        scratch_shapes=[pltpu.VMEM((tm + BLOCK, 2 * kw), BF16), pltpu.VMEM((tm, qw), BF16)],
        compiler_params=_cp(("parallel",), VMEM_BIG),
    )(x, qp, kv, kv, bias, sinkt, memkv, wout)


def _mix_b_bwd(dxm, qp, kv, bias, sinkt, memkv, layer, wout, tm, name):
    t, d = dxm.shape
    n_mem = memkv.shape[0]
    mw = N_MEM_HEADS * HEAD_DIM
    qw = d - mw
    kw = N_KV_HEADS * HEAD_DIM
    nb = tm // BLOCK
    nt = t // tm
    rows = GROUP * BLOCK

    def body(dx_ref, q_ref, kv_ref, kvh_ref, bias_ref, sink_ref, mkv_ref, wo_ref,
             dq_ref, dkv_ref, dkvh_ref, dbias_ref, dsink_ref, dmkv_ref,
             kvx, dkvx, dcat_s, dmk_acc, dmv_acc):
        i = pl.program_id(0)

        @pl.when(i == 0)
        def _():
            dbias_ref[...] = jnp.zeros_like(dbias_ref)
            dsink_ref[...] = jnp.zeros_like(dsink_ref)
            dmk_acc[...] = jnp.zeros_like(dmk_acc)
            dmv_acc[...] = jnp.zeros_like(dmv_acc)

        kvx[0:BLOCK, :] = kvh_ref[...]
        kvx[BLOCK:BLOCK + tm, :] = kv_ref[...]
        dkvx[...] = jnp.zeros_like(dkvx)
        dcat_s[...] = lax.dot_general(dx_ref[...].astype(BF16), wo_ref[...], NT,
                                      preferred_element_type=F32).astype(BF16)
        colh = _col_head(kw)
        lane8 = lax.broadcasted_iota(jnp.int32, (8, 128), 1)

        def blk(b, carry):
            r0 = pl.multiple_of(b * BLOCK, BLOCK)
            win = kvx[pl.ds(r0, 2 * BLOCK), :]
            kwin = win[:, 0:kw] * SCALE
            vwin = win[:, kw:2 * kw]
            qs = _stack_members(q_ref, r0, kw)
            dos = _stack_members(dcat_s, r0, kw)
            first = ((i == 0) & (b == 0)).astype(jnp.int32) * N_KV_HEADS
            heads = range(N_KV_HEADS)
            khs = [_keep_head(kwin, colh, h) for h in heads]
            vhs = [_keep_head(vwin, colh, h) for h in heads]
            ss = [lax.dot_general(khs[h], qs, NT, preferred_element_type=F32) for h in heads]
            dps = [lax.dot_general(vhs[h], dos, NT, preferred_element_type=F32) for h in heads]
            dsink = jnp.zeros((8, 128), F32)
            pbs, dsbs = [], []
            for h in heads:
                p, sinkp = _softmax_cols(ss[h] + bias_ref[first + h], sink_ref[h][0:1, :])
                delta = jnp.sum(p * dps[h], axis=0, keepdims=True)
                ds = p * (dps[h] - delta)
                dbias_ref[h] += ds
                sd = sinkp * delta
                for j in range(GROUP):
                    val = -jnp.sum(sd[:, j * BLOCK:(j + 1) * BLOCK], axis=1, keepdims=True)
                    dsink = dsink + jnp.where(lane8 == 4 * j + h, val, 0.0)
                pbs.append(p.astype(BF16))
                dsbs.append(ds.astype(BF16))
            dq = _add4([lax.dot_general(dsbs[h], khs[h], TN, preferred_element_type=F32) for h in heads])
            dk = _add4([jnp.where(colh == h, jnp.dot(dsbs[h], qs, preferred_element_type=F32) * SCALE, 0.0)
                        for h in heads])
            dv = _add4([jnp.where(colh == h, jnp.dot(pbs[h], dos, preferred_element_type=F32), 0.0) for h in heads])
            for j in range(GROUP):
                dq_ref[pl.ds(r0, BLOCK), j * kw:(j + 1) * kw] = dq[j * BLOCK:(j + 1) * BLOCK].astype(BF16)
            dsink_ref[...] += dsink
            dkvx[pl.ds(r0, 2 * BLOCK), :] += jnp.concatenate([dk, dv], axis=1)
            return carry

        for b_static in range(nb):
            blk(b_static, 0)
        dkvh_ref[0] = dkvx[0:BLOCK, :]
        dkv_ref[...] = dkvx[BLOCK:BLOCK + tm, :]

        mkv = mkv_ref[...]
        dqm, dmk, dmv = _mem_attn_bwd(q_ref[:, qw:d], dcat_s[:, qw:d], mkv[:, 0:mw], mkv[:, mw:2 * mw])
        dq_ref[:, qw:d] = dqm.astype(BF16)
        dmk_acc[...] += dmk
        dmv_acc[...] += dmv

        @pl.when(i == nt - 1)
        def _():
            dmkv_ref[...] = jnp.concatenate([dmk_acc[...], dmv_acc[...]], axis=1)

    return pl.pallas_call(
        body, name=name, grid=(nt,),
        in_specs=[_row_spec(tm, d), _row_spec(tm, d), _row_spec(tm, 2 * kw), _halo_prev_spec(BLOCK, 2 * kw, tm),
                  _const_spec((2 * N_KV_HEADS, 2 * BLOCK, rows)), _const_spec((N_KV_HEADS, 8, rows)),
                  pl.BlockSpec((n_mem, 2 * mw), lambda i: (0, layer)), _const_spec((d, d))],
        out_specs=[_row_spec(tm, d), _row_spec(tm, 2 * kw),
                   pl.BlockSpec((1, BLOCK, 2 * kw), lambda i: (i, 0, 0)),
                   pl.BlockSpec((N_KV_HEADS, 2 * BLOCK, rows), lambda i: (0, 0, 0)),
                   pl.BlockSpec((8, 128), lambda i: (0, 0)),
                   pl.BlockSpec((n_mem, 2 * mw), lambda i: (0, 0))],
        out_shape=[jax.ShapeDtypeStruct((t, d), BF16),
                   jax.ShapeDtypeStruct((t, 2 * kw), F32), jax.ShapeDtypeStruct((nt, BLOCK, 2 * kw), F32),
                   jax.ShapeDtypeStruct((N_KV_HEADS, 2 * BLOCK, rows), F32), jax.ShapeDtypeStruct((8, 128), F32),
                   jax.ShapeDtypeStruct((n_mem, 2 * mw), F32)],
        scratch_shapes=[pltpu.VMEM((tm + BLOCK, 2 * kw), BF16), pltpu.VMEM((tm + BLOCK, 2 * kw), F32),
                        pltpu.VMEM((tm, d), BF16), pltpu.VMEM((n_mem, mw), F32), pltpu.VMEM((n_mem, mw), F32)],
        compiler_params=_cp(("arbitrary",), VMEM_BIG),
    )(dxm, qp, kv, kv, bias, sinkt, memkv, wout)


def _adam_math(w, g, m, v):
    m2 = ADAM_B1 * m + (1.0 - ADAM_B1) * g
    v2 = ADAM_B2 * v + (1.0 - ADAM_B2) * (g * g)
    m_hat = m2 / (1.0 - ADAM_B1 ** ADAM_STEP)
    v_hat = v2 / (1.0 - ADAM_B2 ** ADAM_STEP)
    delta = -ADAM_LR * (m_hat / (jnp.sqrt(v_hat) + ADAM_EPS) + ADAM_WD * w)
    return delta, m2, v2


def _adamw_sharded(w, land, m, v, name):
    nl, r, c = w.shape
    tr = max(cand for cand in range(16, r + 1, 16) if r % cand == 0 and cand * c <= 512 * 1024)

    def body(w_ref, a_ref, m_ref, v_ref, g_ref, d_ref, mo_ref, vo_ref):
        g = a_ref[0, 0].astype(F32) + a_ref[1, 0].astype(F32)
        for k in range(1, N_CHIPS):
            g = g + (a_ref[2 * k, 0].astype(F32) + a_ref[2 * k + 1, 0].astype(F32))
        delta, m2, v2 = _adam_math(w_ref[0], g, m_ref[0], v_ref[0])
        g_ref[0] = g
        d_ref[0] = delta
        mo_ref[0] = m2
        vo_ref[0] = v2

    rs = pl.BlockSpec((1, tr, c), lambda l, i: (l, i, 0))
    ps = pl.BlockSpec((2 * N_CHIPS, 1, tr, c), lambda l, i: (0, l, i, 0))
    sd = jax.ShapeDtypeStruct((nl, r, c), F32)
    return pl.pallas_call(
        body, name=name, grid=(nl, r // tr),
        in_specs=[rs, ps, rs, rs], out_specs=[rs, rs, rs, rs], out_shape=[sd, sd, sd, sd],
        compiler_params=_cp(("parallel", "parallel"), VMEM_BIG),
    )(w, land, m, v)


def _adamw_packed(w, g, m, v, name):
    def body(w_ref, g_ref, m_ref, v_ref, d_ref, mo_ref, vo_ref):
        delta, m2, v2 = _adam_math(w_ref[...], g_ref[...], m_ref[...], v_ref[...])
        d_ref[...] = delta
        mo_ref[...] = m2
        vo_ref[...] = v2

    vm = pl.BlockSpec(memory_space=pltpu.VMEM)
    sd = jax.ShapeDtypeStruct(w.shape, F32)
    return pl.pallas_call(body, name=name, in_specs=[vm] * 4, out_specs=[vm] * 3, out_shape=[sd] * 3)(w, g, m, v)


def _place():
    return lax.axis_index("x"), lax.axis_index("y"), lax.axis_index("c")


def _hbm(a):
    return pltpu.with_memory_space_constraint(a, pltpu.HBM)


def _peers(x, y, c, both_cores):
    chips = [(1 - x, y), (x, 1 - y), (1 - x, 1 - y)]
    if not both_cores:
        return [(px, py, c) for px, py in chips]
    return [(px, py, pc) for px, py in chips for pc in (c, 1 - c)] + [(x, y, 1 - c)]


def _chip_copy(src, land, gather, layer, chip_src, slot, send_sem, recv_sem, peer):
    s = src if gather else src.at[chip_src]
    d = land.at[slot] if layer is None else land.at[slot, layer]
    return pltpu.make_async_remote_copy(src_ref=s, dst_ref=d, send_sem=send_sem, recv_sem=recv_sem,
                                        device_id=peer, device_id_type=MESH)


def _own_copy(src, land, gather, layer, chip, slot, own_sem):
    s = src if gather else src.at[chip]
    d = land.at[slot] if layer is None else land.at[slot, layer]
    return pltpu.make_async_copy(s, d, own_sem)


def _exchange_start(srcs, lands, gather, layers, both_cores, after, name):
    n = len(srcs)
    npeer = 7 if both_cores else 3
    hbm = pl.BlockSpec(memory_space=pltpu.HBM)
    sem = pl.BlockSpec(memory_space=pltpu.SEMAPHORE)

    def body(*refs):
        ins, lds = refs[:n], refs[n:2 * n]
        first_out = 2 * n + len(after)
        send_sems, recv_sems, own_sems, token = refs[first_out], refs[first_out + 1], refs[first_out + 2], refs[-1]
        x, y, c = _place()
        slot = 2 * x + y if gather is True else 2 * (2 * x + y) + c
        for t in range(n):
            _own_copy(ins[t], lds[t], gather, layers[t], 2 * x + y, slot, own_sems.at[t]).start()
            for r, peer in enumerate(_peers(x, y, c, both_cores)):
                _chip_copy(ins[t], lds[t], gather, layers[t], 2 * peer[0] + peer[1], slot,
                           send_sems.at[npeer * t + r], recv_sems.at[npeer * t + r], peer).start()
        token[...] = jnp.zeros_like(token)

    both = list(srcs) + list(lands)
    outs = pl.pallas_call(
        body, name=name, in_specs=[hbm] * (2 * n) + [pl.BlockSpec(memory_space=pl.ANY)] * len(after),
        out_specs=(sem, sem, sem, *([hbm] * (2 * n)), pl.BlockSpec(memory_space=pltpu.VMEM)),
        out_shape=(pltpu.SemaphoreType.DMA((npeer * n,)), pltpu.SemaphoreType.DMA((npeer * n,)),
                   pltpu.SemaphoreType.DMA((n,)),
                   *[pltpu.HBM(a.shape, a.dtype) for a in both], jax.ShapeDtypeStruct((8, 128), F32)),
        input_output_aliases={t: 3 + t for t in range(2 * n)},
        compiler_params=_cp(has_side_effects=pltpu.SideEffectType.DATAFLOW_SIDE_EFFECTING),
    )(*[_hbm(a) for a in both], *after)
    return dict(send=outs[0], recv=outs[1], own=outs[2], srcs=list(outs[3:3 + n]), lands=list(outs[3 + n:3 + 2 * n]),
                token=outs[-1], gather=gather, layers=list(layers), both_cores=both_cores)


def _exchange_wait(groups, lands, land_ids, after, name):
    flat = [s for g in groups for s in g["srcs"]]
    ns, nl, ng, na = len(flat), len(lands), len(groups), len(after)
    hbm = pl.BlockSpec(memory_space=pltpu.HBM)
    sem = pl.BlockSpec(memory_space=pltpu.SEMAPHORE)

    def body(*refs):
        srcs, lds = refs[:ns], refs[ns:ns + nl]
        sems = refs[ns + nl:ns + nl + 3 * ng]
        x, y, c = _place()
        k = 0
        for gi, g in enumerate(groups):
            peers = _peers(x, y, c, g["both_cores"])
            for t in range(len(g["srcs"])):
                _own_copy(srcs[k], lds[land_ids[gi][t]], g["gather"], g["layers"][t], 0, 0, sems[3 * gi + 2].at[t]).wait()
                for r, peer in enumerate(peers):
                    cp = _chip_copy(srcs[k], lds[land_ids[gi][t]], g["gather"], g["layers"][t], 0, 0,
                                    sems[3 * gi].at[len(peers) * t + r], sems[3 * gi + 1].at[len(peers) * t + r], peer)
                    cp.wait_send()
                    cp.wait_recv()
                k += 1

    both = flat + list(lands)
    sem_args = [a for g in groups for a in (g["send"], g["recv"], g["own"])]
    outs = pl.pallas_call(
        body, name=name,
        in_specs=[hbm] * (ns + nl) + [sem] * (3 * ng) + [pl.BlockSpec(memory_space=pl.ANY)] * na,
        out_specs=[hbm] * (ns + nl),
        out_shape=[pltpu.HBM(a.shape, a.dtype) for a in both],
        input_output_aliases={t: t for t in range(ns + nl)},
        compiler_params=_cp(has_side_effects=pltpu.SideEffectType.DATAFLOW_SIDE_EFFECTING),
    )(*both, *sem_args, *after)
    return list(outs[ns:])


def _core_fill(lands, layers, name):
    n = len(lands)
    hbm = pl.BlockSpec(memory_space=pltpu.HBM)

    def body(*refs):
        ins = refs[:n]
        send_sems, recv_sems = refs[2 * n:]
        x, y, c = _place()
        copies = []
        for t in range(n):
            for k in range(N_CHIPS):
                mine = ins[t].at[2 * k + c] if layers[t] is None else ins[t].at[2 * k + c, layers[t]]
                cp = pltpu.make_async_remote_copy(
                    src_ref=mine, dst_ref=mine, send_sem=send_sems.at[N_CHIPS * t + k],
                    recv_sem=recv_sems.at[N_CHIPS * t + k], device_id=(x, y, 1 - c), device_id_type=MESH)
                cp.start()
                copies.append(cp)
        for cp in copies:
            cp.wait()

    return pl.pallas_call(
        body, name=name, in_specs=[hbm] * n, out_specs=[hbm] * n,
        out_shape=[jax.ShapeDtypeStruct(a.shape, a.dtype) for a in lands],
        input_output_aliases={t: t for t in range(n)},
        scratch_shapes=[pltpu.SemaphoreType.DMA((N_CHIPS * n,)), pltpu.SemaphoreType.DMA((N_CHIPS * n,))],
        compiler_params=_cp(has_side_effects=True),
    )(*lands)


def _sum_slots(land, name):
    _, r, c = land.shape
    vm = pl.BlockSpec(memory_space=pltpu.VMEM)

    def body(l_ref, sum_ref):
        total = l_ref[0]
        for k in range(1, 8):
            total = total + l_ref[k]
        sum_ref[...] = total

    return pl.pallas_call(body, name=name, in_specs=[vm], out_specs=vm, out_shape=jax.ShapeDtypeStruct((r, c), F32))(land)


def _pack(items):
    rows = []
    for a in items:
        flat = a.astype(F32).reshape(-1)
        pad = (-flat.shape[0]) % PACK_W
        rows.append(jnp.pad(flat, (0, pad)).reshape(-1, PACK_W))
    out = jnp.concatenate(rows, axis=0)
    pad_r = (-out.shape[0]) % 8
    return jnp.pad(out, ((0, pad_r), (0, 0)))


def _unpack(pack, shapes):
    outs, row = [], 0
    for s in shapes:
        n = int(np.prod(s))
        nr = -(-n // PACK_W)
        outs.append(pack[row:row + nr].reshape(-1)[:n].reshape(s))
        row += nr
    return outs


def _heads_to_member_major(w, axis):
    shp = w.shape
    pre, post = shp[:axis], shp[axis + 1:]
    w4 = w.reshape(pre + (N_KV_HEADS, GROUP, HEAD_DIM) + post)
    w4 = jnp.swapaxes(w4, len(pre), len(pre) + 1)
    return w4.reshape(shp)


def _heads_to_kv_major(w, axis):
    shp = w.shape
    pre, post = shp[:axis], shp[axis + 1:]
    w4 = w.reshape(pre + (GROUP, N_KV_HEADS, HEAD_DIM) + post)
    w4 = jnp.swapaxes(w4, len(pre), len(pre) + 1)
    return w4.reshape(shp)


def kernel(x, mem, norm_mix, norm_ffn, a_w_in, a_conv_w, a_w_out, kv_norm, w_kv, b_w_q, b_sinks, b_w_out, rel_bias, mem_norm, w_mem_kv, w_gate, w_up, w_down, final_norm, loss_target, m_norm_mix, m_norm_ffn, m_a_w_in, m_a_conv_w, m_a_w_out, m_kv_norm, m_w_kv, m_b_w_q, m_b_sinks, m_b_w_out, m_rel_bias, m_mem_norm, m_w_mem_kv, m_w_gate, m_w_up, m_w_down, m_final_norm, v_norm_mix, v_norm_ffn, v_a_w_in, v_a_conv_w, v_a_w_out, v_kv_norm, v_w_kv, v_b_w_q, v_b_sinks, v_b_w_out, v_rel_bias, v_mem_norm, v_w_mem_kv, v_w_gate, v_w_up, v_w_down, v_final_norm):
    t, d = x.shape[1], x.shape[2]
    tm = 512 if t % 512 == 0 and t >= 2048 else 256
    tl = 2 * tm if t % (2 * tm) == 0 else tm
    x0 = x.reshape(t, d)
    target = loss_target.reshape(t, d)
    mem2 = mem.reshape(mem.shape[1], d)
    n_mem = mem2.shape[0]
    ax, ay, ac = _place()
    chip = 2 * ax + ay
    cwid = a_conv_w.shape[2] * N_CHIPS
    qw = N_Q_HEADS * HEAD_DIM
    nq = N_CHIPS

    def landing(piece):
        return lax.empty((nq,) + piece.shape, piece.dtype)

    def mixer_shards(i):
        if i < N_A:
            shards = [a_w_in[i], a_w_out[i]] + ([w_mem_kv] if i == 0 else [])
        else:
            j = i - N_A
            shards = [b_w_q[j], b_w_out[j]] + ([w_kv] if j == 0 else [])
        return [a.astype(BF16) for a in shards]

    def ffn_shards(i):
        return [w_gate[i].T.astype(BF16), w_up[i].T.astype(BF16), w_down[i].astype(BF16)]

    conv_pad = jnp.pad(a_conv_w, ((0, 0), (0, 8 - a_conv_w.shape[1]), (0, (-a_conv_w.shape[2]) % 128)))
    first = mixer_shards(0)
    group_shards = {"0a": first[0:1], "0b": first[1:] + [conv_pad], "0f": ffn_shards(0)}
    for i in range(1, DEPTH):
        group_shards[str(i)] = ffn_shards(i) + mixer_shards(i)
    gathers, prev_tok = {}, []

    def start_group(key, after):
        shards = group_shards[key]
        gathers[key] = _exchange_start(shards, [landing(a) for a in shards], True, [None] * len(shards), False,
                                       after, "gather_start_" + key)
        return [gathers[key]["token"]]

    for key in ("0a", "0b", "0f"):
        prev_tok = start_group(key, prev_tok)

    def rows_full(g):
        return g.reshape((-1,) + g.shape[2:])

    def cols_full(g):
        return jnp.transpose(g, (1, 0, 2)).reshape(g.shape[1], -1)

    def landed_weights(key, after):
        g = gathers[key]
        return _exchange_wait([g], g["lands"], [list(range(len(g["lands"])))], after, "gather_wait_" + key)

    def mixer_weights(i, got):
        w_first, w_out = (cols_full(got[0]) if i < N_A else rows_full(got[0])), rows_full(got[1])
        if i >= N_A:
            w_first = jnp.concatenate([_heads_to_member_major(w_first[:, :qw], 1), w_first[:, qw:]], axis=1)
            w_out = jnp.concatenate([_heads_to_member_major(w_out[:qw, :], 0), w_out[qw:, :]], axis=0)
        return dict(w_first=w_first, w_out=w_out, extra=got[2] if len(got) > 2 else None)

    def ffn_weights(got):
        return dict(wg=rows_full(got[0]), wu=rows_full(got[1]), wd=rows_full(got[2]))

    bias, sinkt = [], []
    for j in range(2):
        bj, sj = _bias_tables(rel_bias, b_sinks[j], "bias_tables")
        bias.append(bj)
        sinkt.append(sj)

    ws = []
    xs, xmids, projs, cats, gates, ups = [x0], [], [], [], [], []
    kv = memkv = wmem = wkv = None
    for i in range(DEPTH):
        xin = xs[-1]
        if i == 0:
            w = dict(w_first=cols_full(landed_weights("0a", prev_tok)[0]))
        else:
            got = landed_weights(str(i), [xin])
            w = dict(mixer_weights(i, got[3:]), **ffn_weights(got[0:3]))
        ws.append(w)
        gm = norm_mix[i].reshape(1, d)
        if i < N_A:
            proj = _norm_mm(xin, gm, w["w_first"], tl, "proj_a")
            if i == 0:
                for key in ("1", "2", "3"):
                    prev_tok = start_group(key, prev_tok + [proj])
                got = landed_weights("0b", prev_tok)
                w["w_out"] = rows_full(got[0])
                full_mem = jnp.swapaxes(got[1], 0, 1).reshape(DEPTH, d, -1)
                wmem = jnp.transpose(full_mem, (1, 0, 2)).reshape(d, -1)
                memkv = _norm_mm(mem2, mem_norm.reshape(1, d), wmem, n_mem, "mem_kv")
                taps = got[2][:, :, 0:3, 0:a_conv_w.shape[2]]
                conv_full = jnp.transpose(taps, (1, 2, 0, 3)).reshape(N_A, 3, cwid)
            xmid, cat = _mix_a_fwd(xin, proj, conv_full[i], memkv, i, w["w_out"], tm, "mix_a_fwd")
        else:
            j = i - N_A
            if j == 0:
                wkv = rows_full(w["extra"])
                proj, kv = _norm_mm_pair(xin, gm, w["w_first"], kv_norm.reshape(1, d), wkv, tl, "proj_b_kv")
            else:
                proj = _norm_mm(xin, gm, w["w_first"], tl, "proj_b")
            xmid, cat = _mix_b_fwd(xin, proj, kv, bias[j], sinkt[j], memkv, i, w["w_out"], tm, "mix_b_fwd")
        if i == 0:
            w.update(ffn_weights(landed_weights("0f", [xmid])))
        if i < DEPTH - 1:
            xout, gate, up = _ffn_fwd(xmid, norm_ffn[i].reshape(1, d), w["wg"], w["wu"], w["wd"], tm, "ffn_fwd")
        else:
            xout = None
            gate, up, loss_part, dx, dg_final = _ffn_fwd_loss(xmid, norm_ffn[i].reshape(1, d), w["wg"], w["wu"], w["wd"],
                                                              final_norm.reshape(1, d), target, tm, "ffn_fwd_loss")
        projs.append(proj)
        cats.append(cat)
        xmids.append(xmid)
        gates.append(gate)
        ups.append(up)
        xs.append(xout)

    def rows_pieces(g):
        return g.astype(BF16).reshape((nq, g.shape[0] // nq) + g.shape[1:])

    def cols_pieces(g):
        return jnp.transpose(g.astype(BF16).reshape(g.shape[0], nq, g.shape[1] // nq), (1, 0, 2))

    swapped = ("w_gate", "w_up")
    stacked = dict(a_w_in=a_w_in, a_w_out=a_w_out, w_kv=w_kv[None], b_w_q=b_w_q, b_w_out=b_w_out,
                   w_mem_kv=w_mem_kv, w_gate=jnp.swapaxes(w_gate, 1, 2), w_up=jnp.swapaxes(w_up, 1, 2), w_down=w_down)
    names = list(stacked)
    land = {k: lax.empty((2 * nq,) + stacked[k].shape, BF16) for k in names}
    scatters, scatter_ids = [], []

    def scatter_start(key, items, both_cores):
        keys = [k for k, _, _ in items]
        st = _exchange_start([p for _, _, p in items], [land[k] for k in keys], False, [l for _, l, _ in items],
                             both_cores, [], "scatter_start_" + key)
        for k, ld in zip(keys, st["lands"]):
            land[k] = ld
        scatters.append(st)
        scatter_ids.append([names.index(k) for k in keys])
        return st["token"][0:1, 0:1]

    g_norm_mix, g_norm_ffn = [None] * DEPTH, [None] * DEPTH
    g_conv, g_sinks = [None] * 2, [None] * 2
    dmemkv = [None] * DEPTH
    dbias, dkv_main, dkv_halo = [None] * 2, [None] * 2, [None] * 2
    g_kv_norm = None
    tok = jnp.zeros((1, 1), F32)
    for i in reversed(range(DEPTH)):
        w = ws[i]
        dxm, dgate, dup, act, h2, dgf = _ffn_bwd(dx, xmids[i], norm_ffn[i].reshape(1, d) + tok, gates[i], ups[i],
                                            w["wg"], w["wu"], w["wd"], tm // 2, "ffn_bwd")
        g_norm_ffn[i] = dgf
        g_wd = _wgrad(act, dx, 2 * tm, "wgrad_down")
        g_wg = _wgrad(dgate, h2, 2 * tm, "wgrad_gate")
        g_wu = _wgrad(dup, h2, 2 * tm, "wgrad_up")
        items = [("w_gate", i, rows_pieces(g_wg)), ("w_up", i, rows_pieces(g_wu)), ("w_down", i, rows_pieces(g_wd))]
        if i == 0:
            tok = scatter_start("0f", items, True)
            items = []
        gm = norm_mix[i].reshape(1, d)
        if i < N_A:
            dproj, dcw, dmemkv[i] = _mix_a_bwd(dxm, projs[i], conv_full[i] + (tok if i == 0 else 0.0), memkv, i,
                                                    w["w_out"], tm, "mix_a_bwd")
            g_conv[i] = dcw[0:3]
            g_out = _wgrad(cats[i], dxm, 2 * tm, "wgrad_out")
            if i == 0:
                dmemkv_all = jnp.concatenate([a.astype(BF16) for a in dmemkv], axis=1)
                _, g_mem_norm, hmem = _mm_nt_normbwd(dmemkv_all, wmem, mem2, mem_norm.reshape(1, d),
                                                     jnp.zeros((n_mem, d), F32), n_mem, "mem_kv_bwd")
                g_wmem = _wgrad(hmem, dmemkv_all, n_mem, "wgrad_mem")
                g_wmem = jnp.transpose(g_wmem.reshape(nq, d // nq, DEPTH, -1), (0, 2, 1, 3))
                gm = gm + scatter_start("0o", [("a_w_out", 0, rows_pieces(g_out)), ("w_mem_kv", None, g_wmem)], False)
            dx, g_norm_mix[i], h = _mm_nt_normbwd(dproj, w["w_first"], xs[i], gm, dxm, tl, "proj_a_bwd")
            g_in = _wgrad(h, dproj, 2 * tm, "wgrad_in_a")
            items.append(("a_w_in", i, cols_pieces(g_in)))
            if i > 0:
                items.append(("a_w_out", i, rows_pieces(g_out)))
        else:
            j = i - N_A
            dqp, dkv_main[j], dkv_halo[j], dbias[j], dsk, dmemkv[i] = _mix_b_bwd(
                dxm, projs[i], kv, bias[j], sinkt[j], memkv, i, w["w_out"], tm, "mix_b_bwd")
            g_sinks[j] = dsk[0, 0:N_Q_HEADS].reshape(GROUP, N_KV_HEADS).T.reshape(N_Q_HEADS)
            g_out = _wgrad(cats[i], dxm, 2 * tm, "wgrad_out")
            if j == 0:
                dx, g_norm_mix[i], g_kv_norm, h, hkv, dkv = _mm_nt_normbwd_pair(
                    dqp, w["w_first"], gm, (dkv_main[0], dkv_halo[0], dkv_main[1], dkv_halo[1]), wkv,
                    kv_norm.reshape(1, d), xs[i], dxm, tm, "proj_b_kv_bwd")
            else:
                dx, g_norm_mix[i], h = _mm_nt_normbwd(dqp, w["w_first"], xs[i], gm, dxm, tl, "proj_b_bwd")
            g_q = _wgrad(h, dqp, 2 * tm, "wgrad_in_b")
            g_q = jnp.concatenate([_heads_to_kv_major(g_q[:, :qw], 1), g_q[:, qw:]], axis=1)
            g_out = jnp.concatenate([_heads_to_kv_major(g_out[:qw, :], 0), g_out[qw:, :]], axis=0)
            items += [("b_w_q", j, rows_pieces(g_q)), ("b_w_out", j, rows_pieces(g_out))]
            if j == 0:
                items.append(("w_kv", 0, rows_pieces(_wgrad(hkv, dkv, 2 * tm, "wgrad_kv"))))
        tok = scatter_start(str(i) if i else "0i", items, i > 0)
    grad_x = dx.reshape(x.shape)
    g_rel = _bias_bwd(dbias[0], dbias[1], "bias_bwd")[:, 0:N_Q_HEADS]

    small_shapes = [(DEPTH, d), (DEPTH, d), (d,), (d,), (d,), (2, N_Q_HEADS), (REL_BUCKETS, N_Q_HEADS),
                    (N_A, 3, cwid), ()]
    small = _pack([jnp.concatenate(g_norm_mix, axis=0), jnp.concatenate(g_norm_ffn, axis=0), g_kv_norm, g_mem_norm,
                   dg_final, jnp.stack(g_sinks), g_rel, jnp.stack(g_conv), loss_part[0, 0]])
    small_grp = _exchange_start([small], [lax.empty((8,) + small.shape, F32)], "all", [None], True,
                                [scatters[-1]["token"]], "reduce_small_start")

    late = ("a_w_in", "a_w_out", "w_mem_kv")
    landed = dict(zip(names, _exchange_wait(scatters[:-2], [land[k] for k in names], scatter_ids[:-2],
                                            [small_grp["token"]], "scatter_wait_a")))

    weights = dict(norm_mix=norm_mix, norm_ffn=norm_ffn, a_w_in=a_w_in, a_conv_w=a_conv_w, a_w_out=a_w_out,
                   kv_norm=kv_norm, w_kv=w_kv, b_w_q=b_w_q, b_sinks=b_sinks, b_w_out=b_w_out, rel_bias=rel_bias,
                   mem_norm=mem_norm, w_mem_kv=w_mem_kv, w_gate=w_gate, w_up=w_up, w_down=w_down,
                   final_norm=final_norm)
    moms = dict(norm_mix=m_norm_mix, norm_ffn=m_norm_ffn, a_w_in=m_a_w_in, a_conv_w=m_a_conv_w, a_w_out=m_a_w_out,
                kv_norm=m_kv_norm, w_kv=m_w_kv, b_w_q=m_b_w_q, b_sinks=m_b_sinks, b_w_out=m_b_w_out,
                rel_bias=m_rel_bias, mem_norm=m_mem_norm, w_mem_kv=m_w_mem_kv, w_gate=m_w_gate, w_up=m_w_up,
                w_down=m_w_down, final_norm=m_final_norm)
    vars_ = dict(norm_mix=v_norm_mix, norm_ffn=v_norm_ffn, a_w_in=v_a_w_in, a_conv_w=v_a_conv_w, a_w_out=v_a_w_out,
                 kv_norm=v_kv_norm, w_kv=v_w_kv, b_w_q=v_b_w_q, b_sinks=v_b_sinks, b_w_out=v_b_w_out,
                 rel_bias=v_rel_bias, mem_norm=v_mem_norm, w_mem_kv=v_w_mem_kv, w_gate=v_w_gate, w_up=v_w_up,
                 w_down=v_w_down, final_norm=v_final_norm)
    order = list(weights)
    grads, deltas, new_m, new_v = {}, {}, {}, {}
    def adamw(k, ld):
        shp, stk = weights[k].shape, ld.shape[1:]
        view = (lambda a: jnp.swapaxes(a, 1, 2)) if k in swapped else (lambda a: a.reshape(stk))
        back = (lambda a: jnp.swapaxes(a, 1, 2)) if k in swapped else (lambda a: a.reshape(shp))
        outs = _adamw_sharded(view(weights[k]), ld, view(moms[k]), view(vars_[k]), "adamw_" + k)
        grads[k], deltas[k], new_m[k], new_v[k] = [back(o) for o in outs]

    for k in names:
        if k not in late:
            adamw(k, landed[k])
    late_ids = [[late.index(names[t]) for t in ids] for ids in scatter_ids[-2:]]
    late_landed = _exchange_wait(scatters[-2:] + [small_grp], [landed[k] for k in late] + small_grp["lands"],
                                 late_ids + [[len(late)]], [deltas["w_down"]], "scatter_wait_b")
    filled = _core_fill(late_landed[:-1], [0, 0, None], "fill_cores")
    for k, ld in zip(late, filled):
        adamw(k, ld)
    small_sum = _sum_slots(late_landed[-1], "reduce_small_sum")
    (gs_norm_mix, gs_norm_ffn, gs_kv_norm, gs_mem_norm, gs_final, gs_sinks, gs_rel, gs_conv_full, loss) = _unpack(
        small_sum, small_shapes)
    cq = cwid // N_CHIPS
    gs_conv = lax.dynamic_slice_in_dim(gs_conv_full, chip * cq, cq, axis=2)
    small_names = ["norm_mix", "norm_ffn", "kv_norm", "mem_norm", "final_norm", "b_sinks", "rel_bias", "a_conv_w"]
    small_g = [gs_norm_mix, gs_norm_ffn, gs_kv_norm, gs_mem_norm, gs_final, gs_sinks, gs_rel, gs_conv]
    shapes = [weights[k].shape for k in small_names]
    dl_p, m_p, v_p = _adamw_packed(_pack([weights[k] for k in small_names]), _pack(small_g),
                                   _pack([moms[k] for k in small_names]), _pack([vars_[k] for k in small_names]),
                                   "adamw_small")
    for k, g, dl, m2, v2 in zip(small_names, small_g, _unpack(dl_p, shapes), _unpack(m_p, shapes), _unpack(v_p, shapes)):
        grads[k], deltas[k], new_m[k], new_v[k] = g.reshape(weights[k].shape), dl, m2, v2

    return (loss, grad_x, *[grads[k] for k in order], *[deltas[k] for k in order],
            *[new_m[k] for k in order], *[new_v[k] for k in order])
```

```python
import functools
import math

import numpy as np
import jax
import jax.numpy as jnp
from jax import lax
from jax.experimental import pallas as pl
from jax.experimental.pallas import tpu as pltpu

F32 = jnp.float32
BF16 = jnp.bfloat16
MESH = pl.DeviceIdType.MESH

EPS = 1e-5
HEAD_DIM = 64
N_MEM_HEADS = 4
N_KV_HEADS = 4
GROUP = 3
N_Q_HEADS = N_KV_HEADS * GROUP
BLOCK = 128
REL_BUCKETS = 32
REL_MAX_DIST = 128
SCALE = HEAD_DIM ** -0.5
NEG = -1e30
N_CHIPS = 4
N_A = 2
DEPTH = 4

ADAM_LR = 0.001
ADAM_B1 = 0.9
ADAM_B2 = 0.999
ADAM_EPS = 1e-08
ADAM_WD = 0.01
ADAM_STEP = 10

VMEM_BIG = 56 * 1024 * 1024
PACK_W = 1024

NT = (((1,), (1,)), ((), ()))
TN = (((0,), (0,)), ((), ()))


def _cp(sem=None, vmem=None, **kw):
    return pltpu.CompilerParams(dimension_semantics=sem, vmem_limit_bytes=vmem, **kw)


def _const_spec(shape):
    nd = len(shape)
    return pl.BlockSpec(shape, lambda i, _n=nd: (0,) * _n, pipeline_mode=pl.Buffered(1))


def _row_spec(tm, n):
    return pl.BlockSpec((tm, n), lambda i: (i, 0))


def _rms_parts(xv):
    r = lax.rsqrt(jnp.mean(xv * xv, axis=-1, keepdims=True) + EPS)
    return xv * r, r


def _sigmoid(z):
    return 1.0 / (1.0 + jnp.exp(-z))


def _ff_chunks(f):
    if f % 512 == 0 or f % 256 != 0:
        return [(0, f)] if f <= 1536 else [(0, f // 2), (f // 2, f - f // 2)]
    n = f // 256
    a = (n + 1) // 2 * 256
    return [(0, a), (a, f - a)]


def _norm_mm(x, g, w, tm, name):
    t, d = x.shape
    n = w.shape[1]

    def body(x_ref, g_ref, w_ref, o_ref):
        xhat, _ = _rms_parts(x_ref[...])
        h = (xhat * g_ref[...]).astype(BF16)
        o_ref[...] = jnp.dot(h, w_ref[...], preferred_element_type=F32).astype(BF16)

    return pl.pallas_call(
        body, name=name, grid=(t // tm,),
        in_specs=[_row_spec(tm, d), _const_spec((1, d)), _const_spec((d, n))],
        out_specs=_row_spec(tm, n),
        out_shape=jax.ShapeDtypeStruct((t, n), BF16),
        compiler_params=_cp(("parallel",), VMEM_BIG),
    )(x, g, w)


def _mm_nt_normbwd(dproj, w, x_in, g, dres, tm, name):
    t, d = x_in.shape
    n = w.shape[1]

    def body(dp_ref, w_ref, x_ref, g_ref, dr_ref, dx_ref, dg_ref, h_ref):
        i = pl.program_id(0)
        xhat, r = _rms_parts(x_ref[...])
        gv = g_ref[...]
        h_ref[...] = (xhat * gv).astype(BF16)
        dh = lax.dot_general(dp_ref[...], w_ref[...], NT, preferred_element_type=F32)
        dxhat = dh * gv
        dx = r * (dxhat - xhat * jnp.mean(dxhat * xhat, axis=-1, keepdims=True))
        dx_ref[...] = dr_ref[...] + dx

        @pl.when(i == 0)
        def _():
            dg_ref[...] = jnp.zeros_like(dg_ref)

        dg_ref[...] += jnp.sum(dh * xhat, axis=0, keepdims=True)

    return pl.pallas_call(
        body, name=name, grid=(t // tm,),
        in_specs=[_row_spec(tm, n), _const_spec((d, n)), _row_spec(tm, d), _const_spec((1, d)), _row_spec(tm, d)],
        out_specs=[_row_spec(tm, d), pl.BlockSpec((1, d), lambda i: (0, 0)), _row_spec(tm, d)],
        out_shape=[jax.ShapeDtypeStruct((t, d), F32), jax.ShapeDtypeStruct((1, d), F32),
                   jax.ShapeDtypeStruct((t, d), BF16)],
        compiler_params=_cp(("arbitrary",), VMEM_BIG),
    )(dproj, w, x_in, g, dres)


def _norm_mm_pair(x, g_a, w_a, g_b, w_b, tm, name):
    t, d = x.shape
    na, nb = w_a.shape[1], w_b.shape[1]

    def body(x_ref, ga_ref, wa_ref, gb_ref, wb_ref, oa_ref, ob_ref):
        xhat, _ = _rms_parts(x_ref[...])
        ha = (xhat * ga_ref[...]).astype(BF16)
        hb = (xhat * gb_ref[...]).astype(BF16)
        oa_ref[...] = jnp.dot(ha, wa_ref[...], preferred_element_type=F32).astype(BF16)
        ob_ref[...] = jnp.dot(hb, wb_ref[...], preferred_element_type=F32).astype(BF16)

    return pl.pallas_call(
        body, name=name, grid=(t // tm,),
        in_specs=[_row_spec(tm, d), _const_spec((1, d)), _const_spec((d, na)), _const_spec((1, d)), _const_spec((d, nb))],
        out_specs=[_row_spec(tm, na), _row_spec(tm, nb)],
        out_shape=[jax.ShapeDtypeStruct((t, na), BF16), jax.ShapeDtypeStruct((t, nb), BF16)],
        compiler_params=_cp(("parallel",), VMEM_BIG),
    )(x, g_a, w_a, g_b, w_b)


def _mm_nt_normbwd_pair(dp_a, w_a, g_a, kv_parts, w_b, g_b, x_in, dres, tm, name):
    t, d = x_in.shape
    na, nb = w_a.shape[1], w_b.shape[1]
    nt = t // tm
    main_1, halo_1, main_2, halo_2 = kv_parts

    def body(dpa_ref, wa_ref, ga_ref, m1_ref, h1_ref, m2_ref, h2_ref, wb_ref, gb_ref, x_ref, dr_ref,
             dx_ref, dga_ref, dgb_ref, ha_ref, hb_ref, dkv_ref):
        i = pl.program_id(0)
        s = m1_ref[...] + m2_ref[...]
        tail = jnp.where(i == nt - 1, 0.0, h1_ref[0] + h2_ref[0])
        dkv = jnp.concatenate([s[0:tm - BLOCK], s[tm - BLOCK:] + tail], axis=0).astype(BF16)
        dkv_ref[...] = dkv
        xhat, r = _rms_parts(x_ref[...])
        ga, gb = ga_ref[...], gb_ref[...]
        ha_ref[...] = (xhat * ga).astype(BF16)
        hb_ref[...] = (xhat * gb).astype(BF16)
        dha = lax.dot_general(dpa_ref[...], wa_ref[...], NT, preferred_element_type=F32)
        dhb = lax.dot_general(dkv, wb_ref[...], NT, preferred_element_type=F32)
        dxhat = dha * ga + dhb * gb
        dx_ref[...] = dr_ref[...] + r * (dxhat - xhat * jnp.mean(dxhat * xhat, axis=-1, keepdims=True))

        @pl.when(i == 0)
        def _():
            dga_ref[...] = jnp.zeros_like(dga_ref)
            dgb_ref[...] = jnp.zeros_like(dgb_ref)

        dga_ref[...] += jnp.sum(dha * xhat, axis=0, keepdims=True)
        dgb_ref[...] += jnp.sum(dhb * xhat, axis=0, keepdims=True)

    halo_spec = pl.BlockSpec((1, BLOCK, nb), lambda i: (jnp.minimum(i + 1, nt - 1), 0, 0))
    row1 = pl.BlockSpec((1, d), lambda i: (0, 0))
    return pl.pallas_call(
        body, name=name, grid=(nt,),
        in_specs=[_row_spec(tm, na), _const_spec((d, na)), _const_spec((1, d)), _row_spec(tm, nb), halo_spec,
                  _row_spec(tm, nb), halo_spec, _const_spec((d, nb)), _const_spec((1, d)), _row_spec(tm, d),
                  _row_spec(tm, d)],
        out_specs=[_row_spec(tm, d), row1, row1, _row_spec(tm, d), _row_spec(tm, d), _row_spec(tm, nb)],
        out_shape=[jax.ShapeDtypeStruct((t, d), F32), jax.ShapeDtypeStruct((1, d), F32), jax.ShapeDtypeStruct((1, d), F32),
                   jax.ShapeDtypeStruct((t, d), BF16), jax.ShapeDtypeStruct((t, d), BF16),
                   jax.ShapeDtypeStruct((t, nb), BF16)],
        compiler_params=_cp(("arbitrary",), VMEM_BIG),
    )(dp_a, w_a, g_a, main_1, halo_1, main_2, halo_2, w_b, g_b, x_in, dres)


def _ffn_fwd(x, g, wg, wu, wd, tm, name):
    t, d = x.shape
    f = wg.shape[0]
    chunks = _ff_chunks(f)

    def body(x_ref, g_ref, wg_ref, wu_ref, wd_ref, xo_ref, gate_ref, up_ref):
        xv = x_ref[...]
        xhat, _ = _rms_parts(xv)
        h = (xhat * g_ref[...]).astype(BF16)
        acc = xv
        for c0, cw in chunks:
            gt = lax.dot_general(h, wg_ref[c0:c0 + cw, :], NT, preferred_element_type=F32)
            ut = lax.dot_general(h, wu_ref[c0:c0 + cw, :], NT, preferred_element_type=F32)
            gate_ref[:, c0:c0 + cw] = gt.astype(BF16)
            up_ref[:, c0:c0 + cw] = ut.astype(BF16)
            a = (gt * _sigmoid(gt) * ut).astype(BF16)
            acc = acc + jnp.dot(a, wd_ref[c0:c0 + cw, :], preferred_element_type=F32)
        xo_ref[...] = acc

    return pl.pallas_call(
        body, name=name, grid=(t // tm,),
        in_specs=[_row_spec(tm, d), _const_spec((1, d)), _const_spec((f, d)), _const_spec((f, d)), _const_spec((f, d))],
        out_specs=[_row_spec(tm, d), _row_spec(tm, f), _row_spec(tm, f)],
        out_shape=[jax.ShapeDtypeStruct((t, d), F32), jax.ShapeDtypeStruct((t, f), BF16),
                   jax.ShapeDtypeStruct((t, f), BF16)],
        compiler_params=_cp(("parallel",), VMEM_BIG),
    )(x, g, wg, wu, wd)


def _ffn_fwd_loss(x, g, wg, wu, wd, g_final, target, tm, name):
    t, d = x.shape
    f = wg.shape[0]
    chunks = _ff_chunks(f)

    def body(x_ref, g_ref, wg_ref, wu_ref, wd_ref, gf_ref, t_ref, gate_ref, up_ref, loss_ref, dx_ref, dgf_ref):
        i = pl.program_id(0)
        xv = x_ref[...]
        xhat, _ = _rms_parts(xv)
        h = (xhat * g_ref[...]).astype(BF16)
        acc = xv
        for c0, cw in chunks:
            gt = lax.dot_general(h, wg_ref[c0:c0 + cw, :], NT, preferred_element_type=F32)
            ut = lax.dot_general(h, wu_ref[c0:c0 + cw, :], NT, preferred_element_type=F32)
            gate_ref[:, c0:c0 + cw] = gt.astype(BF16)
            up_ref[:, c0:c0 + cw] = ut.astype(BF16)
            a = (gt * _sigmoid(gt) * ut).astype(BF16)
            acc = acc + jnp.dot(a, wd_ref[c0:c0 + cw, :], preferred_element_type=F32)
        xhat_o, r_o = _rms_parts(acc)
        gf = gf_ref[...]
        err = xhat_o * gf - t_ref[...]
        dy = err * (1.0 / d)
        dxhat = dy * gf
        dx_ref[...] = r_o * (dxhat - xhat_o * jnp.mean(dxhat * xhat_o, axis=-1, keepdims=True))

        @pl.when(i == 0)
        def _():
            dgf_ref[...] = jnp.zeros_like(dgf_ref)
            loss_ref[...] = jnp.zeros_like(loss_ref)

        dgf_ref[...] += jnp.sum(dy * xhat_o, axis=0, keepdims=True)
        part = jnp.sum(jnp.sum(err * err, axis=-1, keepdims=True), axis=0, keepdims=True) * (0.5 / d)
        loss_ref[...] += jnp.broadcast_to(part, loss_ref.shape)

    return pl.pallas_call(
        body, name=name, grid=(t // tm,),
        in_specs=[_row_spec(tm, d), _const_spec((1, d)), _const_spec((f, d)), _const_spec((f, d)), _const_spec((f, d)),
                  _const_spec((1, d)), _row_spec(tm, d)],
        out_specs=[_row_spec(tm, f), _row_spec(tm, f), pl.BlockSpec((8, 128), lambda i: (0, 0)), _row_spec(tm, d),
                   pl.BlockSpec((1, d), lambda i: (0, 0))],
        out_shape=[jax.ShapeDtypeStruct((t, f), BF16), jax.ShapeDtypeStruct((t, f), BF16),
                   jax.ShapeDtypeStruct((8, 128), F32), jax.ShapeDtypeStruct((t, d), F32),
                   jax.ShapeDtypeStruct((1, d), F32)],
        compiler_params=_cp(("arbitrary",), VMEM_BIG),
    )(x, g, wg, wu, wd, g_final, target)


def _ffn_bwd(dxo, xm, g, gate, up, wg, wu, wd, tm, name):
    t, d = xm.shape
    f = wg.shape[0]
    chunks = _ff_chunks(f)

    def body(dxo_ref, xm_ref, g_ref, gate_ref, up_ref, wg_ref, wu_ref, wd_ref,
             dxm_ref, dgate_ref, dup_ref, act_ref, h2_ref, dg_ref):
        i = pl.program_id(0)
        dxo_v = dxo_ref[...]
        dxo_b = dxo_v.astype(BF16)
        xhat, r = _rms_parts(xm_ref[...])
        gv = g_ref[...]
        h2_ref[...] = (xhat * gv).astype(BF16)
        dh = jnp.zeros((tm, d), F32)
        for c0, cw in chunks:
            dact = lax.dot_general(dxo_b, wd_ref[c0:c0 + cw, :], NT, preferred_element_type=F32)
            gt = gate_ref[:, c0:c0 + cw].astype(F32)
            ut = up_ref[:, c0:c0 + cw].astype(F32)
            sg = _sigmoid(gt)
            sl = gt * sg
            act_ref[:, c0:c0 + cw] = (sl * ut).astype(BF16)
            dgt = (dact * ut * (sg * (1.0 + gt * (1.0 - sg)))).astype(BF16)
            dut = (dact * sl).astype(BF16)
            dgate_ref[:, c0:c0 + cw] = dgt
            dup_ref[:, c0:c0 + cw] = dut
            dh = dh + jnp.dot(dgt, wg_ref[c0:c0 + cw, :], preferred_element_type=F32)
            dh = dh + jnp.dot(dut, wu_ref[c0:c0 + cw, :], preferred_element_type=F32)
        dxhat = dh * gv
        dx = r * (dxhat - xhat * jnp.mean(dxhat * xhat, axis=-1, keepdims=True))
        dxm_ref[...] = dxo_v + dx

        @pl.when(i == 0)
        def _():
            dg_ref[...] = jnp.zeros_like(dg_ref)

        dg_ref[...] += jnp.sum(dh * xhat, axis=0, keepdims=True)

    return pl.pallas_call(
        body, name=name, grid=(t // tm,),
        in_specs=[_row_spec(tm, d), _row_spec(tm, d), _const_spec((1, d)), _row_spec(tm, f), _row_spec(tm, f),
                  _const_spec((f, d)), _const_spec((f, d)), _const_spec((f, d))],
        out_specs=[_row_spec(tm, d), _row_spec(tm, f), _row_spec(tm, f), _row_spec(tm, f), _row_spec(tm, d),
                   pl.BlockSpec((1, d), lambda i: (0, 0))],
        out_shape=[jax.ShapeDtypeStruct((t, d), F32), jax.ShapeDtypeStruct((t, f), BF16),
                   jax.ShapeDtypeStruct((t, f), BF16), jax.ShapeDtypeStruct((t, f), BF16),
                   jax.ShapeDtypeStruct((t, d), BF16), jax.ShapeDtypeStruct((1, d), F32)],
        compiler_params=_cp(("arbitrary",), VMEM_BIG),
    )(dxo, xm, g, gate, up, wg, wu, wd)


def _wgrad(a, b, tt, name):
    t, k = a.shape
    n = b.shape[1]
    nt = t // tt

    def body(a_ref, b_ref, o_ref, acc):
        i = pl.program_id(0)

        @pl.when(i == 0)
        def _():
            acc[...] = jnp.zeros_like(acc)

        acc[...] += lax.dot_general(a_ref[...].astype(BF16), b_ref[...].astype(BF16), TN,
                                    preferred_element_type=F32)

        @pl.when(i == nt - 1)
        def _():
            o_ref[...] = acc[...].astype(BF16)

    return pl.pallas_call(
        body, name=name, grid=(nt,),
        in_specs=[_row_spec(tt, k), _row_spec(tt, n)],
        out_specs=pl.BlockSpec((k, n), lambda i: (0, 0)),
        out_shape=jax.ShapeDtypeStruct((k, n), BF16),
        scratch_shapes=[pltpu.VMEM((k, n), F32)],
        compiler_params=_cp(("arbitrary",), VMEM_BIG),
    )(a, b)


def _col_head(width):
    return lax.broadcasted_iota(jnp.int32, (1, width), 1) // HEAD_DIM


def _keep_head(a, colh, h):
    return jnp.where(colh == h, a, jnp.zeros_like(a))


def _softmax_cols(s, sink=None):
    m = jnp.max(s, axis=0, keepdims=True)
    if sink is not None:
        m = jnp.maximum(m, sink)
    p = jnp.exp(s - m)
    l = jnp.sum(p, axis=0, keepdims=True)
    if sink is None:
        return p * (1.0 / l), None
    es = jnp.exp(sink - m)
    inv = 1.0 / (l + es)
    return p * inv, es * inv


def _add4(v):
    return (v[0] + v[1]) + (v[2] + v[3])


def _mem_attn_fwd(qm, mk, mv):
    colh = _col_head(mk.shape[1])
    mks = mk * SCALE
    heads = range(N_MEM_HEADS)
    ss = [lax.dot_general(_keep_head(mks, colh, h), qm, NT, preferred_element_type=F32) for h in heads]
    ps = [_softmax_cols(s)[0].astype(BF16) for s in ss]
    return _add4([lax.dot_general(ps[h], _keep_head(mv, colh, h), TN, preferred_element_type=F32) for h in heads])


def _mem_attn_bwd(qm, dy_b, mk, mv):
    colh = _col_head(mk.shape[1])
    mks = mk * SCALE
    heads = range(N_MEM_HEADS)
    khs = [_keep_head(mks, colh, h) for h in heads]
    vhs = [_keep_head(mv, colh, h) for h in heads]
    ss = [lax.dot_general(khs[h], qm, NT, preferred_element_type=F32) for h in heads]
    dps = [lax.dot_general(vhs[h], dy_b, NT, preferred_element_type=F32) for h in heads]
    pbs, dsbs = [], []
    for h in heads:
        p, _ = _softmax_cols(ss[h])
        ds = p * (dps[h] - jnp.sum(p * dps[h], axis=0, keepdims=True))
        pbs.append(p.astype(BF16))
        dsbs.append(ds.astype(BF16))
    dq = _add4([lax.dot_general(dsbs[h], khs[h], TN, preferred_element_type=F32) for h in heads])
    dmk = _add4([jnp.where(colh == h, jnp.dot(dsbs[h], qm, preferred_element_type=F32) * SCALE, 0.0) for h in heads])
    dmv = _add4([jnp.where(colh == h, jnp.dot(pbs[h], dy_b, preferred_element_type=F32), 0.0) for h in heads])
    return dq, dmk, dmv


def _shift_down(v, halo, k):
    rolled = pltpu.roll(v, k, 0)
    hrolled = pltpu.roll(halo, k, 0)[0:8]
    rows = lax.broadcasted_iota(jnp.int32, (8, v.shape[1]), 0)
    first = jnp.where(rows < k, hrolled, rolled[0:8])
    return jnp.concatenate([first, rolled[8:]], axis=0)


def _shift_up(v, halo, k):
    n = v.shape[0]
    rolled = pltpu.roll(v, n - k, 0)
    hrolled = pltpu.roll(halo, 8 - k, 0)[0:8]
    rows = lax.broadcasted_iota(jnp.int32, (8, v.shape[1]), 0)
    last = jnp.where(rows >= 8 - k, hrolled, rolled[n - 8:])
    return jnp.concatenate([rolled[:n - 8], last], axis=0)


def _conv_parts(p, ph, cw, first_tile, cwid):
    u = p[:, 0:cwid].astype(F32)
    bg = p[:, cwid:2 * cwid].astype(F32)
    cg = p[:, 2 * cwid:3 * cwid].astype(F32)
    v = cg * u
    vh = ph[:, 2 * cwid:3 * cwid].astype(F32) * ph[:, 0:cwid].astype(F32)
    vh = jnp.where(first_tile, 0.0, vh)
    v1 = _shift_down(v, vh, 1)
    v2 = _shift_down(v, vh, 2)
    conv = cw[0:1, :] * v2 + cw[1:2, :] * v1 + cw[2:3, :] * v
    return u, bg, cg, v, v1, v2, conv


def _halo_prev_spec(rows, n, tm):
    per = tm // rows
    return pl.BlockSpec((rows, n), lambda i: (jnp.maximum(i * per - 1, 0), 0))


def _halo_next_spec(rows, n, tm, t):
    per = tm // rows
    last = t // rows - 1
    return pl.BlockSpec((rows, n), lambda i: (jnp.minimum((i + 1) * per, last), 0))


def _mix_a_fwd(x, proj, convw, memkv, layer, wout, tm, name):
    t, d = x.shape
    n_mem = memkv.shape[0]
    mw = N_MEM_HEADS * HEAD_DIM
    cwid = d - mw
    pw = proj.shape[1]

    def body(x_ref, p_ref, ph_ref, cw_ref, mkv_ref, wo_ref, xo_ref, cat_ref):
        i = pl.program_id(0)
        p = p_ref[...]
        _, bg, _, _, _, _, conv = _conv_parts(p, ph_ref[...], cw_ref[...], i == 0, cwid)
        ytok = (bg * conv).astype(BF16)
        mkv = mkv_ref[...]
        ymem = _mem_attn_fwd(p[:, 3 * cwid:3 * cwid + mw], mkv[:, 0:mw], mkv[:, mw:2 * mw])
        cat = jnp.concatenate([ytok, ymem.astype(BF16)], axis=1)
        cat_ref[...] = cat
        xo_ref[...] = x_ref[...] + jnp.dot(cat, wo_ref[...], preferred_element_type=F32)

    return pl.pallas_call(
        body, name=name, grid=(t // tm,),
        in_specs=[_row_spec(tm, d), _row_spec(tm, pw), _halo_prev_spec(16, pw, tm), _const_spec((3, cwid)),
                  pl.BlockSpec((n_mem, 2 * mw), lambda i: (0, layer)), _const_spec((d, d))],
        out_specs=[_row_spec(tm, d), _row_spec(tm, d)],
        out_shape=[jax.ShapeDtypeStruct((t, d), F32), jax.ShapeDtypeStruct((t, d), BF16)],
        compiler_params=_cp(("parallel",), VMEM_BIG),
    )(x, proj, proj, convw, memkv, wout)


def _mix_a_bwd(dxm, proj, convw, memkv, layer, wout, tm, name):
    t, d = dxm.shape
    n_mem = memkv.shape[0]
    mw = N_MEM_HEADS * HEAD_DIM
    cwid = d - mw
    pw = proj.shape[1]
    nt = t // tm

    def body(dx_ref, dxn_ref, p_ref, ph_ref, pn_ref, cw_ref, mkv_ref, wo_ref,
             dp_ref, dcw_ref, dmkv_ref, dmk_acc, dmv_acc):
        i = pl.program_id(0)
        p = p_ref[...]
        cw = cw_ref[...]
        wo = wo_ref[...]
        u, bg, cg, v, v1, v2, conv = _conv_parts(p, ph_ref[...], cw, i == 0, cwid)
        dcat = lax.dot_general(dx_ref[...].astype(BF16), wo, NT, preferred_element_type=F32)
        dytok = dcat[:, 0:cwid]
        dymem_b = dcat[:, cwid:d].astype(BF16)
        pn = pn_ref[...]
        dcat_n = lax.dot_general(dxn_ref[...].astype(BF16), wo[0:cwid, :], NT, preferred_element_type=F32)
        dconv_n = jnp.where(i == nt - 1, 0.0, dcat_n * pn[:, cwid:2 * cwid].astype(F32))
        dbg = dytok * conv
        dconv = dytok * bg
        dv = cw[2:3, :] * dconv + cw[1:2, :] * _shift_up(dconv, dconv_n, 1) + cw[0:1, :] * _shift_up(dconv, dconv_n, 2)
        du = dv * cg
        dcg = dv * u
        rows8 = lax.broadcasted_iota(jnp.int32, (8, cwid), 0)
        dcw = (jnp.where(rows8 == 0, jnp.sum(dconv * v2, axis=0, keepdims=True), 0.0)
               + jnp.where(rows8 == 1, jnp.sum(dconv * v1, axis=0, keepdims=True), 0.0)
               + jnp.where(rows8 == 2, jnp.sum(dconv * v, axis=0, keepdims=True), 0.0))
        mkv = mkv_ref[...]
        qm = p[:, 3 * cwid:3 * cwid + mw]
        dqm, dmk, dmv = _mem_attn_bwd(qm, dymem_b, mkv[:, 0:mw], mkv[:, mw:2 * mw])
        dp_ref[...] = jnp.concatenate([du.astype(BF16), dbg.astype(BF16), dcg.astype(BF16), dqm.astype(BF16)], axis=1)

        @pl.when(i == 0)
        def _():
            dcw_ref[...] = jnp.zeros_like(dcw_ref)
            dmk_acc[...] = jnp.zeros_like(dmk_acc)
            dmv_acc[...] = jnp.zeros_like(dmv_acc)

        dcw_ref[...] += dcw
        dmk_acc[...] += dmk
        dmv_acc[...] += dmv

        @pl.when(i == nt - 1)
        def _():
            dmkv_ref[...] = jnp.concatenate([dmk_acc[...], dmv_acc[...]], axis=1)

    return pl.pallas_call(
        body, name=name, grid=(nt,),
        in_specs=[_row_spec(tm, d), _halo_next_spec(16, d, tm, t), _row_spec(tm, pw), _halo_prev_spec(16, pw, tm),
                  _halo_next_spec(16, pw, tm, t), _const_spec((3, cwid)),
                  pl.BlockSpec((n_mem, 2 * mw), lambda i: (0, layer)), _const_spec((d, d))],
        out_specs=[_row_spec(tm, pw), pl.BlockSpec((8, cwid), lambda i: (0, 0)),
                   pl.BlockSpec((n_mem, 2 * mw), lambda i: (0, 0))],
        out_shape=[jax.ShapeDtypeStruct((t, pw), BF16),
                   jax.ShapeDtypeStruct((8, cwid), F32), jax.ShapeDtypeStruct((n_mem, 2 * mw), F32)],
        scratch_shapes=[pltpu.VMEM((n_mem, mw), F32), pltpu.VMEM((n_mem, mw), F32)],
        compiler_params=_cp(("arbitrary",), VMEM_BIG),
    )(dxm, dxm, proj, proj, proj, convw, memkv, wout)


def _rel_tables():
    qi = np.arange(BLOCK, dtype=np.int32)[:, None]
    kj = np.arange(2 * BLOCK, dtype=np.int32)[None, :]
    dist = qi + BLOCK - kj
    inw = (dist >= 0) & (dist < BLOCK)
    max_exact = REL_BUCKETS // 2
    dd = np.maximum(np.maximum(dist, 0), 1).astype(np.float32)
    large = max_exact + (np.log(dd / np.float32(max_exact)) / np.float32(math.log(REL_MAX_DIST / max_exact))
                         * np.float32(REL_BUCKETS - max_exact)).astype(np.int32)
    large = np.minimum(large, REL_BUCKETS - 1)
    bucket = np.where(np.maximum(dist, 0) < max_exact, np.maximum(dist, 0), large)
    return np.where(inw, bucket, -1).astype(np.int32)


def _bias_tables(rel_bias, sinks, name):
    bucket_t = jnp.asarray(_rel_tables().T)

    def body(rb_ref, sk_ref, bk_ref, bias_ref, sink_ref):
        bk = bk_ref[...]
        prev = lax.broadcasted_iota(jnp.int32, bk.shape, 0) < BLOCK
        for h in range(N_KV_HEADS):
            for j in range(GROUP):
                head = GROUP * h + j
                acc = jnp.full(bk.shape, NEG, F32)
                for b in range(REL_BUCKETS):
                    acc = jnp.where(bk == b, rb_ref[b, head], acc)
                bias_ref[h, :, j * BLOCK:(j + 1) * BLOCK] = acc
                bias_ref[N_KV_HEADS + h, :, j * BLOCK:(j + 1) * BLOCK] = jnp.where(prev, NEG, acc)
                sink_ref[h, :, j * BLOCK:(j + 1) * BLOCK] = jnp.full((8, BLOCK), sk_ref[0, head], F32)

    smem = pl.BlockSpec(memory_space=pltpu.SMEM)
    return pl.pallas_call(
        body, name=name,
        in_specs=[smem, smem, pl.BlockSpec(memory_space=pltpu.VMEM)],
        out_specs=[pl.BlockSpec(memory_space=pltpu.VMEM), pl.BlockSpec(memory_space=pltpu.VMEM)],
        out_shape=[jax.ShapeDtypeStruct((2 * N_KV_HEADS, 2 * BLOCK, GROUP * BLOCK), F32),
                   jax.ShapeDtypeStruct((N_KV_HEADS, 8, GROUP * BLOCK), F32)],
    )(rel_bias, sinks.reshape(1, N_Q_HEADS), bucket_t)


def _bias_bwd(dbias_a, dbias_b, name):
    bucket_t = jnp.asarray(_rel_tables().T)

    def body(da_ref, db_ref, bk_ref, o_ref):
        bk = bk_ref[...]
        ri = lax.broadcasted_iota(jnp.int32, (REL_BUCKETS, 128), 0)
        ci = lax.broadcasted_iota(jnp.int32, (REL_BUCKETS, 128), 1)
        out = jnp.zeros((REL_BUCKETS, 128), F32)
        for h in range(N_KV_HEADS):
            dsum = da_ref[h] + db_ref[h]
            for j in range(GROUP):
                head = GROUP * h + j
                seg = dsum[:, j * BLOCK:(j + 1) * BLOCK]
                for b in range(REL_BUCKETS):
                    val = jnp.sum(jnp.sum(jnp.where(bk == b, seg, 0.0), axis=0, keepdims=True), axis=1, keepdims=True)
                    out = out + jnp.where((ri == b) & (ci == head), val, 0.0)
        o_ref[...] = out

    vm = pl.BlockSpec(memory_space=pltpu.VMEM)
    return pl.pallas_call(
        body, name=name, in_specs=[vm, vm, vm], out_specs=vm,
        out_shape=jax.ShapeDtypeStruct((REL_BUCKETS, 128), F32),
    )(dbias_a, dbias_b, bucket_t)


def _stack_members(ref, r0, width):
    blk = ref[pl.ds(r0, BLOCK), 0:GROUP * width]
    return jnp.concatenate([blk[:, j * width:(j + 1) * width] for j in range(GROUP)], axis=0)


def _mix_b_fwd(x, qp, kv, bias, sinkt, memkv, layer, wout, tm, name):
    t, d = x.shape
    n_mem = memkv.shape[0]
    mw = N_MEM_HEADS * HEAD_DIM
    qw = d - mw
    kw = N_KV_HEADS * HEAD_DIM
    nb = tm // BLOCK
    rows = GROUP * BLOCK

    def body(x_ref, q_ref, kv_ref, kvh_ref, bias_ref, sink_ref, mkv_ref, wo_ref, xo_ref, cat_ref, kvx, ytok):
        i = pl.program_id(0)
        kvx[0:BLOCK, :] = kvh_ref[...]
        kvx[BLOCK:BLOCK + tm, :] = kv_ref[...]
        colh = _col_head(kw)

        def blk(b, carry):
            r0 = pl.multiple_of(b * BLOCK, BLOCK)
            win = kvx[pl.ds(r0, 2 * BLOCK), :]
            kwin = win[:, 0:kw] * SCALE
            vwin = win[:, kw:2 * kw]
            qs = _stack_members(q_ref, r0, kw)
            first = ((i == 0) & (b == 0)).astype(jnp.int32) * N_KV_HEADS
            heads = range(N_KV_HEADS)
            ss = [lax.dot_general(_keep_head(kwin, colh, h), qs, NT, preferred_element_type=F32) for h in heads]
            ps = [_softmax_cols(ss[h] + bias_ref[first + h], sink_ref[h][0:1, :])[0].astype(BF16) for h in heads]
            o = _add4([lax.dot_general(ps[h], _keep_head(vwin, colh, h), TN, preferred_element_type=F32)
                       for h in heads])
            for j in range(GROUP):
                ytok[pl.ds(r0, BLOCK), j * kw:(j + 1) * kw] = o[j * BLOCK:(j + 1) * BLOCK].astype(BF16)
            return carry

        for b_static in range(nb):
            blk(b_static, 0)
        mkv = mkv_ref[...]
        ymem = _mem_attn_fwd(q_ref[:, qw:d], mkv[:, 0:mw], mkv[:, mw:2 * mw])
        cat = jnp.concatenate([ytok[...], ymem.astype(BF16)], axis=1)
        cat_ref[...] = cat
        xo_ref[...] = x_ref[...] + jnp.dot(cat, wo_ref[...], preferred_element_type=F32)

    return pl.pallas_call(
        body, name=name, grid=(t // tm,),
        in_specs=[_row_spec(tm, d), _row_spec(tm, d), _row_spec(tm, 2 * kw), _halo_prev_spec(BLOCK, 2 * kw, tm),
                  _const_spec((2 * N_KV_HEADS, 2 * BLOCK, rows)), _const_spec((N_KV_HEADS, 8, rows)),
                  pl.BlockSpec((n_mem, 2 * mw), lambda i: (0, layer)), _const_spec((d, d))],
        out_specs=[_row_spec(tm, d), _row_spec(tm, d)],
        out_shape=[jax.ShapeDtypeStruct((t, d), F32), jax.ShapeDtypeStruct((t, d), BF16)],
        scratch_shapes=[pltpu.VMEM((tm + BLOCK, 2 * kw), BF16), pltpu.VMEM((tm, qw), BF16)],
        compiler_params=_cp(("parallel",), VMEM_BIG),
    )(x, qp, kv, kv, bias, sinkt, memkv, wout)


def _mix_b_bwd(dxm, qp, kv, bias, sinkt, memkv, layer, wout, tm, name):
    t, d = dxm.shape
    n_mem = memkv.shape[0]
    mw = N_MEM_HEADS * HEAD_DIM
    qw = d - mw
    kw = N_KV_HEADS * HEAD_DIM
    nb = tm // BLOCK
    nt = t // tm
    rows = GROUP * BLOCK

    def body(dx_ref, q_ref, kv_ref, kvh_ref, bias_ref, sink_ref, mkv_ref, wo_ref,
             dq_ref, dkv_ref, dkvh_ref, dbias_ref, dsink_ref, dmkv_ref,
             kvx, dkvx, dcat_s, dmk_acc, dmv_acc):
        i = pl.program_id(0)

        @pl.when(i == 0)
        def _():
            dbias_ref[...] = jnp.zeros_like(dbias_ref)
            dsink_ref[...] = jnp.zeros_like(dsink_ref)
            dmk_acc[...] = jnp.zeros_like(dmk_acc)
            dmv_acc[...] = jnp.zeros_like(dmv_acc)

        kvx[0:BLOCK, :] = kvh_ref[...]
        kvx[BLOCK:BLOCK + tm, :] = kv_ref[...]
        dkvx[...] = jnp.zeros_like(dkvx)
        dcat_s[...] = lax.dot_general(dx_ref[...].astype(BF16), wo_ref[...], NT,
                                      preferred_element_type=F32).astype(BF16)
        colh = _col_head(kw)
        lane8 = lax.broadcasted_iota(jnp.int32, (8, 128), 1)

        def blk(b, carry):
            r0 = pl.multiple_of(b * BLOCK, BLOCK)
            win = kvx[pl.ds(r0, 2 * BLOCK), :]
            kwin = win[:, 0:kw] * SCALE
            vwin = win[:, kw:2 * kw]
            qs = _stack_members(q_ref, r0, kw)
            dos = _stack_members(dcat_s, r0, kw)
            first = ((i == 0) & (b == 0)).astype(jnp.int32) * N_KV_HEADS
            heads = range(N_KV_HEADS)
            khs = [_keep_head(kwin, colh, h) for h in heads]
            vhs = [_keep_head(vwin, colh, h) for h in heads]
            ss = [lax.dot_general(khs[h], qs, NT, preferred_element_type=F32) for h in heads]
            dps = [lax.dot_general(vhs[h], dos, NT, preferred_element_type=F32) for h in heads]
            dsink = jnp.zeros((8, 128), F32)
            pbs, dsbs = [], []
            for h in heads:
                p, sinkp = _softmax_cols(ss[h] + bias_ref[first + h], sink_ref[h][0:1, :])
                delta = jnp.sum(p * dps[h], axis=0, keepdims=True)
                ds = p * (dps[h] - delta)
                dbias_ref[h] += ds
                sd = sinkp * delta
                for j in range(GROUP):
                    val = -jnp.sum(sd[:, j * BLOCK:(j + 1) * BLOCK], axis=1, keepdims=True)
                    dsink = dsink + jnp.where(lane8 == 4 * j + h, val, 0.0)
                pbs.append(p.astype(BF16))
                dsbs.append(ds.astype(BF16))
            dq = _add4([lax.dot_general(dsbs[h], khs[h], TN, preferred_element_type=F32) for h in heads])
            dk = _add4([jnp.where(colh == h, jnp.dot(dsbs[h], qs, preferred_element_type=F32) * SCALE, 0.0)
                        for h in heads])
            dv = _add4([jnp.where(colh == h, jnp.dot(pbs[h], dos, preferred_element_type=F32), 0.0) for h in heads])
            for j in range(GROUP):
                dq_ref[pl.ds(r0, BLOCK), j * kw:(j + 1) * kw] = dq[j * BLOCK:(j + 1) * BLOCK].astype(BF16)
            dsink_ref[...] += dsink
            dkvx[pl.ds(r0, 2 * BLOCK), :] += jnp.concatenate([dk, dv], axis=1)
            return carry

        for b_static in range(nb):
            blk(b_static, 0)
        dkvh_ref[0] = dkvx[0:BLOCK, :]
        dkv_ref[...] = dkvx[BLOCK:BLOCK + tm, :]

        mkv = mkv_ref[...]
        dqm, dmk, dmv = _mem_attn_bwd(q_ref[:, qw:d], dcat_s[:, qw:d], mkv[:, 0:mw], mkv[:, mw:2 * mw])
        dq_ref[:, qw:d] = dqm.astype(BF16)
        dmk_acc[...] += dmk
        dmv_acc[...] += dmv

        @pl.when(i == nt - 1)
        def _():
            dmkv_ref[...] = jnp.concatenate([dmk_acc[...], dmv_acc[...]], axis=1)

    return pl.pallas_call(
        body, name=name, grid=(nt,),
        in_specs=[_row_spec(tm, d), _row_spec(tm, d), _row_spec(tm, 2 * kw), _halo_prev_spec(BLOCK, 2 * kw, tm),
                  _const_spec((2 * N_KV_HEADS, 2 * BLOCK, rows)), _const_spec((N_KV_HEADS, 8, rows)),
                  pl.BlockSpec((n_mem, 2 * mw), lambda i: (0, layer)), _const_spec((d, d))],
        out_specs=[_row_spec(tm, d), _row_spec(tm, 2 * kw),
                   pl.BlockSpec((1, BLOCK, 2 * kw), lambda i: (i, 0, 0)),
                   pl.BlockSpec((N_KV_HEADS, 2 * BLOCK, rows), lambda i: (0, 0, 0)),
                   pl.BlockSpec((8, 128), lambda i: (0, 0)),
                   pl.BlockSpec((n_mem, 2 * mw), lambda i: (0, 0))],
        out_shape=[jax.ShapeDtypeStruct((t, d), BF16),
                   jax.ShapeDtypeStruct((t, 2 * kw), F32), jax.ShapeDtypeStruct((nt, BLOCK, 2 * kw), F32),
                   jax.ShapeDtypeStruct((N_KV_HEADS, 2 * BLOCK, rows), F32), jax.ShapeDtypeStruct((8, 128), F32),
                   jax.ShapeDtypeStruct((n_mem, 2 * mw), F32)],
        scratch_shapes=[pltpu.VMEM((tm + BLOCK, 2 * kw), BF16), pltpu.VMEM((tm + BLOCK, 2 * kw), F32),
                        pltpu.VMEM((tm, d), BF16), pltpu.VMEM((n_mem, mw), F32), pltpu.VMEM((n_mem, mw), F32)],
        compiler_params=_cp(("arbitrary",), VMEM_BIG),
    )(dxm, qp, kv, kv, bias, sinkt, memkv, wout)


def _adam_math(w, g, m, v):
    m2 = ADAM_B1 * m + (1.0 - ADAM_B1) * g
    v2 = ADAM_B2 * v + (1.0 - ADAM_B2) * (g * g)
    m_hat = m2 / (1.0 - ADAM_B1 ** ADAM_STEP)
    v_hat = v2 / (1.0 - ADAM_B2 ** ADAM_STEP)
    delta = -ADAM_LR * (m_hat / (jnp.sqrt(v_hat) + ADAM_EPS) + ADAM_WD * w)
    return delta, m2, v2


def _adamw_sharded(w, land, m, v, name, after=()):
    nl, r, c = w.shape
    tr = max(cand for cand in range(16, r + 1, 16) if r % cand == 0 and cand * c <= 512 * 1024)

    def body(w_ref, a_ref, m_ref, v_ref, *rest):
        g_ref, d_ref, mo_ref, vo_ref = rest[-4:]
        g = a_ref[0, 0].astype(F32) + a_ref[1, 0].astype(F32)
        for k in range(1, N_CHIPS):
            g = g + (a_ref[2 * k, 0].astype(F32) + a_ref[2 * k + 1, 0].astype(F32))
        delta, m2, v2 = _adam_math(w_ref[0], g, m_ref[0], v_ref[0])
        g_ref[0] = g
        d_ref[0] = delta
        mo_ref[0] = m2
        vo_ref[0] = v2

    rs = pl.BlockSpec((1, tr, c), lambda l, i: (l, i, 0))
    ps = pl.BlockSpec((2 * N_CHIPS, 1, tr, c), lambda l, i: (0, l, i, 0))
    sd = jax.ShapeDtypeStruct((nl, r, c), F32)
    return pl.pallas_call(
        body, name=name, grid=(nl, r // tr),
        in_specs=[rs, ps, rs, rs] + [pl.BlockSpec(memory_space=pl.ANY)] * len(after),
        out_specs=[rs, rs, rs, rs], out_shape=[sd, sd, sd, sd],
        compiler_params=_cp(("parallel", "parallel"), VMEM_BIG),
    )(w, land, m, v, *after)


def _adamw_packed(w, g, m, v, name):
    def body(w_ref, g_ref, m_ref, v_ref, d_ref, mo_ref, vo_ref):
        delta, m2, v2 = _adam_math(w_ref[...], g_ref[...], m_ref[...], v_ref[...])
        d_ref[...] = delta
        mo_ref[...] = m2
        vo_ref[...] = v2

    vm = pl.BlockSpec(memory_space=pltpu.VMEM)
    sd = jax.ShapeDtypeStruct(w.shape, F32)
    return pl.pallas_call(body, name=name, in_specs=[vm] * 4, out_specs=[vm] * 3, out_shape=[sd] * 3)(w, g, m, v)


def _place():
    return lax.axis_index("x"), lax.axis_index("y"), lax.axis_index("c")


def _hbm(a):
    return pltpu.with_memory_space_constraint(a, pltpu.HBM)


def _peers(x, y, c, both_cores):
    chips = [(1 - x, y), (x, 1 - y), (1 - x, 1 - y)]
    if not both_cores:
        return [(px, py, c) for px, py in chips]
    return [(px, py, pc) for px, py in chips for pc in (c, 1 - c)] + [(x, y, 1 - c)]


def _chip_copy(src, land, gather, layer, chip_src, slot, send_sem, recv_sem, peer):
    s = src if gather else src.at[chip_src]
    d = land.at[slot] if layer is None else land.at[slot, layer]
    return pltpu.make_async_remote_copy(src_ref=s, dst_ref=d, send_sem=send_sem, recv_sem=recv_sem,
                                        device_id=peer, device_id_type=MESH)


def _own_copy(src, land, gather, layer, chip, slot, own_sem):
    s = src if gather else src.at[chip]
    d = land.at[slot] if layer is None else land.at[slot, layer]
    return pltpu.make_async_copy(s, d, own_sem)


def _exchange_start(srcs, lands, gather, layers, both_cores, after, name):
    n = len(srcs)
    npeer = 7 if both_cores else 3
    hbm = pl.BlockSpec(memory_space=pltpu.HBM)
    sem = pl.BlockSpec(memory_space=pltpu.SEMAPHORE)

    def body(*refs):
        ins, lds = refs[:n], refs[n:2 * n]
        first_out = 2 * n + len(after)
        send_sems, recv_sems, own_sems, token = refs[first_out], refs[first_out + 1], refs[first_out + 2], refs[-1]
        x, y, c = _place()
        slot = 2 * x + y if gather is True else 2 * (2 * x + y) + c
        for t in range(n):
            _own_copy(ins[t], lds[t], gather, layers[t], 2 * x + y, slot, own_sems.at[t]).start()
            for r, peer in enumerate(_peers(x, y, c, both_cores)):
                _chip_copy(ins[t], lds[t], gather, layers[t], 2 * peer[0] + peer[1], slot,
                           send_sems.at[npeer * t + r], recv_sems.at[npeer * t + r], peer).start()
        token[...] = jnp.zeros_like(token)

    both = list(srcs) + list(lands)
    outs = pl.pallas_call(
        body, name=name, in_specs=[hbm] * (2 * n) + [pl.BlockSpec(memory_space=pl.ANY)] * len(after),
        out_specs=(sem, sem, sem, *([hbm] * (2 * n)), pl.BlockSpec(memory_space=pltpu.VMEM)),
        out_shape=(pltpu.SemaphoreType.DMA((npeer * n,)), pltpu.SemaphoreType.DMA((npeer * n,)),
                   pltpu.SemaphoreType.DMA((n,)),
                   *[pltpu.HBM(a.shape, a.dtype) for a in both], jax.ShapeDtypeStruct((8, 128), F32)),
        input_output_aliases={t: 3 + t for t in range(2 * n)},
        compiler_params=_cp(has_side_effects=pltpu.SideEffectType.DATAFLOW_SIDE_EFFECTING),
    )(*[_hbm(a) for a in both], *after)
    return dict(send=outs[0], recv=outs[1], own=outs[2], srcs=list(outs[3:3 + n]), lands=list(outs[3 + n:3 + 2 * n]),
                token=outs[-1], gather=gather, layers=list(layers), both_cores=both_cores)


def _exchange_wait(groups, lands, land_ids, after, name):
    flat = [s for g in groups for s in g["srcs"]]
    ns, nl, ng, na = len(flat), len(lands), len(groups), len(after)
    hbm = pl.BlockSpec(memory_space=pltpu.HBM)
    sem = pl.BlockSpec(memory_space=pltpu.SEMAPHORE)

    def body(*refs):
        srcs, lds = refs[:ns], refs[ns:ns + nl]
        sems = refs[ns + nl:ns + nl + 3 * ng]
        x, y, c = _place()
        k = 0
        for gi, g in enumerate(groups):
            peers = _peers(x, y, c, g["both_cores"])
            for t in range(len(g["srcs"])):
                _own_copy(srcs[k], lds[land_ids[gi][t]], g["gather"], g["layers"][t], 0, 0, sems[3 * gi + 2].at[t]).wait()
                for r, peer in enumerate(peers):
                    cp = _chip_copy(srcs[k], lds[land_ids[gi][t]], g["gather"], g["layers"][t], 0, 0,
                                    sems[3 * gi].at[len(peers) * t + r], sems[3 * gi + 1].at[len(peers) * t + r], peer)
                    cp.wait_send()
                    cp.wait_recv()
                k += 1

    both = flat + list(lands)
    sem_args = [a for g in groups for a in (g["send"], g["recv"], g["own"])]
    outs = pl.pallas_call(
        body, name=name,
        in_specs=[hbm] * (ns + nl) + [sem] * (3 * ng) + [pl.BlockSpec(memory_space=pl.ANY)] * na,
        out_specs=[hbm] * (ns + nl),
        out_shape=[pltpu.HBM(a.shape, a.dtype) for a in both],
        input_output_aliases={t: t for t in range(ns + nl)},
        compiler_params=_cp(has_side_effects=pltpu.SideEffectType.DATAFLOW_SIDE_EFFECTING),
    )(*both, *sem_args, *after)
    return list(outs[ns:])


def _fill_copy(land, layer, slot, send_sem, recv_sem, x, y, c):
    mine = land.at[slot] if layer is None else land.at[slot, layer]
    return pltpu.make_async_remote_copy(src_ref=mine, dst_ref=mine, send_sem=send_sem, recv_sem=recv_sem,
                                        device_id=(x, y, 1 - c), device_id_type=MESH)


def _core_fill_start(lands, layers, after, name):
    n = len(lands)
    hbm = pl.BlockSpec(memory_space=pltpu.HBM)
    sem = pl.BlockSpec(memory_space=pltpu.SEMAPHORE)

    def body(*refs):
        ins = refs[:n]
        first_out = n + len(after)
        send_sems, recv_sems, token = refs[first_out], refs[first_out + 1], refs[-1]
        x, y, c = _place()
        for t in range(n):
            for k in range(N_CHIPS):
                _fill_copy(ins[t], layers[t], 2 * k + c, send_sems.at[N_CHIPS * t + k], recv_sems.at[N_CHIPS * t + k],
                           x, y, c).start()
        token[...] = jnp.zeros_like(token)

    outs = pl.pallas_call(
        body, name=name, in_specs=[hbm] * n + [pl.BlockSpec(memory_space=pl.ANY)] * len(after),
        out_specs=(sem, sem, *([hbm] * n), pl.BlockSpec(memory_space=pltpu.VMEM)),
        out_shape=(pltpu.SemaphoreType.DMA((N_CHIPS * n,)), pltpu.SemaphoreType.DMA((N_CHIPS * n,)),
                   *[pltpu.HBM(a.shape, a.dtype) for a in lands], jax.ShapeDtypeStruct((8, 128), F32)),
        input_output_aliases={t: 2 + t for t in range(n)},
        compiler_params=_cp(has_side_effects=pltpu.SideEffectType.DATAFLOW_SIDE_EFFECTING),
    )(*[_hbm(a) for a in lands], *after)
    return dict(send=outs[0], recv=outs[1], lands=list(outs[2:2 + n]), token=outs[-1], layers=list(layers))


def _core_fill_wait(fill, after, name):
    lands, layers = fill["lands"], fill["layers"]
    n = len(lands)
    hbm = pl.BlockSpec(memory_space=pltpu.HBM)
    sem = pl.BlockSpec(memory_space=pltpu.SEMAPHORE)

    def body(*refs):
        ins = refs[:n]
        send_sems, recv_sems = refs[n], refs[n + 1]
        x, y, c = _place()
        for t in range(n):
            for k in range(N_CHIPS):
                cp = _fill_copy(ins[t], layers[t], 2 * k + c, send_sems.at[N_CHIPS * t + k],
                                recv_sems.at[N_CHIPS * t + k], x, y, c)
                cp.wait_send()
                cp.wait_recv()

    outs = pl.pallas_call(
        body, name=name, in_specs=[hbm] * n + [sem, sem] + [pl.BlockSpec(memory_space=pl.ANY)] * len(after),
        out_specs=[hbm] * n, out_shape=[pltpu.HBM(a.shape, a.dtype) for a in lands],
        input_output_aliases={t: t for t in range(n)},
        compiler_params=_cp(has_side_effects=pltpu.SideEffectType.DATAFLOW_SIDE_EFFECTING),
    )(*lands, fill["send"], fill["recv"], *after)
    return list(outs)


def _sum_slots(land, name):
    _, r, c = land.shape
    vm = pl.BlockSpec(memory_space=pltpu.VMEM)

    def body(l_ref, sum_ref):
        total = l_ref[0]
        for k in range(1, 8):
            total = total + l_ref[k]
        sum_ref[...] = total

    return pl.pallas_call(body, name=name, in_specs=[vm], out_specs=vm, out_shape=jax.ShapeDtypeStruct((r, c), F32))(land)


def _pack(items):
    rows = []
    for a in items:
        flat = a.astype(F32).reshape(-1)
        pad = (-flat.shape[0]) % PACK_W
        rows.append(jnp.pad(flat, (0, pad)).reshape(-1, PACK_W))
    out = jnp.concatenate(rows, axis=0)
    pad_r = (-out.shape[0]) % 8
    return jnp.pad(out, ((0, pad_r), (0, 0)))


def _unpack(pack, shapes):
    outs, row = [], 0
    for s in shapes:
        n = int(np.prod(s))
        nr = -(-n // PACK_W)
        outs.append(pack[row:row + nr].reshape(-1)[:n].reshape(s))
        row += nr
    return outs


def _heads_to_member_major(w, axis):
    shp = w.shape
    pre, post = shp[:axis], shp[axis + 1:]
    w4 = w.reshape(pre + (N_KV_HEADS, GROUP, HEAD_DIM) + post)
    w4 = jnp.swapaxes(w4, len(pre), len(pre) + 1)
    return w4.reshape(shp)


def _heads_to_kv_major(w, axis):
    shp = w.shape
    pre, post = shp[:axis], shp[axis + 1:]
    w4 = w.reshape(pre + (GROUP, N_KV_HEADS, HEAD_DIM) + post)
    w4 = jnp.swapaxes(w4, len(pre), len(pre) + 1)
    return w4.reshape(shp)


def kernel(x, mem, norm_mix, norm_ffn, a_w_in, a_conv_w, a_w_out, kv_norm, w_kv, b_w_q, b_sinks, b_w_out, rel_bias, mem_norm, w_mem_kv, w_gate, w_up, w_down, final_norm, loss_target, m_norm_mix, m_norm_ffn, m_a_w_in, m_a_conv_w, m_a_w_out, m_kv_norm, m_w_kv, m_b_w_q, m_b_sinks, m_b_w_out, m_rel_bias, m_mem_norm, m_w_mem_kv, m_w_gate, m_w_up, m_w_down, m_final_norm, v_norm_mix, v_norm_ffn, v_a_w_in, v_a_conv_w, v_a_w_out, v_kv_norm, v_w_kv, v_b_w_q, v_b_sinks, v_b_w_out, v_rel_bias, v_mem_norm, v_w_mem_kv, v_w_gate, v_w_up, v_w_down, v_final_norm):
    t, d = x.shape[1], x.shape[2]
    tm = 512 if t % 512 == 0 and t >= 2048 else 256
    tl = 2 * tm if t % (2 * tm) == 0 else tm
    x0 = x.reshape(t, d)
    target = loss_target.reshape(t, d)
    mem2 = mem.reshape(mem.shape[1], d)
    n_mem = mem2.shape[0]
    ax, ay, ac = _place()
    chip = 2 * ax + ay
    cwid = a_conv_w.shape[2] * N_CHIPS
    qw = N_Q_HEADS * HEAD_DIM
    nq = N_CHIPS

    def landing(piece):
        return lax.empty((nq,) + piece.shape, piece.dtype)

    def mixer_shards(i):
        if i < N_A:
            shards = [a_w_in[i], a_w_out[i]] + ([w_mem_kv] if i == 0 else [])
        else:
            j = i - N_A
            shards = [b_w_q[j], b_w_out[j]] + ([w_kv] if j == 0 else [])
        return [a.astype(BF16) for a in shards]

    def ffn_shards(i):
        return [w_gate[i].T.astype(BF16), w_up[i].T.astype(BF16), w_down[i].astype(BF16)]

    conv_pad = jnp.pad(a_conv_w, ((0, 0), (0, 8 - a_conv_w.shape[1]), (0, (-a_conv_w.shape[2]) % 128)))
    first = mixer_shards(0)
    group_shards = {"0a": first[0:1], "0b": first[1:] + [conv_pad], "0f": ffn_shards(0)}
    for i in range(1, DEPTH):
        group_shards[str(i)] = ffn_shards(i) + mixer_shards(i)
    gathers, prev_tok = {}, []

    def start_group(key, after):
        shards = group_shards[key]
        gathers[key] = _exchange_start(shards, [landing(a) for a in shards], True, [None] * len(shards), False,
                                       after, "gather_start_" + key)
        return [gathers[key]["token"]]

    for key in ("0a", "0b", "0f"):
        prev_tok = start_group(key, prev_tok)

    def rows_full(g):
        return g.reshape((-1,) + g.shape[2:])

    def cols_full(g):
        return jnp.transpose(g, (1, 0, 2)).reshape(g.shape[1], -1)

    def landed_weights(key, after):
        g = gathers[key]
        return _exchange_wait([g], g["lands"], [list(range(len(g["lands"])))], after, "gather_wait_" + key)

    def mixer_weights(i, got):
        w_first, w_out = (cols_full(got[0]) if i < N_A else rows_full(got[0])), rows_full(got[1])
        if i >= N_A:
            w_first = jnp.concatenate([_heads_to_member_major(w_first[:, :qw], 1), w_first[:, qw:]], axis=1)
            w_out = jnp.concatenate([_heads_to_member_major(w_out[:qw, :], 0), w_out[qw:, :]], axis=0)
        return dict(w_first=w_first, w_out=w_out, extra=got[2] if len(got) > 2 else None)

    def ffn_weights(got):
        return dict(wg=rows_full(got[0]), wu=rows_full(got[1]), wd=rows_full(got[2]))

    bias, sinkt = [], []
    for j in range(2):
        bj, sj = _bias_tables(rel_bias, b_sinks[j], "bias_tables")
        bias.append(bj)
        sinkt.append(sj)

    ws = []
    xs, xmids, projs, cats, gates, ups = [x0], [], [], [], [], []
    kv = memkv = wmem = wkv = None
    for i in range(DEPTH):
        xin = xs[-1]
        if i == 0:
            w = dict(w_first=cols_full(landed_weights("0a", prev_tok)[0]))
        else:
            got = landed_weights(str(i), [xin])
            w = dict(mixer_weights(i, got[3:]), **ffn_weights(got[0:3]))
        ws.append(w)
        gm = norm_mix[i].reshape(1, d)
        if i < N_A:
            proj = _norm_mm(xin, gm, w["w_first"], tl, "proj_a")
            if i == 0:
                for key in ("1", "2", "3"):
                    prev_tok = start_group(key, prev_tok + [proj])
                got = landed_weights("0b", prev_tok)
                w["w_out"] = rows_full(got[0])
                full_mem = jnp.swapaxes(got[1], 0, 1).reshape(DEPTH, d, -1)
                wmem = jnp.transpose(full_mem, (1, 0, 2)).reshape(d, -1)
                memkv = _norm_mm(mem2, mem_norm.reshape(1, d), wmem, n_mem, "mem_kv")
                taps = got[2][:, :, 0:3, 0:a_conv_w.shape[2]]
                conv_full = jnp.transpose(taps, (1, 2, 0, 3)).reshape(N_A, 3, cwid)
            xmid, cat = _mix_a_fwd(xin, proj, conv_full[i], memkv, i, w["w_out"], tm, "mix_a_fwd")
        else:
            j = i - N_A
            if j == 0:
                wkv = rows_full(w["extra"])
                proj, kv = _norm_mm_pair(xin, gm, w["w_first"], kv_norm.reshape(1, d), wkv, tl, "proj_b_kv")
            else:
                proj = _norm_mm(xin, gm, w["w_first"], tl, "proj_b")
            xmid, cat = _mix_b_fwd(xin, proj, kv, bias[j], sinkt[j], memkv, i, w["w_out"], tm, "mix_b_fwd")
        if i == 0:
            w.update(ffn_weights(landed_weights("0f", [xmid])))
        if i < DEPTH - 1:
            xout, gate, up = _ffn_fwd(xmid, norm_ffn[i].reshape(1, d), w["wg"], w["wu"], w["wd"], tm, "ffn_fwd")
        else:
            xout = None
            gate, up, loss_part, dx, dg_final = _ffn_fwd_loss(xmid, norm_ffn[i].reshape(1, d), w["wg"], w["wu"], w["wd"],
                                                              final_norm.reshape(1, d), target, tm, "ffn_fwd_loss")
        projs.append(proj)
        cats.append(cat)
        xmids.append(xmid)
        gates.append(gate)
        ups.append(up)
        xs.append(xout)

    def rows_pieces(g):
        return g.astype(BF16).reshape((nq, g.shape[0] // nq) + g.shape[1:])

    def cols_pieces(g):
        return jnp.transpose(g.astype(BF16).reshape(g.shape[0], nq, g.shape[1] // nq), (1, 0, 2))

    swapped = ("w_gate", "w_up")
    stacked = dict(a_w_in=a_w_in, a_w_out=a_w_out, w_kv=w_kv[None], b_w_q=b_w_q, b_w_out=b_w_out,
                   w_mem_kv=w_mem_kv, w_gate=jnp.swapaxes(w_gate, 1, 2), w_up=jnp.swapaxes(w_up, 1, 2), w_down=w_down)
    names = list(stacked)
    land = {k: lax.empty((2 * nq,) + stacked[k].shape, BF16) for k in names}
    scatters, scatter_ids = [], []

    def scatter_start(key, items, both_cores):
        keys = [k for k, _, _ in items]
        st = _exchange_start([p for _, _, p in items], [land[k] for k in keys], False, [l for _, l, _ in items],
                             both_cores, [], "scatter_start_" + key)
        for k, ld in zip(keys, st["lands"]):
            land[k] = ld
        scatters.append(st)
        scatter_ids.append([names.index(k) for k in keys])
        return st["token"][0:1, 0:1]

    g_norm_mix, g_norm_ffn = [None] * DEPTH, [None] * DEPTH
    g_conv, g_sinks = [None] * 2, [None] * 2
    dmemkv = [None] * DEPTH
    dbias, dkv_main, dkv_halo = [None] * 2, [None] * 2, [None] * 2
    g_kv_norm = None
    tok = jnp.zeros((1, 1), F32)
    for i in reversed(range(DEPTH)):
        w = ws[i]
        dxm, dgate, dup, act, h2, dgf = _ffn_bwd(dx, xmids[i], norm_ffn[i].reshape(1, d) + tok, gates[i], ups[i],
                                            w["wg"], w["wu"], w["wd"], tm // 2, "ffn_bwd")
        g_norm_ffn[i] = dgf
        g_wd = _wgrad(act, dx, 2 * tm, "wgrad_down")
        g_wg = _wgrad(dgate, h2, 2 * tm, "wgrad_gate")
        g_wu = _wgrad(dup, h2, 2 * tm, "wgrad_up")
        items = [("w_gate", i, rows_pieces(g_wg)), ("w_up", i, rows_pieces(g_wu)), ("w_down", i, rows_pieces(g_wd))]
        if i == 0:
            tok = scatter_start("0f", items, True)
            items = []
        gm = norm_mix[i].reshape(1, d)
        if i < N_A:
            dproj, dcw, dmemkv[i] = _mix_a_bwd(dxm, projs[i], conv_full[i] + (tok if i == 0 else 0.0), memkv, i,
                                                    w["w_out"], tm, "mix_a_bwd")
            g_conv[i] = dcw[0:3]
            g_out = _wgrad(cats[i], dxm, 2 * tm, "wgrad_out")
            if i == 0:
                dmemkv_all = jnp.concatenate([a.astype(BF16) for a in dmemkv], axis=1)
                _, g_mem_norm, hmem = _mm_nt_normbwd(dmemkv_all, wmem, mem2, mem_norm.reshape(1, d),
                                                     jnp.zeros((n_mem, d), F32), n_mem, "mem_kv_bwd")
                g_wmem = _wgrad(hmem, dmemkv_all, n_mem, "wgrad_mem")
                g_wmem = jnp.transpose(g_wmem.reshape(nq, d // nq, DEPTH, -1), (0, 2, 1, 3))
                gm = gm + scatter_start("0o", [("a_w_out", 0, rows_pieces(g_out)), ("w_mem_kv", None, g_wmem)], False)
            dx, g_norm_mix[i], h = _mm_nt_normbwd(dproj, w["w_first"], xs[i], gm, dxm, tl, "proj_a_bwd")
            g_in = _wgrad(h, dproj, 2 * tm, "wgrad_in_a")
            items.append(("a_w_in", i, cols_pieces(g_in)))
            if i > 0:
                items.append(("a_w_out", i, rows_pieces(g_out)))
        else:
            j = i - N_A
            dqp, dkv_main[j], dkv_halo[j], dbias[j], dsk, dmemkv[i] = _mix_b_bwd(
                dxm, projs[i], kv, bias[j], sinkt[j], memkv, i, w["w_out"], tm, "mix_b_bwd")
            g_sinks[j] = dsk[0, 0:N_Q_HEADS].reshape(GROUP, N_KV_HEADS).T.reshape(N_Q_HEADS)
            g_out = _wgrad(cats[i], dxm, 2 * tm, "wgrad_out")
            if j == 0:
                dx, g_norm_mix[i], g_kv_norm, h, hkv, dkv = _mm_nt_normbwd_pair(
                    dqp, w["w_first"], gm, (dkv_main[0], dkv_halo[0], dkv_main[1], dkv_halo[1]), wkv,
                    kv_norm.reshape(1, d), xs[i], dxm, tm, "proj_b_kv_bwd")
            else:
                dx, g_norm_mix[i], h = _mm_nt_normbwd(dqp, w["w_first"], xs[i], gm, dxm, tl, "proj_b_bwd")
            g_q = _wgrad(h, dqp, 2 * tm, "wgrad_in_b")
            g_q = jnp.concatenate([_heads_to_kv_major(g_q[:, :qw], 1), g_q[:, qw:]], axis=1)
            g_out = jnp.concatenate([_heads_to_kv_major(g_out[:qw, :], 0), g_out[qw:, :]], axis=0)
            items += [("b_w_q", j, rows_pieces(g_q)), ("b_w_out", j, rows_pieces(g_out))]
            if j == 0:
                items.append(("w_kv", 0, rows_pieces(_wgrad(hkv, dkv, 2 * tm, "wgrad_kv"))))
        tok = scatter_start(str(i) if i else "0i", items, i > 0)
    grad_x = dx.reshape(x.shape)
    g_rel = _bias_bwd(dbias[0], dbias[1], "bias_bwd")[:, 0:N_Q_HEADS]

    small_shapes = [(DEPTH, d), (DEPTH, d), (d,), (d,), (d,), (2, N_Q_HEADS), (REL_BUCKETS, N_Q_HEADS),
                    (N_A, 3, cwid), ()]
    small = _pack([jnp.concatenate(g_norm_mix, axis=0), jnp.concatenate(g_norm_ffn, axis=0), g_kv_norm, g_mem_norm,
                   dg_final, jnp.stack(g_sinks), g_rel, jnp.stack(g_conv), loss_part[0, 0]])
    small_grp = _exchange_start([small], [lax.empty((8,) + small.shape, F32)], "all", [None], True,
                                [scatters[-1]["token"]], "reduce_small_start")

    late = ("a_w_in", "a_w_out", "w_mem_kv")
    landed = dict(zip(names, _exchange_wait(scatters[:-2], [land[k] for k in names], scatter_ids[:-2],
                                            [small_grp["token"]], "scatter_wait_a")))

    weights = dict(norm_mix=norm_mix, norm_ffn=norm_ffn, a_w_in=a_w_in, a_conv_w=a_conv_w, a_w_out=a_w_out,
                   kv_norm=kv_norm, w_kv=w_kv, b_w_q=b_w_q, b_sinks=b_sinks, b_w_out=b_w_out, rel_bias=rel_bias,
                   mem_norm=mem_norm, w_mem_kv=w_mem_kv, w_gate=w_gate, w_up=w_up, w_down=w_down,
                   final_norm=final_norm)
    moms = dict(norm_mix=m_norm_mix, norm_ffn=m_norm_ffn, a_w_in=m_a_w_in, a_conv_w=m_a_conv_w, a_w_out=m_a_w_out,
                kv_norm=m_kv_norm, w_kv=m_w_kv, b_w_q=m_b_w_q, b_sinks=m_b_sinks, b_w_out=m_b_w_out,
                rel_bias=m_rel_bias, mem_norm=m_mem_norm, w_mem_kv=m_w_mem_kv, w_gate=m_w_gate, w_up=m_w_up,
                w_down=m_w_down, final_norm=m_final_norm)
    vars_ = dict(norm_mix=v_norm_mix, norm_ffn=v_norm_ffn, a_w_in=v_a_w_in, a_conv_w=v_a_conv_w, a_w_out=v_a_w_out,
                 kv_norm=v_kv_norm, w_kv=v_w_kv, b_w_q=v_b_w_q, b_sinks=v_b_sinks, b_w_out=v_b_w_out,
                 rel_bias=v_rel_bias, mem_norm=v_mem_norm, w_mem_kv=v_w_mem_kv, w_gate=v_w_gate, w_up=v_w_up,
                 w_down=v_w_down, final_norm=v_final_norm)
    order = list(weights)
    grads, deltas, new_m, new_v = {}, {}, {}, {}
    def adamw(k, ld, after=()):
        shp, stk = weights[k].shape, ld.shape[1:]
        view = (lambda a: jnp.swapaxes(a, 1, 2)) if k in swapped else (lambda a: a.reshape(stk))
        back = (lambda a: jnp.swapaxes(a, 1, 2)) if k in swapped else (lambda a: a.reshape(shp))
        outs = _adamw_sharded(view(weights[k]), ld, view(moms[k]), view(vars_[k]), "adamw_" + k, after)
        grads[k], deltas[k], new_m[k], new_v[k] = [back(o) for o in outs]

    early = [k for k in names if k not in late and k != "w_down"]
    for k in early:
        adamw(k, landed[k])
    late_ids = [[late.index(names[t]) for t in ids] for ids in scatter_ids[-2:]]
    late_landed = _exchange_wait(scatters[-2:] + [small_grp], [landed[k] for k in late] + small_grp["lands"],
                                 late_ids + [[len(late)]], [deltas[early[-1]]], "scatter_wait_b")
    fill = _core_fill_start(late_landed[:-1], [0, 0, None], [], "fill_cores_start")
    adamw("w_down", landed["w_down"], [fill["token"]])
    filled = _core_fill_wait(fill, [deltas["w_down"]], "fill_cores_wait")
    for k, ld in zip(late, filled):
        adamw(k, ld)
    small_sum = _sum_slots(late_landed[-1], "reduce_small_sum")
    (gs_norm_mix, gs_norm_ffn, gs_kv_norm, gs_mem_norm, gs_final, gs_sinks, gs_rel, gs_conv_full, loss) = _unpack(
        small_sum, small_shapes)
    cq = cwid // N_CHIPS
    gs_conv = lax.dynamic_slice_in_dim(gs_conv_full, chip * cq, cq, axis=2)
    small_names = ["norm_mix", "norm_ffn", "kv_norm", "mem_norm", "final_norm", "b_sinks", "rel_bias", "a_conv_w"]
    small_g = [gs_norm_mix, gs_norm_ffn, gs_kv_norm, gs_mem_norm, gs_final, gs_sinks, gs_rel, gs_conv]
    shapes = [weights[k].shape for k in small_names]
    dl_p, m_p, v_p = _adamw_packed(_pack([weights[k] for k in small_names]), _pack(small_g),
                                   _pack([moms[k] for k in small_names]), _pack([vars_[k] for k in small_names]),
                                   "adamw_small")
    for k, g, dl, m2, v2 in zip(small_names, small_g, _unpack(dl_p, shapes), _unpack(m_p, shapes), _unpack(v_p, shapes)):
        grads[k], deltas[k], new_m[k], new_v[k] = g.reshape(weights[k].shape), dl, m2, v2

    return (loss, grad_x, *[grads[k] for k in order], *[deltas[k] for k in order],
            *[new_m[k] for k in order], *[new_v[k] for k in order])
```

```python
import functools
import math

import numpy as np
import jax
import jax.numpy as jnp
from jax import lax
from jax.experimental import pallas as pl
from jax.experimental.pallas import tpu as pltpu

F32 = jnp.float32
BF16 = jnp.bfloat16
MESH = pl.DeviceIdType.MESH

EPS = 1e-5
HEAD_DIM = 64
N_MEM_HEADS = 4
N_KV_HEADS = 4
GROUP = 3
N_Q_HEADS = N_KV_HEADS * GROUP
BLOCK = 128
REL_BUCKETS = 32
REL_MAX_DIST = 128
SCALE = HEAD_DIM ** -0.5
NEG = -1e30
N_CHIPS = 4
N_A = 2
DEPTH = 4

ADAM_LR = 0.001
ADAM_B1 = 0.9
ADAM_B2 = 0.999
ADAM_EPS = 1e-08
ADAM_WD = 0.01
ADAM_STEP = 10

VMEM_BIG = 56 * 1024 * 1024
PACK_W = 1024

NT = (((1,), (1,)), ((), ()))
TN = (((0,), (0,)), ((), ()))


def _cp(sem=None, vmem=None, **kw):
    return pltpu.CompilerParams(dimension_semantics=sem, vmem_limit_bytes=vmem, **kw)


def _const_spec(shape):
    nd = len(shape)
    return pl.BlockSpec(shape, lambda i, _n=nd: (0,) * _n, pipeline_mode=pl.Buffered(1))


def _row_spec(tm, n):
    return pl.BlockSpec((tm, n), lambda i: (i, 0))


def _rms_parts(xv):
    r = lax.rsqrt(jnp.mean(xv * xv, axis=-1, keepdims=True) + EPS)
    return xv * r, r


def _sigmoid(z):
    return 1.0 / (1.0 + jnp.exp(-z))


def _ff_chunks(f):
    if f % 512 == 0 or f % 256 != 0:
        return [(0, f)] if f <= 1536 else [(0, f // 2), (f // 2, f - f // 2)]
    n = f // 256
    a = (n + 1) // 2 * 256
    return [(0, a), (a, f - a)]


def _norm_mm(x, g, w, tm, name):
    t, d = x.shape
    n = w.shape[1]

    def body(x_ref, g_ref, w_ref, o_ref):
        xhat, _ = _rms_parts(x_ref[...])
        h = (xhat * g_ref[...]).astype(BF16)
        o_ref[...] = jnp.dot(h, w_ref[...], preferred_element_type=F32).astype(BF16)

    return pl.pallas_call(
        body, name=name, grid=(t // tm,),
        in_specs=[_row_spec(tm, d), _const_spec((1, d)), _const_spec((d, n))],
        out_specs=_row_spec(tm, n),
        out_shape=jax.ShapeDtypeStruct((t, n), BF16),
        compiler_params=_cp(("parallel",), VMEM_BIG),
    )(x, g, w)


def _mm_nt_normbwd(dproj, w, x_in, g, dres, tm, name):
    t, d = x_in.shape
    n = w.shape[1]

    def body(dp_ref, w_ref, x_ref, g_ref, dr_ref, dx_ref, dg_ref, h_ref):
        i = pl.program_id(0)
        xhat, r = _rms_parts(x_ref[...])
        gv = g_ref[...]
        h_ref[...] = (xhat * gv).astype(BF16)
        dh = lax.dot_general(dp_ref[...], w_ref[...], NT, preferred_element_type=F32)
        dxhat = dh * gv
        dx = r * (dxhat - xhat * jnp.mean(dxhat * xhat, axis=-1, keepdims=True))
        dx_ref[...] = dr_ref[...] + dx

        @pl.when(i == 0)
        def _():
            dg_ref[...] = jnp.zeros_like(dg_ref)

        dg_ref[...] += jnp.sum(dh * xhat, axis=0, keepdims=True)

    return pl.pallas_call(
        body, name=name, grid=(t // tm,),
        in_specs=[_row_spec(tm, n), _const_spec((d, n)), _row_spec(tm, d), _const_spec((1, d)), _row_spec(tm, d)],
        out_specs=[_row_spec(tm, d), pl.BlockSpec((1, d), lambda i: (0, 0)), _row_spec(tm, d)],
        out_shape=[jax.ShapeDtypeStruct((t, d), F32), jax.ShapeDtypeStruct((1, d), F32),
                   jax.ShapeDtypeStruct((t, d), BF16)],
        compiler_params=_cp(("arbitrary",), VMEM_BIG),
    )(dproj, w, x_in, g, dres)


def _norm_mm_pair(x, g_a, w_a, g_b, w_b, tm, name):
    t, d = x.shape
    na, nb = w_a.shape[1], w_b.shape[1]

    def body(x_ref, ga_ref, wa_ref, gb_ref, wb_ref, oa_ref, ob_ref):
        xhat, _ = _rms_parts(x_ref[...])
        ha = (xhat * ga_ref[...]).astype(BF16)
        hb = (xhat * gb_ref[...]).astype(BF16)
        oa_ref[...] = jnp.dot(ha, wa_ref[...], preferred_element_type=F32).astype(BF16)
        ob_ref[...] = jnp.dot(hb, wb_ref[...], preferred_element_type=F32).astype(BF16)

    return pl.pallas_call(
        body, name=name, grid=(t // tm,),
        in_specs=[_row_spec(tm, d), _const_spec((1, d)), _const_spec((d, na)), _const_spec((1, d)), _const_spec((d, nb))],
        out_specs=[_row_spec(tm, na), _row_spec(tm, nb)],
        out_shape=[jax.ShapeDtypeStruct((t, na), BF16), jax.ShapeDtypeStruct((t, nb), BF16)],
        compiler_params=_cp(("parallel",), VMEM_BIG),
    )(x, g_a, w_a, g_b, w_b)


def _mm_nt_normbwd_pair(dp_a, w_a, g_a, kv_parts, w_b, g_b, x_in, dres, tm, name):
    t, d = x_in.shape
    na, nb = w_a.shape[1], w_b.shape[1]
    nt = t // tm
    main_1, halo_1, main_2, halo_2 = kv_parts

    def body(dpa_ref, wa_ref, ga_ref, m1_ref, h1_ref, m2_ref, h2_ref, wb_ref, gb_ref, x_ref, dr_ref,
             dx_ref, dga_ref, dgb_ref, ha_ref, hb_ref, dkv_ref):
        i = pl.program_id(0)
        s = m1_ref[...] + m2_ref[...]
        tail = jnp.where(i == nt - 1, 0.0, h1_ref[0] + h2_ref[0])
        dkv = jnp.concatenate([s[0:tm - BLOCK], s[tm - BLOCK:] + tail], axis=0).astype(BF16)
        dkv_ref[...] = dkv
        xhat, r = _rms_parts(x_ref[...])
        ga, gb = ga_ref[...], gb_ref[...]
        ha_ref[...] = (xhat * ga).astype(BF16)
        hb_ref[...] = (xhat * gb).astype(BF16)
        dha = lax.dot_general(dpa_ref[...], wa_ref[...], NT, preferred_element_type=F32)
        dhb = lax.dot_general(dkv, wb_ref[...], NT, preferred_element_type=F32)
        dxhat = dha * ga + dhb * gb
        dx_ref[...] = dr_ref[...] + r * (dxhat - xhat * jnp.mean(dxhat * xhat, axis=-1, keepdims=True))

        @pl.when(i == 0)
        def _():
            dga_ref[...] = jnp.zeros_like(dga_ref)
            dgb_ref[...] = jnp.zeros_like(dgb_ref)

        dga_ref[...] += jnp.sum(dha * xhat, axis=0, keepdims=True)
        dgb_ref[...] += jnp.sum(dhb * xhat, axis=0, keepdims=True)

    halo_spec = pl.BlockSpec((1, BLOCK, nb), lambda i: (jnp.minimum(i + 1, nt - 1), 0, 0))
    row1 = pl.BlockSpec((1, d), lambda i: (0, 0))
    return pl.pallas_call(
        body, name=name, grid=(nt,),
        in_specs=[_row_spec(tm, na), _const_spec((d, na)), _const_spec((1, d)), _row_spec(tm, nb), halo_spec,
                  _row_spec(tm, nb), halo_spec, _const_spec((d, nb)), _const_spec((1, d)), _row_spec(tm, d),
                  _row_spec(tm, d)],
        out_specs=[_row_spec(tm, d), row1, row1, _row_spec(tm, d), _row_spec(tm, d), _row_spec(tm, nb)],
        out_shape=[jax.ShapeDtypeStruct((t, d), F32), jax.ShapeDtypeStruct((1, d), F32), jax.ShapeDtypeStruct((1, d), F32),
                   jax.ShapeDtypeStruct((t, d), BF16), jax.ShapeDtypeStruct((t, d), BF16),
                   jax.ShapeDtypeStruct((t, nb), BF16)],
        compiler_params=_cp(("arbitrary",), VMEM_BIG),
    )(dp_a, w_a, g_a, main_1, halo_1, main_2, halo_2, w_b, g_b, x_in, dres)


def _ffn_fwd(x, g, wg, wu, wd, tm, name):
    t, d = x.shape
    f = wg.shape[0]
    chunks = _ff_chunks(f)

    def body(x_ref, g_ref, wg_ref, wu_ref, wd_ref, xo_ref, gate_ref, up_ref):
        xv = x_ref[...]
        xhat, _ = _rms_parts(xv)
        h = (xhat * g_ref[...]).astype(BF16)
        acc = xv
        for c0, cw in chunks:
            gt = lax.dot_general(h, wg_ref[c0:c0 + cw, :], NT, preferred_element_type=F32)
            ut = lax.dot_general(h, wu_ref[c0:c0 + cw, :], NT, preferred_element_type=F32)
            gate_ref[:, c0:c0 + cw] = gt.astype(BF16)
            up_ref[:, c0:c0 + cw] = ut.astype(BF16)
            a = (gt * _sigmoid(gt) * ut).astype(BF16)
            acc = acc + jnp.dot(a, wd_ref[c0:c0 + cw, :], preferred_element_type=F32)
        xo_ref[...] = acc

    return pl.pallas_call(
        body, name=name, grid=(t // tm,),
        in_specs=[_row_spec(tm, d), _const_spec((1, d)), _const_spec((f, d)), _const_spec((f, d)), _const_spec((f, d))],
        out_specs=[_row_spec(tm, d), _row_spec(tm, f), _row_spec(tm, f)],
        out_shape=[jax.ShapeDtypeStruct((t, d), F32), jax.ShapeDtypeStruct((t, f), BF16),
                   jax.ShapeDtypeStruct((t, f), BF16)],
        compiler_params=_cp(("parallel",), VMEM_BIG),
    )(x, g, wg, wu, wd)


def _ffn_fwd_loss(x, g, wg, wu, wd, g_final, target, tm, name):
    t, d = x.shape
    f = wg.shape[0]
    chunks = _ff_chunks(f)

    def body(x_ref, g_ref, wg_ref, wu_ref, wd_ref, gf_ref, t_ref, gate_ref, up_ref, loss_ref, dx_ref, dgf_ref):
        i = pl.program_id(0)
        xv = x_ref[...]
        xhat, _ = _rms_parts(xv)
        h = (xhat * g_ref[...]).astype(BF16)
        acc = xv
        for c0, cw in chunks:
            gt = lax.dot_general(h, wg_ref[c0:c0 + cw, :], NT, preferred_element_type=F32)
            ut = lax.dot_general(h, wu_ref[c0:c0 + cw, :], NT, preferred_element_type=F32)
            gate_ref[:, c0:c0 + cw] = gt.astype(BF16)
            up_ref[:, c0:c0 + cw] = ut.astype(BF16)
            a = (gt * _sigmoid(gt) * ut).astype(BF16)
            acc = acc + jnp.dot(a, wd_ref[c0:c0 + cw, :], preferred_element_type=F32)
        xhat_o, r_o = _rms_parts(acc)
        gf = gf_ref[...]
        err = xhat_o * gf - t_ref[...]
        dy = err * (1.0 / d)
        dxhat = dy * gf
        dx_ref[...] = r_o * (dxhat - xhat_o * jnp.mean(dxhat * xhat_o, axis=-1, keepdims=True))

        @pl.when(i == 0)
        def _():
            dgf_ref[...] = jnp.zeros_like(dgf_ref)
            loss_ref[...] = jnp.zeros_like(loss_ref)

        dgf_ref[...] += jnp.sum(dy * xhat_o, axis=0, keepdims=True)
        part = jnp.sum(jnp.sum(err * err, axis=-1, keepdims=True), axis=0, keepdims=True) * (0.5 / d)
        loss_ref[...] += jnp.broadcast_to(part, loss_ref.shape)

    return pl.pallas_call(
        body, name=name, grid=(t // tm,),
        in_specs=[_row_spec(tm, d), _const_spec((1, d)), _const_spec((f, d)), _const_spec((f, d)), _const_spec((f, d)),
                  _const_spec((1, d)), _row_spec(tm, d)],
        out_specs=[_row_spec(tm, f), _row_spec(tm, f), pl.BlockSpec((8, 128), lambda i: (0, 0)), _row_spec(tm, d),
                   pl.BlockSpec((1, d), lambda i: (0, 0))],
        out_shape=[jax.ShapeDtypeStruct((t, f), BF16), jax.ShapeDtypeStruct((t, f), BF16),
                   jax.ShapeDtypeStruct((8, 128), F32), jax.ShapeDtypeStruct((t, d), F32),
                   jax.ShapeDtypeStruct((1, d), F32)],
        compiler_params=_cp(("arbitrary",), VMEM_BIG),
    )(x, g, wg, wu, wd, g_final, target)


def _ffn_bwd(dxo, xm, g, gate, up, wg, wu, wd, tm, name):
    t, d = xm.shape
    f = wg.shape[0]
    chunks = _ff_chunks(f)

    def body(dxo_ref, xm_ref, g_ref, gate_ref, up_ref, wg_ref, wu_ref, wd_ref,
             dxm_ref, dgate_ref, dup_ref, act_ref, h2_ref, dg_ref):
        i = pl.program_id(0)
        dxo_v = dxo_ref[...]
        dxo_b = dxo_v.astype(BF16)
        xhat, r = _rms_parts(xm_ref[...])
        gv = g_ref[...]
        h2_ref[...] = (xhat * gv).astype(BF16)
        dh = jnp.zeros((tm, d), F32)
        for c0, cw in chunks:
            dact = lax.dot_general(dxo_b, wd_ref[c0:c0 + cw, :], NT, preferred_element_type=F32)
            gt = gate_ref[:, c0:c0 + cw].astype(F32)
            ut = up_ref[:, c0:c0 + cw].astype(F32)
            sg = _sigmoid(gt)
            sl = gt * sg
            act_ref[:, c0:c0 + cw] = (sl * ut).astype(BF16)
            dgt = (dact * ut * (sg * (1.0 + gt * (1.0 - sg)))).astype(BF16)
            dut = (dact * sl).astype(BF16)
            dgate_ref[:, c0:c0 + cw] = dgt
            dup_ref[:, c0:c0 + cw] = dut
            dh = dh + jnp.dot(dgt, wg_ref[c0:c0 + cw, :], preferred_element_type=F32)
            dh = dh + jnp.dot(dut, wu_ref[c0:c0 + cw, :], preferred_element_type=F32)
        dxhat = dh * gv
        dx = r * (dxhat - xhat * jnp.mean(dxhat * xhat, axis=-1, keepdims=True))
        dxm_ref[...] = dxo_v + dx

        @pl.when(i == 0)
        def _():
            dg_ref[...] = jnp.zeros_like(dg_ref)

        dg_ref[...] += jnp.sum(dh * xhat, axis=0, keepdims=True)

    return pl.pallas_call(
        body, name=name, grid=(t // tm,),
        in_specs=[_row_spec(tm, d), _row_spec(tm, d), _const_spec((1, d)), _row_spec(tm, f), _row_spec(tm, f),
                  _const_spec((f, d)), _const_spec((f, d)), _const_spec((f, d))],
        out_specs=[_row_spec(tm, d), _row_spec(tm, f), _row_spec(tm, f), _row_spec(tm, f), _row_spec(tm, d),
                   pl.BlockSpec((1, d), lambda i: (0, 0))],
        out_shape=[jax.ShapeDtypeStruct((t, d), F32), jax.ShapeDtypeStruct((t, f), BF16),
                   jax.ShapeDtypeStruct((t, f), BF16), jax.ShapeDtypeStruct((t, f), BF16),
                   jax.ShapeDtypeStruct((t, d), BF16), jax.ShapeDtypeStruct((1, d), F32)],
        compiler_params=_cp(("arbitrary",), VMEM_BIG),
    )(dxo, xm, g, gate, up, wg, wu, wd)


def _wgrad(a, b, tt, name):
    t, k = a.shape
    n = b.shape[1]
    nt = t // tt

    def body(a_ref, b_ref, o_ref, acc):
        i = pl.program_id(0)

        @pl.when(i == 0)
        def _():
            acc[...] = jnp.zeros_like(acc)

        acc[...] += lax.dot_general(a_ref[...].astype(BF16), b_ref[...].astype(BF16), TN,
                                    preferred_element_type=F32)

        @pl.when(i == nt - 1)
        def _():
            o_ref[...] = acc[...].astype(BF16)

    return pl.pallas_call(
        body, name=name, grid=(nt,),
        in_specs=[_row_spec(tt, k), _row_spec(tt, n)],
        out_specs=pl.BlockSpec((k, n), lambda i: (0, 0)),
        out_shape=jax.ShapeDtypeStruct((k, n), BF16),
        scratch_shapes=[pltpu.VMEM((k, n), F32)],
        compiler_params=_cp(("arbitrary",), VMEM_BIG),
    )(a, b)


def _col_head(width):
    return lax.broadcasted_iota(jnp.int32, (1, width), 1) // HEAD_DIM


def _keep_head(a, colh, h):
    return jnp.where(colh == h, a, jnp.zeros_like(a))


def _softmax_cols(s, sink=None):
    m = jnp.max(s, axis=0, keepdims=True)
    if sink is not None:
        m = jnp.maximum(m, sink)
    p = jnp.exp(s - m)
    l = jnp.sum(p, axis=0, keepdims=True)
    if sink is None:
        return p * (1.0 / l), None
    es = jnp.exp(sink - m)
    inv = 1.0 / (l + es)
    return p * inv, es * inv


def _add4(v):
    return (v[0] + v[1]) + (v[2] + v[3])


def _mem_attn_fwd(qm, mk, mv):
    colh = _col_head(mk.shape[1])
    mks = mk * SCALE
    heads = range(N_MEM_HEADS)
    ss = [lax.dot_general(_keep_head(mks, colh, h), qm, NT, preferred_element_type=F32) for h in heads]
    ps = [_softmax_cols(s)[0].astype(BF16) for s in ss]
    return _add4([lax.dot_general(ps[h], _keep_head(mv, colh, h), TN, preferred_element_type=F32) for h in heads])


def _mem_attn_bwd(qm, dy_b, mk, mv):
    colh = _col_head(mk.shape[1])
    mks = mk * SCALE
    heads = range(N_MEM_HEADS)
    khs = [_keep_head(mks, colh, h) for h in heads]
    vhs = [_keep_head(mv, colh, h) for h in heads]
    ss = [lax.dot_general(khs[h], qm, NT, preferred_element_type=F32) for h in heads]
    dps = [lax.dot_general(vhs[h], dy_b, NT, preferred_element_type=F32) for h in heads]
    pbs, dsbs = [], []
    for h in heads:
        p, _ = _softmax_cols(ss[h])
        ds = p * (dps[h] - jnp.sum(p * dps[h], axis=0, keepdims=True))
        pbs.append(p.astype(BF16))
        dsbs.append(ds.astype(BF16))
    dq = _add4([lax.dot_general(dsbs[h], khs[h], TN, preferred_element_type=F32) for h in heads])
    dmk = _add4([jnp.where(colh == h, jnp.dot(dsbs[h], qm, preferred_element_type=F32) * SCALE, 0.0) for h in heads])
    dmv = _add4([jnp.where(colh == h, jnp.dot(pbs[h], dy_b, preferred_element_type=F32), 0.0) for h in heads])
    return dq, dmk, dmv


def _shift_down(v, halo, k):
    rolled = pltpu.roll(v, k, 0)
    hrolled = pltpu.roll(halo, k, 0)[0:8]
    rows = lax.broadcasted_iota(jnp.int32, (8, v.shape[1]), 0)
    first = jnp.where(rows < k, hrolled, rolled[0:8])
    return jnp.concatenate([first, rolled[8:]], axis=0)


def _shift_up(v, halo, k):
    n = v.shape[0]
    rolled = pltpu.roll(v, n - k, 0)
    hrolled = pltpu.roll(halo, 8 - k, 0)[0:8]
    rows = lax.broadcasted_iota(jnp.int32, (8, v.shape[1]), 0)
    last = jnp.where(rows >= 8 - k, hrolled, rolled[n - 8:])
    return jnp.concatenate([rolled[:n - 8], last], axis=0)


def _conv_parts(p, ph, cw, first_tile, cwid):
    u = p[:, 0:cwid].astype(F32)
    bg = p[:, cwid:2 * cwid].astype(F32)
    cg = p[:, 2 * cwid:3 * cwid].astype(F32)
    v = cg * u
    vh = ph[:, 2 * cwid:3 * cwid].astype(F32) * ph[:, 0:cwid].astype(F32)
    vh = jnp.where(first_tile, 0.0, vh)
    v1 = _shift_down(v, vh, 1)
    v2 = _shift_down(v, vh, 2)
    conv = cw[0:1, :] * v2 + cw[1:2, :] * v1 + cw[2:3, :] * v
    return u, bg, cg, v, v1, v2, conv


def _halo_prev_spec(rows, n, tm):
    per = tm // rows
    return pl.BlockSpec((rows, n), lambda i: (jnp.maximum(i * per - 1, 0), 0))


def _halo_next_spec(rows, n, tm, t):
    per = tm // rows
    last = t // rows - 1
    return pl.BlockSpec((rows, n), lambda i: (jnp.minimum((i + 1) * per, last), 0))


def _mix_a_fwd(x, proj, convw, memkv, layer, wout, tm, name):
    t, d = x.shape
    n_mem = memkv.shape[0]
    mw = N_MEM_HEADS * HEAD_DIM
    cwid = d - mw
    pw = proj.shape[1]

    def body(x_ref, p_ref, ph_ref, cw_ref, mkv_ref, wo_ref, xo_ref, cat_ref):
        i = pl.program_id(0)
        p = p_ref[...]
        _, bg, _, _, _, _, conv = _conv_parts(p, ph_ref[...], cw_ref[...], i == 0, cwid)
        ytok = (bg * conv).astype(BF16)
        mkv = mkv_ref[...]
        ymem = _mem_attn_fwd(p[:, 3 * cwid:3 * cwid + mw], mkv[:, 0:mw], mkv[:, mw:2 * mw])
        cat = jnp.concatenate([ytok, ymem.astype(BF16)], axis=1)
        cat_ref[...] = cat
        xo_ref[...] = x_ref[...] + jnp.dot(cat, wo_ref[...], preferred_element_type=F32)

    return pl.pallas_call(
        body, name=name, grid=(t // tm,),
        in_specs=[_row_spec(tm, d), _row_spec(tm, pw), _halo_prev_spec(16, pw, tm), _const_spec((3, cwid)),
                  pl.BlockSpec((n_mem, 2 * mw), lambda i: (0, layer)), _const_spec((d, d))],
        out_specs=[_row_spec(tm, d), _row_spec(tm, d)],
        out_shape=[jax.ShapeDtypeStruct((t, d), F32), jax.ShapeDtypeStruct((t, d), BF16)],
        compiler_params=_cp(("parallel",), VMEM_BIG),
    )(x, proj, proj, convw, memkv, wout)


def _mix_a_bwd(dxm, proj, convw, memkv, layer, wout, tm, name):
    t, d = dxm.shape
    n_mem = memkv.shape[0]
    mw = N_MEM_HEADS * HEAD_DIM
    cwid = d - mw
    pw = proj.shape[1]
    nt = t // tm

    def body(dx_ref, dxn_ref, p_ref, ph_ref, pn_ref, cw_ref, mkv_ref, wo_ref,
             dp_ref, dcw_ref, dmkv_ref, dmk_acc, dmv_acc):
        i = pl.program_id(0)
        p = p_ref[...]
        cw = cw_ref[...]
        wo = wo_ref[...]
        u, bg, cg, v, v1, v2, conv = _conv_parts(p, ph_ref[...], cw, i == 0, cwid)
        dcat = lax.dot_general(dx_ref[...].astype(BF16), wo, NT, preferred_element_type=F32)
        dytok = dcat[:, 0:cwid]
        dymem_b = dcat[:, cwid:d].astype(BF16)
        pn = pn_ref[...]
        dcat_n = lax.dot_general(dxn_ref[...].astype(BF16), wo[0:cwid, :], NT, preferred_element_type=F32)
        dconv_n = jnp.where(i == nt - 1, 0.0, dcat_n * pn[:, cwid:2 * cwid].astype(F32))
        dbg = dytok * conv
        dconv = dytok * bg
        dv = cw[2:3, :] * dconv + cw[1:2, :] * _shift_up(dconv, dconv_n, 1) + cw[0:1, :] * _shift_up(dconv, dconv_n, 2)
        du = dv * cg
        dcg = dv * u
        rows8 = lax.broadcasted_iota(jnp.int32, (8, cwid), 0)
        dcw = (jnp.where(rows8 == 0, jnp.sum(dconv * v2, axis=0, keepdims=True), 0.0)
               + jnp.where(rows8 == 1, jnp.sum(dconv * v1, axis=0, keepdims=True), 0.0)
               + jnp.where(rows8 == 2, jnp.sum(dconv * v, axis=0, keepdims=True), 0.0))
        mkv = mkv_ref[...]
        qm = p[:, 3 * cwid:3 * cwid + mw]
        dqm, dmk, dmv = _mem_attn_bwd(qm, dymem_b, mkv[:, 0:mw], mkv[:, mw:2 * mw])
        dp_ref[...] = jnp.concatenate([du.astype(BF16), dbg.astype(BF16), dcg.astype(BF16), dqm.astype(BF16)], axis=1)

        @pl.when(i == 0)
        def _():
            dcw_ref[...] = jnp.zeros_like(dcw_ref)
            dmk_acc[...] = jnp.zeros_like(dmk_acc)
            dmv_acc[...] = jnp.zeros_like(dmv_acc)

        dcw_ref[...] += dcw
        dmk_acc[...] += dmk
        dmv_acc[...] += dmv

        @pl.when(i == nt - 1)
        def _():
            dmkv_ref[...] = jnp.concatenate([dmk_acc[...], dmv_acc[...]], axis=1)

    return pl.pallas_call(
        body, name=name, grid=(nt,),
        in_specs=[_row_spec(tm, d), _halo_next_spec(16, d, tm, t), _row_spec(tm, pw), _halo_prev_spec(16, pw, tm),
                  _halo_next_spec(16, pw, tm, t), _const_spec((3, cwid)),
                  pl.BlockSpec((n_mem, 2 * mw), lambda i: (0, layer)), _const_spec((d, d))],
        out_specs=[_row_spec(tm, pw), pl.BlockSpec((8, cwid), lambda i: (0, 0)),
                   pl.BlockSpec((n_mem, 2 * mw), lambda i: (0, 0))],
        out_shape=[jax.ShapeDtypeStruct((t, pw), BF16),
                   jax.ShapeDtypeStruct((8, cwid), F32), jax.ShapeDtypeStruct((n_mem, 2 * mw), F32)],
        scratch_shapes=[pltpu.VMEM((n_mem, mw), F32), pltpu.VMEM((n_mem, mw), F32)],
        compiler_params=_cp(("arbitrary",), VMEM_BIG),
    )(dxm, dxm, proj, proj, proj, convw, memkv, wout)


def _rel_tables():
    qi = np.arange(BLOCK, dtype=np.int32)[:, None]
    kj = np.arange(2 * BLOCK, dtype=np.int32)[None, :]
    dist = qi + BLOCK - kj
    inw = (dist >= 0) & (dist < BLOCK)
    max_exact = REL_BUCKETS // 2
    dd = np.maximum(np.maximum(dist, 0), 1).astype(np.float32)
    large = max_exact + (np.log(dd / np.float32(max_exact)) / np.float32(math.log(REL_MAX_DIST / max_exact))
                         * np.float32(REL_BUCKETS - max_exact)).astype(np.int32)
    large = np.minimum(large, REL_BUCKETS - 1)
    bucket = np.where(np.maximum(dist, 0) < max_exact, np.maximum(dist, 0), large)
    return np.where(inw, bucket, -1).astype(np.int32)


def _bias_tables(rel_bias, sinks, name):
    bucket_t = jnp.asarray(_rel_tables().T)

    def body(rb_ref, sk_ref, bk_ref, bias_ref, sink_ref):
        bk = bk_ref[...]
        prev = lax.broadcasted_iota(jnp.int32, bk.shape, 0) < BLOCK
        for h in range(N_KV_HEADS):
            for j in range(GROUP):
                head = GROUP * h + j
                acc = jnp.full(bk.shape, NEG, F32)
                for b in range(REL_BUCKETS):
                    acc = jnp.where(bk == b, rb_ref[b, head], acc)
                bias_ref[h, :, j * BLOCK:(j + 1) * BLOCK] = acc
                bias_ref[N_KV_HEADS + h, :, j * BLOCK:(j + 1) * BLOCK] = jnp.where(prev, NEG, acc)
                sink_ref[h, :, j * BLOCK:(j + 1) * BLOCK] = jnp.full((8, BLOCK), sk_ref[0, head], F32)

    smem = pl.BlockSpec(memory_space=pltpu.SMEM)
    return pl.pallas_call(
        body, name=name,
        in_specs=[smem, smem, pl.BlockSpec(memory_space=pltpu.VMEM)],
        out_specs=[pl.BlockSpec(memory_space=pltpu.VMEM), pl.BlockSpec(memory_space=pltpu.VMEM)],
        out_shape=[jax.ShapeDtypeStruct((2 * N_KV_HEADS, 2 * BLOCK, GROUP * BLOCK), F32),
                   jax.ShapeDtypeStruct((N_KV_HEADS, 8, GROUP * BLOCK), F32)],
    )(rel_bias, sinks.reshape(1, N_Q_HEADS), bucket_t)


def _bias_bwd(dbias_a, dbias_b, name):
    bucket_t = jnp.asarray(_rel_tables().T)

    def body(da_ref, db_ref, bk_ref, o_ref):
        bk = bk_ref[...]
        ri = lax.broadcasted_iota(jnp.int32, (REL_BUCKETS, 128), 0)
        ci = lax.broadcasted_iota(jnp.int32, (REL_BUCKETS, 128), 1)
        out = jnp.zeros((REL_BUCKETS, 128), F32)
        for h in range(N_KV_HEADS):
            dsum = da_ref[h] + db_ref[h]
            for j in range(GROUP):
                head = GROUP * h + j
                seg = dsum[:, j * BLOCK:(j + 1) * BLOCK]
                for b in range(REL_BUCKETS):
                    val = jnp.sum(jnp.sum(jnp.where(bk == b, seg, 0.0), axis=0, keepdims=True), axis=1, keepdims=True)
                    out = out + jnp.where((ri == b) & (ci == head), val, 0.0)
        o_ref[...] = out

    vm = pl.BlockSpec(memory_space=pltpu.VMEM)
    return pl.pallas_call(
        body, name=name, in_specs=[vm, vm, vm], out_specs=vm,
        out_shape=jax.ShapeDtypeStruct((REL_BUCKETS, 128), F32),
    )(dbias_a, dbias_b, bucket_t)


def _stack_members(ref, r0, width):
    blk = ref[pl.ds(r0, BLOCK), 0:GROUP * width]
    return jnp.concatenate([blk[:, j * width:(j + 1) * width] for j in range(GROUP)], axis=0)


def _mix_b_fwd(x, qp, kv, bias, sinkt, memkv, layer, wout, tm, name):
    t, d = x.shape
    n_mem = memkv.shape[0]
    mw = N_MEM_HEADS * HEAD_DIM
    qw = d - mw
    kw = N_KV_HEADS * HEAD_DIM
    nb = tm // BLOCK
    rows = GROUP * BLOCK

    def body(x_ref, q_ref, kv_ref, kvh_ref, bias_ref, sink_ref, mkv_ref, wo_ref, xo_ref, cat_ref, kvx, ytok):
        i = pl.program_id(0)
        kvx[0:BLOCK, :] = kvh_ref[...]
        kvx[BLOCK:BLOCK + tm, :] = kv_ref[...]
        colh = _col_head(kw)

        def blk(b, carry):
            r0 = pl.multiple_of(b * BLOCK, BLOCK)
            win = kvx[pl.ds(r0, 2 * BLOCK), :]
            kwin = win[:, 0:kw] * SCALE
            vwin = win[:, kw:2 * kw]
            qs = _stack_members(q_ref, r0, kw)
            first = ((i == 0) & (b == 0)).astype(jnp.int32) * N_KV_HEADS
            heads = range(N_KV_HEADS)
            ss = [lax.dot_general(_keep_head(kwin, colh, h), qs, NT, preferred_element_type=F32) for h in heads]
            ps = [_softmax_cols(ss[h] + bias_ref[first + h], sink_ref[h][0:1, :])[0].astype(BF16) for h in heads]
            o = _add4([lax.dot_general(ps[h], _keep_head(vwin, colh, h), TN, preferred_element_type=F32)
                       for h in heads])
            for j in range(GROUP):
                ytok[pl.ds(r0, BLOCK), j * kw:(j + 1) * kw] = o[j * BLOCK:(j + 1) * BLOCK].astype(BF16)
            return carry

        for b_static in range(nb):
            blk(b_static, 0)
        mkv = mkv_ref[...]
        ymem = _mem_attn_fwd(q_ref[:, qw:d], mkv[:, 0:mw], mkv[:, mw:2 * mw])
        cat = jnp.concatenate([ytok[...], ymem.astype(BF16)], axis=1)
        cat_ref[...] = cat
        xo_ref[...] = x_ref[...] + jnp.dot(cat, wo_ref[...], preferred_element_type=F32)

    return pl.pallas_call(
        body, name=name, grid=(t // tm,),
        in_specs=[_row_spec(tm, d), _row_spec(tm, d), _row_spec(tm, 2 * kw), _halo_prev_spec(BLOCK, 2 * kw, tm),
                  _const_spec((2 * N_KV_HEADS, 2 * BLOCK, rows)), _const_spec((N_KV_HEADS, 8, rows)),
                  pl.BlockSpec((n_mem, 2 * mw), lambda i: (0, layer)), _const_spec((d, d))],
        out_specs=[_row_spec(tm, d), _row_spec(tm, d)],
        out_shape=[jax.ShapeDtypeStruct((t, d), F32), jax.ShapeDtypeStruct((t, d), BF16)],
        scratch_shapes=[pltpu.VMEM((tm + BLOCK, 2 * kw), BF16), pltpu.VMEM((tm, qw), BF16)],
        compiler_params=_cp(("parallel",), VMEM_BIG),
    )(x, qp, kv, kv, bias, sinkt, memkv, wout)


def _mix_b_bwd(dxm, qp, kv, bias, sinkt, memkv, layer, wout, tm, name):
    t, d = dxm.shape
    n_mem = memkv.shape[0]
    mw = N_MEM_HEADS * HEAD_DIM
    qw = d - mw
    kw = N_KV_HEADS * HEAD_DIM
    nb = tm // BLOCK
    nt = t // tm
    rows = GROUP * BLOCK

    def body(dx_ref, q_ref, kv_ref, kvh_ref, bias_ref, sink_ref, mkv_ref, wo_ref,
             dq_ref, dkv_ref, dkvh_ref, dbias_ref, dsink_ref, dmkv_ref,
             kvx, dkvx, dcat_s, dmk_acc, dmv_acc):
        i = pl.program_id(0)

        @pl.when(i == 0)
        def _():
            dbias_ref[...] = jnp.zeros_like(dbias_ref)
            dsink_ref[...] = jnp.zeros_like(dsink_ref)
            dmk_acc[...] = jnp.zeros_like(dmk_acc)
            dmv_acc[...] = jnp.zeros_like(dmv_acc)

        kvx[0:BLOCK, :] = kvh_ref[...]
        kvx[BLOCK:BLOCK + tm, :] = kv_ref[...]
        dkvx[...] = jnp.zeros_like(dkvx)
        dcat_s[...] = lax.dot_general(dx_ref[...].astype(BF16), wo_ref[...], NT,
                                      preferred_element_type=F32).astype(BF16)
        colh = _col_head(kw)
        lane8 = lax.broadcasted_iota(jnp.int32, (8, 128), 1)

        def blk(b, carry):
            r0 = pl.multiple_of(b * BLOCK, BLOCK)
            win = kvx[pl.ds(r0, 2 * BLOCK), :]
            kwin = win[:, 0:kw] * SCALE
            vwin = win[:, kw:2 * kw]
            qs = _stack_members(q_ref, r0, kw)
            dos = _stack_members(dcat_s, r0, kw)
            first = ((i == 0) & (b == 0)).astype(jnp.int32) * N_KV_HEADS
            heads = range(N_KV_HEADS)
            khs = [_keep_head(kwin, colh, h) for h in heads]
            vhs = [_keep_head(vwin, colh, h) for h in heads]
            ss = [lax.dot_general(khs[h], qs, NT, preferred_element_type=F32) for h in heads]
            dps = [lax.dot_general(vhs[h], dos, NT, preferred_element_type=F32) for h in heads]
            dsink = jnp.zeros((8, 128), F32)
            pbs, dsbs = [], []
            for h in heads:
                p, sinkp = _softmax_cols(ss[h] + bias_ref[first + h], sink_ref[h][0:1, :])
                delta = jnp.sum(p * dps[h], axis=0, keepdims=True)
                ds = p * (dps[h] - delta)
                dbias_ref[h] += ds
                sd = sinkp * delta
                for j in range(GROUP):
                    val = -jnp.sum(sd[:, j * BLOCK:(j + 1) * BLOCK], axis=1, keepdims=True)
                    dsink = dsink + jnp.where(lane8 == 4 * j + h, val, 0.0)
                pbs.append(p.astype(BF16))
                dsbs.append(ds.astype(BF16))
            dq = _add4([lax.dot_general(dsbs[h], khs[h], TN, preferred_element_type=F32) for h in heads])
            dk = _add4([jnp.where(colh == h, jnp.dot(dsbs[h], qs, preferred_element_type=F32) * SCALE, 0.0)
                        for h in heads])
            dv = _add4([jnp.where(colh == h, jnp.dot(pbs[h], dos, preferred_element_type=F32), 0.0) for h in heads])
            for j in range(GROUP):
                dq_ref[pl.ds(r0, BLOCK), j * kw:(j + 1) * kw] = dq[j * BLOCK:(j + 1) * BLOCK].astype(BF16)
            dsink_ref[...] += dsink
            dkvx[pl.ds(r0, 2 * BLOCK), :] += jnp.concatenate([dk, dv], axis=1)
            return carry

        for b_static in range(nb):
            blk(b_static, 0)
        dkvh_ref[0] = dkvx[0:BLOCK, :]
        dkv_ref[...] = dkvx[BLOCK:BLOCK + tm, :]

        mkv = mkv_ref[...]
        dqm, dmk, dmv = _mem_attn_bwd(q_ref[:, qw:d], dcat_s[:, qw:d], mkv[:, 0:mw], mkv[:, mw:2 * mw])
        dq_ref[:, qw:d] = dqm.astype(BF16)
        dmk_acc[...] += dmk
        dmv_acc[...] += dmv

        @pl.when(i == nt - 1)
        def _():
            dmkv_ref[...] = jnp.concatenate([dmk_acc[...], dmv_acc[...]], axis=1)

    return pl.pallas_call(
        body, name=name, grid=(nt,),
        in_specs=[_row_spec(tm, d), _row_spec(tm, d), _row_spec(tm, 2 * kw), _halo_prev_spec(BLOCK, 2 * kw, tm),
                  _const_spec((2 * N_KV_HEADS, 2 * BLOCK, rows)), _const_spec((N_KV_HEADS, 8, rows)),
                  pl.BlockSpec((n_mem, 2 * mw), lambda i: (0, layer)), _const_spec((d, d))],
        out_specs=[_row_spec(tm, d), _row_spec(tm, 2 * kw),
                   pl.BlockSpec((1, BLOCK, 2 * kw), lambda i: (i, 0, 0)),
                   pl.BlockSpec((N_KV_HEADS, 2 * BLOCK, rows), lambda i: (0, 0, 0)),
                   pl.BlockSpec((8, 128), lambda i: (0, 0)),
                   pl.BlockSpec((n_mem, 2 * mw), lambda i: (0, 0))],
        out_shape=[jax.ShapeDtypeStruct((t, d), BF16),
                   jax.ShapeDtypeStruct((t, 2 * kw), F32), jax.ShapeDtypeStruct((nt, BLOCK, 2 * kw), F32),
                   jax.ShapeDtypeStruct((N_KV_HEADS, 2 * BLOCK, rows), F32), jax.ShapeDtypeStruct((8, 128), F32),
                   jax.ShapeDtypeStruct((n_mem, 2 * mw), F32)],
        scratch_shapes=[pltpu.VMEM((tm + BLOCK, 2 * kw), BF16), pltpu.VMEM((tm + BLOCK, 2 * kw), F32),
                        pltpu.VMEM((tm, d), BF16), pltpu.VMEM((n_mem, mw), F32), pltpu.VMEM((n_mem, mw), F32)],
        compiler_params=_cp(("arbitrary",), VMEM_BIG),
    )(dxm, qp, kv, kv, bias, sinkt, memkv, wout)


def _adam_math(w, g, m, v):
    m2 = ADAM_B1 * m + (1.0 - ADAM_B1) * g
    v2 = ADAM_B2 * v + (1.0 - ADAM_B2) * (g * g)
    m_hat = m2 / (1.0 - ADAM_B1 ** ADAM_STEP)
    v_hat = v2 / (1.0 - ADAM_B2 ** ADAM_STEP)
    delta = -ADAM_LR * (m_hat / (jnp.sqrt(v_hat) + ADAM_EPS) + ADAM_WD * w)
    return delta, m2, v2


def _adamw_sharded(w, land, m, v, name, after=()):
    nl, r, c = w.shape
    tr = max(cand for cand in range(16, r + 1, 16) if r % cand == 0 and cand * c <= 512 * 1024)

    def body(w_ref, a_ref, m_ref, v_ref, *rest):
        g_ref, d_ref, mo_ref, vo_ref = rest[-4:]
        g = a_ref[0, 0].astype(F32) + a_ref[1, 0].astype(F32)
        for k in range(1, N_CHIPS):
            g = g + (a_ref[2 * k, 0].astype(F32) + a_ref[2 * k + 1, 0].astype(F32))
        delta, m2, v2 = _adam_math(w_ref[0], g, m_ref[0], v_ref[0])
        g_ref[0] = g
        d_ref[0] = delta
        mo_ref[0] = m2
        vo_ref[0] = v2

    rs = pl.BlockSpec((1, tr, c), lambda l, i: (l, i, 0))
    ps = pl.BlockSpec((2 * N_CHIPS, 1, tr, c), lambda l, i: (0, l, i, 0))
    sd = jax.ShapeDtypeStruct((nl, r, c), F32)
    return pl.pallas_call(
        body, name=name, grid=(nl, r // tr),
        in_specs=[rs, ps, rs, rs] + [pl.BlockSpec(memory_space=pl.ANY)] * len(after),
        out_specs=[rs, rs, rs, rs], out_shape=[sd, sd, sd, sd],
        compiler_params=_cp(("parallel", "parallel"), VMEM_BIG),
    )(w, land, m, v, *after)


def _adamw_packed(w, g, m, v, name):
    def body(w_ref, g_ref, m_ref, v_ref, d_ref, mo_ref, vo_ref):
        delta, m2, v2 = _adam_math(w_ref[...], g_ref[...], m_ref[...], v_ref[...])
        d_ref[...] = delta
        mo_ref[...] = m2
        vo_ref[...] = v2

    vm = pl.BlockSpec(memory_space=pltpu.VMEM)
    sd = jax.ShapeDtypeStruct(w.shape, F32)
    return pl.pallas_call(body, name=name, in_specs=[vm] * 4, out_specs=[vm] * 3, out_shape=[sd] * 3)(w, g, m, v)


def _place():
    return lax.axis_index("x"), lax.axis_index("y"), lax.axis_index("c")


def _hbm(a):
    return pltpu.with_memory_space_constraint(a, pltpu.HBM)


def _peers(x, y, c, both_cores):
    chips = [(1 - x, y), (x, 1 - y), (1 - x, 1 - y)]
    if not both_cores:
        return [(px, py, c) for px, py in chips]
    return [(px, py, pc) for px, py in chips for pc in (c, 1 - c)] + [(x, y, 1 - c)]


def _chip_copy(src, land, gather, layer, chip_src, slot, send_sem, recv_sem, peer):
    s = src if gather else src.at[chip_src]
    d = land.at[slot] if layer is None else land.at[slot, layer]
    return pltpu.make_async_remote_copy(src_ref=s, dst_ref=d, send_sem=send_sem, recv_sem=recv_sem,
                                        device_id=peer, device_id_type=MESH)


def _own_copy(src, land, gather, layer, chip, slot, own_sem):
    s = src if gather else src.at[chip]
    d = land.at[slot] if layer is None else land.at[slot, layer]
    return pltpu.make_async_copy(s, d, own_sem)


def _exchange_start(srcs, lands, gather, layers, both_cores, after, name):
    n = len(srcs)
    npeer = 7 if both_cores else 3
    hbm = pl.BlockSpec(memory_space=pltpu.HBM)
    sem = pl.BlockSpec(memory_space=pltpu.SEMAPHORE)

    def body(*refs):
        ins, lds = refs[:n], refs[n:2 * n]
        first_out = 2 * n + len(after)
        send_sems, recv_sems, own_sems, token = refs[first_out], refs[first_out + 1], refs[first_out + 2], refs[-1]
        x, y, c = _place()
        slot = 2 * x + y if gather is True else 2 * (2 * x + y) + c
        for t in range(n):
            _own_copy(ins[t], lds[t], gather, layers[t], 2 * x + y, slot, own_sems.at[t]).start()
            for r, peer in enumerate(_peers(x, y, c, both_cores)):
                _chip_copy(ins[t], lds[t], gather, layers[t], 2 * peer[0] + peer[1], slot,
                           send_sems.at[npeer * t + r], recv_sems.at[npeer * t + r], peer).start()
        token[...] = jnp.zeros_like(token)

    both = list(srcs) + list(lands)
    outs = pl.pallas_call(
        body, name=name, in_specs=[hbm] * (2 * n) + [pl.BlockSpec(memory_space=pl.ANY)] * len(after),
        out_specs=(sem, sem, sem, *([hbm] * (2 * n)), pl.BlockSpec(memory_space=pltpu.VMEM)),
        out_shape=(pltpu.SemaphoreType.DMA((npeer * n,)), pltpu.SemaphoreType.DMA((npeer * n,)),
                   pltpu.SemaphoreType.DMA((n,)),
                   *[pltpu.HBM(a.shape, a.dtype) for a in both], jax.ShapeDtypeStruct((8, 128), F32)),
        input_output_aliases={t: 3 + t for t in range(2 * n)},
        compiler_params=_cp(has_side_effects=pltpu.SideEffectType.DATAFLOW_SIDE_EFFECTING),
    )(*[_hbm(a) for a in both], *after)
    return dict(send=outs[0], recv=outs[1], own=outs[2], srcs=list(outs[3:3 + n]), lands=list(outs[3 + n:3 + 2 * n]),
                token=outs[-1], gather=gather, layers=list(layers), both_cores=both_cores)


def _exchange_wait(groups, lands, land_ids, after, name):
    flat = [s for g in groups for s in g["srcs"]]
    ns, nl, ng, na = len(flat), len(lands), len(groups), len(after)
    hbm = pl.BlockSpec(memory_space=pltpu.HBM)
    sem = pl.BlockSpec(memory_space=pltpu.SEMAPHORE)

    def body(*refs):
        srcs, lds = refs[:ns], refs[ns:ns + nl]
        sems = refs[ns + nl:ns + nl + 3 * ng]
        x, y, c = _place()
        k = 0
        for gi, g in enumerate(groups):
            peers = _peers(x, y, c, g["both_cores"])
            for t in range(len(g["srcs"])):
                _own_copy(srcs[k], lds[land_ids[gi][t]], g["gather"], g["layers"][t], 0, 0, sems[3 * gi + 2].at[t]).wait()
                for r, peer in enumerate(peers):
                    cp = _chip_copy(srcs[k], lds[land_ids[gi][t]], g["gather"], g["layers"][t], 0, 0,
                                    sems[3 * gi].at[len(peers) * t + r], sems[3 * gi + 1].at[len(peers) * t + r], peer)
                    cp.wait_send()
                    cp.wait_recv()
                k += 1

    both = flat + list(lands)
    sem_args = [a for g in groups for a in (g["send"], g["recv"], g["own"])]
    outs = pl.pallas_call(
        body, name=name,
        in_specs=[hbm] * (ns + nl) + [sem] * (3 * ng) + [pl.BlockSpec(memory_space=pl.ANY)] * na,
        out_specs=[hbm] * (ns + nl),
        out_shape=[pltpu.HBM(a.shape, a.dtype) for a in both],
        input_output_aliases={t: t for t in range(ns + nl)},
        compiler_params=_cp(has_side_effects=pltpu.SideEffectType.DATAFLOW_SIDE_EFFECTING),
    )(*both, *sem_args, *after)
    return list(outs[ns:])


def _fill_copy(land, layer, slot, send_sem, recv_sem, x, y, c):
    mine = land.at[slot] if layer is None else land.at[slot, layer]
    return pltpu.make_async_remote_copy(src_ref=mine, dst_ref=mine, send_sem=send_sem, recv_sem=recv_sem,
                                        device_id=(x, y, 1 - c), device_id_type=MESH)


def _core_fill_start(lands, layers, after, name):
    n = len(lands)
    hbm = pl.BlockSpec(memory_space=pltpu.HBM)
    sem = pl.BlockSpec(memory_space=pltpu.SEMAPHORE)

    def body(*refs):
        ins = refs[:n]
        first_out = n + len(after)
        send_sems, recv_sems, token = refs[first_out], refs[first_out + 1], refs[-1]
        x, y, c = _place()
        for t in range(n):
            for k in range(N_CHIPS):
                _fill_copy(ins[t], layers[t], 2 * k + c, send_sems.at[N_CHIPS * t + k], recv_sems.at[N_CHIPS * t + k],
                           x, y, c).start()
        token[...] = jnp.zeros_like(token)

    outs = pl.pallas_call(
        body, name=name, in_specs=[hbm] * n + [pl.BlockSpec(memory_space=pl.ANY)] * len(after),
        out_specs=(sem, sem, *([hbm] * n), pl.BlockSpec(memory_space=pltpu.VMEM)),
        out_shape=(pltpu.SemaphoreType.DMA((N_CHIPS * n,)), pltpu.SemaphoreType.DMA((N_CHIPS * n,)),
                   *[pltpu.HBM(a.shape, a.dtype) for a in lands], jax.ShapeDtypeStruct((8, 128), F32)),
        input_output_aliases={t: 2 + t for t in range(n)},
        compiler_params=_cp(has_side_effects=pltpu.SideEffectType.DATAFLOW_SIDE_EFFECTING),
    )(*[_hbm(a) for a in lands], *after)
    return dict(send=outs[0], recv=outs[1], lands=list(outs[2:2 + n]), token=outs[-1], layers=list(layers))


def _core_fill_wait(fill, after, name):
    lands, layers = fill["lands"], fill["layers"]
    n = len(lands)
    hbm = pl.BlockSpec(memory_space=pltpu.HBM)
    sem = pl.BlockSpec(memory_space=pltpu.SEMAPHORE)

    def body(*refs):
        ins = refs[:n]
        send_sems, recv_sems = refs[n], refs[n + 1]
        x, y, c = _place()
        for t in range(n):
            for k in range(N_CHIPS):
                cp = _fill_copy(ins[t], layers[t], 2 * k + c, send_sems.at[N_CHIPS * t + k],
                                recv_sems.at[N_CHIPS * t + k], x, y, c)
                cp.wait_send()
                cp.wait_recv()

    outs = pl.pallas_call(
        body, name=name, in_specs=[hbm] * n + [sem, sem] + [pl.BlockSpec(memory_space=pl.ANY)] * len(after),
        out_specs=[hbm] * n, out_shape=[pltpu.HBM(a.shape, a.dtype) for a in lands],
        input_output_aliases={t: t for t in range(n)},
        compiler_params=_cp(has_side_effects=pltpu.SideEffectType.DATAFLOW_SIDE_EFFECTING),
    )(*lands, fill["send"], fill["recv"], *after)
    return list(outs)


def _sum_slots(land, name):
    _, r, c = land.shape
    vm = pl.BlockSpec(memory_space=pltpu.VMEM)

    def body(l_ref, sum_ref):
        total = l_ref[0]
        for k in range(1, 8):
            total = total + l_ref[k]
        sum_ref[...] = total

    return pl.pallas_call(body, name=name, in_specs=[vm], out_specs=vm, out_shape=jax.ShapeDtypeStruct((r, c), F32))(land)


def _pack(items):
    rows = []
    for a in items:
        flat = a.astype(F32).reshape(-1)
        pad = (-flat.shape[0]) % PACK_W
        rows.append(jnp.pad(flat, (0, pad)).reshape(-1, PACK_W))
    out = jnp.concatenate(rows, axis=0)
    pad_r = (-out.shape[0]) % 8
    return jnp.pad(out, ((0, pad_r), (0, 0)))


def _unpack(pack, shapes):
    outs, row = [], 0
    for s in shapes:
        n = int(np.prod(s))
        nr = -(-n // PACK_W)
        outs.append(pack[row:row + nr].reshape(-1)[:n].reshape(s))
        row += nr
    return outs


def _heads_to_member_major(w, axis):
    shp = w.shape
    pre, post = shp[:axis], shp[axis + 1:]
    w4 = w.reshape(pre + (N_KV_HEADS, GROUP, HEAD_DIM) + post)
    w4 = jnp.swapaxes(w4, len(pre), len(pre) + 1)
    return w4.reshape(shp)


def _heads_to_kv_major(w, axis):
    shp = w.shape
    pre, post = shp[:axis], shp[axis + 1:]
    w4 = w.reshape(pre + (GROUP, N_KV_HEADS, HEAD_DIM) + post)
    w4 = jnp.swapaxes(w4, len(pre), len(pre) + 1)
    return w4.reshape(shp)


def kernel(x, mem, norm_mix, norm_ffn, a_w_in, a_conv_w, a_w_out, kv_norm, w_kv, b_w_q, b_sinks, b_w_out, rel_bias, mem_norm, w_mem_kv, w_gate, w_up, w_down, final_norm, loss_target, m_norm_mix, m_norm_ffn, m_a_w_in, m_a_conv_w, m_a_w_out, m_kv_norm, m_w_kv, m_b_w_q, m_b_sinks, m_b_w_out, m_rel_bias, m_mem_norm, m_w_mem_kv, m_w_gate, m_w_up, m_w_down, m_final_norm, v_norm_mix, v_norm_ffn, v_a_w_in, v_a_conv_w, v_a_w_out, v_kv_norm, v_w_kv, v_b_w_q, v_b_sinks, v_b_w_out, v_rel_bias, v_mem_norm, v_w_mem_kv, v_w_gate, v_w_up, v_w_down, v_final_norm):
    t, d = x.shape[1], x.shape[2]
    tm = 512 if t % 512 == 0 and t >= 2048 else 256
    tl = 2 * tm if t % (2 * tm) == 0 else tm
    x0 = x.reshape(t, d)
    target = loss_target.reshape(t, d)
    mem2 = mem.reshape(mem.shape[1], d)
    n_mem = mem2.shape[0]
    ax, ay, ac = _place()
    chip = 2 * ax + ay
    cwid = a_conv_w.shape[2] * N_CHIPS
    qw = N_Q_HEADS * HEAD_DIM
    nq = N_CHIPS

    def landing(piece):
        return lax.empty((nq,) + piece.shape, piece.dtype)

    def mixer_shards(i):
        if i < N_A:
            shards = [a_w_in[i], a_w_out[i]] + ([w_mem_kv] if i == 0 else [])
        else:
            j = i - N_A
            shards = [b_w_q[j], b_w_out[j]] + ([w_kv] if j == 0 else [])
        return [a.astype(BF16) for a in shards]

    def ffn_shards(i):
        return [w_gate[i].T.astype(BF16), w_up[i].T.astype(BF16), w_down[i].astype(BF16)]

    conv_pad = jnp.pad(a_conv_w, ((0, 0), (0, 8 - a_conv_w.shape[1]), (0, (-a_conv_w.shape[2]) % 128)))
    first = mixer_shards(0)
    group_shards = {"0a": first[0:1], "0b": first[1:] + [conv_pad], "0f": ffn_shards(0)}
    for i in range(1, DEPTH):
        group_shards[str(i)] = ffn_shards(i) + mixer_shards(i)
    gathers, prev_tok = {}, []

    def start_group(key, after):
        shards = group_shards[key]
        gathers[key] = _exchange_start(shards, [landing(a) for a in shards], True, [None] * len(shards), False,
                                       after, "gather_start_" + key)
        return [gathers[key]["token"]]

    for key in ("0a", "0b", "0f"):
        prev_tok = start_group(key, prev_tok)

    def rows_full(g):
        return g.reshape((-1,) + g.shape[2:])

    def cols_full(g):
        return jnp.transpose(g, (1, 0, 2)).reshape(g.shape[1], -1)

    def landed_weights(key, after):
        g = gathers[key]
        return _exchange_wait([g], g["lands"], [list(range(len(g["lands"])))], after, "gather_wait_" + key)

    def mixer_weights(i, got):
        w_first, w_out = (cols_full(got[0]) if i < N_A else rows_full(got[0])), rows_full(got[1])
        if i >= N_A:
            w_first = jnp.concatenate([_heads_to_member_major(w_first[:, :qw], 1), w_first[:, qw:]], axis=1)
            w_out = jnp.concatenate([_heads_to_member_major(w_out[:qw, :], 0), w_out[qw:, :]], axis=0)
        return dict(w_first=w_first, w_out=w_out, extra=got[2] if len(got) > 2 else None)

    def ffn_weights(got):
        return dict(wg=rows_full(got[0]), wu=rows_full(got[1]), wd=rows_full(got[2]))

    bias, sinkt = [], []
    for j in range(2):
        bj, sj = _bias_tables(rel_bias, b_sinks[j], "bias_tables")
        bias.append(bj)
        sinkt.append(sj)

    ws = []
    xs, xmids, projs, cats, gates, ups = [x0], [], [], [], [], []
    kv = memkv = wmem = wkv = None
    for i in range(DEPTH):
        xin = xs[-1]
        if i == 0:
            w = dict(w_first=cols_full(landed_weights("0a", prev_tok)[0]))
        else:
            got = landed_weights(str(i), [xin])
            w = dict(mixer_weights(i, got[3:]), **ffn_weights(got[0:3]))
        ws.append(w)
        gm = norm_mix[i].reshape(1, d)
        if i < N_A:
            proj = _norm_mm(xin, gm, w["w_first"], tl, "proj_a")
            if i == 0:
                for key in ("1", "2", "3"):
                    prev_tok = start_group(key, prev_tok + [proj])
                got = landed_weights("0b", prev_tok)
                w["w_out"] = rows_full(got[0])
                full_mem = jnp.swapaxes(got[1], 0, 1).reshape(DEPTH, d, -1)
                wmem = jnp.transpose(full_mem, (1, 0, 2)).reshape(d, -1)
                memkv = _norm_mm(mem2, mem_norm.reshape(1, d), wmem, n_mem, "mem_kv")
                taps = got[2][:, :, 0:3, 0:a_conv_w.shape[2]]
                conv_full = jnp.transpose(taps, (1, 2, 0, 3)).reshape(N_A, 3, cwid)
            xmid, cat = _mix_a_fwd(xin, proj, conv_full[i], memkv, i, w["w_out"], tm, "mix_a_fwd")
        else:
            j = i - N_A
            if j == 0:
                wkv = rows_full(w["extra"])
                proj, kv = _norm_mm_pair(xin, gm, w["w_first"], kv_norm.reshape(1, d), wkv, tl, "proj_b_kv")
            else:
                proj = _norm_mm(xin, gm, w["w_first"], tl, "proj_b")
            xmid, cat = _mix_b_fwd(xin, proj, kv, bias[j], sinkt[j], memkv, i, w["w_out"], tm, "mix_b_fwd")
        if i == 0:
            w.update(ffn_weights(landed_weights("0f", [xmid])))
        if i < DEPTH - 1:
            xout, gate, up = _ffn_fwd(xmid, norm_ffn[i].reshape(1, d), w["wg"], w["wu"], w["wd"], tm, "ffn_fwd")
        else:
            xout = None
            gate, up, loss_part, dx, dg_final = _ffn_fwd_loss(xmid, norm_ffn[i].reshape(1, d), w["wg"], w["wu"], w["wd"],
                                                              final_norm.reshape(1, d), target, tm, "ffn_fwd_loss")
        projs.append(proj)
        cats.append(cat)
        xmids.append(xmid)
        gates.append(gate)
        ups.append(up)
        xs.append(xout)

    def rows_pieces(g):
        return g.astype(BF16).reshape((nq, g.shape[0] // nq) + g.shape[1:])

    def cols_pieces(g):
        return jnp.transpose(g.astype(BF16).reshape(g.shape[0], nq, g.shape[1] // nq), (1, 0, 2))

    swapped = ("w_gate", "w_up")
    stacked = dict(a_w_in=a_w_in, a_w_out=a_w_out, w_kv=w_kv[None], b_w_q=b_w_q, b_w_out=b_w_out,
                   w_mem_kv=w_mem_kv, w_gate=jnp.swapaxes(w_gate, 1, 2), w_up=jnp.swapaxes(w_up, 1, 2), w_down=w_down)
    names = list(stacked)
    land = {k: lax.empty((2 * nq,) + stacked[k].shape, BF16) for k in names}
    scatters, scatter_ids = [], []

    def scatter_start(key, items, both_cores):
        keys = [k for k, _, _ in items]
        st = _exchange_start([p for _, _, p in items], [land[k] for k in keys], False, [l for _, l, _ in items],
                             both_cores, [], "scatter_start_" + key)
        for k, ld in zip(keys, st["lands"]):
            land[k] = ld
        scatters.append(st)
        scatter_ids.append([names.index(k) for k in keys])
        return st["token"][0:1, 0:1]

    g_norm_mix, g_norm_ffn = [None] * DEPTH, [None] * DEPTH
    g_conv, g_sinks = [None] * 2, [None] * 2
    dmemkv = [None] * DEPTH
    dbias, dkv_main, dkv_halo = [None] * 2, [None] * 2, [None] * 2
    g_kv_norm = None
    tok = jnp.zeros((1, 1), F32)
    for i in reversed(range(DEPTH)):
        w = ws[i]
        dxm, dgate, dup, act, h2, dgf = _ffn_bwd(dx, xmids[i], norm_ffn[i].reshape(1, d) + tok, gates[i], ups[i],
                                            w["wg"], w["wu"], w["wd"], tm // 2, "ffn_bwd")
        g_norm_ffn[i] = dgf
        g_wd = _wgrad(act, dx, 2 * tm, "wgrad_down")
        g_wg = _wgrad(dgate, h2, 2 * tm, "wgrad_gate")
        g_wu = _wgrad(dup, h2, 2 * tm, "wgrad_up")
        items = [("w_gate", i, rows_pieces(g_wg)), ("w_up", i, rows_pieces(g_wu)), ("w_down", i, rows_pieces(g_wd))]
        if i == 0:
            tok = scatter_start("0f", items, True)
            items = []
        gm = norm_mix[i].reshape(1, d)
        if i < N_A:
            dproj, dcw, dmemkv[i] = _mix_a_bwd(dxm, projs[i], conv_full[i] + (tok if i == 0 else 0.0), memkv, i,
                                                    w["w_out"], tm, "mix_a_bwd")
            g_conv[i] = dcw[0:3]
            g_out = _wgrad(cats[i], dxm, 2 * tm, "wgrad_out")
            if i == 0:
                dmemkv_all = jnp.concatenate([a.astype(BF16) for a in dmemkv], axis=1)
                _, g_mem_norm, hmem = _mm_nt_normbwd(dmemkv_all, wmem, mem2, mem_norm.reshape(1, d),
                                                     jnp.zeros((n_mem, d), F32), n_mem, "mem_kv_bwd")
                g_wmem = _wgrad(hmem, dmemkv_all, n_mem, "wgrad_mem")
                g_wmem = jnp.transpose(g_wmem.reshape(nq, d // nq, DEPTH, -1), (0, 2, 1, 3))
                gm = gm + scatter_start("0o", [("a_w_out", 0, rows_pieces(g_out)), ("w_mem_kv", None, g_wmem)], False)
            dx, g_norm_mix[i], h = _mm_nt_normbwd(dproj, w["w_first"], xs[i], gm, dxm, tl, "proj_a_bwd")
            g_in = _wgrad(h, dproj, 2 * tm, "wgrad_in_a")
            items.append(("a_w_in", i, cols_pieces(g_in)))
            if i > 0:
                items.append(("a_w_out", i, rows_pieces(g_out)))
        else:
            j = i - N_A
            dqp, dkv_main[j], dkv_halo[j], dbias[j], dsk, dmemkv[i] = _mix_b_bwd(
                dxm, projs[i], kv, bias[j], sinkt[j], memkv, i, w["w_out"], tm, "mix_b_bwd")
            g_sinks[j] = dsk[0, 0:N_Q_HEADS].reshape(GROUP, N_KV_HEADS).T.reshape(N_Q_HEADS)
            g_out = _wgrad(cats[i], dxm, 2 * tm, "wgrad_out")
            if j == 0:
                dx, g_norm_mix[i], g_kv_norm, h, hkv, dkv = _mm_nt_normbwd_pair(
                    dqp, w["w_first"], gm, (dkv_main[0], dkv_halo[0], dkv_main[1], dkv_halo[1]), wkv,
                    kv_norm.reshape(1, d), xs[i], dxm, tm, "proj_b_kv_bwd")
            else:
                dx, g_norm_mix[i], h = _mm_nt_normbwd(dqp, w["w_first"], xs[i], gm, dxm, tl, "proj_b_bwd")
            g_q = _wgrad(h, dqp, 2 * tm, "wgrad_in_b")
            g_q = jnp.concatenate([_heads_to_kv_major(g_q[:, :qw], 1), g_q[:, qw:]], axis=1)
            g_out = jnp.concatenate([_heads_to_kv_major(g_out[:qw, :], 0), g_out[qw:, :]], axis=0)
            items += [("b_w_q", j, rows_pieces(g_q)), ("b_w_out", j, rows_pieces(g_out))]
            if j == 0:
                items.append(("w_kv", 0, rows_pieces(_wgrad(hkv, dkv, 2 * tm, "wgrad_kv"))))
        tok = scatter_start(str(i) if i else "0i", items, i > 0)
    grad_x = dx.reshape(x.shape)
    g_rel = _bias_bwd(dbias[0], dbias[1], "bias_bwd")[:, 0:N_Q_HEADS]

    small_shapes = [(DEPTH, d), (DEPTH, d), (d,), (d,), (d,), (2, N_Q_HEADS), (REL_BUCKETS, N_Q_HEADS),
                    (N_A, 3, cwid), ()]
    small = _pack([jnp.concatenate(g_norm_mix, axis=0), jnp.concatenate(g_norm_ffn, axis=0), g_kv_norm, g_mem_norm,
                   dg_final, jnp.stack(g_sinks), g_rel, jnp.stack(g_conv), loss_part[0, 0]])
    small_grp = _exchange_start([small], [lax.empty((8,) + small.shape, F32)], "all", [None], True,
                                [scatters[-1]["token"]], "reduce_small_start")

    late = ("a_w_in", "a_w_out", "w_mem_kv")
    landed = dict(zip(names, _exchange_wait(scatters[:-2], [land[k] for k in names], scatter_ids[:-2],
                                            [small_grp["token"]], "scatter_wait_a")))

    weights = dict(norm_mix=norm_mix, norm_ffn=norm_ffn, a_w_in=a_w_in, a_conv_w=a_conv_w, a_w_out=a_w_out,
                   kv_norm=kv_norm, w_kv=w_kv, b_w_q=b_w_q, b_sinks=b_sinks, b_w_out=b_w_out, rel_bias=rel_bias,
                   mem_norm=mem_norm, w_mem_kv=w_mem_kv, w_gate=w_gate, w_up=w_up, w_down=w_down,
                   final_norm=final_norm)
    moms = dict(norm_mix=m_norm_mix, norm_ffn=m_norm_ffn, a_w_in=m_a_w_in, a_conv_w=m_a_conv_w, a_w_out=m_a_w_out,
                kv_norm=m_kv_norm, w_kv=m_w_kv, b_w_q=m_b_w_q, b_sinks=m_b_sinks, b_w_out=m_b_w_out,
                rel_bias=m_rel_bias, mem_norm=m_mem_norm, w_mem_kv=m_w_mem_kv, w_gate=m_w_gate, w_up=m_w_up,
                w_down=m_w_down, final_norm=m_final_norm)
    vars_ = dict(norm_mix=v_norm_mix, norm_ffn=v_norm_ffn, a_w_in=v_a_w_in, a_conv_w=v_a_conv_w, a_w_out=v_a_w_out,
                 kv_norm=v_kv_norm, w_kv=v_w_kv, b_w_q=v_b_w_q, b_sinks=v_b_sinks, b_w_out=v_b_w_out,
                 rel_bias=v_rel_bias, mem_norm=v_mem_norm, w_mem_kv=v_w_mem_kv, w_gate=v_w_gate, w_up=v_w_up,
                 w_down=v_w_down, final_norm=v_final_norm)
    order = list(weights)
    grads, deltas, new_m, new_v = {}, {}, {}, {}
    def adamw(k, ld, after=()):
        shp, stk = weights[k].shape, ld.shape[1:]
        view = (lambda a: jnp.swapaxes(a, 1, 2)) if k in swapped else (lambda a: a.reshape(stk))
        back = (lambda a: jnp.swapaxes(a, 1, 2)) if k in swapped else (lambda a: a.reshape(shp))
        outs = _adamw_sharded(view(weights[k]), ld, view(moms[k]), view(vars_[k]), "adamw_" + k, after)
        grads[k], deltas[k], new_m[k], new_v[k] = [back(o) for o in outs]

    early = sorted([k for k in names if k not in late and k != "w_down"], key=lambda k: k not in swapped)
    for k in early:
        adamw(k, landed[k])
    late_ids = [[late.index(names[t]) for t in ids] for ids in scatter_ids[-2:]]
    late_landed = _exchange_wait(scatters[-2:] + [small_grp], [landed[k] for k in late] + small_grp["lands"],
                                 late_ids + [[len(late)]], [deltas[early[-1]]], "scatter_wait_b")
    fill = _core_fill_start(late_landed[:-1], [0, 0, None], [], "fill_cores_start")
    adamw("w_down", landed["w_down"], [fill["token"]])
    filled = _core_fill_wait(fill, [deltas["w_down"]], "fill_cores_wait")
    for k, ld in zip(late, filled):
        adamw(k, ld)
    small_sum = _sum_slots(late_landed[-1], "reduce_small_sum")
    (gs_norm_mix, gs_norm_ffn, gs_kv_norm, gs_mem_norm, gs_final, gs_sinks, gs_rel, gs_conv_full, loss) = _unpack(
        small_sum, small_shapes)
    cq = cwid // N_CHIPS
    gs_conv = lax.dynamic_slice_in_dim(gs_conv_full, chip * cq, cq, axis=2)
    small_names = ["norm_mix", "norm_ffn", "kv_norm", "mem_norm", "final_norm", "b_sinks", "rel_bias", "a_conv_w"]
    small_g = [gs_norm_mix, gs_norm_ffn, gs_kv_norm, gs_mem_norm, gs_final, gs_sinks, gs_rel, gs_conv]
    shapes = [weights[k].shape for k in small_names]
    dl_p, m_p, v_p = _adamw_packed(_pack([weights[k] for k in small_names]), _pack(small_g),
                                   _pack([moms[k] for k in small_names]), _pack([vars_[k] for k in small_names]),
                                   "adamw_small")
    for k, g, dl, m2, v2 in zip(small_names, small_g, _unpack(dl_p, shapes), _unpack(m_p, shapes), _unpack(v_p, shapes)):
        grads[k], deltas[k], new_m[k], new_v[k] = g.reshape(weights[k].shape), dl, m2, v2

    return (loss, grad_x, *[grads[k] for k in order], *[deltas[k] for k in order],
            *[new_m[k] for k in order], *[new_v[k] for k in order])
```

```python
import functools
import math

import numpy as np
import jax
import jax.numpy as jnp
from jax import lax
from jax.experimental import pallas as pl
from jax.experimental.pallas import tpu as pltpu

F32 = jnp.float32
BF16 = jnp.bfloat16
MESH = pl.DeviceIdType.MESH

EPS = 1e-5
HEAD_DIM = 64
N_MEM_HEADS = 4
N_KV_HEADS = 4
GROUP = 3
N_Q_HEADS = N_KV_HEADS * GROUP
BLOCK = 128
REL_BUCKETS = 32
REL_MAX_DIST = 128
SCALE = HEAD_DIM ** -0.5
NEG = -1e30
N_CHIPS = 4
N_A = 2
DEPTH = 4

ADAM_LR = 0.001
ADAM_B1 = 0.9
ADAM_B2 = 0.999
ADAM_EPS = 1e-08
ADAM_WD = 0.01
ADAM_STEP = 10

VMEM_BIG = 56 * 1024 * 1024
PACK_W = 1024

NT = (((1,), (1,)), ((), ()))
TN = (((0,), (0,)), ((), ()))


def _cp(sem=None, vmem=None, **kw):
    return pltpu.CompilerParams(dimension_semantics=sem, vmem_limit_bytes=vmem, **kw)


def _const_spec(shape):
    nd = len(shape)
    return pl.BlockSpec(shape, lambda i, _n=nd: (0,) * _n, pipeline_mode=pl.Buffered(1))


def _row_spec(tm, n):
    return pl.BlockSpec((tm, n), lambda i: (i, 0))


def _rms_parts(xv):
    r = lax.rsqrt(jnp.mean(xv * xv, axis=-1, keepdims=True) + EPS)
    return xv * r, r


def _sigmoid(z):
    return 1.0 / (1.0 + jnp.exp(-z))


def _ff_chunks(f):
    if f % 512 == 0 or f % 256 != 0:
        return [(0, f)] if f <= 1536 else [(0, f // 2), (f // 2, f - f // 2)]
    n = f // 256
    a = (n + 1) // 2 * 256
    return [(0, a), (a, f - a)]


def _norm_mm(x, g, w, tm, name):
    t, d = x.shape
    n = w.shape[1]

    def body(x_ref, g_ref, w_ref, o_ref):
        xhat, _ = _rms_parts(x_ref[...])
        h = (xhat * g_ref[...]).astype(BF16)
        o_ref[...] = jnp.dot(h, w_ref[...], preferred_element_type=F32).astype(BF16)

    return pl.pallas_call(
        body, name=name, grid=(t // tm,),
        in_specs=[_row_spec(tm, d), _const_spec((1, d)), _const_spec((d, n))],
        out_specs=_row_spec(tm, n),
        out_shape=jax.ShapeDtypeStruct((t, n), BF16),
        compiler_params=_cp(("parallel",), VMEM_BIG),
    )(x, g, w)


def _mm_nt_normbwd(dproj, w, x_in, g, dres, tm, name):
    t, d = x_in.shape
    n = w.shape[1]

    def body(dp_ref, w_ref, x_ref, g_ref, dr_ref, dx_ref, dg_ref, h_ref):
        i = pl.program_id(0)
        xhat, r = _rms_parts(x_ref[...])
        gv = g_ref[...]
        h_ref[...] = (xhat * gv).astype(BF16)
        dh = lax.dot_general(dp_ref[...], w_ref[...], NT, preferred_element_type=F32)
        dxhat = dh * gv
        dx = r * (dxhat - xhat * jnp.mean(dxhat * xhat, axis=-1, keepdims=True))
        dx_ref[...] = dr_ref[...] + dx

        @pl.when(i == 0)
        def _():
            dg_ref[...] = jnp.zeros_like(dg_ref)

        dg_ref[...] += jnp.sum(dh * xhat, axis=0, keepdims=True)

    return pl.pallas_call(
        body, name=name, grid=(t // tm,),
        in_specs=[_row_spec(tm, n), _const_spec((d, n)), _row_spec(tm, d), _const_spec((1, d)), _row_spec(tm, d)],
        out_specs=[_row_spec(tm, d), pl.BlockSpec((1, d), lambda i: (0, 0)), _row_spec(tm, d)],
        out_shape=[jax.ShapeDtypeStruct((t, d), F32), jax.ShapeDtypeStruct((1, d), F32),
                   jax.ShapeDtypeStruct((t, d), BF16)],
        compiler_params=_cp(("arbitrary",), VMEM_BIG),
    )(dproj, w, x_in, g, dres)


def _norm_mm_pair(x, g_a, w_a, g_b, w_b, tm, name):
    t, d = x.shape
    na, nb = w_a.shape[1], w_b.shape[1]

    def body(x_ref, ga_ref, wa_ref, gb_ref, wb_ref, oa_ref, ob_ref):
        xhat, _ = _rms_parts(x_ref[...])
        ha = (xhat * ga_ref[...]).astype(BF16)
        hb = (xhat * gb_ref[...]).astype(BF16)
        oa_ref[...] = jnp.dot(ha, wa_ref[...], preferred_element_type=F32).astype(BF16)
        ob_ref[...] = jnp.dot(hb, wb_ref[...], preferred_element_type=F32).astype(BF16)

    return pl.pallas_call(
        body, name=name, grid=(t // tm,),
        in_specs=[_row_spec(tm, d), _const_spec((1, d)), _const_spec((d, na)), _const_spec((1, d)), _const_spec((d, nb))],
        out_specs=[_row_spec(tm, na), _row_spec(tm, nb)],
        out_shape=[jax.ShapeDtypeStruct((t, na), BF16), jax.ShapeDtypeStruct((t, nb), BF16)],
        compiler_params=_cp(("parallel",), VMEM_BIG),
    )(x, g_a, w_a, g_b, w_b)


def _mm_nt_normbwd_pair(dp_a, w_a, g_a, kv_parts, w_b, g_b, x_in, dres, tm, name):
    t, d = x_in.shape
    na, nb = w_a.shape[1], w_b.shape[1]
    nt = t // tm
    main_1, halo_1, main_2, halo_2 = kv_parts

    def body(dpa_ref, wa_ref, ga_ref, m1_ref, h1_ref, m2_ref, h2_ref, wb_ref, gb_ref, x_ref, dr_ref,
             dx_ref, dga_ref, dgb_ref, ha_ref, hb_ref, dkv_ref):
        i = pl.program_id(0)
        s = m1_ref[...] + m2_ref[...]
        tail = jnp.where(i == nt - 1, 0.0, h1_ref[0] + h2_ref[0])
        dkv = jnp.concatenate([s[0:tm - BLOCK], s[tm - BLOCK:] + tail], axis=0).astype(BF16)
        dkv_ref[...] = dkv
        xhat, r = _rms_parts(x_ref[...])
        ga, gb = ga_ref[...], gb_ref[...]
        ha_ref[...] = (xhat * ga).astype(BF16)
        hb_ref[...] = (xhat * gb).astype(BF16)
        dha = lax.dot_general(dpa_ref[...], wa_ref[...], NT, preferred_element_type=F32)
        dhb = lax.dot_general(dkv, wb_ref[...], NT, preferred_element_type=F32)
        dxhat = dha * ga + dhb * gb
        dx_ref[...] = dr_ref[...] + r * (dxhat - xhat * jnp.mean(dxhat * xhat, axis=-1, keepdims=True))

        @pl.when(i == 0)
        def _():
            dga_ref[...] = jnp.zeros_like(dga_ref)
            dgb_ref[...] = jnp.zeros_like(dgb_ref)

        dga_ref[...] += jnp.sum(dha * xhat, axis=0, keepdims=True)
        dgb_ref[...] += jnp.sum(dhb * xhat, axis=0, keepdims=True)

    halo_spec = pl.BlockSpec((1, BLOCK, nb), lambda i: (jnp.minimum(i + 1, nt - 1), 0, 0))
    row1 = pl.BlockSpec((1, d), lambda i: (0, 0))
    return pl.pallas_call(
        body, name=name, grid=(nt,),
        in_specs=[_row_spec(tm, na), _const_spec((d, na)), _const_spec((1, d)), _row_spec(tm, nb), halo_spec,
                  _row_spec(tm, nb), halo_spec, _const_spec((d, nb)), _const_spec((1, d)), _row_spec(tm, d),
                  _row_spec(tm, d)],
        out_specs=[_row_spec(tm, d), row1, row1, _row_spec(tm, d), _row_spec(tm, d), _row_spec(tm, nb)],
        out_shape=[jax.ShapeDtypeStruct((t, d), F32), jax.ShapeDtypeStruct((1, d), F32), jax.ShapeDtypeStruct((1, d), F32),
                   jax.ShapeDtypeStruct((t, d), BF16), jax.ShapeDtypeStruct((t, d), BF16),
                   jax.ShapeDtypeStruct((t, nb), BF16)],
        compiler_params=_cp(("arbitrary",), VMEM_BIG),
    )(dp_a, w_a, g_a, main_1, halo_1, main_2, halo_2, w_b, g_b, x_in, dres)


def _ffn_fwd(x, g, wg, wu, wd, tm, name):
    t, d = x.shape
    f = wg.shape[0]
    chunks = _ff_chunks(f)

    def body(x_ref, g_ref, wg_ref, wu_ref, wd_ref, xo_ref, gate_ref, up_ref):
        xv = x_ref[...]
        xhat, _ = _rms_parts(xv)
        h = (xhat * g_ref[...]).astype(BF16)
        acc = xv
        for c0, cw in chunks:
            gt = lax.dot_general(h, wg_ref[c0:c0 + cw, :], NT, preferred_element_type=F32)
            ut = lax.dot_general(h, wu_ref[c0:c0 + cw, :], NT, preferred_element_type=F32)
            gate_ref[:, c0:c0 + cw] = gt.astype(BF16)
            up_ref[:, c0:c0 + cw] = ut.astype(BF16)
            a = (gt * _sigmoid(gt) * ut).astype(BF16)
            acc = acc + jnp.dot(a, wd_ref[c0:c0 + cw, :], preferred_element_type=F32)
        xo_ref[...] = acc

    return pl.pallas_call(
        body, name=name, grid=(t // tm,),
        in_specs=[_row_spec(tm, d), _const_spec((1, d)), _const_spec((f, d)), _const_spec((f, d)), _const_spec((f, d))],
        out_specs=[_row_spec(tm, d), _row_spec(tm, f), _row_spec(tm, f)],
        out_shape=[jax.ShapeDtypeStruct((t, d), F32), jax.ShapeDtypeStruct((t, f), BF16),
                   jax.ShapeDtypeStruct((t, f), BF16)],
        compiler_params=_cp(("parallel",), VMEM_BIG),
    )(x, g, wg, wu, wd)


def _ffn_fwd_loss(x, g, wg, wu, wd, g_final, target, tm, name):
    t, d = x.shape
    f = wg.shape[0]
    chunks = _ff_chunks(f)

    def body(x_ref, g_ref, wg_ref, wu_ref, wd_ref, gf_ref, t_ref, gate_ref, up_ref, loss_ref, dx_ref, dgf_ref):
        i = pl.program_id(0)
        xv = x_ref[...]
        xhat, _ = _rms_parts(xv)
        h = (xhat * g_ref[...]).astype(BF16)
        acc = xv
        for c0, cw in chunks:
            gt = lax.dot_general(h, wg_ref[c0:c0 + cw, :], NT, preferred_element_type=F32)
            ut = lax.dot_general(h, wu_ref[c0:c0 + cw, :], NT, preferred_element_type=F32)
            gate_ref[:, c0:c0 + cw] = gt.astype(BF16)
            up_ref[:, c0:c0 + cw] = ut.astype(BF16)
            a = (gt * _sigmoid(gt) * ut).astype(BF16)
            acc = acc + jnp.dot(a, wd_ref[c0:c0 + cw, :], preferred_element_type=F32)
        xhat_o, r_o = _rms_parts(acc)
        gf = gf_ref[...]
        err = xhat_o * gf - t_ref[...]
        dy = err * (1.0 / d)
        dxhat = dy * gf
        dx_ref[...] = r_o * (dxhat - xhat_o * jnp.mean(dxhat * xhat_o, axis=-1, keepdims=True))

        @pl.when(i == 0)
        def _():
            dgf_ref[...] = jnp.zeros_like(dgf_ref)
            loss_ref[...] = jnp.zeros_like(loss_ref)

        dgf_ref[...] += jnp.sum(dy * xhat_o, axis=0, keepdims=True)
        part = jnp.sum(jnp.sum(err * err, axis=-1, keepdims=True), axis=0, keepdims=True) * (0.5 / d)
        loss_ref[...] += jnp.broadcast_to(part, loss_ref.shape)

    return pl.pallas_call(
        body, name=name, grid=(t // tm,),
        in_specs=[_row_spec(tm, d), _const_spec((1, d)), _const_spec((f, d)), _const_spec((f, d)), _const_spec((f, d)),
                  _const_spec((1, d)), _row_spec(tm, d)],
        out_specs=[_row_spec(tm, f), _row_spec(tm, f), pl.BlockSpec((8, 128), lambda i: (0, 0)), _row_spec(tm, d),
                   pl.BlockSpec((1, d), lambda i: (0, 0))],
        out_shape=[jax.ShapeDtypeStruct((t, f), BF16), jax.ShapeDtypeStruct((t, f), BF16),
                   jax.ShapeDtypeStruct((8, 128), F32), jax.ShapeDtypeStruct((t, d), F32),
                   jax.ShapeDtypeStruct((1, d), F32)],
        compiler_params=_cp(("arbitrary",), VMEM_BIG),
    )(x, g, wg, wu, wd, g_final, target)


def _ffn_bwd(dxo, xm, g, gate, up, wg, wu, wd, tm, name):
    t, d = xm.shape
    f = wg.shape[0]
    chunks = _ff_chunks(f)

    def body(dxo_ref, xm_ref, g_ref, gate_ref, up_ref, wg_ref, wu_ref, wd_ref,
             dxm_ref, dgate_ref, dup_ref, act_ref, h2_ref, dg_ref):
        i = pl.program_id(0)
        dxo_v = dxo_ref[...]
        dxo_b = dxo_v.astype(BF16)
        xhat, r = _rms_parts(xm_ref[...])
        gv = g_ref[...]
        h2_ref[...] = (xhat * gv).astype(BF16)
        dh = jnp.zeros((tm, d), F32)
        for c0, cw in chunks:
            dact = lax.dot_general(dxo_b, wd_ref[c0:c0 + cw, :], NT, preferred_element_type=F32)
            gt = gate_ref[:, c0:c0 + cw].astype(F32)
            ut = up_ref[:, c0:c0 + cw].astype(F32)
            sg = _sigmoid(gt)
            sl = gt * sg
            act_ref[:, c0:c0 + cw] = (sl * ut).astype(BF16)
            dgt = (dact * ut * (sg * (1.0 + gt * (1.0 - sg)))).astype(BF16)
            dut = (dact * sl).astype(BF16)
            dgate_ref[:, c0:c0 + cw] = dgt
            dup_ref[:, c0:c0 + cw] = dut
            dh = dh + jnp.dot(dgt, wg_ref[c0:c0 + cw, :], preferred_element_type=F32)
            dh = dh + jnp.dot(dut, wu_ref[c0:c0 + cw, :], preferred_element_type=F32)
        dxhat = dh * gv
        dx = r * (dxhat - xhat * jnp.mean(dxhat * xhat, axis=-1, keepdims=True))
        dxm_ref[...] = dxo_v + dx

        @pl.when(i == 0)
        def _():
            dg_ref[...] = jnp.zeros_like(dg_ref)

        dg_ref[...] += jnp.sum(dh * xhat, axis=0, keepdims=True)

    return pl.pallas_call(
        body, name=name, grid=(t // tm,),
        in_specs=[_row_spec(tm, d), _row_spec(tm, d), _const_spec((1, d)), _row_spec(tm, f), _row_spec(tm, f),
                  _const_spec((f, d)), _const_spec((f, d)), _const_spec((f, d))],
        out_specs=[_row_spec(tm, d), _row_spec(tm, f), _row_spec(tm, f), _row_spec(tm, f), _row_spec(tm, d),
                   pl.BlockSpec((1, d), lambda i: (0, 0))],
        out_shape=[jax.ShapeDtypeStruct((t, d), F32), jax.ShapeDtypeStruct((t, f), BF16),
                   jax.ShapeDtypeStruct((t, f), BF16), jax.ShapeDtypeStruct((t, f), BF16),
                   jax.ShapeDtypeStruct((t, d), BF16), jax.ShapeDtypeStruct((1, d), F32)],
        compiler_params=_cp(("arbitrary",), VMEM_BIG),
    )(dxo, xm, g, gate, up, wg, wu, wd)


def _wgrad(a, b, tt, name):
    t, k = a.shape
    n = b.shape[1]
    nt = t // tt

    def body(a_ref, b_ref, o_ref, acc):
        i = pl.program_id(0)

        @pl.when(i == 0)
        def _():
            acc[...] = jnp.zeros_like(acc)

        acc[...] += lax.dot_general(a_ref[...].astype(BF16), b_ref[...].astype(BF16), TN,
                                    preferred_element_type=F32)

        @pl.when(i == nt - 1)
        def _():
            o_ref[...] = acc[...].astype(BF16)

    return pl.pallas_call(
        body, name=name, grid=(nt,),
        in_specs=[_row_spec(tt, k), _row_spec(tt, n)],
        out_specs=pl.BlockSpec((k, n), lambda i: (0, 0)),
        out_shape=jax.ShapeDtypeStruct((k, n), BF16),
        scratch_shapes=[pltpu.VMEM((k, n), F32)],
        compiler_params=_cp(("arbitrary",), VMEM_BIG),
    )(a, b)


def _col_head(width):
    return lax.broadcasted_iota(jnp.int32, (1, width), 1) // HEAD_DIM


def _keep_head(a, colh, h):
    return jnp.where(colh == h, a, jnp.zeros_like(a))


def _softmax_cols(s, sink=None):
    m = jnp.max(s, axis=0, keepdims=True)
    if sink is not None:
        m = jnp.maximum(m, sink)
    p = jnp.exp(s - m)
    l = jnp.sum(p, axis=0, keepdims=True)
    if sink is None:
        return p * (1.0 / l), None
    es = jnp.exp(sink - m)
    inv = 1.0 / (l + es)
    return p * inv, es * inv


def _add4(v):
    return (v[0] + v[1]) + (v[2] + v[3])


def _mem_attn_fwd(qm, mk, mv):
    colh = _col_head(mk.shape[1])
    mks = mk * SCALE
    heads = range(N_MEM_HEADS)
    ss = [lax.dot_general(_keep_head(mks, colh, h), qm, NT, preferred_element_type=F32) for h in heads]
    ps = [_softmax_cols(s)[0].astype(BF16) for s in ss]
    return _add4([lax.dot_general(ps[h], _keep_head(mv, colh, h), TN, preferred_element_type=F32) for h in heads])


def _mem_attn_bwd(qm, dy_b, mk, mv):
    colh = _col_head(mk.shape[1])
    mks = mk * SCALE
    heads = range(N_MEM_HEADS)
    khs = [_keep_head(mks, colh, h) for h in heads]
    vhs = [_keep_head(mv, colh, h) for h in heads]
    ss = [lax.dot_general(khs[h], qm, NT, preferred_element_type=F32) for h in heads]
    dps = [lax.dot_general(vhs[h], dy_b, NT, preferred_element_type=F32) for h in heads]
    pbs, dsbs = [], []
    for h in heads:
        p, _ = _softmax_cols(ss[h])
        ds = p * (dps[h] - jnp.sum(p * dps[h], axis=0, keepdims=True))
        pbs.append(p.astype(BF16))
        dsbs.append(ds.astype(BF16))
    dq = _add4([lax.dot_general(dsbs[h], khs[h], TN, preferred_element_type=F32) for h in heads])
    dmk = _add4([jnp.where(colh == h, jnp.dot(dsbs[h], qm, preferred_element_type=F32) * SCALE, 0.0) for h in heads])
    dmv = _add4([jnp.where(colh == h, jnp.dot(pbs[h], dy_b, preferred_element_type=F32), 0.0) for h in heads])
    return dq, dmk, dmv


def _shift_down(v, halo, k):
    rolled = pltpu.roll(v, k, 0)
    hrolled = pltpu.roll(halo, k, 0)[0:8]
    rows = lax.broadcasted_iota(jnp.int32, (8, v.shape[1]), 0)
    first = jnp.where(rows < k, hrolled, rolled[0:8])
    return jnp.concatenate([first, rolled[8:]], axis=0)


def _shift_up(v, halo, k):
    n = v.shape[0]
    rolled = pltpu.roll(v, n - k, 0)
    hrolled = pltpu.roll(halo, 8 - k, 0)[0:8]
    rows = lax.broadcasted_iota(jnp.int32, (8, v.shape[1]), 0)
    last = jnp.where(rows >= 8 - k, hrolled, rolled[n - 8:])
    return jnp.concatenate([rolled[:n - 8], last], axis=0)


def _conv_parts(p, ph, cw, first_tile, cwid):
    u = p[:, 0:cwid].astype(F32)
    bg = p[:, cwid:2 * cwid].astype(F32)
    cg = p[:, 2 * cwid:3 * cwid].astype(F32)
    v = cg * u
    vh = ph[:, 2 * cwid:3 * cwid].astype(F32) * ph[:, 0:cwid].astype(F32)
    vh = jnp.where(first_tile, 0.0, vh)
    v1 = _shift_down(v, vh, 1)
    v2 = _shift_down(v, vh, 2)
    conv = cw[0:1, :] * v2 + cw[1:2, :] * v1 + cw[2:3, :] * v
    return u, bg, cg, v, v1, v2, conv


def _halo_prev_spec(rows, n, tm):
    per = tm // rows
    return pl.BlockSpec((rows, n), lambda i: (jnp.maximum(i * per - 1, 0), 0))


def _halo_next_spec(rows, n, tm, t):
    per = tm // rows
    last = t // rows - 1
    return pl.BlockSpec((rows, n), lambda i: (jnp.minimum((i + 1) * per, last), 0))


def _mix_a_fwd(x, proj, convw, memkv, layer, wout, tm, name):
    t, d = x.shape
    n_mem = memkv.shape[0]
    mw = N_MEM_HEADS * HEAD_DIM
    cwid = d - mw
    pw = proj.shape[1]

    def body(x_ref, p_ref, ph_ref, cw_ref, mkv_ref, wo_ref, xo_ref, cat_ref):
        i = pl.program_id(0)
        p = p_ref[...]
        _, bg, _, _, _, _, conv = _conv_parts(p, ph_ref[...], cw_ref[...], i == 0, cwid)
        ytok = (bg * conv).astype(BF16)
        mkv = mkv_ref[...]
        ymem = _mem_attn_fwd(p[:, 3 * cwid:3 * cwid + mw], mkv[:, 0:mw], mkv[:, mw:2 * mw])
        cat = jnp.concatenate([ytok, ymem.astype(BF16)], axis=1)
        cat_ref[...] = cat
        xo_ref[...] = x_ref[...] + jnp.dot(cat, wo_ref[...], preferred_element_type=F32)

    return pl.pallas_call(
        body, name=name, grid=(t // tm,),
        in_specs=[_row_spec(tm, d), _row_spec(tm, pw), _halo_prev_spec(16, pw, tm), _const_spec((3, cwid)),
                  pl.BlockSpec((n_mem, 2 * mw), lambda i: (0, layer)), _const_spec((d, d))],
        out_specs=[_row_spec(tm, d), _row_spec(tm, d)],
        out_shape=[jax.ShapeDtypeStruct((t, d), F32), jax.ShapeDtypeStruct((t, d), BF16)],
        compiler_params=_cp(("parallel",), VMEM_BIG),
    )(x, proj, proj, convw, memkv, wout)


def _mix_a_bwd(dxm, proj, convw, memkv, layer, wout, tm, name):
    t, d = dxm.shape
    n_mem = memkv.shape[0]
    mw = N_MEM_HEADS * HEAD_DIM
    cwid = d - mw
    pw = proj.shape[1]
    nt = t // tm

    def body(dx_ref, dxn_ref, p_ref, ph_ref, pn_ref, cw_ref, mkv_ref, wo_ref,
             dp_ref, dcw_ref, dmkv_ref, dmk_acc, dmv_acc):
        i = pl.program_id(0)
        p = p_ref[...]
        cw = cw_ref[...]
        wo = wo_ref[...]
        u, bg, cg, v, v1, v2, conv = _conv_parts(p, ph_ref[...], cw, i == 0, cwid)
        dcat = lax.dot_general(dx_ref[...].astype(BF16), wo, NT, preferred_element_type=F32)
        dytok = dcat[:, 0:cwid]
        dymem_b = dcat[:, cwid:d].astype(BF16)
        pn = pn_ref[...]
        dcat_n = lax.dot_general(dxn_ref[...].astype(BF16), wo[0:cwid, :], NT, preferred_element_type=F32)
        dconv_n = jnp.where(i == nt - 1, 0.0, dcat_n * pn[:, cwid:2 * cwid].astype(F32))
        dbg = dytok * conv
        dconv = dytok * bg
        dv = cw[2:3, :] * dconv + cw[1:2, :] * _shift_up(dconv, dconv_n, 1) + cw[0:1, :] * _shift_up(dconv, dconv_n, 2)
        du = dv * cg
        dcg = dv * u
        rows8 = lax.broadcasted_iota(jnp.int32, (8, cwid), 0)
        dcw = (jnp.where(rows8 == 0, jnp.sum(dconv * v2, axis=0, keepdims=True), 0.0)
               + jnp.where(rows8 == 1, jnp.sum(dconv * v1, axis=0, keepdims=True), 0.0)
               + jnp.where(rows8 == 2, jnp.sum(dconv * v, axis=0, keepdims=True), 0.0))
        mkv = mkv_ref[...]
        qm = p[:, 3 * cwid:3 * cwid + mw]
        dqm, dmk, dmv = _mem_attn_bwd(qm, dymem_b, mkv[:, 0:mw], mkv[:, mw:2 * mw])
        dp_ref[...] = jnp.concatenate([du.astype(BF16), dbg.astype(BF16), dcg.astype(BF16), dqm.astype(BF16)], axis=1)

        @pl.when(i == 0)
        def _():
            dcw_ref[...] = jnp.zeros_like(dcw_ref)
            dmk_acc[...] = jnp.zeros_like(dmk_acc)
            dmv_acc[...] = jnp.zeros_like(dmv_acc)

        dcw_ref[...] += dcw
        dmk_acc[...] += dmk
        dmv_acc[...] += dmv

        @pl.when(i == nt - 1)
        def _():
            dmkv_ref[...] = jnp.concatenate([dmk_acc[...], dmv_acc[...]], axis=1)

    return pl.pallas_call(
        body, name=name, grid=(nt,),
        in_specs=[_row_spec(tm, d), _halo_next_spec(16, d, tm, t), _row_spec(tm, pw), _halo_prev_spec(16, pw, tm),
                  _halo_next_spec(16, pw, tm, t), _const_spec((3, cwid)),
                  pl.BlockSpec((n_mem, 2 * mw), lambda i: (0, layer)), _const_spec((d, d))],
        out_specs=[_row_spec(tm, pw), pl.BlockSpec((8, cwid), lambda i: (0, 0)),
                   pl.BlockSpec((n_mem, 2 * mw), lambda i: (0, 0))],
        out_shape=[jax.ShapeDtypeStruct((t, pw), BF16),
                   jax.ShapeDtypeStruct((8, cwid), F32), jax.ShapeDtypeStruct((n_mem, 2 * mw), F32)],
        scratch_shapes=[pltpu.VMEM((n_mem, mw), F32), pltpu.VMEM((n_mem, mw), F32)],
        compiler_params=_cp(("arbitrary",), VMEM_BIG),
    )(dxm, dxm, proj, proj, proj, convw, memkv, wout)


def _rel_tables():
    qi = np.arange(BLOCK, dtype=np.int32)[:, None]
    kj = np.arange(2 * BLOCK, dtype=np.int32)[None, :]
    dist = qi + BLOCK - kj
    inw = (dist >= 0) & (dist < BLOCK)
    max_exact = REL_BUCKETS // 2
    dd = np.maximum(np.maximum(dist, 0), 1).astype(np.float32)
    large = max_exact + (np.log(dd / np.float32(max_exact)) / np.float32(math.log(REL_MAX_DIST / max_exact))
                         * np.float32(REL_BUCKETS - max_exact)).astype(np.int32)
    large = np.minimum(large, REL_BUCKETS - 1)
    bucket = np.where(np.maximum(dist, 0) < max_exact, np.maximum(dist, 0), large)
    return np.where(inw, bucket, -1).astype(np.int32)


def _bias_tables(rel_bias, sinks, name):
    bucket_t = jnp.asarray(_rel_tables().T)

    def body(rb_ref, sk_ref, bk_ref, bias_ref, sink_ref):
        bk = bk_ref[...]
        prev = lax.broadcasted_iota(jnp.int32, bk.shape, 0) < BLOCK
        for h in range(N_KV_HEADS):
            for j in range(GROUP):
                head = GROUP * h + j
                acc = jnp.full(bk.shape, NEG, F32)
                for b in range(REL_BUCKETS):
                    acc = jnp.where(bk == b, rb_ref[b, head], acc)
                bias_ref[h, :, j * BLOCK:(j + 1) * BLOCK] = acc
                bias_ref[N_KV_HEADS + h, :, j * BLOCK:(j + 1) * BLOCK] = jnp.where(prev, NEG, acc)
                sink_ref[h, :, j * BLOCK:(j + 1) * BLOCK] = jnp.full((8, BLOCK), sk_ref[0, head], F32)

    smem = pl.BlockSpec(memory_space=pltpu.SMEM)
    return pl.pallas_call(
        body, name=name,
        in_specs=[smem, smem, pl.BlockSpec(memory_space=pltpu.VMEM)],
        out_specs=[pl.BlockSpec(memory_space=pltpu.VMEM), pl.BlockSpec(memory_space=pltpu.VMEM)],
        out_shape=[jax.ShapeDtypeStruct((2 * N_KV_HEADS, 2 * BLOCK, GROUP * BLOCK), F32),
                   jax.ShapeDtypeStruct((N_KV_HEADS, 8, GROUP * BLOCK), F32)],
    )(rel_bias, sinks.reshape(1, N_Q_HEADS), bucket_t)


def _bias_bwd(dbias_a, dbias_b, name):
    bucket_t = jnp.asarray(_rel_tables().T)

    def body(da_ref, db_ref, bk_ref, o_ref):
        bk = bk_ref[...]
        ri = lax.broadcasted_iota(jnp.int32, (REL_BUCKETS, 128), 0)
        ci = lax.broadcasted_iota(jnp.int32, (REL_BUCKETS, 128), 1)
        out = jnp.zeros((REL_BUCKETS, 128), F32)
        for h in range(N_KV_HEADS):
            dsum = da_ref[h] + db_ref[h]
            for j in range(GROUP):
                head = GROUP * h + j
                seg = dsum[:, j * BLOCK:(j + 1) * BLOCK]
                for b in range(REL_BUCKETS):
                    val = jnp.sum(jnp.sum(jnp.where(bk == b, seg, 0.0), axis=0, keepdims=True), axis=1, keepdims=True)
                    out = out + jnp.where((ri == b) & (ci == head), val, 0.0)
        o_ref[...] = out

    vm = pl.BlockSpec(memory_space=pltpu.VMEM)
    return pl.pallas_call(
        body, name=name, in_specs=[vm, vm, vm], out_specs=vm,
        out_shape=jax.ShapeDtypeStruct((REL_BUCKETS, 128), F32),
    )(dbias_a, dbias_b, bucket_t)


def _stack_members(ref, r0, width):
    blk = ref[pl.ds(r0, BLOCK), 0:GROUP * width]
    return jnp.concatenate([blk[:, j * width:(j + 1) * width] for j in range(GROUP)], axis=0)


def _mix_b_fwd(x, qp, kv, bias, sinkt, memkv, layer, wout, tm, name):
    t, d = x.shape
    n_mem = memkv.shape[0]
    mw = N_MEM_HEADS * HEAD_DIM
    qw = d - mw
    kw = N_KV_HEADS * HEAD_DIM
    nb = tm // BLOCK
    rows = GROUP * BLOCK

    def body(x_ref, q_ref, kv_ref, kvh_ref, bias_ref, sink_ref, mkv_ref, wo_ref, xo_ref, cat_ref, kvx, ytok):
        i = pl.program_id(0)
        kvx[0:BLOCK, :] = kvh_ref[...]
        kvx[BLOCK:BLOCK + tm, :] = kv_ref[...]
        colh = _col_head(kw)

        def blk(b, carry):
            r0 = pl.multiple_of(b * BLOCK, BLOCK)
            win = kvx[pl.ds(r0, 2 * BLOCK), :]
            kwin = win[:, 0:kw] * SCALE
            vwin = win[:, kw:2 * kw]
            qs = _stack_members(q_ref, r0, kw)
            first = ((i == 0) & (b == 0)).astype(jnp.int32) * N_KV_HEADS
            heads = range(N_KV_HEADS)
            ss = [lax.dot_general(_keep_head(kwin, colh, h), qs, NT, preferred_element_type=F32) for h in heads]
            ps = [_softmax_cols(ss[h] + bias_ref[first + h], sink_ref[h][0:1, :])[0].astype(BF16) for h in heads]
            o = _add4([lax.dot_general(ps[h], _keep_head(vwin, colh, h), TN, preferred_element_type=F32)
                       for h in heads])
            for j in range(GROUP):
                ytok[pl.ds(r0, BLOCK), j * kw:(j + 1) * kw] = o[j * BLOCK:(j + 1) * BLOCK].astype(BF16)
            return carry

        for b_static in range(nb):
            blk(b_static, 0)
        mkv = mkv_ref[...]
        ymem = _mem_attn_fwd(q_ref[:, qw:d], mkv[:, 0:mw], mkv[:, mw:2 * mw])
        cat = jnp.concatenate([ytok[...], ymem.astype(BF16)], axis=1)
        cat_ref[...] = cat
        xo_ref[...] = x_ref[...] + jnp.dot(cat, wo_ref[...], preferred_element_type=F32)

    return pl.pallas_call(
        body, name=name, grid=(t // tm,),
        in_specs=[_row_spec(tm, d), _row_spec(tm, d), _row_spec(tm, 2 * kw), _halo_prev_spec(BLOCK, 2 * kw, tm),
                  _const_spec((2 * N_KV_HEADS, 2 * BLOCK, rows)), _const_spec((N_KV_HEADS, 8, rows)),
                  pl.BlockSpec((n_mem, 2 * mw), lambda i: (0, layer)), _const_spec((d, d))],
        out_specs=[_row_spec(tm, d), _row_spec(tm, d)],
        out_shape=[jax.ShapeDtypeStruct((t, d), F32), jax.ShapeDtypeStruct((t, d), BF16)],
        scratch_shapes=[pltpu.VMEM((tm + BLOCK, 2 * kw), BF16), pltpu.VMEM((tm, qw), BF16)],
        compiler_params=_cp(("parallel",), VMEM_BIG),
    )(x, qp, kv, kv, bias, sinkt, memkv, wout)


def _mix_b_bwd(dxm, qp, kv, bias, sinkt, memkv, layer, wout, tm, name):
    t, d = dxm.shape
    n_mem = memkv.shape[0]
    mw = N_MEM_HEADS * HEAD_DIM
    qw = d - mw
    kw = N_KV_HEADS * HEAD_DIM
    nb = tm // BLOCK
    nt = t // tm
    rows = GROUP * BLOCK

    def body(dx_ref, q_ref, kv_ref, kvh_ref, bias_ref, sink_ref, mkv_ref, wo_ref,
             dq_ref, dkv_ref, dkvh_ref, dbias_ref, dsink_ref, dmkv_ref,
             kvx, dkvx, dcat_s, dmk_acc, dmv_acc):
        i = pl.program_id(0)

        @pl.when(i == 0)
        def _():
            dbias_ref[...] = jnp.zeros_like(dbias_ref)
            dsink_ref[...] = jnp.zeros_like(dsink_ref)
            dmk_acc[...] = jnp.zeros_like(dmk_acc)
            dmv_acc[...] = jnp.zeros_like(dmv_acc)

        kvx[0:BLOCK, :] = kvh_ref[...]
        kvx[BLOCK:BLOCK + tm, :] = kv_ref[...]
        dkvx[...] = jnp.zeros_like(dkvx)
        dcat_s[...] = lax.dot_general(dx_ref[...].astype(BF16), wo_ref[...], NT,
                                      preferred_element_type=F32).astype(BF16)
        colh = _col_head(kw)
        lane8 = lax.broadcasted_iota(jnp.int32, (8, 128), 1)

        def blk(b, carry):
            r0 = pl.multiple_of(b * BLOCK, BLOCK)
            win = kvx[pl.ds(r0, 2 * BLOCK), :]
            kwin = win[:, 0:kw] * SCALE
            vwin = win[:, kw:2 * kw]
            qs = _stack_members(q_ref, r0, kw)
            dos = _stack_members(dcat_s, r0, kw)
            first = ((i == 0) & (b == 0)).astype(jnp.int32) * N_KV_HEADS
            heads = range(N_KV_HEADS)
            khs = [_keep_head(kwin, colh, h) for h in heads]
            vhs = [_keep_head(vwin, colh, h) for h in heads]
            ss = [lax.dot_general(khs[h], qs, NT, preferred_element_type=F32) for h in heads]
            dps = [lax.dot_general(vhs[h], dos, NT, preferred_element_type=F32) for h in heads]
            dsink = jnp.zeros((8, 128), F32)
            pbs, dsbs = [], []
            for h in heads:
                p, sinkp = _softmax_cols(ss[h] + bias_ref[first + h], sink_ref[h][0:1, :])
                delta = jnp.sum(p * dps[h], axis=0, keepdims=True)
                ds = p * (dps[h] - delta)
                dbias_ref[h] += ds
                sd = sinkp * delta
                for j in range(GROUP):
                    val = -jnp.sum(sd[:, j * BLOCK:(j + 1) * BLOCK], axis=1, keepdims=True)
                    dsink = dsink + jnp.where(lane8 == 4 * j + h, val, 0.0)
                pbs.append(p.astype(BF16))
                dsbs.append(ds.astype(BF16))
            dq = _add4([lax.dot_general(dsbs[h], khs[h], TN, preferred_element_type=F32) for h in heads])
            dk = _add4([jnp.where(colh == h, jnp.dot(dsbs[h], qs, preferred_element_type=F32) * SCALE, 0.0)
                        for h in heads])
            dv = _add4([jnp.where(colh == h, jnp.dot(pbs[h], dos, preferred_element_type=F32), 0.0) for h in heads])
            for j in range(GROUP):
                dq_ref[pl.ds(r0, BLOCK), j * kw:(j + 1) * kw] = dq[j * BLOCK:(j + 1) * BLOCK].astype(BF16)
            dsink_ref[...] += dsink
            dkvx[pl.ds(r0, 2 * BLOCK), :] += jnp.concatenate([dk, dv], axis=1)
            return carry

        for b_static in range(nb):
            blk(b_static, 0)
        dkvh_ref[0] = dkvx[0:BLOCK, :]
        dkv_ref[...] = dkvx[BLOCK:BLOCK + tm, :]

        mkv = mkv_ref[...]
        dqm, dmk, dmv = _mem_attn_bwd(q_ref[:, qw:d], dcat_s[:, qw:d], mkv[:, 0:mw], mkv[:, mw:2 * mw])
        dq_ref[:, qw:d] = dqm.astype(BF16)
        dmk_acc[...] += dmk
        dmv_acc[...] += dmv

        @pl.when(i == nt - 1)
        def _():
            dmkv_ref[...] = jnp.concatenate([dmk_acc[...], dmv_acc[...]], axis=1)

    return pl.pallas_call(
        body, name=name, grid=(nt,),
        in_specs=[_row_spec(tm, d), _row_spec(tm, d), _row_spec(tm, 2 * kw), _halo_prev_spec(BLOCK, 2 * kw, tm),
                  _const_spec((2 * N_KV_HEADS, 2 * BLOCK, rows)), _const_spec((N_KV_HEADS, 8, rows)),
                  pl.BlockSpec((n_mem, 2 * mw), lambda i: (0, layer)), _const_spec((d, d))],
        out_specs=[_row_spec(tm, d), _row_spec(tm, 2 * kw),
                   pl.BlockSpec((1, BLOCK, 2 * kw), lambda i: (i, 0, 0)),
                   pl.BlockSpec((N_KV_HEADS, 2 * BLOCK, rows), lambda i: (0, 0, 0)),
                   pl.BlockSpec((8, 128), lambda i: (0, 0)),
                   pl.BlockSpec((n_mem, 2 * mw), lambda i: (0, 0))],
        out_shape=[jax.ShapeDtypeStruct((t, d), BF16),
                   jax.ShapeDtypeStruct((t, 2 * kw), F32), jax.ShapeDtypeStruct((nt, BLOCK, 2 * kw), F32),
                   jax.ShapeDtypeStruct((N_KV_HEADS, 2 * BLOCK, rows), F32), jax.ShapeDtypeStruct((8, 128), F32),
                   jax.ShapeDtypeStruct((n_mem, 2 * mw), F32)],
        scratch_shapes=[pltpu.VMEM((tm + BLOCK, 2 * kw), BF16), pltpu.VMEM((tm + BLOCK, 2 * kw), F32),
                        pltpu.VMEM((tm, d), BF16), pltpu.VMEM((n_mem, mw), F32), pltpu.VMEM((n_mem, mw), F32)],
        compiler_params=_cp(("arbitrary",), VMEM_BIG),
    )(dxm, qp, kv, kv, bias, sinkt, memkv, wout)


def _adam_math(w, g, m, v):
    m2 = ADAM_B1 * m + (1.0 - ADAM_B1) * g
    v2 = ADAM_B2 * v + (1.0 - ADAM_B2) * (g * g)
    m_hat = m2 / (1.0 - ADAM_B1 ** ADAM_STEP)
    v_hat = v2 / (1.0 - ADAM_B2 ** ADAM_STEP)
    delta = -ADAM_LR * (m_hat / (jnp.sqrt(v_hat) + ADAM_EPS) + ADAM_WD * w)
    return delta, m2, v2


def _adamw_sharded(w, land, m, v, name, after=()):
    nl, r, c = w.shape
    tr = max(cand for cand in range(16, r + 1, 16) if r % cand == 0 and cand * c <= 512 * 1024)

    def body(w_ref, a_ref, m_ref, v_ref, *rest):
        g_ref, d_ref, mo_ref, vo_ref = rest[-4:]
        g = a_ref[0, 0].astype(F32) + a_ref[1, 0].astype(F32)
        for k in range(1, N_CHIPS):
            g = g + (a_ref[2 * k, 0].astype(F32) + a_ref[2 * k + 1, 0].astype(F32))
        delta, m2, v2 = _adam_math(w_ref[0], g, m_ref[0], v_ref[0])
        g_ref[0] = g
        d_ref[0] = delta
        mo_ref[0] = m2
        vo_ref[0] = v2

    rs = pl.BlockSpec((1, tr, c), lambda l, i: (l, i, 0))
    ps = pl.BlockSpec((2 * N_CHIPS, 1, tr, c), lambda l, i: (0, l, i, 0))
    sd = jax.ShapeDtypeStruct((nl, r, c), F32)
    return pl.pallas_call(
        body, name=name, grid=(nl, r // tr),
        in_specs=[rs, ps, rs, rs] + [pl.BlockSpec(memory_space=pl.ANY)] * len(after),
        out_specs=[rs, rs, rs, rs], out_shape=[sd, sd, sd, sd],
        compiler_params=_cp(("parallel", "parallel"), VMEM_BIG),
    )(w, land, m, v, *after)


def _adamw_packed(w, g, m, v, name):
    def body(w_ref, g_ref, m_ref, v_ref, d_ref, mo_ref, vo_ref):
        delta, m2, v2 = _adam_math(w_ref[...], g_ref[...], m_ref[...], v_ref[...])
        d_ref[...] = delta
        mo_ref[...] = m2
        vo_ref[...] = v2

    vm = pl.BlockSpec(memory_space=pltpu.VMEM)
    sd = jax.ShapeDtypeStruct(w.shape, F32)
    return pl.pallas_call(body, name=name, in_specs=[vm] * 4, out_specs=[vm] * 3, out_shape=[sd] * 3)(w, g, m, v)


def _place():
    return lax.axis_index("x"), lax.axis_index("y"), lax.axis_index("c")


def _hbm(a):
    return pltpu.with_memory_space_constraint(a, pltpu.HBM)


def _peers(x, y, c, both_cores):
    chips = [(1 - x, y), (x, 1 - y), (1 - x, 1 - y)]
    if not both_cores:
        return [(px, py, c) for px, py in chips]
    return [(px, py, pc) for px, py in chips for pc in (c, 1 - c)] + [(x, y, 1 - c)]


def _chip_copy(src, land, gather, layer, chip_src, slot, send_sem, recv_sem, peer):
    s = src if gather else src.at[chip_src]
    d = land.at[slot] if layer is None else land.at[slot, layer]
    return pltpu.make_async_remote_copy(src_ref=s, dst_ref=d, send_sem=send_sem, recv_sem=recv_sem,
                                        device_id=peer, device_id_type=MESH)


def _own_copy(src, land, gather, layer, chip, slot, own_sem):
    s = src if gather else src.at[chip]
    d = land.at[slot] if layer is None else land.at[slot, layer]
    return pltpu.make_async_copy(s, d, own_sem)


def _exchange_start(srcs, lands, gather, layers, both_cores, after, name):
    n = len(srcs)
    npeer = 7 if both_cores else 3
    hbm = pl.BlockSpec(memory_space=pltpu.HBM)
    sem = pl.BlockSpec(memory_space=pltpu.SEMAPHORE)

    def body(*refs):
        ins, lds = refs[:n], refs[n:2 * n]
        first_out = 2 * n + len(after)
        send_sems, recv_sems, own_sems, token = refs[first_out], refs[first_out + 1], refs[first_out + 2], refs[-1]
        x, y, c = _place()
        slot = 2 * x + y if gather is True else 2 * (2 * x + y) + c
        for t in range(n):
            _own_copy(ins[t], lds[t], gather, layers[t], 2 * x + y, slot, own_sems.at[t]).start()
            for r, peer in enumerate(_peers(x, y, c, both_cores)):
                _chip_copy(ins[t], lds[t], gather, layers[t], 2 * peer[0] + peer[1], slot,
                           send_sems.at[npeer * t + r], recv_sems.at[npeer * t + r], peer).start()
        token[...] = jnp.zeros_like(token)

    both = list(srcs) + list(lands)
    outs = pl.pallas_call(
        body, name=name, in_specs=[hbm] * (2 * n) + [pl.BlockSpec(memory_space=pl.ANY)] * len(after),
        out_specs=(sem, sem, sem, *([hbm] * (2 * n)), pl.BlockSpec(memory_space=pltpu.VMEM)),
        out_shape=(pltpu.SemaphoreType.DMA((npeer * n,)), pltpu.SemaphoreType.DMA((npeer * n,)),
                   pltpu.SemaphoreType.DMA((n,)),
                   *[pltpu.HBM(a.shape, a.dtype) for a in both], jax.ShapeDtypeStruct((8, 128), F32)),
        input_output_aliases={t: 3 + t for t in range(2 * n)},
        compiler_params=_cp(has_side_effects=pltpu.SideEffectType.DATAFLOW_SIDE_EFFECTING),
    )(*[_hbm(a) for a in both], *after)
    return dict(send=outs[0], recv=outs[1], own=outs[2], srcs=list(outs[3:3 + n]), lands=list(outs[3 + n:3 + 2 * n]),
                token=outs[-1], gather=gather, layers=list(layers), both_cores=both_cores)


def _exchange_wait(groups, lands, land_ids, after, name):
    flat = [s for g in groups for s in g["srcs"]]
    ns, nl, ng, na = len(flat), len(lands), len(groups), len(after)
    hbm = pl.BlockSpec(memory_space=pltpu.HBM)
    sem = pl.BlockSpec(memory_space=pltpu.SEMAPHORE)

    def body(*refs):
        srcs, lds = refs[:ns], refs[ns:ns + nl]
        sems = refs[ns + nl:ns + nl + 3 * ng]
        x, y, c = _place()
        k = 0
        for gi, g in enumerate(groups):
            peers = _peers(x, y, c, g["both_cores"])
            for t in range(len(g["srcs"])):
                _own_copy(srcs[k], lds[land_ids[gi][t]], g["gather"], g["layers"][t], 0, 0, sems[3 * gi + 2].at[t]).wait()
                for r, peer in enumerate(peers):
                    cp = _chip_copy(srcs[k], lds[land_ids[gi][t]], g["gather"], g["layers"][t], 0, 0,
                                    sems[3 * gi].at[len(peers) * t + r], sems[3 * gi + 1].at[len(peers) * t + r], peer)
                    cp.wait_send()
                    cp.wait_recv()
                k += 1

    both = flat + list(lands)
    sem_args = [a for g in groups for a in (g["send"], g["recv"], g["own"])]
    outs = pl.pallas_call(
        body, name=name,
        in_specs=[hbm] * (ns + nl) + [sem] * (3 * ng) + [pl.BlockSpec(memory_space=pl.ANY)] * na,
        out_specs=[hbm] * (ns + nl),
        out_shape=[pltpu.HBM(a.shape, a.dtype) for a in both],
        input_output_aliases={t: t for t in range(ns + nl)},
        compiler_params=_cp(has_side_effects=pltpu.SideEffectType.DATAFLOW_SIDE_EFFECTING),
    )(*both, *sem_args, *after)
    return list(outs[ns:])


def _fill_copy(land, layer, slot, send_sem, recv_sem, x, y, c):
    mine = land.at[slot] if layer is None else land.at[slot, layer]
    return pltpu.make_async_remote_copy(src_ref=mine, dst_ref=mine, send_sem=send_sem, recv_sem=recv_sem,
                                        device_id=(x, y, 1 - c), device_id_type=MESH)


def _core_fill_start(lands, layers, after, name):
    n = len(lands)
    hbm = pl.BlockSpec(memory_space=pltpu.HBM)
    sem = pl.BlockSpec(memory_space=pltpu.SEMAPHORE)

    def body(*refs):
        ins = refs[:n]
        first_out = n + len(after)
        send_sems, recv_sems, token = refs[first_out], refs[first_out + 1], refs[-1]
        x, y, c = _place()
        for t in range(n):
            for k in range(N_CHIPS):
                _fill_copy(ins[t], layers[t], 2 * k + c, send_sems.at[N_CHIPS * t + k], recv_sems.at[N_CHIPS * t + k],
                           x, y, c).start()
        token[...] = jnp.zeros_like(token)

    outs = pl.pallas_call(
        body, name=name, in_specs=[hbm] * n + [pl.BlockSpec(memory_space=pl.ANY)] * len(after),
        out_specs=(sem, sem, *([hbm] * n), pl.BlockSpec(memory_space=pltpu.VMEM)),
        out_shape=(pltpu.SemaphoreType.DMA((N_CHIPS * n,)), pltpu.SemaphoreType.DMA((N_CHIPS * n,)),
                   *[pltpu.HBM(a.shape, a.dtype) for a in lands], jax.ShapeDtypeStruct((8, 128), F32)),
        input_output_aliases={t: 2 + t for t in range(n)},
        compiler_params=_cp(has_side_effects=pltpu.SideEffectType.DATAFLOW_SIDE_EFFECTING),
    )(*[_hbm(a) for a in lands], *after)
    return dict(send=outs[0], recv=outs[1], lands=list(outs[2:2 + n]), token=outs[-1], layers=list(layers))


def _core_fill_wait(fill, after, name):
    lands, layers = fill["lands"], fill["layers"]
    n = len(lands)
    hbm = pl.BlockSpec(memory_space=pltpu.HBM)
    sem = pl.BlockSpec(memory_space=pltpu.SEMAPHORE)

    def body(*refs):
        ins = refs[:n]
        send_sems, recv_sems = refs[n], refs[n + 1]
        x, y, c = _place()
        for t in range(n):
            for k in range(N_CHIPS):
                cp = _fill_copy(ins[t], layers[t], 2 * k + c, send_sems.at[N_CHIPS * t + k],
                                recv_sems.at[N_CHIPS * t + k], x, y, c)
                cp.wait_send()
                cp.wait_recv()

    outs = pl.pallas_call(
        body, name=name, in_specs=[hbm] * n + [sem, sem] + [pl.BlockSpec(memory_space=pl.ANY)] * len(after),
        out_specs=[hbm] * n, out_shape=[pltpu.HBM(a.shape, a.dtype) for a in lands],
        input_output_aliases={t: t for t in range(n)},
        compiler_params=_cp(has_side_effects=pltpu.SideEffectType.DATAFLOW_SIDE_EFFECTING),
    )(*lands, fill["send"], fill["recv"], *after)
    return list(outs)


def _sum_slots(land, name):
    _, r, c = land.shape
    vm = pl.BlockSpec(memory_space=pltpu.VMEM)

    def body(l_ref, sum_ref):
        total = l_ref[0]
        for k in range(1, 8):
            total = total + l_ref[k]
        sum_ref[...] = total

    return pl.pallas_call(body, name=name, in_specs=[vm], out_specs=vm, out_shape=jax.ShapeDtypeStruct((r, c), F32))(land)


def _pack(items):
    rows = []
    for a in items:
        flat = a.astype(F32).reshape(-1)
        pad = (-flat.shape[0]) % PACK_W
        rows.append(jnp.pad(flat, (0, pad)).reshape(-1, PACK_W))
    out = jnp.concatenate(rows, axis=0)
    pad_r = (-out.shape[0]) % 8
    return jnp.pad(out, ((0, pad_r), (0, 0)))


def _unpack(pack, shapes):
    outs, row = [], 0
    for s in shapes:
        n = int(np.prod(s))
        nr = -(-n // PACK_W)
        outs.append(pack[row:row + nr].reshape(-1)[:n].reshape(s))
        row += nr
    return outs


def _heads_to_member_major(w, axis):
    shp = w.shape
    pre, post = shp[:axis], shp[axis + 1:]
    w4 = w.reshape(pre + (N_KV_HEADS, GROUP, HEAD_DIM) + post)
    w4 = jnp.swapaxes(w4, len(pre), len(pre) + 1)
    return w4.reshape(shp)


def _heads_to_kv_major(w, axis):
    shp = w.shape
    pre, post = shp[:axis], shp[axis + 1:]
    w4 = w.reshape(pre + (GROUP, N_KV_HEADS, HEAD_DIM) + post)
    w4 = jnp.swapaxes(w4, len(pre), len(pre) + 1)
    return w4.reshape(shp)


def kernel(x, mem, norm_mix, norm_ffn, a_w_in, a_conv_w, a_w_out, kv_norm, w_kv, b_w_q, b_sinks, b_w_out, rel_bias, mem_norm, w_mem_kv, w_gate, w_up, w_down, final_norm, loss_target, m_norm_mix, m_norm_ffn, m_a_w_in, m_a_conv_w, m_a_w_out, m_kv_norm, m_w_kv, m_b_w_q, m_b_sinks, m_b_w_out, m_rel_bias, m_mem_norm, m_w_mem_kv, m_w_gate, m_w_up, m_w_down, m_final_norm, v_norm_mix, v_norm_ffn, v_a_w_in, v_a_conv_w, v_a_w_out, v_kv_norm, v_w_kv, v_b_w_q, v_b_sinks, v_b_w_out, v_rel_bias, v_mem_norm, v_w_mem_kv, v_w_gate, v_w_up, v_w_down, v_final_norm):
    t, d = x.shape[1], x.shape[2]
    tm = 512 if t % 512 == 0 and t >= 2048 else 256
    tl = 2 * tm if t % (2 * tm) == 0 else tm
    x0 = x.reshape(t, d)
    target = loss_target.reshape(t, d)
    mem2 = mem.reshape(mem.shape[1], d)
    n_mem = mem2.shape[0]
    ax, ay, ac = _place()
    chip = 2 * ax + ay
    cwid = a_conv_w.shape[2] * N_CHIPS
    qw = N_Q_HEADS * HEAD_DIM
    nq = N_CHIPS

    def landing(piece):
        return lax.empty((nq,) + piece.shape, piece.dtype)

    def mixer_shards(i):
        if i < N_A:
            shards = [a_w_in[i], a_w_out[i]] + ([w_mem_kv] if i == 0 else [])
        else:
            j = i - N_A
            shards = [b_w_q[j], b_w_out[j]] + ([w_kv] if j == 0 else [])
        return [a.astype(BF16) for a in shards]

    def ffn_shards(i):
        return [w_gate[i].T.astype(BF16), w_up[i].T.astype(BF16), w_down[i].astype(BF16)]

    conv_pad = jnp.pad(a_conv_w, ((0, 0), (0, 8 - a_conv_w.shape[1]), (0, (-a_conv_w.shape[2]) % 128)))
    first = mixer_shards(0)
    group_shards = {"0a": first[0:1], "0b": first[1:] + [conv_pad], "0f": ffn_shards(0)}
    for i in range(1, DEPTH):
        group_shards[str(i)] = ffn_shards(i) + mixer_shards(i)
    gathers, prev_tok = {}, []

    def start_group(key, after):
        shards = group_shards[key]
        gathers[key] = _exchange_start(shards, [landing(a) for a in shards], True, [None] * len(shards), False,
                                       after, "gather_start_" + key)
        return [gathers[key]["token"]]

    for key in ("0a", "0b", "0f"):
        prev_tok = start_group(key, prev_tok)

    def rows_full(g):
        return g.reshape((-1,) + g.shape[2:])

    def cols_full(g):
        return jnp.transpose(g, (1, 0, 2)).reshape(g.shape[1], -1)

    def landed_weights(key, after):
        g = gathers[key]
        return _exchange_wait([g], g["lands"], [list(range(len(g["lands"])))], after, "gather_wait_" + key)

    def mixer_weights(i, got):
        w_first, w_out = (cols_full(got[0]) if i < N_A else rows_full(got[0])), rows_full(got[1])
        if i >= N_A:
            w_first = jnp.concatenate([_heads_to_member_major(w_first[:, :qw], 1), w_first[:, qw:]], axis=1)
            w_out = jnp.concatenate([_heads_to_member_major(w_out[:qw, :], 0), w_out[qw:, :]], axis=0)
        return dict(w_first=w_first, w_out=w_out, extra=got[2] if len(got) > 2 else None)

    def ffn_weights(got):
        return dict(wg=rows_full(got[0]), wu=rows_full(got[1]), wd=rows_full(got[2]))

    bias, sinkt = [], []
    for j in range(2):
        bj, sj = _bias_tables(rel_bias, b_sinks[j], "bias_tables")
        bias.append(bj)
        sinkt.append(sj)

    ws = []
    xs, xmids, projs, cats, gates, ups = [x0], [], [], [], [], []
    kv = memkv = wmem = wkv = None
    for i in range(DEPTH):
        xin = xs[-1]
        if i == 0:
            w = dict(w_first=cols_full(landed_weights("0a", prev_tok)[0]))
        else:
            got = landed_weights(str(i), [xin])
            w = dict(mixer_weights(i, got[3:]), **ffn_weights(got[0:3]))
        ws.append(w)
        gm = norm_mix[i].reshape(1, d)
        if i < N_A:
            proj = _norm_mm(xin, gm, w["w_first"], tl, "proj_a")
            if i == 0:
                for key in ("1", "2", "3"):
                    prev_tok = start_group(key, prev_tok + [proj])
                got = landed_weights("0b", prev_tok)
                w["w_out"] = rows_full(got[0])
                full_mem = jnp.swapaxes(got[1], 0, 1).reshape(DEPTH, d, -1)
                wmem = jnp.transpose(full_mem, (1, 0, 2)).reshape(d, -1)
                memkv = _norm_mm(mem2, mem_norm.reshape(1, d), wmem, n_mem, "mem_kv")
                taps = got[2][:, :, 0:3, 0:a_conv_w.shape[2]]
                conv_full = jnp.transpose(taps, (1, 2, 0, 3)).reshape(N_A, 3, cwid)
            xmid, cat = _mix_a_fwd(xin, proj, conv_full[i], memkv, i, w["w_out"], tm, "mix_a_fwd")
        else:
            j = i - N_A
            if j == 0:
                wkv = rows_full(w["extra"])
                proj, kv = _norm_mm_pair(xin, gm, w["w_first"], kv_norm.reshape(1, d), wkv, tl, "proj_b_kv")
            else:
                proj = _norm_mm(xin, gm, w["w_first"], tl, "proj_b")
            xmid, cat = _mix_b_fwd(xin, proj, kv, bias[j], sinkt[j], memkv, i, w["w_out"], tm, "mix_b_fwd")
        if i == 0:
            w.update(ffn_weights(landed_weights("0f", [xmid])))
        if i < DEPTH - 1:
            xout, gate, up = _ffn_fwd(xmid, norm_ffn[i].reshape(1, d), w["wg"], w["wu"], w["wd"], tm, "ffn_fwd")
        else:
            xout = None
            gate, up, loss_part, dx, dg_final = _ffn_fwd_loss(xmid, norm_ffn[i].reshape(1, d), w["wg"], w["wu"], w["wd"],
                                                              final_norm.reshape(1, d), target, tm, "ffn_fwd_loss")
        projs.append(proj)
        cats.append(cat)
        xmids.append(xmid)
        gates.append(gate)
        ups.append(up)
        xs.append(xout)

    def rows_pieces(g):
        return g.astype(BF16).reshape((nq, g.shape[0] // nq) + g.shape[1:])

    def cols_pieces(g):
        return jnp.transpose(g.astype(BF16).reshape(g.shape[0], nq, g.shape[1] // nq), (1, 0, 2))

    swapped = ("w_gate", "w_up")
    stacked = dict(a_w_in=a_w_in, a_w_out=a_w_out, w_kv=w_kv[None], b_w_q=b_w_q, b_w_out=b_w_out,
                   w_mem_kv=w_mem_kv, w_gate=jnp.swapaxes(w_gate, 1, 2), w_up=jnp.swapaxes(w_up, 1, 2), w_down=w_down)
    names = list(stacked)
    land = {k: lax.empty((2 * nq,) + stacked[k].shape, BF16) for k in names}
    scatters, scatter_ids = [], []

    def scatter_start(key, items, both_cores):
        keys = [k for k, _, _ in items]
        st = _exchange_start([p for _, _, p in items], [land[k] for k in keys], False, [l for _, l, _ in items],
                             both_cores, [], "scatter_start_" + key)
        for k, ld in zip(keys, st["lands"]):
            land[k] = ld
        scatters.append(st)
        scatter_ids.append([names.index(k) for k in keys])
        return st["token"][0:1, 0:1]

    g_norm_mix, g_norm_ffn = [None] * DEPTH, [None] * DEPTH
    g_conv, g_sinks = [None] * 2, [None] * 2
    dmemkv = [None] * DEPTH
    dbias, dkv_main, dkv_halo = [None] * 2, [None] * 2, [None] * 2
    g_kv_norm = None
    tok = jnp.zeros((1, 1), F32)
    for i in reversed(range(DEPTH)):
        w = ws[i]
        dxm, dgate, dup, act, h2, dgf = _ffn_bwd(dx, xmids[i], norm_ffn[i].reshape(1, d) + tok, gates[i], ups[i],
                                            w["wg"], w["wu"], w["wd"], tm // 2, "ffn_bwd")
        g_norm_ffn[i] = dgf
        g_wd = _wgrad(act, dx, 2 * tm, "wgrad_down")
        g_wg = _wgrad(dgate, h2, 2 * tm, "wgrad_gate")
        g_wu = _wgrad(dup, h2, 2 * tm, "wgrad_up")
        items = [("w_gate", i, rows_pieces(g_wg)), ("w_up", i, rows_pieces(g_wu)), ("w_down", i, rows_pieces(g_wd))]
        if i == 0:
            tok = scatter_start("0f", items, True)
            items = []
        gm = norm_mix[i].reshape(1, d)
        if i < N_A:
            dproj, dcw, dmemkv[i] = _mix_a_bwd(dxm, projs[i], conv_full[i] + (tok if i == 0 else 0.0), memkv, i,
                                                    w["w_out"], tm, "mix_a_bwd")
            g_conv[i] = dcw[0:3]
            g_out = _wgrad(cats[i], dxm, 2 * tm, "wgrad_out")
            if i == 0:
                dmemkv_all = jnp.concatenate([a.astype(BF16) for a in dmemkv], axis=1)
                _, g_mem_norm, hmem = _mm_nt_normbwd(dmemkv_all, wmem, mem2, mem_norm.reshape(1, d),
                                                     jnp.zeros((n_mem, d), F32), n_mem, "mem_kv_bwd")
                g_wmem = _wgrad(hmem, dmemkv_all, n_mem, "wgrad_mem")
                g_wmem = jnp.transpose(g_wmem.reshape(nq, d // nq, DEPTH, -1), (0, 2, 1, 3))
                gm = gm + scatter_start("0o", [("a_w_out", 0, rows_pieces(g_out)), ("w_mem_kv", None, g_wmem)], False)
            dx, g_norm_mix[i], h = _mm_nt_normbwd(dproj, w["w_first"], xs[i], gm, dxm, tl, "proj_a_bwd")
            g_in = _wgrad(h, dproj, 2 * tm, "wgrad_in_a")
            items.append(("a_w_in", i, cols_pieces(g_in)))
            if i > 0:
                items.append(("a_w_out", i, rows_pieces(g_out)))
        else:
            j = i - N_A
            dqp, dkv_main[j], dkv_halo[j], dbias[j], dsk, dmemkv[i] = _mix_b_bwd(
                dxm, projs[i], kv, bias[j], sinkt[j], memkv, i, w["w_out"], tm, "mix_b_bwd")
            g_sinks[j] = dsk[0, 0:N_Q_HEADS].reshape(GROUP, N_KV_HEADS).T.reshape(N_Q_HEADS)
            g_out = _wgrad(cats[i], dxm, 2 * tm, "wgrad_out")
            if j == 0:
                dx, g_norm_mix[i], g_kv_norm, h, hkv, dkv = _mm_nt_normbwd_pair(
                    dqp, w["w_first"], gm, (dkv_main[0], dkv_halo[0], dkv_main[1], dkv_halo[1]), wkv,
                    kv_norm.reshape(1, d), xs[i], dxm, tm, "proj_b_kv_bwd")
            else:
                dx, g_norm_mix[i], h = _mm_nt_normbwd(dqp, w["w_first"], xs[i], gm, dxm, tl, "proj_b_bwd")
            g_q = _wgrad(h, dqp, 2 * tm, "wgrad_in_b")
            g_q = jnp.concatenate([_heads_to_kv_major(g_q[:, :qw], 1), g_q[:, qw:]], axis=1)
            g_out = jnp.concatenate([_heads_to_kv_major(g_out[:qw, :], 0), g_out[qw:, :]], axis=0)
            items += [("b_w_q", j, rows_pieces(g_q)), ("b_w_out", j, rows_pieces(g_out))]
            if j == 0:
                items.append(("w_kv", 0, rows_pieces(_wgrad(hkv, dkv, 2 * tm, "wgrad_kv"))))
        tok = scatter_start(str(i) if i else "0i", items, i > 0)
    grad_x = dx.reshape(x.shape)
    g_rel = _bias_bwd(dbias[0], dbias[1], "bias_bwd")[:, 0:N_Q_HEADS]

    small_shapes = [(DEPTH, d), (DEPTH, d), (d,), (d,), (d,), (2, N_Q_HEADS), (REL_BUCKETS, N_Q_HEADS),
                    (N_A, 3, cwid), ()]
    small = _pack([jnp.concatenate(g_norm_mix, axis=0), jnp.concatenate(g_norm_ffn, axis=0), g_kv_norm, g_mem_norm,
                   dg_final, jnp.stack(g_sinks), g_rel, jnp.stack(g_conv), loss_part[0, 0]])
    small_grp = _exchange_start([small], [lax.empty((8,) + small.shape, F32)], "all", [None], True,
                                [scatters[-1]["token"]], "reduce_small_start")

    late = ("a_w_in", "a_w_out", "w_mem_kv")
    landed = dict(zip(names, _exchange_wait(scatters[:-2], [land[k] for k in names], scatter_ids[:-2],
                                            [small_grp["token"]], "scatter_wait_a")))

    weights = dict(norm_mix=norm_mix, norm_ffn=norm_ffn, a_w_in=a_w_in, a_conv_w=a_conv_w, a_w_out=a_w_out,
                   kv_norm=kv_norm, w_kv=w_kv, b_w_q=b_w_q, b_sinks=b_sinks, b_w_out=b_w_out, rel_bias=rel_bias,
                   mem_norm=mem_norm, w_mem_kv=w_mem_kv, w_gate=w_gate, w_up=w_up, w_down=w_down,
                   final_norm=final_norm)
    moms = dict(norm_mix=m_norm_mix, norm_ffn=m_norm_ffn, a_w_in=m_a_w_in, a_conv_w=m_a_conv_w, a_w_out=m_a_w_out,
                kv_norm=m_kv_norm, w_kv=m_w_kv, b_w_q=m_b_w_q, b_sinks=m_b_sinks, b_w_out=m_b_w_out,
                rel_bias=m_rel_bias, mem_norm=m_mem_norm, w_mem_kv=m_w_mem_kv, w_gate=m_w_gate, w_up=m_w_up,
                w_down=m_w_down, final_norm=m_final_norm)
    vars_ = dict(norm_mix=v_norm_mix, norm_ffn=v_norm_ffn, a_w_in=v_a_w_in, a_conv_w=v_a_conv_w, a_w_out=v_a_w_out,
                 kv_norm=v_kv_norm, w_kv=v_w_kv, b_w_q=v_b_w_q, b_sinks=v_b_sinks, b_w_out=v_b_w_out,
                 rel_bias=v_rel_bias, mem_norm=v_mem_norm, w_mem_kv=v_w_mem_kv, w_gate=v_w_gate, w_up=v_w_up,
                 w_down=v_w_down, final_norm=v_final_norm)
    order = list(weights)
    grads, deltas, new_m, new_v = {}, {}, {}, {}
    def adamw(k, ld, after=()):
        shp, stk = weights[k].shape, ld.shape[1:]
        view = (lambda a: jnp.swapaxes(a, 1, 2)) if k in swapped else (lambda a: a.reshape(stk))
        back = (lambda a: jnp.swapaxes(a, 1, 2)) if k in swapped else (lambda a: a.reshape(shp))
        outs = _adamw_sharded(view(weights[k]), ld, view(moms[k]), view(vars_[k]), "adamw_" + k, after)
        grads[k], deltas[k], new_m[k], new_v[k] = [back(o) for o in outs]
        return outs[1]

    early = [k for k in names if k not in late and k != "w_down"]
    for k in early:
        last_early = adamw(k, landed[k])
    late_ids = [[late.index(names[t]) for t in ids] for ids in scatter_ids[-2:]]
    late_landed = _exchange_wait(scatters[-2:] + [small_grp], [landed[k] for k in late] + small_grp["lands"],
                                 late_ids + [[len(late)]], [last_early], "scatter_wait_b")
    fill = _core_fill_start(late_landed[:-1], [0, 0, None], [], "fill_cores_start")
    adamw("w_down", landed["w_down"], [fill["token"]])
    filled = _core_fill_wait(fill, [deltas["w_down"]], "fill_cores_wait")
    for k, ld in zip(late, filled):
        adamw(k, ld)
    small_sum = _sum_slots(late_landed[-1], "reduce_small_sum")
    (gs_norm_mix, gs_norm_ffn, gs_kv_norm, gs_mem_norm, gs_final, gs_sinks, gs_rel, gs_conv_full, loss) = _unpack(
        small_sum, small_shapes)
    cq = cwid // N_CHIPS
    gs_conv = lax.dynamic_slice_in_dim(gs_conv_full, chip * cq, cq, axis=2)
    small_names = ["norm_mix", "norm_ffn", "kv_norm", "mem_norm", "final_norm", "b_sinks", "rel_bias", "a_conv_w"]
    small_g = [gs_norm_mix, gs_norm_ffn, gs_kv_norm, gs_mem_norm, gs_final, gs_sinks, gs_rel, gs_conv]
    shapes = [weights[k].shape for k in small_names]
    dl_p, m_p, v_p = _adamw_packed(_pack([weights[k] for k in small_names]), _pack(small_g),
                                   _pack([moms[k] for k in small_names]), _pack([vars_[k] for k in small_names]),
                                   "adamw_small")
    for k, g, dl, m2, v2 in zip(small_names, small_g, _unpack(dl_p, shapes), _unpack(m_p, shapes), _unpack(v_p, shapes)):
        grads[k], deltas[k], new_m[k], new_v[k] = g.reshape(weights[k].shape), dl, m2, v2

    return (loss, grad_x, *[grads[k] for k in order], *[deltas[k] for k in order],
            *[new_m[k] for k in order], *[new_v[k] for k in order])
```
